```python
import jax, jax.numpy as jnp
from jax import lax
import numpy as np

D_MODEL = 2048
BATCH = 8
SEQ = 8192
DEPTH = 4

PLE_DIM = 256
EPS = 1e-6
MIX = D_MODEL

CHUNK = 128
A_HEAD = 128
A_WIDTH = MIX // 4
A_HEADS = A_WIDTH // A_HEAD

B_GROUP = 128
B_WIDTH = MIX // 4
B_GROUPS = B_WIDTH // B_GROUP
CONV_W = 3

C_WIDTH = MIX - A_WIDTH - B_WIDTH
C_V = 128
C_HEADS = C_WIDTH // C_V
C_NOPE = 128
C_ROPE = 64
KV_RANK = 512
ROPE_BASE = 10000.0
Q_BLOCK = 128

IN_SPLITS = (A_WIDTH, A_WIDTH, A_WIDTH,
             B_WIDTH, B_WIDTH, B_WIDTH, B_WIDTH,
             C_HEADS * (C_NOPE + C_ROPE), KV_RANK, C_ROPE, C_WIDTH)
IN_WIDTH = 3 * A_WIDTH + 4 * B_WIDTH + C_HEADS * (C_NOPE + C_ROPE) + KV_RANK + C_ROPE + C_WIDTH

kernel_name = 'hybrid_sgu_shortconv_mla_encoder'


def rms_norm(x, g):
    x32 = x.astype(jnp.float32)
    y = x32 * lax.rsqrt(jnp.mean(x32 * x32, axis=-1, keepdims=True) + EPS)
    return y.astype(x.dtype) * g


def rope_tables(positions):
    inv = 1.0 / (ROPE_BASE ** (jnp.arange(0, C_ROPE, 2, dtype=jnp.float32) / C_ROPE))
    ang = positions.astype(jnp.float32)[..., None] * inv
    return jnp.cos(ang), jnp.sin(ang)


def apply_rope(x, cos, sin):
    half = x.shape[-1] // 2
    x1, x2 = x[..., :half], x[..., half:]
    out = jnp.concatenate([x1 * cos - x2 * sin, x2 * cos + x1 * sin], axis=-1)
    return out.astype(x.dtype)


def spatial_gating(u, v, z, v_gain, w_s, b_s):
    bsz, s_len, _ = u.shape
    v = rms_norm(v.reshape(bsz, s_len, A_HEADS, A_HEAD), v_gain)
    vc = v.reshape(bsz, s_len // CHUNK, CHUNK, A_HEADS, A_HEAD)
    s = jnp.einsum('hnm,bkmhc->bknhc', w_s, vc) + b_s.T[None, None, :, :, None]
    return u * s.reshape(bsz, s_len, A_WIDTH) * jax.nn.silu(z)


def short_conv(gate_b, gate_c, h, z, conv_w, conv_b):
    s_len = h.shape[1]
    pad = CONV_W // 2
    xp = jnp.pad(gate_c * h, ((0, 0), (pad, pad), (0, 0)))
    y = conv_b + sum(xp[:, j:j + s_len] * conv_w[j] for j in range(CONV_W))
    return gate_b * y * jax.nn.silu(z)


def latent_attention(q, c_kv, k_rope, z, cos, sin, kv_gain, w_ukv, qn_g, qr_g, kn_g, kr_g):
    bsz, s_len, _ = q.shape
    q = q.reshape(bsz, s_len, C_HEADS, C_NOPE + C_ROPE)
    q_nope = rms_norm(q[..., :C_NOPE], qn_g)
    q_rope = apply_rope(rms_norm(q[..., C_NOPE:], qr_g), cos[:, :, None], sin[:, :, None])
    kv = (rms_norm(c_kv, kv_gain) @ w_ukv).reshape(bsz, s_len, C_HEADS, C_NOPE + C_V)
    k_nope = rms_norm(kv[..., :C_NOPE], kn_g)
    v = kv[..., C_NOPE:]
    k_r = apply_rope(rms_norm(k_rope, kr_g), cos, sin)
    scale = (C_NOPE + C_ROPE) ** -0.5
    n_blk = s_len // Q_BLOCK

    def to_blocks(t):
        return jnp.moveaxis(t.reshape(bsz, n_blk, Q_BLOCK, *t.shape[2:]), 1, 0)

    def attend(blk):
        qn, qr = blk
        s = jnp.einsum('bqhd,bkhd->bhqk', qn, k_nope) + jnp.einsum('bqhr,bkr->bhqk', qr, k_r)
        w = jax.nn.softmax(s.astype(jnp.float32) * scale, axis=-1).astype(v.dtype)
        return jnp.einsum('bhqk,bkhd->bqhd', w, v)

    o = lax.map(attend, (to_blocks(q_nope), to_blocks(q_rope)))
    o = jnp.moveaxis(o, 0, 1).reshape(bsz, s_len, C_WIDTH)
    return o * jax.nn.silu(z)


def _fwd_setup_inputs(seed: int = 0) -> dict:
    key = jax.random.key(seed)
    ks = jax.random.split(key, 24)
    f32 = jnp.float32

    def nrm(k, shape, scale):
        return jax.random.normal(k, shape, f32) * scale

    def gain(k, shape):
        return 1.0 + 0.01 * jax.random.normal(k, shape, f32)

    return {
        'x': jax.random.normal(ks[0], (BATCH, SEQ, D_MODEL), f32),
        'p': jax.random.normal(ks[1], (DEPTH, BATCH, SEQ, PLE_DIM), f32),
        'positions': jnp.broadcast_to(jnp.arange(SEQ, dtype=jnp.int32), (BATCH, SEQ)),
        'attn_norm': gain(ks[2], (DEPTH, D_MODEL)),
        'w_in': nrm(ks[3], (DEPTH, D_MODEL, IN_WIDTH), D_MODEL ** -0.5),
        'sgu_norm': gain(ks[4], (DEPTH, A_HEADS, A_HEAD)),
        'w_spatial': nrm(ks[5], (DEPTH, A_HEADS, CHUNK, CHUNK), CHUNK ** -0.5),
        'b_spatial': gain(ks[6], (DEPTH, A_HEADS, CHUNK)),
        'conv_w': nrm(ks[7], (DEPTH, CONV_W, B_WIDTH), CONV_W ** -0.5),
        'conv_b': nrm(ks[8], (DEPTH, B_WIDTH), 0.01),
        'kv_norm': gain(ks[9], (DEPTH, KV_RANK)),
        'w_ukv': nrm(ks[10], (DEPTH, KV_RANK, C_HEADS * (C_NOPE + C_V)), KV_RANK ** -0.5),
        'q_nope_norm': gain(ks[11], (DEPTH, C_NOPE)),
        'q_rope_norm': gain(ks[12], (DEPTH, C_ROPE)),
        'k_nope_norm': gain(ks[13], (DEPTH, C_NOPE)),
        'k_rope_norm': gain(ks[14], (DEPTH, C_ROPE)),
        'out_norm': gain(ks[15], (DEPTH, MIX)),
        'w_out': nrm(ks[16], (DEPTH, MIX, D_MODEL), MIX ** -0.5),
        'ple_norm': gain(ks[17], (DEPTH, D_MODEL)),
        'w_ple_gate': nrm(ks[18], (DEPTH, D_MODEL, D_MODEL), D_MODEL ** -0.5),
        'w_ple_proj': nrm(ks[19], (DEPTH, PLE_DIM, D_MODEL), PLE_DIM ** -0.5),
    }


def _fwd_reference(x, p, positions, attn_norm, w_in, sgu_norm, w_spatial, b_spatial, conv_w, conv_b,
              kv_norm, w_ukv, q_nope_norm, q_rope_norm, k_nope_norm, k_rope_norm,
              out_norm, w_out, ple_norm, w_ple_gate, w_ple_proj):
    cos, sin = rope_tables(positions)
    split_pts = [int(c) for c in np.cumsum(IN_SPLITS)[:-1]]
    out_pts = [A_WIDTH, A_WIDTH + B_WIDTH]
    h = x
    for i in range(DEPTH):
        hn = rms_norm(h, attn_norm[i])
        proj = hn @ w_in[i]
        (a_u, a_v, a_z, b_b, b_c, b_h, b_z, c_q, c_kv, c_kr, c_z) = jnp.split(proj, split_pts, axis=-1)
        y_a = spatial_gating(a_u, a_v, a_z, sgu_norm[i], w_spatial[i], b_spatial[i])
        y_b = short_conv(b_b, b_c, b_h, b_z, conv_w[i], conv_b[i])
        y_c = latent_attention(c_q, c_kv, c_kr, c_z, cos, sin, kv_norm[i], w_ukv[i],
                               q_nope_norm[i], q_rope_norm[i], k_nope_norm[i], k_rope_norm[i])
        g_a, g_b, g_c = jnp.split(out_norm[i], out_pts)
        y = jnp.concatenate([rms_norm(y_a, g_a), rms_norm(y_b, g_b), rms_norm(y_c, g_c)], axis=-1)
        h = h + y @ w_out[i]
        gate = jax.nn.sigmoid(rms_norm(h, ple_norm[i]) @ w_ple_gate[i])
        h = h + gate * (p[i] @ w_ple_proj[i])
    return h


import jax as _jax
import jax.numpy as _jnp

TWIN_FORMAT = 'train_step'
FWD_PARAMS = ['x', 'p', 'positions', 'attn_norm', 'w_in', 'sgu_norm', 'w_spatial', 'b_spatial', 'conv_w', 'conv_b', 'kv_norm', 'w_ukv', 'q_nope_norm', 'q_rope_norm', 'k_nope_norm', 'k_rope_norm', 'out_norm', 'w_out', 'ple_norm', 'w_ple_gate', 'w_ple_proj']
TWIN_WEIGHTS = ['attn_norm', 'w_in', 'sgu_norm', 'w_spatial', 'b_spatial', 'conv_w', 'conv_b', 'kv_norm', 'w_ukv', 'q_nope_norm', 'q_rope_norm', 'k_nope_norm', 'k_rope_norm', 'out_norm', 'w_out', 'ple_norm', 'w_ple_gate', 'w_ple_proj']
TWIN_DIFF_INPUT = 'x'
TWIN_INPUTS = ['x', 'p', 'positions', 'attn_norm', 'w_in', 'sgu_norm', 'w_spatial', 'b_spatial', 'conv_w', 'conv_b', 'kv_norm', 'w_ukv', 'q_nope_norm', 'q_rope_norm', 'k_nope_norm', 'k_rope_norm', 'out_norm', 'w_out', 'ple_norm', 'w_ple_gate', 'w_ple_proj', 'loss_target', 'm_attn_norm', 'm_w_in', 'm_sgu_norm', 'm_w_spatial', 'm_b_spatial', 'm_conv_w', 'm_conv_b', 'm_kv_norm', 'm_w_ukv', 'm_q_nope_norm', 'm_q_rope_norm', 'm_k_nope_norm', 'm_k_rope_norm', 'm_out_norm', 'm_w_out', 'm_ple_norm', 'm_w_ple_gate', 'm_w_ple_proj', 'v_attn_norm', 'v_w_in', 'v_sgu_norm', 'v_w_spatial', 'v_b_spatial', 'v_conv_w', 'v_conv_b', 'v_kv_norm', 'v_w_ukv', 'v_q_nope_norm', 'v_q_rope_norm', 'v_k_nope_norm', 'v_k_rope_norm', 'v_out_norm', 'v_w_out', 'v_ple_norm', 'v_w_ple_gate', 'v_w_ple_proj']
TWIN_OUTPUTS = ['loss', 'grad_x', 'grad_attn_norm', 'grad_w_in', 'grad_sgu_norm', 'grad_w_spatial', 'grad_b_spatial', 'grad_conv_w', 'grad_conv_b', 'grad_kv_norm', 'grad_w_ukv', 'grad_q_nope_norm', 'grad_q_rope_norm', 'grad_k_nope_norm', 'grad_k_rope_norm', 'grad_out_norm', 'grad_w_out', 'grad_ple_norm', 'grad_w_ple_gate', 'grad_w_ple_proj', 'delta_attn_norm', 'delta_w_in', 'delta_sgu_norm', 'delta_w_spatial', 'delta_b_spatial', 'delta_conv_w', 'delta_conv_b', 'delta_kv_norm', 'delta_w_ukv', 'delta_q_nope_norm', 'delta_q_rope_norm', 'delta_k_nope_norm', 'delta_k_rope_norm', 'delta_out_norm', 'delta_w_out', 'delta_ple_norm', 'delta_w_ple_gate', 'delta_w_ple_proj', 'new_m_attn_norm', 'new_m_w_in', 'new_m_sgu_norm', 'new_m_w_spatial', 'new_m_b_spatial', 'new_m_conv_w', 'new_m_conv_b', 'new_m_kv_norm', 'new_m_w_ukv', 'new_m_q_nope_norm', 'new_m_q_rope_norm', 'new_m_k_nope_norm', 'new_m_k_rope_norm', 'new_m_out_norm', 'new_m_w_out', 'new_m_ple_norm', 'new_m_w_ple_gate', 'new_m_w_ple_proj', 'new_v_attn_norm', 'new_v_w_in', 'new_v_sgu_norm', 'new_v_w_spatial', 'new_v_b_spatial', 'new_v_conv_w', 'new_v_conv_b', 'new_v_kv_norm', 'new_v_w_ukv', 'new_v_q_nope_norm', 'new_v_q_rope_norm', 'new_v_k_nope_norm', 'new_v_k_rope_norm', 'new_v_out_norm', 'new_v_w_out', 'new_v_ple_norm', 'new_v_w_ple_gate', 'new_v_w_ple_proj']
TWIN_LEAF_KINDS = {'loss': 'loss', 'grad_x': 'grad_x', 'grad_attn_norm': 'grad_w', 'grad_w_in': 'grad_w', 'grad_sgu_norm': 'grad_w', 'grad_w_spatial': 'grad_w', 'grad_b_spatial': 'grad_w', 'grad_conv_w': 'grad_w', 'grad_conv_b': 'grad_w', 'grad_kv_norm': 'grad_w', 'grad_w_ukv': 'grad_w', 'grad_q_nope_norm': 'grad_w', 'grad_q_rope_norm': 'grad_w', 'grad_k_nope_norm': 'grad_w', 'grad_k_rope_norm': 'grad_w', 'grad_out_norm': 'grad_w', 'grad_w_out': 'grad_w', 'grad_ple_norm': 'grad_w', 'grad_w_ple_gate': 'grad_w', 'grad_w_ple_proj': 'grad_w', 'delta_attn_norm': 'delta_w', 'delta_w_in': 'delta_w', 'delta_sgu_norm': 'delta_w', 'delta_w_spatial': 'delta_w', 'delta_b_spatial': 'delta_w', 'delta_conv_w': 'delta_w', 'delta_conv_b': 'delta_w', 'delta_kv_norm': 'delta_w', 'delta_w_ukv': 'delta_w', 'delta_q_nope_norm': 'delta_w', 'delta_q_rope_norm': 'delta_w', 'delta_k_nope_norm': 'delta_w', 'delta_k_rope_norm': 'delta_w', 'delta_out_norm': 'delta_w', 'delta_w_out': 'delta_w', 'delta_ple_norm': 'delta_w', 'delta_w_ple_gate': 'delta_w', 'delta_w_ple_proj': 'delta_w', 'new_m_attn_norm': 'new_m', 'new_m_w_in': 'new_m', 'new_m_sgu_norm': 'new_m', 'new_m_w_spatial': 'new_m', 'new_m_b_spatial': 'new_m', 'new_m_conv_w': 'new_m', 'new_m_conv_b': 'new_m', 'new_m_kv_norm': 'new_m', 'new_m_w_ukv': 'new_m', 'new_m_q_nope_norm': 'new_m', 'new_m_q_rope_norm': 'new_m', 'new_m_k_nope_norm': 'new_m', 'new_m_k_rope_norm': 'new_m', 'new_m_out_norm': 'new_m', 'new_m_w_out': 'new_m', 'new_m_ple_norm': 'new_m', 'new_m_w_ple_gate': 'new_m', 'new_m_w_ple_proj': 'new_m', 'new_v_attn_norm': 'new_v', 'new_v_w_in': 'new_v', 'new_v_sgu_norm': 'new_v', 'new_v_w_spatial': 'new_v', 'new_v_b_spatial': 'new_v', 'new_v_conv_w': 'new_v', 'new_v_conv_b': 'new_v', 'new_v_kv_norm': 'new_v', 'new_v_w_ukv': 'new_v', 'new_v_q_nope_norm': 'new_v', 'new_v_q_rope_norm': 'new_v', 'new_v_k_nope_norm': 'new_v', 'new_v_k_rope_norm': 'new_v', 'new_v_out_norm': 'new_v', 'new_v_w_out': 'new_v', 'new_v_ple_norm': 'new_v', 'new_v_w_ple_gate': 'new_v', 'new_v_w_ple_proj': 'new_v'}


def _forward(args):
    return _fwd_reference(*[args[k] for k in FWD_PARAMS])


def _output_shape():
    def fwd():
        inp = _fwd_setup_inputs(0)
        return _fwd_reference(*[inp[k] for k in FWD_PARAMS])
    out = _jax.eval_shape(fwd)
    return out.shape, out.dtype

N_MICROBATCH = 1
ADAM_LR = 0.001
ADAM_B1 = 0.9
ADAM_B2 = 0.999
ADAM_EPS = 1e-08
ADAM_WD = 0.01
ADAM_STEP = 10
PER_EXAMPLE_BATCH_AXIS = {'x': 0, 'p': 1, 'positions': 0, 'loss_target': 0}
SHARED_INPUTS = []
_WEIGHT_DTYPES = {'attn_norm': _jnp.float32, 'w_in': _jnp.float32, 'sgu_norm': _jnp.float32, 'w_spatial': _jnp.float32, 'b_spatial': _jnp.float32, 'conv_w': _jnp.float32, 'conv_b': _jnp.float32, 'kv_norm': _jnp.float32, 'w_ukv': _jnp.float32, 'q_nope_norm': _jnp.float32, 'q_rope_norm': _jnp.float32, 'k_nope_norm': _jnp.float32, 'k_rope_norm': _jnp.float32, 'out_norm': _jnp.float32, 'w_out': _jnp.float32, 'ple_norm': _jnp.float32, 'w_ple_gate': _jnp.float32, 'w_ple_proj': _jnp.float32}
MOMENT_SCALE = {'attn_norm': 1.418244e+00, 'w_in': 7.749196e-01, 'sgu_norm': 4.709857e-01, 'w_spatial': 4.285893e-01, 'b_spatial': 3.984355e-01, 'conv_w': 6.308708e-01, 'conv_b': 6.674360e-01, 'kv_norm': 2.928655e+00, 'w_ukv': 1.319635e+00, 'q_nope_norm': 1.304815e+00, 'q_rope_norm': 1.121492e+00, 'k_nope_norm': 1.302006e+00, 'k_rope_norm': 1.042880e+00, 'out_norm': 3.513692e+01, 'w_out': 1.391821e+00, 'ple_norm': 8.480678e-01, 'w_ple_gate': 1.446700e-01, 'w_ple_proj': 4.392115e-01}


def _to_microbatches(a, axis):
    t = _jnp.moveaxis(a, axis, 0)
    t = t.reshape((N_MICROBATCH, t.shape[0] // N_MICROBATCH) + t.shape[1:])
    return _jnp.moveaxis(t, 1, axis + 1)


def setup_inputs(seed: int = 0) -> dict:
    inp = _fwd_setup_inputs(seed)
    key = _jax.random.fold_in(_jax.random.key(seed), 7919)
    shape, _ = _output_shape()
    out = dict(inp)
    out["loss_target"] = _jax.random.normal(_jax.random.fold_in(key, 0), shape, _jnp.float32)
    for i, name in enumerate(TWIN_WEIGHTS):
        w = inp[name].astype(_jnp.float32)
        if MOMENT_SCALE is None:
            s = _jnp.sqrt(_jnp.mean(_jnp.square(w)) + 1e-30)
        else:
            s = MOMENT_SCALE[name]
        km, kv = _jax.random.split(_jax.random.fold_in(key, i + 1))
        out[name] = w
        out["m_" + name] = s * _jax.random.normal(km, w.shape, _jnp.float32)
        out["v_" + name] = (s * s) * _jax.random.uniform(kv, w.shape, _jnp.float32, 0.5, 1.5)
    if N_MICROBATCH > 1:
        for name, axis in PER_EXAMPLE_BATCH_AXIS.items():
            out[name] = _to_microbatches(out[name], axis)
    return {'x': out['x'], 'p': out['p'], 'positions': out['positions'], 'attn_norm': out['attn_norm'], 'w_in': out['w_in'], 'sgu_norm': out['sgu_norm'], 'w_spatial': out['w_spatial'], 'b_spatial': out['b_spatial'], 'conv_w': out['conv_w'], 'conv_b': out['conv_b'], 'kv_norm': out['kv_norm'], 'w_ukv': out['w_ukv'], 'q_nope_norm': out['q_nope_norm'], 'q_rope_norm': out['q_rope_norm'], 'k_nope_norm': out['k_nope_norm'], 'k_rope_norm': out['k_rope_norm'], 'out_norm': out['out_norm'], 'w_out': out['w_out'], 'ple_norm': out['ple_norm'], 'w_ple_gate': out['w_ple_gate'], 'w_ple_proj': out['w_ple_proj'], 'loss_target': out['loss_target'], 'm_attn_norm': out['m_attn_norm'], 'm_w_in': out['m_w_in'], 'm_sgu_norm': out['m_sgu_norm'], 'm_w_spatial': out['m_w_spatial'], 'm_b_spatial': out['m_b_spatial'], 'm_conv_w': out['m_conv_w'], 'm_conv_b': out['m_conv_b'], 'm_kv_norm': out['m_kv_norm'], 'm_w_ukv': out['m_w_ukv'], 'm_q_nope_norm': out['m_q_nope_norm'], 'm_q_rope_norm': out['m_q_rope_norm'], 'm_k_nope_norm': out['m_k_nope_norm'], 'm_k_rope_norm': out['m_k_rope_norm'], 'm_out_norm': out['m_out_norm'], 'm_w_out': out['m_w_out'], 'm_ple_norm': out['m_ple_norm'], 'm_w_ple_gate': out['m_w_ple_gate'], 'm_w_ple_proj': out['m_w_ple_proj'], 'v_attn_norm': out['v_attn_norm'], 'v_w_in': out['v_w_in'], 'v_sgu_norm': out['v_sgu_norm'], 'v_w_spatial': out['v_w_spatial'], 'v_b_spatial': out['v_b_spatial'], 'v_conv_w': out['v_conv_w'], 'v_conv_b': out['v_conv_b'], 'v_kv_norm': out['v_kv_norm'], 'v_w_ukv': out['v_w_ukv'], 'v_q_nope_norm': out['v_q_nope_norm'], 'v_q_rope_norm': out['v_q_rope_norm'], 'v_k_nope_norm': out['v_k_nope_norm'], 'v_k_rope_norm': out['v_k_rope_norm'], 'v_out_norm': out['v_out_norm'], 'v_w_out': out['v_w_out'], 'v_ple_norm': out['v_ple_norm'], 'v_w_ple_gate': out['v_w_ple_gate'], 'v_w_ple_proj': out['v_w_ple_proj']}


def _loss(weights, diff, rest, loss_target):
    with _jax.named_scope("forward"):
        args = {**rest, TWIN_DIFF_INPUT: diff, **{k: w.astype(_WEIGHT_DTYPES[k]) for k, w in weights.items()}}
        y = _forward(args)
    with _jax.named_scope("loss_head"):
        err = _jnp.square(y.astype(_jnp.float32) - loss_target)
        return 0.5 * _jnp.sum(_jnp.mean(err, axis=-1)) if err.ndim else 0.5 * err


def _adamw(w, g, m, v):
    m = ADAM_B1 * m + (1.0 - ADAM_B1) * g
    v = ADAM_B2 * v + (1.0 - ADAM_B2) * _jnp.square(g)
    m_hat = m / (1.0 - ADAM_B1 ** ADAM_STEP)
    v_hat = v / (1.0 - ADAM_B2 ** ADAM_STEP)
    delta = -ADAM_LR * (m_hat / (_jnp.sqrt(v_hat) + ADAM_EPS) + ADAM_WD * w)
    return delta, m, v


def reference(x, p, positions, attn_norm, w_in, sgu_norm, w_spatial, b_spatial, conv_w, conv_b, kv_norm, w_ukv, q_nope_norm, q_rope_norm, k_nope_norm, k_rope_norm, out_norm, w_out, ple_norm, w_ple_gate, w_ple_proj, loss_target, m_attn_norm, m_w_in, m_sgu_norm, m_w_spatial, m_b_spatial, m_conv_w, m_conv_b, m_kv_norm, m_w_ukv, m_q_nope_norm, m_q_rope_norm, m_k_nope_norm, m_k_rope_norm, m_out_norm, m_w_out, m_ple_norm, m_w_ple_gate, m_w_ple_proj, v_attn_norm, v_w_in, v_sgu_norm, v_w_spatial, v_b_spatial, v_conv_w, v_conv_b, v_kv_norm, v_w_ukv, v_q_nope_norm, v_q_rope_norm, v_k_nope_norm, v_k_rope_norm, v_out_norm, v_w_out, v_ple_norm, v_w_ple_gate, v_w_ple_proj):
    given = dict(x=x, p=p, positions=positions, attn_norm=attn_norm, w_in=w_in, sgu_norm=sgu_norm, w_spatial=w_spatial, b_spatial=b_spatial, conv_w=conv_w, conv_b=conv_b, kv_norm=kv_norm, w_ukv=w_ukv, q_nope_norm=q_nope_norm, q_rope_norm=q_rope_norm, k_nope_norm=k_nope_norm, k_rope_norm=k_rope_norm, out_norm=out_norm, w_out=w_out, ple_norm=ple_norm, w_ple_gate=w_ple_gate, w_ple_proj=w_ple_proj, loss_target=loss_target, m_attn_norm=m_attn_norm, m_w_in=m_w_in, m_sgu_norm=m_sgu_norm, m_w_spatial=m_w_spatial, m_b_spatial=m_b_spatial, m_conv_w=m_conv_w, m_conv_b=m_conv_b, m_kv_norm=m_kv_norm, m_w_ukv=m_w_ukv, m_q_nope_norm=m_q_nope_norm, m_q_rope_norm=m_q_rope_norm, m_k_nope_norm=m_k_nope_norm, m_k_rope_norm=m_k_rope_norm, m_out_norm=m_out_norm, m_w_out=m_w_out, m_ple_norm=m_ple_norm, m_w_ple_gate=m_w_ple_gate, m_w_ple_proj=m_w_ple_proj, v_attn_norm=v_attn_norm, v_w_in=v_w_in, v_sgu_norm=v_sgu_norm, v_w_spatial=v_w_spatial, v_b_spatial=v_b_spatial, v_conv_w=v_conv_w, v_conv_b=v_conv_b, v_kv_norm=v_kv_norm, v_w_ukv=v_w_ukv, v_q_nope_norm=v_q_nope_norm, v_q_rope_norm=v_q_rope_norm, v_k_nope_norm=v_k_nope_norm, v_k_rope_norm=v_k_rope_norm, v_out_norm=v_out_norm, v_w_out=v_w_out, v_ple_norm=v_ple_norm, v_w_ple_gate=v_w_ple_gate, v_w_ple_proj=v_w_ple_proj)
    weights = {n: given[n] for n in TWIN_WEIGHTS}
    shared = {n: given[n] for n in SHARED_INPUTS}
    per_example = {n: given[n] for n in ['x', 'p', 'positions']}
    grad_fn = _jax.value_and_grad(_loss, argnums=(0, 1))

    def one_microbatch(ex, loss_target):
        ex = dict(ex)
        diff = ex.pop(TWIN_DIFF_INPUT)
        return grad_fn(weights, diff, {**shared, **ex}, loss_target)

    if N_MICROBATCH == 1:
        loss, (grad_w, grad_x) = one_microbatch(per_example, given["loss_target"])
    else:
        def body(carry, xs):
            loss_sum, grad_sum = carry
            l_k, (gw_k, gx_k) = one_microbatch(xs[0], xs[1])
            with _jax.named_scope("update"):
                return (loss_sum + l_k, _jax.tree.map(_jnp.add, grad_sum, gw_k)), gx_k

        init = (_jnp.zeros((), _jnp.float32), _jax.tree.map(_jnp.zeros_like, weights))
        (loss, grad_w), grad_x = _jax.lax.scan(body, init, (per_example, given["loss_target"]))
    with _jax.named_scope("update"):
        delta_w, new_m, new_v = {}, {}, {}
        for n in TWIN_WEIGHTS:
            delta_w[n], new_m[n], new_v[n] = _adamw(weights[n], grad_w[n], given["m_" + n], given["v_" + n])
    return (loss, grad_x, *[grad_w[n] for n in TWIN_WEIGHTS], *[delta_w[n] for n in TWIN_WEIGHTS],
            *[new_m[n] for n in TWIN_WEIGHTS], *[new_v[n] for n in TWIN_WEIGHTS])
```

```python
import functools

import jax
import jax.numpy as jnp
from jax import lax
from jax.experimental import pallas as pl
from jax.experimental.pallas import tpu as pltpu

F32 = jnp.float32
BF16 = jnp.bfloat16

N_DEV = 8
DEPTH = 4
D_MODEL = 2048
EPS = 1e-6
CHUNK = 128
A_HEADS = 4
HEAD = 128
ROPE = 64
HALF = ROPE // 2
C_HEADS = 8
KV_RANK = 512
PLE_DIM = 256
ROPE_BASE = 10000.0
IN_WIDTH = 6720
QK = HEAD + ROPE
SCALE = QK ** -0.5
HALO = 8

O_B = 0
W_B = 2048
O_CZ = 2048
W_CZ = 1024
O_A = 3072
W_A = 1536
O_M = 4608
W_M = 2304
M_QN, M_QR, M_CKV, M_KR = 0, 1024, 1536, 2048
PROJ_W = 6912

ADAM_LR = 0.001
ADAM_B1 = 0.9
ADAM_B2 = 0.999
ADAM_EPS = 1e-08
ADAM_WD = 0.01
ADAM_STEP = 10

VMEM_LIMIT = 56 * 1024 * 1024

_NT = (((1,), (1,)), ((), ()))
_TN = (((0,), (0,)), ((), ()))


def _params(*sem):
    return pltpu.CompilerParams(dimension_semantics=sem, vmem_limit_bytes=VMEM_LIMIT)


@jax.custom_vjp
def _bdot(a, b):
    return jnp.dot(a.astype(BF16), b.astype(BF16), preferred_element_type=F32)


def _bdot_fwd(a, b):
    return _bdot(a, b), (a, b)


def _bdot_bwd(res, g):
    a, b = res
    gb = g.astype(BF16)
    da = lax.dot_general(gb, b.astype(BF16), _NT, preferred_element_type=F32)
    db = lax.dot_general(a.astype(BF16), gb, _TN, preferred_element_type=F32)
    return da.astype(a.dtype), db.astype(b.dtype)


_bdot.defvjp(_bdot_fwd, _bdot_bwd)


@functools.partial(jax.custom_vjp, nondiff_argnums=(1,))
def _split(x, n):
    w = x.shape[-1] // n
    return tuple(x[:, i * w:(i + 1) * w] for i in range(n))


def _split_fwd(x, n):
    return _split(x, n), None


def _split_bwd(n, _, gs):
    return (jnp.concatenate(gs, axis=-1),)


_split.defvjp(_split_fwd, _split_bwd)


@functools.partial(jax.custom_vjp, nondiff_argnums=(1,))
def _shift_rows(x, k):
    return pltpu.roll(x, k % x.shape[0], 0)


def _shift_rows_fwd(x, k):
    return _shift_rows(x, k), None


def _shift_rows_bwd(k, _, g):
    return (_shift_rows(g, -k),)


_shift_rows.defvjp(_shift_rows_fwd, _shift_rows_bwd)


@jax.custom_vjp
def _swap_halves(x):
    h = x.shape[-1] // 2
    return jnp.concatenate([x[:, h:], x[:, :h]], axis=-1)


def _swap_halves_fwd(x):
    return _swap_halves(x), None


def _swap_halves_bwd(_, g):
    return (_swap_halves(g),)


_swap_halves.defvjp(_swap_halves_fwd, _swap_halves_bwd)


def _rms(x, g):
    return x * lax.rsqrt(jnp.mean(x * x, axis=-1, keepdims=True) + EPS) * g


def _rope(x, cosf, sins):
    return x * cosf + _swap_halves(x) * sins


def _sgu_chunk(u, v, z, gain, ws, bs, ga):
    ys = []
    for h in range(A_HEADS):
        vn = _rms(v[h], gain[h])
        s = _bdot(ws[h], vn) + bs[h]
        ys.append(u[h] * s * jax.nn.silu(z[h]))
    ss = sum(jnp.sum(y * y, axis=-1, keepdims=True) for y in ys) * (1.0 / (A_HEADS * HEAD))
    r = lax.rsqrt(ss + EPS)
    return tuple(ys[h] * r * ga[h] for h in range(A_HEADS))


def _conv_tile(bb, bc, bh, bz, w0, w1, w2, cb, gb, w0h, w1h, w2h, cbh, gbh, valid, core):
    t = jnp.where(valid, bc * bh, 0.0)
    y = (jnp.where(core, cb, cbh)
         + _shift_rows(t, 1) * jnp.where(core, w0, w0h)
         + t * jnp.where(core, w1, w1h)
         + _shift_rows(t, -1) * jnp.where(core, w2, w2h))
    return _rms(bb * y * jax.nn.silu(bz), jnp.where(core, gb, gbh))


def _cgate_tile(o, cz, gc):
    return _rms(o * jax.nn.silu(cz), gc)


def _mla_tile(qn, qr, ckv, kr, cosf, sins, wukv, kvg, qng, qrg, kng, krg):
    kv = _split(_bdot(_rms(ckv, kvg), wukv), 2 * C_HEADS)
    k_r = _rope(_rms(kr, krg), cosf, sins)
    qn_h = _split(qn, C_HEADS)
    qr_h = _split(qr, C_HEADS)
    q, k, v = [], [], []
    for h in range(C_HEADS):
        q.append(jnp.concatenate([_rms(qn_h[h], qng), _rope(_rms(qr_h[h], qrg), cosf, sins)], axis=-1))
        k.append(jnp.concatenate([_rms(kv[h], kng), k_r], axis=-1))
        v.append(kv[C_HEADS + h])
    return tuple(q), tuple(k), tuple(v)


def _norm_matmul(h, gain, w, tm, tn):
    s_len, k = h.shape
    n = w.shape[1]

    def body(h_ref, g_ref, w_ref, o_ref, hn_ref):
        @pl.when(pl.program_id(1) == 0)
        def _():
            hn_ref[...] = _rms(h_ref[...], g_ref[...]).astype(BF16)

        o_ref[...] = jnp.dot(hn_ref[...], w_ref[...], preferred_element_type=F32).astype(BF16)

    return pl.pallas_call(
        body, name="norm_matmul",
        out_shape=(jax.ShapeDtypeStruct((s_len, n), BF16), jax.ShapeDtypeStruct((s_len, k), BF16)),
        grid=(s_len // tm, n // tn),
        in_specs=[pl.BlockSpec((tm, k), lambda i, j: (i, 0)), pl.BlockSpec((1, k), lambda i, j: (0, 0)),
                  pl.BlockSpec((k, tn), lambda i, j: (0, j))],
        out_specs=(pl.BlockSpec((tm, tn), lambda i, j: (i, j)), pl.BlockSpec((tm, k), lambda i, j: (i, 0))),
        compiler_params=_params("parallel", "arbitrary"),
    )(h, gain, w)


def _out_matmul(h, ya, yb, yc, w, tm, tn):
    s_len, n = h.shape
    ka, kb = ya.shape[1], yb.shape[1]
    kc = yc.shape[1]

    def body(h_ref, ya_ref, yb_ref, yc_ref, w_ref, o_ref):
        acc = jnp.dot(ya_ref[...], w_ref[0:ka, :], preferred_element_type=F32)
        acc += jnp.dot(yb_ref[...], w_ref[ka:ka + kb, :], preferred_element_type=F32)
        acc += jnp.dot(yc_ref[...], w_ref[ka + kb:ka + kb + kc, :], preferred_element_type=F32)
        o_ref[...] = h_ref[...] + acc

    return pl.pallas_call(
        body, name="out_matmul",
        out_shape=jax.ShapeDtypeStruct((s_len, n), F32),
        grid=(s_len // tm, n // tn),
        in_specs=[pl.BlockSpec((tm, tn), lambda i, j: (i, j)), pl.BlockSpec((tm, ka), lambda i, j: (i, 0)),
                  pl.BlockSpec((tm, kb), lambda i, j: (i, 0)), pl.BlockSpec((tm, kc), lambda i, j: (i, 0)),
                  pl.BlockSpec((ka + kb + kc, tn), lambda i, j: (0, j))],
        out_specs=pl.BlockSpec((tm, tn), lambda i, j: (i, j)),
        compiler_params=_params("parallel", "parallel"),
    )(h, ya, yb, yc, w)


def _ple_forward(h1, gain, p, wg, wp, tm, tn):
    s_len, d = h1.shape
    kp = p.shape[1]

    def body(hrow_ref, hcol_ref, g_ref, p_ref, wg_ref, wp_ref, o_ref, n1_ref, gate_ref, pp_ref):
        @pl.when(pl.program_id(1) == 0)
        def _():
            n1_ref[...] = _rms(hrow_ref[...], g_ref[...]).astype(BF16)

        gate = jax.nn.sigmoid(jnp.dot(n1_ref[...], wg_ref[...], preferred_element_type=F32))
        pp = jnp.dot(p_ref[...].astype(BF16), wp_ref[...], preferred_element_type=F32)
        o_ref[...] = hcol_ref[...] + gate * pp
        gate_ref[...] = gate.astype(BF16)
        pp_ref[...] = pp.astype(BF16)

    col = pl.BlockSpec((tm, tn), lambda i, j: (i, j))
    return pl.pallas_call(
        body, name="ple_forward",
        out_shape=(jax.ShapeDtypeStruct((s_len, d), F32), jax.ShapeDtypeStruct((s_len, d), BF16),
                   jax.ShapeDtypeStruct((s_len, d), BF16), jax.ShapeDtypeStruct((s_len, d), BF16)),
        grid=(s_len // tm, d // tn),
        in_specs=[pl.BlockSpec((tm, d), lambda i, j: (i, 0)), col, pl.BlockSpec((1, d), lambda i, j: (0, 0)),
                  pl.BlockSpec((tm, kp), lambda i, j: (i, 0)), pl.BlockSpec((d, tn), lambda i, j: (0, j)),
                  pl.BlockSpec((kp, tn), lambda i, j: (0, j))],
        out_specs=(col, pl.BlockSpec((tm, d), lambda i, j: (i, 0)), col, col),
        compiler_params=_params("parallel", "arbitrary"),
    )(h1, h1, gain, p, wg, wp)


def _matmul_nt(a, b, tm, tk, name):
    m, n = a.shape
    k = b.shape[0]

    def body(a_ref, b_ref, o_ref):
        o_ref[...] = lax.dot_general(a_ref[...].astype(BF16), b_ref[...].astype(BF16), _NT, preferred_element_type=F32)

    return pl.pallas_call(
        body, name=name,
        out_shape=jax.ShapeDtypeStruct((m, k), F32),
        grid=(m // tm, k // tk),
        in_specs=[pl.BlockSpec((tm, n), lambda i, j: (i, 0)), pl.BlockSpec((tk, n), lambda i, j: (j, 0))],
        out_specs=pl.BlockSpec((tm, tk), lambda i, j: (i, j)),
        compiler_params=_params("parallel", "parallel"),
    )(a, b)


def _matmul_tn(a, b, tm, tk, tn, name):
    m, k = a.shape
    n = b.shape[1]

    def body(a_ref, b_ref, o_ref):
        @pl.when(pl.program_id(2) == 0)
        def _():
            o_ref[...] = jnp.zeros_like(o_ref)

        o_ref[...] += lax.dot_general(a_ref[...].astype(BF16), b_ref[...].astype(BF16), _TN, preferred_element_type=F32)

    return pl.pallas_call(
        body, name=name,
        out_shape=jax.ShapeDtypeStruct((k, n), F32),
        grid=(k // tk, n // tn, m // tm),
        in_specs=[pl.BlockSpec((tm, tk), lambda kk, nn, mm: (mm, kk)), pl.BlockSpec((tm, tn), lambda kk, nn, mm: (mm, nn))],
        out_specs=pl.BlockSpec((tk, tn), lambda kk, nn, mm: (kk, nn)),
        compiler_params=_params("parallel", "parallel", "arbitrary"),
    )(a, b)


def _acc(ref, val, first):
    @pl.when(first)
    def _():
        ref[...] = val

    @pl.when(jnp.logical_not(first))
    def _():
        ref[...] += val


def _loss_grad(h, target, tm):
    s_len, d = h.shape

    def body(h_ref, t_ref, dh_ref, loss_ref):
        e = h_ref[...] - t_ref[...]
        dh_ref[...] = e * (1.0 / d)
        part = jnp.sum(jnp.sum(e * e, axis=-1, keepdims=True), axis=0, keepdims=True) * (0.5 / d)
        _acc(loss_ref, jnp.broadcast_to(part, loss_ref.shape), pl.program_id(0) == 0)

    row = pl.BlockSpec((tm, d), lambda i: (i, 0))
    return pl.pallas_call(
        body, name="loss_grad",
        out_shape=(jax.ShapeDtypeStruct((s_len, d), F32), jax.ShapeDtypeStruct((1, 128), F32)),
        grid=(s_len // tm,),
        in_specs=[row, row],
        out_specs=(row, pl.BlockSpec((1, 128), lambda i: (0, 0))),
        compiler_params=_params("arbitrary"),
    )(h, target)


def _rms_backward(x, gain, dn, dres, tm, name):
    s_len, d = x.shape

    def body(x_ref, g_ref, dn_ref, dres_ref, dx_ref, dg_ref):
        _, vjp = jax.vjp(_rms, x_ref[...], g_ref[...])
        dx, dg = vjp(dn_ref[...])
        dx_ref[...] = dres_ref[...] + dx
        _acc(dg_ref, dg, pl.program_id(0) == 0)

    row = pl.BlockSpec((tm, d), lambda i: (i, 0))
    vec = pl.BlockSpec((1, d), lambda i: (0, 0))
    return pl.pallas_call(
        body, name=name,
        out_shape=(jax.ShapeDtypeStruct((s_len, d), F32), jax.ShapeDtypeStruct((1, d), F32)),
        grid=(s_len // tm,),
        in_specs=[row, vec, row, row],
        out_specs=(row, vec),
        compiler_params=_params("arbitrary"),
    )(x, gain, dn, dres)


def _ple_backward_gate(dh2, gate, pp, tm):
    s_len, d = dh2.shape

    def body(dh_ref, gate_ref, pp_ref, dg_ref, dpp_ref):
        dh = dh_ref[...]
        gate = gate_ref[...].astype(F32)
        dg_ref[...] = (dh * pp_ref[...].astype(F32) * gate * (1.0 - gate)).astype(BF16)
        dpp_ref[...] = (dh * gate).astype(BF16)

    row = pl.BlockSpec((tm, d), lambda i: (i, 0))
    return pl.pallas_call(
        body, name="ple_backward_gate",
        out_shape=(jax.ShapeDtypeStruct((s_len, d), BF16), jax.ShapeDtypeStruct((s_len, d), BF16)),
        grid=(s_len // tm,),
        in_specs=[row, row, row],
        out_specs=(row, row),
        compiler_params=_params("parallel"),
    )(dh2, gate, pp)


def _sgu_in_specs(tm):
    return [pl.BlockSpec((tm, W_A), lambda i: (i, O_A // W_A)),
            pl.BlockSpec((A_HEADS, HEAD), lambda i: (0, 0)), pl.BlockSpec((A_HEADS, CHUNK, CHUNK), lambda i: (0, 0, 0)),
            pl.BlockSpec((A_HEADS, CHUNK, 1), lambda i: (0, 0, 0)), pl.BlockSpec((1, 512), lambda i: (0, 0))]


def _sgu_load(a_ref, gain_ref, ws_ref, bs_ref, ga_ref, c):
    rows = slice(c * CHUNK, (c + 1) * CHUNK)
    heads = range(A_HEADS)
    u = tuple(a_ref[rows, h * HEAD:(h + 1) * HEAD].astype(F32) for h in heads)
    v = tuple(a_ref[rows, 512 + h * HEAD:512 + (h + 1) * HEAD].astype(F32) for h in heads)
    z = tuple(a_ref[rows, 1024 + h * HEAD:1024 + (h + 1) * HEAD].astype(F32) for h in heads)
    gain = tuple(gain_ref[h:h + 1, :] for h in heads)
    ws = tuple(ws_ref[h] for h in heads)
    bs = tuple(bs_ref[h] for h in heads)
    ga = tuple(ga_ref[:, h * HEAD:(h + 1) * HEAD] for h in heads)
    return u, v, z, gain, ws, bs, ga


def _sgu_forward(proj, gain, ws, bs, ga, tm):
    s_len = proj.shape[0]

    def body(a_ref, gain_ref, ws_ref, bs_ref, ga_ref, o_ref):
        for c in range(tm // CHUNK):
            out = _sgu_chunk(*_sgu_load(a_ref, gain_ref, ws_ref, bs_ref, ga_ref, c))
            for h in range(A_HEADS):
                o_ref[c * CHUNK:(c + 1) * CHUNK, h * HEAD:(h + 1) * HEAD] = out[h].astype(BF16)

    return pl.pallas_call(
        body, name="sgu_forward",
        out_shape=jax.ShapeDtypeStruct((s_len, 512), BF16),
        grid=(s_len // tm,),
        in_specs=_sgu_in_specs(tm),
        out_specs=pl.BlockSpec((tm, 512), lambda i: (i, 0)),
        compiler_params=_params("parallel"),
    )(proj, gain, ws, bs, ga)


def _sgu_backward(proj, gain, ws, bs, ga, dy, dproj, tm):
    s_len = proj.shape[0]

    def body(a_ref, gain_ref, ws_ref, bs_ref, ga_ref, dy_ref, _, da_ref, dgain_ref, dws_ref, dbs_ref, dga_ref):
        tot = None
        for c in range(tm // CHUNK):
            args = _sgu_load(a_ref, gain_ref, ws_ref, bs_ref, ga_ref, c)
            _, vjp = jax.vjp(_sgu_chunk, *args)
            rows = slice(c * CHUNK, (c + 1) * CHUNK)
            du, dv, dz, dgain, dws, dbs, dga = vjp(tuple(dy_ref[rows, h * HEAD:(h + 1) * HEAD] for h in range(A_HEADS)))
            for h in range(A_HEADS):
                da_ref[rows, h * HEAD:(h + 1) * HEAD] = du[h].astype(BF16)
                da_ref[rows, 512 + h * HEAD:512 + (h + 1) * HEAD] = dv[h].astype(BF16)
                da_ref[rows, 1024 + h * HEAD:1024 + (h + 1) * HEAD] = dz[h].astype(BF16)
            part = (dgain, dws, dbs, dga)
            tot = part if tot is None else jax.tree.map(jnp.add, tot, part)
        dgain, dws, dbs, dga = tot
        first = pl.program_id(0) == 0
        _acc(dgain_ref, jnp.concatenate(dgain, axis=0), first)
        _acc(dga_ref, jnp.concatenate(dga, axis=-1), first)
        for h in range(A_HEADS):
            _acc(dws_ref.at[h], dws[h], first)
            _acc(dbs_ref.at[h], dbs[h], first)

    small = [pl.BlockSpec((A_HEADS, HEAD), lambda i: (0, 0)), pl.BlockSpec((A_HEADS, CHUNK, CHUNK), lambda i: (0, 0, 0)),
             pl.BlockSpec((A_HEADS, CHUNK, 1), lambda i: (0, 0, 0)), pl.BlockSpec((1, 512), lambda i: (0, 0))]
    return pl.pallas_call(
        body, name="sgu_backward",
        out_shape=(jax.ShapeDtypeStruct(dproj.shape, BF16),
                   jax.ShapeDtypeStruct((A_HEADS, HEAD), F32), jax.ShapeDtypeStruct((A_HEADS, CHUNK, CHUNK), F32),
                   jax.ShapeDtypeStruct((A_HEADS, CHUNK, 1), F32), jax.ShapeDtypeStruct((1, 512), F32)),
        grid=(s_len // tm,),
        in_specs=_sgu_in_specs(tm) + [pl.BlockSpec((tm, 512), lambda i: (i, 0)), pl.BlockSpec(memory_space=pl.ANY)],
        out_specs=(pl.BlockSpec((tm, W_A), lambda i: (i, O_A // W_A)), *small),
        input_output_aliases={6: 0},
        compiler_params=_params("arbitrary"),
    )(proj, gain, ws, bs, ga, dy, dproj)


def _halo_specs(tm, width, col, n_rows):
    per = tm // HALO
    last = n_rows // HALO - 1
    return [pl.BlockSpec((HALO, width), lambda i: (jnp.maximum(i * per - 1, 0), col)),
            pl.BlockSpec((tm, width), lambda i: (i, col)),
            pl.BlockSpec((HALO, width), lambda i: (jnp.minimum((i + 1) * per, last), col))]


def _conv_masks(tm, s_len):
    r = lax.broadcasted_iota(jnp.int32, (tm + 2 * HALO, 1), 0)
    g = pl.program_id(0) * tm - HALO + r
    return (g >= 0) & (g < s_len), (r >= HALO) & (r < HALO + tm)


def _conv_inputs(b_refs, cw_ref, cb_ref, gb_ref):
    ext = jnp.concatenate([r[...] for r in b_refs], axis=0).astype(F32)
    bb, bc, bh, bz = (ext[:, j * 512:(j + 1) * 512] for j in range(4))
    prm = (cw_ref[0:1, :], cw_ref[1:2, :], cw_ref[2:3, :], cb_ref[...], gb_ref[...])
    return (bb, bc, bh, bz), prm


def _conv_forward(proj, cw, cb, gb, tm):
    s_len = proj.shape[0]

    def body(p0, p1, p2, cw_ref, cb_ref, gb_ref, o_ref):
        acts, prm = _conv_inputs((p0, p1, p2), cw_ref, cb_ref, gb_ref)
        valid, core = _conv_masks(tm, s_len)
        y = _conv_tile(*acts, *prm, *prm, valid, core)
        o_ref[...] = y[HALO:HALO + tm].astype(BF16)

    vec = pl.BlockSpec((1, 512), lambda i: (0, 0))
    return pl.pallas_call(
        body, name="conv_forward",
        out_shape=jax.ShapeDtypeStruct((s_len, 512), BF16),
        grid=(s_len // tm,),
        in_specs=_halo_specs(tm, W_B, O_B // W_B, s_len) + [pl.BlockSpec((3, 512), lambda i: (0, 0)), vec, vec],
        out_specs=pl.BlockSpec((tm, 512), lambda i: (i, 0)),
        compiler_params=_params("parallel"),
    )(proj, proj, proj, cw, cb, gb)


def _conv_backward(proj, cw, cb, gb, dy, tm):
    s_len = proj.shape[0]

    def body(p0, p1, p2, cw_ref, cb_ref, gb_ref, d0, d1, d2, db_ref, dcw_ref, dcb_ref, dgb_ref):
        acts, prm = _conv_inputs((p0, p1, p2), cw_ref, cb_ref, gb_ref)
        valid, core = _conv_masks(tm, s_len)
        _, vjp = jax.vjp(lambda a, p: _conv_tile(*a, *p, *prm, valid, core), acts, prm)
        dy_ext = jnp.where(valid, jnp.concatenate([d0[...], d1[...], d2[...]], axis=0), 0.0)
        dacts, dprm = vjp(dy_ext)
        for j in range(4):
            db_ref[:, j * 512:(j + 1) * 512] = dacts[j][HALO:HALO + tm].astype(BF16)
        first = pl.program_id(0) == 0
        _acc(dcw_ref, jnp.concatenate(dprm[0:3], axis=0), first)
        _acc(dcb_ref, dprm[3], first)
        _acc(dgb_ref, dprm[4], first)

    vec = pl.BlockSpec((1, 512), lambda i: (0, 0))
    mat = pl.BlockSpec((3, 512), lambda i: (0, 0))
    return pl.pallas_call(
        body, name="conv_backward",
        out_shape=(jax.ShapeDtypeStruct((s_len, PROJ_W), BF16), jax.ShapeDtypeStruct((3, 512), F32),
                   jax.ShapeDtypeStruct((1, 512), F32), jax.ShapeDtypeStruct((1, 512), F32)),
        grid=(s_len // tm,),
        in_specs=_halo_specs(tm, W_B, O_B // W_B, s_len) + [mat, vec, vec] + _halo_specs(tm, 512, 1, s_len),
        out_specs=(pl.BlockSpec((tm, W_B), lambda i: (i, O_B // W_B)), mat, vec, vec),
        compiler_params=_params("arbitrary"),
    )(proj, proj, proj, cw, cb, gb, dy, dy, dy)


def _cgate_forward(o, proj, gc, tm):
    s_len = o.shape[0]

    def body(o_ref, cz_ref, gc_ref, y_ref):
        y_ref[...] = _cgate_tile(o_ref[...], cz_ref[...].astype(F32), gc_ref[...]).astype(BF16)

    row = pl.BlockSpec((tm, W_CZ), lambda i: (i, 0))
    return pl.pallas_call(
        body, name="cgate_forward",
        out_shape=jax.ShapeDtypeStruct((s_len, W_CZ), BF16),
        grid=(s_len // tm,),
        in_specs=[row, pl.BlockSpec((tm, W_CZ), lambda i: (i, O_CZ // W_CZ)), pl.BlockSpec((1, W_CZ), lambda i: (0, 0))],
        out_specs=row,
        compiler_params=_params("parallel"),
    )(o, proj, gc)


def _cgate_backward(o, proj, gc, dy, dproj, tm):
    s_len = o.shape[0]

    def body(o_ref, cz_ref, gc_ref, dy_ref, _, dcz_ref, do_ref, dsum_ref, dgc_ref):
        o = o_ref[...]
        _, vjp = jax.vjp(_cgate_tile, o, cz_ref[...].astype(F32), gc_ref[...])
        do, dcz, dgc = vjp(dy_ref[...])
        dcz_ref[...] = dcz.astype(BF16)
        do_ref[...] = do.astype(BF16)
        for h in range(C_HEADS):
            cols = slice(h * HEAD, (h + 1) * HEAD)
            dsum_ref[h] = jnp.sum(do[:, cols] * o[:, cols], axis=-1, keepdims=True)
        _acc(dgc_ref, dgc, pl.program_id(0) == 0)

    row = pl.BlockSpec((tm, W_CZ), lambda i: (i, 0))
    vec = pl.BlockSpec((1, W_CZ), lambda i: (0, 0))
    return pl.pallas_call(
        body, name="cgate_backward",
        out_shape=(jax.ShapeDtypeStruct(dproj.shape, BF16), jax.ShapeDtypeStruct((s_len, W_CZ), BF16),
                   jax.ShapeDtypeStruct((C_HEADS, s_len, 1), F32), jax.ShapeDtypeStruct((1, W_CZ), F32)),
        grid=(s_len // tm,),
        in_specs=[row, pl.BlockSpec((tm, W_CZ), lambda i: (i, O_CZ // W_CZ)), vec,
                  pl.BlockSpec((tm, W_CZ), lambda i: (i, 1)), pl.BlockSpec(memory_space=pl.ANY)],
        out_specs=(pl.BlockSpec((tm, W_CZ), lambda i: (i, O_CZ // W_CZ)), row,
                   pl.BlockSpec((C_HEADS, tm, 1), lambda i: (0, i, 0)), vec),
        input_output_aliases={4: 0},
        compiler_params=_params("arbitrary"),
    )(o, proj, gc, dy, dproj)


def _mla_small_specs():
    return [pl.BlockSpec((KV_RANK, 2 * C_HEADS * HEAD), lambda i: (0, 0)), pl.BlockSpec((1, KV_RANK), lambda i: (0, 0)),
            pl.BlockSpec((1, HEAD), lambda i: (0, 0)), pl.BlockSpec((1, ROPE), lambda i: (0, 0)),
            pl.BlockSpec((1, HEAD), lambda i: (0, 0)), pl.BlockSpec((1, ROPE), lambda i: (0, 0))]


def _mla_load(m_ref, cos_ref, sin_ref):
    qn = m_ref[:, M_QN:M_QN + C_HEADS * HEAD].astype(F32)
    qr = m_ref[:, M_QR:M_QR + C_HEADS * ROPE].astype(F32)
    ckv = m_ref[:, M_CKV:M_CKV + KV_RANK].astype(F32)
    kr = m_ref[:, M_KR:M_KR + ROPE].astype(F32)
    return qn, qr, ckv, kr, cos_ref[...], sin_ref[...]


def _mla_forward(proj, cosf, sins, wukv, kvg, qng, qrg, kng, krg, tm):
    s_len = proj.shape[0]

    def body(m_ref, cos_ref, sin_ref, w_ref, kvg_ref, qng_ref, qrg_ref, kng_ref, krg_ref, q_ref, k_ref, v_ref):
        q, k, v = _mla_tile(*_mla_load(m_ref, cos_ref, sin_ref), w_ref[...], kvg_ref[...], qng_ref[...], qrg_ref[...],
                            kng_ref[...], krg_ref[...])
        for h in range(C_HEADS):
            q_ref[h] = q[h].astype(BF16)
            k_ref[h] = k[h].astype(BF16)
            v_ref[h] = v[h].astype(BF16)

    rope_spec = pl.BlockSpec((tm, ROPE), lambda i: (i, 0))
    return pl.pallas_call(
        body, name="mla_forward",
        out_shape=(jax.ShapeDtypeStruct((C_HEADS, s_len, QK), BF16), jax.ShapeDtypeStruct((C_HEADS, s_len, QK), BF16),
                   jax.ShapeDtypeStruct((C_HEADS, s_len, HEAD), BF16)),
        grid=(s_len // tm,),
        in_specs=[pl.BlockSpec((tm, W_M), lambda i: (i, O_M // W_M)), rope_spec, rope_spec] + _mla_small_specs(),
        out_specs=(pl.BlockSpec((C_HEADS, tm, QK), lambda i: (0, i, 0)), pl.BlockSpec((C_HEADS, tm, QK), lambda i: (0, i, 0)),
                   pl.BlockSpec((C_HEADS, tm, HEAD), lambda i: (0, i, 0))),
        compiler_params=_params("parallel"),
    )(proj, cosf, sins, wukv, kvg, qng, qrg, kng, krg)


def _mla_backward(proj, cosf, sins, wukv, kvg, qng, qrg, kng, krg, dq, dk, dv, dproj, tm):
    s_len = proj.shape[0]

    def body(m_ref, cos_ref, sin_ref, w_ref, kvg_ref, qng_ref, qrg_ref, kng_ref, krg_ref, dq_ref, dk_ref, dv_ref, _,
             dm_ref, dw_ref, dkvg_ref, dqng_ref, dqrg_ref, dkng_ref, dkrg_ref):
        qn, qr, ckv, kr, cosf_t, sins_t = _mla_load(m_ref, cos_ref, sin_ref)
        prm = (w_ref[...], kvg_ref[...], qng_ref[...], qrg_ref[...], kng_ref[...], krg_ref[...])
        _, vjp = jax.vjp(lambda a, p: _mla_tile(*a, cosf_t, sins_t, *p), (qn, qr, ckv, kr), prm)
        heads = range(C_HEADS)
        dacts, dprm = vjp((tuple(dq_ref[h] for h in heads), tuple(dk_ref[h] for h in heads), tuple(dv_ref[h] for h in heads)))
        dm_ref[:, M_QN:M_QN + C_HEADS * HEAD] = dacts[0].astype(BF16)
        dm_ref[:, M_QR:M_QR + C_HEADS * ROPE] = dacts[1].astype(BF16)
        dm_ref[:, M_CKV:M_CKV + KV_RANK] = dacts[2].astype(BF16)
        pad = jnp.zeros((tm, W_M - M_KR - ROPE), F32)
        dm_ref[:, M_KR:W_M] = jnp.concatenate([dacts[3], pad], axis=-1).astype(BF16)
        first = pl.program_id(0) == 0
        for ref, val in zip((dw_ref, dkvg_ref, dqng_ref, dqrg_ref, dkng_ref, dkrg_ref), dprm):
            _acc(ref, val.astype(F32), first)

    rope_spec = pl.BlockSpec((tm, ROPE), lambda i: (i, 0))
    qk_spec = pl.BlockSpec((C_HEADS, tm, QK), lambda i: (0, i, 0))
    small = _mla_small_specs()
    return pl.pallas_call(
        body, name="mla_backward",
        out_shape=(jax.ShapeDtypeStruct(dproj.shape, BF16), jax.ShapeDtypeStruct((KV_RANK, 2 * C_HEADS * HEAD), F32),
                   jax.ShapeDtypeStruct((1, KV_RANK), F32), jax.ShapeDtypeStruct((1, HEAD), F32),
                   jax.ShapeDtypeStruct((1, ROPE), F32), jax.ShapeDtypeStruct((1, HEAD), F32),
                   jax.ShapeDtypeStruct((1, ROPE), F32)),
        grid=(s_len // tm,),
        in_specs=[pl.BlockSpec((tm, W_M), lambda i: (i, O_M // W_M)), rope_spec, rope_spec] + small
                 + [qk_spec, qk_spec, pl.BlockSpec((C_HEADS, tm, HEAD), lambda i: (0, i, 0)), pl.BlockSpec(memory_space=pl.ANY)],
        out_specs=(pl.BlockSpec((tm, W_M), lambda i: (i, O_M // W_M)), *small),
        input_output_aliases={12: 0},
        compiler_params=_params("arbitrary"),
    )(proj, cosf, sins, wukv, kvg, qng, qrg, kng, krg, dq, dk, dv, dproj)


def _attention_forward(q, k, v, tq, tk):
    n_heads, s_len, _ = q.shape
    n_kv = s_len // tk

    def body(q_ref, k_ref, v_ref, o_ref, lse_ref, m_sc, l_sc, acc_sc):
        j = pl.program_id(2)

        @pl.when(j == 0)
        def _():
            m_sc[...] = jnp.full_like(m_sc, -jnp.inf)
            l_sc[...] = jnp.zeros_like(l_sc)
            acc_sc[...] = jnp.zeros_like(acc_sc)

        s = lax.dot_general(q_ref[0], k_ref[0], _NT, preferred_element_type=F32) * SCALE
        m_old = m_sc[...]
        m_new = jnp.maximum(m_old, jnp.max(s, axis=-1, keepdims=True))
        p = jnp.exp(s - m_new)
        alpha = jnp.exp(m_old - m_new)
        l_sc[...] = alpha * l_sc[...] + jnp.sum(p, axis=-1, keepdims=True)
        acc_sc[...] = alpha * acc_sc[...] + jnp.dot(p.astype(BF16), v_ref[0], preferred_element_type=F32)
        m_sc[...] = m_new

        @pl.when(j == n_kv - 1)
        def _():
            o_ref[...] = acc_sc[...] / l_sc[...]
            lse_ref[0] = m_sc[...] + jnp.log(l_sc[...])

    return pl.pallas_call(
        body, name="attention_forward",
        out_shape=(jax.ShapeDtypeStruct((s_len, n_heads * HEAD), F32), jax.ShapeDtypeStruct((n_heads, s_len, 1), F32)),
        grid=(n_heads, s_len // tq, n_kv),
        in_specs=[pl.BlockSpec((1, tq, QK), lambda h, i, j: (h, i, 0)), pl.BlockSpec((1, tk, QK), lambda h, i, j: (h, j, 0)),
                  pl.BlockSpec((1, tk, HEAD), lambda h, i, j: (h, j, 0))],
        out_specs=(pl.BlockSpec((tq, HEAD), lambda h, i, j: (i, h)), pl.BlockSpec((1, tq, 1), lambda h, i, j: (h, i, 0))),
        scratch_shapes=[pltpu.VMEM((tq, 1), F32), pltpu.VMEM((tq, 1), F32), pltpu.VMEM((tq, HEAD), F32)],
        compiler_params=_params("parallel", "parallel", "arbitrary"),
    )(q, k, v)


def _attention_backward(q, k, v, do, lse, dsum, tq, tk):
    n_heads, s_len, _ = q.shape
    n_q = s_len // tq

    def body(q_ref, k_ref, v_ref, do_ref, lse_ref, dsum_ref, dq_ref, dk_ref, dv_ref):
        j, i = pl.program_id(1), pl.program_id(2)
        q_t, k_t, do_t = q_ref[0], k_ref[0], do_ref[...]
        s = lax.dot_general(q_t, k_t, _NT, preferred_element_type=F32) * SCALE
        p = jnp.exp(s - lse_ref[0])
        dp = lax.dot_general(do_t, v_ref[0], _NT, preferred_element_type=F32)
        ds = (p * (dp - dsum_ref[0]) * SCALE).astype(BF16)
        dv_part = lax.dot_general(p.astype(BF16), do_t, _TN, preferred_element_type=F32)
        dk_part = lax.dot_general(ds, q_t, _TN, preferred_element_type=F32)
        _acc(dv_ref.at[0], dv_part, i == 0)
        _acc(dk_ref.at[0], dk_part, i == 0)
        rows = pl.ds(pl.multiple_of(i * tq, tq), tq)
        dq_part = jnp.dot(ds, k_t, preferred_element_type=F32)
        _acc(dq_ref.at[0, rows, :], dq_part, j == 0)

    stat = pl.BlockSpec((1, tq, 1), lambda h, j, i: (h, i, 0))
    return pl.pallas_call(
        body, name="attention_backward",
        out_shape=(jax.ShapeDtypeStruct((n_heads, s_len, QK), F32), jax.ShapeDtypeStruct((n_heads, s_len, QK), F32),
                   jax.ShapeDtypeStruct((n_heads, s_len, HEAD), F32)),
        grid=(n_heads, s_len // tk, n_q),
        in_specs=[pl.BlockSpec((1, tq, QK), lambda h, j, i: (h, i, 0)), pl.BlockSpec((1, tk, QK), lambda h, j, i: (h, j, 0)),
                  pl.BlockSpec((1, tk, HEAD), lambda h, j, i: (h, j, 0)), pl.BlockSpec((tq, HEAD), lambda h, j, i: (i, h)),
                  stat, stat],
        out_specs=(pl.BlockSpec((1, s_len, QK), lambda h, j, i: (h, 0, 0)), pl.BlockSpec((1, tk, QK), lambda h, j, i: (h, j, 0)),
                   pl.BlockSpec((1, tk, HEAD), lambda h, j, i: (h, j, 0))),
        compiler_params=_params("parallel", "arbitrary", "arbitrary"),
    )(q, k, v, do, lse, dsum)


def _exchange(arrs, gather, name):
    n = len(arrs)
    n_peers = N_DEV - 1

    def body(*refs):
        ins, outs = refs[:n], refs[n:2 * n]
        send_sems, recv_sems, local_sems = refs[2 * n:]
        x, y, c = lax.axis_index("x"), lax.axis_index("y"), lax.axis_index("c")
        me = 4 * x + 2 * y + c

        def block_for(a, dev):
            return ins[a] if gather[a] else ins[a].at[dev]

        local = []
        for a in range(n):
            cp = pltpu.make_async_copy(block_for(a, me), outs[a].at[me], local_sems.at[a])
            cp.start()
            local.append(cp)
        remote = []
        for k in range(1, N_DEV):
            px = 1 - x if k & 4 else x
            py = 1 - y if k & 2 else y
            pc = 1 - c if k & 1 else c
            peer = 4 * px + 2 * py + pc
            for a in range(n):
                idx = a * n_peers + k - 1
                send = pltpu.make_async_remote_copy(
                    src_ref=block_for(a, peer), dst_ref=outs[a].at[me], send_sem=send_sems.at[idx], recv_sem=recv_sems.at[idx],
                    device_id=(px, py, pc), device_id_type=pl.DeviceIdType.MESH)
                send.start()
                arrive = pltpu.make_async_remote_copy(
                    src_ref=block_for(a, peer), dst_ref=outs[a].at[peer], send_sem=send_sems.at[idx], recv_sem=recv_sems.at[idx],
                    device_id=(px, py, pc), device_id_type=pl.DeviceIdType.MESH)
                remote.append((send, arrive))
        for send, arrive in remote:
            send.wait_send()
            arrive.wait_recv()
        for cp in local:
            cp.wait()

    def out_shape(a):
        blk = arrs[a].shape if gather[a] else arrs[a].shape[1:]
        return jax.ShapeDtypeStruct((N_DEV, *blk), arrs[a].dtype)

    any_spec = pl.BlockSpec(memory_space=pl.ANY)
    return pl.pallas_call(
        body, name=name,
        out_shape=tuple(out_shape(a) for a in range(n)),
        in_specs=[any_spec] * n,
        out_specs=tuple([any_spec] * n),
        scratch_shapes=[pltpu.SemaphoreType.DMA((n * n_peers,)), pltpu.SemaphoreType.DMA((n * n_peers,)),
                        pltpu.SemaphoreType.DMA((n,))],
        compiler_params=pltpu.CompilerParams(has_side_effects=True),
    )(*arrs)


ADAM_TILE_ELEMS = 256 * 1024


def _sum_adam(parts, w, m, v, name):
    n_parts, r, c = parts.shape
    tm = r
    while tm * c > ADAM_TILE_ELEMS and tm % 16 == 0:
        tm //= 2

    def body(p_ref, w_ref, m_ref, v_ref, g_ref, d_ref, nm_ref, nv_ref):
        g = p_ref[0].astype(F32)
        for s in range(1, n_parts):
            g = g + p_ref[s].astype(F32)
        m_new = ADAM_B1 * m_ref[...] + (1.0 - ADAM_B1) * g
        v_new = ADAM_B2 * v_ref[...] + (1.0 - ADAM_B2) * (g * g)
        m_hat = m_new / (1.0 - ADAM_B1 ** ADAM_STEP)
        v_hat = v_new / (1.0 - ADAM_B2 ** ADAM_STEP)
        g_ref[...] = g
        d_ref[...] = -ADAM_LR * (m_hat / (jnp.sqrt(v_hat) + ADAM_EPS) + ADAM_WD * w_ref[...])
        nm_ref[...] = m_new
        nv_ref[...] = v_new

    row = pl.BlockSpec((tm, c), lambda i: (i, 0))
    return pl.pallas_call(
        body, name=name,
        out_shape=(jax.ShapeDtypeStruct((r, c), F32),) * 4,
        grid=(r // tm,),
        in_specs=[pl.BlockSpec((n_parts, tm, c), lambda i: (0, i, 0)), row, row, row],
        out_specs=(row, row, row, row),
        compiler_params=_params("parallel"),
    )(parts, w, m, v)


def _permute_in(w):
    k = w.shape[0]
    q = w[:, 3584:5120].reshape(k, C_HEADS, QK)
    return jnp.concatenate(
        [w[:, 1536:3584], w[:, 5696:6720], w[:, 0:1536], q[:, :, :HEAD].reshape(k, C_HEADS * HEAD),
         q[:, :, HEAD:].reshape(k, C_HEADS * ROPE), w[:, 5120:5632], w[:, 5632:5696],
         jnp.zeros((k, PROJ_W - IN_WIDTH), w.dtype)], axis=1)


def _unpermute_in(g):
    k = g.shape[0]
    qn = g[:, O_M + M_QN:O_M + M_QR].reshape(k, C_HEADS, HEAD)
    qr = g[:, O_M + M_QR:O_M + M_CKV].reshape(k, C_HEADS, ROPE)
    q = jnp.concatenate([qn, qr], axis=-1).reshape(k, C_HEADS * QK)
    return jnp.concatenate(
        [g[:, O_A:O_A + W_A], g[:, O_B:O_B + W_B], q, g[:, O_M + M_CKV:O_M + M_KR],
         g[:, O_M + M_KR:O_M + M_KR + ROPE], g[:, O_CZ:O_CZ + W_CZ]], axis=1)


SMALL = ("attn_norm", "sgu_norm", "w_spatial", "b_spatial", "conv_b", "kv_norm", "q_nope_norm", "q_rope_norm",
         "k_nope_norm", "k_rope_norm", "out_norm", "ple_norm")
PACK_ROWS = 256


def _pack(tensors):
    flat = jnp.concatenate([t.reshape(-1) for t in tensors])
    rows = -(-flat.shape[0] // (128 * PACK_ROWS)) * PACK_ROWS
    return jnp.pad(flat, (0, rows * 128 - flat.shape[0])).reshape(rows, 128)


def _unpack(packed, like):
    flat = packed.reshape(-1)
    out, pos = [], 0
    for t in like:
        out.append(flat[pos:pos + t.size].reshape(t.shape))
        pos += t.size
    return out


def _tile(s_len, want):
    return min(want, s_len)


def _layer_forward(h, p_l, cosf, sins, w, sm):
    s_len = h.shape[0]
    tm = _tile(s_len, 512)
    proj, hn = _norm_matmul(h, sm["attn_norm"], w["w_in"], tm, 768)
    ga, gb, gc = sm["out_norm"][:, 0:512], sm["out_norm"][:, 512:1024], sm["out_norm"][:, 1024:2048]
    ya = _sgu_forward(proj, sm["sgu_norm"], sm["w_spatial"], sm["b_spatial"], ga, _tile(s_len, 256))
    yb = _conv_forward(proj, w["conv_w"], sm["conv_b"], gb, _tile(s_len, 256))
    q, k, v = _mla_forward(proj, cosf, sins, w["w_ukv"], sm["kv_norm"], sm["q_nope_norm"], sm["q_rope_norm"],
                           sm["k_nope_norm"], sm["k_rope_norm"], _tile(s_len, 256))
    o, lse = _attention_forward(q, k, v, _tile(s_len, 512), _tile(s_len, 512))
    yc = _cgate_forward(o, proj, gc, _tile(s_len, 256))
    h1 = _out_matmul(h, ya, yb, yc, w["w_out"], tm, 512)
    h2, n1, gate, pp = _ple_forward(h1, sm["ple_norm"], p_l, w["w_ple_gate"], w["w_ple_proj"], tm, 512)
    saved = dict(h=h, hn=hn, proj=proj, ya=ya, yb=yb, yc=yc, q=q, k=k, v=v, o=o, lse=lse, h1=h1, n1=n1, gate=gate, pp=pp)
    return h2, saved


def _layer_backward(dh2, p_l, cosf, sins, w, sm, sv):
    s_len = dh2.shape[0]
    tm = _tile(s_len, 512)
    tr = _tile(s_len, 256)
    big, small = {}, {}
    dgp, dpp = _ple_backward_gate(dh2, sv["gate"], sv["pp"], tm)
    big["w_ple_proj"] = _matmul_tn(p_l, dpp, _tile(s_len, 1024), PLE_DIM, 1024, "grad_w_ple_proj")
    big["w_ple_gate"] = _matmul_tn(sv["n1"], dgp, _tile(s_len, 1024), 512, 1024, "grad_w_ple_gate")
    dn1 = _matmul_nt(dgp, w["w_ple_gate"], tm, 1024, "grad_ple_norm_in")
    dh1, small["ple_norm"] = _rms_backward(sv["h1"], sm["ple_norm"], dn1, dh2, tr, "ple_norm_backward")
    dy = _matmul_nt(dh1, w["w_out"], tm, 1024, "grad_branches")
    big["w_out"] = jnp.concatenate(
        [_matmul_tn(sv["ya"], dh1, _tile(s_len, 1024), 512, 1024, "grad_w_out_a"),
         _matmul_tn(sv["yb"], dh1, _tile(s_len, 1024), 512, 1024, "grad_w_out_b"),
         _matmul_tn(sv["yc"], dh1, _tile(s_len, 1024), 512, 1024, "grad_w_out_c")], axis=0)
    ga, gb, gc = sm["out_norm"][:, 0:512], sm["out_norm"][:, 512:1024], sm["out_norm"][:, 1024:2048]
    dproj, dcw, small["conv_b"], dgb = _conv_backward(sv["proj"], w["conv_w"], sm["conv_b"], gb, dy, tr)
    big["conv_w"] = dcw
    dproj, do, dsum, dgc = _cgate_backward(sv["o"], sv["proj"], gc, dy, dproj, tr)
    dq, dk, dv = _attention_backward(sv["q"], sv["k"], sv["v"], do, sv["lse"], dsum, _tile(s_len, 512), _tile(s_len, 512))
    (dproj, big["w_ukv"], small["kv_norm"], small["q_nope_norm"], small["q_rope_norm"], small["k_nope_norm"],
     small["k_rope_norm"]) = _mla_backward(sv["proj"], cosf, sins, w["w_ukv"], sm["kv_norm"], sm["q_nope_norm"],
                                            sm["q_rope_norm"], sm["k_nope_norm"], sm["k_rope_norm"], dq, dk, dv, dproj,
                                            _tile(s_len, 128))
    dproj, small["sgu_norm"], small["w_spatial"], small["b_spatial"], dga = _sgu_backward(
        sv["proj"], sm["sgu_norm"], sm["w_spatial"], sm["b_spatial"], ga, dy, dproj, tr)
    small["out_norm"] = jnp.concatenate([dga, dgb, dgc], axis=1)
    big["w_in"] = _matmul_tn(sv["hn"], dproj, _tile(s_len, 1024), 512, 2304, "grad_w_in")
    dhn = _matmul_nt(dproj, w["w_in"], tm, 512, "grad_attn_norm_in")
    dh, small["attn_norm"] = _rms_backward(sv["h"], sm["attn_norm"], dhn, dh1, tr, "attn_norm_backward")
    return dh, big, small


def _layer_small(params, layer):
    return dict(
        attn_norm=params["attn_norm"][layer][None, :], sgu_norm=params["sgu_norm"][layer],
        w_spatial=params["w_spatial"][layer], b_spatial=params["b_spatial"][layer][:, :, None],
        conv_b=params["conv_b"][layer][None, :], kv_norm=params["kv_norm"][layer][None, :],
        q_nope_norm=params["q_nope_norm"][layer][None, :], q_rope_norm=params["q_rope_norm"][layer][None, :],
        k_nope_norm=params["k_nope_norm"][layer][None, :], k_rope_norm=params["k_rope_norm"][layer][None, :],
        out_norm=params["out_norm"][layer][None, :], ple_norm=params["ple_norm"][layer][None, :])


def _step_local(xs, ps, pos, target, weights, params):
    inv = 1.0 / (ROPE_BASE ** (jnp.arange(0, ROPE, 2, dtype=F32) / ROPE))
    ang = pos.astype(F32)[:, None] * inv
    cos, sin = jnp.cos(ang), jnp.sin(ang)
    cosf = jnp.concatenate([cos, cos], axis=-1)
    sins = jnp.concatenate([-sin, sin], axis=-1)
    h = xs
    saved = []
    smalls = [_layer_small(params, layer) for layer in range(DEPTH)]
    for layer in range(DEPTH):
        h, sv = _layer_forward(h, ps[layer], cosf, sins, weights[layer], smalls[layer])
        saved.append(sv)
    dh, loss = _loss_grad(h, target, _tile(h.shape[0], 512))
    bigs, small_grads = [None] * DEPTH, [None] * DEPTH
    for layer in reversed(range(DEPTH)):
        dh, bigs[layer], small_grads[layer] = _layer_backward(dh, ps[layer], cosf, sins, weights[layer], smalls[layer],
                                                              saved[layer])
    return loss, dh, bigs, small_grads


BIG = ("w_in", "conv_w", "w_ukv", "w_out", "w_ple_gate", "w_ple_proj")


def _gather_weights(shards):
    send = [shards["w_in"].astype(BF16), shards["conv_w"].reshape(-1, 128), shards["w_ukv"].astype(BF16),
            shards["w_out"].astype(BF16), shards["w_ple_gate"].astype(BF16), shards["w_ple_proj"].astype(BF16)]
    g_in, g_cw, g_ukv, g_out, g_gate, g_proj = _exchange(send, [True] * len(send), "gather_weights")
    k_in = g_in.shape[2]
    conv_w = g_cw.reshape(N_DEV, DEPTH, 3, -1).transpose(1, 2, 0, 3).reshape(DEPTH, 3, -1)
    weights = []
    for layer in range(DEPTH):
        w_in = g_in[:, layer].transpose(1, 0, 2).reshape(k_in, IN_WIDTH)
        w_ukv = g_ukv[:, layer].reshape(N_DEV, KV_RANK, 2, HEAD).transpose(1, 2, 0, 3).reshape(KV_RANK, 2 * C_HEADS * HEAD)
        weights.append(dict(
            w_in=_permute_in(w_in), conv_w=conv_w[layer], w_ukv=w_ukv,
            w_out=g_out[:, layer].reshape(D_MODEL, D_MODEL), w_ple_gate=g_gate[:, layer].reshape(D_MODEL, D_MODEL),
            w_ple_proj=g_proj[:, layer].transpose(1, 0, 2).reshape(PLE_DIM, D_MODEL)))
    return weights


def _scatter_parts(bigs):
    def stack(name, fn):
        return jnp.stack([fn(b[name]) for b in bigs], axis=1)

    k_in = bigs[0]["w_in"].shape[0]
    parts = dict(
        w_in=stack("w_in", lambda g: _unpermute_in(g).reshape(k_in, N_DEV, -1).transpose(1, 0, 2)),
        conv_w=stack("conv_w", lambda g: g.reshape(3, N_DEV, -1).transpose(1, 0, 2)),
        w_ukv=stack("w_ukv", lambda g: g.reshape(KV_RANK, 2, N_DEV, HEAD).transpose(2, 0, 1, 3).reshape(N_DEV, KV_RANK, 2 * HEAD)),
        w_out=stack("w_out", lambda g: g.reshape(N_DEV, -1, D_MODEL)),
        w_ple_gate=stack("w_ple_gate", lambda g: g.reshape(N_DEV, -1, D_MODEL)),
        w_ple_proj=stack("w_ple_proj", lambda g: g.reshape(PLE_DIM, N_DEV, -1).transpose(1, 0, 2)))
    out = {}
    for name, t in parts.items():
        cols = 128 if name == "conv_w" else t.shape[-1]
        out[name] = t.astype(BF16).reshape(N_DEV, -1, cols)
    return out


def kernel(x, p, positions, attn_norm, w_in, sgu_norm, w_spatial, b_spatial, conv_w, conv_b, kv_norm, w_ukv, q_nope_norm, q_rope_norm, k_nope_norm, k_rope_norm, out_norm, w_out, ple_norm, w_ple_gate, w_ple_proj, loss_target, m_attn_norm, m_w_in, m_sgu_norm, m_w_spatial, m_b_spatial, m_conv_w, m_conv_b, m_kv_norm, m_w_ukv, m_q_nope_norm, m_q_rope_norm, m_k_nope_norm, m_k_rope_norm, m_out_norm, m_w_out, m_ple_norm, m_w_ple_gate, m_w_ple_proj, v_attn_norm, v_w_in, v_sgu_norm, v_w_spatial, v_b_spatial, v_conv_w, v_conv_b, v_kv_norm, v_w_ukv, v_q_nope_norm, v_q_rope_norm, v_k_nope_norm, v_k_rope_norm, v_out_norm, v_w_out, v_ple_norm, v_w_ple_gate, v_w_ple_proj):
    order = ("attn_norm", "w_in", "sgu_norm", "w_spatial", "b_spatial", "conv_w", "conv_b", "kv_norm", "w_ukv",
             "q_nope_norm", "q_rope_norm", "k_nope_norm", "k_rope_norm", "out_norm", "w_out", "ple_norm", "w_ple_gate",
             "w_ple_proj")
    wts = dict(zip(order, (attn_norm, w_in, sgu_norm, w_spatial, b_spatial, conv_w, conv_b, kv_norm, w_ukv, q_nope_norm,
                           q_rope_norm, k_nope_norm, k_rope_norm, out_norm, w_out, ple_norm, w_ple_gate, w_ple_proj)))
    mom = dict(zip(order, (m_attn_norm, m_w_in, m_sgu_norm, m_w_spatial, m_b_spatial, m_conv_w, m_conv_b, m_kv_norm, m_w_ukv,
                           m_q_nope_norm, m_q_rope_norm, m_k_nope_norm, m_k_rope_norm, m_out_norm, m_w_out, m_ple_norm,
                           m_w_ple_gate, m_w_ple_proj)))
    var = dict(zip(order, (v_attn_norm, v_w_in, v_sgu_norm, v_w_spatial, v_b_spatial, v_conv_w, v_conv_b, v_kv_norm, v_w_ukv,
                           v_q_nope_norm, v_q_rope_norm, v_k_nope_norm, v_k_rope_norm, v_out_norm, v_w_out, v_ple_norm,
                           v_w_ple_gate, v_w_ple_proj)))

    weights = _gather_weights({n: wts[n] for n in BIG})
    loss_part, grad_x, bigs, small_grads = _step_local(x[0], p[:, 0], positions[0], loss_target[0], weights, wts)
    loss = lax.psum(loss_part[0, 0], ("x", "y", "c"))

    def small_grad(name):
        g = jnp.stack([sg[name] for sg in small_grads])
        return g.reshape(wts[name].shape)

    packed = _pack([small_grad(n) for n in SMALL])
    (small_parts,) = _exchange([packed], [True], "gather_small_grads")
    small_out = _sum_adam(small_parts, _pack([wts[n] for n in SMALL]), _pack([mom[n] for n in SMALL]),
                          _pack([var[n] for n in SMALL]), "adam_small")
    like = [wts[n] for n in SMALL]
    results = {n: vals for n, vals in zip(SMALL, zip(*[_unpack(o, like) for o in small_out]))}

    parts = _scatter_parts(bigs)
    received = _exchange([parts[n] for n in BIG], [False] * len(BIG), "scatter_grads")
    for name, part in zip(BIG, received):
        shape = wts[name].shape
        flat = (lambda t, part=part: t.reshape(part.shape[1:]))
        outs = _sum_adam(part, flat(wts[name]), flat(mom[name]), flat(var[name]), "adam_" + name)
        results[name] = tuple(o.reshape(shape) for o in outs)

    grads, deltas, new_m, new_v = ([results[n][j] for n in order] for j in range(4))
    return (loss, grad_x[None], *grads, *deltas, *new_m, *new_v)
```

```python
import functools

import jax
import jax.numpy as jnp
from jax import lax
from jax.experimental import pallas as pl
from jax.experimental.pallas import tpu as pltpu

F32 = jnp.float32
BF16 = jnp.bfloat16

N_DEV = 8
DEPTH = 4
D_MODEL = 2048
EPS = 1e-6
CHUNK = 128
A_HEADS = 4
HEAD = 128
ROPE = 64
HALF = ROPE // 2
C_HEADS = 8
KV_RANK = 512
PLE_DIM = 256
ROPE_BASE = 10000.0
IN_WIDTH = 6720
QK = HEAD + ROPE
SCALE = QK ** -0.5
LOG2_E = 1.4426950408889634
LN_2 = 0.6931471805599453
Q_SCALE = SCALE * LOG2_E
HALO = 8

O_B = 0
W_B = 2048
O_CZ = 2048
W_CZ = 1024
O_A = 3072
W_A = 1536
O_M = 4608
W_M = 2304
M_QN, M_QR, M_CKV, M_KR = 0, 1024, 1536, 2048
PROJ_W = 6912

ADAM_LR = 0.001
ADAM_B1 = 0.9
ADAM_B2 = 0.999
ADAM_EPS = 1e-08
ADAM_WD = 0.01
ADAM_STEP = 10

VMEM_LIMIT = 56 * 1024 * 1024

_NT = (((1,), (1,)), ((), ()))
_TN = (((0,), (0,)), ((), ()))


def _params(*sem):
    return pltpu.CompilerParams(dimension_semantics=sem, vmem_limit_bytes=VMEM_LIMIT)


@jax.custom_vjp
def _bdot(a, b):
    return jnp.dot(a.astype(BF16), b.astype(BF16), preferred_element_type=F32)


def _bdot_fwd(a, b):
    return _bdot(a, b), (a, b)


def _bdot_bwd(res, g):
    a, b = res
    gb = g.astype(BF16)
    da = lax.dot_general(gb, b.astype(BF16), _NT, preferred_element_type=F32)
    db = lax.dot_general(a.astype(BF16), gb, _TN, preferred_element_type=F32)
    return da.astype(a.dtype), db.astype(b.dtype)


_bdot.defvjp(_bdot_fwd, _bdot_bwd)


@functools.partial(jax.custom_vjp, nondiff_argnums=(1,))
def _split(x, n):
    w = x.shape[-1] // n
    return tuple(x[:, i * w:(i + 1) * w] for i in range(n))


def _split_fwd(x, n):
    return _split(x, n), None


def _split_bwd(n, _, gs):
    return (jnp.concatenate(gs, axis=-1),)


_split.defvjp(_split_fwd, _split_bwd)


@functools.partial(jax.custom_vjp, nondiff_argnums=(1,))
def _shift_rows(x, k):
    return pltpu.roll(x, k % x.shape[0], 0)


def _shift_rows_fwd(x, k):
    return _shift_rows(x, k), None


def _shift_rows_bwd(k, _, g):
    return (_shift_rows(g, -k),)


_shift_rows.defvjp(_shift_rows_fwd, _shift_rows_bwd)


@jax.custom_vjp
def _swap_halves(x):
    h = x.shape[-1] // 2
    return jnp.concatenate([x[:, h:], x[:, :h]], axis=-1)


def _swap_halves_fwd(x):
    return _swap_halves(x), None


def _swap_halves_bwd(_, g):
    return (_swap_halves(g),)


_swap_halves.defvjp(_swap_halves_fwd, _swap_halves_bwd)


def _rms(x, g):
    return x * lax.rsqrt(jnp.mean(x * x, axis=-1, keepdims=True) + EPS) * g


def _rope(x, cosf, sins):
    return x * cosf + _swap_halves(x) * sins


def _sgu_chunk(u, v, z, gain, ws, bs, ga):
    ys = []
    for h in range(A_HEADS):
        vn = _rms(v[h], gain[h])
        s = _bdot(ws[h], vn) + bs[h]
        ys.append(u[h] * s * jax.nn.silu(z[h]))
    ss = sum(jnp.sum(y * y, axis=-1, keepdims=True) for y in ys) * (1.0 / (A_HEADS * HEAD))
    r = lax.rsqrt(ss + EPS)
    return tuple(ys[h] * r * ga[h] for h in range(A_HEADS))


def _conv_tile(bb, bc, bh, bz, w0, w1, w2, cb, gb, w0h, w1h, w2h, cbh, gbh, valid, core):
    t = jnp.where(valid, bc * bh, 0.0)
    y = (jnp.where(core, cb, cbh)
         + _shift_rows(t, 1) * jnp.where(core, w0, w0h)
         + t * jnp.where(core, w1, w1h)
         + _shift_rows(t, -1) * jnp.where(core, w2, w2h))
    return _rms(bb * y * jax.nn.silu(bz), jnp.where(core, gb, gbh))


def _cgate_tile(o, cz, gc):
    return _rms(o * jax.nn.silu(cz), gc)


def _mla_tile(qn, qr, ckv, kr, cosf, sins, wukv, kvg, qng, qrg, kng, krg):
    kv = _split(_bdot(_rms(ckv, kvg), wukv), 2 * C_HEADS)
    k_r = _rope(_rms(kr, krg), cosf, sins)
    qn_h = _split(qn, C_HEADS)
    qr_h = _split(qr, C_HEADS)
    q, k, v = [], [], []
    for h in range(C_HEADS):
        q.append(jnp.concatenate([_rms(qn_h[h], qng), _rope(_rms(qr_h[h], qrg), cosf, sins)], axis=-1) * Q_SCALE)
        k.append(jnp.concatenate([_rms(kv[h], kng), k_r], axis=-1))
        v.append(kv[C_HEADS + h])
    return tuple(q), tuple(k), tuple(v)


def _norm_matmul(h, gain, w, tm, tn):
    s_len, k = h.shape
    n = w.shape[1]

    def body(h_ref, g_ref, w_ref, o_ref, hn_ref):
        @pl.when(pl.program_id(1) == 0)
        def _():
            hn_ref[...] = _rms(h_ref[...], g_ref[...]).astype(BF16)

        o_ref[...] = jnp.dot(hn_ref[...], w_ref[...], preferred_element_type=F32).astype(BF16)

    return pl.pallas_call(
        body, name="norm_matmul",
        out_shape=(jax.ShapeDtypeStruct((s_len, n), BF16), jax.ShapeDtypeStruct((s_len, k), BF16)),
        grid=(s_len // tm, n // tn),
        in_specs=[pl.BlockSpec((tm, k), lambda i, j: (i, 0)), pl.BlockSpec((1, k), lambda i, j: (0, 0)),
                  pl.BlockSpec((k, tn), lambda i, j: (0, j))],
        out_specs=(pl.BlockSpec((tm, tn), lambda i, j: (i, j)), pl.BlockSpec((tm, k), lambda i, j: (i, 0))),
        compiler_params=_params("parallel", "arbitrary"),
    )(h, gain, w)


def _out_matmul(h, ya, yb, yc, w, tm, tn):
    s_len, n = h.shape
    ka, kb = ya.shape[1], yb.shape[1]
    kc = yc.shape[1]

    def body(h_ref, ya_ref, yb_ref, yc_ref, w_ref, o_ref):
        acc = jnp.dot(ya_ref[...], w_ref[0:ka, :], preferred_element_type=F32)
        acc += jnp.dot(yb_ref[...], w_ref[ka:ka + kb, :], preferred_element_type=F32)
        acc += jnp.dot(yc_ref[...], w_ref[ka + kb:ka + kb + kc, :], preferred_element_type=F32)
        o_ref[...] = h_ref[...] + acc

    return pl.pallas_call(
        body, name="out_matmul",
        out_shape=jax.ShapeDtypeStruct((s_len, n), F32),
        grid=(s_len // tm, n // tn),
        in_specs=[pl.BlockSpec((tm, tn), lambda i, j: (i, j)), pl.BlockSpec((tm, ka), lambda i, j: (i, 0)),
                  pl.BlockSpec((tm, kb), lambda i, j: (i, 0)), pl.BlockSpec((tm, kc), lambda i, j: (i, 0)),
                  pl.BlockSpec((ka + kb + kc, tn), lambda i, j: (0, j))],
        out_specs=pl.BlockSpec((tm, tn), lambda i, j: (i, j)),
        compiler_params=_params("parallel", "parallel"),
    )(h, ya, yb, yc, w)


def _ple_forward(h1, gain, p, wg, wp, tm, tn):
    s_len, d = h1.shape
    kp = p.shape[1]

    def body(hrow_ref, hcol_ref, g_ref, p_ref, wg_ref, wp_ref, o_ref, n1_ref, gate_ref, pp_ref):
        @pl.when(pl.program_id(1) == 0)
        def _():
            n1_ref[...] = _rms(hrow_ref[...], g_ref[...]).astype(BF16)

        gate = jax.nn.sigmoid(jnp.dot(n1_ref[...], wg_ref[...], preferred_element_type=F32))
        pp = jnp.dot(p_ref[...].astype(BF16), wp_ref[...], preferred_element_type=F32)
        o_ref[...] = hcol_ref[...] + gate * pp
        gate_ref[...] = gate.astype(BF16)
        pp_ref[...] = pp.astype(BF16)

    col = pl.BlockSpec((tm, tn), lambda i, j: (i, j))
    return pl.pallas_call(
        body, name="ple_forward",
        out_shape=(jax.ShapeDtypeStruct((s_len, d), F32), jax.ShapeDtypeStruct((s_len, d), BF16),
                   jax.ShapeDtypeStruct((s_len, d), BF16), jax.ShapeDtypeStruct((s_len, d), BF16)),
        grid=(s_len // tm, d // tn),
        in_specs=[pl.BlockSpec((tm, d), lambda i, j: (i, 0)), col, pl.BlockSpec((1, d), lambda i, j: (0, 0)),
                  pl.BlockSpec((tm, kp), lambda i, j: (i, 0)), pl.BlockSpec((d, tn), lambda i, j: (0, j)),
                  pl.BlockSpec((kp, tn), lambda i, j: (0, j))],
        out_specs=(col, pl.BlockSpec((tm, d), lambda i, j: (i, 0)), col, col),
        compiler_params=_params("parallel", "arbitrary"),
    )(h1, h1, gain, p, wg, wp)


def _matmul_nt(a, b, tm, tk, name):
    m, n = a.shape
    k = b.shape[0]

    def body(a_ref, b_ref, o_ref):
        o_ref[...] = lax.dot_general(a_ref[...].astype(BF16), b_ref[...].astype(BF16), _NT, preferred_element_type=F32)

    return pl.pallas_call(
        body, name=name,
        out_shape=jax.ShapeDtypeStruct((m, k), F32),
        grid=(m // tm, k // tk),
        in_specs=[pl.BlockSpec((tm, n), lambda i, j: (i, 0)), pl.BlockSpec((tk, n), lambda i, j: (j, 0))],
        out_specs=pl.BlockSpec((tm, tk), lambda i, j: (i, j)),
        compiler_params=_params("parallel", "parallel"),
    )(a, b)


def _matmul_tn(a, b, tm, tk, tn, name):
    m, k = a.shape
    n = b.shape[1]

    def body(a_ref, b_ref, o_ref):
        @pl.when(pl.program_id(2) == 0)
        def _():
            o_ref[...] = jnp.zeros_like(o_ref)

        o_ref[...] += lax.dot_general(a_ref[...].astype(BF16), b_ref[...].astype(BF16), _TN, preferred_element_type=F32)

    return pl.pallas_call(
        body, name=name,
        out_shape=jax.ShapeDtypeStruct((k, n), F32),
        grid=(k // tk, n // tn, m // tm),
        in_specs=[pl.BlockSpec((tm, tk), lambda kk, nn, mm: (mm, kk)), pl.BlockSpec((tm, tn), lambda kk, nn, mm: (mm, nn))],
        out_specs=pl.BlockSpec((tk, tn), lambda kk, nn, mm: (kk, nn)),
        compiler_params=_params("parallel", "parallel", "arbitrary"),
    )(a, b)


def _acc(ref, val, first):
    @pl.when(first)
    def _():
        ref[...] = val

    @pl.when(jnp.logical_not(first))
    def _():
        ref[...] += val


def _loss_grad(h, target, tm):
    s_len, d = h.shape

    def body(h_ref, t_ref, dh_ref, loss_ref):
        e = h_ref[...] - t_ref[...]
        dh_ref[...] = e * (1.0 / d)
        part = jnp.sum(jnp.sum(e * e, axis=-1, keepdims=True), axis=0, keepdims=True) * (0.5 / d)
        _acc(loss_ref, jnp.broadcast_to(part, loss_ref.shape), pl.program_id(0) == 0)

    row = pl.BlockSpec((tm, d), lambda i: (i, 0))
    return pl.pallas_call(
        body, name="loss_grad",
        out_shape=(jax.ShapeDtypeStruct((s_len, d), F32), jax.ShapeDtypeStruct((1, 128), F32)),
        grid=(s_len // tm,),
        in_specs=[row, row],
        out_specs=(row, pl.BlockSpec((1, 128), lambda i: (0, 0))),
        compiler_params=_params("arbitrary"),
    )(h, target)


def _rms_backward(x, gain, dn, dres, tm, name):
    s_len, d = x.shape

    def body(x_ref, g_ref, dn_ref, dres_ref, dx_ref, dg_ref):
        _, vjp = jax.vjp(_rms, x_ref[...], g_ref[...])
        dx, dg = vjp(dn_ref[...])
        dx_ref[...] = dres_ref[...] + dx
        _acc(dg_ref, dg, pl.program_id(0) == 0)

    row = pl.BlockSpec((tm, d), lambda i: (i, 0))
    vec = pl.BlockSpec((1, d), lambda i: (0, 0))
    return pl.pallas_call(
        body, name=name,
        out_shape=(jax.ShapeDtypeStruct((s_len, d), F32), jax.ShapeDtypeStruct((1, d), F32)),
        grid=(s_len // tm,),
        in_specs=[row, vec, row, row],
        out_specs=(row, vec),
        compiler_params=_params("arbitrary"),
    )(x, gain, dn, dres)


def _ple_backward_gate(dh2, gate, pp, tm):
    s_len, d = dh2.shape

    def body(dh_ref, gate_ref, pp_ref, dg_ref, dpp_ref):
        dh = dh_ref[...]
        gate = gate_ref[...].astype(F32)
        dg_ref[...] = (dh * pp_ref[...].astype(F32) * gate * (1.0 - gate)).astype(BF16)
        dpp_ref[...] = (dh * gate).astype(BF16)

    row = pl.BlockSpec((tm, d), lambda i: (i, 0))
    return pl.pallas_call(
        body, name="ple_backward_gate",
        out_shape=(jax.ShapeDtypeStruct((s_len, d), BF16), jax.ShapeDtypeStruct((s_len, d), BF16)),
        grid=(s_len // tm,),
        in_specs=[row, row, row],
        out_specs=(row, row),
        compiler_params=_params("parallel"),
    )(dh2, gate, pp)


def _sgu_in_specs(tm):
    return [pl.BlockSpec((tm, W_A), lambda i: (i, O_A // W_A)),
            pl.BlockSpec((A_HEADS, HEAD), lambda i: (0, 0)), pl.BlockSpec((A_HEADS, CHUNK, CHUNK), lambda i: (0, 0, 0)),
            pl.BlockSpec((A_HEADS, CHUNK, 1), lambda i: (0, 0, 0)), pl.BlockSpec((1, 512), lambda i: (0, 0))]


def _sgu_load(a_ref, gain_ref, ws_ref, bs_ref, ga_ref, c):
    rows = slice(c * CHUNK, (c + 1) * CHUNK)
    heads = range(A_HEADS)
    u = tuple(a_ref[rows, h * HEAD:(h + 1) * HEAD].astype(F32) for h in heads)
    v = tuple(a_ref[rows, 512 + h * HEAD:512 + (h + 1) * HEAD].astype(F32) for h in heads)
    z = tuple(a_ref[rows, 1024 + h * HEAD:1024 + (h + 1) * HEAD].astype(F32) for h in heads)
    gain = tuple(gain_ref[h:h + 1, :] for h in heads)
    ws = tuple(ws_ref[h] for h in heads)
    bs = tuple(bs_ref[h] for h in heads)
    ga = tuple(ga_ref[:, h * HEAD:(h + 1) * HEAD] for h in heads)
    return u, v, z, gain, ws, bs, ga


def _sgu_forward(proj, gain, ws, bs, ga, tm):
    s_len = proj.shape[0]

    def body(a_ref, gain_ref, ws_ref, bs_ref, ga_ref, o_ref):
        for c in range(tm // CHUNK):
            out = _sgu_chunk(*_sgu_load(a_ref, gain_ref, ws_ref, bs_ref, ga_ref, c))
            for h in range(A_HEADS):
                o_ref[c * CHUNK:(c + 1) * CHUNK, h * HEAD:(h + 1) * HEAD] = out[h].astype(BF16)

    return pl.pallas_call(
        body, name="sgu_forward",
        out_shape=jax.ShapeDtypeStruct((s_len, 512), BF16),
        grid=(s_len // tm,),
        in_specs=_sgu_in_specs(tm),
        out_specs=pl.BlockSpec((tm, 512), lambda i: (i, 0)),
        compiler_params=_params("parallel"),
    )(proj, gain, ws, bs, ga)


def _sgu_backward(proj, gain, ws, bs, ga, dy, dproj, tm):
    s_len = proj.shape[0]

    def body(a_ref, gain_ref, ws_ref, bs_ref, ga_ref, dy_ref, _, da_ref, dgain_ref, dws_ref, dbs_ref, dga_ref):
        tot = None
        for c in range(tm // CHUNK):
            args = _sgu_load(a_ref, gain_ref, ws_ref, bs_ref, ga_ref, c)
            _, vjp = jax.vjp(_sgu_chunk, *args)
            rows = slice(c * CHUNK, (c + 1) * CHUNK)
            du, dv, dz, dgain, dws, dbs, dga = vjp(tuple(dy_ref[rows, h * HEAD:(h + 1) * HEAD] for h in range(A_HEADS)))
            for h in range(A_HEADS):
                da_ref[rows, h * HEAD:(h + 1) * HEAD] = du[h].astype(BF16)
                da_ref[rows, 512 + h * HEAD:512 + (h + 1) * HEAD] = dv[h].astype(BF16)
                da_ref[rows, 1024 + h * HEAD:1024 + (h + 1) * HEAD] = dz[h].astype(BF16)
            part = (dgain, dws, dbs, dga)
            tot = part if tot is None else jax.tree.map(jnp.add, tot, part)
        dgain, dws, dbs, dga = tot
        first = pl.program_id(0) == 0
        _acc(dgain_ref, jnp.concatenate(dgain, axis=0), first)
        _acc(dga_ref, jnp.concatenate(dga, axis=-1), first)
        for h in range(A_HEADS):
            _acc(dws_ref.at[h], dws[h], first)
            _acc(dbs_ref.at[h], dbs[h], first)

    small = [pl.BlockSpec((A_HEADS, HEAD), lambda i: (0, 0)), pl.BlockSpec((A_HEADS, CHUNK, CHUNK), lambda i: (0, 0, 0)),
             pl.BlockSpec((A_HEADS, CHUNK, 1), lambda i: (0, 0, 0)), pl.BlockSpec((1, 512), lambda i: (0, 0))]
    return pl.pallas_call(
        body, name="sgu_backward",
        out_shape=(jax.ShapeDtypeStruct(dproj.shape, BF16),
                   jax.ShapeDtypeStruct((A_HEADS, HEAD), F32), jax.ShapeDtypeStruct((A_HEADS, CHUNK, CHUNK), F32),
                   jax.ShapeDtypeStruct((A_HEADS, CHUNK, 1), F32), jax.ShapeDtypeStruct((1, 512), F32)),
        grid=(s_len // tm,),
        in_specs=_sgu_in_specs(tm) + [pl.BlockSpec((tm, 512), lambda i: (i, 0)), pl.BlockSpec(memory_space=pl.ANY)],
        out_specs=(pl.BlockSpec((tm, W_A), lambda i: (i, O_A // W_A)), *small),
        input_output_aliases={6: 0},
        compiler_params=_params("arbitrary"),
    )(proj, gain, ws, bs, ga, dy, dproj)


def _halo_specs(tm, width, col, n_rows):
    per = tm // HALO
    last = n_rows // HALO - 1
    return [pl.BlockSpec((HALO, width), lambda i: (jnp.maximum(i * per - 1, 0), col)),
            pl.BlockSpec((tm, width), lambda i: (i, col)),
            pl.BlockSpec((HALO, width), lambda i: (jnp.minimum((i + 1) * per, last), col))]


def _conv_masks(tm, s_len):
    r = lax.broadcasted_iota(jnp.int32, (tm + 2 * HALO, 1), 0)
    g = pl.program_id(0) * tm - HALO + r
    return (g >= 0) & (g < s_len), (r >= HALO) & (r < HALO + tm)


def _conv_inputs(b_refs, cw_ref, cb_ref, gb_ref):
    ext = jnp.concatenate([r[...] for r in b_refs], axis=0).astype(F32)
    bb, bc, bh, bz = (ext[:, j * 512:(j + 1) * 512] for j in range(4))
    prm = (cw_ref[0:1, :], cw_ref[1:2, :], cw_ref[2:3, :], cb_ref[...], gb_ref[...])
    return (bb, bc, bh, bz), prm


def _conv_forward(proj, cw, cb, gb, tm):
    s_len = proj.shape[0]

    def body(p0, p1, p2, cw_ref, cb_ref, gb_ref, o_ref):
        acts, prm = _conv_inputs((p0, p1, p2), cw_ref, cb_ref, gb_ref)
        valid, core = _conv_masks(tm, s_len)
        y = _conv_tile(*acts, *prm, *prm, valid, core)
        o_ref[...] = y[HALO:HALO + tm].astype(BF16)

    vec = pl.BlockSpec((1, 512), lambda i: (0, 0))
    return pl.pallas_call(
        body, name="conv_forward",
        out_shape=jax.ShapeDtypeStruct((s_len, 512), BF16),
        grid=(s_len // tm,),
        in_specs=_halo_specs(tm, W_B, O_B // W_B, s_len) + [pl.BlockSpec((3, 512), lambda i: (0, 0)), vec, vec],
        out_specs=pl.BlockSpec((tm, 512), lambda i: (i, 0)),
        compiler_params=_params("parallel"),
    )(proj, proj, proj, cw, cb, gb)


def _conv_backward(proj, cw, cb, gb, dy, tm):
    s_len = proj.shape[0]

    def body(p0, p1, p2, cw_ref, cb_ref, gb_ref, d0, d1, d2, db_ref, dcw_ref, dcb_ref, dgb_ref):
        acts, prm = _conv_inputs((p0, p1, p2), cw_ref, cb_ref, gb_ref)
        valid, core = _conv_masks(tm, s_len)
        _, vjp = jax.vjp(lambda a, p: _conv_tile(*a, *p, *prm, valid, core), acts, prm)
        dy_ext = jnp.where(valid, jnp.concatenate([d0[...], d1[...], d2[...]], axis=0), 0.0)
        dacts, dprm = vjp(dy_ext)
        for j in range(4):
            db_ref[:, j * 512:(j + 1) * 512] = dacts[j][HALO:HALO + tm].astype(BF16)
        first = pl.program_id(0) == 0
        _acc(dcw_ref, jnp.concatenate(dprm[0:3], axis=0), first)
        _acc(dcb_ref, dprm[3], first)
        _acc(dgb_ref, dprm[4], first)

    vec = pl.BlockSpec((1, 512), lambda i: (0, 0))
    mat = pl.BlockSpec((3, 512), lambda i: (0, 0))
    return pl.pallas_call(
        body, name="conv_backward",
        out_shape=(jax.ShapeDtypeStruct((s_len, PROJ_W), BF16), jax.ShapeDtypeStruct((3, 512), F32),
                   jax.ShapeDtypeStruct((1, 512), F32), jax.ShapeDtypeStruct((1, 512), F32)),
        grid=(s_len // tm,),
        in_specs=_halo_specs(tm, W_B, O_B // W_B, s_len) + [mat, vec, vec] + _halo_specs(tm, 512, 1, s_len),
        out_specs=(pl.BlockSpec((tm, W_B), lambda i: (i, O_B // W_B)), mat, vec, vec),
        compiler_params=_params("arbitrary"),
    )(proj, proj, proj, cw, cb, gb, dy, dy, dy)


def _cgate_forward(o, proj, gc, tm):
    s_len = o.shape[0]

    def body(o_ref, cz_ref, gc_ref, y_ref):
        y_ref[...] = _cgate_tile(o_ref[...], cz_ref[...].astype(F32), gc_ref[...]).astype(BF16)

    row = pl.BlockSpec((tm, W_CZ), lambda i: (i, 0))
    return pl.pallas_call(
        body, name="cgate_forward",
        out_shape=jax.ShapeDtypeStruct((s_len, W_CZ), BF16),
        grid=(s_len // tm,),
        in_specs=[row, pl.BlockSpec((tm, W_CZ), lambda i: (i, O_CZ // W_CZ)), pl.BlockSpec((1, W_CZ), lambda i: (0, 0))],
        out_specs=row,
        compiler_params=_params("parallel"),
    )(o, proj, gc)


def _cgate_backward(o, proj, gc, dy, dproj, tm, stat_chunk):
    s_len = o.shape[0]
    per_stat = stat_chunk // tm

    def body(o_ref, cz_ref, gc_ref, dy_ref, _, dcz_ref, do_ref, dsum_ref, dgc_ref):
        o = o_ref[...]
        _, vjp = jax.vjp(_cgate_tile, o, cz_ref[...].astype(F32), gc_ref[...])
        do, dcz, dgc = vjp(dy_ref[...])
        dcz_ref[...] = dcz.astype(BF16)
        do_ref[...] = do.astype(BF16)
        ones = jnp.ones((8, HEAD), F32)
        for h in range(C_HEADS):
            cols = slice(h * HEAD, (h + 1) * HEAD)
            sums = lax.dot_general(ones, do[:, cols] * o[:, cols], _NT, precision=lax.Precision.HIGHEST,
                                   preferred_element_type=F32)
            dsum_ref[h, 0] = sums[0:1]
        _acc(dgc_ref, dgc, pl.program_id(0) == 0)

    row = pl.BlockSpec((tm, W_CZ), lambda i: (i, 0))
    vec = pl.BlockSpec((1, W_CZ), lambda i: (0, 0))
    return pl.pallas_call(
        body, name="cgate_backward",
        out_shape=(jax.ShapeDtypeStruct(dproj.shape, BF16), jax.ShapeDtypeStruct((s_len, W_CZ), BF16),
                   jax.ShapeDtypeStruct((C_HEADS, s_len // stat_chunk, 1, stat_chunk), F32), jax.ShapeDtypeStruct((1, W_CZ), F32)),
        grid=(s_len // tm,),
        in_specs=[row, pl.BlockSpec((tm, W_CZ), lambda i: (i, O_CZ // W_CZ)), vec,
                  pl.BlockSpec((tm, W_CZ), lambda i: (i, 1)), pl.BlockSpec(memory_space=pl.ANY)],
        out_specs=(pl.BlockSpec((tm, W_CZ), lambda i: (i, O_CZ // W_CZ)), row,
                   pl.BlockSpec((C_HEADS, 1, 1, tm), lambda i: (0, i // per_stat, 0, i % per_stat)), vec),
        input_output_aliases={4: 0},
        compiler_params=_params("arbitrary"),
    )(o, proj, gc, dy, dproj)


def _mla_small_specs():
    return [pl.BlockSpec((KV_RANK, 2 * C_HEADS * HEAD), lambda i: (0, 0)), pl.BlockSpec((1, KV_RANK), lambda i: (0, 0)),
            pl.BlockSpec((1, HEAD), lambda i: (0, 0)), pl.BlockSpec((1, ROPE), lambda i: (0, 0)),
            pl.BlockSpec((1, HEAD), lambda i: (0, 0)), pl.BlockSpec((1, ROPE), lambda i: (0, 0))]


def _mla_load(m_ref, cos_ref, sin_ref):
    qn = m_ref[:, M_QN:M_QN + C_HEADS * HEAD].astype(F32)
    qr = m_ref[:, M_QR:M_QR + C_HEADS * ROPE].astype(F32)
    ckv = m_ref[:, M_CKV:M_CKV + KV_RANK].astype(F32)
    kr = m_ref[:, M_KR:M_KR + ROPE].astype(F32)
    return qn, qr, ckv, kr, cos_ref[...], sin_ref[...]


def _mla_forward(proj, cosf, sins, wukv, kvg, qng, qrg, kng, krg, tm, kt_chunk, vt_chunk):
    s_len = proj.shape[0]

    def body(m_ref, cos_ref, sin_ref, w_ref, kvg_ref, qng_ref, qrg_ref, kng_ref, krg_ref, q_ref, k_ref, v_ref, kt_ref, vt_ref):
        q, k, v = _mla_tile(*_mla_load(m_ref, cos_ref, sin_ref), w_ref[...], kvg_ref[...], qng_ref[...], qrg_ref[...],
                            kng_ref[...], krg_ref[...])
        for h in range(C_HEADS):
            q_ref[h] = q[h].astype(BF16)
            k_ref[h] = k[h].astype(BF16)
            v_ref[h] = v[h].astype(BF16)
            kt_ref[h, 0] = jnp.concatenate([k[h][:, :HEAD].T, k[h][:, HEAD:].T], axis=0).astype(BF16)
            vt_ref[h, 0] = v[h].T.astype(BF16)

    rope_spec = pl.BlockSpec((tm, ROPE), lambda i: (i, 0))
    qk_spec = pl.BlockSpec((C_HEADS, tm, QK), lambda i: (0, i, 0))
    per_k, per_v = kt_chunk // tm, vt_chunk // tm
    return pl.pallas_call(
        body, name="mla_forward",
        out_shape=(jax.ShapeDtypeStruct((C_HEADS, s_len, QK), BF16), jax.ShapeDtypeStruct((C_HEADS, s_len, QK), BF16),
                   jax.ShapeDtypeStruct((C_HEADS, s_len, HEAD), BF16),
                   jax.ShapeDtypeStruct((C_HEADS, s_len // kt_chunk, QK, kt_chunk), BF16),
                   jax.ShapeDtypeStruct((C_HEADS, s_len // vt_chunk, HEAD, vt_chunk), BF16)),
        grid=(s_len // tm,),
        in_specs=[pl.BlockSpec((tm, W_M), lambda i: (i, O_M // W_M)), rope_spec, rope_spec] + _mla_small_specs(),
        out_specs=(qk_spec, qk_spec, pl.BlockSpec((C_HEADS, tm, HEAD), lambda i: (0, i, 0)),
                   pl.BlockSpec((C_HEADS, 1, QK, tm), lambda i: (0, i // per_k, 0, i % per_k)),
                   pl.BlockSpec((C_HEADS, 1, HEAD, tm), lambda i: (0, i // per_v, 0, i % per_v))),
        compiler_params=_params("parallel"),
    )(proj, cosf, sins, wukv, kvg, qng, qrg, kng, krg)


def _mla_backward(proj, cosf, sins, wukv, kvg, qng, qrg, kng, krg, dqt, dk, dv, dproj, tm):
    s_len = proj.shape[0]
    per_chunk = dqt.shape[3] // tm

    def body(m_ref, cos_ref, sin_ref, w_ref, kvg_ref, qng_ref, qrg_ref, kng_ref, krg_ref, dq_ref, dk_ref, dv_ref, _,
             dm_ref, dw_ref, dkvg_ref, dqng_ref, dqrg_ref, dkng_ref, dkrg_ref):
        qn, qr, ckv, kr, cosf_t, sins_t = _mla_load(m_ref, cos_ref, sin_ref)
        prm = (w_ref[...], kvg_ref[...], qng_ref[...], qrg_ref[...], kng_ref[...], krg_ref[...])
        _, vjp = jax.vjp(lambda a, p: _mla_tile(*a, cosf_t, sins_t, *p), (qn, qr, ckv, kr), prm)
        heads = range(C_HEADS)
        dacts, dprm = vjp((tuple(dq_ref[h, 0].T for h in heads), tuple(dk_ref[h] for h in heads), tuple(dv_ref[h] for h in heads)))
        dm_ref[:, M_QN:M_QN + C_HEADS * HEAD] = dacts[0].astype(BF16)
        dm_ref[:, M_QR:M_QR + C_HEADS * ROPE] = dacts[1].astype(BF16)
        dm_ref[:, M_CKV:M_CKV + KV_RANK] = dacts[2].astype(BF16)
        pad = jnp.zeros((tm, W_M - M_KR - ROPE), F32)
        dm_ref[:, M_KR:W_M] = jnp.concatenate([dacts[3], pad], axis=-1).astype(BF16)
        first = pl.program_id(0) == 0
        for ref, val in zip((dw_ref, dkvg_ref, dqng_ref, dqrg_ref, dkng_ref, dkrg_ref), dprm):
            _acc(ref, val.astype(F32), first)

    rope_spec = pl.BlockSpec((tm, ROPE), lambda i: (i, 0))
    qk_spec = pl.BlockSpec((C_HEADS, tm, QK), lambda i: (0, i, 0))
    small = _mla_small_specs()
    return pl.pallas_call(
        body, name="mla_backward",
        out_shape=(jax.ShapeDtypeStruct(dproj.shape, BF16), jax.ShapeDtypeStruct((KV_RANK, 2 * C_HEADS * HEAD), F32),
                   jax.ShapeDtypeStruct((1, KV_RANK), F32), jax.ShapeDtypeStruct((1, HEAD), F32),
                   jax.ShapeDtypeStruct((1, ROPE), F32), jax.ShapeDtypeStruct((1, HEAD), F32),
                   jax.ShapeDtypeStruct((1, ROPE), F32)),
        grid=(s_len // tm,),
        in_specs=[pl.BlockSpec((tm, W_M), lambda i: (i, O_M // W_M)), rope_spec, rope_spec] + small
                 + [pl.BlockSpec((C_HEADS, 1, QK, tm), lambda i: (0, i // per_chunk, 0, i % per_chunk)), qk_spec,
                    pl.BlockSpec((C_HEADS, tm, HEAD), lambda i: (0, i, 0)), pl.BlockSpec(memory_space=pl.ANY)],
        out_specs=(pl.BlockSpec((tm, W_M), lambda i: (i, O_M // W_M)), *small),
        input_output_aliases={12: 0},
        compiler_params=_params("arbitrary"),
    )(proj, cosf, sins, wukv, kvg, qng, qrg, kng, krg, dqt, dk, dv, dproj)


def _attention_forward(q, k, vt, tq, stat_chunk):
    n_heads, s_len, _ = q.shape
    n_chunks, _, ck = vt.shape[1:]

    def body(q_ref, k_ref, vt_ref, o_ref, lse_ref):
        q_t = q_ref[0]

        def step(j, carry):
            m_old, l_old, acc = carry
            k_j = k_ref[0, pl.ds(pl.multiple_of(j * ck, ck), ck), :]
            s = lax.dot_general(k_j, q_t, _NT, preferred_element_type=F32)
            m_new = jnp.maximum(m_old, jnp.max(s, axis=0, keepdims=True))
            p = jnp.exp2(s - m_new)
            alpha = jnp.exp2(m_old - m_new)
            l_new = alpha * l_old + jnp.sum(p, axis=0, keepdims=True)
            acc = alpha * acc + jnp.dot(vt_ref[0, j], p.astype(BF16), preferred_element_type=F32)
            return m_new, l_new, acc

        init = (jnp.full((1, tq), -jnp.inf, F32), jnp.zeros((1, tq), F32), jnp.zeros((HEAD, tq), F32))
        m_fin, l_fin, acc = lax.fori_loop(0, n_chunks, step, init)
        o_ref[...] = (acc / l_fin).T
        lse_ref[0, 0] = m_fin + jnp.log2(l_fin)

    per_stat = stat_chunk // tq
    return pl.pallas_call(
        body, name="attention_forward",
        out_shape=(jax.ShapeDtypeStruct((s_len, n_heads * HEAD), F32),
                   jax.ShapeDtypeStruct((n_heads, s_len // stat_chunk, 1, stat_chunk), F32)),
        grid=(n_heads, s_len // tq),
        in_specs=[pl.BlockSpec((1, tq, QK), lambda h, i: (h, i, 0)), pl.BlockSpec((1, s_len, QK), lambda h, i: (h, 0, 0)),
                  pl.BlockSpec((1, n_chunks, HEAD, ck), lambda h, i: (h, 0, 0, 0))],
        out_specs=(pl.BlockSpec((tq, HEAD), lambda h, i: (i, h)),
                   pl.BlockSpec((1, 1, 1, tq), lambda h, i: (h, i // per_stat, 0, i % per_stat))),
        compiler_params=_params("parallel", "parallel"),
    )(q, k, vt)


def _attention_backward(q, k, kt, v, do, lse, dsum):
    n_heads, s_len, _ = q.shape
    tk = kt.shape[3]
    n_q, _, cq = lse.shape[1:]

    def body(q_ref, k_ref, kt_ref, v_ref, do_ref, lse_ref, dsum_ref, dqt_ref, dk_ref, dv_ref):
        first = pl.program_id(1) == 0
        k_j, kt_j, v_j = k_ref[0], kt_ref[0, 0], v_ref[0]

        def step(i, carry):
            dk, dv = carry
            rows = pl.ds(pl.multiple_of(i * cq, cq), cq)
            q_i, do_i = q_ref[0, rows, :], do_ref[rows, :]
            s = lax.dot_general(k_j, q_i, _NT, preferred_element_type=F32)
            p = jnp.exp2(s - lse_ref[0, i])
            dp = lax.dot_general(v_j, do_i, _NT, preferred_element_type=F32)
            ds = (p * (dp - dsum_ref[0, i]) * LN_2).astype(BF16)
            dv = dv + jnp.dot(p.astype(BF16), do_i, preferred_element_type=F32)
            dk = dk + jnp.dot(ds, q_i, preferred_element_type=F32)
            _acc(dqt_ref.at[0, i], jnp.dot(kt_j, ds, preferred_element_type=F32), first)
            return dk, dv

        dk, dv = lax.fori_loop(0, n_q, step, (jnp.zeros((tk, QK), F32), jnp.zeros((tk, HEAD), F32)))
        dk_ref[0] = dk
        dv_ref[0] = dv

    stat = pl.BlockSpec((1, n_q, 1, cq), lambda h, j: (h, 0, 0, 0))
    return pl.pallas_call(
        body, name="attention_backward",
        out_shape=(jax.ShapeDtypeStruct((n_heads, n_q, QK, cq), F32), jax.ShapeDtypeStruct((n_heads, s_len, QK), F32),
                   jax.ShapeDtypeStruct((n_heads, s_len, HEAD), F32)),
        grid=(n_heads, s_len // tk),
        in_specs=[pl.BlockSpec((1, s_len, QK), lambda h, j: (h, 0, 0)), pl.BlockSpec((1, tk, QK), lambda h, j: (h, j, 0)),
                  pl.BlockSpec((1, 1, QK, tk), lambda h, j: (h, j, 0, 0)),
                  pl.BlockSpec((1, tk, HEAD), lambda h, j: (h, j, 0)), pl.BlockSpec((s_len, HEAD), lambda h, j: (0, h)),
                  stat, stat],
        out_specs=(pl.BlockSpec((1, n_q, QK, cq), lambda h, j: (h, 0, 0, 0)), pl.BlockSpec((1, tk, QK), lambda h, j: (h, j, 0)),
                   pl.BlockSpec((1, tk, HEAD), lambda h, j: (h, j, 0))),
        compiler_params=_params("parallel", "arbitrary"),
    )(q, k, kt, v, do, lse, dsum)


def _exchange(arrs, gather, name):
    n = len(arrs)
    n_peers = N_DEV - 1

    def body(*refs):
        ins, outs = refs[:n], refs[n:2 * n]
        send_sems, recv_sems, local_sems = refs[2 * n:]
        x, y, c = lax.axis_index("x"), lax.axis_index("y"), lax.axis_index("c")
        me = 4 * x + 2 * y + c

        def block_for(a, dev):
            return ins[a] if gather[a] else ins[a].at[dev]

        local = []
        for a in range(n):
            cp = pltpu.make_async_copy(block_for(a, me), outs[a].at[me], local_sems.at[a])
            cp.start()
            local.append(cp)
        remote = []
        for k in range(1, N_DEV):
            px = 1 - x if k & 4 else x
            py = 1 - y if k & 2 else y
            pc = 1 - c if k & 1 else c
            peer = 4 * px + 2 * py + pc
            for a in range(n):
                idx = a * n_peers + k - 1
                send = pltpu.make_async_remote_copy(
                    src_ref=block_for(a, peer), dst_ref=outs[a].at[me], send_sem=send_sems.at[idx], recv_sem=recv_sems.at[idx],
                    device_id=(px, py, pc), device_id_type=pl.DeviceIdType.MESH)
                send.start()
                arrive = pltpu.make_async_remote_copy(
                    src_ref=block_for(a, peer), dst_ref=outs[a].at[peer], send_sem=send_sems.at[idx], recv_sem=recv_sems.at[idx],
                    device_id=(px, py, pc), device_id_type=pl.DeviceIdType.MESH)
                remote.append((send, arrive))
        for send, arrive in remote:
            send.wait_send()
            arrive.wait_recv()
        for cp in local:
            cp.wait()

    def out_shape(a):
        blk = arrs[a].shape if gather[a] else arrs[a].shape[1:]
        return jax.ShapeDtypeStruct((N_DEV, *blk), arrs[a].dtype)

    any_spec = pl.BlockSpec(memory_space=pl.ANY)
    return pl.pallas_call(
        body, name=name,
        out_shape=tuple(out_shape(a) for a in range(n)),
        in_specs=[any_spec] * n,
        out_specs=tuple([any_spec] * n),
        scratch_shapes=[pltpu.SemaphoreType.DMA((n * n_peers,)), pltpu.SemaphoreType.DMA((n * n_peers,)),
                        pltpu.SemaphoreType.DMA((n,))],
        compiler_params=pltpu.CompilerParams(has_side_effects=True),
    )(*arrs)


ADAM_TILE_ELEMS = 256 * 1024


def _sum_adam(parts, w, m, v, name):
    n_parts, r, c = parts.shape
    tm = r
    while tm * c > ADAM_TILE_ELEMS and tm % 16 == 0:
        tm //= 2

    def body(p_ref, w_ref, m_ref, v_ref, g_ref, d_ref, nm_ref, nv_ref):
        g = p_ref[0].astype(F32)
        for s in range(1, n_parts):
            g = g + p_ref[s].astype(F32)
        m_new = ADAM_B1 * m_ref[...] + (1.0 - ADAM_B1) * g
        v_new = ADAM_B2 * v_ref[...] + (1.0 - ADAM_B2) * (g * g)
        m_hat = m_new / (1.0 - ADAM_B1 ** ADAM_STEP)
        v_hat = v_new / (1.0 - ADAM_B2 ** ADAM_STEP)
        g_ref[...] = g
        d_ref[...] = -ADAM_LR * (m_hat / (jnp.sqrt(v_hat) + ADAM_EPS) + ADAM_WD * w_ref[...])
        nm_ref[...] = m_new
        nv_ref[...] = v_new

    row = pl.BlockSpec((tm, c), lambda i: (i, 0))
    return pl.pallas_call(
        body, name=name,
        out_shape=(jax.ShapeDtypeStruct((r, c), F32),) * 4,
        grid=(r // tm,),
        in_specs=[pl.BlockSpec((n_parts, tm, c), lambda i: (0, i, 0)), row, row, row],
        out_specs=(row, row, row, row),
        compiler_params=_params("parallel"),
    )(parts, w, m, v)


def _permute_in(w):
    k = w.shape[0]
    q = w[:, 3584:5120].reshape(k, C_HEADS, QK)
    return jnp.concatenate(
        [w[:, 1536:3584], w[:, 5696:6720], w[:, 0:1536], q[:, :, :HEAD].reshape(k, C_HEADS * HEAD),
         q[:, :, HEAD:].reshape(k, C_HEADS * ROPE), w[:, 5120:5632], w[:, 5632:5696],
         jnp.zeros((k, PROJ_W - IN_WIDTH), w.dtype)], axis=1)


def _unpermute_in(g):
    k = g.shape[0]
    qn = g[:, O_M + M_QN:O_M + M_QR].reshape(k, C_HEADS, HEAD)
    qr = g[:, O_M + M_QR:O_M + M_CKV].reshape(k, C_HEADS, ROPE)
    q = jnp.concatenate([qn, qr], axis=-1).reshape(k, C_HEADS * QK)
    return jnp.concatenate(
        [g[:, O_A:O_A + W_A], g[:, O_B:O_B + W_B], q, g[:, O_M + M_CKV:O_M + M_KR],
         g[:, O_M + M_KR:O_M + M_KR + ROPE], g[:, O_CZ:O_CZ + W_CZ]], axis=1)


SMALL = ("attn_norm", "sgu_norm", "w_spatial", "b_spatial", "conv_b", "kv_norm", "q_nope_norm", "q_rope_norm",
         "k_nope_norm", "k_rope_norm", "out_norm", "ple_norm")
PACK_ROWS = 256


def _pack(tensors):
    flat = jnp.concatenate([t.reshape(-1) for t in tensors])
    rows = -(-flat.shape[0] // (128 * PACK_ROWS)) * PACK_ROWS
    return jnp.pad(flat, (0, rows * 128 - flat.shape[0])).reshape(rows, 128)


def _unpack(packed, like):
    flat = packed.reshape(-1)
    out, pos = [], 0
    for t in like:
        out.append(flat[pos:pos + t.size].reshape(t.shape))
        pos += t.size
    return out


def _tile(s_len, want):
    return min(want, s_len)


ATT_FWD_QUERIES = 512
ATT_FWD_KEYS = 8192
ATT_BWD_KEYS = 512
ATT_BWD_QUERIES = 2048


def _layer_forward(h, p_l, cosf, sins, w, sm):
    s_len = h.shape[0]
    tm = _tile(s_len, 512)
    proj, hn = _norm_matmul(h, sm["attn_norm"], w["w_in"], tm, 768)
    ga, gb, gc = sm["out_norm"][:, 0:512], sm["out_norm"][:, 512:1024], sm["out_norm"][:, 1024:2048]
    ya = _sgu_forward(proj, sm["sgu_norm"], sm["w_spatial"], sm["b_spatial"], ga, _tile(s_len, 256))
    yb = _conv_forward(proj, w["conv_w"], sm["conv_b"], gb, _tile(s_len, 256))
    q, k, v, kt, vt = _mla_forward(proj, cosf, sins, w["w_ukv"], sm["kv_norm"], sm["q_nope_norm"], sm["q_rope_norm"],
                                   sm["k_nope_norm"], sm["k_rope_norm"], _tile(s_len, 256), _tile(s_len, ATT_BWD_KEYS),
                                   _tile(s_len, ATT_FWD_KEYS))
    o, lse = _attention_forward(q, k, vt, _tile(s_len, ATT_FWD_QUERIES), _tile(s_len, ATT_BWD_QUERIES))
    yc = _cgate_forward(o, proj, gc, _tile(s_len, 256))
    h1 = _out_matmul(h, ya, yb, yc, w["w_out"], tm, 512)
    h2, n1, gate, pp = _ple_forward(h1, sm["ple_norm"], p_l, w["w_ple_gate"], w["w_ple_proj"], tm, 512)
    saved = dict(h=h, hn=hn, proj=proj, ya=ya, yb=yb, yc=yc, q=q, k=k, v=v, kt=kt, o=o, lse=lse, h1=h1, n1=n1, gate=gate,
                 pp=pp)
    return h2, saved


def _layer_backward(dh2, p_l, cosf, sins, w, sm, sv):
    s_len = dh2.shape[0]
    tm = _tile(s_len, 512)
    tr = _tile(s_len, 256)
    big, small = {}, {}
    dgp, dpp = _ple_backward_gate(dh2, sv["gate"], sv["pp"], tm)
    big["w_ple_proj"] = _matmul_tn(p_l, dpp, _tile(s_len, 1024), PLE_DIM, 1024, "grad_w_ple_proj")
    big["w_ple_gate"] = _matmul_tn(sv["n1"], dgp, _tile(s_len, 1024), 512, 1024, "grad_w_ple_gate")
    dn1 = _matmul_nt(dgp, w["w_ple_gate"], tm, 1024, "grad_ple_norm_in")
    dh1, small["ple_norm"] = _rms_backward(sv["h1"], sm["ple_norm"], dn1, dh2, tr, "ple_norm_backward")
    dy = _matmul_nt(dh1, w["w_out"], tm, 1024, "grad_branches")
    big["w_out"] = jnp.concatenate(
        [_matmul_tn(sv["ya"], dh1, _tile(s_len, 1024), 512, 1024, "grad_w_out_a"),
         _matmul_tn(sv["yb"], dh1, _tile(s_len, 1024), 512, 1024, "grad_w_out_b"),
         _matmul_tn(sv["yc"], dh1, _tile(s_len, 1024), 512, 1024, "grad_w_out_c")], axis=0)
    ga, gb, gc = sm["out_norm"][:, 0:512], sm["out_norm"][:, 512:1024], sm["out_norm"][:, 1024:2048]
    dproj, dcw, small["conv_b"], dgb = _conv_backward(sv["proj"], w["conv_w"], sm["conv_b"], gb, dy, tr)
    big["conv_w"] = dcw
    dproj, do, dsum, dgc = _cgate_backward(sv["o"], sv["proj"], gc, dy, dproj, tr, _tile(s_len, ATT_BWD_QUERIES))
    dqt, dk, dv = _attention_backward(sv["q"], sv["k"], sv["kt"], sv["v"], do, sv["lse"], dsum)
    (dproj, big["w_ukv"], small["kv_norm"], small["q_nope_norm"], small["q_rope_norm"], small["k_nope_norm"],
     small["k_rope_norm"]) = _mla_backward(sv["proj"], cosf, sins, w["w_ukv"], sm["kv_norm"], sm["q_nope_norm"],
                                            sm["q_rope_norm"], sm["k_nope_norm"], sm["k_rope_norm"], dqt, dk, dv, dproj,
                                            _tile(s_len, 128))
    dproj, small["sgu_norm"], small["w_spatial"], small["b_spatial"], dga = _sgu_backward(
        sv["proj"], sm["sgu_norm"], sm["w_spatial"], sm["b_spatial"], ga, dy, dproj, tr)
    small["out_norm"] = jnp.concatenate([dga, dgb, dgc], axis=1)
    big["w_in"] = _matmul_tn(sv["hn"], dproj, _tile(s_len, 1024), 512, 2304, "grad_w_in")
    dhn = _matmul_nt(dproj, w["w_in"], tm, 512, "grad_attn_norm_in")
    dh, small["attn_norm"] = _rms_backward(sv["h"], sm["attn_norm"], dhn, dh1, tr, "attn_norm_backward")
    return dh, big, small


def _layer_small(params, layer):
    return dict(
        attn_norm=params["attn_norm"][layer][None, :], sgu_norm=params["sgu_norm"][layer],
        w_spatial=params["w_spatial"][layer], b_spatial=params["b_spatial"][layer][:, :, None],
        conv_b=params["conv_b"][layer][None, :], kv_norm=params["kv_norm"][layer][None, :],
        q_nope_norm=params["q_nope_norm"][layer][None, :], q_rope_norm=params["q_rope_norm"][layer][None, :],
        k_nope_norm=params["k_nope_norm"][layer][None, :], k_rope_norm=params["k_rope_norm"][layer][None, :],
        out_norm=params["out_norm"][layer][None, :], ple_norm=params["ple_norm"][layer][None, :])


def _step_local(xs, ps, pos, target, weights, params):
    inv = 1.0 / (ROPE_BASE ** (jnp.arange(0, ROPE, 2, dtype=F32) / ROPE))
    ang = pos.astype(F32)[:, None] * inv
    cos, sin = jnp.cos(ang), jnp.sin(ang)
    cosf = jnp.concatenate([cos, cos], axis=-1)
    sins = jnp.concatenate([-sin, sin], axis=-1)
    h = xs
    saved = []
    smalls = [_layer_small(params, layer) for layer in range(DEPTH)]
    for layer in range(DEPTH):
        h, sv = _layer_forward(h, ps[layer], cosf, sins, weights[layer], smalls[layer])
        saved.append(sv)
    dh, loss = _loss_grad(h, target, _tile(h.shape[0], 512))
    bigs, small_grads = [None] * DEPTH, [None] * DEPTH
    for layer in reversed(range(DEPTH)):
        dh, bigs[layer], small_grads[layer] = _layer_backward(dh, ps[layer], cosf, sins, weights[layer], smalls[layer],
                                                              saved[layer])
    return loss, dh, bigs, small_grads


BIG = ("w_in", "conv_w", "w_ukv", "w_out", "w_ple_gate", "w_ple_proj")


def _gather_weights(shards):
    send = [shards["w_in"].astype(BF16), shards["conv_w"].reshape(-1, 128), shards["w_ukv"].astype(BF16),
            shards["w_out"].astype(BF16), shards["w_ple_gate"].astype(BF16), shards["w_ple_proj"].astype(BF16)]
    g_in, g_cw, g_ukv, g_out, g_gate, g_proj = _exchange(send, [True] * len(send), "gather_weights")
    k_in = g_in.shape[2]
    conv_w = g_cw.reshape(N_DEV, DEPTH, 3, -1).transpose(1, 2, 0, 3).reshape(DEPTH, 3, -1)
    weights = []
    for layer in range(DEPTH):
        w_in = g_in[:, layer].transpose(1, 0, 2).reshape(k_in, IN_WIDTH)
        w_ukv = g_ukv[:, layer].reshape(N_DEV, KV_RANK, 2, HEAD).transpose(1, 2, 0, 3).reshape(KV_RANK, 2 * C_HEADS * HEAD)
        weights.append(dict(
            w_in=_permute_in(w_in), conv_w=conv_w[layer], w_ukv=w_ukv,
            w_out=g_out[:, layer].reshape(D_MODEL, D_MODEL), w_ple_gate=g_gate[:, layer].reshape(D_MODEL, D_MODEL),
            w_ple_proj=g_proj[:, layer].transpose(1, 0, 2).reshape(PLE_DIM, D_MODEL)))
    return weights


def _scatter_parts(bigs):
    def stack(name, fn):
        return jnp.stack([fn(b[name]) for b in bigs], axis=1)

    k_in = bigs[0]["w_in"].shape[0]
    parts = dict(
        w_in=stack("w_in", lambda g: _unpermute_in(g).reshape(k_in, N_DEV, -1).transpose(1, 0, 2)),
        conv_w=stack("conv_w", lambda g: g.reshape(3, N_DEV, -1).transpose(1, 0, 2)),
        w_ukv=stack("w_ukv", lambda g: g.reshape(KV_RANK, 2, N_DEV, HEAD).transpose(2, 0, 1, 3).reshape(N_DEV, KV_RANK, 2 * HEAD)),
        w_out=stack("w_out", lambda g: g.reshape(N_DEV, -1, D_MODEL)),
        w_ple_gate=stack("w_ple_gate", lambda g: g.reshape(N_DEV, -1, D_MODEL)),
        w_ple_proj=stack("w_ple_proj", lambda g: g.reshape(PLE_DIM, N_DEV, -1).transpose(1, 0, 2)))
    out = {}
    for name, t in parts.items():
        cols = 128 if name == "conv_w" else t.shape[-1]
        out[name] = t.astype(BF16).reshape(N_DEV, -1, cols)
    return out


def kernel(x, p, positions, attn_norm, w_in, sgu_norm, w_spatial, b_spatial, conv_w, conv_b, kv_norm, w_ukv, q_nope_norm, q_rope_norm, k_nope_norm, k_rope_norm, out_norm, w_out, ple_norm, w_ple_gate, w_ple_proj, loss_target, m_attn_norm, m_w_in, m_sgu_norm, m_w_spatial, m_b_spatial, m_conv_w, m_conv_b, m_kv_norm, m_w_ukv, m_q_nope_norm, m_q_rope_norm, m_k_nope_norm, m_k_rope_norm, m_out_norm, m_w_out, m_ple_norm, m_w_ple_gate, m_w_ple_proj, v_attn_norm, v_w_in, v_sgu_norm, v_w_spatial, v_b_spatial, v_conv_w, v_conv_b, v_kv_norm, v_w_ukv, v_q_nope_norm, v_q_rope_norm, v_k_nope_norm, v_k_rope_norm, v_out_norm, v_w_out, v_ple_norm, v_w_ple_gate, v_w_ple_proj):
    order = ("attn_norm", "w_in", "sgu_norm", "w_spatial", "b_spatial", "conv_w", "conv_b", "kv_norm", "w_ukv",
             "q_nope_norm", "q_rope_norm", "k_nope_norm", "k_rope_norm", "out_norm", "w_out", "ple_norm", "w_ple_gate",
             "w_ple_proj")
    wts = dict(zip(order, (attn_norm, w_in, sgu_norm, w_spatial, b_spatial, conv_w, conv_b, kv_norm, w_ukv, q_nope_norm,
                           q_rope_norm, k_nope_norm, k_rope_norm, out_norm, w_out, ple_norm, w_ple_gate, w_ple_proj)))
    mom = dict(zip(order, (m_attn_norm, m_w_in, m_sgu_norm, m_w_spatial, m_b_spatial, m_conv_w, m_conv_b, m_kv_norm, m_w_ukv,
                           m_q_nope_norm, m_q_rope_norm, m_k_nope_norm, m_k_rope_norm, m_out_norm, m_w_out, m_ple_norm,
                           m_w_ple_gate, m_w_ple_proj)))
    var = dict(zip(order, (v_attn_norm, v_w_in, v_sgu_norm, v_w_spatial, v_b_spatial, v_conv_w, v_conv_b, v_kv_norm, v_w_ukv,
                           v_q_nope_norm, v_q_rope_norm, v_k_nope_norm, v_k_rope_norm, v_out_norm, v_w_out, v_ple_norm,
                           v_w_ple_gate, v_w_ple_proj)))

    weights = _gather_weights({n: wts[n] for n in BIG})
    loss_part, grad_x, bigs, small_grads = _step_local(x[0], p[:, 0], positions[0], loss_target[0], weights, wts)
    loss = lax.psum(loss_part[0, 0], ("x", "y", "c"))

    def small_grad(name):
        g = jnp.stack([sg[name] for sg in small_grads])
        return g.reshape(wts[name].shape)

    packed = _pack([small_grad(n) for n in SMALL])
    (small_parts,) = _exchange([packed], [True], "gather_small_grads")
    small_out = _sum_adam(small_parts, _pack([wts[n] for n in SMALL]), _pack([mom[n] for n in SMALL]),
                          _pack([var[n] for n in SMALL]), "adam_small")
    like = [wts[n] for n in SMALL]
    results = {n: vals for n, vals in zip(SMALL, zip(*[_unpack(o, like) for o in small_out]))}

    parts = _scatter_parts(bigs)
    received = _exchange([parts[n] for n in BIG], [False] * len(BIG), "scatter_grads")
    for name, part in zip(BIG, received):
        shape = wts[name].shape
        flat = (lambda t, part=part: t.reshape(part.shape[1:]))
        outs = _sum_adam(part, flat(wts[name]), flat(mom[name]), flat(var[name]), "adam_" + name)
        results[name] = tuple(o.reshape(shape) for o in outs)

    grads, deltas, new_m, new_v = ([results[n][j] for n in order] for j in range(4))
    return (loss, grad_x[None], *grads, *deltas, *new_m, *new_v)
```

```python
import functools

import jax
import jax.numpy as jnp
from jax import lax
from jax.experimental import pallas as pl
from jax.experimental.pallas import tpu as pltpu

F32 = jnp.float32
BF16 = jnp.bfloat16

N_DEV = 8
DEPTH = 4
D_MODEL = 2048
EPS = 1e-6
CHUNK = 128
A_HEADS = 4
HEAD = 128
ROPE = 64
HALF = ROPE // 2
C_HEADS = 8
KV_RANK = 512
PLE_DIM = 256
ROPE_BASE = 10000.0
IN_WIDTH = 6720
QK = HEAD + ROPE
SCALE = QK ** -0.5
LOG2_E = 1.4426950408889634
LN_2 = 0.6931471805599453
Q_SCALE = SCALE * LOG2_E
HALO = 8

O_B = 0
W_B = 2048
O_CZ = 2048
W_CZ = 1024
O_A = 3072
W_A = 1536
O_M = 4608
W_M = 2304
M_QN, M_QR, M_CKV, M_KR = 0, 1024, 1536, 2048
PROJ_W = 6912

ADAM_LR = 0.001
ADAM_B1 = 0.9
ADAM_B2 = 0.999
ADAM_EPS = 1e-08
ADAM_WD = 0.01
ADAM_STEP = 10

VMEM_LIMIT = 56 * 1024 * 1024

_NT = (((1,), (1,)), ((), ()))
_TN = (((0,), (0,)), ((), ()))


def _params(*sem):
    return pltpu.CompilerParams(dimension_semantics=sem, vmem_limit_bytes=VMEM_LIMIT)


@jax.custom_vjp
def _bdot(a, b):
    return jnp.dot(a.astype(BF16), b.astype(BF16), preferred_element_type=F32)


def _bdot_fwd(a, b):
    return _bdot(a, b), (a, b)


def _bdot_bwd(res, g):
    a, b = res
    gb = g.astype(BF16)
    da = lax.dot_general(gb, b.astype(BF16), _NT, preferred_element_type=F32)
    db = lax.dot_general(a.astype(BF16), gb, _TN, preferred_element_type=F32)
    return da.astype(a.dtype), db.astype(b.dtype)


_bdot.defvjp(_bdot_fwd, _bdot_bwd)


@functools.partial(jax.custom_vjp, nondiff_argnums=(1,))
def _split(x, n):
    w = x.shape[-1] // n
    return tuple(x[:, i * w:(i + 1) * w] for i in range(n))


def _split_fwd(x, n):
    return _split(x, n), None


def _split_bwd(n, _, gs):
    return (jnp.concatenate(gs, axis=-1),)


_split.defvjp(_split_fwd, _split_bwd)


@functools.partial(jax.custom_vjp, nondiff_argnums=(1,))
def _shift_rows(x, k):
    return pltpu.roll(x, k % x.shape[0], 0)


def _shift_rows_fwd(x, k):
    return _shift_rows(x, k), None


def _shift_rows_bwd(k, _, g):
    return (_shift_rows(g, -k),)


_shift_rows.defvjp(_shift_rows_fwd, _shift_rows_bwd)


@jax.custom_vjp
def _swap_halves(x):
    h = x.shape[-1] // 2
    return jnp.concatenate([x[:, h:], x[:, :h]], axis=-1)


def _swap_halves_fwd(x):
    return _swap_halves(x), None


def _swap_halves_bwd(_, g):
    return (_swap_halves(g),)


_swap_halves.defvjp(_swap_halves_fwd, _swap_halves_bwd)


def _rms(x, g):
    return x * lax.rsqrt(jnp.mean(x * x, axis=-1, keepdims=True) + EPS) * g


def _rope(x, cosf, sins):
    return x * cosf + _swap_halves(x) * sins


def _sgu_chunk(u, v, z, gain, ws, bs, ga):
    ys = []
    for h in range(A_HEADS):
        vn = _rms(v[h], gain[h])
        s = _bdot(ws[h], vn) + bs[h]
        ys.append(u[h] * s * jax.nn.silu(z[h]))
    ss = sum(jnp.sum(y * y, axis=-1, keepdims=True) for y in ys) * (1.0 / (A_HEADS * HEAD))
    r = lax.rsqrt(ss + EPS)
    return tuple(ys[h] * r * ga[h] for h in range(A_HEADS))


def _conv_tile(bb, bc, bh, bz, w0, w1, w2, cb, gb, w0h, w1h, w2h, cbh, gbh, valid, core):
    t = jnp.where(valid, bc * bh, 0.0)
    y = (jnp.where(core, cb, cbh)
         + _shift_rows(t, 1) * jnp.where(core, w0, w0h)
         + t * jnp.where(core, w1, w1h)
         + _shift_rows(t, -1) * jnp.where(core, w2, w2h))
    return _rms(bb * y * jax.nn.silu(bz), jnp.where(core, gb, gbh))


def _cgate_tile(o, cz, gc):
    return _rms(o * jax.nn.silu(cz), gc)


def _mla_tile(qn, qr, ckv, kr, cosf, sins, wukv, kvg, qng, qrg, kng, krg):
    kv = _split(_bdot(_rms(ckv, kvg), wukv), 2 * C_HEADS)
    k_r = _rope(_rms(kr, krg), cosf, sins)
    qn_h = _split(qn, C_HEADS)
    qr_h = _split(qr, C_HEADS)
    q, k, v = [], [], []
    for h in range(C_HEADS):
        q.append(jnp.concatenate([_rms(qn_h[h], qng), _rope(_rms(qr_h[h], qrg), cosf, sins)], axis=-1) * Q_SCALE)
        k.append(jnp.concatenate([_rms(kv[h], kng), k_r], axis=-1))
        v.append(kv[C_HEADS + h])
    return tuple(q), tuple(k), tuple(v)


def _norm_matmul(h, gain, w, tm, tn):
    s_len, k = h.shape
    n = w.shape[1]

    def body(h_ref, g_ref, w_ref, o_ref, hn_ref):
        @pl.when(pl.program_id(1) == 0)
        def _():
            hn_ref[...] = _rms(h_ref[...], g_ref[...]).astype(BF16)

        o_ref[...] = jnp.dot(hn_ref[...], w_ref[...], preferred_element_type=F32).astype(BF16)

    return pl.pallas_call(
        body, name="norm_matmul",
        out_shape=(jax.ShapeDtypeStruct((s_len, n), BF16), jax.ShapeDtypeStruct((s_len, k), BF16)),
        grid=(s_len // tm, n // tn),
        in_specs=[pl.BlockSpec((tm, k), lambda i, j: (i, 0)), pl.BlockSpec((1, k), lambda i, j: (0, 0)),
                  pl.BlockSpec((k, tn), lambda i, j: (0, j))],
        out_specs=(pl.BlockSpec((tm, tn), lambda i, j: (i, j)), pl.BlockSpec((tm, k), lambda i, j: (i, 0))),
        compiler_params=_params("parallel", "arbitrary"),
    )(h, gain, w)


def _out_matmul(h, ya, yb, yc, w, tm, tn):
    s_len, n = h.shape
    ka, kb = ya.shape[1], yb.shape[1]
    kc = yc.shape[1]

    def body(h_ref, ya_ref, yb_ref, yc_ref, w_ref, o_ref):
        acc = jnp.dot(ya_ref[...], w_ref[0:ka, :], preferred_element_type=F32)
        acc += jnp.dot(yb_ref[...], w_ref[ka:ka + kb, :], preferred_element_type=F32)
        acc += jnp.dot(yc_ref[...], w_ref[ka + kb:ka + kb + kc, :], preferred_element_type=F32)
        o_ref[...] = h_ref[...] + acc

    return pl.pallas_call(
        body, name="out_matmul",
        out_shape=jax.ShapeDtypeStruct((s_len, n), F32),
        grid=(s_len // tm, n // tn),
        in_specs=[pl.BlockSpec((tm, tn), lambda i, j: (i, j)), pl.BlockSpec((tm, ka), lambda i, j: (i, 0)),
                  pl.BlockSpec((tm, kb), lambda i, j: (i, 0)), pl.BlockSpec((tm, kc), lambda i, j: (i, 0)),
                  pl.BlockSpec((ka + kb + kc, tn), lambda i, j: (0, j))],
        out_specs=pl.BlockSpec((tm, tn), lambda i, j: (i, j)),
        compiler_params=_params("parallel", "parallel"),
    )(h, ya, yb, yc, w)


def _ple_forward(h1, gain, p, wg, wp, tm, tn):
    s_len, d = h1.shape
    kp = p.shape[1]

    def body(hrow_ref, hcol_ref, g_ref, p_ref, wg_ref, wp_ref, o_ref, n1_ref, gate_ref, pp_ref):
        @pl.when(pl.program_id(1) == 0)
        def _():
            n1_ref[...] = _rms(hrow_ref[...], g_ref[...]).astype(BF16)

        gate = jax.nn.sigmoid(jnp.dot(n1_ref[...], wg_ref[...], preferred_element_type=F32))
        pp = jnp.dot(p_ref[...].astype(BF16), wp_ref[...], preferred_element_type=F32)
        o_ref[...] = hcol_ref[...] + gate * pp
        gate_ref[...] = gate.astype(BF16)
        pp_ref[...] = pp.astype(BF16)

    col = pl.BlockSpec((tm, tn), lambda i, j: (i, j))
    return pl.pallas_call(
        body, name="ple_forward",
        out_shape=(jax.ShapeDtypeStruct((s_len, d), F32), jax.ShapeDtypeStruct((s_len, d), BF16),
                   jax.ShapeDtypeStruct((s_len, d), BF16), jax.ShapeDtypeStruct((s_len, d), BF16)),
        grid=(s_len // tm, d // tn),
        in_specs=[pl.BlockSpec((tm, d), lambda i, j: (i, 0)), col, pl.BlockSpec((1, d), lambda i, j: (0, 0)),
                  pl.BlockSpec((tm, kp), lambda i, j: (i, 0)), pl.BlockSpec((d, tn), lambda i, j: (0, j)),
                  pl.BlockSpec((kp, tn), lambda i, j: (0, j))],
        out_specs=(col, pl.BlockSpec((tm, d), lambda i, j: (i, 0)), col, col),
        compiler_params=_params("parallel", "arbitrary"),
    )(h1, h1, gain, p, wg, wp)


def _matmul_nt(a, b, tm, tk, name):
    m, n = a.shape
    k = b.shape[0]

    def body(a_ref, b_ref, o_ref):
        o_ref[...] = lax.dot_general(a_ref[...].astype(BF16), b_ref[...].astype(BF16), _NT, preferred_element_type=F32)

    return pl.pallas_call(
        body, name=name,
        out_shape=jax.ShapeDtypeStruct((m, k), F32),
        grid=(m // tm, k // tk),
        in_specs=[pl.BlockSpec((tm, n), lambda i, j: (i, 0)), pl.BlockSpec((tk, n), lambda i, j: (j, 0))],
        out_specs=pl.BlockSpec((tm, tk), lambda i, j: (i, j)),
        compiler_params=_params("parallel", "parallel"),
    )(a, b)


def _matmul_tn(a, b, tm, tk, tn, name):
    m, k = a.shape
    n = b.shape[1]

    def body(a_ref, b_ref, o_ref):
        @pl.when(pl.program_id(2) == 0)
        def _():
            o_ref[...] = jnp.zeros_like(o_ref)

        o_ref[...] += lax.dot_general(a_ref[...].astype(BF16), b_ref[...].astype(BF16), _TN, preferred_element_type=F32)

    return pl.pallas_call(
        body, name=name,
        out_shape=jax.ShapeDtypeStruct((k, n), F32),
        grid=(k // tk, n // tn, m // tm),
        in_specs=[pl.BlockSpec((tm, tk), lambda kk, nn, mm: (mm, kk)), pl.BlockSpec((tm, tn), lambda kk, nn, mm: (mm, nn))],
        out_specs=pl.BlockSpec((tk, tn), lambda kk, nn, mm: (kk, nn)),
        compiler_params=_params("parallel", "parallel", "arbitrary"),
    )(a, b)


def _acc(ref, val, first):
    @pl.when(first)
    def _():
        ref[...] = val

    @pl.when(jnp.logical_not(first))
    def _():
        ref[...] += val


def _loss_grad(h, target, tm):
    s_len, d = h.shape

    def body(h_ref, t_ref, dh_ref, loss_ref):
        e = h_ref[...] - t_ref[...]
        dh_ref[...] = e * (1.0 / d)
        part = jnp.sum(jnp.sum(e * e, axis=-1, keepdims=True), axis=0, keepdims=True) * (0.5 / d)
        _acc(loss_ref, jnp.broadcast_to(part, loss_ref.shape), pl.program_id(0) == 0)

    row = pl.BlockSpec((tm, d), lambda i: (i, 0))
    return pl.pallas_call(
        body, name="loss_grad",
        out_shape=(jax.ShapeDtypeStruct((s_len, d), F32), jax.ShapeDtypeStruct((1, 128), F32)),
        grid=(s_len // tm,),
        in_specs=[row, row],
        out_specs=(row, pl.BlockSpec((1, 128), lambda i: (0, 0))),
        compiler_params=_params("arbitrary"),
    )(h, target)


def _rms_backward(x, gain, dn, dres, tm, name):
    s_len, d = x.shape

    def body(x_ref, g_ref, dn_ref, dres_ref, dx_ref, dg_ref):
        _, vjp = jax.vjp(_rms, x_ref[...], g_ref[...])
        dx, dg = vjp(dn_ref[...])
        dx_ref[...] = dres_ref[...] + dx
        _acc(dg_ref, dg, pl.program_id(0) == 0)

    row = pl.BlockSpec((tm, d), lambda i: (i, 0))
    vec = pl.BlockSpec((1, d), lambda i: (0, 0))
    return pl.pallas_call(
        body, name=name,
        out_shape=(jax.ShapeDtypeStruct((s_len, d), F32), jax.ShapeDtypeStruct((1, d), F32)),
        grid=(s_len // tm,),
        in_specs=[row, vec, row, row],
        out_specs=(row, vec),
        compiler_params=_params("arbitrary"),
    )(x, gain, dn, dres)


def _ple_backward_gate(dh2, gate, pp, tm):
    s_len, d = dh2.shape

    def body(dh_ref, gate_ref, pp_ref, dg_ref, dpp_ref):
        dh = dh_ref[...]
        gate = gate_ref[...].astype(F32)
        dg_ref[...] = (dh * pp_ref[...].astype(F32) * gate * (1.0 - gate)).astype(BF16)
        dpp_ref[...] = (dh * gate).astype(BF16)

    row = pl.BlockSpec((tm, d), lambda i: (i, 0))
    return pl.pallas_call(
        body, name="ple_backward_gate",
        out_shape=(jax.ShapeDtypeStruct((s_len, d), BF16), jax.ShapeDtypeStruct((s_len, d), BF16)),
        grid=(s_len // tm,),
        in_specs=[row, row, row],
        out_specs=(row, row),
        compiler_params=_params("parallel"),
    )(dh2, gate, pp)


def _sgu_in_specs(tm):
    return [pl.BlockSpec((tm, W_A), lambda i: (i, O_A // W_A)),
            pl.BlockSpec((A_HEADS, HEAD), lambda i: (0, 0)), pl.BlockSpec((A_HEADS, CHUNK, CHUNK), lambda i: (0, 0, 0)),
            pl.BlockSpec((A_HEADS, CHUNK, 1), lambda i: (0, 0, 0)), pl.BlockSpec((1, 512), lambda i: (0, 0))]


def _sgu_load(a_ref, gain_ref, ws_ref, bs_ref, ga_ref, c):
    rows = slice(c * CHUNK, (c + 1) * CHUNK)
    heads = range(A_HEADS)
    u = tuple(a_ref[rows, h * HEAD:(h + 1) * HEAD].astype(F32) for h in heads)
    v = tuple(a_ref[rows, 512 + h * HEAD:512 + (h + 1) * HEAD].astype(F32) for h in heads)
    z = tuple(a_ref[rows, 1024 + h * HEAD:1024 + (h + 1) * HEAD].astype(F32) for h in heads)
    gain = tuple(gain_ref[h:h + 1, :] for h in heads)
    ws = tuple(ws_ref[h] for h in heads)
    bs = tuple(bs_ref[h] for h in heads)
    ga = tuple(ga_ref[:, h * HEAD:(h + 1) * HEAD] for h in heads)
    return u, v, z, gain, ws, bs, ga


def _sgu_forward(proj, gain, ws, bs, ga, tm):
    s_len = proj.shape[0]

    def body(a_ref, gain_ref, ws_ref, bs_ref, ga_ref, o_ref):
        for c in range(tm // CHUNK):
            out = _sgu_chunk(*_sgu_load(a_ref, gain_ref, ws_ref, bs_ref, ga_ref, c))
            for h in range(A_HEADS):
                o_ref[c * CHUNK:(c + 1) * CHUNK, h * HEAD:(h + 1) * HEAD] = out[h].astype(BF16)

    return pl.pallas_call(
        body, name="sgu_forward",
        out_shape=jax.ShapeDtypeStruct((s_len, 512), BF16),
        grid=(s_len // tm,),
        in_specs=_sgu_in_specs(tm),
        out_specs=pl.BlockSpec((tm, 512), lambda i: (i, 0)),
        compiler_params=_params("parallel"),
    )(proj, gain, ws, bs, ga)


def _sgu_backward(proj, gain, ws, bs, ga, dy, dproj, tm):
    s_len = proj.shape[0]

    def body(a_ref, gain_ref, ws_ref, bs_ref, ga_ref, dy_ref, _, da_ref, dgain_ref, dws_ref, dbs_ref, dga_ref):
        tot = None
        for c in range(tm // CHUNK):
            args = _sgu_load(a_ref, gain_ref, ws_ref, bs_ref, ga_ref, c)
            _, vjp = jax.vjp(_sgu_chunk, *args)
            rows = slice(c * CHUNK, (c + 1) * CHUNK)
            du, dv, dz, dgain, dws, dbs, dga = vjp(tuple(dy_ref[rows, h * HEAD:(h + 1) * HEAD] for h in range(A_HEADS)))
            for h in range(A_HEADS):
                da_ref[rows, h * HEAD:(h + 1) * HEAD] = du[h].astype(BF16)
                da_ref[rows, 512 + h * HEAD:512 + (h + 1) * HEAD] = dv[h].astype(BF16)
                da_ref[rows, 1024 + h * HEAD:1024 + (h + 1) * HEAD] = dz[h].astype(BF16)
            part = (dgain, dws, dbs, dga)
            tot = part if tot is None else jax.tree.map(jnp.add, tot, part)
        dgain, dws, dbs, dga = tot
        first = pl.program_id(0) == 0
        _acc(dgain_ref, jnp.concatenate(dgain, axis=0), first)
        _acc(dga_ref, jnp.concatenate(dga, axis=-1), first)
        for h in range(A_HEADS):
            _acc(dws_ref.at[h], dws[h], first)
            _acc(dbs_ref.at[h], dbs[h], first)

    small = [pl.BlockSpec((A_HEADS, HEAD), lambda i: (0, 0)), pl.BlockSpec((A_HEADS, CHUNK, CHUNK), lambda i: (0, 0, 0)),
             pl.BlockSpec((A_HEADS, CHUNK, 1), lambda i: (0, 0, 0)), pl.BlockSpec((1, 512), lambda i: (0, 0))]
    return pl.pallas_call(
        body, name="sgu_backward",
        out_shape=(jax.ShapeDtypeStruct(dproj.shape, BF16),
                   jax.ShapeDtypeStruct((A_HEADS, HEAD), F32), jax.ShapeDtypeStruct((A_HEADS, CHUNK, CHUNK), F32),
                   jax.ShapeDtypeStruct((A_HEADS, CHUNK, 1), F32), jax.ShapeDtypeStruct((1, 512), F32)),
        grid=(s_len // tm,),
        in_specs=_sgu_in_specs(tm) + [pl.BlockSpec((tm, 512), lambda i: (i, 0)), pl.BlockSpec(memory_space=pl.ANY)],
        out_specs=(pl.BlockSpec((tm, W_A), lambda i: (i, O_A // W_A)), *small),
        input_output_aliases={6: 0},
        compiler_params=_params("arbitrary"),
    )(proj, gain, ws, bs, ga, dy, dproj)


def _halo_specs(tm, width, col, n_rows):
    per = tm // HALO
    last = n_rows // HALO - 1
    return [pl.BlockSpec((HALO, width), lambda i: (jnp.maximum(i * per - 1, 0), col)),
            pl.BlockSpec((tm, width), lambda i: (i, col)),
            pl.BlockSpec((HALO, width), lambda i: (jnp.minimum((i + 1) * per, last), col))]


def _conv_masks(tm, s_len):
    r = lax.broadcasted_iota(jnp.int32, (tm + 2 * HALO, 1), 0)
    g = pl.program_id(0) * tm - HALO + r
    return (g >= 0) & (g < s_len), (r >= HALO) & (r < HALO + tm)


def _conv_inputs(b_refs, cw_ref, cb_ref, gb_ref):
    ext = jnp.concatenate([r[...] for r in b_refs], axis=0).astype(F32)
    bb, bc, bh, bz = (ext[:, j * 512:(j + 1) * 512] for j in range(4))
    prm = (cw_ref[0:1, :], cw_ref[1:2, :], cw_ref[2:3, :], cb_ref[...], gb_ref[...])
    return (bb, bc, bh, bz), prm


def _conv_forward(proj, cw, cb, gb, tm):
    s_len = proj.shape[0]

    def body(p0, p1, p2, cw_ref, cb_ref, gb_ref, o_ref):
        acts, prm = _conv_inputs((p0, p1, p2), cw_ref, cb_ref, gb_ref)
        valid, core = _conv_masks(tm, s_len)
        y = _conv_tile(*acts, *prm, *prm, valid, core)
        o_ref[...] = y[HALO:HALO + tm].astype(BF16)

    vec = pl.BlockSpec((1, 512), lambda i: (0, 0))
    return pl.pallas_call(
        body, name="conv_forward",
        out_shape=jax.ShapeDtypeStruct((s_len, 512), BF16),
        grid=(s_len // tm,),
        in_specs=_halo_specs(tm, W_B, O_B // W_B, s_len) + [pl.BlockSpec((3, 512), lambda i: (0, 0)), vec, vec],
        out_specs=pl.BlockSpec((tm, 512), lambda i: (i, 0)),
        compiler_params=_params("parallel"),
    )(proj, proj, proj, cw, cb, gb)


def _conv_backward(proj, cw, cb, gb, dy, tm):
    s_len = proj.shape[0]

    def body(p0, p1, p2, cw_ref, cb_ref, gb_ref, d0, d1, d2, db_ref, dcw_ref, dcb_ref, dgb_ref):
        acts, prm = _conv_inputs((p0, p1, p2), cw_ref, cb_ref, gb_ref)
        valid, core = _conv_masks(tm, s_len)
        _, vjp = jax.vjp(lambda a, p: _conv_tile(*a, *p, *prm, valid, core), acts, prm)
        dy_ext = jnp.where(valid, jnp.concatenate([d0[...], d1[...], d2[...]], axis=0), 0.0)
        dacts, dprm = vjp(dy_ext)
        for j in range(4):
            db_ref[:, j * 512:(j + 1) * 512] = dacts[j][HALO:HALO + tm].astype(BF16)
        first = pl.program_id(0) == 0
        _acc(dcw_ref, jnp.concatenate(dprm[0:3], axis=0), first)
        _acc(dcb_ref, dprm[3], first)
        _acc(dgb_ref, dprm[4], first)

    vec = pl.BlockSpec((1, 512), lambda i: (0, 0))
    mat = pl.BlockSpec((3, 512), lambda i: (0, 0))
    return pl.pallas_call(
        body, name="conv_backward",
        out_shape=(jax.ShapeDtypeStruct((s_len, PROJ_W), BF16), jax.ShapeDtypeStruct((3, 512), F32),
                   jax.ShapeDtypeStruct((1, 512), F32), jax.ShapeDtypeStruct((1, 512), F32)),
        grid=(s_len // tm,),
        in_specs=_halo_specs(tm, W_B, O_B // W_B, s_len) + [mat, vec, vec] + _halo_specs(tm, 512, 1, s_len),
        out_specs=(pl.BlockSpec((tm, W_B), lambda i: (i, O_B // W_B)), mat, vec, vec),
        compiler_params=_params("arbitrary"),
    )(proj, proj, proj, cw, cb, gb, dy, dy, dy)


def _cgate_forward(o, proj, gc, tm):
    s_len = o.shape[0]

    def body(o_ref, cz_ref, gc_ref, y_ref):
        y_ref[...] = _cgate_tile(o_ref[...], cz_ref[...].astype(F32), gc_ref[...]).astype(BF16)

    row = pl.BlockSpec((tm, W_CZ), lambda i: (i, 0))
    return pl.pallas_call(
        body, name="cgate_forward",
        out_shape=jax.ShapeDtypeStruct((s_len, W_CZ), BF16),
        grid=(s_len // tm,),
        in_specs=[row, pl.BlockSpec((tm, W_CZ), lambda i: (i, O_CZ // W_CZ)), pl.BlockSpec((1, W_CZ), lambda i: (0, 0))],
        out_specs=row,
        compiler_params=_params("parallel"),
    )(o, proj, gc)


def _cgate_backward(o, proj, gc, dy, dproj, tm, stat_chunk):
    s_len = o.shape[0]
    per_stat = stat_chunk // tm

    def body(o_ref, cz_ref, gc_ref, dy_ref, _, dcz_ref, do_ref, dsum_ref, dgc_ref):
        o = o_ref[...]
        _, vjp = jax.vjp(_cgate_tile, o, cz_ref[...].astype(F32), gc_ref[...])
        do, dcz, dgc = vjp(dy_ref[...])
        dcz_ref[...] = dcz.astype(BF16)
        do_ref[...] = do.astype(BF16)
        ones = jnp.ones((8, HEAD), F32)
        for h in range(C_HEADS):
            cols = slice(h * HEAD, (h + 1) * HEAD)
            sums = lax.dot_general(ones, do[:, cols] * o[:, cols], _NT, precision=lax.Precision.HIGHEST,
                                   preferred_element_type=F32)
            dsum_ref[h, 0] = sums[0:1]
        _acc(dgc_ref, dgc, pl.program_id(0) == 0)

    row = pl.BlockSpec((tm, W_CZ), lambda i: (i, 0))
    vec = pl.BlockSpec((1, W_CZ), lambda i: (0, 0))
    return pl.pallas_call(
        body, name="cgate_backward",
        out_shape=(jax.ShapeDtypeStruct(dproj.shape, BF16), jax.ShapeDtypeStruct((s_len, W_CZ), BF16),
                   jax.ShapeDtypeStruct((C_HEADS, s_len // stat_chunk, 1, stat_chunk), F32), jax.ShapeDtypeStruct((1, W_CZ), F32)),
        grid=(s_len // tm,),
        in_specs=[row, pl.BlockSpec((tm, W_CZ), lambda i: (i, O_CZ // W_CZ)), vec,
                  pl.BlockSpec((tm, W_CZ), lambda i: (i, 1)), pl.BlockSpec(memory_space=pl.ANY)],
        out_specs=(pl.BlockSpec((tm, W_CZ), lambda i: (i, O_CZ // W_CZ)), row,
                   pl.BlockSpec((C_HEADS, 1, 1, tm), lambda i: (0, i // per_stat, 0, i % per_stat)), vec),
        input_output_aliases={4: 0},
        compiler_params=_params("arbitrary"),
    )(o, proj, gc, dy, dproj)


def _mla_small_specs():
    return [pl.BlockSpec((KV_RANK, 2 * C_HEADS * HEAD), lambda i: (0, 0)), pl.BlockSpec((1, KV_RANK), lambda i: (0, 0)),
            pl.BlockSpec((1, HEAD), lambda i: (0, 0)), pl.BlockSpec((1, ROPE), lambda i: (0, 0)),
            pl.BlockSpec((1, HEAD), lambda i: (0, 0)), pl.BlockSpec((1, ROPE), lambda i: (0, 0))]


def _mla_load(m_ref, cos_ref, sin_ref):
    qn = m_ref[:, M_QN:M_QN + C_HEADS * HEAD].astype(F32)
    qr = m_ref[:, M_QR:M_QR + C_HEADS * ROPE].astype(F32)
    ckv = m_ref[:, M_CKV:M_CKV + KV_RANK].astype(F32)
    kr = m_ref[:, M_KR:M_KR + ROPE].astype(F32)
    return qn, qr, ckv, kr, cos_ref[...], sin_ref[...]


def _mla_forward(proj, cosf, sins, wukv, kvg, qng, qrg, kng, krg, tm, kt_chunk, vt_chunk):
    s_len = proj.shape[0]

    def body(m_ref, cos_ref, sin_ref, w_ref, kvg_ref, qng_ref, qrg_ref, kng_ref, krg_ref, q_ref, k_ref, v_ref, kt_ref, vt_ref):
        q, k, v = _mla_tile(*_mla_load(m_ref, cos_ref, sin_ref), w_ref[...], kvg_ref[...], qng_ref[...], qrg_ref[...],
                            kng_ref[...], krg_ref[...])
        for h in range(C_HEADS):
            q_ref[h] = q[h].astype(BF16)
            k_ref[h] = k[h].astype(BF16)
            v_ref[h] = v[h].astype(BF16)
            kt_ref[h, 0] = jnp.concatenate([k[h][:, :HEAD].T, k[h][:, HEAD:].T], axis=0).astype(BF16)
            vt_ref[h, 0] = v[h].T.astype(BF16)

    rope_spec = pl.BlockSpec((tm, ROPE), lambda i: (i, 0))
    qk_spec = pl.BlockSpec((C_HEADS, tm, QK), lambda i: (0, i, 0))
    per_k, per_v = kt_chunk // tm, vt_chunk // tm
    return pl.pallas_call(
        body, name="mla_forward",
        out_shape=(jax.ShapeDtypeStruct((C_HEADS, s_len, QK), BF16), jax.ShapeDtypeStruct((C_HEADS, s_len, QK), BF16),
                   jax.ShapeDtypeStruct((C_HEADS, s_len, HEAD), BF16),
                   jax.ShapeDtypeStruct((C_HEADS, s_len // kt_chunk, QK, kt_chunk), BF16),
                   jax.ShapeDtypeStruct((C_HEADS, s_len // vt_chunk, HEAD, vt_chunk), BF16)),
        grid=(s_len // tm,),
        in_specs=[pl.BlockSpec((tm, W_M), lambda i: (i, O_M // W_M)), rope_spec, rope_spec] + _mla_small_specs(),
        out_specs=(qk_spec, qk_spec, pl.BlockSpec((C_HEADS, tm, HEAD), lambda i: (0, i, 0)),
                   pl.BlockSpec((C_HEADS, 1, QK, tm), lambda i: (0, i // per_k, 0, i % per_k)),
                   pl.BlockSpec((C_HEADS, 1, HEAD, tm), lambda i: (0, i // per_v, 0, i % per_v))),
        compiler_params=_params("parallel"),
    )(proj, cosf, sins, wukv, kvg, qng, qrg, kng, krg)


def _mla_backward(proj, cosf, sins, wukv, kvg, qng, qrg, kng, krg, dqt, dk, dv, dproj, tm):
    s_len = proj.shape[0]
    per_chunk = dqt.shape[3] // tm

    def body(m_ref, cos_ref, sin_ref, w_ref, kvg_ref, qng_ref, qrg_ref, kng_ref, krg_ref, dq_ref, dk_ref, dv_ref, _,
             dm_ref, dw_ref, dkvg_ref, dqng_ref, dqrg_ref, dkng_ref, dkrg_ref):
        qn, qr, ckv, kr, cosf_t, sins_t = _mla_load(m_ref, cos_ref, sin_ref)
        prm = (w_ref[...], kvg_ref[...], qng_ref[...], qrg_ref[...], kng_ref[...], krg_ref[...])
        _, vjp = jax.vjp(lambda a, p: _mla_tile(*a, cosf_t, sins_t, *p), (qn, qr, ckv, kr), prm)
        heads = range(C_HEADS)
        dacts, dprm = vjp((tuple(dq_ref[h, 0].T for h in heads), tuple(dk_ref[h] for h in heads), tuple(dv_ref[h] for h in heads)))
        dm_ref[:, M_QN:M_QN + C_HEADS * HEAD] = dacts[0].astype(BF16)
        dm_ref[:, M_QR:M_QR + C_HEADS * ROPE] = dacts[1].astype(BF16)
        dm_ref[:, M_CKV:M_CKV + KV_RANK] = dacts[2].astype(BF16)
        pad = jnp.zeros((tm, W_M - M_KR - ROPE), F32)
        dm_ref[:, M_KR:W_M] = jnp.concatenate([dacts[3], pad], axis=-1).astype(BF16)
        first = pl.program_id(0) == 0
        for ref, val in zip((dw_ref, dkvg_ref, dqng_ref, dqrg_ref, dkng_ref, dkrg_ref), dprm):
            _acc(ref, val.astype(F32), first)

    rope_spec = pl.BlockSpec((tm, ROPE), lambda i: (i, 0))
    qk_spec = pl.BlockSpec((C_HEADS, tm, QK), lambda i: (0, i, 0))
    small = _mla_small_specs()
    return pl.pallas_call(
        body, name="mla_backward",
        out_shape=(jax.ShapeDtypeStruct(dproj.shape, BF16), jax.ShapeDtypeStruct((KV_RANK, 2 * C_HEADS * HEAD), F32),
                   jax.ShapeDtypeStruct((1, KV_RANK), F32), jax.ShapeDtypeStruct((1, HEAD), F32),
                   jax.ShapeDtypeStruct((1, ROPE), F32), jax.ShapeDtypeStruct((1, HEAD), F32),
                   jax.ShapeDtypeStruct((1, ROPE), F32)),
        grid=(s_len // tm,),
        in_specs=[pl.BlockSpec((tm, W_M), lambda i: (i, O_M // W_M)), rope_spec, rope_spec] + small
                 + [pl.BlockSpec((C_HEADS, 1, QK, tm), lambda i: (0, i // per_chunk, 0, i % per_chunk)), qk_spec,
                    pl.BlockSpec((C_HEADS, tm, HEAD), lambda i: (0, i, 0)), pl.BlockSpec(memory_space=pl.ANY)],
        out_specs=(pl.BlockSpec((tm, W_M), lambda i: (i, O_M // W_M)), *small),
        input_output_aliases={12: 0},
        compiler_params=_params("arbitrary"),
    )(proj, cosf, sins, wukv, kvg, qng, qrg, kng, krg, dqt, dk, dv, dproj)


def _attention_forward(q, k, vt, tq, stat_chunk, comm=None):
    n_heads, s_len, _ = q.shape
    n_chunks, _, ck = vt.shape[1:]
    c_ops, c_in_specs, c_shapes, c_out_specs, c_sems, c_begin, c_end = _riding_exchange(comm, 2)
    n_c = len(c_ops)

    def body(q_ref, k_ref, vt_ref, *rest):
        c_ins, (o_ref, lse_ref), c_outs, sems = rest[:n_c], rest[n_c:n_c + 2], rest[n_c + 2:2 * n_c + 2], rest[2 * n_c + 2:]
        c_begin(c_ins, c_outs, sems)
        q_t = q_ref[0]

        def step(j, carry):
            m_old, l_old, acc = carry
            k_j = k_ref[0, pl.ds(pl.multiple_of(j * ck, ck), ck), :]
            s = lax.dot_general(k_j, q_t, _NT, preferred_element_type=F32)
            m_new = jnp.maximum(m_old, jnp.max(s, axis=0, keepdims=True))
            p = jnp.exp2(s - m_new)
            alpha = jnp.exp2(m_old - m_new)
            l_new = alpha * l_old + jnp.sum(p, axis=0, keepdims=True)
            acc = alpha * acc + jnp.dot(vt_ref[0, j], p.astype(BF16), preferred_element_type=F32)
            return m_new, l_new, acc

        init = (jnp.full((1, tq), -jnp.inf, F32), jnp.zeros((1, tq), F32), jnp.zeros((HEAD, tq), F32))
        m_fin, l_fin, acc = lax.fori_loop(0, n_chunks, step, init)
        o_ref[...] = (acc / l_fin).T
        lse_ref[0, 0] = m_fin + jnp.log2(l_fin)
        c_end(c_ins, c_outs, sems)

    per_stat = stat_chunk // tq
    outs = pl.pallas_call(
        body, name="attention_forward",
        out_shape=(jax.ShapeDtypeStruct((s_len, n_heads * HEAD), F32),
                   jax.ShapeDtypeStruct((n_heads, s_len // stat_chunk, 1, stat_chunk), F32), *c_shapes),
        grid=(n_heads, s_len // tq),
        in_specs=[pl.BlockSpec((1, tq, QK), lambda h, i: (h, i, 0)), pl.BlockSpec((1, s_len, QK), lambda h, i: (h, 0, 0)),
                  pl.BlockSpec((1, n_chunks, HEAD, ck), lambda h, i: (h, 0, 0, 0))] + c_in_specs,
        out_specs=(pl.BlockSpec((tq, HEAD), lambda h, i: (i, h)),
                   pl.BlockSpec((1, 1, 1, tq), lambda h, i: (h, i // per_stat, 0, i % per_stat)), *c_out_specs),
        scratch_shapes=c_sems,
        compiler_params=_params("arbitrary", "arbitrary") if comm else _params("parallel", "parallel"),
    )(q, k, vt, *c_ops)
    return outs[0], outs[1], outs[2:]


def _attention_backward(q, k, kt, v, do, lse, dsum, comm=None):
    n_heads, s_len, _ = q.shape
    tk = kt.shape[3]
    n_q, _, cq = lse.shape[1:]
    c_ops, c_in_specs, c_shapes, c_out_specs, c_sems, c_begin, c_end = _riding_exchange(comm, 2)
    n_c = len(c_ops)

    def body(q_ref, k_ref, kt_ref, v_ref, do_ref, lse_ref, dsum_ref, *rest):
        c_ins, (dqt_ref, dk_ref, dv_ref), c_outs, sems = rest[:n_c], rest[n_c:n_c + 3], rest[n_c + 3:2 * n_c + 3], rest[2 * n_c + 3:]
        c_begin(c_ins, c_outs, sems)
        first = pl.program_id(1) == 0
        k_j, kt_j, v_j = k_ref[0], kt_ref[0, 0], v_ref[0]

        def step(i, carry):
            dk, dv = carry
            rows = pl.ds(pl.multiple_of(i * cq, cq), cq)
            q_i, do_i = q_ref[0, rows, :], do_ref[rows, :]
            s = lax.dot_general(k_j, q_i, _NT, preferred_element_type=F32)
            p = jnp.exp2(s - lse_ref[0, i])
            dp = lax.dot_general(v_j, do_i, _NT, preferred_element_type=F32)
            ds = (p * (dp - dsum_ref[0, i]) * LN_2).astype(BF16)
            dv = dv + jnp.dot(p.astype(BF16), do_i, preferred_element_type=F32)
            dk = dk + jnp.dot(ds, q_i, preferred_element_type=F32)
            _acc(dqt_ref.at[0, i], jnp.dot(kt_j, ds, preferred_element_type=F32), first)
            return dk, dv

        dk, dv = lax.fori_loop(0, n_q, step, (jnp.zeros((tk, QK), F32), jnp.zeros((tk, HEAD), F32)))
        dk_ref[0] = dk
        dv_ref[0] = dv
        c_end(c_ins, c_outs, sems)

    stat = pl.BlockSpec((1, n_q, 1, cq), lambda h, j: (h, 0, 0, 0))
    outs = pl.pallas_call(
        body, name="attention_backward",
        out_shape=(jax.ShapeDtypeStruct((n_heads, n_q, QK, cq), F32), jax.ShapeDtypeStruct((n_heads, s_len, QK), F32),
                   jax.ShapeDtypeStruct((n_heads, s_len, HEAD), F32), *c_shapes),
        grid=(n_heads, s_len // tk),
        in_specs=[pl.BlockSpec((1, s_len, QK), lambda h, j: (h, 0, 0)), pl.BlockSpec((1, tk, QK), lambda h, j: (h, j, 0)),
                  pl.BlockSpec((1, 1, QK, tk), lambda h, j: (h, j, 0, 0)),
                  pl.BlockSpec((1, tk, HEAD), lambda h, j: (h, j, 0)), pl.BlockSpec((s_len, HEAD), lambda h, j: (0, h)),
                  stat, stat] + c_in_specs,
        out_specs=(pl.BlockSpec((1, n_q, QK, cq), lambda h, j: (h, 0, 0, 0)), pl.BlockSpec((1, tk, QK), lambda h, j: (h, j, 0)),
                   pl.BlockSpec((1, tk, HEAD), lambda h, j: (h, j, 0)), *c_out_specs),
        scratch_shapes=c_sems,
        compiler_params=_params("arbitrary", "arbitrary") if comm else _params("parallel", "arbitrary"),
    )(q, k, kt, v, do, lse, dsum, *c_ops)
    return outs[0], outs[1], outs[2], outs[3:]


def _exchange(arrs, gather, name):
    n = len(arrs)

    def body(*refs):
        plan = _exchange_plan(refs[:n], refs[n:2 * n], gather, *refs[2 * n:])
        _exchange_start(plan)
        _exchange_wait(plan)

    any_spec = pl.BlockSpec(memory_space=pl.ANY)
    return pl.pallas_call(
        body, name=name,
        out_shape=_exchange_out_shapes(arrs, gather),
        in_specs=[any_spec] * n,
        out_specs=tuple([any_spec] * n),
        scratch_shapes=_exchange_semaphores(n),
        compiler_params=pltpu.CompilerParams(has_side_effects=True),
    )(*arrs)


def _exchange_out_shapes(arrs, gather):
    return tuple(jax.ShapeDtypeStruct((N_DEV, *(a.shape if g else a.shape[1:])), a.dtype) for a, g in zip(arrs, gather))


def _exchange_semaphores(n):
    n_remote = n * (N_DEV - 1)
    return [pltpu.SemaphoreType.DMA((n_remote,)), pltpu.SemaphoreType.DMA((n_remote,)), pltpu.SemaphoreType.DMA((n,))]


def _exchange_plan(ins, outs, gather, send_sems, recv_sems, local_sems):
    n = len(ins)
    x, y, c = lax.axis_index("x"), lax.axis_index("y"), lax.axis_index("c")
    me = 4 * x + 2 * y + c

    def block_for(a, dev):
        return ins[a] if gather[a] else ins[a].at[dev]

    local = [pltpu.make_async_copy(block_for(a, me), outs[a].at[me], local_sems.at[a]) for a in range(n)]
    remote = []
    for k in range(1, N_DEV):
        px = 1 - x if k & 4 else x
        py = 1 - y if k & 2 else y
        pc = 1 - c if k & 1 else c
        peer = 4 * px + 2 * py + pc
        for a in range(n):
            idx = a * (N_DEV - 1) + k - 1
            send = pltpu.make_async_remote_copy(
                src_ref=block_for(a, peer), dst_ref=outs[a].at[me], send_sem=send_sems.at[idx], recv_sem=recv_sems.at[idx],
                device_id=(px, py, pc), device_id_type=pl.DeviceIdType.MESH)
            arrive = pltpu.make_async_remote_copy(
                src_ref=block_for(a, peer), dst_ref=outs[a].at[peer], send_sem=send_sems.at[idx], recv_sem=recv_sems.at[idx],
                device_id=(px, py, pc), device_id_type=pl.DeviceIdType.MESH)
            remote.append((send, arrive))
    return local, remote


def _exchange_start(plan):
    local, remote = plan
    for cp in local:
        cp.start()
    for send, _ in remote:
        send.start()


def _exchange_wait(plan):
    local, remote = plan
    for send, arrive in remote:
        send.wait_send()
        arrive.wait_recv()
    for cp in local:
        cp.wait()


def _riding_exchange(comm, n_grid):
    if comm is None:
        return [], [], (), (), [], lambda *_: None, lambda *_: None
    arrs, gather = comm
    n = len(arrs)
    any_spec = pl.BlockSpec(memory_space=pl.ANY)

    def begin(ins, outs, sems):
        @pl.when(functools.reduce(jnp.logical_and, [pl.program_id(d) == 0 for d in range(n_grid)]))
        def _():
            _exchange_start(_exchange_plan(ins, outs, gather, *sems))

    def end(ins, outs, sems):
        @pl.when(functools.reduce(jnp.logical_and, [pl.program_id(d) == pl.num_programs(d) - 1 for d in range(n_grid)]))
        def _():
            _exchange_wait(_exchange_plan(ins, outs, gather, *sems))

    return (list(arrs), [any_spec] * n, _exchange_out_shapes(arrs, gather), tuple([any_spec] * n), _exchange_semaphores(n),
            begin, end)


ADAM_TILE_ELEMS = 256 * 1024


def _sum_adam(parts, w, m, v, layer, prev, name):
    n_parts, r, c = parts.shape
    tm = r
    while tm * c > ADAM_TILE_ELEMS and tm % 16 == 0:
        tm //= 2

    def body(p_ref, w_ref, m_ref, v_ref, *rest):
        g_ref, d_ref, nm_ref, nv_ref = rest[-4:]
        g = p_ref[0].astype(F32)
        for s in range(1, n_parts):
            g = g + p_ref[s].astype(F32)
        m_new = ADAM_B1 * m_ref[...] + (1.0 - ADAM_B1) * g
        v_new = ADAM_B2 * v_ref[...] + (1.0 - ADAM_B2) * (g * g)
        m_hat = m_new / (1.0 - ADAM_B1 ** ADAM_STEP)
        v_hat = v_new / (1.0 - ADAM_B2 ** ADAM_STEP)
        g_ref[...] = g
        d_ref[...] = -ADAM_LR * (m_hat / (jnp.sqrt(v_hat) + ADAM_EPS) + ADAM_WD * w_ref[...])
        nm_ref[...] = m_new
        nv_ref[...] = v_new

    slab = pl.BlockSpec((None, tm, c), lambda i: (layer, i, 0))
    n_prev = 0 if prev is None else 4
    return pl.pallas_call(
        body, name=name,
        out_shape=(jax.ShapeDtypeStruct(w.shape, F32),) * 4,
        grid=(r // tm,),
        in_specs=[pl.BlockSpec((n_parts, tm, c), lambda i: (0, i, 0)), slab, slab, slab]
                 + [pl.BlockSpec(memory_space=pl.ANY)] * n_prev,
        out_specs=(slab, slab, slab, slab),
        input_output_aliases={4 + j: j for j in range(n_prev)},
        compiler_params=_params("parallel"),
    )(parts, w, m, v, *(prev or ()))


def _permute_in(w):
    k = w.shape[0]
    q = w[:, 3584:5120].reshape(k, C_HEADS, QK)
    return jnp.concatenate(
        [w[:, 1536:3584], w[:, 5696:6720], w[:, 0:1536], q[:, :, :HEAD].reshape(k, C_HEADS * HEAD),
         q[:, :, HEAD:].reshape(k, C_HEADS * ROPE), w[:, 5120:5632], w[:, 5632:5696],
         jnp.zeros((k, PROJ_W - IN_WIDTH), w.dtype)], axis=1)


def _unpermute_in(g):
    k = g.shape[0]
    qn = g[:, O_M + M_QN:O_M + M_QR].reshape(k, C_HEADS, HEAD)
    qr = g[:, O_M + M_QR:O_M + M_CKV].reshape(k, C_HEADS, ROPE)
    q = jnp.concatenate([qn, qr], axis=-1).reshape(k, C_HEADS * QK)
    return jnp.concatenate(
        [g[:, O_A:O_A + W_A], g[:, O_B:O_B + W_B], q, g[:, O_M + M_CKV:O_M + M_KR],
         g[:, O_M + M_KR:O_M + M_KR + ROPE], g[:, O_CZ:O_CZ + W_CZ]], axis=1)


SMALL = ("attn_norm", "sgu_norm", "w_spatial", "b_spatial", "conv_b", "kv_norm", "q_nope_norm", "q_rope_norm",
         "k_nope_norm", "k_rope_norm", "out_norm", "ple_norm")
PACK_ROWS = 256


def _pack(tensors):
    flat = jnp.concatenate([t.reshape(-1) for t in tensors])
    rows = -(-flat.shape[0] // (128 * PACK_ROWS)) * PACK_ROWS
    return jnp.pad(flat, (0, rows * 128 - flat.shape[0])).reshape(rows, 128)


def _unpack(packed, like):
    flat = packed.reshape(-1)
    out, pos = [], 0
    for t in like:
        out.append(flat[pos:pos + t.size].reshape(t.shape))
        pos += t.size
    return out


def _tile(s_len, want):
    return min(want, s_len)


ATT_FWD_QUERIES = 512
ATT_FWD_KEYS = 8192
ATT_BWD_KEYS = 512
ATT_BWD_QUERIES = 2048


def _layer_forward(h, p_l, cosf, sins, w, sm, comm):
    s_len = h.shape[0]
    tm = _tile(s_len, 512)
    proj, hn = _norm_matmul(h, sm["attn_norm"], w["w_in"], tm, 768)
    ga, gb, gc = sm["out_norm"][:, 0:512], sm["out_norm"][:, 512:1024], sm["out_norm"][:, 1024:2048]
    ya = _sgu_forward(proj, sm["sgu_norm"], sm["w_spatial"], sm["b_spatial"], ga, _tile(s_len, 256))
    yb = _conv_forward(proj, w["conv_w"], sm["conv_b"], gb, _tile(s_len, 256))
    q, k, v, kt, vt = _mla_forward(proj, cosf, sins, w["w_ukv"], sm["kv_norm"], sm["q_nope_norm"], sm["q_rope_norm"],
                                   sm["k_nope_norm"], sm["k_rope_norm"], _tile(s_len, 256), _tile(s_len, ATT_BWD_KEYS),
                                   _tile(s_len, ATT_FWD_KEYS))
    o, lse, arrived = _attention_forward(q, k, vt, _tile(s_len, ATT_FWD_QUERIES), _tile(s_len, ATT_BWD_QUERIES), comm)
    yc = _cgate_forward(o, proj, gc, _tile(s_len, 256))
    h1 = _out_matmul(h, ya, yb, yc, w["w_out"], tm, 512)
    h2, n1, gate, pp = _ple_forward(h1, sm["ple_norm"], p_l, w["w_ple_gate"], w["w_ple_proj"], tm, 512)
    saved = dict(h=h, hn=hn, proj=proj, ya=ya, yb=yb, yc=yc, q=q, k=k, v=v, kt=kt, o=o, lse=lse, h1=h1, n1=n1, gate=gate,
                 pp=pp)
    return h2, saved, arrived


def _layer_backward(dh2, p_l, cosf, sins, w, sm, sv, comm):
    s_len = dh2.shape[0]
    tm = _tile(s_len, 512)
    tr = _tile(s_len, 256)
    big, small = {}, {}
    dgp, dpp = _ple_backward_gate(dh2, sv["gate"], sv["pp"], tm)
    big["w_ple_proj"] = _matmul_tn(p_l, dpp, _tile(s_len, 1024), PLE_DIM, 1024, "grad_w_ple_proj")
    big["w_ple_gate"] = _matmul_tn(sv["n1"], dgp, _tile(s_len, 1024), 512, 1024, "grad_w_ple_gate")
    dn1 = _matmul_nt(dgp, w["w_ple_gate"], tm, 1024, "grad_ple_norm_in")
    dh1, small["ple_norm"] = _rms_backward(sv["h1"], sm["ple_norm"], dn1, dh2, tr, "ple_norm_backward")
    dy = _matmul_nt(dh1, w["w_out"], tm, 1024, "grad_branches")
    big["w_out"] = jnp.concatenate(
        [_matmul_tn(sv["ya"], dh1, _tile(s_len, 1024), 512, 1024, "grad_w_out_a"),
         _matmul_tn(sv["yb"], dh1, _tile(s_len, 1024), 512, 1024, "grad_w_out_b"),
         _matmul_tn(sv["yc"], dh1, _tile(s_len, 1024), 512, 1024, "grad_w_out_c")], axis=0)
    ga, gb, gc = sm["out_norm"][:, 0:512], sm["out_norm"][:, 512:1024], sm["out_norm"][:, 1024:2048]
    dproj, dcw, small["conv_b"], dgb = _conv_backward(sv["proj"], w["conv_w"], sm["conv_b"], gb, dy, tr)
    big["conv_w"] = dcw
    dproj, do, dsum, dgc = _cgate_backward(sv["o"], sv["proj"], gc, dy, dproj, tr, _tile(s_len, ATT_BWD_QUERIES))
    dqt, dk, dv, arrived = _attention_backward(sv["q"], sv["k"], sv["kt"], sv["v"], do, sv["lse"], dsum, comm)
    (dproj, big["w_ukv"], small["kv_norm"], small["q_nope_norm"], small["q_rope_norm"], small["k_nope_norm"],
     small["k_rope_norm"]) = _mla_backward(sv["proj"], cosf, sins, w["w_ukv"], sm["kv_norm"], sm["q_nope_norm"],
                                            sm["q_rope_norm"], sm["k_nope_norm"], sm["k_rope_norm"], dqt, dk, dv, dproj,
                                            _tile(s_len, 128))
    dproj, small["sgu_norm"], small["w_spatial"], small["b_spatial"], dga = _sgu_backward(
        sv["proj"], sm["sgu_norm"], sm["w_spatial"], sm["b_spatial"], ga, dy, dproj, tr)
    small["out_norm"] = jnp.concatenate([dga, dgb, dgc], axis=1)
    big["w_in"] = _matmul_tn(sv["hn"], dproj, _tile(s_len, 1024), 512, 2304, "grad_w_in")
    dhn = _matmul_nt(dproj, w["w_in"], tm, 512, "grad_attn_norm_in")
    dh, small["attn_norm"] = _rms_backward(sv["h"], sm["attn_norm"], dhn, dh1, tr, "attn_norm_backward")
    return dh, big, small, arrived


def _layer_small(params, layer):
    return dict(
        attn_norm=params["attn_norm"][layer][None, :], sgu_norm=params["sgu_norm"][layer],
        w_spatial=params["w_spatial"][layer], b_spatial=params["b_spatial"][layer][:, :, None],
        conv_b=params["conv_b"][layer][None, :], kv_norm=params["kv_norm"][layer][None, :],
        q_nope_norm=params["q_nope_norm"][layer][None, :], q_rope_norm=params["q_rope_norm"][layer][None, :],
        k_nope_norm=params["k_nope_norm"][layer][None, :], k_rope_norm=params["k_rope_norm"][layer][None, :],
        out_norm=params["out_norm"][layer][None, :], ple_norm=params["ple_norm"][layer][None, :])


BIG = ("w_in", "w_ukv", "w_out", "w_ple_gate", "w_ple_proj")


def _assemble_layer(gathered, conv_w):
    g_in, g_ukv, g_out, g_gate, g_proj = gathered
    w_in = g_in.transpose(1, 0, 2).reshape(g_in.shape[1], IN_WIDTH)
    w_ukv = g_ukv.reshape(N_DEV, KV_RANK, 2, HEAD).transpose(1, 2, 0, 3).reshape(KV_RANK, 2 * C_HEADS * HEAD)
    return dict(w_in=_permute_in(w_in), conv_w=conv_w, w_ukv=w_ukv, w_out=g_out.reshape(D_MODEL, D_MODEL),
                w_ple_gate=g_gate.reshape(D_MODEL, D_MODEL), w_ple_proj=g_proj.transpose(1, 0, 2).reshape(PLE_DIM, D_MODEL))


def _layer_parts(big):
    g_in = big["w_in"]
    return [
        _unpermute_in(g_in).reshape(g_in.shape[0], N_DEV, -1).transpose(1, 0, 2).astype(BF16),
        big["w_ukv"].reshape(KV_RANK, 2, N_DEV, HEAD).transpose(2, 0, 1, 3).reshape(N_DEV, KV_RANK, 2 * HEAD).astype(BF16),
        big["w_out"].reshape(N_DEV, -1, D_MODEL).astype(BF16),
        big["w_ple_gate"].reshape(N_DEV, -1, D_MODEL).astype(BF16),
        big["w_ple_proj"].reshape(PLE_DIM, N_DEV, -1).transpose(1, 0, 2).astype(BF16)]


def _step_local(xs, ps, pos, target, shards, conv_w, params):
    inv = 1.0 / (ROPE_BASE ** (jnp.arange(0, ROPE, 2, dtype=F32) / ROPE))
    ang = pos.astype(F32)[:, None] * inv
    cos, sin = jnp.cos(ang), jnp.sin(ang)
    cosf = jnp.concatenate([cos, cos], axis=-1)
    sins = jnp.concatenate([-sin, sin], axis=-1)
    gather_all = [True] * len(BIG)
    scatter_all = [False] * len(BIG)

    def layer_shards(layer):
        return [shards[n][layer] for n in BIG]

    h = xs
    saved, weights = [], []
    smalls = [_layer_small(params, layer) for layer in range(DEPTH)]
    arrived = _exchange(layer_shards(0), gather_all, "gather_first_layer")
    for layer in range(DEPTH):
        weights.append(_assemble_layer(arrived, conv_w[layer]))
        comm = (layer_shards(layer + 1), gather_all) if layer + 1 < DEPTH else None
        h, sv, arrived = _layer_forward(h, ps[layer], cosf, sins, weights[layer], smalls[layer], comm)
        saved.append(sv)
    dh, loss = _loss_grad(h, target, _tile(h.shape[0], 512))
    received, conv_grads, small_grads = [None] * DEPTH, [None] * DEPTH, [None] * DEPTH
    comm = None
    for layer in reversed(range(DEPTH)):
        dh, big, small_grads[layer], arrived = _layer_backward(dh, ps[layer], cosf, sins, weights[layer], smalls[layer],
                                                               saved[layer], comm)
        if comm is not None:
            received[layer + 1] = arrived
        conv_grads[layer] = big["conv_w"]
        comm = (_layer_parts(big), scatter_all)
    received[0] = _exchange(comm[0], scatter_all, "scatter_last_layer")
    return loss, dh, received, conv_grads, small_grads


def kernel(x, p, positions, attn_norm, w_in, sgu_norm, w_spatial, b_spatial, conv_w, conv_b, kv_norm, w_ukv, q_nope_norm, q_rope_norm, k_nope_norm, k_rope_norm, out_norm, w_out, ple_norm, w_ple_gate, w_ple_proj, loss_target, m_attn_norm, m_w_in, m_sgu_norm, m_w_spatial, m_b_spatial, m_conv_w, m_conv_b, m_kv_norm, m_w_ukv, m_q_nope_norm, m_q_rope_norm, m_k_nope_norm, m_k_rope_norm, m_out_norm, m_w_out, m_ple_norm, m_w_ple_gate, m_w_ple_proj, v_attn_norm, v_w_in, v_sgu_norm, v_w_spatial, v_b_spatial, v_conv_w, v_conv_b, v_kv_norm, v_w_ukv, v_q_nope_norm, v_q_rope_norm, v_k_nope_norm, v_k_rope_norm, v_out_norm, v_w_out, v_ple_norm, v_w_ple_gate, v_w_ple_proj):
    order = ("attn_norm", "w_in", "sgu_norm", "w_spatial", "b_spatial", "conv_w", "conv_b", "kv_norm", "w_ukv",
             "q_nope_norm", "q_rope_norm", "k_nope_norm", "k_rope_norm", "out_norm", "w_out", "ple_norm", "w_ple_gate",
             "w_ple_proj")
    wts = dict(zip(order, (attn_norm, w_in, sgu_norm, w_spatial, b_spatial, conv_w, conv_b, kv_norm, w_ukv, q_nope_norm,
                           q_rope_norm, k_nope_norm, k_rope_norm, out_norm, w_out, ple_norm, w_ple_gate, w_ple_proj)))
    mom = dict(zip(order, (m_attn_norm, m_w_in, m_sgu_norm, m_w_spatial, m_b_spatial, m_conv_w, m_conv_b, m_kv_norm, m_w_ukv,
                           m_q_nope_norm, m_q_rope_norm, m_k_nope_norm, m_k_rope_norm, m_out_norm, m_w_out, m_ple_norm,
                           m_w_ple_gate, m_w_ple_proj)))
    var = dict(zip(order, (v_attn_norm, v_w_in, v_sgu_norm, v_w_spatial, v_b_spatial, v_conv_w, v_conv_b, v_kv_norm, v_w_ukv,
                           v_q_nope_norm, v_q_rope_norm, v_k_nope_norm, v_k_rope_norm, v_out_norm, v_w_out, v_ple_norm,
                           v_w_ple_gate, v_w_ple_proj)))

    conv_shard = wts["conv_w"]
    (conv_all,) = _exchange([conv_shard.reshape(-1, 128)], [True], "gather_conv_w")
    conv_full = conv_all.reshape(N_DEV, DEPTH, 3, -1).transpose(1, 2, 0, 3).reshape(DEPTH, 3, -1)
    shards = {n: wts[n].astype(BF16) for n in BIG}
    loss_part, grad_x, received, conv_grads, small_grads = _step_local(
        x[0], p[:, 0], positions[0], loss_target[0], shards, conv_full, wts)
    loss = lax.psum(loss_part[0, 0], ("x", "y", "c"))

    def small_grad(name):
        g = jnp.stack([sg[name] for sg in small_grads])
        return g.reshape(wts[name].shape)

    conv_grad = jnp.stack(conv_grads)
    like = [wts[n] for n in SMALL] + [conv_grad]
    packed = _pack([small_grad(n) for n in SMALL] + [conv_grad])
    (small_parts,) = _exchange([packed], [True], "gather_small_grads")
    filler = [jnp.zeros_like(conv_grad), jnp.zeros_like(conv_grad), jnp.ones_like(conv_grad)]
    small_out = _sum_adam(small_parts, *(_pack([src[n] for n in SMALL] + [fill])[None] for src, fill in zip((wts, mom, var), filler)),
                          0, None, "adam_small")
    unpacked = [_unpack(o[0], like) for o in small_out]
    results = {n: vals for n, vals in zip(SMALL, zip(*[u[:-1] for u in unpacked]))}
    me = 4 * lax.axis_index("x") + 2 * lax.axis_index("y") + lax.axis_index("c")
    width = conv_shard.shape[2]
    conv_local = lax.dynamic_slice_in_dim(unpacked[0][-1], me * width, width, axis=2)
    as_slab = (lambda t: t.reshape(1, -1, width))
    conv_out = _sum_adam(as_slab(conv_local), as_slab(conv_shard), as_slab(mom["conv_w"]), as_slab(var["conv_w"]), 0, None,
                         "adam_conv_w")
    results["conv_w"] = tuple(o.reshape(conv_shard.shape) for o in conv_out)

    for j, name in enumerate(BIG):
        outs = None
        for layer in range(DEPTH):
            outs = _sum_adam(received[layer][j], wts[name], mom[name], var[name], layer, outs, "adam_" + name)
        results[name] = outs

    grads, deltas, new_m, new_v = ([results[n][j] for n in order] for j in range(4))
    return (loss, grad_x[None], *grads, *deltas, *new_m, *new_v)
```

```python
import functools

import jax
import jax.numpy as jnp
from jax import lax
from jax.experimental import pallas as pl
from jax.experimental.pallas import tpu as pltpu

F32 = jnp.float32
BF16 = jnp.bfloat16

N_DEV = 8
DEPTH = 4
D_MODEL = 2048
EPS = 1e-6
CHUNK = 128
A_HEADS = 4
HEAD = 128
ROPE = 64
HALF = ROPE // 2
C_HEADS = 8
KV_RANK = 512
PLE_DIM = 256
ROPE_BASE = 10000.0
IN_WIDTH = 6720
QK = HEAD + ROPE
SCALE = QK ** -0.5
LOG2_E = 1.4426950408889634
LN_2 = 0.6931471805599453
Q_SCALE = SCALE * LOG2_E
HALO = 8

O_B = 0
W_B = 2048
O_CZ = 2048
W_CZ = 1024
O_A = 3072
W_A = 1536
O_M = 4608
W_M = 2304
M_QN, M_QR, M_CKV, M_KR = 0, 1024, 1536, 2048
PROJ_W = 6912

ADAM_LR = 0.001
ADAM_B1 = 0.9
ADAM_B2 = 0.999
ADAM_EPS = 1e-08
ADAM_WD = 0.01
ADAM_STEP = 10

VMEM_LIMIT = 56 * 1024 * 1024

_NT = (((1,), (1,)), ((), ()))
_TN = (((0,), (0,)), ((), ()))


def _params(*sem):
    return pltpu.CompilerParams(dimension_semantics=sem, vmem_limit_bytes=VMEM_LIMIT)


@jax.custom_vjp
def _bdot(a, b):
    return jnp.dot(a.astype(BF16), b.astype(BF16), preferred_element_type=F32)


def _bdot_fwd(a, b):
    return _bdot(a, b), (a, b)


def _bdot_bwd(res, g):
    a, b = res
    gb = g.astype(BF16)
    da = lax.dot_general(gb, b.astype(BF16), _NT, preferred_element_type=F32)
    db = lax.dot_general(a.astype(BF16), gb, _TN, preferred_element_type=F32)
    return da.astype(a.dtype), db.astype(b.dtype)


_bdot.defvjp(_bdot_fwd, _bdot_bwd)


@functools.partial(jax.custom_vjp, nondiff_argnums=(1,))
def _split(x, n):
    w = x.shape[-1] // n
    return tuple(x[:, i * w:(i + 1) * w] for i in range(n))


def _split_fwd(x, n):
    return _split(x, n), None


def _split_bwd(n, _, gs):
    return (jnp.concatenate(gs, axis=-1),)


_split.defvjp(_split_fwd, _split_bwd)


@functools.partial(jax.custom_vjp, nondiff_argnums=(1,))
def _shift_rows(x, k):
    return pltpu.roll(x, k % x.shape[0], 0)


def _shift_rows_fwd(x, k):
    return _shift_rows(x, k), None


def _shift_rows_bwd(k, _, g):
    return (_shift_rows(g, -k),)


_shift_rows.defvjp(_shift_rows_fwd, _shift_rows_bwd)


@jax.custom_vjp
def _swap_halves(x):
    h = x.shape[-1] // 2
    return jnp.concatenate([x[:, h:], x[:, :h]], axis=-1)


def _swap_halves_fwd(x):
    return _swap_halves(x), None


def _swap_halves_bwd(_, g):
    return (_swap_halves(g),)


_swap_halves.defvjp(_swap_halves_fwd, _swap_halves_bwd)


def _rms(x, g):
    return x * lax.rsqrt(jnp.mean(x * x, axis=-1, keepdims=True) + EPS) * g


def _rope(x, cosf, sins):
    return x * cosf + _swap_halves(x) * sins


def _sgu_chunk(u, v, z, gain, ws, bs, ga):
    ys = []
    for h in range(A_HEADS):
        vn = _rms(v[h], gain[h])
        s = _bdot(ws[h], vn) + bs[h]
        ys.append(u[h] * s * jax.nn.silu(z[h]))
    ss = sum(jnp.sum(y * y, axis=-1, keepdims=True) for y in ys) * (1.0 / (A_HEADS * HEAD))
    r = lax.rsqrt(ss + EPS)
    return tuple(ys[h] * r * ga[h] for h in range(A_HEADS))


def _conv_tile(bb, bc, bh, bz, w0, w1, w2, cb, gb, w0h, w1h, w2h, cbh, gbh, valid, core):
    t = jnp.where(valid, bc * bh, 0.0)
    y = (jnp.where(core, cb, cbh)
         + _shift_rows(t, 1) * jnp.where(core, w0, w0h)
         + t * jnp.where(core, w1, w1h)
         + _shift_rows(t, -1) * jnp.where(core, w2, w2h))
    return _rms(bb * y * jax.nn.silu(bz), jnp.where(core, gb, gbh))


def _cgate_tile(o, cz, gc):
    return _rms(o * jax.nn.silu(cz), gc)


def _mla_tile(qn, qr, ckv, kr, cosf, sins, wukv, kvg, qng, qrg, kng, krg):
    kv = _split(_bdot(_rms(ckv, kvg), wukv), 2 * C_HEADS)
    k_r = _rope(_rms(kr, krg), cosf, sins)
    qn_h = _split(qn, C_HEADS)
    qr_h = _split(qr, C_HEADS)
    q, k, v = [], [], []
    for h in range(C_HEADS):
        q.append(jnp.concatenate([_rms(qn_h[h], qng), _rope(_rms(qr_h[h], qrg), cosf, sins)], axis=-1) * Q_SCALE)
        k.append(jnp.concatenate([_rms(kv[h], kng), k_r], axis=-1))
        v.append(kv[C_HEADS + h])
    return tuple(q), tuple(k), tuple(v)


def _norm_matmul(h, gain, w, tm, tn, comm=None):
    s_len, k = h.shape
    n = w.shape[1]
    c_ops, c_in_specs, c_shapes, c_out_specs, c_sems, c_begin, c_end = _riding_exchange(comm, 2)
    n_c = len(c_ops)

    def body(h_ref, g_ref, w_ref, *rest):
        c_ins, (o_ref, hn_ref), c_outs, sems = rest[:n_c], rest[n_c:n_c + 2], rest[n_c + 2:2 * n_c + 2], rest[2 * n_c + 2:]
        c_begin(c_ins, c_outs, sems)

        @pl.when(pl.program_id(1) == 0)
        def _():
            hn_ref[...] = _rms(h_ref[...], g_ref[...]).astype(BF16)

        o_ref[...] = jnp.dot(hn_ref[...], w_ref[...], preferred_element_type=F32).astype(BF16)
        c_end(c_ins, c_outs, sems)

    outs = pl.pallas_call(
        body, name="norm_matmul",
        out_shape=(jax.ShapeDtypeStruct((s_len, n), BF16), jax.ShapeDtypeStruct((s_len, k), BF16), *c_shapes),
        grid=(s_len // tm, n // tn),
        in_specs=[pl.BlockSpec((tm, k), lambda i, j: (i, 0)), pl.BlockSpec((1, k), lambda i, j: (0, 0)),
                  pl.BlockSpec((k, tn), lambda i, j: (0, j))] + c_in_specs,
        out_specs=(pl.BlockSpec((tm, tn), lambda i, j: (i, j)), pl.BlockSpec((tm, k), lambda i, j: (i, 0)), *c_out_specs),
        scratch_shapes=c_sems,
        compiler_params=_params("arbitrary", "arbitrary") if comm else _params("parallel", "arbitrary"),
    )(h, gain, w, *c_ops)
    return outs[0], outs[1], outs[2:]


def _out_matmul(h, ya, yb, yc, w, tm, tn):
    s_len, n = h.shape
    ka, kb = ya.shape[1], yb.shape[1]
    kc = yc.shape[1]

    def body(h_ref, ya_ref, yb_ref, yc_ref, w_ref, o_ref):
        acc = jnp.dot(ya_ref[...], w_ref[0:ka, :], preferred_element_type=F32)
        acc += jnp.dot(yb_ref[...], w_ref[ka:ka + kb, :], preferred_element_type=F32)
        acc += jnp.dot(yc_ref[...], w_ref[ka + kb:ka + kb + kc, :], preferred_element_type=F32)
        o_ref[...] = h_ref[...] + acc

    return pl.pallas_call(
        body, name="out_matmul",
        out_shape=jax.ShapeDtypeStruct((s_len, n), F32),
        grid=(s_len // tm, n // tn),
        in_specs=[pl.BlockSpec((tm, tn), lambda i, j: (i, j)), pl.BlockSpec((tm, ka), lambda i, j: (i, 0)),
                  pl.BlockSpec((tm, kb), lambda i, j: (i, 0)), pl.BlockSpec((tm, kc), lambda i, j: (i, 0)),
                  pl.BlockSpec((ka + kb + kc, tn), lambda i, j: (0, j))],
        out_specs=pl.BlockSpec((tm, tn), lambda i, j: (i, j)),
        compiler_params=_params("parallel", "parallel"),
    )(h, ya, yb, yc, w)


def _ple_forward(h1, gain, p, wg, wp, tm, tn):
    s_len, d = h1.shape
    kp = p.shape[1]

    def body(hrow_ref, hcol_ref, g_ref, p_ref, wg_ref, wp_ref, o_ref, n1_ref, gate_ref, pp_ref):
        @pl.when(pl.program_id(1) == 0)
        def _():
            n1_ref[...] = _rms(hrow_ref[...], g_ref[...]).astype(BF16)

        gate = jax.nn.sigmoid(jnp.dot(n1_ref[...], wg_ref[...], preferred_element_type=F32))
        pp = jnp.dot(p_ref[...].astype(BF16), wp_ref[...], preferred_element_type=F32)
        o_ref[...] = hcol_ref[...] + gate * pp
        gate_ref[...] = gate.astype(BF16)
        pp_ref[...] = pp.astype(BF16)

    col = pl.BlockSpec((tm, tn), lambda i, j: (i, j))
    return pl.pallas_call(
        body, name="ple_forward",
        out_shape=(jax.ShapeDtypeStruct((s_len, d), F32), jax.ShapeDtypeStruct((s_len, d), BF16),
                   jax.ShapeDtypeStruct((s_len, d), BF16), jax.ShapeDtypeStruct((s_len, d), BF16)),
        grid=(s_len // tm, d // tn),
        in_specs=[pl.BlockSpec((tm, d), lambda i, j: (i, 0)), col, pl.BlockSpec((1, d), lambda i, j: (0, 0)),
                  pl.BlockSpec((tm, kp), lambda i, j: (i, 0)), pl.BlockSpec((d, tn), lambda i, j: (0, j)),
                  pl.BlockSpec((kp, tn), lambda i, j: (0, j))],
        out_specs=(col, pl.BlockSpec((tm, d), lambda i, j: (i, 0)), col, col),
        compiler_params=_params("parallel", "arbitrary"),
    )(h1, h1, gain, p, wg, wp)


def _matmul_nt(a, b, tm, tk, name, comm=None):
    m, n = a.shape
    k = b.shape[0]
    c_ops, c_in_specs, c_shapes, c_out_specs, c_sems, c_begin, c_end = _riding_exchange(comm, 2)
    n_c = len(c_ops)

    def body(a_ref, b_ref, *rest):
        c_ins, o_ref, c_outs, sems = rest[:n_c], rest[n_c], rest[n_c + 1:2 * n_c + 1], rest[2 * n_c + 1:]
        c_begin(c_ins, c_outs, sems)
        o_ref[...] = lax.dot_general(a_ref[...].astype(BF16), b_ref[...].astype(BF16), _NT, preferred_element_type=F32)
        c_end(c_ins, c_outs, sems)

    outs = pl.pallas_call(
        body, name=name,
        out_shape=(jax.ShapeDtypeStruct((m, k), F32), *c_shapes),
        grid=(m // tm, k // tk),
        in_specs=[pl.BlockSpec((tm, n), lambda i, j: (i, 0)), pl.BlockSpec((tk, n), lambda i, j: (j, 0))] + c_in_specs,
        out_specs=(pl.BlockSpec((tm, tk), lambda i, j: (i, j)), *c_out_specs),
        scratch_shapes=c_sems,
        compiler_params=_params("arbitrary", "arbitrary") if comm else _params("parallel", "parallel"),
    )(a, b, *c_ops)
    return outs[0], outs[1:]


def _matmul_tn(a, b, tm, tk, tn, name):
    m, k = a.shape
    n = b.shape[1]
    n_m = m // tm

    def body(a_ref, b_ref, o_ref, acc_ref):
        part = lax.dot_general(a_ref[...].astype(BF16), b_ref[...].astype(BF16), _TN, preferred_element_type=F32)
        _acc(acc_ref, part, pl.program_id(2) == 0)

        @pl.when(pl.program_id(2) == n_m - 1)
        def _():
            o_ref[...] = acc_ref[...].astype(BF16)

    return pl.pallas_call(
        body, name=name,
        out_shape=jax.ShapeDtypeStruct((k, n), BF16),
        grid=(k // tk, n // tn, n_m),
        in_specs=[pl.BlockSpec((tm, tk), lambda kk, nn, mm: (mm, kk)), pl.BlockSpec((tm, tn), lambda kk, nn, mm: (mm, nn))],
        out_specs=pl.BlockSpec((tk, tn), lambda kk, nn, mm: (kk, nn)),
        scratch_shapes=[pltpu.VMEM((tk, tn), F32)],
        compiler_params=_params("parallel", "parallel", "arbitrary"),
    )(a, b)


def _acc(ref, val, first):
    @pl.when(first)
    def _():
        ref[...] = val

    @pl.when(jnp.logical_not(first))
    def _():
        ref[...] += val


def _loss_grad(h, target, tm):
    s_len, d = h.shape

    def body(h_ref, t_ref, dh_ref, loss_ref):
        e = h_ref[...] - t_ref[...]
        dh_ref[...] = e * (1.0 / d)
        part = jnp.sum(jnp.sum(e * e, axis=-1, keepdims=True), axis=0, keepdims=True) * (0.5 / d)
        _acc(loss_ref, jnp.broadcast_to(part, loss_ref.shape), pl.program_id(0) == 0)

    row = pl.BlockSpec((tm, d), lambda i: (i, 0))
    return pl.pallas_call(
        body, name="loss_grad",
        out_shape=(jax.ShapeDtypeStruct((s_len, d), F32), jax.ShapeDtypeStruct((1, 128), F32)),
        grid=(s_len // tm,),
        in_specs=[row, row],
        out_specs=(row, pl.BlockSpec((1, 128), lambda i: (0, 0))),
        compiler_params=_params("arbitrary"),
    )(h, target)


def _rms_backward(x, gain, dn, dres, tm, name):
    s_len, d = x.shape

    def body(x_ref, g_ref, dn_ref, dres_ref, dx_ref, dg_ref):
        _, vjp = jax.vjp(_rms, x_ref[...], g_ref[...])
        dx, dg = vjp(dn_ref[...])
        dx_ref[...] = dres_ref[...] + dx
        _acc(dg_ref, dg, pl.program_id(0) == 0)

    row = pl.BlockSpec((tm, d), lambda i: (i, 0))
    vec = pl.BlockSpec((1, d), lambda i: (0, 0))
    return pl.pallas_call(
        body, name=name,
        out_shape=(jax.ShapeDtypeStruct((s_len, d), F32), jax.ShapeDtypeStruct((1, d), F32)),
        grid=(s_len // tm,),
        in_specs=[row, vec, row, row],
        out_specs=(row, vec),
        compiler_params=_params("arbitrary"),
    )(x, gain, dn, dres)


def _ple_backward_gate(dh2, gate, pp, tm):
    s_len, d = dh2.shape

    def body(dh_ref, gate_ref, pp_ref, dg_ref, dpp_ref):
        dh = dh_ref[...]
        gate = gate_ref[...].astype(F32)
        dg_ref[...] = (dh * pp_ref[...].astype(F32) * gate * (1.0 - gate)).astype(BF16)
        dpp_ref[...] = (dh * gate).astype(BF16)

    row = pl.BlockSpec((tm, d), lambda i: (i, 0))
    return pl.pallas_call(
        body, name="ple_backward_gate",
        out_shape=(jax.ShapeDtypeStruct((s_len, d), BF16), jax.ShapeDtypeStruct((s_len, d), BF16)),
        grid=(s_len // tm,),
        in_specs=[row, row, row],
        out_specs=(row, row),
        compiler_params=_params("parallel"),
    )(dh2, gate, pp)


def _sgu_in_specs(tm):
    return [pl.BlockSpec((tm, W_A), lambda i: (i, O_A // W_A)),
            pl.BlockSpec((A_HEADS, HEAD), lambda i: (0, 0)), pl.BlockSpec((A_HEADS, CHUNK, CHUNK), lambda i: (0, 0, 0)),
            pl.BlockSpec((A_HEADS, CHUNK, 1), lambda i: (0, 0, 0)), pl.BlockSpec((1, 512), lambda i: (0, 0))]


def _sgu_load(a_ref, gain_ref, ws_ref, bs_ref, ga_ref, c):
    rows = slice(c * CHUNK, (c + 1) * CHUNK)
    heads = range(A_HEADS)
    u = tuple(a_ref[rows, h * HEAD:(h + 1) * HEAD].astype(F32) for h in heads)
    v = tuple(a_ref[rows, 512 + h * HEAD:512 + (h + 1) * HEAD].astype(F32) for h in heads)
    z = tuple(a_ref[rows, 1024 + h * HEAD:1024 + (h + 1) * HEAD].astype(F32) for h in heads)
    gain = tuple(gain_ref[h:h + 1, :] for h in heads)
    ws = tuple(ws_ref[h] for h in heads)
    bs = tuple(bs_ref[h] for h in heads)
    ga = tuple(ga_ref[:, h * HEAD:(h + 1) * HEAD] for h in heads)
    return u, v, z, gain, ws, bs, ga


def _sgu_forward(proj, gain, ws, bs, ga, tm):
    s_len = proj.shape[0]

    def body(a_ref, gain_ref, ws_ref, bs_ref, ga_ref, o_ref):
        for c in range(tm // CHUNK):
            out = _sgu_chunk(*_sgu_load(a_ref, gain_ref, ws_ref, bs_ref, ga_ref, c))
            for h in range(A_HEADS):
                o_ref[c * CHUNK:(c + 1) * CHUNK, h * HEAD:(h + 1) * HEAD] = out[h].astype(BF16)

    return pl.pallas_call(
        body, name="sgu_forward",
        out_shape=jax.ShapeDtypeStruct((s_len, 512), BF16),
        grid=(s_len // tm,),
        in_specs=_sgu_in_specs(tm),
        out_specs=pl.BlockSpec((tm, 512), lambda i: (i, 0)),
        compiler_params=_params("parallel"),
    )(proj, gain, ws, bs, ga)


def _sgu_backward(proj, gain, ws, bs, ga, dy, dproj, tm):
    s_len = proj.shape[0]

    def body(a_ref, gain_ref, ws_ref, bs_ref, ga_ref, dy_ref, _, da_ref, dgain_ref, dws_ref, dbs_ref, dga_ref):
        tot = None
        for c in range(tm // CHUNK):
            args = _sgu_load(a_ref, gain_ref, ws_ref, bs_ref, ga_ref, c)
            _, vjp = jax.vjp(_sgu_chunk, *args)
            rows = slice(c * CHUNK, (c + 1) * CHUNK)
            du, dv, dz, dgain, dws, dbs, dga = vjp(tuple(dy_ref[rows, h * HEAD:(h + 1) * HEAD] for h in range(A_HEADS)))
            for h in range(A_HEADS):
                da_ref[rows, h * HEAD:(h + 1) * HEAD] = du[h].astype(BF16)
                da_ref[rows, 512 + h * HEAD:512 + (h + 1) * HEAD] = dv[h].astype(BF16)
                da_ref[rows, 1024 + h * HEAD:1024 + (h + 1) * HEAD] = dz[h].astype(BF16)
            part = (dgain, dws, dbs, dga)
            tot = part if tot is None else jax.tree.map(jnp.add, tot, part)
        dgain, dws, dbs, dga = tot
        first = pl.program_id(0) == 0
        _acc(dgain_ref, jnp.concatenate(dgain, axis=0), first)
        _acc(dga_ref, jnp.concatenate(dga, axis=-1), first)
        for h in range(A_HEADS):
            _acc(dws_ref.at[h], dws[h], first)
            _acc(dbs_ref.at[h], dbs[h], first)

    small = [pl.BlockSpec((A_HEADS, HEAD), lambda i: (0, 0)), pl.BlockSpec((A_HEADS, CHUNK, CHUNK), lambda i: (0, 0, 0)),
             pl.BlockSpec((A_HEADS, CHUNK, 1), lambda i: (0, 0, 0)), pl.BlockSpec((1, 512), lambda i: (0, 0))]
    return pl.pallas_call(
        body, name="sgu_backward",
        out_shape=(jax.ShapeDtypeStruct(dproj.shape, BF16),
                   jax.ShapeDtypeStruct((A_HEADS, HEAD), F32), jax.ShapeDtypeStruct((A_HEADS, CHUNK, CHUNK), F32),
                   jax.ShapeDtypeStruct((A_HEADS, CHUNK, 1), F32), jax.ShapeDtypeStruct((1, 512), F32)),
        grid=(s_len // tm,),
        in_specs=_sgu_in_specs(tm) + [pl.BlockSpec((tm, 512), lambda i: (i, 0)), pl.BlockSpec(memory_space=pl.ANY)],
        out_specs=(pl.BlockSpec((tm, W_A), lambda i: (i, O_A // W_A)), *small),
        input_output_aliases={6: 0},
        compiler_params=_params("arbitrary"),
    )(proj, gain, ws, bs, ga, dy, dproj)


def _halo_specs(tm, width, col, n_rows):
    per = tm // HALO
    last = n_rows // HALO - 1
    return [pl.BlockSpec((HALO, width), lambda i: (jnp.maximum(i * per - 1, 0), col)),
            pl.BlockSpec((tm, width), lambda i: (i, col)),
            pl.BlockSpec((HALO, width), lambda i: (jnp.minimum((i + 1) * per, last), col))]


def _conv_masks(tm, s_len):
    r = lax.broadcasted_iota(jnp.int32, (tm + 2 * HALO, 1), 0)
    g = pl.program_id(0) * tm - HALO + r
    return (g >= 0) & (g < s_len), (r >= HALO) & (r < HALO + tm)


def _conv_inputs(b_refs, cw_ref, cb_ref, gb_ref):
    ext = jnp.concatenate([r[...] for r in b_refs], axis=0).astype(F32)
    bb, bc, bh, bz = (ext[:, j * 512:(j + 1) * 512] for j in range(4))
    prm = (cw_ref[0:1, :], cw_ref[1:2, :], cw_ref[2:3, :], cb_ref[...], gb_ref[...])
    return (bb, bc, bh, bz), prm


def _conv_forward(proj, cw, cb, gb, tm):
    s_len = proj.shape[0]

    def body(p0, p1, p2, cw_ref, cb_ref, gb_ref, o_ref):
        acts, prm = _conv_inputs((p0, p1, p2), cw_ref, cb_ref, gb_ref)
        valid, core = _conv_masks(tm, s_len)
        y = _conv_tile(*acts, *prm, *prm, valid, core)
        o_ref[...] = y[HALO:HALO + tm].astype(BF16)

    vec = pl.BlockSpec((1, 512), lambda i: (0, 0))
    return pl.pallas_call(
        body, name="conv_forward",
        out_shape=jax.ShapeDtypeStruct((s_len, 512), BF16),
        grid=(s_len // tm,),
        in_specs=_halo_specs(tm, W_B, O_B // W_B, s_len) + [pl.BlockSpec((3, 512), lambda i: (0, 0)), vec, vec],
        out_specs=pl.BlockSpec((tm, 512), lambda i: (i, 0)),
        compiler_params=_params("parallel"),
    )(proj, proj, proj, cw, cb, gb)


def _conv_backward(proj, cw, cb, gb, dy, tm):
    s_len = proj.shape[0]

    def body(p0, p1, p2, cw_ref, cb_ref, gb_ref, d0, d1, d2, db_ref, dcw_ref, dcb_ref, dgb_ref):
        acts, prm = _conv_inputs((p0, p1, p2), cw_ref, cb_ref, gb_ref)
        valid, core = _conv_masks(tm, s_len)
        _, vjp = jax.vjp(lambda a, p: _conv_tile(*a, *p, *prm, valid, core), acts, prm)
        dy_ext = jnp.where(valid, jnp.concatenate([d0[...], d1[...], d2[...]], axis=0), 0.0)
        dacts, dprm = vjp(dy_ext)
        for j in range(4):
            db_ref[:, j * 512:(j + 1) * 512] = dacts[j][HALO:HALO + tm].astype(BF16)
        first = pl.program_id(0) == 0
        _acc(dcw_ref, jnp.concatenate(dprm[0:3], axis=0), first)
        _acc(dcb_ref, dprm[3], first)
        _acc(dgb_ref, dprm[4], first)

    vec = pl.BlockSpec((1, 512), lambda i: (0, 0))
    mat = pl.BlockSpec((3, 512), lambda i: (0, 0))
    return pl.pallas_call(
        body, name="conv_backward",
        out_shape=(jax.ShapeDtypeStruct((s_len, PROJ_W), BF16), jax.ShapeDtypeStruct((3, 512), F32),
                   jax.ShapeDtypeStruct((1, 512), F32), jax.ShapeDtypeStruct((1, 512), F32)),
        grid=(s_len // tm,),
        in_specs=_halo_specs(tm, W_B, O_B // W_B, s_len) + [mat, vec, vec] + _halo_specs(tm, 512, 1, s_len),
        out_specs=(pl.BlockSpec((tm, W_B), lambda i: (i, O_B // W_B)), mat, vec, vec),
        compiler_params=_params("arbitrary"),
    )(proj, proj, proj, cw, cb, gb, dy, dy, dy)


def _cgate_forward(o, proj, gc, tm):
    s_len = o.shape[0]

    def body(o_ref, cz_ref, gc_ref, y_ref):
        y_ref[...] = _cgate_tile(o_ref[...], cz_ref[...].astype(F32), gc_ref[...]).astype(BF16)

    row = pl.BlockSpec((tm, W_CZ), lambda i: (i, 0))
    return pl.pallas_call(
        body, name="cgate_forward",
        out_shape=jax.ShapeDtypeStruct((s_len, W_CZ), BF16),
        grid=(s_len // tm,),
        in_specs=[row, pl.BlockSpec((tm, W_CZ), lambda i: (i, O_CZ // W_CZ)), pl.BlockSpec((1, W_CZ), lambda i: (0, 0))],
        out_specs=row,
        compiler_params=_params("parallel"),
    )(o, proj, gc)


def _cgate_backward(o, proj, gc, dy, dproj, tm, stat_chunk):
    s_len = o.shape[0]
    per_stat = stat_chunk // tm

    def body(o_ref, cz_ref, gc_ref, dy_ref, _, dcz_ref, do_ref, dsum_ref, dgc_ref):
        o = o_ref[...]
        _, vjp = jax.vjp(_cgate_tile, o, cz_ref[...].astype(F32), gc_ref[...])
        do, dcz, dgc = vjp(dy_ref[...])
        dcz_ref[...] = dcz.astype(BF16)
        do_ref[...] = do.astype(BF16)
        ones = jnp.ones((8, HEAD), F32)
        for h in range(C_HEADS):
            cols = slice(h * HEAD, (h + 1) * HEAD)
            sums = lax.dot_general(ones, do[:, cols] * o[:, cols], _NT, precision=lax.Precision.HIGHEST,
                                   preferred_element_type=F32)
            dsum_ref[h, 0] = sums[0:1]
        _acc(dgc_ref, dgc, pl.program_id(0) == 0)

    row = pl.BlockSpec((tm, W_CZ), lambda i: (i, 0))
    vec = pl.BlockSpec((1, W_CZ), lambda i: (0, 0))
    return pl.pallas_call(
        body, name="cgate_backward",
        out_shape=(jax.ShapeDtypeStruct(dproj.shape, BF16), jax.ShapeDtypeStruct((s_len, W_CZ), BF16),
                   jax.ShapeDtypeStruct((C_HEADS, s_len // stat_chunk, 1, stat_chunk), F32), jax.ShapeDtypeStruct((1, W_CZ), F32)),
        grid=(s_len // tm,),
        in_specs=[row, pl.BlockSpec((tm, W_CZ), lambda i: (i, O_CZ // W_CZ)), vec,
                  pl.BlockSpec((tm, W_CZ), lambda i: (i, 1)), pl.BlockSpec(memory_space=pl.ANY)],
        out_specs=(pl.BlockSpec((tm, W_CZ), lambda i: (i, O_CZ // W_CZ)), row,
                   pl.BlockSpec((C_HEADS, 1, 1, tm), lambda i: (0, i // per_stat, 0, i % per_stat)), vec),
        input_output_aliases={4: 0},
        compiler_params=_params("arbitrary"),
    )(o, proj, gc, dy, dproj)


def _mla_small_specs():
    return [pl.BlockSpec((KV_RANK, 2 * C_HEADS * HEAD), lambda i: (0, 0)), pl.BlockSpec((1, KV_RANK), lambda i: (0, 0)),
            pl.BlockSpec((1, HEAD), lambda i: (0, 0)), pl.BlockSpec((1, ROPE), lambda i: (0, 0)),
            pl.BlockSpec((1, HEAD), lambda i: (0, 0)), pl.BlockSpec((1, ROPE), lambda i: (0, 0))]


def _mla_load(m_ref, cos_ref, sin_ref):
    qn = m_ref[:, M_QN:M_QN + C_HEADS * HEAD].astype(F32)
    qr = m_ref[:, M_QR:M_QR + C_HEADS * ROPE].astype(F32)
    ckv = m_ref[:, M_CKV:M_CKV + KV_RANK].astype(F32)
    kr = m_ref[:, M_KR:M_KR + ROPE].astype(F32)
    return qn, qr, ckv, kr, cos_ref[...], sin_ref[...]


def _mla_forward(proj, cosf, sins, wukv, kvg, qng, qrg, kng, krg, tm, kt_chunk, vt_chunk):
    s_len = proj.shape[0]

    def body(m_ref, cos_ref, sin_ref, w_ref, kvg_ref, qng_ref, qrg_ref, kng_ref, krg_ref, q_ref, k_ref, v_ref, kt_ref, vt_ref):
        q, k, v = _mla_tile(*_mla_load(m_ref, cos_ref, sin_ref), w_ref[...], kvg_ref[...], qng_ref[...], qrg_ref[...],
                            kng_ref[...], krg_ref[...])
        for h in range(C_HEADS):
            q_ref[h] = q[h].astype(BF16)
            k_ref[h] = k[h].astype(BF16)
            v_ref[h] = v[h].astype(BF16)
            kt_ref[h, 0] = jnp.concatenate([k[h][:, :HEAD].T, k[h][:, HEAD:].T], axis=0).astype(BF16)
            vt_ref[h, 0] = v[h].T.astype(BF16)

    rope_spec = pl.BlockSpec((tm, ROPE), lambda i: (i, 0))
    qk_spec = pl.BlockSpec((C_HEADS, tm, QK), lambda i: (0, i, 0))
    per_k, per_v = kt_chunk // tm, vt_chunk // tm
    return pl.pallas_call(
        body, name="mla_forward",
        out_shape=(jax.ShapeDtypeStruct((C_HEADS, s_len, QK), BF16), jax.ShapeDtypeStruct((C_HEADS, s_len, QK), BF16),
                   jax.ShapeDtypeStruct((C_HEADS, s_len, HEAD), BF16),
                   jax.ShapeDtypeStruct((C_HEADS, s_len // kt_chunk, QK, kt_chunk), BF16),
                   jax.ShapeDtypeStruct((C_HEADS, s_len // vt_chunk, HEAD, vt_chunk), BF16)),
        grid=(s_len // tm,),
        in_specs=[pl.BlockSpec((tm, W_M), lambda i: (i, O_M // W_M)), rope_spec, rope_spec] + _mla_small_specs(),
        out_specs=(qk_spec, qk_spec, pl.BlockSpec((C_HEADS, tm, HEAD), lambda i: (0, i, 0)),
                   pl.BlockSpec((C_HEADS, 1, QK, tm), lambda i: (0, i // per_k, 0, i % per_k)),
                   pl.BlockSpec((C_HEADS, 1, HEAD, tm), lambda i: (0, i // per_v, 0, i % per_v))),
        compiler_params=_params("parallel"),
    )(proj, cosf, sins, wukv, kvg, qng, qrg, kng, krg)


def _mla_backward(proj, cosf, sins, wukv, kvg, qng, qrg, kng, krg, dqt, dk, dv, dproj, tm):
    s_len = proj.shape[0]
    per_chunk = dqt.shape[3] // tm

    def body(m_ref, cos_ref, sin_ref, w_ref, kvg_ref, qng_ref, qrg_ref, kng_ref, krg_ref, dq_ref, dk_ref, dv_ref, _,
             dm_ref, dw_ref, dkvg_ref, dqng_ref, dqrg_ref, dkng_ref, dkrg_ref):
        qn, qr, ckv, kr, cosf_t, sins_t = _mla_load(m_ref, cos_ref, sin_ref)
        prm = (w_ref[...], kvg_ref[...], qng_ref[...], qrg_ref[...], kng_ref[...], krg_ref[...])
        _, vjp = jax.vjp(lambda a, p: _mla_tile(*a, cosf_t, sins_t, *p), (qn, qr, ckv, kr), prm)
        heads = range(C_HEADS)
        dacts, dprm = vjp((tuple(dq_ref[h, 0].T for h in heads), tuple(dk_ref[h] for h in heads), tuple(dv_ref[h] for h in heads)))
        dm_ref[:, M_QN:M_QN + C_HEADS * HEAD] = dacts[0].astype(BF16)
        dm_ref[:, M_QR:M_QR + C_HEADS * ROPE] = dacts[1].astype(BF16)
        dm_ref[:, M_CKV:M_CKV + KV_RANK] = dacts[2].astype(BF16)
        pad = jnp.zeros((tm, W_M - M_KR - ROPE), F32)
        dm_ref[:, M_KR:W_M] = jnp.concatenate([dacts[3], pad], axis=-1).astype(BF16)
        first = pl.program_id(0) == 0
        for ref, val in zip((dw_ref, dkvg_ref, dqng_ref, dqrg_ref, dkng_ref, dkrg_ref), dprm):
            _acc(ref, val.astype(F32), first)

    rope_spec = pl.BlockSpec((tm, ROPE), lambda i: (i, 0))
    qk_spec = pl.BlockSpec((C_HEADS, tm, QK), lambda i: (0, i, 0))
    small = _mla_small_specs()
    return pl.pallas_call(
        body, name="mla_backward",
        out_shape=(jax.ShapeDtypeStruct(dproj.shape, BF16), jax.ShapeDtypeStruct((KV_RANK, 2 * C_HEADS * HEAD), F32),
                   jax.ShapeDtypeStruct((1, KV_RANK), F32), jax.ShapeDtypeStruct((1, HEAD), F32),
                   jax.ShapeDtypeStruct((1, ROPE), F32), jax.ShapeDtypeStruct((1, HEAD), F32),
                   jax.ShapeDtypeStruct((1, ROPE), F32)),
        grid=(s_len // tm,),
        in_specs=[pl.BlockSpec((tm, W_M), lambda i: (i, O_M // W_M)), rope_spec, rope_spec] + small
                 + [pl.BlockSpec((C_HEADS, 1, QK, tm), lambda i: (0, i // per_chunk, 0, i % per_chunk)), qk_spec,
                    pl.BlockSpec((C_HEADS, tm, HEAD), lambda i: (0, i, 0)), pl.BlockSpec(memory_space=pl.ANY)],
        out_specs=(pl.BlockSpec((tm, W_M), lambda i: (i, O_M // W_M)), *small),
        input_output_aliases={12: 0},
        compiler_params=_params("arbitrary"),
    )(proj, cosf, sins, wukv, kvg, qng, qrg, kng, krg, dqt, dk, dv, dproj)


def _attention_forward(q, k, vt, tq, stat_chunk, comm=None):
    n_heads, s_len, _ = q.shape
    n_chunks, _, ck = vt.shape[1:]
    c_ops, c_in_specs, c_shapes, c_out_specs, c_sems, c_begin, c_end = _riding_exchange(comm, 2)
    n_c = len(c_ops)

    def body(q_ref, k_ref, vt_ref, *rest):
        c_ins, (o_ref, lse_ref), c_outs, sems = rest[:n_c], rest[n_c:n_c + 2], rest[n_c + 2:2 * n_c + 2], rest[2 * n_c + 2:]
        c_begin(c_ins, c_outs, sems)
        q_t = q_ref[0]

        def step(j, carry):
            m_old, l_old, acc = carry
            k_j = k_ref[0, pl.ds(pl.multiple_of(j * ck, ck), ck), :]
            s = lax.dot_general(k_j, q_t, _NT, preferred_element_type=F32)
            m_new = jnp.maximum(m_old, jnp.max(s, axis=0, keepdims=True))
            p = jnp.exp2(s - m_new)
            alpha = jnp.exp2(m_old - m_new)
            l_new = alpha * l_old + jnp.sum(p, axis=0, keepdims=True)
            acc = alpha * acc + jnp.dot(vt_ref[0, j], p.astype(BF16), preferred_element_type=F32)
            return m_new, l_new, acc

        init = (jnp.full((1, tq), -jnp.inf, F32), jnp.zeros((1, tq), F32), jnp.zeros((HEAD, tq), F32))
        m_fin, l_fin, acc = lax.fori_loop(0, n_chunks, step, init)
        o_ref[...] = (acc / l_fin).T
        lse_ref[0, 0] = m_fin + jnp.log2(l_fin)
        c_end(c_ins, c_outs, sems)

    per_stat = stat_chunk // tq
    outs = pl.pallas_call(
        body, name="attention_forward",
        out_shape=(jax.ShapeDtypeStruct((s_len, n_heads * HEAD), F32),
                   jax.ShapeDtypeStruct((n_heads, s_len // stat_chunk, 1, stat_chunk), F32), *c_shapes),
        grid=(n_heads, s_len // tq),
        in_specs=[pl.BlockSpec((1, tq, QK), lambda h, i: (h, i, 0)), pl.BlockSpec((1, s_len, QK), lambda h, i: (h, 0, 0)),
                  pl.BlockSpec((1, n_chunks, HEAD, ck), lambda h, i: (h, 0, 0, 0))] + c_in_specs,
        out_specs=(pl.BlockSpec((tq, HEAD), lambda h, i: (i, h)),
                   pl.BlockSpec((1, 1, 1, tq), lambda h, i: (h, i // per_stat, 0, i % per_stat)), *c_out_specs),
        scratch_shapes=c_sems,
        compiler_params=_params("arbitrary", "arbitrary") if comm else _params("parallel", "parallel"),
    )(q, k, vt, *c_ops)
    return outs[0], outs[1], outs[2:]


def _attention_backward(q, k, kt, v, do, lse, dsum, comm=None):
    n_heads, s_len, _ = q.shape
    tk = kt.shape[3]
    n_q, _, cq = lse.shape[1:]
    c_ops, c_in_specs, c_shapes, c_out_specs, c_sems, c_begin, c_end = _riding_exchange(comm, 2)
    n_c = len(c_ops)

    def body(q_ref, k_ref, kt_ref, v_ref, do_ref, lse_ref, dsum_ref, *rest):
        c_ins, (dqt_ref, dk_ref, dv_ref), c_outs, sems = rest[:n_c], rest[n_c:n_c + 3], rest[n_c + 3:2 * n_c + 3], rest[2 * n_c + 3:]
        c_begin(c_ins, c_outs, sems)
        first = pl.program_id(1) == 0
        k_j, kt_j, v_j = k_ref[0], kt_ref[0, 0], v_ref[0]

        def step(i, carry):
            dk, dv = carry
            rows = pl.ds(pl.multiple_of(i * cq, cq), cq)
            q_i, do_i = q_ref[0, rows, :], do_ref[rows, :]
            s = lax.dot_general(k_j, q_i, _NT, preferred_element_type=F32)
            p = jnp.exp2(s - lse_ref[0, i])
            dp = lax.dot_general(v_j, do_i, _NT, preferred_element_type=F32)
            ds = (p * (dp - dsum_ref[0, i]) * LN_2).astype(BF16)
            dv = dv + jnp.dot(p.astype(BF16), do_i, preferred_element_type=F32)
            dk = dk + jnp.dot(ds, q_i, preferred_element_type=F32)
            _acc(dqt_ref.at[0, i], jnp.dot(kt_j, ds, preferred_element_type=F32), first)
            return dk, dv

        dk, dv = lax.fori_loop(0, n_q, step, (jnp.zeros((tk, QK), F32), jnp.zeros((tk, HEAD), F32)))
        dk_ref[0] = dk
        dv_ref[0] = dv
        c_end(c_ins, c_outs, sems)

    stat = pl.BlockSpec((1, n_q, 1, cq), lambda h, j: (h, 0, 0, 0))
    outs = pl.pallas_call(
        body, name="attention_backward",
        out_shape=(jax.ShapeDtypeStruct((n_heads, n_q, QK, cq), F32), jax.ShapeDtypeStruct((n_heads, s_len, QK), F32),
                   jax.ShapeDtypeStruct((n_heads, s_len, HEAD), F32), *c_shapes),
        grid=(n_heads, s_len // tk),
        in_specs=[pl.BlockSpec((1, s_len, QK), lambda h, j: (h, 0, 0)), pl.BlockSpec((1, tk, QK), lambda h, j: (h, j, 0)),
                  pl.BlockSpec((1, 1, QK, tk), lambda h, j: (h, j, 0, 0)),
                  pl.BlockSpec((1, tk, HEAD), lambda h, j: (h, j, 0)), pl.BlockSpec((s_len, HEAD), lambda h, j: (0, h)),
                  stat, stat] + c_in_specs,
        out_specs=(pl.BlockSpec((1, n_q, QK, cq), lambda h, j: (h, 0, 0, 0)), pl.BlockSpec((1, tk, QK), lambda h, j: (h, j, 0)),
                   pl.BlockSpec((1, tk, HEAD), lambda h, j: (h, j, 0)), *c_out_specs),
        scratch_shapes=c_sems,
        compiler_params=_params("arbitrary", "arbitrary") if comm else _params("parallel", "arbitrary"),
    )(q, k, kt, v, do, lse, dsum, *c_ops)
    return outs[0], outs[1], outs[2], outs[3:]


def _exchange(arrs, gather, name):
    n = len(arrs)

    def body(*refs):
        plan = _exchange_plan(refs[:n], refs[n:2 * n], gather, *refs[2 * n:])
        _exchange_start(plan)
        _exchange_wait(plan)

    any_spec = pl.BlockSpec(memory_space=pl.ANY)
    return pl.pallas_call(
        body, name=name,
        out_shape=_exchange_out_shapes(arrs, gather),
        in_specs=[any_spec] * n,
        out_specs=tuple([any_spec] * n),
        scratch_shapes=_exchange_semaphores(n),
        compiler_params=pltpu.CompilerParams(has_side_effects=True),
    )(*arrs)


def _exchange_out_shapes(arrs, gather):
    return tuple(jax.ShapeDtypeStruct((N_DEV, *(a.shape if g else a.shape[1:])), a.dtype) for a, g in zip(arrs, gather))


def _exchange_semaphores(n):
    n_remote = n * (N_DEV - 1)
    return [pltpu.SemaphoreType.DMA((n_remote,)), pltpu.SemaphoreType.DMA((n_remote,)), pltpu.SemaphoreType.DMA((n,))]


def _exchange_plan(ins, outs, gather, send_sems, recv_sems, local_sems):
    n = len(ins)
    x, y, c = lax.axis_index("x"), lax.axis_index("y"), lax.axis_index("c")
    me = 4 * x + 2 * y + c

    def block_for(a, dev):
        return ins[a] if gather[a] else ins[a].at[dev]

    local = [pltpu.make_async_copy(block_for(a, me), outs[a].at[me], local_sems.at[a]) for a in range(n)]
    remote = []
    for k in range(1, N_DEV):
        px = 1 - x if k & 4 else x
        py = 1 - y if k & 2 else y
        pc = 1 - c if k & 1 else c
        peer = 4 * px + 2 * py + pc
        for a in range(n):
            idx = a * (N_DEV - 1) + k - 1
            send = pltpu.make_async_remote_copy(
                src_ref=block_for(a, peer), dst_ref=outs[a].at[me], send_sem=send_sems.at[idx], recv_sem=recv_sems.at[idx],
                device_id=(px, py, pc), device_id_type=pl.DeviceIdType.MESH)
            arrive = pltpu.make_async_remote_copy(
                src_ref=block_for(a, peer), dst_ref=outs[a].at[peer], send_sem=send_sems.at[idx], recv_sem=recv_sems.at[idx],
                device_id=(px, py, pc), device_id_type=pl.DeviceIdType.MESH)
            remote.append((send, arrive))
    return local, remote


def _exchange_start(plan):
    local, remote = plan
    for cp in local:
        cp.start()
    for send, _ in remote:
        send.start()


def _exchange_wait(plan):
    local, remote = plan
    for send, arrive in remote:
        send.wait_send()
        arrive.wait_recv()
    for cp in local:
        cp.wait()


def _riding_exchange(comm, n_grid):
    if comm is None:
        return [], [], (), (), [], lambda *_: None, lambda *_: None
    arrs, gather = comm
    n = len(arrs)
    any_spec = pl.BlockSpec(memory_space=pl.ANY)

    def begin(ins, outs, sems):
        @pl.when(functools.reduce(jnp.logical_and, [pl.program_id(d) == 0 for d in range(n_grid)]))
        def _():
            _exchange_start(_exchange_plan(ins, outs, gather, *sems))

    def end(ins, outs, sems):
        @pl.when(functools.reduce(jnp.logical_and, [pl.program_id(d) == pl.num_programs(d) - 1 for d in range(n_grid)]))
        def _():
            _exchange_wait(_exchange_plan(ins, outs, gather, *sems))

    return (list(arrs), [any_spec] * n, _exchange_out_shapes(arrs, gather), tuple([any_spec] * n), _exchange_semaphores(n),
            begin, end)


ADAM_TILE_ELEMS = 256 * 1024


def _sum_adam(parts, w, m, v, layer, prev, name):
    n_parts, r, c = parts.shape
    tm = r
    while tm * c > ADAM_TILE_ELEMS and tm % 16 == 0:
        tm //= 2

    def body(p_ref, w_ref, m_ref, v_ref, *rest):
        g_ref, d_ref, nm_ref, nv_ref = rest[-4:]
        g = p_ref[0].astype(F32)
        for s in range(1, n_parts):
            g = g + p_ref[s].astype(F32)
        m_new = ADAM_B1 * m_ref[...] + (1.0 - ADAM_B1) * g
        v_new = ADAM_B2 * v_ref[...] + (1.0 - ADAM_B2) * (g * g)
        m_hat = m_new / (1.0 - ADAM_B1 ** ADAM_STEP)
        v_hat = v_new / (1.0 - ADAM_B2 ** ADAM_STEP)
        g_ref[...] = g
        d_ref[...] = -ADAM_LR * (m_hat / (jnp.sqrt(v_hat) + ADAM_EPS) + ADAM_WD * w_ref[...])
        nm_ref[...] = m_new
        nv_ref[...] = v_new

    slab = pl.BlockSpec((None, tm, c), lambda i: (layer, i, 0))
    n_prev = 0 if prev is None else 4
    return pl.pallas_call(
        body, name=name,
        out_shape=(jax.ShapeDtypeStruct(w.shape, F32),) * 4,
        grid=(r // tm,),
        in_specs=[pl.BlockSpec((n_parts, tm, c), lambda i: (0, i, 0)), slab, slab, slab]
                 + [pl.BlockSpec(memory_space=pl.ANY)] * n_prev,
        out_specs=(slab, slab, slab, slab),
        input_output_aliases={4 + j: j for j in range(n_prev)},
        compiler_params=_params("parallel"),
    )(parts, w, m, v, *(prev or ()))


def _permute_in(w):
    k = w.shape[0]
    q = w[:, 3584:5120].reshape(k, C_HEADS, QK)
    return jnp.concatenate(
        [w[:, 1536:3584], w[:, 5696:6720], w[:, 0:1536], q[:, :, :HEAD].reshape(k, C_HEADS * HEAD),
         q[:, :, HEAD:].reshape(k, C_HEADS * ROPE), w[:, 5120:5632], w[:, 5632:5696],
         jnp.zeros((k, PROJ_W - IN_WIDTH), w.dtype)], axis=1)


def _unpermute_in(g):
    k = g.shape[0]
    qn = g[:, O_M + M_QN:O_M + M_QR].reshape(k, C_HEADS, HEAD)
    qr = g[:, O_M + M_QR:O_M + M_CKV].reshape(k, C_HEADS, ROPE)
    q = jnp.concatenate([qn, qr], axis=-1).reshape(k, C_HEADS * QK)
    return jnp.concatenate(
        [g[:, O_A:O_A + W_A], g[:, O_B:O_B + W_B], q, g[:, O_M + M_CKV:O_M + M_KR],
         g[:, O_M + M_KR:O_M + M_KR + ROPE], g[:, O_CZ:O_CZ + W_CZ]], axis=1)


SMALL = ("attn_norm", "sgu_norm", "w_spatial", "b_spatial", "conv_b", "kv_norm", "q_nope_norm", "q_rope_norm",
         "k_nope_norm", "k_rope_norm", "out_norm", "ple_norm")
PACK_ROWS = 256


def _pack(tensors):
    flat = jnp.concatenate([t.reshape(-1) for t in tensors])
    rows = -(-flat.shape[0] // (128 * PACK_ROWS)) * PACK_ROWS
    return jnp.pad(flat, (0, rows * 128 - flat.shape[0])).reshape(rows, 128)


def _unpack(packed, like):
    flat = packed.reshape(-1)
    out, pos = [], 0
    for t in like:
        out.append(flat[pos:pos + t.size].reshape(t.shape))
        pos += t.size
    return out


def _tile(s_len, want):
    return min(want, s_len)


ATT_FWD_QUERIES = 512
ATT_FWD_KEYS = 8192
ATT_BWD_KEYS = 512
ATT_BWD_QUERIES = 2048


def _layer_forward(h, p_l, cosf, sins, w, sm, comm, comm_rest):
    s_len = h.shape[0]
    tm = _tile(s_len, 512)
    proj, hn, rest = _norm_matmul(h, sm["attn_norm"], w["w_in"], tm, 768, comm_rest)
    if comm_rest is not None:
        w = {**w, **_assemble_rest(rest)}
    ga, gb, gc = sm["out_norm"][:, 0:512], sm["out_norm"][:, 512:1024], sm["out_norm"][:, 1024:2048]
    ya = _sgu_forward(proj, sm["sgu_norm"], sm["w_spatial"], sm["b_spatial"], ga, _tile(s_len, 256))
    yb = _conv_forward(proj, w["conv_w"], sm["conv_b"], gb, _tile(s_len, 256))
    q, k, v, kt, vt = _mla_forward(proj, cosf, sins, w["w_ukv"], sm["kv_norm"], sm["q_nope_norm"], sm["q_rope_norm"],
                                   sm["k_nope_norm"], sm["k_rope_norm"], _tile(s_len, 256), _tile(s_len, ATT_BWD_KEYS),
                                   _tile(s_len, ATT_FWD_KEYS))
    o, lse, arrived = _attention_forward(q, k, vt, _tile(s_len, ATT_FWD_QUERIES), _tile(s_len, ATT_BWD_QUERIES), comm)
    yc = _cgate_forward(o, proj, gc, _tile(s_len, 256))
    h1 = _out_matmul(h, ya, yb, yc, w["w_out"], _tile(s_len, 1024), 1024)
    h2, n1, gate, pp = _ple_forward(h1, sm["ple_norm"], p_l, w["w_ple_gate"], w["w_ple_proj"], tm, 1024)
    saved = dict(h=h, hn=hn, proj=proj, ya=ya, yb=yb, yc=yc, q=q, k=k, v=v, kt=kt, o=o, lse=lse, h1=h1, n1=n1, gate=gate,
                 pp=pp)
    return h2, saved, w, arrived


def _layer_backward(dh2, p_l, cosf, sins, w, sm, sv, comm, scatter_own):
    s_len = dh2.shape[0]
    tm = _tile(s_len, 512)
    tr = _tile(s_len, 256)
    big, small = {}, {}
    dgp, dpp = _ple_backward_gate(dh2, sv["gate"], sv["pp"], tm)
    big["w_ple_proj"] = _matmul_tn(p_l, dpp, _tile(s_len, 1024), PLE_DIM, 1024, "grad_w_ple_proj")
    big["w_ple_gate"] = _matmul_tn(sv["n1"], dgp, _tile(s_len, 1024), 512, 1024, "grad_w_ple_gate")
    dn1, _ = _matmul_nt(dgp, w["w_ple_gate"], tm, 1024, "grad_ple_norm_in")
    dh1, small["ple_norm"] = _rms_backward(sv["h1"], sm["ple_norm"], dn1, dh2, tr, "ple_norm_backward")
    dy, _ = _matmul_nt(dh1, w["w_out"], tm, 1024, "grad_branches")
    big["w_out"] = jnp.concatenate(
        [_matmul_tn(sv["ya"], dh1, _tile(s_len, 1024), 512, 1024, "grad_w_out_a"),
         _matmul_tn(sv["yb"], dh1, _tile(s_len, 1024), 512, 1024, "grad_w_out_b"),
         _matmul_tn(sv["yc"], dh1, _tile(s_len, 1024), 512, 1024, "grad_w_out_c")], axis=0)
    ga, gb, gc = sm["out_norm"][:, 0:512], sm["out_norm"][:, 512:1024], sm["out_norm"][:, 1024:2048]
    dproj, dcw, small["conv_b"], dgb = _conv_backward(sv["proj"], w["conv_w"], sm["conv_b"], gb, dy, tr)
    big["conv_w"] = dcw
    dproj, do, dsum, dgc = _cgate_backward(sv["o"], sv["proj"], gc, dy, dproj, tr, _tile(s_len, ATT_BWD_QUERIES))
    dqt, dk, dv, arrived = _attention_backward(sv["q"], sv["k"], sv["kt"], sv["v"], do, sv["lse"], dsum, comm)
    (dproj, big["w_ukv"], small["kv_norm"], small["q_nope_norm"], small["q_rope_norm"], small["k_nope_norm"],
     small["k_rope_norm"]) = _mla_backward(sv["proj"], cosf, sins, w["w_ukv"], sm["kv_norm"], sm["q_nope_norm"],
                                            sm["q_rope_norm"], sm["k_nope_norm"], sm["k_rope_norm"], dqt, dk, dv, dproj, tr)
    dproj, small["sgu_norm"], small["w_spatial"], small["b_spatial"], dga = _sgu_backward(
        sv["proj"], sm["sgu_norm"], sm["w_spatial"], sm["b_spatial"], ga, dy, dproj, tr)
    small["out_norm"] = jnp.concatenate([dga, dgb, dgc], axis=1)
    big["w_in"] = _matmul_tn(sv["hn"], dproj, _tile(s_len, 1024), 512, 2304, "grad_w_in")
    parts = _layer_parts(big)
    dhn, arrived_own = _matmul_nt(dproj, w["w_in"], tm, 512, "grad_attn_norm_in",
                                  (parts, [False] * len(parts)) if scatter_own else None)
    dh, small["attn_norm"] = _rms_backward(sv["h"], sm["attn_norm"], dhn, dh1, tr, "attn_norm_backward")
    return dh, parts, big["conv_w"], small, arrived, arrived_own if scatter_own else None


def _layer_small(params, layer):
    return dict(
        attn_norm=params["attn_norm"][layer][None, :], sgu_norm=params["sgu_norm"][layer],
        w_spatial=params["w_spatial"][layer], b_spatial=params["b_spatial"][layer][:, :, None],
        conv_b=params["conv_b"][layer][None, :], kv_norm=params["kv_norm"][layer][None, :],
        q_nope_norm=params["q_nope_norm"][layer][None, :], q_rope_norm=params["q_rope_norm"][layer][None, :],
        k_nope_norm=params["k_nope_norm"][layer][None, :], k_rope_norm=params["k_rope_norm"][layer][None, :],
        out_norm=params["out_norm"][layer][None, :], ple_norm=params["ple_norm"][layer][None, :])


BIG = ("w_in", "w_ukv", "w_out", "w_ple_gate", "w_ple_proj")


def _assemble_w_in(g_in):
    return _permute_in(g_in.transpose(1, 0, 2).reshape(g_in.shape[1], IN_WIDTH))


def _assemble_rest(gathered):
    g_ukv, g_out, g_gate, g_proj = gathered
    w_ukv = g_ukv.reshape(N_DEV, KV_RANK, 2, HEAD).transpose(1, 2, 0, 3).reshape(KV_RANK, 2 * C_HEADS * HEAD)
    return dict(w_ukv=w_ukv, w_out=g_out.reshape(D_MODEL, D_MODEL), w_ple_gate=g_gate.reshape(D_MODEL, D_MODEL),
                w_ple_proj=g_proj.transpose(1, 0, 2).reshape(PLE_DIM, D_MODEL))


def _layer_parts(big):
    g_in = big["w_in"]
    return [
        _unpermute_in(g_in).reshape(g_in.shape[0], N_DEV, -1).transpose(1, 0, 2).astype(BF16),
        big["w_ukv"].reshape(KV_RANK, 2, N_DEV, HEAD).transpose(2, 0, 1, 3).reshape(N_DEV, KV_RANK, 2 * HEAD).astype(BF16),
        big["w_out"].reshape(N_DEV, -1, D_MODEL).astype(BF16),
        big["w_ple_gate"].reshape(N_DEV, -1, D_MODEL).astype(BF16),
        big["w_ple_proj"].reshape(PLE_DIM, N_DEV, -1).transpose(1, 0, 2).astype(BF16)]


def _step_local(xs, ps, pos, target, shards, conv_w, params):
    inv = 1.0 / (ROPE_BASE ** (jnp.arange(0, ROPE, 2, dtype=F32) / ROPE))
    ang = pos.astype(F32)[:, None] * inv
    cos, sin = jnp.cos(ang), jnp.sin(ang)
    cosf = jnp.concatenate([cos, cos], axis=-1)
    sins = jnp.concatenate([-sin, sin], axis=-1)

    def gather_of(names, layer):
        return [shards[n][layer] for n in names], [True] * len(names)

    h = xs
    saved, weights = [], []
    smalls = [_layer_small(params, layer) for layer in range(DEPTH)]
    (first_w_in,) = _exchange(*gather_of(BIG[:1], 0), "gather_first_w_in")
    w = dict(w_in=_assemble_w_in(first_w_in), conv_w=conv_w[0])
    for layer in range(DEPTH):
        comm = gather_of(BIG, layer + 1) if layer + 1 < DEPTH else None
        comm_rest = gather_of(BIG[1:], 0) if layer == 0 else None
        h, sv, w, arrived = _layer_forward(h, ps[layer], cosf, sins, w, smalls[layer], comm, comm_rest)
        saved.append(sv)
        weights.append(w)
        if comm is not None:
            w = dict(w_in=_assemble_w_in(arrived[0]), conv_w=conv_w[layer + 1], **_assemble_rest(arrived[1:]))
    dh, loss = _loss_grad(h, target, _tile(h.shape[0], 512))
    received, conv_grads, small_grads = [None] * DEPTH, [None] * DEPTH, [None] * DEPTH
    comm = None
    for layer in reversed(range(DEPTH)):
        dh, parts, conv_grads[layer], small_grads[layer], arrived, arrived_own = _layer_backward(
            dh, ps[layer], cosf, sins, weights[layer], smalls[layer], saved[layer], comm, layer == 0)
        if comm is not None:
            received[layer + 1] = arrived
        comm = (parts, [False] * len(parts))
    received[0] = arrived_own
    return loss, dh, received, conv_grads, small_grads


def kernel(x, p, positions, attn_norm, w_in, sgu_norm, w_spatial, b_spatial, conv_w, conv_b, kv_norm, w_ukv, q_nope_norm, q_rope_norm, k_nope_norm, k_rope_norm, out_norm, w_out, ple_norm, w_ple_gate, w_ple_proj, loss_target, m_attn_norm, m_w_in, m_sgu_norm, m_w_spatial, m_b_spatial, m_conv_w, m_conv_b, m_kv_norm, m_w_ukv, m_q_nope_norm, m_q_rope_norm, m_k_nope_norm, m_k_rope_norm, m_out_norm, m_w_out, m_ple_norm, m_w_ple_gate, m_w_ple_proj, v_attn_norm, v_w_in, v_sgu_norm, v_w_spatial, v_b_spatial, v_conv_w, v_conv_b, v_kv_norm, v_w_ukv, v_q_nope_norm, v_q_rope_norm, v_k_nope_norm, v_k_rope_norm, v_out_norm, v_w_out, v_ple_norm, v_w_ple_gate, v_w_ple_proj):
    order = ("attn_norm", "w_in", "sgu_norm", "w_spatial", "b_spatial", "conv_w", "conv_b", "kv_norm", "w_ukv",
             "q_nope_norm", "q_rope_norm", "k_nope_norm", "k_rope_norm", "out_norm", "w_out", "ple_norm", "w_ple_gate",
             "w_ple_proj")
    wts = dict(zip(order, (attn_norm, w_in, sgu_norm, w_spatial, b_spatial, conv_w, conv_b, kv_norm, w_ukv, q_nope_norm,
                           q_rope_norm, k_nope_norm, k_rope_norm, out_norm, w_out, ple_norm, w_ple_gate, w_ple_proj)))
    mom = dict(zip(order, (m_attn_norm, m_w_in, m_sgu_norm, m_w_spatial, m_b_spatial, m_conv_w, m_conv_b, m_kv_norm, m_w_ukv,
                           m_q_nope_norm, m_q_rope_norm, m_k_nope_norm, m_k_rope_norm, m_out_norm, m_w_out, m_ple_norm,
                           m_w_ple_gate, m_w_ple_proj)))
    var = dict(zip(order, (v_attn_norm, v_w_in, v_sgu_norm, v_w_spatial, v_b_spatial, v_conv_w, v_conv_b, v_kv_norm, v_w_ukv,
                           v_q_nope_norm, v_q_rope_norm, v_k_nope_norm, v_k_rope_norm, v_out_norm, v_w_out, v_ple_norm,
                           v_w_ple_gate, v_w_ple_proj)))

    conv_shard = wts["conv_w"]
    (conv_all,) = _exchange([conv_shard.reshape(-1, 128)], [True], "gather_conv_w")
    conv_full = conv_all.reshape(N_DEV, DEPTH, 3, -1).transpose(1, 2, 0, 3).reshape(DEPTH, 3, -1)
    shards = {n: wts[n].astype(BF16) for n in BIG}
    loss_part, grad_x, received, conv_grads, small_grads = _step_local(
        x[0], p[:, 0], positions[0], loss_target[0], shards, conv_full, wts)
    loss = lax.psum(loss_part[0, 0], ("x", "y", "c"))

    def small_grad(name):
        g = jnp.stack([sg[name] for sg in small_grads])
        return g.reshape(wts[name].shape)

    conv_grad = jnp.stack(conv_grads)
    like = [wts[n] for n in SMALL] + [conv_grad]
    packed = _pack([small_grad(n) for n in SMALL] + [conv_grad])
    (small_parts,) = _exchange([packed], [True], "gather_small_grads")
    filler = [jnp.zeros_like(conv_grad), jnp.zeros_like(conv_grad), jnp.ones_like(conv_grad)]
    small_out = _sum_adam(small_parts, *(_pack([src[n] for n in SMALL] + [fill])[None] for src, fill in zip((wts, mom, var), filler)),
                          0, None, "adam_small")
    unpacked = [_unpack(o[0], like) for o in small_out]
    results = {n: vals for n, vals in zip(SMALL, zip(*[u[:-1] for u in unpacked]))}
    me = 4 * lax.axis_index("x") + 2 * lax.axis_index("y") + lax.axis_index("c")
    width = conv_shard.shape[2]
    conv_local = lax.dynamic_slice_in_dim(unpacked[0][-1], me * width, width, axis=2)
    as_slab = (lambda t: t.reshape(1, -1, width))
    conv_out = _sum_adam(as_slab(conv_local), as_slab(conv_shard), as_slab(mom["conv_w"]), as_slab(var["conv_w"]), 0, None,
                         "adam_conv_w")
    results["conv_w"] = tuple(o.reshape(conv_shard.shape) for o in conv_out)

    for j, name in enumerate(BIG):
        outs = None
        for layer in range(DEPTH):
            outs = _sum_adam(received[layer][j], wts[name], mom[name], var[name], layer, outs, "adam_" + name)
        results[name] = outs

    grads, deltas, new_m, new_v = ([results[n][j] for n in order] for j in range(4))
    return (loss, grad_x[None], *grads, *deltas, *new_m, *new_v)
```

```python
import functools

import jax
import jax.numpy as jnp
from jax import lax
from jax.experimental import pallas as pl
from jax.experimental.pallas import tpu as pltpu

F32 = jnp.float32
BF16 = jnp.bfloat16

N_DEV = 8
DEPTH = 4
D_MODEL = 2048
EPS = 1e-6
CHUNK = 128
A_HEADS = 4
HEAD = 128
ROPE = 64
HALF = ROPE // 2
C_HEADS = 8
KV_RANK = 512
PLE_DIM = 256
ROPE_BASE = 10000.0
IN_WIDTH = 6720
QK = HEAD + ROPE
SCALE = QK ** -0.5
LOG2_E = 1.4426950408889634
LN_2 = 0.6931471805599453
Q_SCALE = SCALE * LOG2_E
HALO = 8

O_B = 0
W_B = 2048
O_CZ = 2048
W_CZ = 1024
O_A = 3072
W_A = 1536
O_M = 4608
W_M = 2304
M_QN, M_QR, M_CKV, M_KR = 0, 1024, 1536, 2048
PROJ_W = 6912

ADAM_LR = 0.001
ADAM_B1 = 0.9
ADAM_B2 = 0.999
ADAM_EPS = 1e-08
ADAM_WD = 0.01
ADAM_STEP = 10

VMEM_LIMIT = 56 * 1024 * 1024

_NT = (((1,), (1,)), ((), ()))
_TN = (((0,), (0,)), ((), ()))


def _params(*sem):
    return pltpu.CompilerParams(dimension_semantics=sem, vmem_limit_bytes=VMEM_LIMIT)


@jax.custom_vjp
def _bdot(a, b):
    return jnp.dot(a.astype(BF16), b.astype(BF16), preferred_element_type=F32)


def _bdot_fwd(a, b):
    return _bdot(a, b), (a, b)


def _bdot_bwd(res, g):
    a, b = res
    gb = g.astype(BF16)
    da = lax.dot_general(gb, b.astype(BF16), _NT, preferred_element_type=F32)
    db = lax.dot_general(a.astype(BF16), gb, _TN, preferred_element_type=F32)
    return da.astype(a.dtype), db.astype(b.dtype)


_bdot.defvjp(_bdot_fwd, _bdot_bwd)


@functools.partial(jax.custom_vjp, nondiff_argnums=(1,))
def _split(x, n):
    w = x.shape[-1] // n
    return tuple(x[:, i * w:(i + 1) * w] for i in range(n))


def _split_fwd(x, n):
    return _split(x, n), None


def _split_bwd(n, _, gs):
    return (jnp.concatenate(gs, axis=-1),)


_split.defvjp(_split_fwd, _split_bwd)


@functools.partial(jax.custom_vjp, nondiff_argnums=(1,))
def _shift_rows(x, k):
    return pltpu.roll(x, k % x.shape[0], 0)


def _shift_rows_fwd(x, k):
    return _shift_rows(x, k), None


def _shift_rows_bwd(k, _, g):
    return (_shift_rows(g, -k),)


_shift_rows.defvjp(_shift_rows_fwd, _shift_rows_bwd)


@jax.custom_vjp
def _swap_halves(x):
    h = x.shape[-1] // 2
    return jnp.concatenate([x[:, h:], x[:, :h]], axis=-1)


def _swap_halves_fwd(x):
    return _swap_halves(x), None


def _swap_halves_bwd(_, g):
    return (_swap_halves(g),)


_swap_halves.defvjp(_swap_halves_fwd, _swap_halves_bwd)


def _rms(x, g):
    return x * lax.rsqrt(jnp.mean(x * x, axis=-1, keepdims=True) + EPS) * g


def _rope(x, cosf, sins):
    return x * cosf + _swap_halves(x) * sins


def _sgu_chunk(u, v, z, gain, ws, bs, ga):
    ys = []
    for h in range(A_HEADS):
        vn = _rms(v[h], gain[h])
        s = _bdot(ws[h], vn) + bs[h]
        ys.append(u[h] * s * jax.nn.silu(z[h]))
    ss = sum(jnp.sum(y * y, axis=-1, keepdims=True) for y in ys) * (1.0 / (A_HEADS * HEAD))
    r = lax.rsqrt(ss + EPS)
    return tuple(ys[h] * r * ga[h] for h in range(A_HEADS))


def _conv_tile(bb, bc, bh, bz, w0, w1, w2, cb, gb, w0h, w1h, w2h, cbh, gbh, valid, core):
    t = jnp.where(valid, bc * bh, 0.0)
    y = (jnp.where(core, cb, cbh)
         + _shift_rows(t, 1) * jnp.where(core, w0, w0h)
         + t * jnp.where(core, w1, w1h)
         + _shift_rows(t, -1) * jnp.where(core, w2, w2h))
    return _rms(bb * y * jax.nn.silu(bz), jnp.where(core, gb, gbh))


def _cgate_tile(o, cz, gc):
    return _rms(o * jax.nn.silu(cz), gc)


def _mla_tile(qn, qr, ckv, kr, cosf, sins, wukv, kvg, qng, qrg, kng, krg):
    kv = _split(_bdot(_rms(ckv, kvg), wukv), 2 * C_HEADS)
    k_r = _rope(_rms(kr, krg), cosf, sins)
    qn_h = _split(qn, C_HEADS)
    qr_h = _split(qr, C_HEADS)
    q, k, v = [], [], []
    for h in range(C_HEADS):
        q.append(jnp.concatenate([_rms(qn_h[h], qng), _rope(_rms(qr_h[h], qrg), cosf, sins)], axis=-1) * Q_SCALE)
        k.append(jnp.concatenate([_rms(kv[h], kng), k_r], axis=-1))
        v.append(kv[C_HEADS + h])
    return tuple(q), tuple(k), tuple(v)


def _norm_matmul(h, gain, w, tm, tn, comm=None):
    s_len, k = h.shape
    n = w.shape[1]
    c_ops, c_in_specs, c_shapes, c_out_specs, c_sems, c_begin, c_end = _riding_exchange(comm, 2)
    n_c = len(c_ops)

    def body(h_ref, g_ref, w_ref, *rest):
        c_ins, (o_ref, hn_ref), c_outs, sems = rest[:n_c], rest[n_c:n_c + 2], rest[n_c + 2:2 * n_c + 2], rest[2 * n_c + 2:]
        c_begin(c_ins, c_outs, sems)

        @pl.when(pl.program_id(1) == 0)
        def _():
            hn_ref[...] = _rms(h_ref[...], g_ref[...]).astype(BF16)

        o_ref[...] = jnp.dot(hn_ref[...], w_ref[...], preferred_element_type=F32).astype(BF16)
        c_end(c_ins, c_outs, sems)

    outs = pl.pallas_call(
        body, name="norm_matmul",
        out_shape=(jax.ShapeDtypeStruct((s_len, n), BF16), jax.ShapeDtypeStruct((s_len, k), BF16), *c_shapes),
        grid=(s_len // tm, n // tn),
        in_specs=[pl.BlockSpec((tm, k), lambda i, j: (i, 0)), pl.BlockSpec((1, k), lambda i, j: (0, 0)),
                  pl.BlockSpec((k, tn), lambda i, j: (0, j))] + c_in_specs,
        out_specs=(pl.BlockSpec((tm, tn), lambda i, j: (i, j)), pl.BlockSpec((tm, k), lambda i, j: (i, 0)), *c_out_specs),
        scratch_shapes=c_sems,
        compiler_params=_params("arbitrary", "arbitrary") if comm else _params("parallel", "arbitrary"),
    )(h, gain, w, *c_ops)
    return outs[0], outs[1], outs[2:]


def _out_matmul(h, y, w, tm, tn):
    s_len, n = h.shape
    k = y.shape[1]

    def body(h_ref, y_ref, w_ref, o_ref):
        o_ref[...] = h_ref[...] + jnp.dot(y_ref[...], w_ref[...], preferred_element_type=F32)

    return pl.pallas_call(
        body, name="out_matmul",
        out_shape=jax.ShapeDtypeStruct((s_len, n), F32),
        grid=(s_len // tm, n // tn),
        in_specs=[pl.BlockSpec((tm, tn), lambda i, j: (i, j)), pl.BlockSpec((tm, k), lambda i, j: (i, 0)),
                  pl.BlockSpec((k, tn), lambda i, j: (0, j))],
        out_specs=pl.BlockSpec((tm, tn), lambda i, j: (i, j)),
        compiler_params=_params("parallel", "parallel"),
    )(h, y, w)


def _ple_forward(h1, gain, p, wg, wp, tm, tn):
    s_len, d = h1.shape
    kp = p.shape[1]

    def body(hrow_ref, hcol_ref, g_ref, p_ref, wg_ref, wp_ref, o_ref, n1_ref, gate_ref, pp_ref):
        @pl.when(pl.program_id(1) == 0)
        def _():
            n1_ref[...] = _rms(hrow_ref[...], g_ref[...]).astype(BF16)

        gate = jax.nn.sigmoid(jnp.dot(n1_ref[...], wg_ref[...], preferred_element_type=F32))
        pp = jnp.dot(p_ref[...].astype(BF16), wp_ref[...], preferred_element_type=F32)
        o_ref[...] = hcol_ref[...] + gate * pp
        gate_ref[...] = gate.astype(BF16)
        pp_ref[...] = pp.astype(BF16)

    col = pl.BlockSpec((tm, tn), lambda i, j: (i, j))
    return pl.pallas_call(
        body, name="ple_forward",
        out_shape=(jax.ShapeDtypeStruct((s_len, d), F32), jax.ShapeDtypeStruct((s_len, d), BF16),
                   jax.ShapeDtypeStruct((s_len, d), BF16), jax.ShapeDtypeStruct((s_len, d), BF16)),
        grid=(s_len // tm, d // tn),
        in_specs=[pl.BlockSpec((tm, d), lambda i, j: (i, 0)), col, pl.BlockSpec((1, d), lambda i, j: (0, 0)),
                  pl.BlockSpec((tm, kp), lambda i, j: (i, 0)), pl.BlockSpec((d, tn), lambda i, j: (0, j)),
                  pl.BlockSpec((kp, tn), lambda i, j: (0, j))],
        out_specs=(col, pl.BlockSpec((tm, d), lambda i, j: (i, 0)), col, col),
        compiler_params=_params("parallel", "arbitrary"),
    )(h1, h1, gain, p, wg, wp)


def _matmul_nt(a, b, tm, tk, name, comm=None):
    m, n = a.shape
    k = b.shape[0]
    c_ops, c_in_specs, c_shapes, c_out_specs, c_sems, c_begin, c_end = _riding_exchange(comm, 2)
    n_c = len(c_ops)

    def body(a_ref, b_ref, *rest):
        c_ins, o_ref, c_outs, sems = rest[:n_c], rest[n_c], rest[n_c + 1:2 * n_c + 1], rest[2 * n_c + 1:]
        c_begin(c_ins, c_outs, sems)
        o_ref[...] = lax.dot_general(a_ref[...].astype(BF16), b_ref[...].astype(BF16), _NT, preferred_element_type=F32)
        c_end(c_ins, c_outs, sems)

    outs = pl.pallas_call(
        body, name=name,
        out_shape=(jax.ShapeDtypeStruct((m, k), F32), *c_shapes),
        grid=(m // tm, k // tk),
        in_specs=[pl.BlockSpec((tm, n), lambda i, j: (i, 0)), pl.BlockSpec((tk, n), lambda i, j: (j, 0))] + c_in_specs,
        out_specs=(pl.BlockSpec((tm, tk), lambda i, j: (i, j)), *c_out_specs),
        scratch_shapes=c_sems,
        compiler_params=_params("arbitrary", "arbitrary") if comm else _params("parallel", "parallel"),
    )(a, b, *c_ops)
    return outs[0], outs[1:]


def _matmul_nt_rms_backward(a, b, x, gain, dres, tm, name):
    m, n = a.shape
    k = b.shape[0]

    def body(a_ref, b_ref, x_ref, g_ref, dres_ref, dx_ref, dg_ref):
        dn = lax.dot_general(a_ref[...].astype(BF16), b_ref[...], _NT, preferred_element_type=F32)
        _, vjp = jax.vjp(_rms, x_ref[...], g_ref[...])
        dx, dg = vjp(dn)
        dx_ref[...] = dres_ref[...] + dx
        _acc(dg_ref, dg, pl.program_id(0) == 0)

    row = pl.BlockSpec((tm, k), lambda i: (i, 0))
    vec = pl.BlockSpec((1, k), lambda i: (0, 0))
    return pl.pallas_call(
        body, name=name,
        out_shape=(jax.ShapeDtypeStruct((m, k), F32), jax.ShapeDtypeStruct((1, k), F32)),
        grid=(m // tm,),
        in_specs=[pl.BlockSpec((tm, n), lambda i: (i, 0)), pl.BlockSpec((k, n), lambda i: (0, 0)), row, vec, row],
        out_specs=(row, vec),
        compiler_params=_params("arbitrary"),
    )(a, b, x, gain, dres)


def _matmul_tn(a, b, tm, tk, tn, name):
    m, k = a.shape
    n = b.shape[1]
    n_m = m // tm

    def body(a_ref, b_ref, o_ref, acc_ref):
        part = lax.dot_general(a_ref[...].astype(BF16), b_ref[...].astype(BF16), _TN, preferred_element_type=F32)
        _acc(acc_ref, part, pl.program_id(2) == 0)

        @pl.when(pl.program_id(2) == n_m - 1)
        def _():
            o_ref[...] = acc_ref[...].astype(BF16)

    return pl.pallas_call(
        body, name=name,
        out_shape=jax.ShapeDtypeStruct((k, n), BF16),
        grid=(k // tk, n // tn, n_m),
        in_specs=[pl.BlockSpec((tm, tk), lambda kk, nn, mm: (mm, kk)), pl.BlockSpec((tm, tn), lambda kk, nn, mm: (mm, nn))],
        out_specs=pl.BlockSpec((tk, tn), lambda kk, nn, mm: (kk, nn)),
        scratch_shapes=[pltpu.VMEM((tk, tn), F32)],
        compiler_params=_params("parallel", "parallel", "arbitrary"),
    )(a, b)


def _acc(ref, val, first):
    @pl.when(first)
    def _():
        ref[...] = val

    @pl.when(jnp.logical_not(first))
    def _():
        ref[...] += val


def _loss_grad(h, target, tm):
    s_len, d = h.shape

    def body(h_ref, t_ref, dh_ref, loss_ref):
        e = h_ref[...] - t_ref[...]
        dh_ref[...] = e * (1.0 / d)
        part = jnp.sum(jnp.sum(e * e, axis=-1, keepdims=True), axis=0, keepdims=True) * (0.5 / d)
        _acc(loss_ref, jnp.broadcast_to(part, loss_ref.shape), pl.program_id(0) == 0)

    row = pl.BlockSpec((tm, d), lambda i: (i, 0))
    return pl.pallas_call(
        body, name="loss_grad",
        out_shape=(jax.ShapeDtypeStruct((s_len, d), F32), jax.ShapeDtypeStruct((1, 128), F32)),
        grid=(s_len // tm,),
        in_specs=[row, row],
        out_specs=(row, pl.BlockSpec((1, 128), lambda i: (0, 0))),
        compiler_params=_params("arbitrary"),
    )(h, target)


def _rms_backward(x, gain, dn, dres, tm, name):
    s_len, d = x.shape

    def body(x_ref, g_ref, dn_ref, dres_ref, dx_ref, dg_ref):
        _, vjp = jax.vjp(_rms, x_ref[...], g_ref[...])
        dx, dg = vjp(dn_ref[...])
        dx_ref[...] = dres_ref[...] + dx
        _acc(dg_ref, dg, pl.program_id(0) == 0)

    row = pl.BlockSpec((tm, d), lambda i: (i, 0))
    vec = pl.BlockSpec((1, d), lambda i: (0, 0))
    return pl.pallas_call(
        body, name=name,
        out_shape=(jax.ShapeDtypeStruct((s_len, d), F32), jax.ShapeDtypeStruct((1, d), F32)),
        grid=(s_len // tm,),
        in_specs=[row, vec, row, row],
        out_specs=(row, vec),
        compiler_params=_params("arbitrary"),
    )(x, gain, dn, dres)


def _ple_backward_gate(dh2, gate, pp, tm):
    s_len, d = dh2.shape

    def body(dh_ref, gate_ref, pp_ref, dg_ref, dpp_ref):
        dh = dh_ref[...]
        gate = gate_ref[...].astype(F32)
        dg_ref[...] = (dh * pp_ref[...].astype(F32) * gate * (1.0 - gate)).astype(BF16)
        dpp_ref[...] = (dh * gate).astype(BF16)

    row = pl.BlockSpec((tm, d), lambda i: (i, 0))
    return pl.pallas_call(
        body, name="ple_backward_gate",
        out_shape=(jax.ShapeDtypeStruct((s_len, d), BF16), jax.ShapeDtypeStruct((s_len, d), BF16)),
        grid=(s_len // tm,),
        in_specs=[row, row, row],
        out_specs=(row, row),
        compiler_params=_params("parallel"),
    )(dh2, gate, pp)


def _sgu_in_specs(tm):
    return [pl.BlockSpec((tm, W_A), lambda i: (i, O_A // W_A)),
            pl.BlockSpec((A_HEADS, HEAD), lambda i: (0, 0)), pl.BlockSpec((A_HEADS, CHUNK, CHUNK), lambda i: (0, 0, 0)),
            pl.BlockSpec((A_HEADS, CHUNK, 1), lambda i: (0, 0, 0)), pl.BlockSpec((1, 512), lambda i: (0, 0))]


def _sgu_load(a_ref, gain_ref, ws_ref, bs_ref, ga_ref, c):
    rows = slice(c * CHUNK, (c + 1) * CHUNK)
    heads = range(A_HEADS)
    u = tuple(a_ref[rows, h * HEAD:(h + 1) * HEAD].astype(F32) for h in heads)
    v = tuple(a_ref[rows, 512 + h * HEAD:512 + (h + 1) * HEAD].astype(F32) for h in heads)
    z = tuple(a_ref[rows, 1024 + h * HEAD:1024 + (h + 1) * HEAD].astype(F32) for h in heads)
    gain = tuple(gain_ref[h:h + 1, :] for h in heads)
    ws = tuple(ws_ref[h] for h in heads)
    bs = tuple(bs_ref[h] for h in heads)
    ga = tuple(ga_ref[:, h * HEAD:(h + 1) * HEAD] for h in heads)
    return u, v, z, gain, ws, bs, ga


def _sgu_forward(proj, gain, ws, bs, ga, tm):
    s_len = proj.shape[0]

    def body(a_ref, gain_ref, ws_ref, bs_ref, ga_ref, o_ref):
        for c in range(tm // CHUNK):
            out = _sgu_chunk(*_sgu_load(a_ref, gain_ref, ws_ref, bs_ref, ga_ref, c))
            for h in range(A_HEADS):
                o_ref[c * CHUNK:(c + 1) * CHUNK, h * HEAD:(h + 1) * HEAD] = out[h].astype(BF16)

    return pl.pallas_call(
        body, name="sgu_forward",
        out_shape=jax.ShapeDtypeStruct((s_len, D_MODEL), BF16),
        grid=(s_len // tm,),
        in_specs=_sgu_in_specs(tm),
        out_specs=pl.BlockSpec((tm, 512), lambda i: (i, 0)),
        compiler_params=_params("parallel"),
    )(proj, gain, ws, bs, ga)


def _sgu_backward(proj, gain, ws, bs, ga, dy, dproj, tm):
    s_len = proj.shape[0]

    def body(a_ref, gain_ref, ws_ref, bs_ref, ga_ref, dy_ref, _, da_ref, dgain_ref, dws_ref, dbs_ref, dga_ref):
        tot = None
        for c in range(tm // CHUNK):
            args = _sgu_load(a_ref, gain_ref, ws_ref, bs_ref, ga_ref, c)
            _, vjp = jax.vjp(_sgu_chunk, *args)
            rows = slice(c * CHUNK, (c + 1) * CHUNK)
            du, dv, dz, dgain, dws, dbs, dga = vjp(tuple(dy_ref[rows, h * HEAD:(h + 1) * HEAD] for h in range(A_HEADS)))
            for h in range(A_HEADS):
                da_ref[rows, h * HEAD:(h + 1) * HEAD] = du[h].astype(BF16)
                da_ref[rows, 512 + h * HEAD:512 + (h + 1) * HEAD] = dv[h].astype(BF16)
                da_ref[rows, 1024 + h * HEAD:1024 + (h + 1) * HEAD] = dz[h].astype(BF16)
            part = (dgain, dws, dbs, dga)
            tot = part if tot is None else jax.tree.map(jnp.add, tot, part)
        dgain, dws, dbs, dga = tot
        first = pl.program_id(0) == 0
        _acc(dgain_ref, jnp.concatenate(dgain, axis=0), first)
        _acc(dga_ref, jnp.concatenate(dga, axis=-1), first)
        for h in range(A_HEADS):
            _acc(dws_ref.at[h], dws[h], first)
            _acc(dbs_ref.at[h], dbs[h], first)

    small = [pl.BlockSpec((A_HEADS, HEAD), lambda i: (0, 0)), pl.BlockSpec((A_HEADS, CHUNK, CHUNK), lambda i: (0, 0, 0)),
             pl.BlockSpec((A_HEADS, CHUNK, 1), lambda i: (0, 0, 0)), pl.BlockSpec((1, 512), lambda i: (0, 0))]
    return pl.pallas_call(
        body, name="sgu_backward",
        out_shape=(jax.ShapeDtypeStruct(dproj.shape, BF16),
                   jax.ShapeDtypeStruct((A_HEADS, HEAD), F32), jax.ShapeDtypeStruct((A_HEADS, CHUNK, CHUNK), F32),
                   jax.ShapeDtypeStruct((A_HEADS, CHUNK, 1), F32), jax.ShapeDtypeStruct((1, 512), F32)),
        grid=(s_len // tm,),
        in_specs=_sgu_in_specs(tm) + [pl.BlockSpec((tm, 512), lambda i: (i, 0)), pl.BlockSpec(memory_space=pl.ANY)],
        out_specs=(pl.BlockSpec((tm, W_A), lambda i: (i, O_A // W_A)), *small),
        input_output_aliases={6: 0},
        compiler_params=_params("arbitrary"),
    )(proj, gain, ws, bs, ga, dy, dproj)


def _halo_specs(tm, width, col, n_rows):
    per = tm // HALO
    last = n_rows // HALO - 1
    return [pl.BlockSpec((HALO, width), lambda i: (jnp.maximum(i * per - 1, 0), col)),
            pl.BlockSpec((tm, width), lambda i: (i, col)),
            pl.BlockSpec((HALO, width), lambda i: (jnp.minimum((i + 1) * per, last), col))]


def _conv_masks(tm, s_len):
    r = lax.broadcasted_iota(jnp.int32, (tm + 2 * HALO, 1), 0)
    g = pl.program_id(0) * tm - HALO + r
    return (g >= 0) & (g < s_len), (r >= HALO) & (r < HALO + tm)


def _conv_inputs(b_refs, cw_ref, cb_ref, gb_ref):
    ext = jnp.concatenate([r[...] for r in b_refs], axis=0).astype(F32)
    bb, bc, bh, bz = (ext[:, j * 512:(j + 1) * 512] for j in range(4))
    prm = (cw_ref[0:1, :], cw_ref[1:2, :], cw_ref[2:3, :], cb_ref[...], gb_ref[...])
    return (bb, bc, bh, bz), prm


def _conv_forward(proj, cw, cb, gb, y, tm):
    s_len = proj.shape[0]

    def body(p0, p1, p2, cw_ref, cb_ref, gb_ref, _, o_ref):
        acts, prm = _conv_inputs((p0, p1, p2), cw_ref, cb_ref, gb_ref)
        valid, core = _conv_masks(tm, s_len)
        out = _conv_tile(*acts, *prm, *prm, valid, core)
        o_ref[...] = out[HALO:HALO + tm].astype(BF16)

    vec = pl.BlockSpec((1, 512), lambda i: (0, 0))
    return pl.pallas_call(
        body, name="conv_forward",
        out_shape=jax.ShapeDtypeStruct(y.shape, BF16),
        grid=(s_len // tm,),
        in_specs=_halo_specs(tm, W_B, O_B // W_B, s_len) + [pl.BlockSpec((3, 512), lambda i: (0, 0)), vec, vec,
                                                             pl.BlockSpec(memory_space=pl.ANY)],
        out_specs=pl.BlockSpec((tm, 512), lambda i: (i, 1)),
        input_output_aliases={6: 0},
        compiler_params=_params("parallel"),
    )(proj, proj, proj, cw, cb, gb, y)


def _conv_backward(proj, cw, cb, gb, dy, tm):
    s_len = proj.shape[0]

    def body(p0, p1, p2, cw_ref, cb_ref, gb_ref, d0, d1, d2, db_ref, dcw_ref, dcb_ref, dgb_ref):
        acts, prm = _conv_inputs((p0, p1, p2), cw_ref, cb_ref, gb_ref)
        valid, core = _conv_masks(tm, s_len)
        _, vjp = jax.vjp(lambda a, p: _conv_tile(*a, *p, *prm, valid, core), acts, prm)
        dy_ext = jnp.where(valid, jnp.concatenate([d0[...], d1[...], d2[...]], axis=0), 0.0)
        dacts, dprm = vjp(dy_ext)
        for j in range(4):
            db_ref[:, j * 512:(j + 1) * 512] = dacts[j][HALO:HALO + tm].astype(BF16)
        first = pl.program_id(0) == 0
        _acc(dcw_ref, jnp.concatenate(dprm[0:3], axis=0), first)
        _acc(dcb_ref, dprm[3], first)
        _acc(dgb_ref, dprm[4], first)

    vec = pl.BlockSpec((1, 512), lambda i: (0, 0))
    mat = pl.BlockSpec((3, 512), lambda i: (0, 0))
    return pl.pallas_call(
        body, name="conv_backward",
        out_shape=(jax.ShapeDtypeStruct((s_len, PROJ_W), BF16), jax.ShapeDtypeStruct((3, 512), F32),
                   jax.ShapeDtypeStruct((1, 512), F32), jax.ShapeDtypeStruct((1, 512), F32)),
        grid=(s_len // tm,),
        in_specs=_halo_specs(tm, W_B, O_B // W_B, s_len) + [mat, vec, vec] + _halo_specs(tm, 512, 1, s_len),
        out_specs=(pl.BlockSpec((tm, W_B), lambda i: (i, O_B // W_B)), mat, vec, vec),
        compiler_params=_params("arbitrary"),
    )(proj, proj, proj, cw, cb, gb, dy, dy, dy)


def _cgate_forward(o, proj, gc, y, tm):
    s_len = o.shape[0]

    def body(o_ref, cz_ref, gc_ref, _, y_ref):
        y_ref[...] = _cgate_tile(o_ref[...], cz_ref[...].astype(F32), gc_ref[...]).astype(BF16)

    return pl.pallas_call(
        body, name="cgate_forward",
        out_shape=jax.ShapeDtypeStruct(y.shape, BF16),
        grid=(s_len // tm,),
        in_specs=[pl.BlockSpec((tm, W_CZ), lambda i: (i, 0)), pl.BlockSpec((tm, W_CZ), lambda i: (i, O_CZ // W_CZ)),
                  pl.BlockSpec((1, W_CZ), lambda i: (0, 0)), pl.BlockSpec(memory_space=pl.ANY)],
        out_specs=pl.BlockSpec((tm, W_CZ), lambda i: (i, 1)),
        input_output_aliases={3: 0},
        compiler_params=_params("parallel"),
    )(o, proj, gc, y)


def _cgate_backward(o, proj, gc, dy, dproj, tm, stat_chunk):
    s_len = o.shape[0]
    per_stat = stat_chunk // tm

    def body(o_ref, cz_ref, gc_ref, dy_ref, _, dcz_ref, do_ref, dsum_ref, dgc_ref):
        o = o_ref[...]
        _, vjp = jax.vjp(_cgate_tile, o, cz_ref[...].astype(F32), gc_ref[...])
        do, dcz, dgc = vjp(dy_ref[...])
        dcz_ref[...] = dcz.astype(BF16)
        do_ref[...] = do.astype(BF16)
        ones = jnp.ones((8, HEAD), F32)
        for h in range(C_HEADS):
            cols = slice(h * HEAD, (h + 1) * HEAD)
            sums = lax.dot_general(ones, do[:, cols] * o[:, cols], _NT, precision=lax.Precision.HIGHEST,
                                   preferred_element_type=F32)
            dsum_ref[h, 0] = sums[0:1]
        _acc(dgc_ref, dgc, pl.program_id(0) == 0)

    row = pl.BlockSpec((tm, W_CZ), lambda i: (i, 0))
    vec = pl.BlockSpec((1, W_CZ), lambda i: (0, 0))
    return pl.pallas_call(
        body, name="cgate_backward",
        out_shape=(jax.ShapeDtypeStruct(dproj.shape, BF16), jax.ShapeDtypeStruct((s_len, W_CZ), BF16),
                   jax.ShapeDtypeStruct((C_HEADS, s_len // stat_chunk, 1, stat_chunk), F32), jax.ShapeDtypeStruct((1, W_CZ), F32)),
        grid=(s_len // tm,),
        in_specs=[row, pl.BlockSpec((tm, W_CZ), lambda i: (i, O_CZ // W_CZ)), vec,
                  pl.BlockSpec((tm, W_CZ), lambda i: (i, 1)), pl.BlockSpec(memory_space=pl.ANY)],
        out_specs=(pl.BlockSpec((tm, W_CZ), lambda i: (i, O_CZ // W_CZ)), row,
                   pl.BlockSpec((C_HEADS, 1, 1, tm), lambda i: (0, i // per_stat, 0, i % per_stat)), vec),
        input_output_aliases={4: 0},
        compiler_params=_params("arbitrary"),
    )(o, proj, gc, dy, dproj)


def _mla_small_specs():
    return [pl.BlockSpec((KV_RANK, 2 * C_HEADS * HEAD), lambda i: (0, 0)), pl.BlockSpec((1, KV_RANK), lambda i: (0, 0)),
            pl.BlockSpec((1, HEAD), lambda i: (0, 0)), pl.BlockSpec((1, ROPE), lambda i: (0, 0)),
            pl.BlockSpec((1, HEAD), lambda i: (0, 0)), pl.BlockSpec((1, ROPE), lambda i: (0, 0))]


def _mla_load(m_ref, cos_ref, sin_ref):
    qn = m_ref[:, M_QN:M_QN + C_HEADS * HEAD].astype(F32)
    qr = m_ref[:, M_QR:M_QR + C_HEADS * ROPE].astype(F32)
    ckv = m_ref[:, M_CKV:M_CKV + KV_RANK].astype(F32)
    kr = m_ref[:, M_KR:M_KR + ROPE].astype(F32)
    return qn, qr, ckv, kr, cos_ref[...], sin_ref[...]


def _mla_forward(proj, cosf, sins, wukv, kvg, qng, qrg, kng, krg, tm, kt_chunk, vt_chunk):
    s_len = proj.shape[0]

    def body(m_ref, cos_ref, sin_ref, w_ref, kvg_ref, qng_ref, qrg_ref, kng_ref, krg_ref, q_ref, k_ref, v_ref, kt_ref, vt_ref):
        q, k, v = _mla_tile(*_mla_load(m_ref, cos_ref, sin_ref), w_ref[...], kvg_ref[...], qng_ref[...], qrg_ref[...],
                            kng_ref[...], krg_ref[...])
        for h in range(C_HEADS):
            q_ref[h] = q[h].astype(BF16)
            k_ref[h] = k[h].astype(BF16)
            v_ref[h] = v[h].astype(BF16)
            kt_ref[h, 0] = jnp.concatenate([k[h][:, :HEAD].T, k[h][:, HEAD:].T], axis=0).astype(BF16)
            vt_ref[h, 0] = v[h].T.astype(BF16)

    rope_spec = pl.BlockSpec((tm, ROPE), lambda i: (i, 0))
    qk_spec = pl.BlockSpec((C_HEADS, tm, QK), lambda i: (0, i, 0))
    per_k, per_v = kt_chunk // tm, vt_chunk // tm
    return pl.pallas_call(
        body, name="mla_forward",
        out_shape=(jax.ShapeDtypeStruct((C_HEADS, s_len, QK), BF16), jax.ShapeDtypeStruct((C_HEADS, s_len, QK), BF16),
                   jax.ShapeDtypeStruct((C_HEADS, s_len, HEAD), BF16),
                   jax.ShapeDtypeStruct((C_HEADS, s_len // kt_chunk, QK, kt_chunk), BF16),
                   jax.ShapeDtypeStruct((C_HEADS, s_len // vt_chunk, HEAD, vt_chunk), BF16)),
        grid=(s_len // tm,),
        in_specs=[pl.BlockSpec((tm, W_M), lambda i: (i, O_M // W_M)), rope_spec, rope_spec] + _mla_small_specs(),
        out_specs=(qk_spec, qk_spec, pl.BlockSpec((C_HEADS, tm, HEAD), lambda i: (0, i, 0)),
                   pl.BlockSpec((C_HEADS, 1, QK, tm), lambda i: (0, i // per_k, 0, i % per_k)),
                   pl.BlockSpec((C_HEADS, 1, HEAD, tm), lambda i: (0, i // per_v, 0, i % per_v))),
        compiler_params=_params("parallel"),
    )(proj, cosf, sins, wukv, kvg, qng, qrg, kng, krg)


def _mla_backward(proj, cosf, sins, wukv, kvg, qng, qrg, kng, krg, dqt, dk, dv, dproj, tm):
    s_len = proj.shape[0]
    per_chunk = dqt.shape[3] // tm

    def body(m_ref, cos_ref, sin_ref, w_ref, kvg_ref, qng_ref, qrg_ref, kng_ref, krg_ref, dq_ref, dk_ref, dv_ref, _,
             dm_ref, dw_ref, dkvg_ref, dqng_ref, dqrg_ref, dkng_ref, dkrg_ref):
        qn, qr, ckv, kr, cosf_t, sins_t = _mla_load(m_ref, cos_ref, sin_ref)
        prm = (w_ref[...], kvg_ref[...], qng_ref[...], qrg_ref[...], kng_ref[...], krg_ref[...])
        _, vjp = jax.vjp(lambda a, p: _mla_tile(*a, cosf_t, sins_t, *p), (qn, qr, ckv, kr), prm)
        heads = range(C_HEADS)
        dacts, dprm = vjp((tuple(dq_ref[h, 0].T for h in heads), tuple(dk_ref[h] for h in heads), tuple(dv_ref[h] for h in heads)))
        dm_ref[:, M_QN:M_QN + C_HEADS * HEAD] = dacts[0].astype(BF16)
        dm_ref[:, M_QR:M_QR + C_HEADS * ROPE] = dacts[1].astype(BF16)
        dm_ref[:, M_CKV:M_CKV + KV_RANK] = dacts[2].astype(BF16)
        pad = jnp.zeros((tm, W_M - M_KR - ROPE), F32)
        dm_ref[:, M_KR:W_M] = jnp.concatenate([dacts[3], pad], axis=-1).astype(BF16)
        first = pl.program_id(0) == 0
        for ref, val in zip((dw_ref, dkvg_ref, dqng_ref, dqrg_ref, dkng_ref, dkrg_ref), dprm):
            _acc(ref, val.astype(F32), first)

    rope_spec = pl.BlockSpec((tm, ROPE), lambda i: (i, 0))
    qk_spec = pl.BlockSpec((C_HEADS, tm, QK), lambda i: (0, i, 0))
    small = _mla_small_specs()
    return pl.pallas_call(
        body, name="mla_backward",
        out_shape=(jax.ShapeDtypeStruct(dproj.shape, BF16), jax.ShapeDtypeStruct((KV_RANK, 2 * C_HEADS * HEAD), F32),
                   jax.ShapeDtypeStruct((1, KV_RANK), F32), jax.ShapeDtypeStruct((1, HEAD), F32),
                   jax.ShapeDtypeStruct((1, ROPE), F32), jax.ShapeDtypeStruct((1, HEAD), F32),
                   jax.ShapeDtypeStruct((1, ROPE), F32)),
        grid=(s_len // tm,),
        in_specs=[pl.BlockSpec((tm, W_M), lambda i: (i, O_M // W_M)), rope_spec, rope_spec] + small
                 + [pl.BlockSpec((C_HEADS, 1, QK, tm), lambda i: (0, i // per_chunk, 0, i % per_chunk)), qk_spec,
                    pl.BlockSpec((C_HEADS, tm, HEAD), lambda i: (0, i, 0)), pl.BlockSpec(memory_space=pl.ANY)],
        out_specs=(pl.BlockSpec((tm, W_M), lambda i: (i, O_M // W_M)), *small),
        input_output_aliases={12: 0},
        compiler_params=_params("arbitrary"),
    )(proj, cosf, sins, wukv, kvg, qng, qrg, kng, krg, dqt, dk, dv, dproj)


def _attention_forward(q, k, vt, tq, stat_chunk, comm=None):
    n_heads, s_len, _ = q.shape
    n_chunks, _, ck = vt.shape[1:]
    c_ops, c_in_specs, c_shapes, c_out_specs, c_sems, c_begin, c_end = _riding_exchange(comm, 2)
    n_c = len(c_ops)

    def body(q_ref, k_ref, vt_ref, *rest):
        c_ins, (o_ref, lse_ref), c_outs, sems = rest[:n_c], rest[n_c:n_c + 2], rest[n_c + 2:2 * n_c + 2], rest[2 * n_c + 2:]
        c_begin(c_ins, c_outs, sems)
        q_t = q_ref[0]

        def step(j, carry):
            m_old, l_old, acc = carry
            k_j = k_ref[0, pl.ds(pl.multiple_of(j * ck, ck), ck), :]
            s = lax.dot_general(k_j, q_t, _NT, preferred_element_type=F32)
            m_new = jnp.maximum(m_old, jnp.max(s, axis=0, keepdims=True))
            p = jnp.exp2(s - m_new)
            alpha = jnp.exp2(m_old - m_new)
            l_new = alpha * l_old + jnp.sum(p, axis=0, keepdims=True)
            acc = alpha * acc + jnp.dot(vt_ref[0, j], p.astype(BF16), preferred_element_type=F32)
            return m_new, l_new, acc

        init = (jnp.full((1, tq), -jnp.inf, F32), jnp.zeros((1, tq), F32), jnp.zeros((HEAD, tq), F32))
        m_fin, l_fin, acc = lax.fori_loop(0, n_chunks, step, init)
        o_ref[...] = (acc / l_fin).T
        lse_ref[0, 0] = m_fin + jnp.log2(l_fin)
        c_end(c_ins, c_outs, sems)

    per_stat = stat_chunk // tq
    outs = pl.pallas_call(
        body, name="attention_forward",
        out_shape=(jax.ShapeDtypeStruct((s_len, n_heads * HEAD), F32),
                   jax.ShapeDtypeStruct((n_heads, s_len // stat_chunk, 1, stat_chunk), F32), *c_shapes),
        grid=(n_heads, s_len // tq),
        in_specs=[pl.BlockSpec((1, tq, QK), lambda h, i: (h, i, 0)), pl.BlockSpec((1, s_len, QK), lambda h, i: (h, 0, 0)),
                  pl.BlockSpec((1, n_chunks, HEAD, ck), lambda h, i: (h, 0, 0, 0))] + c_in_specs,
        out_specs=(pl.BlockSpec((tq, HEAD), lambda h, i: (i, h)),
                   pl.BlockSpec((1, 1, 1, tq), lambda h, i: (h, i // per_stat, 0, i % per_stat)), *c_out_specs),
        scratch_shapes=c_sems,
        compiler_params=_params("arbitrary", "arbitrary") if comm else _params("parallel", "parallel"),
    )(q, k, vt, *c_ops)
    return outs[0], outs[1], outs[2:]


def _attention_backward(q, k, kt, v, do, lse, dsum, comm=None):
    n_heads, s_len, _ = q.shape
    tk = kt.shape[3]
    n_q, _, cq = lse.shape[1:]
    c_ops, c_in_specs, c_shapes, c_out_specs, c_sems, c_begin, c_end = _riding_exchange(comm, 2)
    n_c = len(c_ops)

    def body(q_ref, k_ref, kt_ref, v_ref, do_ref, lse_ref, dsum_ref, *rest):
        c_ins, (dqt_ref, dk_ref, dv_ref), c_outs, sems = rest[:n_c], rest[n_c:n_c + 3], rest[n_c + 3:2 * n_c + 3], rest[2 * n_c + 3:]
        c_begin(c_ins, c_outs, sems)
        first = pl.program_id(1) == 0
        k_j, kt_j, v_j = k_ref[0], kt_ref[0, 0], v_ref[0]

        def step(i, carry):
            dk, dv = carry
            rows = pl.ds(pl.multiple_of(i * cq, cq), cq)
            q_i, do_i = q_ref[0, rows, :], do_ref[rows, :]
            s = lax.dot_general(k_j, q_i, _NT, preferred_element_type=F32)
            p = jnp.exp2(s - lse_ref[0, i])
            dp = lax.dot_general(v_j, do_i, _NT, preferred_element_type=F32)
            ds = (p * (dp - dsum_ref[0, i]) * LN_2).astype(BF16)
            dv = dv + jnp.dot(p.astype(BF16), do_i, preferred_element_type=F32)
            dk = dk + jnp.dot(ds, q_i, preferred_element_type=F32)
            _acc(dqt_ref.at[0, i], jnp.dot(kt_j, ds, preferred_element_type=F32), first)
            return dk, dv

        dk, dv = lax.fori_loop(0, n_q, step, (jnp.zeros((tk, QK), F32), jnp.zeros((tk, HEAD), F32)))
        dk_ref[0] = dk
        dv_ref[0] = dv
        c_end(c_ins, c_outs, sems)

    stat = pl.BlockSpec((1, n_q, 1, cq), lambda h, j: (h, 0, 0, 0))
    outs = pl.pallas_call(
        body, name="attention_backward",
        out_shape=(jax.ShapeDtypeStruct((n_heads, n_q, QK, cq), F32), jax.ShapeDtypeStruct((n_heads, s_len, QK), F32),
                   jax.ShapeDtypeStruct((n_heads, s_len, HEAD), F32), *c_shapes),
        grid=(n_heads, s_len // tk),
        in_specs=[pl.BlockSpec((1, s_len, QK), lambda h, j: (h, 0, 0)), pl.BlockSpec((1, tk, QK), lambda h, j: (h, j, 0)),
                  pl.BlockSpec((1, 1, QK, tk), lambda h, j: (h, j, 0, 0)),
                  pl.BlockSpec((1, tk, HEAD), lambda h, j: (h, j, 0)), pl.BlockSpec((s_len, HEAD), lambda h, j: (0, h)),
                  stat, stat] + c_in_specs,
        out_specs=(pl.BlockSpec((1, n_q, QK, cq), lambda h, j: (h, 0, 0, 0)), pl.BlockSpec((1, tk, QK), lambda h, j: (h, j, 0)),
                   pl.BlockSpec((1, tk, HEAD), lambda h, j: (h, j, 0)), *c_out_specs),
        scratch_shapes=c_sems,
        compiler_params=_params("arbitrary", "arbitrary") if comm else _params("parallel", "arbitrary"),
    )(q, k, kt, v, do, lse, dsum, *c_ops)
    return outs[0], outs[1], outs[2], outs[3:]


def _exchange(arrs, gather, name):
    n = len(arrs)

    def body(*refs):
        plan = _exchange_plan(refs[:n], refs[n:2 * n], gather, *refs[2 * n:])
        _exchange_start(plan)
        _exchange_wait(plan)

    any_spec = pl.BlockSpec(memory_space=pl.ANY)
    return pl.pallas_call(
        body, name=name,
        out_shape=_exchange_out_shapes(arrs, gather),
        in_specs=[any_spec] * n,
        out_specs=tuple([any_spec] * n),
        scratch_shapes=_exchange_semaphores(n),
        compiler_params=pltpu.CompilerParams(has_side_effects=True),
    )(*arrs)


def _exchange_out_shapes(arrs, gather):
    return tuple(jax.ShapeDtypeStruct((N_DEV, *(a.shape if g else a.shape[1:])), a.dtype) for a, g in zip(arrs, gather))


def _exchange_semaphores(n):
    n_remote = n * (N_DEV - 1)
    return [pltpu.SemaphoreType.DMA((n_remote,)), pltpu.SemaphoreType.DMA((n_remote,)), pltpu.SemaphoreType.DMA((n,))]


def _exchange_plan(ins, outs, gather, send_sems, recv_sems, local_sems):
    n = len(ins)
    x, y, c = lax.axis_index("x"), lax.axis_index("y"), lax.axis_index("c")
    me = 4 * x + 2 * y + c

    def block_for(a, dev):
        return ins[a] if gather[a] else ins[a].at[dev]

    local = [pltpu.make_async_copy(block_for(a, me), outs[a].at[me], local_sems.at[a]) for a in range(n)]
    remote = []
    for k in range(1, N_DEV):
        px = 1 - x if k & 4 else x
        py = 1 - y if k & 2 else y
        pc = 1 - c if k & 1 else c
        peer = 4 * px + 2 * py + pc
        for a in range(n):
            idx = a * (N_DEV - 1) + k - 1
            send = pltpu.make_async_remote_copy(
                src_ref=block_for(a, peer), dst_ref=outs[a].at[me], send_sem=send_sems.at[idx], recv_sem=recv_sems.at[idx],
                device_id=(px, py, pc), device_id_type=pl.DeviceIdType.MESH)
            arrive = pltpu.make_async_remote_copy(
                src_ref=block_for(a, peer), dst_ref=outs[a].at[peer], send_sem=send_sems.at[idx], recv_sem=recv_sems.at[idx],
                device_id=(px, py, pc), device_id_type=pl.DeviceIdType.MESH)
            remote.append((send, arrive))
    return local, remote


def _exchange_start(plan):
    local, remote = plan
    for cp in local:
        cp.start()
    for send, _ in remote:
        send.start()


def _exchange_wait(plan):
    local, remote = plan
    for send, arrive in remote:
        send.wait_send()
        arrive.wait_recv()
    for cp in local:
        cp.wait()


def _riding_exchange(comm, n_grid):
    if comm is None:
        return [], [], (), (), [], lambda *_: None, lambda *_: None
    arrs, gather = comm
    n = len(arrs)
    any_spec = pl.BlockSpec(memory_space=pl.ANY)

    def begin(ins, outs, sems):
        @pl.when(functools.reduce(jnp.logical_and, [pl.program_id(d) == 0 for d in range(n_grid)]))
        def _():
            _exchange_start(_exchange_plan(ins, outs, gather, *sems))

    def end(ins, outs, sems):
        @pl.when(functools.reduce(jnp.logical_and, [pl.program_id(d) == pl.num_programs(d) - 1 for d in range(n_grid)]))
        def _():
            _exchange_wait(_exchange_plan(ins, outs, gather, *sems))

    return (list(arrs), [any_spec] * n, _exchange_out_shapes(arrs, gather), tuple([any_spec] * n), _exchange_semaphores(n),
            begin, end)


ADAM_TILE_ELEMS = 256 * 1024


def _sum_adam(parts, w, m, v, layer, prev, name):
    n_parts, r, c = parts.shape
    tm, tc = r, c
    if r % 16 == 0:
        while tm * c > ADAM_TILE_ELEMS and tm % 16 == 0:
            tm //= 2
    else:
        while r * tc > ADAM_TILE_ELEMS and tc % 256 == 0:
            tc //= 2

    def body(p_ref, w_ref, m_ref, v_ref, *rest):
        g_ref, d_ref, nm_ref, nv_ref = rest[-4:]
        g = p_ref[0].astype(F32)
        for s in range(1, n_parts):
            g = g + p_ref[s].astype(F32)
        m_new = ADAM_B1 * m_ref[...] + (1.0 - ADAM_B1) * g
        v_new = ADAM_B2 * v_ref[...] + (1.0 - ADAM_B2) * (g * g)
        m_hat = m_new / (1.0 - ADAM_B1 ** ADAM_STEP)
        v_hat = v_new / (1.0 - ADAM_B2 ** ADAM_STEP)
        g_ref[...] = g
        d_ref[...] = -ADAM_LR * (m_hat / (jnp.sqrt(v_hat) + ADAM_EPS) + ADAM_WD * w_ref[...])
        nm_ref[...] = m_new
        nv_ref[...] = v_new

    slab = pl.BlockSpec((None, tm, tc), lambda i, j: (layer, i, j))
    n_prev = 0 if prev is None else 4
    return pl.pallas_call(
        body, name=name,
        out_shape=(jax.ShapeDtypeStruct(w.shape, F32),) * 4,
        grid=(r // tm, c // tc),
        in_specs=[pl.BlockSpec((n_parts, tm, tc), lambda i, j: (0, i, j)), slab, slab, slab]
                 + [pl.BlockSpec(memory_space=pl.ANY)] * n_prev,
        out_specs=(slab, slab, slab, slab),
        input_output_aliases={4 + j: j for j in range(n_prev)},
        compiler_params=_params("parallel", "parallel"),
    )(parts, w, m, v, *(prev or ()))


def _permute_in(w):
    k = w.shape[0]
    q = w[:, 3584:5120].reshape(k, C_HEADS, QK)
    return jnp.concatenate(
        [w[:, 1536:3584], w[:, 5696:6720], w[:, 0:1536], q[:, :, :HEAD].reshape(k, C_HEADS * HEAD),
         q[:, :, HEAD:].reshape(k, C_HEADS * ROPE), w[:, 5120:5632], w[:, 5632:5696],
         jnp.zeros((k, PROJ_W - IN_WIDTH), w.dtype)], axis=1)


def _unpermute_in(g):
    k = g.shape[0]
    qn = g[:, O_M + M_QN:O_M + M_QR].reshape(k, C_HEADS, HEAD)
    qr = g[:, O_M + M_QR:O_M + M_CKV].reshape(k, C_HEADS, ROPE)
    q = jnp.concatenate([qn, qr], axis=-1).reshape(k, C_HEADS * QK)
    return jnp.concatenate(
        [g[:, O_A:O_A + W_A], g[:, O_B:O_B + W_B], q, g[:, O_M + M_CKV:O_M + M_KR],
         g[:, O_M + M_KR:O_M + M_KR + ROPE], g[:, O_CZ:O_CZ + W_CZ]], axis=1)


SMALL = ("attn_norm", "sgu_norm", "w_spatial", "b_spatial", "conv_b", "kv_norm", "q_nope_norm", "q_rope_norm",
         "k_nope_norm", "k_rope_norm", "out_norm", "ple_norm")
PACK_ROWS = 256


def _pack(tensors):
    flat = jnp.concatenate([t.reshape(-1) for t in tensors])
    rows = -(-flat.shape[0] // (128 * PACK_ROWS)) * PACK_ROWS
    return jnp.pad(flat, (0, rows * 128 - flat.shape[0])).reshape(rows, 128)


def _unpack(packed, like):
    flat = packed.reshape(-1)
    out, pos = [], 0
    for t in like:
        out.append(flat[pos:pos + t.size].reshape(t.shape))
        pos += t.size
    return out


def _tile(s_len, want):
    return min(want, s_len)


ATT_FWD_QUERIES = 512
ATT_FWD_KEYS = 8192
ATT_BWD_KEYS = 512
ATT_BWD_QUERIES = 2048


def _layer_forward(h, p_l, cosf, sins, w, sm, comm, comm_rest):
    s_len = h.shape[0]
    tm = _tile(s_len, 512)
    proj, hn, rest = _norm_matmul(h, sm["attn_norm"], w["w_in"], tm, 768, comm_rest)
    if comm_rest is not None:
        w = {**w, **_assemble_rest(rest)}
    ga, gb, gc = sm["out_norm"][:, 0:512], sm["out_norm"][:, 512:1024], sm["out_norm"][:, 1024:2048]
    y = _sgu_forward(proj, sm["sgu_norm"], sm["w_spatial"], sm["b_spatial"], ga, _tile(s_len, 256))
    y = _conv_forward(proj, w["conv_w"], sm["conv_b"], gb, y, _tile(s_len, 256))
    q, k, v, kt, vt = _mla_forward(proj, cosf, sins, w["w_ukv"], sm["kv_norm"], sm["q_nope_norm"], sm["q_rope_norm"],
                                   sm["k_nope_norm"], sm["k_rope_norm"], _tile(s_len, 256), _tile(s_len, ATT_BWD_KEYS),
                                   _tile(s_len, ATT_FWD_KEYS))
    o, lse, arrived = _attention_forward(q, k, vt, _tile(s_len, ATT_FWD_QUERIES), _tile(s_len, ATT_BWD_QUERIES), comm)
    y = _cgate_forward(o, proj, gc, y, _tile(s_len, 256))
    h1 = _out_matmul(h, y, w["w_out"], _tile(s_len, 1024), 1024)
    h2, n1, gate, pp = _ple_forward(h1, sm["ple_norm"], p_l, w["w_ple_gate"], w["w_ple_proj"], tm, 1024)
    saved = dict(h=h, hn=hn, proj=proj, y=y, q=q, k=k, v=v, kt=kt, o=o, lse=lse, h1=h1, n1=n1, gate=gate, pp=pp)
    return h2, saved, w, arrived


def _layer_backward(dh2, p_l, cosf, sins, w, sm, sv, comm, scatter_own):
    s_len = dh2.shape[0]
    tm = _tile(s_len, 512)
    tr = _tile(s_len, 256)
    big, small = {}, {}
    dgp, dpp = _ple_backward_gate(dh2, sv["gate"], sv["pp"], tm)
    big["w_ple_proj"] = _matmul_tn(p_l, dpp, _tile(s_len, 1024), PLE_DIM, 1024, "grad_w_ple_proj")
    big["w_ple_gate"] = _matmul_tn(sv["n1"], dgp, _tile(s_len, 1024), 512, 1024, "grad_w_ple_gate")
    dh1, small["ple_norm"] = _matmul_nt_rms_backward(dgp, w["w_ple_gate"], sv["h1"], sm["ple_norm"], dh2, tm,
                                                     "ple_norm_backward")
    dy, _ = _matmul_nt(dh1, w["w_out"], tm, 1024, "grad_branches")
    big["w_out"] = _matmul_tn(sv["y"], dh1, _tile(s_len, 1024), 512, 1024, "grad_w_out")
    ga, gb, gc = sm["out_norm"][:, 0:512], sm["out_norm"][:, 512:1024], sm["out_norm"][:, 1024:2048]
    dproj, dcw, small["conv_b"], dgb = _conv_backward(sv["proj"], w["conv_w"], sm["conv_b"], gb, dy, tr)
    big["conv_w"] = dcw
    dproj, do, dsum, dgc = _cgate_backward(sv["o"], sv["proj"], gc, dy, dproj, tr, _tile(s_len, ATT_BWD_QUERIES))
    dqt, dk, dv, arrived = _attention_backward(sv["q"], sv["k"], sv["kt"], sv["v"], do, sv["lse"], dsum, comm)
    (dproj, big["w_ukv"], small["kv_norm"], small["q_nope_norm"], small["q_rope_norm"], small["k_nope_norm"],
     small["k_rope_norm"]) = _mla_backward(sv["proj"], cosf, sins, w["w_ukv"], sm["kv_norm"], sm["q_nope_norm"],
                                            sm["q_rope_norm"], sm["k_nope_norm"], sm["k_rope_norm"], dqt, dk, dv, dproj, tr)
    dproj, small["sgu_norm"], small["w_spatial"], small["b_spatial"], dga = _sgu_backward(
        sv["proj"], sm["sgu_norm"], sm["w_spatial"], sm["b_spatial"], ga, dy, dproj, tr)
    small["out_norm"] = jnp.concatenate([dga, dgb, dgc], axis=1)
    big["w_in"] = _matmul_tn(sv["hn"], dproj, _tile(s_len, 1024), 512, 2304, "grad_w_in")
    parts = _layer_parts(big)
    dhn, arrived_own = _matmul_nt(dproj, w["w_in"], tm, 512, "grad_attn_norm_in",
                                  (parts, [False] * len(parts)) if scatter_own else None)
    dh, small["attn_norm"] = _rms_backward(sv["h"], sm["attn_norm"], dhn, dh1, tr, "attn_norm_backward")
    return dh, parts, big["conv_w"], small, arrived, arrived_own if scatter_own else None


def _layer_small(params, layer):
    return dict(
        attn_norm=params["attn_norm"][layer][None, :], sgu_norm=params["sgu_norm"][layer],
        w_spatial=params["w_spatial"][layer], b_spatial=params["b_spatial"][layer][:, :, None],
        conv_b=params["conv_b"][layer][None, :], kv_norm=params["kv_norm"][layer][None, :],
        q_nope_norm=params["q_nope_norm"][layer][None, :], q_rope_norm=params["q_rope_norm"][layer][None, :],
        k_nope_norm=params["k_nope_norm"][layer][None, :], k_rope_norm=params["k_rope_norm"][layer][None, :],
        out_norm=params["out_norm"][layer][None, :], ple_norm=params["ple_norm"][layer][None, :])


BIG = ("w_in", "w_ukv", "w_out", "w_ple_gate", "w_ple_proj")


def _assemble_w_in(g_in):
    return _permute_in(g_in.transpose(1, 0, 2).reshape(g_in.shape[1], IN_WIDTH))


def _assemble_rest(gathered):
    g_ukv, g_out, g_gate, g_proj = gathered
    w_ukv = g_ukv.reshape(N_DEV, KV_RANK, 2, HEAD).transpose(1, 2, 0, 3).reshape(KV_RANK, 2 * C_HEADS * HEAD)
    return dict(w_ukv=w_ukv, w_out=g_out.reshape(D_MODEL, D_MODEL), w_ple_gate=g_gate.reshape(D_MODEL, D_MODEL),
                w_ple_proj=g_proj.transpose(1, 0, 2).reshape(PLE_DIM, D_MODEL))


def _layer_parts(big):
    g_in = big["w_in"]
    return [
        _unpermute_in(g_in).reshape(g_in.shape[0], N_DEV, -1).transpose(1, 2, 0).astype(BF16),
        big["w_ukv"].reshape(KV_RANK, 2, N_DEV, HEAD).transpose(2, 0, 1, 3).reshape(N_DEV, KV_RANK, 2 * HEAD).astype(BF16),
        big["w_out"].reshape(N_DEV, -1, D_MODEL).astype(BF16),
        big["w_ple_gate"].reshape(N_DEV, -1, D_MODEL).astype(BF16),
        big["w_ple_proj"].reshape(PLE_DIM, N_DEV, -1).transpose(1, 0, 2).astype(BF16)]


def _step_local(xs, ps, pos, target, shards, conv_w, params):
    inv = 1.0 / (ROPE_BASE ** (jnp.arange(0, ROPE, 2, dtype=F32) / ROPE))
    ang = pos.astype(F32)[:, None] * inv
    cos, sin = jnp.cos(ang), jnp.sin(ang)
    cosf = jnp.concatenate([cos, cos], axis=-1)
    sins = jnp.concatenate([-sin, sin], axis=-1)

    def gather_of(names, layer):
        return [shards[n][layer] for n in names], [True] * len(names)

    h = xs
    saved, weights = [], []
    smalls = [_layer_small(params, layer) for layer in range(DEPTH)]
    (first_w_in,) = _exchange(*gather_of(BIG[:1], 0), "gather_first_w_in")
    w = dict(w_in=_assemble_w_in(first_w_in), conv_w=conv_w[0])
    for layer in range(DEPTH):
        comm = gather_of(BIG, layer + 1) if layer + 1 < DEPTH else None
        comm_rest = gather_of(BIG[1:], 0) if layer == 0 else None
        h, sv, w, arrived = _layer_forward(h, ps[layer], cosf, sins, w, smalls[layer], comm, comm_rest)
        saved.append(sv)
        weights.append(w)
        if comm is not None:
            w = dict(w_in=_assemble_w_in(arrived[0]), conv_w=conv_w[layer + 1], **_assemble_rest(arrived[1:]))
    dh, loss = _loss_grad(h, target, _tile(h.shape[0], 512))
    received, conv_grads, small_grads = [None] * DEPTH, [None] * DEPTH, [None] * DEPTH
    comm = None
    for layer in reversed(range(DEPTH)):
        dh, parts, conv_grads[layer], small_grads[layer], arrived, arrived_own = _layer_backward(
            dh, ps[layer], cosf, sins, weights[layer], smalls[layer], saved[layer], comm, layer == 0)
        if comm is not None:
            received[layer + 1] = arrived
        comm = (parts, [False] * len(parts))
    received[0] = arrived_own
    return loss, dh, received, conv_grads, small_grads


def kernel(x, p, positions, attn_norm, w_in, sgu_norm, w_spatial, b_spatial, conv_w, conv_b, kv_norm, w_ukv, q_nope_norm, q_rope_norm, k_nope_norm, k_rope_norm, out_norm, w_out, ple_norm, w_ple_gate, w_ple_proj, loss_target, m_attn_norm, m_w_in, m_sgu_norm, m_w_spatial, m_b_spatial, m_conv_w, m_conv_b, m_kv_norm, m_w_ukv, m_q_nope_norm, m_q_rope_norm, m_k_nope_norm, m_k_rope_norm, m_out_norm, m_w_out, m_ple_norm, m_w_ple_gate, m_w_ple_proj, v_attn_norm, v_w_in, v_sgu_norm, v_w_spatial, v_b_spatial, v_conv_w, v_conv_b, v_kv_norm, v_w_ukv, v_q_nope_norm, v_q_rope_norm, v_k_nope_norm, v_k_rope_norm, v_out_norm, v_w_out, v_ple_norm, v_w_ple_gate, v_w_ple_proj):
    order = ("attn_norm", "w_in", "sgu_norm", "w_spatial", "b_spatial", "conv_w", "conv_b", "kv_norm", "w_ukv",
             "q_nope_norm", "q_rope_norm", "k_nope_norm", "k_rope_norm", "out_norm", "w_out", "ple_norm", "w_ple_gate",
             "w_ple_proj")
    wts = dict(zip(order, (attn_norm, w_in, sgu_norm, w_spatial, b_spatial, conv_w, conv_b, kv_norm, w_ukv, q_nope_norm,
                           q_rope_norm, k_nope_norm, k_rope_norm, out_norm, w_out, ple_norm, w_ple_gate, w_ple_proj)))
    mom = dict(zip(order, (m_attn_norm, m_w_in, m_sgu_norm, m_w_spatial, m_b_spatial, m_conv_w, m_conv_b, m_kv_norm, m_w_ukv,
                           m_q_nope_norm, m_q_rope_norm, m_k_nope_norm, m_k_rope_norm, m_out_norm, m_w_out, m_ple_norm,
                           m_w_ple_gate, m_w_ple_proj)))
    var = dict(zip(order, (v_attn_norm, v_w_in, v_sgu_norm, v_w_spatial, v_b_spatial, v_conv_w, v_conv_b, v_kv_norm, v_w_ukv,
                           v_q_nope_norm, v_q_rope_norm, v_k_nope_norm, v_k_rope_norm, v_out_norm, v_w_out, v_ple_norm,
                           v_w_ple_gate, v_w_ple_proj)))

    conv_shard = wts["conv_w"]
    (conv_all,) = _exchange([conv_shard.reshape(-1, 128)], [True], "gather_conv_w")
    conv_full = conv_all.reshape(N_DEV, DEPTH, 3, -1).transpose(1, 2, 0, 3).reshape(DEPTH, 3, -1)
    shards = {n: wts[n].astype(BF16) for n in BIG}
    loss_part, grad_x, received, conv_grads, small_grads = _step_local(
        x[0], p[:, 0], positions[0], loss_target[0], shards, conv_full, wts)
    loss = lax.psum(loss_part[0, 0], ("x", "y", "c"))

    def small_grad(name):
        g = jnp.stack([sg[name] for sg in small_grads])
        return g.reshape(wts[name].shape)

    conv_grad = jnp.stack(conv_grads)
    like = [wts[n] for n in SMALL] + [conv_grad]
    packed = _pack([small_grad(n) for n in SMALL] + [conv_grad])
    (small_parts,) = _exchange([packed], [True], "gather_small_grads")
    filler = [jnp.zeros_like(conv_grad), jnp.zeros_like(conv_grad), jnp.ones_like(conv_grad)]
    small_out = _sum_adam(small_parts, *(_pack([src[n] for n in SMALL] + [fill])[None] for src, fill in zip((wts, mom, var), filler)),
                          0, None, "adam_small")
    unpacked = [_unpack(o[0], like) for o in small_out]
    results = {n: vals for n, vals in zip(SMALL, zip(*[u[:-1] for u in unpacked]))}
    me = 4 * lax.axis_index("x") + 2 * lax.axis_index("y") + lax.axis_index("c")
    width = conv_shard.shape[2]
    conv_local = lax.dynamic_slice_in_dim(unpacked[0][-1], me * width, width, axis=2)
    as_slab = (lambda t: t.reshape(1, -1, width))
    conv_out = _sum_adam(as_slab(conv_local), as_slab(conv_shard), as_slab(mom["conv_w"]), as_slab(var["conv_w"]), 0, None,
                         "adam_conv_w")
    results["conv_w"] = tuple(o.reshape(conv_shard.shape) for o in conv_out)

    for j, name in enumerate(BIG):
        view = (lambda t: jnp.swapaxes(t, 1, 2)) if name == "w_in" else (lambda t: t)
        outs = None
        for layer in range(DEPTH):
            outs = _sum_adam(received[layer][j], view(wts[name]), view(mom[name]), view(var[name]), layer, outs, "adam_" + name)
        results[name] = tuple(view(o) for o in outs)

    grads, deltas, new_m, new_v = ([results[n][j] for n in order] for j in range(4))
    return (loss, grad_x[None], *grads, *deltas, *new_m, *new_v)
```

```python
import functools

import jax
import jax.numpy as jnp
from jax import lax
from jax.experimental import pallas as pl
from jax.experimental.pallas import tpu as pltpu

F32 = jnp.float32
BF16 = jnp.bfloat16

N_DEV = 8
DEPTH = 4
D_MODEL = 2048
EPS = 1e-6
CHUNK = 128
A_HEADS = 4
HEAD = 128
ROPE = 64
HALF = ROPE // 2
C_HEADS = 8
KV_RANK = 512
PLE_DIM = 256
ROPE_BASE = 10000.0
IN_WIDTH = 6720
QK = HEAD + ROPE
SCALE = QK ** -0.5
LOG2_E = 1.4426950408889634
LN_2 = 0.6931471805599453
Q_SCALE = SCALE * LOG2_E
HALO = 8

O_B = 0
W_B = 2048
O_CZ = 2048
W_CZ = 1024
O_A = 3072
W_A = 1536
O_M = 4608
W_M = 2304
M_QN, M_QR, M_CKV, M_KR = 0, 1024, 1536, 2048
PROJ_W = 6912

ADAM_LR = 0.001
ADAM_B1 = 0.9
ADAM_B2 = 0.999
ADAM_EPS = 1e-08
ADAM_WD = 0.01
ADAM_STEP = 10

VMEM_LIMIT = 56 * 1024 * 1024

_NT = (((1,), (1,)), ((), ()))
_TN = (((0,), (0,)), ((), ()))


def _params(*sem):
    return pltpu.CompilerParams(dimension_semantics=sem, vmem_limit_bytes=VMEM_LIMIT)


@jax.custom_vjp
def _bdot(a, b):
    return jnp.dot(a.astype(BF16), b.astype(BF16), preferred_element_type=F32)


def _bdot_fwd(a, b):
    return _bdot(a, b), (a, b)


def _bdot_bwd(res, g):
    a, b = res
    gb = g.astype(BF16)
    da = lax.dot_general(gb, b.astype(BF16), _NT, preferred_element_type=F32)
    db = lax.dot_general(a.astype(BF16), gb, _TN, preferred_element_type=F32)
    return da.astype(a.dtype), db.astype(b.dtype)


_bdot.defvjp(_bdot_fwd, _bdot_bwd)


@functools.partial(jax.custom_vjp, nondiff_argnums=(1,))
def _split(x, n):
    w = x.shape[-1] // n
    return tuple(x[:, i * w:(i + 1) * w] for i in range(n))


def _split_fwd(x, n):
    return _split(x, n), None


def _split_bwd(n, _, gs):
    return (jnp.concatenate(gs, axis=-1),)


_split.defvjp(_split_fwd, _split_bwd)


@functools.partial(jax.custom_vjp, nondiff_argnums=(1,))
def _shift_rows(x, k):
    return pltpu.roll(x, k % x.shape[0], 0)


def _shift_rows_fwd(x, k):
    return _shift_rows(x, k), None


def _shift_rows_bwd(k, _, g):
    return (_shift_rows(g, -k),)


_shift_rows.defvjp(_shift_rows_fwd, _shift_rows_bwd)


@jax.custom_vjp
def _swap_halves(x):
    h = x.shape[-1] // 2
    return jnp.concatenate([x[:, h:], x[:, :h]], axis=-1)


def _swap_halves_fwd(x):
    return _swap_halves(x), None


def _swap_halves_bwd(_, g):
    return (_swap_halves(g),)


_swap_halves.defvjp(_swap_halves_fwd, _swap_halves_bwd)


def _rms(x, g):
    return x * lax.rsqrt(jnp.mean(x * x, axis=-1, keepdims=True) + EPS) * g


def _rope(x, cosf, sins):
    return x * cosf + _swap_halves(x) * sins


def _sgu_chunk(u, v, z, gain, ws, bs, ga):
    ys = []
    for h in range(A_HEADS):
        vn = _rms(v[h], gain[h])
        s = _bdot(ws[h], vn) + bs[h]
        ys.append(u[h] * s * jax.nn.silu(z[h]))
    ss = sum(jnp.sum(y * y, axis=-1, keepdims=True) for y in ys) * (1.0 / (A_HEADS * HEAD))
    r = lax.rsqrt(ss + EPS)
    return tuple(ys[h] * r * ga[h] for h in range(A_HEADS))


def _conv_tile(bb, bc, bh, bz, w0, w1, w2, cb, gb, w0h, w1h, w2h, cbh, gbh, valid, core):
    t = jnp.where(valid, bc * bh, 0.0)
    y = (jnp.where(core, cb, cbh)
         + _shift_rows(t, 1) * jnp.where(core, w0, w0h)
         + t * jnp.where(core, w1, w1h)
         + _shift_rows(t, -1) * jnp.where(core, w2, w2h))
    return _rms(bb * y * jax.nn.silu(bz), jnp.where(core, gb, gbh))


def _cgate_tile(o, cz, gc):
    return _rms(o * jax.nn.silu(cz), gc)


def _mla_tile(qn, qr, ckv, kr, cosf, sins, wukv, kvg, qng, qrg, kng, krg):
    kv = _split(_bdot(_rms(ckv, kvg), wukv), 2 * C_HEADS)
    k_r = _rope(_rms(kr, krg), cosf, sins)
    qn_h = _split(qn, C_HEADS)
    qr_h = _split(qr, C_HEADS)
    q, k, v = [], [], []
    for h in range(C_HEADS):
        q.append(jnp.concatenate([_rms(qn_h[h], qng), _rope(_rms(qr_h[h], qrg), cosf, sins)], axis=-1) * Q_SCALE)
        k.append(jnp.concatenate([_rms(kv[h], kng), k_r], axis=-1))
        v.append(kv[C_HEADS + h])
    return tuple(q), tuple(k), tuple(v)


def _norm_matmul(h, gain, w, tm, tn, comm=None):
    s_len, k = h.shape
    n = w.shape[1]
    c_ops, c_in_specs, c_shapes, c_out_specs, c_sems, c_begin, c_end = _riding_exchange(comm, 2)
    n_c = len(c_ops)

    def body(h_ref, g_ref, w_ref, *rest):
        c_ins, (o_ref, hn_ref), c_outs, sems = rest[:n_c], rest[n_c:n_c + 2], rest[n_c + 2:2 * n_c + 2], rest[2 * n_c + 2:]
        c_begin(c_ins, c_outs, sems)

        @pl.when(pl.program_id(1) == 0)
        def _():
            hn_ref[...] = _rms(h_ref[...], g_ref[...]).astype(BF16)

        o_ref[...] = jnp.dot(hn_ref[...], w_ref[...], preferred_element_type=F32).astype(BF16)
        c_end(c_ins, c_outs, sems)

    outs = pl.pallas_call(
        body, name="norm_matmul",
        out_shape=(jax.ShapeDtypeStruct((s_len, n), BF16), jax.ShapeDtypeStruct((s_len, k), BF16), *c_shapes),
        grid=(s_len // tm, n // tn),
        in_specs=[pl.BlockSpec((tm, k), lambda i, j: (i, 0)), pl.BlockSpec((1, k), lambda i, j: (0, 0)),
                  pl.BlockSpec((k, tn), lambda i, j: (0, j))] + c_in_specs,
        out_specs=(pl.BlockSpec((tm, tn), lambda i, j: (i, j)), pl.BlockSpec((tm, k), lambda i, j: (i, 0)), *c_out_specs),
        scratch_shapes=c_sems,
        compiler_params=_params("arbitrary", "arbitrary") if comm else _params("parallel", "arbitrary"),
    )(h, gain, w, *c_ops)
    return outs[0], outs[1], outs[2:]


def _out_matmul(h, y, w, tm, tn):
    s_len, n = h.shape
    k = y.shape[1]

    def body(h_ref, y_ref, w_ref, o_ref):
        o_ref[...] = h_ref[...] + jnp.dot(y_ref[...], w_ref[...], preferred_element_type=F32)

    return pl.pallas_call(
        body, name="out_matmul",
        out_shape=jax.ShapeDtypeStruct((s_len, n), F32),
        grid=(s_len // tm, n // tn),
        in_specs=[pl.BlockSpec((tm, tn), lambda i, j: (i, j)), pl.BlockSpec((tm, k), lambda i, j: (i, 0)),
                  pl.BlockSpec((k, tn), lambda i, j: (0, j))],
        out_specs=pl.BlockSpec((tm, tn), lambda i, j: (i, j)),
        compiler_params=_params("parallel", "parallel"),
    )(h, y, w)


def _ple_forward(h1, gain, p, wg, wp, tm, tn):
    s_len, d = h1.shape
    kp = p.shape[1]

    def body(hrow_ref, hcol_ref, g_ref, p_ref, wg_ref, wp_ref, o_ref, n1_ref, gate_ref, pp_ref):
        @pl.when(pl.program_id(1) == 0)
        def _():
            n1_ref[...] = _rms(hrow_ref[...], g_ref[...]).astype(BF16)

        gate = jax.nn.sigmoid(jnp.dot(n1_ref[...], wg_ref[...], preferred_element_type=F32))
        pp = jnp.dot(p_ref[...].astype(BF16), wp_ref[...], preferred_element_type=F32)
        o_ref[...] = hcol_ref[...] + gate * pp
        gate_ref[...] = gate.astype(BF16)
        pp_ref[...] = pp.astype(BF16)

    col = pl.BlockSpec((tm, tn), lambda i, j: (i, j))
    return pl.pallas_call(
        body, name="ple_forward",
        out_shape=(jax.ShapeDtypeStruct((s_len, d), F32), jax.ShapeDtypeStruct((s_len, d), BF16),
                   jax.ShapeDtypeStruct((s_len, d), BF16), jax.ShapeDtypeStruct((s_len, d), BF16)),
        grid=(s_len // tm, d // tn),
        in_specs=[pl.BlockSpec((tm, d), lambda i, j: (i, 0)), col, pl.BlockSpec((1, d), lambda i, j: (0, 0)),
                  pl.BlockSpec((tm, kp), lambda i, j: (i, 0)), pl.BlockSpec((d, tn), lambda i, j: (0, j)),
                  pl.BlockSpec((kp, tn), lambda i, j: (0, j))],
        out_specs=(col, pl.BlockSpec((tm, d), lambda i, j: (i, 0)), col, col),
        compiler_params=_params("parallel", "arbitrary"),
    )(h1, h1, gain, p, wg, wp)


def _matmul_nt(a, b, tm, tk, name, comm=None):
    m, n = a.shape
    k = b.shape[0]
    c_ops, c_in_specs, c_shapes, c_out_specs, c_sems, c_begin, c_end = _riding_exchange(comm, 2)
    n_c = len(c_ops)

    def body(a_ref, b_ref, *rest):
        c_ins, o_ref, c_outs, sems = rest[:n_c], rest[n_c], rest[n_c + 1:2 * n_c + 1], rest[2 * n_c + 1:]
        c_begin(c_ins, c_outs, sems)
        o_ref[...] = lax.dot_general(a_ref[...].astype(BF16), b_ref[...].astype(BF16), _NT, preferred_element_type=F32)
        c_end(c_ins, c_outs, sems)

    outs = pl.pallas_call(
        body, name=name,
        out_shape=(jax.ShapeDtypeStruct((m, k), F32), *c_shapes),
        grid=(m // tm, k // tk),
        in_specs=[pl.BlockSpec((tm, n), lambda i, j: (i, 0)), pl.BlockSpec((tk, n), lambda i, j: (j, 0))] + c_in_specs,
        out_specs=(pl.BlockSpec((tm, tk), lambda i, j: (i, j)), *c_out_specs),
        scratch_shapes=c_sems,
        compiler_params=_params("arbitrary", "arbitrary") if comm else _params("parallel", "parallel"),
    )(a, b, *c_ops)
    return outs[0], outs[1:]


def _matmul_tn(a, b, tm, tk, tn, name):
    m, k = a.shape
    n = b.shape[1]
    n_m = m // tm

    def body(a_ref, b_ref, o_ref, acc_ref):
        part = lax.dot_general(a_ref[...].astype(BF16), b_ref[...].astype(BF16), _TN, preferred_element_type=F32)
        _acc(acc_ref, part, pl.program_id(2) == 0)

        @pl.when(pl.program_id(2) == n_m - 1)
        def _():
            o_ref[...] = acc_ref[...].astype(BF16)

    return pl.pallas_call(
        body, name=name,
        out_shape=jax.ShapeDtypeStruct((k, n), BF16),
        grid=(k // tk, n // tn, n_m),
        in_specs=[pl.BlockSpec((tm, tk), lambda kk, nn, mm: (mm, kk)), pl.BlockSpec((tm, tn), lambda kk, nn, mm: (mm, nn))],
        out_specs=pl.BlockSpec((tk, tn), lambda kk, nn, mm: (kk, nn)),
        scratch_shapes=[pltpu.VMEM((tk, tn), F32)],
        compiler_params=_params("parallel", "parallel", "arbitrary"),
    )(a, b)


def _acc(ref, val, first):
    @pl.when(first)
    def _():
        ref[...] = val

    @pl.when(jnp.logical_not(first))
    def _():
        ref[...] += val


def _loss_grad(h, target, tm):
    s_len, d = h.shape

    def body(h_ref, t_ref, dh_ref, loss_ref):
        e = h_ref[...] - t_ref[...]
        dh_ref[...] = e * (1.0 / d)
        part = jnp.sum(jnp.sum(e * e, axis=-1, keepdims=True), axis=0, keepdims=True) * (0.5 / d)
        _acc(loss_ref, jnp.broadcast_to(part, loss_ref.shape), pl.program_id(0) == 0)

    row = pl.BlockSpec((tm, d), lambda i: (i, 0))
    return pl.pallas_call(
        body, name="loss_grad",
        out_shape=(jax.ShapeDtypeStruct((s_len, d), F32), jax.ShapeDtypeStruct((1, 128), F32)),
        grid=(s_len // tm,),
        in_specs=[row, row],
        out_specs=(row, pl.BlockSpec((1, 128), lambda i: (0, 0))),
        compiler_params=_params("arbitrary"),
    )(h, target)


def _rms_backward(x, gain, dn, dres, tm, name):
    s_len, d = x.shape

    def body(x_ref, g_ref, dn_ref, dres_ref, dx_ref, dg_ref):
        _, vjp = jax.vjp(_rms, x_ref[...], g_ref[...])
        dx, dg = vjp(dn_ref[...])
        dx_ref[...] = dres_ref[...] + dx
        _acc(dg_ref, dg, pl.program_id(0) == 0)

    row = pl.BlockSpec((tm, d), lambda i: (i, 0))
    vec = pl.BlockSpec((1, d), lambda i: (0, 0))
    return pl.pallas_call(
        body, name=name,
        out_shape=(jax.ShapeDtypeStruct((s_len, d), F32), jax.ShapeDtypeStruct((1, d), F32)),
        grid=(s_len // tm,),
        in_specs=[row, vec, row, row],
        out_specs=(row, vec),
        compiler_params=_params("arbitrary"),
    )(x, gain, dn, dres)


def _ple_backward(dh2, gate, pp, wg, h1, gain, tm):
    s_len, d = dh2.shape

    def body(dh_ref, gate_ref, pp_ref, wg_ref, h1_ref, g_ref, dh1_ref, dgain_ref, dgp_ref, dpp_ref):
        dh = dh_ref[...]
        gate = gate_ref[...].astype(F32)
        dgp = (dh * pp_ref[...].astype(F32) * gate * (1.0 - gate)).astype(BF16)
        dgp_ref[...] = dgp
        dpp_ref[...] = (dh * gate).astype(BF16)
        dn = lax.dot_general(dgp, wg_ref[...], _NT, preferred_element_type=F32)
        _, vjp = jax.vjp(_rms, h1_ref[...], g_ref[...])
        dx, dgain = vjp(dn)
        dh1_ref[...] = dh + dx
        _acc(dgain_ref, dgain, pl.program_id(0) == 0)

    row = pl.BlockSpec((tm, d), lambda i: (i, 0))
    vec = pl.BlockSpec((1, d), lambda i: (0, 0))
    return pl.pallas_call(
        body, name="ple_backward",
        out_shape=(jax.ShapeDtypeStruct((s_len, d), F32), jax.ShapeDtypeStruct((1, d), F32),
                   jax.ShapeDtypeStruct((s_len, d), BF16), jax.ShapeDtypeStruct((s_len, d), BF16)),
        grid=(s_len // tm,),
        in_specs=[row, row, row, pl.BlockSpec((d, d), lambda i: (0, 0)), row, vec],
        out_specs=(row, vec, row, row),
        compiler_params=_params("arbitrary"),
    )(dh2, gate, pp, wg, h1, gain)


def _sgu_in_specs(tm):
    return [pl.BlockSpec((tm, W_A), lambda i: (i, O_A // W_A)),
            pl.BlockSpec((A_HEADS, HEAD), lambda i: (0, 0)), pl.BlockSpec((A_HEADS, CHUNK, CHUNK), lambda i: (0, 0, 0)),
            pl.BlockSpec((A_HEADS, CHUNK, 1), lambda i: (0, 0, 0)), pl.BlockSpec((1, 512), lambda i: (0, 0))]


def _sgu_load(a_ref, gain_ref, ws_ref, bs_ref, ga_ref, c):
    rows = slice(c * CHUNK, (c + 1) * CHUNK)
    heads = range(A_HEADS)
    u = tuple(a_ref[rows, h * HEAD:(h + 1) * HEAD].astype(F32) for h in heads)
    v = tuple(a_ref[rows, 512 + h * HEAD:512 + (h + 1) * HEAD].astype(F32) for h in heads)
    z = tuple(a_ref[rows, 1024 + h * HEAD:1024 + (h + 1) * HEAD].astype(F32) for h in heads)
    gain = tuple(gain_ref[h:h + 1, :] for h in heads)
    ws = tuple(ws_ref[h] for h in heads)
    bs = tuple(bs_ref[h] for h in heads)
    ga = tuple(ga_ref[:, h * HEAD:(h + 1) * HEAD] for h in heads)
    return u, v, z, gain, ws, bs, ga


def _sgu_forward(proj, gain, ws, bs, ga, tm):
    s_len = proj.shape[0]

    def body(a_ref, gain_ref, ws_ref, bs_ref, ga_ref, o_ref):
        for c in range(tm // CHUNK):
            out = _sgu_chunk(*_sgu_load(a_ref, gain_ref, ws_ref, bs_ref, ga_ref, c))
            for h in range(A_HEADS):
                o_ref[c * CHUNK:(c + 1) * CHUNK, h * HEAD:(h + 1) * HEAD] = out[h].astype(BF16)

    return pl.pallas_call(
        body, name="sgu_forward",
        out_shape=jax.ShapeDtypeStruct((s_len, D_MODEL), BF16),
        grid=(s_len // tm,),
        in_specs=_sgu_in_specs(tm),
        out_specs=pl.BlockSpec((tm, 512), lambda i: (i, 0)),
        compiler_params=_params("parallel"),
    )(proj, gain, ws, bs, ga)


def _sgu_backward(proj, gain, ws, bs, ga, dy, dproj, tm):
    s_len = proj.shape[0]

    def body(a_ref, gain_ref, ws_ref, bs_ref, ga_ref, dy_ref, _, da_ref, dgain_ref, dws_ref, dbs_ref, dga_ref):
        tot = None
        for c in range(tm // CHUNK):
            args = _sgu_load(a_ref, gain_ref, ws_ref, bs_ref, ga_ref, c)
            _, vjp = jax.vjp(_sgu_chunk, *args)
            rows = slice(c * CHUNK, (c + 1) * CHUNK)
            du, dv, dz, dgain, dws, dbs, dga = vjp(tuple(dy_ref[rows, h * HEAD:(h + 1) * HEAD] for h in range(A_HEADS)))
            for h in range(A_HEADS):
                da_ref[rows, h * HEAD:(h + 1) * HEAD] = du[h].astype(BF16)
                da_ref[rows, 512 + h * HEAD:512 + (h + 1) * HEAD] = dv[h].astype(BF16)
                da_ref[rows, 1024 + h * HEAD:1024 + (h + 1) * HEAD] = dz[h].astype(BF16)
            part = (dgain, dws, dbs, dga)
            tot = part if tot is None else jax.tree.map(jnp.add, tot, part)
        dgain, dws, dbs, dga = tot
        first = pl.program_id(0) == 0
        _acc(dgain_ref, jnp.concatenate(dgain, axis=0), first)
        _acc(dga_ref, jnp.concatenate(dga, axis=-1), first)
        for h in range(A_HEADS):
            _acc(dws_ref.at[h], dws[h], first)
            _acc(dbs_ref.at[h], dbs[h], first)

    small = [pl.BlockSpec((A_HEADS, HEAD), lambda i: (0, 0)), pl.BlockSpec((A_HEADS, CHUNK, CHUNK), lambda i: (0, 0, 0)),
             pl.BlockSpec((A_HEADS, CHUNK, 1), lambda i: (0, 0, 0)), pl.BlockSpec((1, 512), lambda i: (0, 0))]
    return pl.pallas_call(
        body, name="sgu_backward",
        out_shape=(jax.ShapeDtypeStruct(dproj.shape, BF16),
                   jax.ShapeDtypeStruct((A_HEADS, HEAD), F32), jax.ShapeDtypeStruct((A_HEADS, CHUNK, CHUNK), F32),
                   jax.ShapeDtypeStruct((A_HEADS, CHUNK, 1), F32), jax.ShapeDtypeStruct((1, 512), F32)),
        grid=(s_len // tm,),
        in_specs=_sgu_in_specs(tm) + [pl.BlockSpec((tm, 512), lambda i: (i, 0)), pl.BlockSpec(memory_space=pl.ANY)],
        out_specs=(pl.BlockSpec((tm, W_A), lambda i: (i, O_A // W_A)), *small),
        input_output_aliases={6: 0},
        compiler_params=_params("arbitrary"),
    )(proj, gain, ws, bs, ga, dy, dproj)


def _halo_specs(tm, width, col, n_rows):
    per = tm // HALO
    last = n_rows // HALO - 1
    return [pl.BlockSpec((HALO, width), lambda i: (jnp.maximum(i * per - 1, 0), col)),
            pl.BlockSpec((tm, width), lambda i: (i, col)),
            pl.BlockSpec((HALO, width), lambda i: (jnp.minimum((i + 1) * per, last), col))]


def _conv_masks(tm, s_len):
    r = lax.broadcasted_iota(jnp.int32, (tm + 2 * HALO, 1), 0)
    g = pl.program_id(0) * tm - HALO + r
    return (g >= 0) & (g < s_len), (r >= HALO) & (r < HALO + tm)


def _conv_inputs(b_refs, cw_ref, cb_ref, gb_ref):
    ext = jnp.concatenate([r[...] for r in b_refs], axis=0).astype(F32)
    bb, bc, bh, bz = (ext[:, j * 512:(j + 1) * 512] for j in range(4))
    prm = (cw_ref[0:1, :], cw_ref[1:2, :], cw_ref[2:3, :], cb_ref[...], gb_ref[...])
    return (bb, bc, bh, bz), prm


def _conv_forward(proj, cw, cb, gb, y, tm):
    s_len = proj.shape[0]

    def body(p0, p1, p2, cw_ref, cb_ref, gb_ref, _, o_ref):
        acts, prm = _conv_inputs((p0, p1, p2), cw_ref, cb_ref, gb_ref)
        valid, core = _conv_masks(tm, s_len)
        out = _conv_tile(*acts, *prm, *prm, valid, core)
        o_ref[...] = out[HALO:HALO + tm].astype(BF16)

    vec = pl.BlockSpec((1, 512), lambda i: (0, 0))
    return pl.pallas_call(
        body, name="conv_forward",
        out_shape=jax.ShapeDtypeStruct(y.shape, BF16),
        grid=(s_len // tm,),
        in_specs=_halo_specs(tm, W_B, O_B // W_B, s_len) + [pl.BlockSpec((3, 512), lambda i: (0, 0)), vec, vec,
                                                             pl.BlockSpec(memory_space=pl.ANY)],
        out_specs=pl.BlockSpec((tm, 512), lambda i: (i, 1)),
        input_output_aliases={6: 0},
        compiler_params=_params("parallel"),
    )(proj, proj, proj, cw, cb, gb, y)


def _conv_backward(proj, cw, cb, gb, dy, tm):
    s_len = proj.shape[0]

    def body(p0, p1, p2, cw_ref, cb_ref, gb_ref, d0, d1, d2, db_ref, dcw_ref, dcb_ref, dgb_ref):
        acts, prm = _conv_inputs((p0, p1, p2), cw_ref, cb_ref, gb_ref)
        valid, core = _conv_masks(tm, s_len)
        _, vjp = jax.vjp(lambda a, p: _conv_tile(*a, *p, *prm, valid, core), acts, prm)
        dy_ext = jnp.where(valid, jnp.concatenate([d0[...], d1[...], d2[...]], axis=0), 0.0)
        dacts, dprm = vjp(dy_ext)
        for j in range(4):
            db_ref[:, j * 512:(j + 1) * 512] = dacts[j][HALO:HALO + tm].astype(BF16)
        first = pl.program_id(0) == 0
        _acc(dcw_ref, jnp.concatenate(dprm[0:3], axis=0), first)
        _acc(dcb_ref, dprm[3], first)
        _acc(dgb_ref, dprm[4], first)

    vec = pl.BlockSpec((1, 512), lambda i: (0, 0))
    mat = pl.BlockSpec((3, 512), lambda i: (0, 0))
    return pl.pallas_call(
        body, name="conv_backward",
        out_shape=(jax.ShapeDtypeStruct((s_len, PROJ_W), BF16), jax.ShapeDtypeStruct((3, 512), F32),
                   jax.ShapeDtypeStruct((1, 512), F32), jax.ShapeDtypeStruct((1, 512), F32)),
        grid=(s_len // tm,),
        in_specs=_halo_specs(tm, W_B, O_B // W_B, s_len) + [mat, vec, vec] + _halo_specs(tm, 512, 1, s_len),
        out_specs=(pl.BlockSpec((tm, W_B), lambda i: (i, O_B // W_B)), mat, vec, vec),
        compiler_params=_params("arbitrary"),
    )(proj, proj, proj, cw, cb, gb, dy, dy, dy)


def _cgate_forward(o, proj, gc, y, tm):
    s_len = o.shape[0]

    def body(o_ref, cz_ref, gc_ref, _, y_ref):
        y_ref[...] = _cgate_tile(o_ref[...], cz_ref[...].astype(F32), gc_ref[...]).astype(BF16)

    return pl.pallas_call(
        body, name="cgate_forward",
        out_shape=jax.ShapeDtypeStruct(y.shape, BF16),
        grid=(s_len // tm,),
        in_specs=[pl.BlockSpec((tm, W_CZ), lambda i: (i, 0)), pl.BlockSpec((tm, W_CZ), lambda i: (i, O_CZ // W_CZ)),
                  pl.BlockSpec((1, W_CZ), lambda i: (0, 0)), pl.BlockSpec(memory_space=pl.ANY)],
        out_specs=pl.BlockSpec((tm, W_CZ), lambda i: (i, 1)),
        input_output_aliases={3: 0},
        compiler_params=_params("parallel"),
    )(o, proj, gc, y)


def _cgate_backward(o, proj, gc, dy, dproj, tm, stat_chunk):
    s_len = o.shape[0]
    per_stat = stat_chunk // tm

    def body(o_ref, cz_ref, gc_ref, dy_ref, _, dcz_ref, do_ref, dsum_ref, dgc_ref):
        o = o_ref[...]
        _, vjp = jax.vjp(_cgate_tile, o, cz_ref[...].astype(F32), gc_ref[...])
        do, dcz, dgc = vjp(dy_ref[...])
        dcz_ref[...] = dcz.astype(BF16)
        do_ref[...] = do.astype(BF16)
        ones = jnp.ones((8, HEAD), F32)
        for h in range(C_HEADS):
            cols = slice(h * HEAD, (h + 1) * HEAD)
            sums = lax.dot_general(ones, do[:, cols] * o[:, cols], _NT, precision=lax.Precision.HIGHEST,
                                   preferred_element_type=F32)
            dsum_ref[h, 0] = sums[0:1]
        _acc(dgc_ref, dgc, pl.program_id(0) == 0)

    row = pl.BlockSpec((tm, W_CZ), lambda i: (i, 0))
    vec = pl.BlockSpec((1, W_CZ), lambda i: (0, 0))
    return pl.pallas_call(
        body, name="cgate_backward",
        out_shape=(jax.ShapeDtypeStruct(dproj.shape, BF16), jax.ShapeDtypeStruct((s_len, W_CZ), BF16),
                   jax.ShapeDtypeStruct((C_HEADS, s_len // stat_chunk, 1, stat_chunk), F32), jax.ShapeDtypeStruct((1, W_CZ), F32)),
        grid=(s_len // tm,),
        in_specs=[row, pl.BlockSpec((tm, W_CZ), lambda i: (i, O_CZ // W_CZ)), vec,
                  pl.BlockSpec((tm, W_CZ), lambda i: (i, 1)), pl.BlockSpec(memory_space=pl.ANY)],
        out_specs=(pl.BlockSpec((tm, W_CZ), lambda i: (i, O_CZ // W_CZ)), row,
                   pl.BlockSpec((C_HEADS, 1, 1, tm), lambda i: (0, i // per_stat, 0, i % per_stat)), vec),
        input_output_aliases={4: 0},
        compiler_params=_params("arbitrary"),
    )(o, proj, gc, dy, dproj)


def _mla_small_specs():
    return [pl.BlockSpec((KV_RANK, 2 * C_HEADS * HEAD), lambda i: (0, 0)), pl.BlockSpec((1, KV_RANK), lambda i: (0, 0)),
            pl.BlockSpec((1, HEAD), lambda i: (0, 0)), pl.BlockSpec((1, ROPE), lambda i: (0, 0)),
            pl.BlockSpec((1, HEAD), lambda i: (0, 0)), pl.BlockSpec((1, ROPE), lambda i: (0, 0))]


def _mla_load(m_ref, cos_ref, sin_ref):
    qn = m_ref[:, M_QN:M_QN + C_HEADS * HEAD].astype(F32)
    qr = m_ref[:, M_QR:M_QR + C_HEADS * ROPE].astype(F32)
    ckv = m_ref[:, M_CKV:M_CKV + KV_RANK].astype(F32)
    kr = m_ref[:, M_KR:M_KR + ROPE].astype(F32)
    return qn, qr, ckv, kr, cos_ref[...], sin_ref[...]


def _mla_forward(proj, cosf, sins, wukv, kvg, qng, qrg, kng, krg, tm, kt_chunk, vt_chunk):
    s_len = proj.shape[0]

    def body(m_ref, cos_ref, sin_ref, w_ref, kvg_ref, qng_ref, qrg_ref, kng_ref, krg_ref, q_ref, k_ref, v_ref, kt_ref, vt_ref):
        q, k, v = _mla_tile(*_mla_load(m_ref, cos_ref, sin_ref), w_ref[...], kvg_ref[...], qng_ref[...], qrg_ref[...],
                            kng_ref[...], krg_ref[...])
        for h in range(C_HEADS):
            q_ref[h] = q[h].astype(BF16)
            k_ref[h] = k[h].astype(BF16)
            v_ref[h] = v[h].astype(BF16)
            kt_ref[h, 0] = jnp.concatenate([k[h][:, :HEAD].T, k[h][:, HEAD:].T], axis=0).astype(BF16)
            vt_ref[h, 0] = v[h].T.astype(BF16)

    rope_spec = pl.BlockSpec((tm, ROPE), lambda i: (i, 0))
    qk_spec = pl.BlockSpec((C_HEADS, tm, QK), lambda i: (0, i, 0))
    per_k, per_v = kt_chunk // tm, vt_chunk // tm
    return pl.pallas_call(
        body, name="mla_forward",
        out_shape=(jax.ShapeDtypeStruct((C_HEADS, s_len, QK), BF16), jax.ShapeDtypeStruct((C_HEADS, s_len, QK), BF16),
                   jax.ShapeDtypeStruct((C_HEADS, s_len, HEAD), BF16),
                   jax.ShapeDtypeStruct((C_HEADS, s_len // kt_chunk, QK, kt_chunk), BF16),
                   jax.ShapeDtypeStruct((C_HEADS, s_len // vt_chunk, HEAD, vt_chunk), BF16)),
        grid=(s_len // tm,),
        in_specs=[pl.BlockSpec((tm, W_M), lambda i: (i, O_M // W_M)), rope_spec, rope_spec] + _mla_small_specs(),
        out_specs=(qk_spec, qk_spec, pl.BlockSpec((C_HEADS, tm, HEAD), lambda i: (0, i, 0)),
                   pl.BlockSpec((C_HEADS, 1, QK, tm), lambda i: (0, i // per_k, 0, i % per_k)),
                   pl.BlockSpec((C_HEADS, 1, HEAD, tm), lambda i: (0, i // per_v, 0, i % per_v))),
        compiler_params=_params("parallel"),
    )(proj, cosf, sins, wukv, kvg, qng, qrg, kng, krg)


def _mla_backward(proj, cosf, sins, wukv, kvg, qng, qrg, kng, krg, dqt, dk, dv, dproj, tm):
    s_len = proj.shape[0]
    per_chunk = dqt.shape[3] // tm

    def body(m_ref, cos_ref, sin_ref, w_ref, kvg_ref, qng_ref, qrg_ref, kng_ref, krg_ref, dq_ref, dk_ref, dv_ref, _,
             dm_ref, dw_ref, dkvg_ref, dqng_ref, dqrg_ref, dkng_ref, dkrg_ref):
        qn, qr, ckv, kr, cosf_t, sins_t = _mla_load(m_ref, cos_ref, sin_ref)
        prm = (w_ref[...], kvg_ref[...], qng_ref[...], qrg_ref[...], kng_ref[...], krg_ref[...])
        _, vjp = jax.vjp(lambda a, p: _mla_tile(*a, cosf_t, sins_t, *p), (qn, qr, ckv, kr), prm)
        heads = range(C_HEADS)
        dacts, dprm = vjp((tuple(dq_ref[h, 0].T for h in heads), tuple(dk_ref[h] for h in heads), tuple(dv_ref[h] for h in heads)))
        dm_ref[:, M_QN:M_QN + C_HEADS * HEAD] = dacts[0].astype(BF16)
        dm_ref[:, M_QR:M_QR + C_HEADS * ROPE] = dacts[1].astype(BF16)
        dm_ref[:, M_CKV:M_CKV + KV_RANK] = dacts[2].astype(BF16)
        pad = jnp.zeros((tm, W_M - M_KR - ROPE), F32)
        dm_ref[:, M_KR:W_M] = jnp.concatenate([dacts[3], pad], axis=-1).astype(BF16)
        first = pl.program_id(0) == 0
        for ref, val in zip((dw_ref, dkvg_ref, dqng_ref, dqrg_ref, dkng_ref, dkrg_ref), dprm):
            _acc(ref, val.astype(F32), first)

    rope_spec = pl.BlockSpec((tm, ROPE), lambda i: (i, 0))
    qk_spec = pl.BlockSpec((C_HEADS, tm, QK), lambda i: (0, i, 0))
    small = _mla_small_specs()
    return pl.pallas_call(
        body, name="mla_backward",
        out_shape=(jax.ShapeDtypeStruct(dproj.shape, BF16), jax.ShapeDtypeStruct((KV_RANK, 2 * C_HEADS * HEAD), F32),
                   jax.ShapeDtypeStruct((1, KV_RANK), F32), jax.ShapeDtypeStruct((1, HEAD), F32),
                   jax.ShapeDtypeStruct((1, ROPE), F32), jax.ShapeDtypeStruct((1, HEAD), F32),
                   jax.ShapeDtypeStruct((1, ROPE), F32)),
        grid=(s_len // tm,),
        in_specs=[pl.BlockSpec((tm, W_M), lambda i: (i, O_M // W_M)), rope_spec, rope_spec] + small
                 + [pl.BlockSpec((C_HEADS, 1, QK, tm), lambda i: (0, i // per_chunk, 0, i % per_chunk)), qk_spec,
                    pl.BlockSpec((C_HEADS, tm, HEAD), lambda i: (0, i, 0)), pl.BlockSpec(memory_space=pl.ANY)],
        out_specs=(pl.BlockSpec((tm, W_M), lambda i: (i, O_M // W_M)), *small),
        input_output_aliases={12: 0},
        compiler_params=_params("arbitrary"),
    )(proj, cosf, sins, wukv, kvg, qng, qrg, kng, krg, dqt, dk, dv, dproj)


def _attention_forward(q, k, vt, tq, stat_chunk, comm=None):
    n_heads, s_len, _ = q.shape
    n_chunks, _, ck = vt.shape[1:]
    c_ops, c_in_specs, c_shapes, c_out_specs, c_sems, c_begin, c_end = _riding_exchange(comm, 2)
    n_c = len(c_ops)

    def body(q_ref, k_ref, vt_ref, *rest):
        c_ins, (o_ref, lse_ref), c_outs, sems = rest[:n_c], rest[n_c:n_c + 2], rest[n_c + 2:2 * n_c + 2], rest[2 * n_c + 2:]
        c_begin(c_ins, c_outs, sems)
        q_t = q_ref[0]

        def step(j, carry):
            m_old, l_old, acc = carry
            k_j = k_ref[0, pl.ds(pl.multiple_of(j * ck, ck), ck), :]
            s = lax.dot_general(k_j, q_t, _NT, preferred_element_type=F32)
            m_new = jnp.maximum(m_old, jnp.max(s, axis=0, keepdims=True))
            p = jnp.exp2(s - m_new)
            alpha = jnp.exp2(m_old - m_new)
            l_new = alpha * l_old + jnp.sum(p, axis=0, keepdims=True)
            acc = alpha * acc + jnp.dot(vt_ref[0, j], p.astype(BF16), preferred_element_type=F32)
            return m_new, l_new, acc

        init = (jnp.full((1, tq), -jnp.inf, F32), jnp.zeros((1, tq), F32), jnp.zeros((HEAD, tq), F32))
        m_fin, l_fin, acc = lax.fori_loop(0, n_chunks, step, init)
        o_ref[...] = (acc / l_fin).T
        lse_ref[0, 0] = m_fin + jnp.log2(l_fin)
        c_end(c_ins, c_outs, sems)

    per_stat = stat_chunk // tq
    outs = pl.pallas_call(
        body, name="attention_forward",
        out_shape=(jax.ShapeDtypeStruct((s_len, n_heads * HEAD), F32),
                   jax.ShapeDtypeStruct((n_heads, s_len // stat_chunk, 1, stat_chunk), F32), *c_shapes),
        grid=(n_heads, s_len // tq),
        in_specs=[pl.BlockSpec((1, tq, QK), lambda h, i: (h, i, 0)), pl.BlockSpec((1, s_len, QK), lambda h, i: (h, 0, 0)),
                  pl.BlockSpec((1, n_chunks, HEAD, ck), lambda h, i: (h, 0, 0, 0))] + c_in_specs,
        out_specs=(pl.BlockSpec((tq, HEAD), lambda h, i: (i, h)),
                   pl.BlockSpec((1, 1, 1, tq), lambda h, i: (h, i // per_stat, 0, i % per_stat)), *c_out_specs),
        scratch_shapes=c_sems,
        compiler_params=_params("arbitrary", "arbitrary") if comm else _params("parallel", "parallel"),
    )(q, k, vt, *c_ops)
    return outs[0], outs[1], outs[2:]


def _attention_backward(q, k, kt, v, do, lse, dsum, comm=None):
    n_heads, s_len, _ = q.shape
    tk = kt.shape[3]
    n_q, _, cq = lse.shape[1:]
    c_ops, c_in_specs, c_shapes, c_out_specs, c_sems, c_begin, c_end = _riding_exchange(comm, 2)
    n_c = len(c_ops)

    def body(q_ref, k_ref, kt_ref, v_ref, do_ref, lse_ref, dsum_ref, *rest):
        c_ins, (dqt_ref, dk_ref, dv_ref), c_outs, sems = rest[:n_c], rest[n_c:n_c + 3], rest[n_c + 3:2 * n_c + 3], rest[2 * n_c + 3:]
        c_begin(c_ins, c_outs, sems)
        first = pl.program_id(1) == 0
        k_j, kt_j, v_j = k_ref[0], kt_ref[0, 0], v_ref[0]

        def step(i, carry):
            dk, dv = carry
            rows = pl.ds(pl.multiple_of(i * cq, cq), cq)
            q_i, do_i = q_ref[0, rows, :], do_ref[rows, :]
            s = lax.dot_general(k_j, q_i, _NT, preferred_element_type=F32)
            p = jnp.exp2(s - lse_ref[0, i])
            dp = lax.dot_general(v_j, do_i, _NT, preferred_element_type=F32)
            ds = (p * (dp - dsum_ref[0, i]) * LN_2).astype(BF16)
            dv = dv + jnp.dot(p.astype(BF16), do_i, preferred_element_type=F32)
            dk = dk + jnp.dot(ds, q_i, preferred_element_type=F32)
            _acc(dqt_ref.at[0, i], jnp.dot(kt_j, ds, preferred_element_type=F32), first)
            return dk, dv

        dk, dv = lax.fori_loop(0, n_q, step, (jnp.zeros((tk, QK), F32), jnp.zeros((tk, HEAD), F32)))
        dk_ref[0] = dk
        dv_ref[0] = dv
        c_end(c_ins, c_outs, sems)

    stat = pl.BlockSpec((1, n_q, 1, cq), lambda h, j: (h, 0, 0, 0))
    outs = pl.pallas_call(
        body, name="attention_backward",
        out_shape=(jax.ShapeDtypeStruct((n_heads, n_q, QK, cq), F32), jax.ShapeDtypeStruct((n_heads, s_len, QK), F32),
                   jax.ShapeDtypeStruct((n_heads, s_len, HEAD), F32), *c_shapes),
        grid=(n_heads, s_len // tk),
        in_specs=[pl.BlockSpec((1, s_len, QK), lambda h, j: (h, 0, 0)), pl.BlockSpec((1, tk, QK), lambda h, j: (h, j, 0)),
                  pl.BlockSpec((1, 1, QK, tk), lambda h, j: (h, j, 0, 0)),
                  pl.BlockSpec((1, tk, HEAD), lambda h, j: (h, j, 0)), pl.BlockSpec((s_len, HEAD), lambda h, j: (0, h)),
                  stat, stat] + c_in_specs,
        out_specs=(pl.BlockSpec((1, n_q, QK, cq), lambda h, j: (h, 0, 0, 0)), pl.BlockSpec((1, tk, QK), lambda h, j: (h, j, 0)),
                   pl.BlockSpec((1, tk, HEAD), lambda h, j: (h, j, 0)), *c_out_specs),
        scratch_shapes=c_sems,
        compiler_params=_params("arbitrary", "arbitrary") if comm else _params("parallel", "arbitrary"),
    )(q, k, kt, v, do, lse, dsum, *c_ops)
    return outs[0], outs[1], outs[2], outs[3:]


def _exchange(arrs, gather, name):
    n = len(arrs)

    def body(*refs):
        plan = _exchange_plan(refs[:n], refs[n:2 * n], gather, *refs[2 * n:])
        _exchange_start(plan)
        _exchange_wait(plan)

    any_spec = pl.BlockSpec(memory_space=pl.ANY)
    return pl.pallas_call(
        body, name=name,
        out_shape=_exchange_out_shapes(arrs, gather),
        in_specs=[any_spec] * n,
        out_specs=tuple([any_spec] * n),
        scratch_shapes=_exchange_semaphores(n),
        compiler_params=pltpu.CompilerParams(has_side_effects=True),
    )(*arrs)


def _exchange_out_shapes(arrs, gather):
    return tuple(jax.ShapeDtypeStruct((N_DEV, *(a.shape if g else a.shape[1:])), a.dtype) for a, g in zip(arrs, gather))


def _exchange_semaphores(n):
    n_remote = n * (N_DEV - 1)
    return [pltpu.SemaphoreType.DMA((n_remote,)), pltpu.SemaphoreType.DMA((n_remote,)), pltpu.SemaphoreType.DMA((n,))]


def _exchange_plan(ins, outs, gather, send_sems, recv_sems, local_sems):
    n = len(ins)
    x, y, c = lax.axis_index("x"), lax.axis_index("y"), lax.axis_index("c")
    me = 4 * x + 2 * y + c

    def block_for(a, dev):
        return ins[a] if gather[a] else ins[a].at[dev]

    local = [pltpu.make_async_copy(block_for(a, me), outs[a].at[me], local_sems.at[a]) for a in range(n)]
    remote = []
    for k in range(1, N_DEV):
        px = 1 - x if k & 4 else x
        py = 1 - y if k & 2 else y
        pc = 1 - c if k & 1 else c
        peer = 4 * px + 2 * py + pc
        for a in range(n):
            idx = a * (N_DEV - 1) + k - 1
            send = pltpu.make_async_remote_copy(
                src_ref=block_for(a, peer), dst_ref=outs[a].at[me], send_sem=send_sems.at[idx], recv_sem=recv_sems.at[idx],
                device_id=(px, py, pc), device_id_type=pl.DeviceIdType.MESH)
            arrive = pltpu.make_async_remote_copy(
                src_ref=block_for(a, peer), dst_ref=outs[a].at[peer], send_sem=send_sems.at[idx], recv_sem=recv_sems.at[idx],
                device_id=(px, py, pc), device_id_type=pl.DeviceIdType.MESH)
            remote.append((send, arrive))
    return local, remote


def _exchange_start(plan):
    local, remote = plan
    for cp in local:
        cp.start()
    for send, _ in remote:
        send.start()


def _exchange_wait(plan):
    local, remote = plan
    for send, arrive in remote:
        send.wait_send()
        arrive.wait_recv()
    for cp in local:
        cp.wait()


def _riding_exchange(comm, n_grid):
    if comm is None:
        return [], [], (), (), [], lambda *_: None, lambda *_: None
    arrs, gather = comm
    n = len(arrs)
    any_spec = pl.BlockSpec(memory_space=pl.ANY)

    def begin(ins, outs, sems):
        @pl.when(functools.reduce(jnp.logical_and, [pl.program_id(d) == 0 for d in range(n_grid)]))
        def _():
            _exchange_start(_exchange_plan(ins, outs, gather, *sems))

    def end(ins, outs, sems):
        @pl.when(functools.reduce(jnp.logical_and, [pl.program_id(d) == pl.num_programs(d) - 1 for d in range(n_grid)]))
        def _():
            _exchange_wait(_exchange_plan(ins, outs, gather, *sems))

    return (list(arrs), [any_spec] * n, _exchange_out_shapes(arrs, gather), tuple([any_spec] * n), _exchange_semaphores(n),
            begin, end)


ADAM_TILE_ELEMS = 256 * 1024


def _sum_adam(parts, w, m, v, layer, prev, name):
    n_parts, r, c = parts.shape
    tm, tc = r, c
    if r % 16 == 0:
        while tm * c > ADAM_TILE_ELEMS and tm % 16 == 0:
            tm //= 2
    else:
        while r * tc > ADAM_TILE_ELEMS and tc % 256 == 0:
            tc //= 2

    def body(p_ref, w_ref, m_ref, v_ref, *rest):
        g_ref, d_ref, nm_ref, nv_ref = rest[-4:]
        g = p_ref[0].astype(F32)
        for s in range(1, n_parts):
            g = g + p_ref[s].astype(F32)
        m_new = ADAM_B1 * m_ref[...] + (1.0 - ADAM_B1) * g
        v_new = ADAM_B2 * v_ref[...] + (1.0 - ADAM_B2) * (g * g)
        m_hat = m_new / (1.0 - ADAM_B1 ** ADAM_STEP)
        v_hat = v_new / (1.0 - ADAM_B2 ** ADAM_STEP)
        g_ref[...] = g
        d_ref[...] = -ADAM_LR * (m_hat / (jnp.sqrt(v_hat) + ADAM_EPS) + ADAM_WD * w_ref[...])
        nm_ref[...] = m_new
        nv_ref[...] = v_new

    slab = pl.BlockSpec((None, tm, tc), lambda i, j: (layer, i, j))
    n_prev = 0 if prev is None else 4
    return pl.pallas_call(
        body, name=name,
        out_shape=(jax.ShapeDtypeStruct(w.shape, F32),) * 4,
        grid=(r // tm, c // tc),
        in_specs=[pl.BlockSpec((n_parts, tm, tc), lambda i, j: (0, i, j)), slab, slab, slab]
                 + [pl.BlockSpec(memory_space=pl.ANY)] * n_prev,
        out_specs=(slab, slab, slab, slab),
        input_output_aliases={4 + j: j for j in range(n_prev)},
        compiler_params=_params("parallel", "parallel"),
    )(parts, w, m, v, *(prev or ()))


def _permute_in(w):
    k = w.shape[0]
    q = w[:, 3584:5120].reshape(k, C_HEADS, QK)
    return jnp.concatenate(
        [w[:, 1536:3584], w[:, 5696:6720], w[:, 0:1536], q[:, :, :HEAD].reshape(k, C_HEADS * HEAD),
         q[:, :, HEAD:].reshape(k, C_HEADS * ROPE), w[:, 5120:5632], w[:, 5632:5696],
         jnp.zeros((k, PROJ_W - IN_WIDTH), w.dtype)], axis=1)


def _unpermute_in(g):
    k = g.shape[0]
    qn = g[:, O_M + M_QN:O_M + M_QR].reshape(k, C_HEADS, HEAD)
    qr = g[:, O_M + M_QR:O_M + M_CKV].reshape(k, C_HEADS, ROPE)
    q = jnp.concatenate([qn, qr], axis=-1).reshape(k, C_HEADS * QK)
    return jnp.concatenate(
        [g[:, O_A:O_A + W_A], g[:, O_B:O_B + W_B], q, g[:, O_M + M_CKV:O_M + M_KR],
         g[:, O_M + M_KR:O_M + M_KR + ROPE], g[:, O_CZ:O_CZ + W_CZ]], axis=1)


SMALL = ("attn_norm", "sgu_norm", "w_spatial", "b_spatial", "conv_b", "kv_norm", "q_nope_norm", "q_rope_norm",
         "k_nope_norm", "k_rope_norm", "out_norm", "ple_norm")
PACK_ROWS = 256


def _pack(tensors):
    flat = jnp.concatenate([t.reshape(-1) for t in tensors])
    rows = -(-flat.shape[0] // (128 * PACK_ROWS)) * PACK_ROWS
    return jnp.pad(flat, (0, rows * 128 - flat.shape[0])).reshape(rows, 128)


def _unpack(packed, like):
    flat = packed.reshape(-1)
    out, pos = [], 0
    for t in like:
        out.append(flat[pos:pos + t.size].reshape(t.shape))
        pos += t.size
    return out


def _tile(s_len, want):
    return min(want, s_len)


ATT_FWD_QUERIES = 512
ATT_FWD_KEYS = 8192
ATT_BWD_KEYS = 512
ATT_BWD_QUERIES = 2048


def _layer_forward(h, p_l, cosf, sins, w, sm, comm, comm_rest):
    s_len = h.shape[0]
    tm = _tile(s_len, 512)
    proj, hn, rest = _norm_matmul(h, sm["attn_norm"], w["w_in"], _tile(s_len, 1024), 768, comm_rest)
    if comm_rest is not None:
        w = {**w, **_assemble_rest(rest)}
    ga, gb, gc = sm["out_norm"][:, 0:512], sm["out_norm"][:, 512:1024], sm["out_norm"][:, 1024:2048]
    y = _sgu_forward(proj, sm["sgu_norm"], sm["w_spatial"], sm["b_spatial"], ga, _tile(s_len, 256))
    y = _conv_forward(proj, w["conv_w"], sm["conv_b"], gb, y, _tile(s_len, 256))
    q, k, v, kt, vt = _mla_forward(proj, cosf, sins, w["w_ukv"], sm["kv_norm"], sm["q_nope_norm"], sm["q_rope_norm"],
                                   sm["k_nope_norm"], sm["k_rope_norm"], _tile(s_len, 256), _tile(s_len, ATT_BWD_KEYS),
                                   _tile(s_len, ATT_FWD_KEYS))
    o, lse, arrived = _attention_forward(q, k, vt, _tile(s_len, ATT_FWD_QUERIES), _tile(s_len, ATT_BWD_QUERIES), comm)
    y = _cgate_forward(o, proj, gc, y, _tile(s_len, 256))
    h1 = _out_matmul(h, y, w["w_out"], _tile(s_len, 1024), 1024)
    h2, n1, gate, pp = _ple_forward(h1, sm["ple_norm"], p_l, w["w_ple_gate"], w["w_ple_proj"], tm, 1024)
    saved = dict(h=h, hn=hn, proj=proj, y=y, q=q, k=k, v=v, kt=kt, o=o, lse=lse, h1=h1, n1=n1, gate=gate, pp=pp)
    return h2, saved, w, arrived


def _layer_backward(dh2, p_l, cosf, sins, w, sm, sv, comm, scatter_own):
    s_len = dh2.shape[0]
    tm = _tile(s_len, 512)
    tr = _tile(s_len, 256)
    big, small = {}, {}
    dh1, small["ple_norm"], dgp, dpp = _ple_backward(dh2, sv["gate"], sv["pp"], w["w_ple_gate"], sv["h1"], sm["ple_norm"], tm)
    big["w_ple_proj"] = _matmul_tn(p_l, dpp, _tile(s_len, 1024), PLE_DIM, 1024, "grad_w_ple_proj")
    big["w_ple_gate"] = _matmul_tn(sv["n1"], dgp, _tile(s_len, 1024), 512, 1024, "grad_w_ple_gate")
    dy, _ = _matmul_nt(dh1, w["w_out"], _tile(s_len, 1024), 1024, "grad_branches")
    big["w_out"] = _matmul_tn(sv["y"], dh1, _tile(s_len, 1024), 512, 1024, "grad_w_out")
    ga, gb, gc = sm["out_norm"][:, 0:512], sm["out_norm"][:, 512:1024], sm["out_norm"][:, 1024:2048]
    dproj, dcw, small["conv_b"], dgb = _conv_backward(sv["proj"], w["conv_w"], sm["conv_b"], gb, dy, tr)
    big["conv_w"] = dcw
    dproj, do, dsum, dgc = _cgate_backward(sv["o"], sv["proj"], gc, dy, dproj, tr, _tile(s_len, ATT_BWD_QUERIES))
    dqt, dk, dv, arrived = _attention_backward(sv["q"], sv["k"], sv["kt"], sv["v"], do, sv["lse"], dsum, comm)
    (dproj, big["w_ukv"], small["kv_norm"], small["q_nope_norm"], small["q_rope_norm"], small["k_nope_norm"],
     small["k_rope_norm"]) = _mla_backward(sv["proj"], cosf, sins, w["w_ukv"], sm["kv_norm"], sm["q_nope_norm"],
                                            sm["q_rope_norm"], sm["k_nope_norm"], sm["k_rope_norm"], dqt, dk, dv, dproj, tr)
    dproj, small["sgu_norm"], small["w_spatial"], small["b_spatial"], dga = _sgu_backward(
        sv["proj"], sm["sgu_norm"], sm["w_spatial"], sm["b_spatial"], ga, dy, dproj, tr)
    small["out_norm"] = jnp.concatenate([dga, dgb, dgc], axis=1)
    big["w_in"] = _matmul_tn(sv["hn"], dproj, _tile(s_len, 2048), 512, 2304, "grad_w_in")
    parts = _layer_parts(big)
    dhn, arrived_own = _matmul_nt(dproj, w["w_in"], tm, 512, "grad_attn_norm_in",
                                  (parts, [False] * len(parts)) if scatter_own else None)
    dh, small["attn_norm"] = _rms_backward(sv["h"], sm["attn_norm"], dhn, dh1, tr, "attn_norm_backward")
    return dh, parts, big["conv_w"], small, arrived, arrived_own if scatter_own else None


def _layer_small(params, layer):
    return dict(
        attn_norm=params["attn_norm"][layer][None, :], sgu_norm=params["sgu_norm"][layer],
        w_spatial=params["w_spatial"][layer], b_spatial=params["b_spatial"][layer][:, :, None],
        conv_b=params["conv_b"][layer][None, :], kv_norm=params["kv_norm"][layer][None, :],
        q_nope_norm=params["q_nope_norm"][layer][None, :], q_rope_norm=params["q_rope_norm"][layer][None, :],
        k_nope_norm=params["k_nope_norm"][layer][None, :], k_rope_norm=params["k_rope_norm"][layer][None, :],
        out_norm=params["out_norm"][layer][None, :], ple_norm=params["ple_norm"][layer][None, :])


BIG = ("w_in", "w_ukv", "w_out", "w_ple_gate", "w_ple_proj")


def _assemble_w_in(g_in):
    return _permute_in(g_in.transpose(1, 0, 2).reshape(g_in.shape[1], IN_WIDTH))


def _assemble_rest(gathered):
    g_ukv, g_out, g_gate, g_proj = gathered
    w_ukv = g_ukv.reshape(N_DEV, KV_RANK, 2, HEAD).transpose(1, 2, 0, 3).reshape(KV_RANK, 2 * C_HEADS * HEAD)
    return dict(w_ukv=w_ukv, w_out=g_out.reshape(D_MODEL, D_MODEL), w_ple_gate=g_gate.reshape(D_MODEL, D_MODEL),
                w_ple_proj=g_proj.transpose(1, 0, 2).reshape(PLE_DIM, D_MODEL))


def _layer_parts(big):
    g_in = big["w_in"]
    return [
        _unpermute_in(g_in).reshape(g_in.shape[0], N_DEV, -1).transpose(1, 2, 0).astype(BF16),
        big["w_ukv"].reshape(KV_RANK, 2, N_DEV, HEAD).transpose(2, 0, 1, 3).reshape(N_DEV, KV_RANK, 2 * HEAD).astype(BF16),
        big["w_out"].reshape(N_DEV, -1, D_MODEL).astype(BF16),
        big["w_ple_gate"].reshape(N_DEV, -1, D_MODEL).astype(BF16),
        big["w_ple_proj"].reshape(PLE_DIM, N_DEV, -1).transpose(1, 0, 2).astype(BF16)]


def _step_local(xs, ps, pos, target, shards, conv_w, params):
    inv = 1.0 / (ROPE_BASE ** (jnp.arange(0, ROPE, 2, dtype=F32) / ROPE))
    ang = pos.astype(F32)[:, None] * inv
    cos, sin = jnp.cos(ang), jnp.sin(ang)
    cosf = jnp.concatenate([cos, cos], axis=-1)
    sins = jnp.concatenate([-sin, sin], axis=-1)

    def gather_of(names, layer):
        return [shards[n][layer] for n in names], [True] * len(names)

    h = xs
    saved, weights = [], []
    smalls = [_layer_small(params, layer) for layer in range(DEPTH)]
    (first_w_in,) = _exchange(*gather_of(BIG[:1], 0), "gather_first_w_in")
    w = dict(w_in=_assemble_w_in(first_w_in), conv_w=conv_w[0])
    for layer in range(DEPTH):
        comm = gather_of(BIG, layer + 1) if layer + 1 < DEPTH else None
        comm_rest = gather_of(BIG[1:], 0) if layer == 0 else None
        h, sv, w, arrived = _layer_forward(h, ps[layer], cosf, sins, w, smalls[layer], comm, comm_rest)
        saved.append(sv)
        weights.append(w)
        if comm is not None:
            w = dict(w_in=_assemble_w_in(arrived[0]), conv_w=conv_w[layer + 1], **_assemble_rest(arrived[1:]))
    dh, loss = _loss_grad(h, target, _tile(h.shape[0], 512))
    received, conv_grads, small_grads = [None] * DEPTH, [None] * DEPTH, [None] * DEPTH
    comm = None
    for layer in reversed(range(DEPTH)):
        dh, parts, conv_grads[layer], small_grads[layer], arrived, arrived_own = _layer_backward(
            dh, ps[layer], cosf, sins, weights[layer], smalls[layer], saved[layer], comm, layer == 0)
        if comm is not None:
            received[layer + 1] = arrived
        comm = (parts, [False] * len(parts))
    received[0] = arrived_own
    return loss, dh, received, conv_grads, small_grads


def kernel(x, p, positions, attn_norm, w_in, sgu_norm, w_spatial, b_spatial, conv_w, conv_b, kv_norm, w_ukv, q_nope_norm, q_rope_norm, k_nope_norm, k_rope_norm, out_norm, w_out, ple_norm, w_ple_gate, w_ple_proj, loss_target, m_attn_norm, m_w_in, m_sgu_norm, m_w_spatial, m_b_spatial, m_conv_w, m_conv_b, m_kv_norm, m_w_ukv, m_q_nope_norm, m_q_rope_norm, m_k_nope_norm, m_k_rope_norm, m_out_norm, m_w_out, m_ple_norm, m_w_ple_gate, m_w_ple_proj, v_attn_norm, v_w_in, v_sgu_norm, v_w_spatial, v_b_spatial, v_conv_w, v_conv_b, v_kv_norm, v_w_ukv, v_q_nope_norm, v_q_rope_norm, v_k_nope_norm, v_k_rope_norm, v_out_norm, v_w_out, v_ple_norm, v_w_ple_gate, v_w_ple_proj):
    order = ("attn_norm", "w_in", "sgu_norm", "w_spatial", "b_spatial", "conv_w", "conv_b", "kv_norm", "w_ukv",
             "q_nope_norm", "q_rope_norm", "k_nope_norm", "k_rope_norm", "out_norm", "w_out", "ple_norm", "w_ple_gate",
             "w_ple_proj")
    wts = dict(zip(order, (attn_norm, w_in, sgu_norm, w_spatial, b_spatial, conv_w, conv_b, kv_norm, w_ukv, q_nope_norm,
                           q_rope_norm, k_nope_norm, k_rope_norm, out_norm, w_out, ple_norm, w_ple_gate, w_ple_proj)))
    mom = dict(zip(order, (m_attn_norm, m_w_in, m_sgu_norm, m_w_spatial, m_b_spatial, m_conv_w, m_conv_b, m_kv_norm, m_w_ukv,
                           m_q_nope_norm, m_q_rope_norm, m_k_nope_norm, m_k_rope_norm, m_out_norm, m_w_out, m_ple_norm,
                           m_w_ple_gate, m_w_ple_proj)))
    var = dict(zip(order, (v_attn_norm, v_w_in, v_sgu_norm, v_w_spatial, v_b_spatial, v_conv_w, v_conv_b, v_kv_norm, v_w_ukv,
                           v_q_nope_norm, v_q_rope_norm, v_k_nope_norm, v_k_rope_norm, v_out_norm, v_w_out, v_ple_norm,
                           v_w_ple_gate, v_w_ple_proj)))

    conv_shard = wts["conv_w"]
    (conv_all,) = _exchange([conv_shard.reshape(-1, 128)], [True], "gather_conv_w")
    conv_full = conv_all.reshape(N_DEV, DEPTH, 3, -1).transpose(1, 2, 0, 3).reshape(DEPTH, 3, -1)
    shards = {n: wts[n].astype(BF16) for n in BIG}
    loss_part, grad_x, received, conv_grads, small_grads = _step_local(
        x[0], p[:, 0], positions[0], loss_target[0], shards, conv_full, wts)
    loss = lax.psum(loss_part[0, 0], ("x", "y", "c"))

    def small_grad(name):
        g = jnp.stack([sg[name] for sg in small_grads])
        return g.reshape(wts[name].shape)

    conv_grad = jnp.stack(conv_grads)
    like = [wts[n] for n in SMALL] + [conv_grad]
    packed = _pack([small_grad(n) for n in SMALL] + [conv_grad])
    (small_parts,) = _exchange([packed], [True], "gather_small_grads")
    filler = [jnp.zeros_like(conv_grad), jnp.zeros_like(conv_grad), jnp.ones_like(conv_grad)]
    small_out = _sum_adam(small_parts, *(_pack([src[n] for n in SMALL] + [fill])[None] for src, fill in zip((wts, mom, var), filler)),
                          0, None, "adam_small")
    unpacked = [_unpack(o[0], like) for o in small_out]
    results = {n: vals for n, vals in zip(SMALL, zip(*[u[:-1] for u in unpacked]))}
    me = 4 * lax.axis_index("x") + 2 * lax.axis_index("y") + lax.axis_index("c")
    width = conv_shard.shape[2]
    conv_local = lax.dynamic_slice_in_dim(unpacked[0][-1], me * width, width, axis=2)
    as_slab = (lambda t: t.reshape(1, -1, width))
    conv_out = _sum_adam(as_slab(conv_local), as_slab(conv_shard), as_slab(mom["conv_w"]), as_slab(var["conv_w"]), 0, None,
                         "adam_conv_w")
    results["conv_w"] = tuple(o.reshape(conv_shard.shape) for o in conv_out)

    for j, name in enumerate(BIG):
        view = (lambda t: jnp.swapaxes(t, 1, 2)) if name == "w_in" else (lambda t: t)
        outs = None
        for layer in range(DEPTH):
            outs = _sum_adam(received[layer][j], view(wts[name]), view(mom[name]), view(var[name]), layer, outs, "adam_" + name)
        results[name] = tuple(view(o) for o in outs)

    grads, deltas, new_m, new_v = ([results[n][j] for n in order] for j in range(4))
    return (loss, grad_x[None], *grads, *deltas, *new_m, *new_v)
```

```python
import functools

import jax
import jax.numpy as jnp
from jax import lax
from jax.experimental import pallas as pl
from jax.experimental.pallas import tpu as pltpu

F32 = jnp.float32
BF16 = jnp.bfloat16

N_DEV = 8
DEPTH = 4
D_MODEL = 2048
EPS = 1e-6
CHUNK = 128
A_HEADS = 4
HEAD = 128
ROPE = 64
HALF = ROPE // 2
C_HEADS = 8
KV_RANK = 512
PLE_DIM = 256
ROPE_BASE = 10000.0
IN_WIDTH = 6720
QK = HEAD + ROPE
SCALE = QK ** -0.5
LOG2_E = 1.4426950408889634
LN_2 = 0.6931471805599453
Q_SCALE = SCALE * LOG2_E
HALO = 8

O_B = 0
W_B = 2048
O_CZ = 2048
W_CZ = 1024
O_A = 3072
W_A = 1536
O_M = 4608
W_M = 2304
M_QN, M_QR, M_CKV, M_KR = 0, 1024, 1536, 2048
PROJ_W = 6912

ADAM_LR = 0.001
ADAM_B1 = 0.9
ADAM_B2 = 0.999
ADAM_EPS = 1e-08
ADAM_WD = 0.01
ADAM_STEP = 10

VMEM_LIMIT = 56 * 1024 * 1024

_NT = (((1,), (1,)), ((), ()))
_TN = (((0,), (0,)), ((), ()))


def _params(*sem):
    return pltpu.CompilerParams(dimension_semantics=sem, vmem_limit_bytes=VMEM_LIMIT)


@jax.custom_vjp
def _bdot(a, b):
    return jnp.dot(a.astype(BF16), b.astype(BF16), preferred_element_type=F32)


def _bdot_fwd(a, b):
    return _bdot(a, b), (a, b)


def _bdot_bwd(res, g):
    a, b = res
    gb = g.astype(BF16)
    da = lax.dot_general(gb, b.astype(BF16), _NT, preferred_element_type=F32)
    db = lax.dot_general(a.astype(BF16), gb, _TN, preferred_element_type=F32)
    return da.astype(a.dtype), db.astype(b.dtype)


_bdot.defvjp(_bdot_fwd, _bdot_bwd)


@functools.partial(jax.custom_vjp, nondiff_argnums=(1,))
def _split(x, n):
    w = x.shape[-1] // n
    return tuple(x[:, i * w:(i + 1) * w] for i in range(n))


def _split_fwd(x, n):
    return _split(x, n), None


def _split_bwd(n, _, gs):
    return (jnp.concatenate(gs, axis=-1),)


_split.defvjp(_split_fwd, _split_bwd)


@functools.partial(jax.custom_vjp, nondiff_argnums=(1,))
def _shift_rows(x, k):
    return pltpu.roll(x, k % x.shape[0], 0)


def _shift_rows_fwd(x, k):
    return _shift_rows(x, k), None


def _shift_rows_bwd(k, _, g):
    return (_shift_rows(g, -k),)


_shift_rows.defvjp(_shift_rows_fwd, _shift_rows_bwd)


@jax.custom_vjp
def _swap_halves(x):
    h = x.shape[-1] // 2
    return jnp.concatenate([x[:, h:], x[:, :h]], axis=-1)


def _swap_halves_fwd(x):
    return _swap_halves(x), None


def _swap_halves_bwd(_, g):
    return (_swap_halves(g),)


_swap_halves.defvjp(_swap_halves_fwd, _swap_halves_bwd)


def _rms(x, g):
    return x * lax.rsqrt(jnp.mean(x * x, axis=-1, keepdims=True) + EPS) * g


def _rope(x, cosf, sins):
    return x * cosf + _swap_halves(x) * sins


def _sgu_chunk(u, v, z, gain, ws, bs, ga):
    ys = []
    for h in range(A_HEADS):
        vn = _rms(v[h], gain[h])
        s = _bdot(ws[h], vn) + bs[h]
        ys.append(u[h] * s * jax.nn.silu(z[h]))
    ss = sum(jnp.sum(y * y, axis=-1, keepdims=True) for y in ys) * (1.0 / (A_HEADS * HEAD))
    r = lax.rsqrt(ss + EPS)
    return tuple(ys[h] * r * ga[h] for h in range(A_HEADS))


def _conv_tile(bb, bc, bh, bz, w0, w1, w2, cb, gb, w0h, w1h, w2h, cbh, gbh, valid, core):
    t = jnp.where(valid, bc * bh, 0.0)
    y = (jnp.where(core, cb, cbh)
         + _shift_rows(t, 1) * jnp.where(core, w0, w0h)
         + t * jnp.where(core, w1, w1h)
         + _shift_rows(t, -1) * jnp.where(core, w2, w2h))
    return _rms(bb * y * jax.nn.silu(bz), jnp.where(core, gb, gbh))


def _cgate_tile(o, cz, gc):
    return _rms(o * jax.nn.silu(cz), gc)


def _mla_tile(qn, qr, ckv, kr, cosf, sins, wukv, kvg, qng, qrg, kng, krg):
    kv = _split(_bdot(_rms(ckv, kvg), wukv), 2 * C_HEADS)
    k_r = _rope(_rms(kr, krg), cosf, sins)
    qn_h = _split(qn, C_HEADS)
    qr_h = _split(qr, C_HEADS)
    q, k, v = [], [], []
    for h in range(C_HEADS):
        q.append(jnp.concatenate([_rms(qn_h[h], qng), _rope(_rms(qr_h[h], qrg), cosf, sins)], axis=-1) * Q_SCALE)
        k.append(jnp.concatenate([_rms(kv[h], kng), k_r], axis=-1))
        v.append(kv[C_HEADS + h])
    return tuple(q), tuple(k), tuple(v)


def _norm_matmul(h, gain, w, tm, tn, comm=None):
    s_len, k = h.shape
    n = w.shape[1]
    c_ops, c_in_specs, c_shapes, c_out_specs, c_sems, c_begin, c_end = _riding_exchange(comm, 2)
    n_c = len(c_ops)

    def body(h_ref, g_ref, w_ref, *rest):
        c_ins, (o_ref, hn_ref), c_outs, sems = rest[:n_c], rest[n_c:n_c + 2], rest[n_c + 2:2 * n_c + 2], rest[2 * n_c + 2:]
        c_begin(c_ins, c_outs, sems)

        @pl.when(pl.program_id(1) == 0)
        def _():
            hn_ref[...] = _rms(h_ref[...], g_ref[...]).astype(BF16)

        o_ref[...] = jnp.dot(hn_ref[...], w_ref[...], preferred_element_type=F32).astype(BF16)
        c_end(c_ins, c_outs, sems)

    outs = pl.pallas_call(
        body, name="norm_matmul",
        out_shape=(jax.ShapeDtypeStruct((s_len, n), BF16), jax.ShapeDtypeStruct((s_len, k), BF16), *c_shapes),
        grid=(s_len // tm, n // tn),
        in_specs=[pl.BlockSpec((tm, k), lambda i, j: (i, 0)), pl.BlockSpec((1, k), lambda i, j: (0, 0)),
                  pl.BlockSpec((k, tn), lambda i, j: (0, j))] + c_in_specs,
        out_specs=(pl.BlockSpec((tm, tn), lambda i, j: (i, j)), pl.BlockSpec((tm, k), lambda i, j: (i, 0)), *c_out_specs),
        scratch_shapes=c_sems,
        compiler_params=_params("arbitrary", "arbitrary") if comm else _params("parallel", "arbitrary"),
    )(h, gain, w, *c_ops)
    return outs[0], outs[1], outs[2:]


def _out_matmul(h, y, w, tm, tn):
    s_len, n = h.shape
    k = y.shape[1]

    def body(h_ref, y_ref, w_ref, o_ref):
        o_ref[...] = h_ref[...] + jnp.dot(y_ref[...], w_ref[...], preferred_element_type=F32)

    return pl.pallas_call(
        body, name="out_matmul",
        out_shape=jax.ShapeDtypeStruct((s_len, n), F32),
        grid=(s_len // tm, n // tn),
        in_specs=[pl.BlockSpec((tm, tn), lambda i, j: (i, j)), pl.BlockSpec((tm, k), lambda i, j: (i, 0)),
                  pl.BlockSpec((k, tn), lambda i, j: (0, j))],
        out_specs=pl.BlockSpec((tm, tn), lambda i, j: (i, j)),
        compiler_params=_params("parallel", "parallel"),
    )(h, y, w)


def _ple_forward(h1, gain, p, wg, wp, tm, tn):
    s_len, d = h1.shape
    kp = p.shape[1]

    def body(hrow_ref, hcol_ref, g_ref, p_ref, wg_ref, wp_ref, o_ref, n1_ref, gate_ref, pp_ref):
        @pl.when(pl.program_id(1) == 0)
        def _():
            n1_ref[...] = _rms(hrow_ref[...], g_ref[...]).astype(BF16)

        gate = jax.nn.sigmoid(jnp.dot(n1_ref[...], wg_ref[...], preferred_element_type=F32))
        pp = jnp.dot(p_ref[...].astype(BF16), wp_ref[...], preferred_element_type=F32)
        o_ref[...] = hcol_ref[...] + gate * pp
        gate_ref[...] = gate.astype(BF16)
        pp_ref[...] = pp.astype(BF16)

    col = pl.BlockSpec((tm, tn), lambda i, j: (i, j))
    return pl.pallas_call(
        body, name="ple_forward",
        out_shape=(jax.ShapeDtypeStruct((s_len, d), F32), jax.ShapeDtypeStruct((s_len, d), BF16),
                   jax.ShapeDtypeStruct((s_len, d), BF16), jax.ShapeDtypeStruct((s_len, d), BF16)),
        grid=(s_len // tm, d // tn),
        in_specs=[pl.BlockSpec((tm, d), lambda i, j: (i, 0)), col, pl.BlockSpec((1, d), lambda i, j: (0, 0)),
                  pl.BlockSpec((tm, kp), lambda i, j: (i, 0)), pl.BlockSpec((d, tn), lambda i, j: (0, j)),
                  pl.BlockSpec((kp, tn), lambda i, j: (0, j))],
        out_specs=(col, pl.BlockSpec((tm, d), lambda i, j: (i, 0)), col, col),
        compiler_params=_params("parallel", "arbitrary"),
    )(h1, h1, gain, p, wg, wp)


def _matmul_nt(a, b, tm, tk, name, comm=None):
    m, n = a.shape
    k = b.shape[0]
    c_ops, c_in_specs, c_shapes, c_out_specs, c_sems, c_begin, c_end = _riding_exchange(comm, 2)
    n_c = len(c_ops)

    def body(a_ref, b_ref, *rest):
        c_ins, o_ref, c_outs, sems = rest[:n_c], rest[n_c], rest[n_c + 1:2 * n_c + 1], rest[2 * n_c + 1:]
        c_begin(c_ins, c_outs, sems)
        o_ref[...] = lax.dot_general(a_ref[...].astype(BF16), b_ref[...].astype(BF16), _NT, preferred_element_type=F32)
        c_end(c_ins, c_outs, sems)

    outs = pl.pallas_call(
        body, name=name,
        out_shape=(jax.ShapeDtypeStruct((m, k), F32), *c_shapes),
        grid=(m // tm, k // tk),
        in_specs=[pl.BlockSpec((tm, n), lambda i, j: (i, 0)), pl.BlockSpec((tk, n), lambda i, j: (j, 0))] + c_in_specs,
        out_specs=(pl.BlockSpec((tm, tk), lambda i, j: (i, j)), *c_out_specs),
        scratch_shapes=c_sems,
        compiler_params=_params("arbitrary", "arbitrary") if comm else _params("parallel", "parallel"),
    )(a, b, *c_ops)
    return outs[0], outs[1:]


def _matmul_tn(a, b, tm, tk, tn, name):
    m, k = a.shape
    n = b.shape[1]
    n_m = m // tm

    def body(a_ref, b_ref, o_ref, acc_ref):
        part = lax.dot_general(a_ref[...].astype(BF16), b_ref[...].astype(BF16), _TN, preferred_element_type=F32)
        _acc(acc_ref, part, pl.program_id(2) == 0)

        @pl.when(pl.program_id(2) == n_m - 1)
        def _():
            o_ref[...] = acc_ref[...].astype(BF16)

    return pl.pallas_call(
        body, name=name,
        out_shape=jax.ShapeDtypeStruct((k, n), BF16),
        grid=(k // tk, n // tn, n_m),
        in_specs=[pl.BlockSpec((tm, tk), lambda kk, nn, mm: (mm, kk)), pl.BlockSpec((tm, tn), lambda kk, nn, mm: (mm, nn))],
        out_specs=pl.BlockSpec((tk, tn), lambda kk, nn, mm: (kk, nn)),
        scratch_shapes=[pltpu.VMEM((tk, tn), F32)],
        compiler_params=_params("parallel", "parallel", "arbitrary"),
    )(a, b)


def _acc(ref, val, first):
    @pl.when(first)
    def _():
        ref[...] = val

    @pl.when(jnp.logical_not(first))
    def _():
        ref[...] += val


def _loss_grad(h, target, tm):
    s_len, d = h.shape

    def body(h_ref, t_ref, dh_ref, loss_ref):
        e = h_ref[...] - t_ref[...]
        dh_ref[...] = e * (1.0 / d)
        part = jnp.sum(jnp.sum(e * e, axis=-1, keepdims=True), axis=0, keepdims=True) * (0.5 / d)
        _acc(loss_ref, jnp.broadcast_to(part, loss_ref.shape), pl.program_id(0) == 0)

    row = pl.BlockSpec((tm, d), lambda i: (i, 0))
    return pl.pallas_call(
        body, name="loss_grad",
        out_shape=(jax.ShapeDtypeStruct((s_len, d), F32), jax.ShapeDtypeStruct((1, 128), F32)),
        grid=(s_len // tm,),
        in_specs=[row, row],
        out_specs=(row, pl.BlockSpec((1, 128), lambda i: (0, 0))),
        compiler_params=_params("arbitrary"),
    )(h, target)


def _rms_backward(x, gain, dn, dres, tm, name):
    s_len, d = x.shape

    def body(x_ref, g_ref, dn_ref, dres_ref, dx_ref, dg_ref):
        _, vjp = jax.vjp(_rms, x_ref[...], g_ref[...])
        dx, dg = vjp(dn_ref[...])
        dx_ref[...] = dres_ref[...] + dx
        _acc(dg_ref, dg, pl.program_id(0) == 0)

    row = pl.BlockSpec((tm, d), lambda i: (i, 0))
    vec = pl.BlockSpec((1, d), lambda i: (0, 0))
    return pl.pallas_call(
        body, name=name,
        out_shape=(jax.ShapeDtypeStruct((s_len, d), F32), jax.ShapeDtypeStruct((1, d), F32)),
        grid=(s_len // tm,),
        in_specs=[row, vec, row, row],
        out_specs=(row, vec),
        compiler_params=_params("arbitrary"),
    )(x, gain, dn, dres)


def _ple_backward(dh2, gate, pp, wg, h1, gain, tm):
    s_len, d = dh2.shape

    def body(dh_ref, gate_ref, pp_ref, wg_ref, h1_ref, g_ref, dh1_ref, dgain_ref, dgp_ref, dpp_ref):
        dh = dh_ref[...]
        gate = gate_ref[...].astype(F32)
        dgp = (dh * pp_ref[...].astype(F32) * gate * (1.0 - gate)).astype(BF16)
        dgp_ref[...] = dgp
        dpp_ref[...] = (dh * gate).astype(BF16)
        dn = lax.dot_general(dgp, wg_ref[...], _NT, preferred_element_type=F32)
        _, vjp = jax.vjp(_rms, h1_ref[...], g_ref[...])
        dx, dgain = vjp(dn)
        dh1_ref[...] = dh + dx
        _acc(dgain_ref, dgain, pl.program_id(0) == 0)

    row = pl.BlockSpec((tm, d), lambda i: (i, 0))
    vec = pl.BlockSpec((1, d), lambda i: (0, 0))
    return pl.pallas_call(
        body, name="ple_backward",
        out_shape=(jax.ShapeDtypeStruct((s_len, d), F32), jax.ShapeDtypeStruct((1, d), F32),
                   jax.ShapeDtypeStruct((s_len, d), BF16), jax.ShapeDtypeStruct((s_len, d), BF16)),
        grid=(s_len // tm,),
        in_specs=[row, row, row, pl.BlockSpec((d, d), lambda i: (0, 0)), row, vec],
        out_specs=(row, vec, row, row),
        compiler_params=_params("arbitrary"),
    )(dh2, gate, pp, wg, h1, gain)


def _sgu_in_specs(tm):
    return [pl.BlockSpec((tm, W_A), lambda i: (i, O_A // W_A)),
            pl.BlockSpec((A_HEADS, HEAD), lambda i: (0, 0)), pl.BlockSpec((A_HEADS, CHUNK, CHUNK), lambda i: (0, 0, 0)),
            pl.BlockSpec((A_HEADS, CHUNK, 1), lambda i: (0, 0, 0)), pl.BlockSpec((1, 512), lambda i: (0, 0))]


def _sgu_load(a_ref, gain_ref, ws_ref, bs_ref, ga_ref, c):
    rows = slice(c * CHUNK, (c + 1) * CHUNK)
    heads = range(A_HEADS)
    u = tuple(a_ref[rows, h * HEAD:(h + 1) * HEAD].astype(F32) for h in heads)
    v = tuple(a_ref[rows, 512 + h * HEAD:512 + (h + 1) * HEAD].astype(F32) for h in heads)
    z = tuple(a_ref[rows, 1024 + h * HEAD:1024 + (h + 1) * HEAD].astype(F32) for h in heads)
    gain = tuple(gain_ref[h:h + 1, :] for h in heads)
    ws = tuple(ws_ref[h] for h in heads)
    bs = tuple(bs_ref[h] for h in heads)
    ga = tuple(ga_ref[:, h * HEAD:(h + 1) * HEAD] for h in heads)
    return u, v, z, gain, ws, bs, ga


def _sgu_forward(proj, gain, ws, bs, ga, tm):
    s_len = proj.shape[0]

    def body(a_ref, gain_ref, ws_ref, bs_ref, ga_ref, o_ref):
        for c in range(tm // CHUNK):
            out = _sgu_chunk(*_sgu_load(a_ref, gain_ref, ws_ref, bs_ref, ga_ref, c))
            for h in range(A_HEADS):
                o_ref[c * CHUNK:(c + 1) * CHUNK, h * HEAD:(h + 1) * HEAD] = out[h].astype(BF16)

    return pl.pallas_call(
        body, name="sgu_forward",
        out_shape=jax.ShapeDtypeStruct((s_len, D_MODEL), BF16),
        grid=(s_len // tm,),
        in_specs=_sgu_in_specs(tm),
        out_specs=pl.BlockSpec((tm, 512), lambda i: (i, 0)),
        compiler_params=_params("parallel"),
    )(proj, gain, ws, bs, ga)


def _sgu_backward(proj, gain, ws, bs, ga, dy, dproj, tm):
    s_len = proj.shape[0]

    def body(a_ref, gain_ref, ws_ref, bs_ref, ga_ref, dy_ref, _, da_ref, dgain_ref, dws_ref, dbs_ref, dga_ref):
        tot = None
        for c in range(tm // CHUNK):
            args = _sgu_load(a_ref, gain_ref, ws_ref, bs_ref, ga_ref, c)
            _, vjp = jax.vjp(_sgu_chunk, *args)
            rows = slice(c * CHUNK, (c + 1) * CHUNK)
            du, dv, dz, dgain, dws, dbs, dga = vjp(tuple(dy_ref[rows, h * HEAD:(h + 1) * HEAD] for h in range(A_HEADS)))
            for h in range(A_HEADS):
                da_ref[rows, h * HEAD:(h + 1) * HEAD] = du[h].astype(BF16)
                da_ref[rows, 512 + h * HEAD:512 + (h + 1) * HEAD] = dv[h].astype(BF16)
                da_ref[rows, 1024 + h * HEAD:1024 + (h + 1) * HEAD] = dz[h].astype(BF16)
            part = (dgain, dws, dbs, dga)
            tot = part if tot is None else jax.tree.map(jnp.add, tot, part)
        dgain, dws, dbs, dga = tot
        first = pl.program_id(0) == 0
        _acc(dgain_ref, jnp.concatenate(dgain, axis=0), first)
        _acc(dga_ref, jnp.concatenate(dga, axis=-1), first)
        for h in range(A_HEADS):
            _acc(dws_ref.at[h], dws[h], first)
            _acc(dbs_ref.at[h], dbs[h], first)

    small = [pl.BlockSpec((A_HEADS, HEAD), lambda i: (0, 0)), pl.BlockSpec((A_HEADS, CHUNK, CHUNK), lambda i: (0, 0, 0)),
             pl.BlockSpec((A_HEADS, CHUNK, 1), lambda i: (0, 0, 0)), pl.BlockSpec((1, 512), lambda i: (0, 0))]
    return pl.pallas_call(
        body, name="sgu_backward",
        out_shape=(jax.ShapeDtypeStruct(dproj.shape, BF16),
                   jax.ShapeDtypeStruct((A_HEADS, HEAD), F32), jax.ShapeDtypeStruct((A_HEADS, CHUNK, CHUNK), F32),
                   jax.ShapeDtypeStruct((A_HEADS, CHUNK, 1), F32), jax.ShapeDtypeStruct((1, 512), F32)),
        grid=(s_len // tm,),
        in_specs=_sgu_in_specs(tm) + [pl.BlockSpec((tm, 512), lambda i: (i, 0)), pl.BlockSpec(memory_space=pl.ANY)],
        out_specs=(pl.BlockSpec((tm, W_A), lambda i: (i, O_A // W_A)), *small),
        input_output_aliases={6: 0},
        compiler_params=_params("arbitrary"),
    )(proj, gain, ws, bs, ga, dy, dproj)


def _halo_specs(tm, width, col, n_rows):
    per = tm // HALO
    last = n_rows // HALO - 1
    return [pl.BlockSpec((HALO, width), lambda i: (jnp.maximum(i * per - 1, 0), col)),
            pl.BlockSpec((tm, width), lambda i: (i, col)),
            pl.BlockSpec((HALO, width), lambda i: (jnp.minimum((i + 1) * per, last), col))]


def _conv_masks(tm, s_len):
    r = lax.broadcasted_iota(jnp.int32, (tm + 2 * HALO, 1), 0)
    g = pl.program_id(0) * tm - HALO + r
    return (g >= 0) & (g < s_len), (r >= HALO) & (r < HALO + tm)


def _conv_inputs(b_refs, cw_ref, cb_ref, gb_ref):
    ext = jnp.concatenate([r[...] for r in b_refs], axis=0).astype(F32)
    bb, bc, bh, bz = (ext[:, j * 512:(j + 1) * 512] for j in range(4))
    prm = (cw_ref[0:1, :], cw_ref[1:2, :], cw_ref[2:3, :], cb_ref[...], gb_ref[...])
    return (bb, bc, bh, bz), prm


def _conv_forward(proj, cw, cb, gb, y, tm):
    s_len = proj.shape[0]

    def body(p0, p1, p2, cw_ref, cb_ref, gb_ref, _, o_ref):
        acts, prm = _conv_inputs((p0, p1, p2), cw_ref, cb_ref, gb_ref)
        valid, core = _conv_masks(tm, s_len)
        out = _conv_tile(*acts, *prm, *prm, valid, core)
        o_ref[...] = out[HALO:HALO + tm].astype(BF16)

    vec = pl.BlockSpec((1, 512), lambda i: (0, 0))
    return pl.pallas_call(
        body, name="conv_forward",
        out_shape=jax.ShapeDtypeStruct(y.shape, BF16),
        grid=(s_len // tm,),
        in_specs=_halo_specs(tm, W_B, O_B // W_B, s_len) + [pl.BlockSpec((3, 512), lambda i: (0, 0)), vec, vec,
                                                             pl.BlockSpec(memory_space=pl.ANY)],
        out_specs=pl.BlockSpec((tm, 512), lambda i: (i, 1)),
        input_output_aliases={6: 0},
        compiler_params=_params("parallel"),
    )(proj, proj, proj, cw, cb, gb, y)


def _conv_backward(proj, cw, cb, gb, dy, tm):
    s_len = proj.shape[0]

    def body(p0, p1, p2, cw_ref, cb_ref, gb_ref, d0, d1, d2, db_ref, dcw_ref, dcb_ref, dgb_ref):
        acts, prm = _conv_inputs((p0, p1, p2), cw_ref, cb_ref, gb_ref)
        valid, core = _conv_masks(tm, s_len)
        _, vjp = jax.vjp(lambda a, p: _conv_tile(*a, *p, *prm, valid, core), acts, prm)
        dy_ext = jnp.where(valid, jnp.concatenate([d0[...], d1[...], d2[...]], axis=0), 0.0)
        dacts, dprm = vjp(dy_ext)
        for j in range(4):
            db_ref[:, j * 512:(j + 1) * 512] = dacts[j][HALO:HALO + tm].astype(BF16)
        first = pl.program_id(0) == 0
        _acc(dcw_ref, jnp.concatenate(dprm[0:3], axis=0), first)
        _acc(dcb_ref, dprm[3], first)
        _acc(dgb_ref, dprm[4], first)

    vec = pl.BlockSpec((1, 512), lambda i: (0, 0))
    mat = pl.BlockSpec((3, 512), lambda i: (0, 0))
    return pl.pallas_call(
        body, name="conv_backward",
        out_shape=(jax.ShapeDtypeStruct((s_len, PROJ_W), BF16), jax.ShapeDtypeStruct((3, 512), F32),
                   jax.ShapeDtypeStruct((1, 512), F32), jax.ShapeDtypeStruct((1, 512), F32)),
        grid=(s_len // tm,),
        in_specs=_halo_specs(tm, W_B, O_B // W_B, s_len) + [mat, vec, vec] + _halo_specs(tm, 512, 1, s_len),
        out_specs=(pl.BlockSpec((tm, W_B), lambda i: (i, O_B // W_B)), mat, vec, vec),
        compiler_params=_params("arbitrary"),
    )(proj, proj, proj, cw, cb, gb, dy, dy, dy)


def _cgate_forward(o, proj, gc, y, tm):
    s_len = o.shape[0]

    def body(o_ref, cz_ref, gc_ref, _, y_ref):
        y_ref[...] = _cgate_tile(o_ref[...], cz_ref[...].astype(F32), gc_ref[...]).astype(BF16)

    return pl.pallas_call(
        body, name="cgate_forward",
        out_shape=jax.ShapeDtypeStruct(y.shape, BF16),
        grid=(s_len // tm,),
        in_specs=[pl.BlockSpec((tm, W_CZ), lambda i: (i, 0)), pl.BlockSpec((tm, W_CZ), lambda i: (i, O_CZ // W_CZ)),
                  pl.BlockSpec((1, W_CZ), lambda i: (0, 0)), pl.BlockSpec(memory_space=pl.ANY)],
        out_specs=pl.BlockSpec((tm, W_CZ), lambda i: (i, 1)),
        input_output_aliases={3: 0},
        compiler_params=_params("parallel"),
    )(o, proj, gc, y)


def _cgate_backward(o, proj, gc, dy, dproj, tm, stat_chunk):
    s_len = o.shape[0]
    per_stat = stat_chunk // tm

    def body(o_ref, cz_ref, gc_ref, dy_ref, _, dcz_ref, do_ref, dsum_ref, dgc_ref):
        o = o_ref[...]
        _, vjp = jax.vjp(_cgate_tile, o, cz_ref[...].astype(F32), gc_ref[...])
        do, dcz, dgc = vjp(dy_ref[...])
        dcz_ref[...] = dcz.astype(BF16)
        do_ref[...] = do.astype(BF16)
        ones = jnp.ones((8, HEAD), F32)
        for h in range(C_HEADS):
            cols = slice(h * HEAD, (h + 1) * HEAD)
            sums = lax.dot_general(ones, do[:, cols] * o[:, cols], _NT, precision=lax.Precision.HIGHEST,
                                   preferred_element_type=F32)
            dsum_ref[h, 0] = sums[0:1]
        _acc(dgc_ref, dgc, pl.program_id(0) == 0)

    row = pl.BlockSpec((tm, W_CZ), lambda i: (i, 0))
    vec = pl.BlockSpec((1, W_CZ), lambda i: (0, 0))
    return pl.pallas_call(
        body, name="cgate_backward",
        out_shape=(jax.ShapeDtypeStruct(dproj.shape, BF16), jax.ShapeDtypeStruct((s_len, W_CZ), BF16),
                   jax.ShapeDtypeStruct((C_HEADS, s_len // stat_chunk, 1, stat_chunk), F32), jax.ShapeDtypeStruct((1, W_CZ), F32)),
        grid=(s_len // tm,),
        in_specs=[row, pl.BlockSpec((tm, W_CZ), lambda i: (i, O_CZ // W_CZ)), vec,
                  pl.BlockSpec((tm, W_CZ), lambda i: (i, 1)), pl.BlockSpec(memory_space=pl.ANY)],
        out_specs=(pl.BlockSpec((tm, W_CZ), lambda i: (i, O_CZ // W_CZ)), row,
                   pl.BlockSpec((C_HEADS, 1, 1, tm), lambda i: (0, i // per_stat, 0, i % per_stat)), vec),
        input_output_aliases={4: 0},
        compiler_params=_params("arbitrary"),
    )(o, proj, gc, dy, dproj)


def _mla_small_specs():
    return [pl.BlockSpec((KV_RANK, 2 * C_HEADS * HEAD), lambda i: (0, 0)), pl.BlockSpec((1, KV_RANK), lambda i: (0, 0)),
            pl.BlockSpec((1, HEAD), lambda i: (0, 0)), pl.BlockSpec((1, ROPE), lambda i: (0, 0)),
            pl.BlockSpec((1, HEAD), lambda i: (0, 0)), pl.BlockSpec((1, ROPE), lambda i: (0, 0))]


def _mla_load(m_ref, cos_ref, sin_ref):
    qn = m_ref[:, M_QN:M_QN + C_HEADS * HEAD].astype(F32)
    qr = m_ref[:, M_QR:M_QR + C_HEADS * ROPE].astype(F32)
    ckv = m_ref[:, M_CKV:M_CKV + KV_RANK].astype(F32)
    kr = m_ref[:, M_KR:M_KR + ROPE].astype(F32)
    return qn, qr, ckv, kr, cos_ref[...], sin_ref[...]


def _mla_forward(proj, cosf, sins, wukv, kvg, qng, qrg, kng, krg, tm, kt_chunk, vt_chunk):
    s_len = proj.shape[0]

    def body(m_ref, cos_ref, sin_ref, w_ref, kvg_ref, qng_ref, qrg_ref, kng_ref, krg_ref, q_ref, k_ref, v_ref, kt_ref, vt_ref):
        q, k, v = _mla_tile(*_mla_load(m_ref, cos_ref, sin_ref), w_ref[...], kvg_ref[...], qng_ref[...], qrg_ref[...],
                            kng_ref[...], krg_ref[...])
        for h in range(C_HEADS):
            q_ref[h] = q[h].astype(BF16)
            k_ref[h] = k[h].astype(BF16)
            v_ref[h] = v[h].astype(BF16)
            kt_ref[h, 0] = jnp.concatenate([k[h][:, :HEAD].T, k[h][:, HEAD:].T], axis=0).astype(BF16)
            vt_ref[h, 0] = v[h].T.astype(BF16)

    rope_spec = pl.BlockSpec((tm, ROPE), lambda i: (i, 0))
    qk_spec = pl.BlockSpec((C_HEADS, tm, QK), lambda i: (0, i, 0))
    per_k, per_v = kt_chunk // tm, vt_chunk // tm
    return pl.pallas_call(
        body, name="mla_forward",
        out_shape=(jax.ShapeDtypeStruct((C_HEADS, s_len, QK), BF16), jax.ShapeDtypeStruct((C_HEADS, s_len, QK), BF16),
                   jax.ShapeDtypeStruct((C_HEADS, s_len, HEAD), BF16),
                   jax.ShapeDtypeStruct((C_HEADS, s_len // kt_chunk, QK, kt_chunk), BF16),
                   jax.ShapeDtypeStruct((C_HEADS, s_len // vt_chunk, HEAD, vt_chunk), BF16)),
        grid=(s_len // tm,),
        in_specs=[pl.BlockSpec((tm, W_M), lambda i: (i, O_M // W_M)), rope_spec, rope_spec] + _mla_small_specs(),
        out_specs=(qk_spec, qk_spec, pl.BlockSpec((C_HEADS, tm, HEAD), lambda i: (0, i, 0)),
                   pl.BlockSpec((C_HEADS, 1, QK, tm), lambda i: (0, i // per_k, 0, i % per_k)),
                   pl.BlockSpec((C_HEADS, 1, HEAD, tm), lambda i: (0, i // per_v, 0, i % per_v))),
        compiler_params=_params("parallel"),
    )(proj, cosf, sins, wukv, kvg, qng, qrg, kng, krg)


def _mla_backward(proj, cosf, sins, wukv, kvg, qng, qrg, kng, krg, dqt, dk, dv, dproj, tm):
    s_len = proj.shape[0]
    per_chunk = dqt.shape[3] // tm

    def body(m_ref, cos_ref, sin_ref, w_ref, kvg_ref, qng_ref, qrg_ref, kng_ref, krg_ref, dq_ref, dk_ref, dv_ref, _,
             dm_ref, dw_ref, dkvg_ref, dqng_ref, dqrg_ref, dkng_ref, dkrg_ref):
        qn, qr, ckv, kr, cosf_t, sins_t = _mla_load(m_ref, cos_ref, sin_ref)
        prm = (w_ref[...], kvg_ref[...], qng_ref[...], qrg_ref[...], kng_ref[...], krg_ref[...])
        _, vjp = jax.vjp(lambda a, p: _mla_tile(*a, cosf_t, sins_t, *p), (qn, qr, ckv, kr), prm)
        heads = range(C_HEADS)
        dacts, dprm = vjp((tuple(dq_ref[h, 0].T for h in heads), tuple(dk_ref[h] for h in heads), tuple(dv_ref[h] for h in heads)))
        dm_ref[:, M_QN:M_QN + C_HEADS * HEAD] = dacts[0].astype(BF16)
        dm_ref[:, M_QR:M_QR + C_HEADS * ROPE] = dacts[1].astype(BF16)
        dm_ref[:, M_CKV:M_CKV + KV_RANK] = dacts[2].astype(BF16)
        pad = jnp.zeros((tm, W_M - M_KR - ROPE), F32)
        dm_ref[:, M_KR:W_M] = jnp.concatenate([dacts[3], pad], axis=-1).astype(BF16)
        first = pl.program_id(0) == 0
        for ref, val in zip((dw_ref, dkvg_ref, dqng_ref, dqrg_ref, dkng_ref, dkrg_ref), dprm):
            _acc(ref, val.astype(F32), first)

    rope_spec = pl.BlockSpec((tm, ROPE), lambda i: (i, 0))
    qk_spec = pl.BlockSpec((C_HEADS, tm, QK), lambda i: (0, i, 0))
    small = _mla_small_specs()
    return pl.pallas_call(
        body, name="mla_backward",
        out_shape=(jax.ShapeDtypeStruct(dproj.shape, BF16), jax.ShapeDtypeStruct((KV_RANK, 2 * C_HEADS * HEAD), F32),
                   jax.ShapeDtypeStruct((1, KV_RANK), F32), jax.ShapeDtypeStruct((1, HEAD), F32),
                   jax.ShapeDtypeStruct((1, ROPE), F32), jax.ShapeDtypeStruct((1, HEAD), F32),
                   jax.ShapeDtypeStruct((1, ROPE), F32)),
        grid=(s_len // tm,),
        in_specs=[pl.BlockSpec((tm, W_M), lambda i: (i, O_M // W_M)), rope_spec, rope_spec] + small
                 + [pl.BlockSpec((C_HEADS, 1, QK, tm), lambda i: (0, i // per_chunk, 0, i % per_chunk)), qk_spec,
                    pl.BlockSpec((C_HEADS, tm, HEAD), lambda i: (0, i, 0)), pl.BlockSpec(memory_space=pl.ANY)],
        out_specs=(pl.BlockSpec((tm, W_M), lambda i: (i, O_M // W_M)), *small),
        input_output_aliases={12: 0},
        compiler_params=_params("arbitrary"),
    )(proj, cosf, sins, wukv, kvg, qng, qrg, kng, krg, dqt, dk, dv, dproj)


def _attention_forward(q, k, vt, tq, stat_chunk, comm=None):
    n_heads, s_len, _ = q.shape
    n_chunks, _, ck = vt.shape[1:]
    c_ops, c_in_specs, c_shapes, c_out_specs, c_sems, c_begin, c_end = _riding_exchange(comm, 2)
    n_c = len(c_ops)

    def body(q_ref, k_ref, vt_ref, *rest):
        c_ins, (o_ref, lse_ref), c_outs, sems = rest[:n_c], rest[n_c:n_c + 2], rest[n_c + 2:2 * n_c + 2], rest[2 * n_c + 2:]
        c_begin(c_ins, c_outs, sems)
        q_t = q_ref[0]

        def step(j, carry):
            m_old, l_old, acc = carry
            k_j = k_ref[0, pl.ds(pl.multiple_of(j * ck, ck), ck), :]
            s = lax.dot_general(k_j, q_t, _NT, preferred_element_type=F32)
            m_new = jnp.maximum(m_old, jnp.max(s, axis=0, keepdims=True))
            p = jnp.exp2(s - m_new)
            alpha = jnp.exp2(m_old - m_new)
            l_new = alpha * l_old + jnp.sum(p, axis=0, keepdims=True)
            acc = alpha * acc + jnp.dot(vt_ref[0, j], p.astype(BF16), preferred_element_type=F32)
            return m_new, l_new, acc

        init = (jnp.full((1, tq), -jnp.inf, F32), jnp.zeros((1, tq), F32), jnp.zeros((HEAD, tq), F32))
        m_fin, l_fin, acc = lax.fori_loop(0, n_chunks, step, init)
        o_ref[...] = (acc / l_fin).T
        lse_ref[0, 0] = m_fin + jnp.log2(l_fin)
        c_end(c_ins, c_outs, sems)

    per_stat = stat_chunk // tq
    outs = pl.pallas_call(
        body, name="attention_forward",
        out_shape=(jax.ShapeDtypeStruct((s_len, n_heads * HEAD), F32),
                   jax.ShapeDtypeStruct((n_heads, s_len // stat_chunk, 1, stat_chunk), F32), *c_shapes),
        grid=(n_heads, s_len // tq),
        in_specs=[pl.BlockSpec((1, tq, QK), lambda h, i: (h, i, 0)), pl.BlockSpec((1, s_len, QK), lambda h, i: (h, 0, 0)),
                  pl.BlockSpec((1, n_chunks, HEAD, ck), lambda h, i: (h, 0, 0, 0))] + c_in_specs,
        out_specs=(pl.BlockSpec((tq, HEAD), lambda h, i: (i, h)),
                   pl.BlockSpec((1, 1, 1, tq), lambda h, i: (h, i // per_stat, 0, i % per_stat)), *c_out_specs),
        scratch_shapes=c_sems,
        compiler_params=_params("arbitrary", "arbitrary") if comm else _params("parallel", "parallel"),
    )(q, k, vt, *c_ops)
    return outs[0], outs[1], outs[2:]


def _attention_backward(q, k, kt, v, do, lse, dsum, comm=None):
    n_heads, s_len, _ = q.shape
    tk = kt.shape[3]
    n_q, _, cq = lse.shape[1:]
    c_ops, c_in_specs, c_shapes, c_out_specs, c_sems, c_begin, c_end = _riding_exchange(comm, 2)
    n_c = len(c_ops)

    def body(q_ref, k_ref, kt_ref, v_ref, do_ref, lse_ref, dsum_ref, *rest):
        c_ins, (dqt_ref, dk_ref, dv_ref), c_outs, sems = rest[:n_c], rest[n_c:n_c + 3], rest[n_c + 3:2 * n_c + 3], rest[2 * n_c + 3:]
        c_begin(c_ins, c_outs, sems)
        first = pl.program_id(1) == 0
        k_j, kt_j, v_j = k_ref[0], kt_ref[0, 0], v_ref[0]

        def step(i, carry):
            dk, dv = carry
            rows = pl.ds(pl.multiple_of(i * cq, cq), cq)
            q_i, do_i = q_ref[0, rows, :], do_ref[rows, :]
            s = lax.dot_general(k_j, q_i, _NT, preferred_element_type=F32)
            p = jnp.exp2(s - lse_ref[0, i])
            dp = lax.dot_general(v_j, do_i, _NT, preferred_element_type=F32)
            ds = (p * (dp - dsum_ref[0, i]) * LN_2).astype(BF16)
            dv = dv + jnp.dot(p.astype(BF16), do_i, preferred_element_type=F32)
            dk = dk + jnp.dot(ds, q_i, preferred_element_type=F32)
            _acc(dqt_ref.at[0, i], jnp.dot(kt_j, ds, preferred_element_type=F32), first)
            return dk, dv

        dk, dv = lax.fori_loop(0, n_q, step, (jnp.zeros((tk, QK), F32), jnp.zeros((tk, HEAD), F32)))
        dk_ref[0] = dk
        dv_ref[0] = dv
        c_end(c_ins, c_outs, sems)

    stat = pl.BlockSpec((1, n_q, 1, cq), lambda h, j: (h, 0, 0, 0))
    outs = pl.pallas_call(
        body, name="attention_backward",
        out_shape=(jax.ShapeDtypeStruct((n_heads, n_q, QK, cq), F32), jax.ShapeDtypeStruct((n_heads, s_len, QK), F32),
                   jax.ShapeDtypeStruct((n_heads, s_len, HEAD), F32), *c_shapes),
        grid=(n_heads, s_len // tk),
        in_specs=[pl.BlockSpec((1, s_len, QK), lambda h, j: (h, 0, 0)), pl.BlockSpec((1, tk, QK), lambda h, j: (h, j, 0)),
                  pl.BlockSpec((1, 1, QK, tk), lambda h, j: (h, j, 0, 0)),
                  pl.BlockSpec((1, tk, HEAD), lambda h, j: (h, j, 0)), pl.BlockSpec((s_len, HEAD), lambda h, j: (0, h)),
                  stat, stat] + c_in_specs,
        out_specs=(pl.BlockSpec((1, n_q, QK, cq), lambda h, j: (h, 0, 0, 0)), pl.BlockSpec((1, tk, QK), lambda h, j: (h, j, 0)),
                   pl.BlockSpec((1, tk, HEAD), lambda h, j: (h, j, 0)), *c_out_specs),
        scratch_shapes=c_sems,
        compiler_params=_params("arbitrary", "arbitrary") if comm else _params("parallel", "arbitrary"),
    )(q, k, kt, v, do, lse, dsum, *c_ops)
    return outs[0], outs[1], outs[2], outs[3:]


def _exchange(arrs, gather, name):
    n = len(arrs)

    def body(*refs):
        plan = _exchange_plan(refs[:n], refs[n:2 * n], gather, *refs[2 * n:])
        _exchange_start(plan)
        _exchange_wait(plan)

    any_spec = pl.BlockSpec(memory_space=pl.ANY)
    return pl.pallas_call(
        body, name=name,
        out_shape=_exchange_out_shapes(arrs, gather),
        in_specs=[any_spec] * n,
        out_specs=tuple([any_spec] * n),
        scratch_shapes=_exchange_semaphores(n),
        compiler_params=pltpu.CompilerParams(has_side_effects=True),
    )(*arrs)


def _exchange_out_shapes(arrs, gather):
    return tuple(jax.ShapeDtypeStruct((N_DEV, *(a.shape if g else a.shape[1:])), a.dtype) for a, g in zip(arrs, gather))


def _exchange_semaphores(n):
    n_remote = n * (N_DEV - 1)
    return [pltpu.SemaphoreType.DMA((n_remote,)), pltpu.SemaphoreType.DMA((n_remote,)), pltpu.SemaphoreType.DMA((n,))]


def _exchange_plan(ins, outs, gather, send_sems, recv_sems, local_sems):
    n = len(ins)
    x, y, c = lax.axis_index("x"), lax.axis_index("y"), lax.axis_index("c")
    me = 4 * x + 2 * y + c

    def block_for(a, dev):
        return ins[a] if gather[a] else ins[a].at[dev]

    local = [pltpu.make_async_copy(block_for(a, me), outs[a].at[me], local_sems.at[a]) for a in range(n)]
    remote = []
    for k in range(1, N_DEV):
        px = 1 - x if k & 4 else x
        py = 1 - y if k & 2 else y
        pc = 1 - c if k & 1 else c
        peer = 4 * px + 2 * py + pc
        for a in range(n):
            idx = a * (N_DEV - 1) + k - 1
            send = pltpu.make_async_remote_copy(
                src_ref=block_for(a, peer), dst_ref=outs[a].at[me], send_sem=send_sems.at[idx], recv_sem=recv_sems.at[idx],
                device_id=(px, py, pc), device_id_type=pl.DeviceIdType.MESH)
            arrive = pltpu.make_async_remote_copy(
                src_ref=block_for(a, peer), dst_ref=outs[a].at[peer], send_sem=send_sems.at[idx], recv_sem=recv_sems.at[idx],
                device_id=(px, py, pc), device_id_type=pl.DeviceIdType.MESH)
            remote.append((send, arrive))
    return local, remote


def _exchange_start(plan):
    local, remote = plan
    for cp in local:
        cp.start()
    for send, _ in remote:
        send.start()


def _exchange_wait(plan):
    local, remote = plan
    for send, arrive in remote:
        send.wait_send()
        arrive.wait_recv()
    for cp in local:
        cp.wait()


def _riding_exchange(comm, n_grid):
    if comm is None:
        return [], [], (), (), [], lambda *_: None, lambda *_: None
    arrs, gather = comm
    n = len(arrs)
    any_spec = pl.BlockSpec(memory_space=pl.ANY)

    def begin(ins, outs, sems):
        @pl.when(functools.reduce(jnp.logical_and, [pl.program_id(d) == 0 for d in range(n_grid)]))
        def _():
            _exchange_start(_exchange_plan(ins, outs, gather, *sems))

    def end(ins, outs, sems):
        @pl.when(functools.reduce(jnp.logical_and, [pl.program_id(d) == pl.num_programs(d) - 1 for d in range(n_grid)]))
        def _():
            _exchange_wait(_exchange_plan(ins, outs, gather, *sems))

    return (list(arrs), [any_spec] * n, _exchange_out_shapes(arrs, gather), tuple([any_spec] * n), _exchange_semaphores(n),
            begin, end)


ADAM_TILE_ELEMS = 256 * 1024


def _sum_adam(parts, w, m, v, layer, prev, name):
    n_parts, r, c = parts.shape
    tm, tc = r, c
    if r % 16 == 0:
        while tm * c > ADAM_TILE_ELEMS and tm % 16 == 0:
            tm //= 2
    else:
        while r * tc > ADAM_TILE_ELEMS and tc % 256 == 0:
            tc //= 2

    def body(p_ref, w_ref, m_ref, v_ref, *rest):
        g_ref, d_ref, nm_ref, nv_ref = rest[-4:]
        g = p_ref[0].astype(F32)
        for s in range(1, n_parts):
            g = g + p_ref[s].astype(F32)
        m_new = ADAM_B1 * m_ref[...] + (1.0 - ADAM_B1) * g
        v_new = ADAM_B2 * v_ref[...] + (1.0 - ADAM_B2) * (g * g)
        m_hat = m_new / (1.0 - ADAM_B1 ** ADAM_STEP)
        v_hat = v_new / (1.0 - ADAM_B2 ** ADAM_STEP)
        g_ref[...] = g
        d_ref[...] = -ADAM_LR * (m_hat / (jnp.sqrt(v_hat) + ADAM_EPS) + ADAM_WD * w_ref[...])
        nm_ref[...] = m_new
        nv_ref[...] = v_new

    slab = pl.BlockSpec((None, tm, tc), lambda i, j: (layer, i, j))
    n_prev = 0 if prev is None else 4
    return pl.pallas_call(
        body, name=name,
        out_shape=(jax.ShapeDtypeStruct(w.shape, F32),) * 4,
        grid=(r // tm, c // tc),
        in_specs=[pl.BlockSpec((n_parts, tm, tc), lambda i, j: (0, i, j)), slab, slab, slab]
                 + [pl.BlockSpec(memory_space=pl.ANY)] * n_prev,
        out_specs=(slab, slab, slab, slab),
        input_output_aliases={4 + j: j for j in range(n_prev)},
        compiler_params=_params("parallel", "parallel"),
    )(parts, w, m, v, *(prev or ()))


def _permute_in(w):
    k = w.shape[0]
    q = w[:, 3584:5120].reshape(k, C_HEADS, QK)
    return jnp.concatenate(
        [w[:, 1536:3584], w[:, 5696:6720], w[:, 0:1536], q[:, :, :HEAD].reshape(k, C_HEADS * HEAD),
         q[:, :, HEAD:].reshape(k, C_HEADS * ROPE), w[:, 5120:5632], w[:, 5632:5696],
         jnp.zeros((k, PROJ_W - IN_WIDTH), w.dtype)], axis=1)


def _unpermute_in(g):
    k = g.shape[0]
    qn = g[:, O_M + M_QN:O_M + M_QR].reshape(k, C_HEADS, HEAD)
    qr = g[:, O_M + M_QR:O_M + M_CKV].reshape(k, C_HEADS, ROPE)
    q = jnp.concatenate([qn, qr], axis=-1).reshape(k, C_HEADS * QK)
    return jnp.concatenate(
        [g[:, O_A:O_A + W_A], g[:, O_B:O_B + W_B], q, g[:, O_M + M_CKV:O_M + M_KR],
         g[:, O_M + M_KR:O_M + M_KR + ROPE], g[:, O_CZ:O_CZ + W_CZ]], axis=1)


SMALL = ("attn_norm", "sgu_norm", "w_spatial", "b_spatial", "conv_b", "kv_norm", "q_nope_norm", "q_rope_norm",
         "k_nope_norm", "k_rope_norm", "out_norm", "ple_norm")
PACK_ROWS = 256


def _pack(tensors):
    flat = jnp.concatenate([t.reshape(-1) for t in tensors])
    rows = -(-flat.shape[0] // (128 * PACK_ROWS)) * PACK_ROWS
    return jnp.pad(flat, (0, rows * 128 - flat.shape[0])).reshape(rows, 128)


def _unpack(packed, like):
    flat = packed.reshape(-1)
    out, pos = [], 0
    for t in like:
        out.append(flat[pos:pos + t.size].reshape(t.shape))
        pos += t.size
    return out


def _tile(s_len, want):
    return min(want, s_len)


ATT_FWD_QUERIES = 512
ATT_FWD_KEYS = 8192
ATT_BWD_KEYS = 512
ATT_BWD_QUERIES = 2048


def _layer_forward(h, p_l, cosf, sins, w, sm, comm, comm_rest):
    s_len = h.shape[0]
    tm = _tile(s_len, 512)
    proj, hn, rest = _norm_matmul(h, sm["attn_norm"], w["w_in"], _tile(s_len, 1024), 768, comm_rest)
    if comm_rest is not None:
        w = {**w, **_assemble_rest(rest)}
    ga, gb, gc = sm["out_norm"][:, 0:512], sm["out_norm"][:, 512:1024], sm["out_norm"][:, 1024:2048]
    y = _sgu_forward(proj, sm["sgu_norm"], sm["w_spatial"], sm["b_spatial"], ga, _tile(s_len, 256))
    y = _conv_forward(proj, w["conv_w"], sm["conv_b"], gb, y, _tile(s_len, 256))
    q, k, v, kt, vt = _mla_forward(proj, cosf, sins, w["w_ukv"], sm["kv_norm"], sm["q_nope_norm"], sm["q_rope_norm"],
                                   sm["k_nope_norm"], sm["k_rope_norm"], _tile(s_len, 256), _tile(s_len, ATT_BWD_KEYS),
                                   _tile(s_len, ATT_FWD_KEYS))
    o, lse, arrived = _attention_forward(q, k, vt, _tile(s_len, ATT_FWD_QUERIES), _tile(s_len, ATT_BWD_QUERIES), comm)
    y = _cgate_forward(o, proj, gc, y, _tile(s_len, 256))
    h1 = _out_matmul(h, y, w["w_out"], _tile(s_len, 1024), 1024)
    h2, n1, gate, pp = _ple_forward(h1, sm["ple_norm"], p_l, w["w_ple_gate"], w["w_ple_proj"], tm, 1024)
    saved = dict(h=h, hn=hn, proj=proj, y=y, q=q, k=k, v=v, kt=kt, o=o, lse=lse, h1=h1, n1=n1, gate=gate, pp=pp)
    return h2, saved, w, arrived


def _layer_backward(dh2, p_l, cosf, sins, w, sm, sv, comm, scatter_own):
    s_len = dh2.shape[0]
    tm = _tile(s_len, 512)
    tr = _tile(s_len, 256)
    big, small = {}, {}
    dh1, small["ple_norm"], dgp, dpp = _ple_backward(dh2, sv["gate"], sv["pp"], w["w_ple_gate"], sv["h1"], sm["ple_norm"], tm)
    big["w_ple_proj"] = _matmul_tn(p_l, dpp, _tile(s_len, 2048), PLE_DIM, 1024, "grad_w_ple_proj")
    big["w_ple_gate"] = _matmul_tn(sv["n1"], dgp, _tile(s_len, 2048), 1024, 1024, "grad_w_ple_gate")
    dy, _ = _matmul_nt(dh1, w["w_out"], _tile(s_len, 1024), 1024, "grad_branches")
    big["w_out"] = _matmul_tn(sv["y"], dh1, _tile(s_len, 2048), 1024, 1024, "grad_w_out")
    ga, gb, gc = sm["out_norm"][:, 0:512], sm["out_norm"][:, 512:1024], sm["out_norm"][:, 1024:2048]
    dproj, dcw, small["conv_b"], dgb = _conv_backward(sv["proj"], w["conv_w"], sm["conv_b"], gb, dy, tr)
    big["conv_w"] = dcw
    dproj, do, dsum, dgc = _cgate_backward(sv["o"], sv["proj"], gc, dy, dproj, tm, _tile(s_len, ATT_BWD_QUERIES))
    dqt, dk, dv, arrived = _attention_backward(sv["q"], sv["k"], sv["kt"], sv["v"], do, sv["lse"], dsum, comm)
    (dproj, big["w_ukv"], small["kv_norm"], small["q_nope_norm"], small["q_rope_norm"], small["k_nope_norm"],
     small["k_rope_norm"]) = _mla_backward(sv["proj"], cosf, sins, w["w_ukv"], sm["kv_norm"], sm["q_nope_norm"],
                                            sm["q_rope_norm"], sm["k_nope_norm"], sm["k_rope_norm"], dqt, dk, dv, dproj, tr)
    dproj, small["sgu_norm"], small["w_spatial"], small["b_spatial"], dga = _sgu_backward(
        sv["proj"], sm["sgu_norm"], sm["w_spatial"], sm["b_spatial"], ga, dy, dproj, tm)
    small["out_norm"] = jnp.concatenate([dga, dgb, dgc], axis=1)
    big["w_in"] = _matmul_tn(sv["hn"], dproj, _tile(s_len, 2048), 512, 2304, "grad_w_in")
    parts = _layer_parts(big)
    dhn, arrived_own = _matmul_nt(dproj, w["w_in"], _tile(s_len, 1024), 256, "grad_attn_norm_in",
                                  (parts, [False] * len(parts)) if scatter_own else None)
    dh, small["attn_norm"] = _rms_backward(sv["h"], sm["attn_norm"], dhn, dh1, tr, "attn_norm_backward")
    return dh, parts, big["conv_w"], small, arrived, arrived_own if scatter_own else None


def _layer_small(params, layer):
    return dict(
        attn_norm=params["attn_norm"][layer][None, :], sgu_norm=params["sgu_norm"][layer],
        w_spatial=params["w_spatial"][layer], b_spatial=params["b_spatial"][layer][:, :, None],
        conv_b=params["conv_b"][layer][None, :], kv_norm=params["kv_norm"][layer][None, :],
        q_nope_norm=params["q_nope_norm"][layer][None, :], q_rope_norm=params["q_rope_norm"][layer][None, :],
        k_nope_norm=params["k_nope_norm"][layer][None, :], k_rope_norm=params["k_rope_norm"][layer][None, :],
        out_norm=params["out_norm"][layer][None, :], ple_norm=params["ple_norm"][layer][None, :])


BIG = ("w_in", "w_ukv", "w_out", "w_ple_gate", "w_ple_proj")


def _assemble_w_in(g_in):
    return _permute_in(g_in.transpose(1, 0, 2).reshape(g_in.shape[1], IN_WIDTH))


def _assemble_rest(gathered):
    g_ukv, g_out, g_gate, g_proj = gathered
    w_ukv = g_ukv.reshape(N_DEV, KV_RANK, 2, HEAD).transpose(1, 2, 0, 3).reshape(KV_RANK, 2 * C_HEADS * HEAD)
    return dict(w_ukv=w_ukv, w_out=g_out.reshape(D_MODEL, D_MODEL), w_ple_gate=g_gate.reshape(D_MODEL, D_MODEL),
                w_ple_proj=g_proj.transpose(1, 0, 2).reshape(PLE_DIM, D_MODEL))


def _layer_parts(big):
    g_in = big["w_in"]
    return [
        _unpermute_in(g_in).reshape(g_in.shape[0], N_DEV, -1).transpose(1, 2, 0).astype(BF16),
        big["w_ukv"].reshape(KV_RANK, 2, N_DEV, HEAD).transpose(2, 0, 1, 3).reshape(N_DEV, KV_RANK, 2 * HEAD).astype(BF16),
        big["w_out"].reshape(N_DEV, -1, D_MODEL).astype(BF16),
        big["w_ple_gate"].reshape(N_DEV, -1, D_MODEL).astype(BF16),
        big["w_ple_proj"].reshape(PLE_DIM, N_DEV, -1).transpose(1, 0, 2).astype(BF16)]


def _step_local(xs, ps, pos, target, shards, conv_w, params):
    inv = 1.0 / (ROPE_BASE ** (jnp.arange(0, ROPE, 2, dtype=F32) / ROPE))
    ang = pos.astype(F32)[:, None] * inv
    cos, sin = jnp.cos(ang), jnp.sin(ang)
    cosf = jnp.concatenate([cos, cos], axis=-1)
    sins = jnp.concatenate([-sin, sin], axis=-1)

    def gather_of(names, layer):
        return [shards[n][layer] for n in names], [True] * len(names)

    h = xs
    saved, weights = [], []
    smalls = [_layer_small(params, layer) for layer in range(DEPTH)]
    (first_w_in,) = _exchange(*gather_of(BIG[:1], 0), "gather_first_w_in")
    w = dict(w_in=_assemble_w_in(first_w_in), conv_w=conv_w[0])
    for layer in range(DEPTH):
        comm = gather_of(BIG, layer + 1) if layer + 1 < DEPTH else None
        comm_rest = gather_of(BIG[1:], 0) if layer == 0 else None
        h, sv, w, arrived = _layer_forward(h, ps[layer], cosf, sins, w, smalls[layer], comm, comm_rest)
        saved.append(sv)
        weights.append(w)
        if comm is not None:
            w = dict(w_in=_assemble_w_in(arrived[0]), conv_w=conv_w[layer + 1], **_assemble_rest(arrived[1:]))
    dh, loss = _loss_grad(h, target, _tile(h.shape[0], 512))
    received, conv_grads, small_grads = [None] * DEPTH, [None] * DEPTH, [None] * DEPTH
    comm = None
    for layer in reversed(range(DEPTH)):
        dh, parts, conv_grads[layer], small_grads[layer], arrived, arrived_own = _layer_backward(
            dh, ps[layer], cosf, sins, weights[layer], smalls[layer], saved[layer], comm, layer == 0)
        if comm is not None:
            received[layer + 1] = arrived
        comm = (parts, [False] * len(parts))
    received[0] = arrived_own
    return loss, dh, received, conv_grads, small_grads


def kernel(x, p, positions, attn_norm, w_in, sgu_norm, w_spatial, b_spatial, conv_w, conv_b, kv_norm, w_ukv, q_nope_norm, q_rope_norm, k_nope_norm, k_rope_norm, out_norm, w_out, ple_norm, w_ple_gate, w_ple_proj, loss_target, m_attn_norm, m_w_in, m_sgu_norm, m_w_spatial, m_b_spatial, m_conv_w, m_conv_b, m_kv_norm, m_w_ukv, m_q_nope_norm, m_q_rope_norm, m_k_nope_norm, m_k_rope_norm, m_out_norm, m_w_out, m_ple_norm, m_w_ple_gate, m_w_ple_proj, v_attn_norm, v_w_in, v_sgu_norm, v_w_spatial, v_b_spatial, v_conv_w, v_conv_b, v_kv_norm, v_w_ukv, v_q_nope_norm, v_q_rope_norm, v_k_nope_norm, v_k_rope_norm, v_out_norm, v_w_out, v_ple_norm, v_w_ple_gate, v_w_ple_proj):
    order = ("attn_norm", "w_in", "sgu_norm", "w_spatial", "b_spatial", "conv_w", "conv_b", "kv_norm", "w_ukv",
             "q_nope_norm", "q_rope_norm", "k_nope_norm", "k_rope_norm", "out_norm", "w_out", "ple_norm", "w_ple_gate",
             "w_ple_proj")
    wts = dict(zip(order, (attn_norm, w_in, sgu_norm, w_spatial, b_spatial, conv_w, conv_b, kv_norm, w_ukv, q_nope_norm,
                           q_rope_norm, k_nope_norm, k_rope_norm, out_norm, w_out, ple_norm, w_ple_gate, w_ple_proj)))
    mom = dict(zip(order, (m_attn_norm, m_w_in, m_sgu_norm, m_w_spatial, m_b_spatial, m_conv_w, m_conv_b, m_kv_norm, m_w_ukv,
                           m_q_nope_norm, m_q_rope_norm, m_k_nope_norm, m_k_rope_norm, m_out_norm, m_w_out, m_ple_norm,
                           m_w_ple_gate, m_w_ple_proj)))
    var = dict(zip(order, (v_attn_norm, v_w_in, v_sgu_norm, v_w_spatial, v_b_spatial, v_conv_w, v_conv_b, v_kv_norm, v_w_ukv,
                           v_q_nope_norm, v_q_rope_norm, v_k_nope_norm, v_k_rope_norm, v_out_norm, v_w_out, v_ple_norm,
                           v_w_ple_gate, v_w_ple_proj)))

    conv_shard = wts["conv_w"]
    (conv_all,) = _exchange([conv_shard.reshape(-1, 128)], [True], "gather_conv_w")
    conv_full = conv_all.reshape(N_DEV, DEPTH, 3, -1).transpose(1, 2, 0, 3).reshape(DEPTH, 3, -1)
    shards = {n: wts[n].astype(BF16) for n in BIG}
    loss_part, grad_x, received, conv_grads, small_grads = _step_local(
        x[0], p[:, 0], positions[0], loss_target[0], shards, conv_full, wts)
    loss = lax.psum(loss_part[0, 0], ("x", "y", "c"))

    def small_grad(name):
        g = jnp.stack([sg[name] for sg in small_grads])
        return g.reshape(wts[name].shape)

    conv_grad = jnp.stack(conv_grads)
    like = [wts[n] for n in SMALL] + [conv_grad]
    packed = _pack([small_grad(n) for n in SMALL] + [conv_grad])
    (small_parts,) = _exchange([packed], [True], "gather_small_grads")
    filler = [jnp.zeros_like(conv_grad), jnp.zeros_like(conv_grad), jnp.ones_like(conv_grad)]
    small_out = _sum_adam(small_parts, *(_pack([src[n] for n in SMALL] + [fill])[None] for src, fill in zip((wts, mom, var), filler)),
                          0, None, "adam_small")
    unpacked = [_unpack(o[0], like) for o in small_out]
    results = {n: vals for n, vals in zip(SMALL, zip(*[u[:-1] for u in unpacked]))}
    me = 4 * lax.axis_index("x") + 2 * lax.axis_index("y") + lax.axis_index("c")
    width = conv_shard.shape[2]
    conv_local = lax.dynamic_slice_in_dim(unpacked[0][-1], me * width, width, axis=2)
    as_slab = (lambda t: t.reshape(1, -1, width))
    conv_out = _sum_adam(as_slab(conv_local), as_slab(conv_shard), as_slab(mom["conv_w"]), as_slab(var["conv_w"]), 0, None,
                         "adam_conv_w")
    results["conv_w"] = tuple(o.reshape(conv_shard.shape) for o in conv_out)

    for j, name in enumerate(BIG):
        view = (lambda t: jnp.swapaxes(t, 1, 2)) if name == "w_in" else (lambda t: t)
        outs = None
        for layer in range(DEPTH):
            outs = _sum_adam(received[layer][j], view(wts[name]), view(mom[name]), view(var[name]), layer, outs, "adam_" + name)
        results[name] = tuple(view(o) for o in outs)

    grads, deltas, new_m, new_v = ([results[n][j] for n in order] for j in range(4))
    return (loss, grad_x[None], *grads, *deltas, *new_m, *new_v)
```

```python
import functools

import jax
import jax.numpy as jnp
from jax import lax
from jax.experimental import pallas as pl
from jax.experimental.pallas import tpu as pltpu

F32 = jnp.float32
BF16 = jnp.bfloat16

N_DEV = 8
DEPTH = 4
D_MODEL = 2048
EPS = 1e-6
CHUNK = 128
A_HEADS = 4
HEAD = 128
ROPE = 64
HALF = ROPE // 2
C_HEADS = 8
KV_RANK = 512
PLE_DIM = 256
ROPE_BASE = 10000.0
IN_WIDTH = 6720
QK = HEAD + ROPE
SCALE = QK ** -0.5
LOG2_E = 1.4426950408889634
LN_2 = 0.6931471805599453
Q_SCALE = SCALE * LOG2_E
HALO = 8

O_B = 0
W_B = 2048
O_CZ = 2048
W_CZ = 1024
O_A = 3072
W_A = 1536
O_M = 4608
W_M = 2304
M_QN, M_QR, M_CKV, M_KR = 0, 1024, 1536, 2048
PROJ_W = 6912

ADAM_LR = 0.001
ADAM_B1 = 0.9
ADAM_B2 = 0.999
ADAM_EPS = 1e-08
ADAM_WD = 0.01
ADAM_STEP = 10

VMEM_LIMIT = 56 * 1024 * 1024

_NT = (((1,), (1,)), ((), ()))
_TN = (((0,), (0,)), ((), ()))


def _params(*sem):
    return pltpu.CompilerParams(dimension_semantics=sem, vmem_limit_bytes=VMEM_LIMIT)


@jax.custom_vjp
def _bdot(a, b):
    return jnp.dot(a.astype(BF16), b.astype(BF16), preferred_element_type=F32)


def _bdot_fwd(a, b):
    return _bdot(a, b), (a, b)


def _bdot_bwd(res, g):
    a, b = res
    gb = g.astype(BF16)
    da = lax.dot_general(gb, b.astype(BF16), _NT, preferred_element_type=F32)
    db = lax.dot_general(a.astype(BF16), gb, _TN, preferred_element_type=F32)
    return da.astype(a.dtype), db.astype(b.dtype)


_bdot.defvjp(_bdot_fwd, _bdot_bwd)


@functools.partial(jax.custom_vjp, nondiff_argnums=(1,))
def _split(x, n):
    w = x.shape[-1] // n
    return tuple(x[:, i * w:(i + 1) * w] for i in range(n))


def _split_fwd(x, n):
    return _split(x, n), None


def _split_bwd(n, _, gs):
    return (jnp.concatenate(gs, axis=-1),)


_split.defvjp(_split_fwd, _split_bwd)


@functools.partial(jax.custom_vjp, nondiff_argnums=(1,))
def _shift_rows(x, k):
    return pltpu.roll(x, k % x.shape[0], 0)


def _shift_rows_fwd(x, k):
    return _shift_rows(x, k), None


def _shift_rows_bwd(k, _, g):
    return (_shift_rows(g, -k),)


_shift_rows.defvjp(_shift_rows_fwd, _shift_rows_bwd)


@jax.custom_vjp
def _swap_halves(x):
    h = x.shape[-1] // 2
    return jnp.concatenate([x[:, h:], x[:, :h]], axis=-1)


def _swap_halves_fwd(x):
    return _swap_halves(x), None


def _swap_halves_bwd(_, g):
    return (_swap_halves(g),)


_swap_halves.defvjp(_swap_halves_fwd, _swap_halves_bwd)


def _rms(x, g):
    return x * lax.rsqrt(jnp.mean(x * x, axis=-1, keepdims=True) + EPS) * g


def _rope(x, cosf, sins):
    return x * cosf + _swap_halves(x) * sins


def _sgu_chunk(u, v, z, gain, ws, bs, ga):
    ys = []
    for h in range(A_HEADS):
        vn = _rms(v[h], gain[h])
        s = _bdot(ws[h], vn) + bs[h]
        ys.append(u[h] * s * jax.nn.silu(z[h]))
    ss = sum(jnp.sum(y * y, axis=-1, keepdims=True) for y in ys) * (1.0 / (A_HEADS * HEAD))
    r = lax.rsqrt(ss + EPS)
    return tuple(ys[h] * r * ga[h] for h in range(A_HEADS))


def _conv_tile(bb, bc, bh, bz, w0, w1, w2, cb, gb, w0h, w1h, w2h, cbh, gbh, valid, core):
    t = jnp.where(valid, bc * bh, 0.0)
    y = (jnp.where(core, cb, cbh)
         + _shift_rows(t, 1) * jnp.where(core, w0, w0h)
         + t * jnp.where(core, w1, w1h)
         + _shift_rows(t, -1) * jnp.where(core, w2, w2h))
    return _rms(bb * y * jax.nn.silu(bz), jnp.where(core, gb, gbh))


def _cgate_tile(o, cz, gc):
    return _rms(o * jax.nn.silu(cz), gc)


def _mla_tile(qn, qr, ckv, kr, cosf, sins, wukv, kvg, qng, qrg, kng, krg):
    kv = _split(_bdot(_rms(ckv, kvg), wukv), 2 * C_HEADS)
    k_r = _rope(_rms(kr, krg), cosf, sins)
    qn_h = _split(qn, C_HEADS)
    qr_h = _split(qr, C_HEADS)
    q, k, v = [], [], []
    for h in range(C_HEADS):
        q.append(jnp.concatenate([_rms(qn_h[h], qng), _rope(_rms(qr_h[h], qrg), cosf, sins)], axis=-1) * Q_SCALE)
        k.append(jnp.concatenate([_rms(kv[h], kng), k_r], axis=-1))
        v.append(kv[C_HEADS + h])
    return tuple(q), tuple(k), tuple(v)


def _norm_matmul(h, gain, w, tm, tn, comm=None):
    s_len, k = h.shape
    n = w.shape[1]
    c_ops, c_in_specs, c_shapes, c_out_specs, c_sems, c_begin, c_end = _riding_exchange(comm, 2)
    n_c = len(c_ops)

    def body(h_ref, g_ref, w_ref, *rest):
        c_ins, (o_ref, hn_ref), c_outs, sems = rest[:n_c], rest[n_c:n_c + 2], rest[n_c + 2:2 * n_c + 2], rest[2 * n_c + 2:]
        c_begin(c_ins, c_outs, sems)

        @pl.when(pl.program_id(1) == 0)
        def _():
            hn_ref[...] = _rms(h_ref[...], g_ref[...]).astype(BF16)

        o_ref[...] = jnp.dot(hn_ref[...], w_ref[...], preferred_element_type=F32).astype(BF16)
        c_end(c_ins, c_outs, sems)

    outs = pl.pallas_call(
        body, name="norm_matmul",
        out_shape=(jax.ShapeDtypeStruct((s_len, n), BF16), jax.ShapeDtypeStruct((s_len, k), BF16), *c_shapes),
        grid=(s_len // tm, n // tn),
        in_specs=[pl.BlockSpec((tm, k), lambda i, j: (i, 0)), pl.BlockSpec((1, k), lambda i, j: (0, 0)),
                  pl.BlockSpec((k, tn), lambda i, j: (0, j))] + c_in_specs,
        out_specs=(pl.BlockSpec((tm, tn), lambda i, j: (i, j)), pl.BlockSpec((tm, k), lambda i, j: (i, 0)), *c_out_specs),
        scratch_shapes=c_sems,
        compiler_params=_params("arbitrary", "arbitrary") if comm else _params("parallel", "arbitrary"),
    )(h, gain, w, *c_ops)
    return outs[0], outs[1], outs[2:]


def _out_matmul(h, y, w, tm, tn):
    s_len, n = h.shape
    k = y.shape[1]

    def body(h_ref, y_ref, w_ref, o_ref):
        o_ref[...] = h_ref[...] + jnp.dot(y_ref[...], w_ref[...], preferred_element_type=F32)

    return pl.pallas_call(
        body, name="out_matmul",
        out_shape=jax.ShapeDtypeStruct((s_len, n), F32),
        grid=(s_len // tm, n // tn),
        in_specs=[pl.BlockSpec((tm, tn), lambda i, j: (i, j)), pl.BlockSpec((tm, k), lambda i, j: (i, 0)),
                  pl.BlockSpec((k, tn), lambda i, j: (0, j))],
        out_specs=pl.BlockSpec((tm, tn), lambda i, j: (i, j)),
        compiler_params=_params("parallel", "parallel"),
    )(h, y, w)


def _ple_forward(h1, gain, p, wg, wp, tm, tn):
    s_len, d = h1.shape
    kp = p.shape[1]

    def body(hrow_ref, hcol_ref, g_ref, p_ref, wg_ref, wp_ref, o_ref, n1_ref, gate_ref, pp_ref):
        @pl.when(pl.program_id(1) == 0)
        def _():
            n1_ref[...] = _rms(hrow_ref[...], g_ref[...]).astype(BF16)

        gate = jax.nn.sigmoid(jnp.dot(n1_ref[...], wg_ref[...], preferred_element_type=F32))
        pp = jnp.dot(p_ref[...].astype(BF16), wp_ref[...], preferred_element_type=F32)
        o_ref[...] = hcol_ref[...] + gate * pp
        gate_ref[...] = gate.astype(BF16)
        pp_ref[...] = pp.astype(BF16)

    col = pl.BlockSpec((tm, tn), lambda i, j: (i, j))
    return pl.pallas_call(
        body, name="ple_forward",
        out_shape=(jax.ShapeDtypeStruct((s_len, d), F32), jax.ShapeDtypeStruct((s_len, d), BF16),
                   jax.ShapeDtypeStruct((s_len, d), BF16), jax.ShapeDtypeStruct((s_len, d), BF16)),
        grid=(s_len // tm, d // tn),
        in_specs=[pl.BlockSpec((tm, d), lambda i, j: (i, 0)), col, pl.BlockSpec((1, d), lambda i, j: (0, 0)),
                  pl.BlockSpec((tm, kp), lambda i, j: (i, 0)), pl.BlockSpec((d, tn), lambda i, j: (0, j)),
                  pl.BlockSpec((kp, tn), lambda i, j: (0, j))],
        out_specs=(col, pl.BlockSpec((tm, d), lambda i, j: (i, 0)), col, col),
        compiler_params=_params("parallel", "arbitrary"),
    )(h1, h1, gain, p, wg, wp)


def _matmul_nt(a, b, tm, tk, name, comm=None):
    m, n = a.shape
    k = b.shape[0]
    c_ops, c_in_specs, c_shapes, c_out_specs, c_sems, c_begin, c_end = _riding_exchange(comm, 2)
    n_c = len(c_ops)

    def body(a_ref, b_ref, *rest):
        c_ins, o_ref, c_outs, sems = rest[:n_c], rest[n_c], rest[n_c + 1:2 * n_c + 1], rest[2 * n_c + 1:]
        c_begin(c_ins, c_outs, sems)
        o_ref[...] = lax.dot_general(a_ref[...].astype(BF16), b_ref[...].astype(BF16), _NT, preferred_element_type=F32)
        c_end(c_ins, c_outs, sems)

    outs = pl.pallas_call(
        body, name=name,
        out_shape=(jax.ShapeDtypeStruct((m, k), F32), *c_shapes),
        grid=(m // tm, k // tk),
        in_specs=[pl.BlockSpec((tm, n), lambda i, j: (i, 0)), pl.BlockSpec((tk, n), lambda i, j: (j, 0))] + c_in_specs,
        out_specs=(pl.BlockSpec((tm, tk), lambda i, j: (i, j)), *c_out_specs),
        scratch_shapes=c_sems,
        compiler_params=_params("arbitrary", "arbitrary") if comm else _params("parallel", "parallel"),
    )(a, b, *c_ops)
    return outs[0], outs[1:]


def _matmul_tn(a, b, tm, tk, tn, name, comm=None):
    m, k = a.shape
    n = b.shape[1]
    n_m = m // tm
    c_ops, c_in_specs, c_shapes, c_out_specs, c_sems, c_begin, c_end = _riding_exchange(comm, 3)
    n_c = len(c_ops)

    def body(a_ref, b_ref, *rest):
        c_ins, o_ref, c_outs, acc_ref, sems = rest[:n_c], rest[n_c], rest[n_c + 1:2 * n_c + 1], rest[2 * n_c + 1], rest[2 * n_c + 2:]
        c_begin(c_ins, c_outs, sems)
        part = lax.dot_general(a_ref[...].astype(BF16), b_ref[...].astype(BF16), _TN, preferred_element_type=F32)
        _acc(acc_ref, part, pl.program_id(2) == 0)

        @pl.when(pl.program_id(2) == n_m - 1)
        def _():
            o_ref[...] = acc_ref[...].astype(BF16)

        c_end(c_ins, c_outs, sems)

    outs = pl.pallas_call(
        body, name=name,
        out_shape=(jax.ShapeDtypeStruct((k, n), BF16), *c_shapes),
        grid=(k // tk, n // tn, n_m),
        in_specs=[pl.BlockSpec((tm, tk), lambda kk, nn, mm: (mm, kk)), pl.BlockSpec((tm, tn), lambda kk, nn, mm: (mm, nn))]
                 + c_in_specs,
        out_specs=(pl.BlockSpec((tk, tn), lambda kk, nn, mm: (kk, nn)), *c_out_specs),
        scratch_shapes=[pltpu.VMEM((tk, tn), F32)] + c_sems,
        compiler_params=_params("arbitrary", "arbitrary", "arbitrary") if comm else _params("parallel", "parallel", "arbitrary"),
    )(a, b, *c_ops)
    return outs[0], outs[1:]


def _acc(ref, val, first):
    @pl.when(first)
    def _():
        ref[...] = val

    @pl.when(jnp.logical_not(first))
    def _():
        ref[...] += val


def _loss_grad(h, target, tm):
    s_len, d = h.shape

    def body(h_ref, t_ref, dh_ref, loss_ref):
        e = h_ref[...] - t_ref[...]
        dh_ref[...] = e * (1.0 / d)
        part = jnp.sum(jnp.sum(e * e, axis=-1, keepdims=True), axis=0, keepdims=True) * (0.5 / d)
        _acc(loss_ref, jnp.broadcast_to(part, loss_ref.shape), pl.program_id(0) == 0)

    row = pl.BlockSpec((tm, d), lambda i: (i, 0))
    return pl.pallas_call(
        body, name="loss_grad",
        out_shape=(jax.ShapeDtypeStruct((s_len, d), F32), jax.ShapeDtypeStruct((1, 128), F32)),
        grid=(s_len // tm,),
        in_specs=[row, row],
        out_specs=(row, pl.BlockSpec((1, 128), lambda i: (0, 0))),
        compiler_params=_params("arbitrary"),
    )(h, target)


def _rms_backward(x, gain, dn, dres, tm, name):
    s_len, d = x.shape

    def body(x_ref, g_ref, dn_ref, dres_ref, dx_ref, dg_ref):
        _, vjp = jax.vjp(_rms, x_ref[...], g_ref[...])
        dx, dg = vjp(dn_ref[...])
        dx_ref[...] = dres_ref[...] + dx
        _acc(dg_ref, dg, pl.program_id(0) == 0)

    row = pl.BlockSpec((tm, d), lambda i: (i, 0))
    vec = pl.BlockSpec((1, d), lambda i: (0, 0))
    return pl.pallas_call(
        body, name=name,
        out_shape=(jax.ShapeDtypeStruct((s_len, d), F32), jax.ShapeDtypeStruct((1, d), F32)),
        grid=(s_len // tm,),
        in_specs=[row, vec, row, row],
        out_specs=(row, vec),
        compiler_params=_params("arbitrary"),
    )(x, gain, dn, dres)


def _ple_backward(dh2, gate, pp, wg, h1, gain, tm):
    s_len, d = dh2.shape

    def body(dh_ref, gate_ref, pp_ref, wg_ref, h1_ref, g_ref, dh1_ref, dgain_ref, dgp_ref, dpp_ref):
        dh = dh_ref[...]
        gate = gate_ref[...].astype(F32)
        dgp = (dh * pp_ref[...].astype(F32) * gate * (1.0 - gate)).astype(BF16)
        dgp_ref[...] = dgp
        dpp_ref[...] = (dh * gate).astype(BF16)
        dn = lax.dot_general(dgp, wg_ref[...], _NT, preferred_element_type=F32)
        _, vjp = jax.vjp(_rms, h1_ref[...], g_ref[...])
        dx, dgain = vjp(dn)
        dh1_ref[...] = dh + dx
        _acc(dgain_ref, dgain, pl.program_id(0) == 0)

    row = pl.BlockSpec((tm, d), lambda i: (i, 0))
    vec = pl.BlockSpec((1, d), lambda i: (0, 0))
    return pl.pallas_call(
        body, name="ple_backward",
        out_shape=(jax.ShapeDtypeStruct((s_len, d), F32), jax.ShapeDtypeStruct((1, d), F32),
                   jax.ShapeDtypeStruct((s_len, d), BF16), jax.ShapeDtypeStruct((s_len, d), BF16)),
        grid=(s_len // tm,),
        in_specs=[row, row, row, pl.BlockSpec((d, d), lambda i: (0, 0)), row, vec],
        out_specs=(row, vec, row, row),
        compiler_params=_params("arbitrary"),
    )(dh2, gate, pp, wg, h1, gain)


def _sgu_in_specs(tm):
    return [pl.BlockSpec((tm, W_A), lambda i: (i, O_A // W_A)),
            pl.BlockSpec((A_HEADS, HEAD), lambda i: (0, 0)), pl.BlockSpec((A_HEADS, CHUNK, CHUNK), lambda i: (0, 0, 0)),
            pl.BlockSpec((A_HEADS, CHUNK, 1), lambda i: (0, 0, 0)), pl.BlockSpec((1, 512), lambda i: (0, 0))]


def _sgu_load(a_ref, gain_ref, ws_ref, bs_ref, ga_ref, c):
    rows = slice(c * CHUNK, (c + 1) * CHUNK)
    heads = range(A_HEADS)
    u = tuple(a_ref[rows, h * HEAD:(h + 1) * HEAD].astype(F32) for h in heads)
    v = tuple(a_ref[rows, 512 + h * HEAD:512 + (h + 1) * HEAD].astype(F32) for h in heads)
    z = tuple(a_ref[rows, 1024 + h * HEAD:1024 + (h + 1) * HEAD].astype(F32) for h in heads)
    gain = tuple(gain_ref[h:h + 1, :] for h in heads)
    ws = tuple(ws_ref[h] for h in heads)
    bs = tuple(bs_ref[h] for h in heads)
    ga = tuple(ga_ref[:, h * HEAD:(h + 1) * HEAD] for h in heads)
    return u, v, z, gain, ws, bs, ga


def _sgu_forward(proj, gain, ws, bs, ga, tm):
    s_len = proj.shape[0]

    def body(a_ref, gain_ref, ws_ref, bs_ref, ga_ref, o_ref):
        for c in range(tm // CHUNK):
            out = _sgu_chunk(*_sgu_load(a_ref, gain_ref, ws_ref, bs_ref, ga_ref, c))
            for h in range(A_HEADS):
                o_ref[c * CHUNK:(c + 1) * CHUNK, h * HEAD:(h + 1) * HEAD] = out[h].astype(BF16)

    return pl.pallas_call(
        body, name="sgu_forward",
        out_shape=jax.ShapeDtypeStruct((s_len, D_MODEL), BF16),
        grid=(s_len // tm,),
        in_specs=_sgu_in_specs(tm),
        out_specs=pl.BlockSpec((tm, 512), lambda i: (i, 0)),
        compiler_params=_params("parallel"),
    )(proj, gain, ws, bs, ga)


def _sgu_backward(proj, gain, ws, bs, ga, dy, dproj, tm):
    s_len = proj.shape[0]

    def body(a_ref, gain_ref, ws_ref, bs_ref, ga_ref, dy_ref, _, da_ref, dgain_ref, dws_ref, dbs_ref, dga_ref):
        tot = None
        for c in range(tm // CHUNK):
            args = _sgu_load(a_ref, gain_ref, ws_ref, bs_ref, ga_ref, c)
            _, vjp = jax.vjp(_sgu_chunk, *args)
            rows = slice(c * CHUNK, (c + 1) * CHUNK)
            du, dv, dz, dgain, dws, dbs, dga = vjp(tuple(dy_ref[rows, h * HEAD:(h + 1) * HEAD] for h in range(A_HEADS)))
            for h in range(A_HEADS):
                da_ref[rows, h * HEAD:(h + 1) * HEAD] = du[h].astype(BF16)
                da_ref[rows, 512 + h * HEAD:512 + (h + 1) * HEAD] = dv[h].astype(BF16)
                da_ref[rows, 1024 + h * HEAD:1024 + (h + 1) * HEAD] = dz[h].astype(BF16)
            part = (dgain, dws, dbs, dga)
            tot = part if tot is None else jax.tree.map(jnp.add, tot, part)
        dgain, dws, dbs, dga = tot
        first = pl.program_id(0) == 0
        _acc(dgain_ref, jnp.concatenate(dgain, axis=0), first)
        _acc(dga_ref, jnp.concatenate(dga, axis=-1), first)
        for h in range(A_HEADS):
            _acc(dws_ref.at[h], dws[h], first)
            _acc(dbs_ref.at[h], dbs[h], first)

    small = [pl.BlockSpec((A_HEADS, HEAD), lambda i: (0, 0)), pl.BlockSpec((A_HEADS, CHUNK, CHUNK), lambda i: (0, 0, 0)),
             pl.BlockSpec((A_HEADS, CHUNK, 1), lambda i: (0, 0, 0)), pl.BlockSpec((1, 512), lambda i: (0, 0))]
    return pl.pallas_call(
        body, name="sgu_backward",
        out_shape=(jax.ShapeDtypeStruct(dproj.shape, BF16),
                   jax.ShapeDtypeStruct((A_HEADS, HEAD), F32), jax.ShapeDtypeStruct((A_HEADS, CHUNK, CHUNK), F32),
                   jax.ShapeDtypeStruct((A_HEADS, CHUNK, 1), F32), jax.ShapeDtypeStruct((1, 512), F32)),
        grid=(s_len // tm,),
        in_specs=_sgu_in_specs(tm) + [pl.BlockSpec((tm, 512), lambda i: (i, 0)), pl.BlockSpec(memory_space=pl.ANY)],
        out_specs=(pl.BlockSpec((tm, W_A), lambda i: (i, O_A // W_A)), *small),
        input_output_aliases={6: 0},
        compiler_params=_params("arbitrary"),
    )(proj, gain, ws, bs, ga, dy, dproj)


def _halo_specs(tm, width, col, n_rows):
    per = tm // HALO
    last = n_rows // HALO - 1
    return [pl.BlockSpec((HALO, width), lambda i: (jnp.maximum(i * per - 1, 0), col)),
            pl.BlockSpec((tm, width), lambda i: (i, col)),
            pl.BlockSpec((HALO, width), lambda i: (jnp.minimum((i + 1) * per, last), col))]


def _conv_masks(tm, s_len):
    r = lax.broadcasted_iota(jnp.int32, (tm + 2 * HALO, 1), 0)
    g = pl.program_id(0) * tm - HALO + r
    return (g >= 0) & (g < s_len), (r >= HALO) & (r < HALO + tm)


def _conv_inputs(b_refs, cw_ref, cb_ref, gb_ref):
    ext = jnp.concatenate([r[...] for r in b_refs], axis=0).astype(F32)
    bb, bc, bh, bz = (ext[:, j * 512:(j + 1) * 512] for j in range(4))
    prm = (cw_ref[0:1, :], cw_ref[1:2, :], cw_ref[2:3, :], cb_ref[...], gb_ref[...])
    return (bb, bc, bh, bz), prm


def _conv_forward(proj, cw, cb, gb, y, tm):
    s_len = proj.shape[0]

    def body(p0, p1, p2, cw_ref, cb_ref, gb_ref, _, o_ref):
        acts, prm = _conv_inputs((p0, p1, p2), cw_ref, cb_ref, gb_ref)
        valid, core = _conv_masks(tm, s_len)
        out = _conv_tile(*acts, *prm, *prm, valid, core)
        o_ref[...] = out[HALO:HALO + tm].astype(BF16)

    vec = pl.BlockSpec((1, 512), lambda i: (0, 0))
    return pl.pallas_call(
        body, name="conv_forward",
        out_shape=jax.ShapeDtypeStruct(y.shape, BF16),
        grid=(s_len // tm,),
        in_specs=_halo_specs(tm, W_B, O_B // W_B, s_len) + [pl.BlockSpec((3, 512), lambda i: (0, 0)), vec, vec,
                                                             pl.BlockSpec(memory_space=pl.ANY)],
        out_specs=pl.BlockSpec((tm, 512), lambda i: (i, 1)),
        input_output_aliases={6: 0},
        compiler_params=_params("parallel"),
    )(proj, proj, proj, cw, cb, gb, y)


def _conv_backward(proj, cw, cb, gb, dy, tm):
    s_len = proj.shape[0]

    def body(p0, p1, p2, cw_ref, cb_ref, gb_ref, d0, d1, d2, db_ref, dcw_ref, dcb_ref, dgb_ref):
        acts, prm = _conv_inputs((p0, p1, p2), cw_ref, cb_ref, gb_ref)
        valid, core = _conv_masks(tm, s_len)
        _, vjp = jax.vjp(lambda a, p: _conv_tile(*a, *p, *prm, valid, core), acts, prm)
        dy_ext = jnp.where(valid, jnp.concatenate([d0[...], d1[...], d2[...]], axis=0), 0.0)
        dacts, dprm = vjp(dy_ext)
        for j in range(4):
            db_ref[:, j * 512:(j + 1) * 512] = dacts[j][HALO:HALO + tm].astype(BF16)
        first = pl.program_id(0) == 0
        _acc(dcw_ref, jnp.concatenate(dprm[0:3], axis=0), first)
        _acc(dcb_ref, dprm[3], first)
        _acc(dgb_ref, dprm[4], first)

    vec = pl.BlockSpec((1, 512), lambda i: (0, 0))
    mat = pl.BlockSpec((3, 512), lambda i: (0, 0))
    return pl.pallas_call(
        body, name="conv_backward",
        out_shape=(jax.ShapeDtypeStruct((s_len, PROJ_W), BF16), jax.ShapeDtypeStruct((3, 512), F32),
                   jax.ShapeDtypeStruct((1, 512), F32), jax.ShapeDtypeStruct((1, 512), F32)),
        grid=(s_len // tm,),
        in_specs=_halo_specs(tm, W_B, O_B // W_B, s_len) + [mat, vec, vec] + _halo_specs(tm, 512, 1, s_len),
        out_specs=(pl.BlockSpec((tm, W_B), lambda i: (i, O_B // W_B)), mat, vec, vec),
        compiler_params=_params("arbitrary"),
    )(proj, proj, proj, cw, cb, gb, dy, dy, dy)


def _cgate_forward(o, proj, gc, y, tm):
    s_len = o.shape[0]

    def body(o_ref, cz_ref, gc_ref, _, y_ref):
        y_ref[...] = _cgate_tile(o_ref[...], cz_ref[...].astype(F32), gc_ref[...]).astype(BF16)

    return pl.pallas_call(
        body, name="cgate_forward",
        out_shape=jax.ShapeDtypeStruct(y.shape, BF16),
        grid=(s_len // tm,),
        in_specs=[pl.BlockSpec((tm, W_CZ), lambda i: (i, 0)), pl.BlockSpec((tm, W_CZ), lambda i: (i, O_CZ // W_CZ)),
                  pl.BlockSpec((1, W_CZ), lambda i: (0, 0)), pl.BlockSpec(memory_space=pl.ANY)],
        out_specs=pl.BlockSpec((tm, W_CZ), lambda i: (i, 1)),
        input_output_aliases={3: 0},
        compiler_params=_params("parallel"),
    )(o, proj, gc, y)


def _cgate_backward(o, proj, gc, dy, dproj, tm, stat_chunk):
    s_len = o.shape[0]
    per_stat = stat_chunk // tm

    def body(o_ref, cz_ref, gc_ref, dy_ref, _, dcz_ref, do_ref, dsum_ref, dgc_ref):
        o = o_ref[...]
        _, vjp = jax.vjp(_cgate_tile, o, cz_ref[...].astype(F32), gc_ref[...])
        do, dcz, dgc = vjp(dy_ref[...])
        dcz_ref[...] = dcz.astype(BF16)
        do_ref[...] = do.astype(BF16)
        ones = jnp.ones((8, HEAD), F32)
        for h in range(C_HEADS):
            cols = slice(h * HEAD, (h + 1) * HEAD)
            sums = lax.dot_general(ones, do[:, cols] * o[:, cols], _NT, precision=lax.Precision.HIGHEST,
                                   preferred_element_type=F32)
            dsum_ref[h, 0] = sums[0:1]
        _acc(dgc_ref, dgc, pl.program_id(0) == 0)

    row = pl.BlockSpec((tm, W_CZ), lambda i: (i, 0))
    vec = pl.BlockSpec((1, W_CZ), lambda i: (0, 0))
    return pl.pallas_call(
        body, name="cgate_backward",
        out_shape=(jax.ShapeDtypeStruct(dproj.shape, BF16), jax.ShapeDtypeStruct((s_len, W_CZ), BF16),
                   jax.ShapeDtypeStruct((C_HEADS, s_len // stat_chunk, 1, stat_chunk), F32), jax.ShapeDtypeStruct((1, W_CZ), F32)),
        grid=(s_len // tm,),
        in_specs=[row, pl.BlockSpec((tm, W_CZ), lambda i: (i, O_CZ // W_CZ)), vec,
                  pl.BlockSpec((tm, W_CZ), lambda i: (i, 1)), pl.BlockSpec(memory_space=pl.ANY)],
        out_specs=(pl.BlockSpec((tm, W_CZ), lambda i: (i, O_CZ // W_CZ)), row,
                   pl.BlockSpec((C_HEADS, 1, 1, tm), lambda i: (0, i // per_stat, 0, i % per_stat)), vec),
        input_output_aliases={4: 0},
        compiler_params=_params("arbitrary"),
    )(o, proj, gc, dy, dproj)


def _mla_small_specs():
    return [pl.BlockSpec((KV_RANK, 2 * C_HEADS * HEAD), lambda i: (0, 0)), pl.BlockSpec((1, KV_RANK), lambda i: (0, 0)),
            pl.BlockSpec((1, HEAD), lambda i: (0, 0)), pl.BlockSpec((1, ROPE), lambda i: (0, 0)),
            pl.BlockSpec((1, HEAD), lambda i: (0, 0)), pl.BlockSpec((1, ROPE), lambda i: (0, 0))]


def _mla_load(m_ref, cos_ref, sin_ref):
    qn = m_ref[:, M_QN:M_QN + C_HEADS * HEAD].astype(F32)
    qr = m_ref[:, M_QR:M_QR + C_HEADS * ROPE].astype(F32)
    ckv = m_ref[:, M_CKV:M_CKV + KV_RANK].astype(F32)
    kr = m_ref[:, M_KR:M_KR + ROPE].astype(F32)
    return qn, qr, ckv, kr, cos_ref[...], sin_ref[...]


def _mla_forward(proj, cosf, sins, wukv, kvg, qng, qrg, kng, krg, tm, kt_chunk, vt_chunk):
    s_len = proj.shape[0]

    def body(m_ref, cos_ref, sin_ref, w_ref, kvg_ref, qng_ref, qrg_ref, kng_ref, krg_ref, q_ref, k_ref, v_ref, kt_ref, vt_ref):
        q, k, v = _mla_tile(*_mla_load(m_ref, cos_ref, sin_ref), w_ref[...], kvg_ref[...], qng_ref[...], qrg_ref[...],
                            kng_ref[...], krg_ref[...])
        for h in range(C_HEADS):
            q_ref[h] = q[h].astype(BF16)
            k_ref[h] = k[h].astype(BF16)
            v_ref[h] = v[h].astype(BF16)
            kt_ref[h, 0] = jnp.concatenate([k[h][:, :HEAD].T, k[h][:, HEAD:].T], axis=0).astype(BF16)
            vt_ref[h, 0] = v[h].T.astype(BF16)

    rope_spec = pl.BlockSpec((tm, ROPE), lambda i: (i, 0))
    qk_spec = pl.BlockSpec((C_HEADS, tm, QK), lambda i: (0, i, 0))
    per_k, per_v = kt_chunk // tm, vt_chunk // tm
    return pl.pallas_call(
        body, name="mla_forward",
        out_shape=(jax.ShapeDtypeStruct((C_HEADS, s_len, QK), BF16), jax.ShapeDtypeStruct((C_HEADS, s_len, QK), BF16),
                   jax.ShapeDtypeStruct((C_HEADS, s_len, HEAD), BF16),
                   jax.ShapeDtypeStruct((C_HEADS, s_len // kt_chunk, QK, kt_chunk), BF16),
                   jax.ShapeDtypeStruct((C_HEADS, s_len // vt_chunk, HEAD, vt_chunk), BF16)),
        grid=(s_len // tm,),
        in_specs=[pl.BlockSpec((tm, W_M), lambda i: (i, O_M // W_M)), rope_spec, rope_spec] + _mla_small_specs(),
        out_specs=(qk_spec, qk_spec, pl.BlockSpec((C_HEADS, tm, HEAD), lambda i: (0, i, 0)),
                   pl.BlockSpec((C_HEADS, 1, QK, tm), lambda i: (0, i // per_k, 0, i % per_k)),
                   pl.BlockSpec((C_HEADS, 1, HEAD, tm), lambda i: (0, i // per_v, 0, i % per_v))),
        compiler_params=_params("parallel"),
    )(proj, cosf, sins, wukv, kvg, qng, qrg, kng, krg)


def _mla_backward(proj, cosf, sins, wukv, kvg, qng, qrg, kng, krg, dqt, dk, dv, dproj, tm):
    s_len = proj.shape[0]
    per_chunk = dqt.shape[3] // tm

    def body(m_ref, cos_ref, sin_ref, w_ref, kvg_ref, qng_ref, qrg_ref, kng_ref, krg_ref, dq_ref, dk_ref, dv_ref, _,
             dm_ref, dw_ref, dkvg_ref, dqng_ref, dqrg_ref, dkng_ref, dkrg_ref):
        qn, qr, ckv, kr, cosf_t, sins_t = _mla_load(m_ref, cos_ref, sin_ref)
        prm = (w_ref[...], kvg_ref[...], qng_ref[...], qrg_ref[...], kng_ref[...], krg_ref[...])
        _, vjp = jax.vjp(lambda a, p: _mla_tile(*a, cosf_t, sins_t, *p), (qn, qr, ckv, kr), prm)
        heads = range(C_HEADS)
        dacts, dprm = vjp((tuple(dq_ref[h, 0].T for h in heads), tuple(dk_ref[h] for h in heads), tuple(dv_ref[h] for h in heads)))
        dm_ref[:, M_QN:M_QN + C_HEADS * HEAD] = dacts[0].astype(BF16)
        dm_ref[:, M_QR:M_QR + C_HEADS * ROPE] = dacts[1].astype(BF16)
        dm_ref[:, M_CKV:M_CKV + KV_RANK] = dacts[2].astype(BF16)
        pad = jnp.zeros((tm, W_M - M_KR - ROPE), F32)
        dm_ref[:, M_KR:W_M] = jnp.concatenate([dacts[3], pad], axis=-1).astype(BF16)
        first = pl.program_id(0) == 0
        for ref, val in zip((dw_ref, dkvg_ref, dqng_ref, dqrg_ref, dkng_ref, dkrg_ref), dprm):
            _acc(ref, val.astype(F32), first)

    rope_spec = pl.BlockSpec((tm, ROPE), lambda i: (i, 0))
    qk_spec = pl.BlockSpec((C_HEADS, tm, QK), lambda i: (0, i, 0))
    small = _mla_small_specs()
    return pl.pallas_call(
        body, name="mla_backward",
        out_shape=(jax.ShapeDtypeStruct(dproj.shape, BF16), jax.ShapeDtypeStruct((KV_RANK, 2 * C_HEADS * HEAD), F32),
                   jax.ShapeDtypeStruct((1, KV_RANK), F32), jax.ShapeDtypeStruct((1, HEAD), F32),
                   jax.ShapeDtypeStruct((1, ROPE), F32), jax.ShapeDtypeStruct((1, HEAD), F32),
                   jax.ShapeDtypeStruct((1, ROPE), F32)),
        grid=(s_len // tm,),
        in_specs=[pl.BlockSpec((tm, W_M), lambda i: (i, O_M // W_M)), rope_spec, rope_spec] + small
                 + [pl.BlockSpec((C_HEADS, 1, QK, tm), lambda i: (0, i // per_chunk, 0, i % per_chunk)), qk_spec,
                    pl.BlockSpec((C_HEADS, tm, HEAD), lambda i: (0, i, 0)), pl.BlockSpec(memory_space=pl.ANY)],
        out_specs=(pl.BlockSpec((tm, W_M), lambda i: (i, O_M // W_M)), *small),
        input_output_aliases={12: 0},
        compiler_params=_params("arbitrary"),
    )(proj, cosf, sins, wukv, kvg, qng, qrg, kng, krg, dqt, dk, dv, dproj)


def _attention_forward(q, k, vt, tq, stat_chunk, comm=None):
    n_heads, s_len, _ = q.shape
    n_chunks, _, ck = vt.shape[1:]
    c_ops, c_in_specs, c_shapes, c_out_specs, c_sems, c_begin, c_end = _riding_exchange(comm, 2)
    n_c = len(c_ops)

    def body(q_ref, k_ref, vt_ref, *rest):
        c_ins, (o_ref, lse_ref), c_outs, sems = rest[:n_c], rest[n_c:n_c + 2], rest[n_c + 2:2 * n_c + 2], rest[2 * n_c + 2:]
        c_begin(c_ins, c_outs, sems)
        q_t = q_ref[0]

        def step(j, carry):
            m_old, l_old, acc = carry
            k_j = k_ref[0, pl.ds(pl.multiple_of(j * ck, ck), ck), :]
            s = lax.dot_general(k_j, q_t, _NT, preferred_element_type=F32)
            m_new = jnp.maximum(m_old, jnp.max(s, axis=0, keepdims=True))
            p = jnp.exp2(s - m_new)
            alpha = jnp.exp2(m_old - m_new)
            l_new = alpha * l_old + jnp.sum(p, axis=0, keepdims=True)
            acc = alpha * acc + jnp.dot(vt_ref[0, j], p.astype(BF16), preferred_element_type=F32)
            return m_new, l_new, acc

        init = (jnp.full((1, tq), -jnp.inf, F32), jnp.zeros((1, tq), F32), jnp.zeros((HEAD, tq), F32))
        m_fin, l_fin, acc = lax.fori_loop(0, n_chunks, step, init)
        o_ref[...] = (acc / l_fin).T
        lse_ref[0, 0] = m_fin + jnp.log2(l_fin)
        c_end(c_ins, c_outs, sems)

    per_stat = stat_chunk // tq
    outs = pl.pallas_call(
        body, name="attention_forward",
        out_shape=(jax.ShapeDtypeStruct((s_len, n_heads * HEAD), F32),
                   jax.ShapeDtypeStruct((n_heads, s_len // stat_chunk, 1, stat_chunk), F32), *c_shapes),
        grid=(n_heads, s_len // tq),
        in_specs=[pl.BlockSpec((1, tq, QK), lambda h, i: (h, i, 0)), pl.BlockSpec((1, s_len, QK), lambda h, i: (h, 0, 0)),
                  pl.BlockSpec((1, n_chunks, HEAD, ck), lambda h, i: (h, 0, 0, 0))] + c_in_specs,
        out_specs=(pl.BlockSpec((tq, HEAD), lambda h, i: (i, h)),
                   pl.BlockSpec((1, 1, 1, tq), lambda h, i: (h, i // per_stat, 0, i % per_stat)), *c_out_specs),
        scratch_shapes=c_sems,
        compiler_params=_params("arbitrary", "arbitrary") if comm else _params("parallel", "parallel"),
    )(q, k, vt, *c_ops)
    return outs[0], outs[1], outs[2:]


def _attention_backward(q, k, kt, v, do, lse, dsum, comm=None):
    n_heads, s_len, _ = q.shape
    tk = kt.shape[3]
    n_q, _, cq = lse.shape[1:]
    c_ops, c_in_specs, c_shapes, c_out_specs, c_sems, c_begin, c_end = _riding_exchange(comm, 2)
    n_c = len(c_ops)

    def body(q_ref, k_ref, kt_ref, v_ref, do_ref, lse_ref, dsum_ref, *rest):
        c_ins, (dqt_ref, dk_ref, dv_ref), c_outs, sems = rest[:n_c], rest[n_c:n_c + 3], rest[n_c + 3:2 * n_c + 3], rest[2 * n_c + 3:]
        c_begin(c_ins, c_outs, sems)
        first = pl.program_id(1) == 0
        k_j, kt_j, v_j = k_ref[0], kt_ref[0, 0], v_ref[0]

        def step(i, carry):
            dk, dv = carry
            rows = pl.ds(pl.multiple_of(i * cq, cq), cq)
            q_i, do_i = q_ref[0, rows, :], do_ref[rows, :]
            s = lax.dot_general(k_j, q_i, _NT, preferred_element_type=F32)
            p = jnp.exp2(s - lse_ref[0, i])
            dp = lax.dot_general(v_j, do_i, _NT, preferred_element_type=F32)
            ds = (p * (dp - dsum_ref[0, i]) * LN_2).astype(BF16)
            dv = dv + jnp.dot(p.astype(BF16), do_i, preferred_element_type=F32)
            dk = dk + jnp.dot(ds, q_i, preferred_element_type=F32)
            _acc(dqt_ref.at[0, i], jnp.dot(kt_j, ds, preferred_element_type=F32), first)
            return dk, dv

        dk, dv = lax.fori_loop(0, n_q, step, (jnp.zeros((tk, QK), F32), jnp.zeros((tk, HEAD), F32)))
        dk_ref[0] = dk
        dv_ref[0] = dv
        c_end(c_ins, c_outs, sems)

    stat = pl.BlockSpec((1, n_q, 1, cq), lambda h, j: (h, 0, 0, 0))
    outs = pl.pallas_call(
        body, name="attention_backward",
        out_shape=(jax.ShapeDtypeStruct((n_heads, n_q, QK, cq), F32), jax.ShapeDtypeStruct((n_heads, s_len, QK), F32),
                   jax.ShapeDtypeStruct((n_heads, s_len, HEAD), F32), *c_shapes),
        grid=(n_heads, s_len // tk),
        in_specs=[pl.BlockSpec((1, s_len, QK), lambda h, j: (h, 0, 0)), pl.BlockSpec((1, tk, QK), lambda h, j: (h, j, 0)),
                  pl.BlockSpec((1, 1, QK, tk), lambda h, j: (h, j, 0, 0)),
                  pl.BlockSpec((1, tk, HEAD), lambda h, j: (h, j, 0)), pl.BlockSpec((s_len, HEAD), lambda h, j: (0, h)),
                  stat, stat] + c_in_specs,
        out_specs=(pl.BlockSpec((1, n_q, QK, cq), lambda h, j: (h, 0, 0, 0)), pl.BlockSpec((1, tk, QK), lambda h, j: (h, j, 0)),
                   pl.BlockSpec((1, tk, HEAD), lambda h, j: (h, j, 0)), *c_out_specs),
        scratch_shapes=c_sems,
        compiler_params=_params("arbitrary", "arbitrary") if comm else _params("parallel", "arbitrary"),
    )(q, k, kt, v, do, lse, dsum, *c_ops)
    return outs[0], outs[1], outs[2], outs[3:]


def _exchange(arrs, gather, name):
    n = len(arrs)

    def body(*refs):
        plan = _exchange_plan(refs[:n], refs[n:2 * n], gather, *refs[2 * n:])
        _exchange_start(plan)
        _exchange_wait(plan)

    any_spec = pl.BlockSpec(memory_space=pl.ANY)
    return pl.pallas_call(
        body, name=name,
        out_shape=_exchange_out_shapes(arrs, gather),
        in_specs=[any_spec] * n,
        out_specs=tuple([any_spec] * n),
        scratch_shapes=_exchange_semaphores(n),
        compiler_params=pltpu.CompilerParams(has_side_effects=True),
    )(*arrs)


def _exchange_out_shapes(arrs, gather):
    return tuple(jax.ShapeDtypeStruct((N_DEV, *(a.shape if g else a.shape[1:])), a.dtype) for a, g in zip(arrs, gather))


def _exchange_semaphores(n):
    n_remote = n * (N_DEV - 1)
    return [pltpu.SemaphoreType.DMA((n_remote,)), pltpu.SemaphoreType.DMA((n_remote,)), pltpu.SemaphoreType.DMA((n,))]


def _exchange_plan(ins, outs, gather, send_sems, recv_sems, local_sems):
    n = len(ins)
    x, y, c = lax.axis_index("x"), lax.axis_index("y"), lax.axis_index("c")
    me = 4 * x + 2 * y + c

    def block_for(a, dev):
        return ins[a] if gather[a] else ins[a].at[dev]

    local = [pltpu.make_async_copy(block_for(a, me), outs[a].at[me], local_sems.at[a]) for a in range(n)]
    remote = []
    for k in range(1, N_DEV):
        px = 1 - x if k & 4 else x
        py = 1 - y if k & 2 else y
        pc = 1 - c if k & 1 else c
        peer = 4 * px + 2 * py + pc
        for a in range(n):
            idx = a * (N_DEV - 1) + k - 1
            send = pltpu.make_async_remote_copy(
                src_ref=block_for(a, peer), dst_ref=outs[a].at[me], send_sem=send_sems.at[idx], recv_sem=recv_sems.at[idx],
                device_id=(px, py, pc), device_id_type=pl.DeviceIdType.MESH)
            arrive = pltpu.make_async_remote_copy(
                src_ref=block_for(a, peer), dst_ref=outs[a].at[peer], send_sem=send_sems.at[idx], recv_sem=recv_sems.at[idx],
                device_id=(px, py, pc), device_id_type=pl.DeviceIdType.MESH)
            remote.append((send, arrive))
    return local, remote


def _exchange_start(plan):
    local, remote = plan
    for cp in local:
        cp.start()
    for send, _ in remote:
        send.start()


def _exchange_wait(plan):
    local, remote = plan
    for send, arrive in remote:
        send.wait_send()
        arrive.wait_recv()
    for cp in local:
        cp.wait()


def _riding_exchange(comm, n_grid):
    if comm is None:
        return [], [], (), (), [], lambda *_: None, lambda *_: None
    arrs, gather = comm
    n = len(arrs)
    any_spec = pl.BlockSpec(memory_space=pl.ANY)

    def begin(ins, outs, sems):
        @pl.when(functools.reduce(jnp.logical_and, [pl.program_id(d) == 0 for d in range(n_grid)]))
        def _():
            _exchange_start(_exchange_plan(ins, outs, gather, *sems))

    def end(ins, outs, sems):
        @pl.when(functools.reduce(jnp.logical_and, [pl.program_id(d) == pl.num_programs(d) - 1 for d in range(n_grid)]))
        def _():
            _exchange_wait(_exchange_plan(ins, outs, gather, *sems))

    return (list(arrs), [any_spec] * n, _exchange_out_shapes(arrs, gather), tuple([any_spec] * n), _exchange_semaphores(n),
            begin, end)


ADAM_TILE_ELEMS = 256 * 1024


def _sum_adam(parts, w, m, v, layer, prev, name):
    n_parts, r, c = parts.shape
    tm, tc = r, c
    if r % 16 == 0:
        while tm * c > ADAM_TILE_ELEMS and tm % 16 == 0:
            tm //= 2
    else:
        while r * tc > ADAM_TILE_ELEMS and tc % 256 == 0:
            tc //= 2

    def body(p_ref, w_ref, m_ref, v_ref, *rest):
        g_ref, d_ref, nm_ref, nv_ref = rest[-4:]
        g = p_ref[0].astype(F32)
        for s in range(1, n_parts):
            g = g + p_ref[s].astype(F32)
        m_new = ADAM_B1 * m_ref[...] + (1.0 - ADAM_B1) * g
        v_new = ADAM_B2 * v_ref[...] + (1.0 - ADAM_B2) * (g * g)
        m_hat = m_new / (1.0 - ADAM_B1 ** ADAM_STEP)
        v_hat = v_new / (1.0 - ADAM_B2 ** ADAM_STEP)
        g_ref[...] = g
        d_ref[...] = -ADAM_LR * (m_hat / (jnp.sqrt(v_hat) + ADAM_EPS) + ADAM_WD * w_ref[...])
        nm_ref[...] = m_new
        nv_ref[...] = v_new

    slab = pl.BlockSpec((None, tm, tc), lambda i, j: (layer, i, j))
    n_prev = 0 if prev is None else 4
    return pl.pallas_call(
        body, name=name,
        out_shape=(jax.ShapeDtypeStruct(w.shape, F32),) * 4,
        grid=(r // tm, c // tc),
        in_specs=[pl.BlockSpec((n_parts, tm, tc), lambda i, j: (0, i, j)), slab, slab, slab]
                 + [pl.BlockSpec(memory_space=pl.ANY)] * n_prev,
        out_specs=(slab, slab, slab, slab),
        input_output_aliases={4 + j: j for j in range(n_prev)},
        compiler_params=_params("parallel", "parallel"),
    )(parts, w, m, v, *(prev or ()))


def _permute_in(w):
    k = w.shape[0]
    q = w[:, 3584:5120].reshape(k, C_HEADS, QK)
    return jnp.concatenate(
        [w[:, 1536:3584], w[:, 5696:6720], w[:, 0:1536], q[:, :, :HEAD].reshape(k, C_HEADS * HEAD),
         q[:, :, HEAD:].reshape(k, C_HEADS * ROPE), w[:, 5120:5632], w[:, 5632:5696],
         jnp.zeros((k, PROJ_W - IN_WIDTH), w.dtype)], axis=1)


def _unpermute_in(g):
    k = g.shape[0]
    qn = g[:, O_M + M_QN:O_M + M_QR].reshape(k, C_HEADS, HEAD)
    qr = g[:, O_M + M_QR:O_M + M_CKV].reshape(k, C_HEADS, ROPE)
    q = jnp.concatenate([qn, qr], axis=-1).reshape(k, C_HEADS * QK)
    return jnp.concatenate(
        [g[:, O_A:O_A + W_A], g[:, O_B:O_B + W_B], q, g[:, O_M + M_CKV:O_M + M_KR],
         g[:, O_M + M_KR:O_M + M_KR + ROPE], g[:, O_CZ:O_CZ + W_CZ]], axis=1)


SMALL = ("attn_norm", "sgu_norm", "w_spatial", "b_spatial", "conv_b", "kv_norm", "q_nope_norm", "q_rope_norm",
         "k_nope_norm", "k_rope_norm", "out_norm", "ple_norm")
PACK_ROWS = 256


def _pack(tensors):
    flat = jnp.concatenate([t.reshape(-1) for t in tensors])
    rows = -(-flat.shape[0] // (128 * PACK_ROWS)) * PACK_ROWS
    return jnp.pad(flat, (0, rows * 128 - flat.shape[0])).reshape(rows, 128)


def _unpack(packed, like):
    flat = packed.reshape(-1)
    out, pos = [], 0
    for t in like:
        out.append(flat[pos:pos + t.size].reshape(t.shape))
        pos += t.size
    return out


def _tile(s_len, want):
    return min(want, s_len)


ATT_FWD_QUERIES = 512
ATT_FWD_KEYS = 8192
ATT_BWD_KEYS = 512
ATT_BWD_QUERIES = 2048


def _layer_forward(h, p_l, cosf, sins, w, sm, comm, comm_rest):
    s_len = h.shape[0]
    tm = _tile(s_len, 512)
    proj, hn, rest = _norm_matmul(h, sm["attn_norm"], w["w_in"], _tile(s_len, 1024), 768, comm_rest)
    if comm_rest is not None:
        w = {**w, **_assemble_rest(rest)}
    ga, gb, gc = sm["out_norm"][:, 0:512], sm["out_norm"][:, 512:1024], sm["out_norm"][:, 1024:2048]
    y = _sgu_forward(proj, sm["sgu_norm"], sm["w_spatial"], sm["b_spatial"], ga, _tile(s_len, 256))
    y = _conv_forward(proj, w["conv_w"], sm["conv_b"], gb, y, _tile(s_len, 256))
    q, k, v, kt, vt = _mla_forward(proj, cosf, sins, w["w_ukv"], sm["kv_norm"], sm["q_nope_norm"], sm["q_rope_norm"],
                                   sm["k_nope_norm"], sm["k_rope_norm"], _tile(s_len, 256), _tile(s_len, ATT_BWD_KEYS),
                                   _tile(s_len, ATT_FWD_KEYS))
    o, lse, arrived = _attention_forward(q, k, vt, _tile(s_len, ATT_FWD_QUERIES), _tile(s_len, ATT_BWD_QUERIES), comm)
    y = _cgate_forward(o, proj, gc, y, _tile(s_len, 256))
    h1 = _out_matmul(h, y, w["w_out"], _tile(s_len, 1024), 1024)
    h2, n1, gate, pp = _ple_forward(h1, sm["ple_norm"], p_l, w["w_ple_gate"], w["w_ple_proj"], tm, 1024)
    saved = dict(h=h, hn=hn, proj=proj, y=y, q=q, k=k, v=v, kt=kt, o=o, lse=lse, h1=h1, n1=n1, gate=gate, pp=pp)
    return h2, saved, w, arrived


def _layer_backward(dh2, p_l, cosf, sins, w, sm, sv, comm, scatter_own):
    s_len = dh2.shape[0]
    tm = _tile(s_len, 512)
    tr = _tile(s_len, 256)
    big, small = {}, {}
    dh1, small["ple_norm"], dgp, dpp = _ple_backward(dh2, sv["gate"], sv["pp"], w["w_ple_gate"], sv["h1"], sm["ple_norm"], tm)
    big["w_ple_proj"], _ = _matmul_tn(p_l, dpp, _tile(s_len, 2048), PLE_DIM, 1024, "grad_w_ple_proj")
    big["w_ple_gate"], _ = _matmul_tn(sv["n1"], dgp, _tile(s_len, 2048), 1024, 1024, "grad_w_ple_gate")
    dy, _ = _matmul_nt(dh1, w["w_out"], _tile(s_len, 1024), 1024, "grad_branches")
    big["w_out"], _ = _matmul_tn(sv["y"], dh1, _tile(s_len, 2048), 1024, 1024, "grad_w_out")
    ga, gb, gc = sm["out_norm"][:, 0:512], sm["out_norm"][:, 512:1024], sm["out_norm"][:, 1024:2048]
    dproj, dcw, small["conv_b"], dgb = _conv_backward(sv["proj"], w["conv_w"], sm["conv_b"], gb, dy, tr)
    big["conv_w"] = dcw
    dproj, do, dsum, dgc = _cgate_backward(sv["o"], sv["proj"], gc, dy, dproj, tm, _tile(s_len, ATT_BWD_QUERIES))
    dqt, dk, dv, arrived = _attention_backward(sv["q"], sv["k"], sv["kt"], sv["v"], do, sv["lse"], dsum, comm)
    (dproj, big["w_ukv"], small["kv_norm"], small["q_nope_norm"], small["q_rope_norm"], small["k_nope_norm"],
     small["k_rope_norm"]) = _mla_backward(sv["proj"], cosf, sins, w["w_ukv"], sm["kv_norm"], sm["q_nope_norm"],
                                            sm["q_rope_norm"], sm["k_nope_norm"], sm["k_rope_norm"], dqt, dk, dv, dproj, tr)
    dproj, small["sgu_norm"], small["w_spatial"], small["b_spatial"], dga = _sgu_backward(
        sv["proj"], sm["sgu_norm"], sm["w_spatial"], sm["b_spatial"], ga, dy, dproj, tm)
    small["out_norm"] = jnp.concatenate([dga, dgb, dgc], axis=1)
    parts_rest = _parts_rest(big)
    g_in, arrived_rest = _matmul_tn(sv["hn"], dproj, _tile(s_len, 2048), 512, 2304, "grad_w_in",
                                    (parts_rest, [False] * len(parts_rest)) if scatter_own else None)
    parts = [_part_w_in(g_in)] + parts_rest
    dhn, arrived_in = _matmul_nt(dproj, w["w_in"], _tile(s_len, 1024), 256, "grad_attn_norm_in",
                                 (parts[:1], [False]) if scatter_own else None)
    dh, small["attn_norm"] = _rms_backward(sv["h"], sm["attn_norm"], dhn, dh1, tr, "attn_norm_backward")
    return dh, parts, big["conv_w"], small, arrived, (*arrived_in, *arrived_rest) if scatter_own else None


def _layer_small(params, layer):
    return dict(
        attn_norm=params["attn_norm"][layer][None, :], sgu_norm=params["sgu_norm"][layer],
        w_spatial=params["w_spatial"][layer], b_spatial=params["b_spatial"][layer][:, :, None],
        conv_b=params["conv_b"][layer][None, :], kv_norm=params["kv_norm"][layer][None, :],
        q_nope_norm=params["q_nope_norm"][layer][None, :], q_rope_norm=params["q_rope_norm"][layer][None, :],
        k_nope_norm=params["k_nope_norm"][layer][None, :], k_rope_norm=params["k_rope_norm"][layer][None, :],
        out_norm=params["out_norm"][layer][None, :], ple_norm=params["ple_norm"][layer][None, :])


BIG = ("w_in", "w_ukv", "w_out", "w_ple_gate", "w_ple_proj")


def _assemble_w_in(g_in):
    return _permute_in(g_in.transpose(1, 0, 2).reshape(g_in.shape[1], IN_WIDTH))


def _assemble_rest(gathered):
    g_ukv, g_out, g_gate, g_proj = gathered
    w_ukv = g_ukv.reshape(N_DEV, KV_RANK, 2, HEAD).transpose(1, 2, 0, 3).reshape(KV_RANK, 2 * C_HEADS * HEAD)
    return dict(w_ukv=w_ukv, w_out=g_out.reshape(D_MODEL, D_MODEL), w_ple_gate=g_gate.reshape(D_MODEL, D_MODEL),
                w_ple_proj=g_proj.transpose(1, 0, 2).reshape(PLE_DIM, D_MODEL))


def _part_w_in(g_in):
    return _unpermute_in(g_in).reshape(g_in.shape[0], N_DEV, -1).transpose(1, 2, 0).astype(BF16)


def _parts_rest(big):
    return [
        big["w_ukv"].reshape(KV_RANK, 2, N_DEV, HEAD).transpose(2, 0, 1, 3).reshape(N_DEV, KV_RANK, 2 * HEAD).astype(BF16),
        big["w_out"].reshape(N_DEV, -1, D_MODEL).astype(BF16),
        big["w_ple_gate"].reshape(N_DEV, -1, D_MODEL).astype(BF16),
        big["w_ple_proj"].reshape(PLE_DIM, N_DEV, -1).transpose(1, 0, 2).astype(BF16)]


def _step_local(xs, ps, pos, target, shards, conv_w, params):
    inv = 1.0 / (ROPE_BASE ** (jnp.arange(0, ROPE, 2, dtype=F32) / ROPE))
    ang = pos.astype(F32)[:, None] * inv
    cos, sin = jnp.cos(ang), jnp.sin(ang)
    cosf = jnp.concatenate([cos, cos], axis=-1)
    sins = jnp.concatenate([-sin, sin], axis=-1)

    def gather_of(names, layer):
        return [shards[n][layer] for n in names], [True] * len(names)

    h = xs
    saved, weights = [], []
    smalls = [_layer_small(params, layer) for layer in range(DEPTH)]
    (first_w_in,) = _exchange(*gather_of(BIG[:1], 0), "gather_first_w_in")
    w = dict(w_in=_assemble_w_in(first_w_in), conv_w=conv_w[0])
    for layer in range(DEPTH):
        comm = gather_of(BIG, layer + 1) if layer + 1 < DEPTH else None
        comm_rest = gather_of(BIG[1:], 0) if layer == 0 else None
        h, sv, w, arrived = _layer_forward(h, ps[layer], cosf, sins, w, smalls[layer], comm, comm_rest)
        saved.append(sv)
        weights.append(w)
        if comm is not None:
            w = dict(w_in=_assemble_w_in(arrived[0]), conv_w=conv_w[layer + 1], **_assemble_rest(arrived[1:]))
    dh, loss = _loss_grad(h, target, _tile(h.shape[0], 512))
    received, conv_grads, small_grads = [None] * DEPTH, [None] * DEPTH, [None] * DEPTH
    comm = None
    for layer in reversed(range(DEPTH)):
        dh, parts, conv_grads[layer], small_grads[layer], arrived, arrived_own = _layer_backward(
            dh, ps[layer], cosf, sins, weights[layer], smalls[layer], saved[layer], comm, layer == 0)
        if comm is not None:
            received[layer + 1] = arrived
        comm = (parts, [False] * len(parts))
    received[0] = arrived_own
    return loss, dh, received, conv_grads, small_grads


def kernel(x, p, positions, attn_norm, w_in, sgu_norm, w_spatial, b_spatial, conv_w, conv_b, kv_norm, w_ukv, q_nope_norm, q_rope_norm, k_nope_norm, k_rope_norm, out_norm, w_out, ple_norm, w_ple_gate, w_ple_proj, loss_target, m_attn_norm, m_w_in, m_sgu_norm, m_w_spatial, m_b_spatial, m_conv_w, m_conv_b, m_kv_norm, m_w_ukv, m_q_nope_norm, m_q_rope_norm, m_k_nope_norm, m_k_rope_norm, m_out_norm, m_w_out, m_ple_norm, m_w_ple_gate, m_w_ple_proj, v_attn_norm, v_w_in, v_sgu_norm, v_w_spatial, v_b_spatial, v_conv_w, v_conv_b, v_kv_norm, v_w_ukv, v_q_nope_norm, v_q_rope_norm, v_k_nope_norm, v_k_rope_norm, v_out_norm, v_w_out, v_ple_norm, v_w_ple_gate, v_w_ple_proj):
    order = ("attn_norm", "w_in", "sgu_norm", "w_spatial", "b_spatial", "conv_w", "conv_b", "kv_norm", "w_ukv",
             "q_nope_norm", "q_rope_norm", "k_nope_norm", "k_rope_norm", "out_norm", "w_out", "ple_norm", "w_ple_gate",
             "w_ple_proj")
    wts = dict(zip(order, (attn_norm, w_in, sgu_norm, w_spatial, b_spatial, conv_w, conv_b, kv_norm, w_ukv, q_nope_norm,
                           q_rope_norm, k_nope_norm, k_rope_norm, out_norm, w_out, ple_norm, w_ple_gate, w_ple_proj)))
    mom = dict(zip(order, (m_attn_norm, m_w_in, m_sgu_norm, m_w_spatial, m_b_spatial, m_conv_w, m_conv_b, m_kv_norm, m_w_ukv,
                           m_q_nope_norm, m_q_rope_norm, m_k_nope_norm, m_k_rope_norm, m_out_norm, m_w_out, m_ple_norm,
                           m_w_ple_gate, m_w_ple_proj)))
    var = dict(zip(order, (v_attn_norm, v_w_in, v_sgu_norm, v_w_spatial, v_b_spatial, v_conv_w, v_conv_b, v_kv_norm, v_w_ukv,
                           v_q_nope_norm, v_q_rope_norm, v_k_nope_norm, v_k_rope_norm, v_out_norm, v_w_out, v_ple_norm,
                           v_w_ple_gate, v_w_ple_proj)))

    conv_shard = wts["conv_w"]
    (conv_all,) = _exchange([conv_shard.reshape(-1, 128)], [True], "gather_conv_w")
    conv_full = conv_all.reshape(N_DEV, DEPTH, 3, -1).transpose(1, 2, 0, 3).reshape(DEPTH, 3, -1)
    shards = {n: wts[n].astype(BF16) for n in BIG}
    loss_part, grad_x, received, conv_grads, small_grads = _step_local(
        x[0], p[:, 0], positions[0], loss_target[0], shards, conv_full, wts)
    loss = lax.psum(loss_part[0, 0], ("x", "y", "c"))

    def small_grad(name):
        g = jnp.stack([sg[name] for sg in small_grads])
        return g.reshape(wts[name].shape)

    conv_grad = jnp.stack(conv_grads)
    like = [wts[n] for n in SMALL] + [conv_grad]
    packed = _pack([small_grad(n) for n in SMALL] + [conv_grad])
    (small_parts,) = _exchange([packed], [True], "gather_small_grads")
    filler = [jnp.zeros_like(conv_grad), jnp.zeros_like(conv_grad), jnp.ones_like(conv_grad)]
    small_out = _sum_adam(small_parts, *(_pack([src[n] for n in SMALL] + [fill])[None] for src, fill in zip((wts, mom, var), filler)),
                          0, None, "adam_small")
    unpacked = [_unpack(o[0], like) for o in small_out]
    results = {n: vals for n, vals in zip(SMALL, zip(*[u[:-1] for u in unpacked]))}
    me = 4 * lax.axis_index("x") + 2 * lax.axis_index("y") + lax.axis_index("c")
    width = conv_shard.shape[2]
    conv_local = lax.dynamic_slice_in_dim(unpacked[0][-1], me * width, width, axis=2)
    as_slab = (lambda t: t.reshape(1, -1, width))
    conv_out = _sum_adam(as_slab(conv_local), as_slab(conv_shard), as_slab(mom["conv_w"]), as_slab(var["conv_w"]), 0, None,
                         "adam_conv_w")
    results["conv_w"] = tuple(o.reshape(conv_shard.shape) for o in conv_out)

    for j, name in enumerate(BIG):
        view = (lambda t: jnp.swapaxes(t, 1, 2)) if name == "w_in" else (lambda t: t)
        outs = None
        for layer in range(DEPTH):
            outs = _sum_adam(received[layer][j], view(wts[name]), view(mom[name]), view(var[name]), layer, outs, "adam_" + name)
        results[name] = tuple(view(o) for o in outs)

    grads, deltas, new_m, new_v = ([results[n][j] for n in order] for j in range(4))
    return (loss, grad_x[None], *grads, *deltas, *new_m, *new_v)
```

```python
import functools

import jax
import jax.numpy as jnp
from jax import lax
from jax.experimental import pallas as pl
from jax.experimental.pallas import tpu as pltpu

F32 = jnp.float32
BF16 = jnp.bfloat16

N_DEV = 8
DEPTH = 4
D_MODEL = 2048
EPS = 1e-6
CHUNK = 128
A_HEADS = 4
HEAD = 128
ROPE = 64
HALF = ROPE // 2
C_HEADS = 8
KV_RANK = 512
PLE_DIM = 256
ROPE_BASE = 10000.0
IN_WIDTH = 6720
QK = HEAD + ROPE
SCALE = QK ** -0.5
LOG2_E = 1.4426950408889634
LN_2 = 0.6931471805599453
Q_SCALE = SCALE * LOG2_E
HALO = 8

O_B = 0
W_B = 2048
O_CZ = 2048
W_CZ = 1024
O_A = 3072
W_A = 1536
O_M = 4608
W_M = 2304
M_QN, M_QR, M_CKV, M_KR = 0, 1024, 1536, 2048
PROJ_W = 6912

ADAM_LR = 0.001
ADAM_B1 = 0.9
ADAM_B2 = 0.999
ADAM_EPS = 1e-08
ADAM_WD = 0.01
ADAM_STEP = 10

VMEM_LIMIT = 56 * 1024 * 1024

_NT = (((1,), (1,)), ((), ()))
_TN = (((0,), (0,)), ((), ()))


def _params(*sem):
    return pltpu.CompilerParams(dimension_semantics=sem, vmem_limit_bytes=VMEM_LIMIT)


@jax.custom_vjp
def _bdot(a, b):
    return jnp.dot(a.astype(BF16), b.astype(BF16), preferred_element_type=F32)


def _bdot_fwd(a, b):
    return _bdot(a, b), (a, b)


def _bdot_bwd(res, g):
    a, b = res
    gb = g.astype(BF16)
    da = lax.dot_general(gb, b.astype(BF16), _NT, preferred_element_type=F32)
    db = lax.dot_general(a.astype(BF16), gb, _TN, preferred_element_type=F32)
    return da.astype(a.dtype), db.astype(b.dtype)


_bdot.defvjp(_bdot_fwd, _bdot_bwd)


@functools.partial(jax.custom_vjp, nondiff_argnums=(1,))
def _split(x, n):
    w = x.shape[-1] // n
    return tuple(x[:, i * w:(i + 1) * w] for i in range(n))


def _split_fwd(x, n):
    return _split(x, n), None


def _split_bwd(n, _, gs):
    return (jnp.concatenate(gs, axis=-1),)


_split.defvjp(_split_fwd, _split_bwd)


@functools.partial(jax.custom_vjp, nondiff_argnums=(1,))
def _shift_rows(x, k):
    return pltpu.roll(x, k % x.shape[0], 0)


def _shift_rows_fwd(x, k):
    return _shift_rows(x, k), None


def _shift_rows_bwd(k, _, g):
    return (_shift_rows(g, -k),)


_shift_rows.defvjp(_shift_rows_fwd, _shift_rows_bwd)


@jax.custom_vjp
def _swap_halves(x):
    h = x.shape[-1] // 2
    return jnp.concatenate([x[:, h:], x[:, :h]], axis=-1)


def _swap_halves_fwd(x):
    return _swap_halves(x), None


def _swap_halves_bwd(_, g):
    return (_swap_halves(g),)


_swap_halves.defvjp(_swap_halves_fwd, _swap_halves_bwd)


def _rms(x, g):
    return x * lax.rsqrt(jnp.mean(x * x, axis=-1, keepdims=True) + EPS) * g


def _rope(x, cosf, sins):
    return x * cosf + _swap_halves(x) * sins


def _sgu_chunk(u, v, z, gain, ws, bs, ga):
    ys = []
    for h in range(A_HEADS):
        vn = _rms(v[h], gain[h])
        s = _bdot(ws[h], vn) + bs[h]
        ys.append(u[h] * s * jax.nn.silu(z[h]))
    ss = sum(jnp.sum(y * y, axis=-1, keepdims=True) for y in ys) * (1.0 / (A_HEADS * HEAD))
    r = lax.rsqrt(ss + EPS)
    return tuple(ys[h] * r * ga[h] for h in range(A_HEADS))


def _conv_tile(bb, bc, bh, bz, w0, w1, w2, cb, gb, w0h, w1h, w2h, cbh, gbh, valid, core):
    t = jnp.where(valid, bc * bh, 0.0)
    y = (jnp.where(core, cb, cbh)
         + _shift_rows(t, 1) * jnp.where(core, w0, w0h)
         + t * jnp.where(core, w1, w1h)
         + _shift_rows(t, -1) * jnp.where(core, w2, w2h))
    return _rms(bb * y * jax.nn.silu(bz), jnp.where(core, gb, gbh))


def _cgate_tile(o, cz, gc):
    return _rms(o * jax.nn.silu(cz), gc)


def _mla_tile(qn, qr, ckv, kr, cosf, sins, wukv, kvg, qng, qrg, kng, krg):
    kv = _split(_bdot(_rms(ckv, kvg), wukv), 2 * C_HEADS)
    k_r = _rope(_rms(kr, krg), cosf, sins)
    qn_h = _split(qn, C_HEADS)
    qr_h = _split(qr, C_HEADS)
    q, k, v = [], [], []
    for h in range(C_HEADS):
        q.append(jnp.concatenate([_rms(qn_h[h], qng), _rope(_rms(qr_h[h], qrg), cosf, sins)], axis=-1) * Q_SCALE)
        k.append(jnp.concatenate([_rms(kv[h], kng), k_r], axis=-1))
        v.append(kv[C_HEADS + h])
    return tuple(q), tuple(k), tuple(v)


def _norm_matmul(h, gain, w, tm, tn, comm=None):
    s_len, k = h.shape
    n = w.shape[1]
    c_ops, c_in_specs, c_shapes, c_out_specs, c_sems, c_begin, c_end = _riding_exchange(comm, 2)
    n_c = len(c_ops)

    def body(h_ref, g_ref, w_ref, *rest):
        c_ins, (o_ref, hn_ref), c_outs, sems = rest[:n_c], rest[n_c:n_c + 2], rest[n_c + 2:2 * n_c + 2], rest[2 * n_c + 2:]
        c_begin(c_ins, c_outs, sems)

        @pl.when(pl.program_id(1) == 0)
        def _():
            hn_ref[...] = _rms(h_ref[...], g_ref[...]).astype(BF16)

        o_ref[...] = jnp.dot(hn_ref[...], w_ref[...], preferred_element_type=F32).astype(BF16)
        c_end(c_ins, c_outs, sems)

    outs = pl.pallas_call(
        body, name="norm_matmul",
        out_shape=(jax.ShapeDtypeStruct((s_len, n), BF16), jax.ShapeDtypeStruct((s_len, k), BF16), *c_shapes),
        grid=(s_len // tm, n // tn),
        in_specs=[pl.BlockSpec((tm, k), lambda i, j: (i, 0)), pl.BlockSpec((1, k), lambda i, j: (0, 0)),
                  pl.BlockSpec((k, tn), lambda i, j: (0, j))] + c_in_specs,
        out_specs=(pl.BlockSpec((tm, tn), lambda i, j: (i, j)), pl.BlockSpec((tm, k), lambda i, j: (i, 0)), *c_out_specs),
        scratch_shapes=c_sems,
        compiler_params=_params("arbitrary", "arbitrary") if comm else _params("parallel", "arbitrary"),
    )(h, gain, w, *c_ops)
    return outs[0], outs[1], outs[2:]


def _out_matmul(h, y, w, tm, tn):
    s_len, n = h.shape
    k = y.shape[1]

    def body(h_ref, y_ref, w_ref, o_ref):
        o_ref[...] = h_ref[...] + jnp.dot(y_ref[...], w_ref[...], preferred_element_type=F32)

    return pl.pallas_call(
        body, name="out_matmul",
        out_shape=jax.ShapeDtypeStruct((s_len, n), F32),
        grid=(s_len // tm, n // tn),
        in_specs=[pl.BlockSpec((tm, tn), lambda i, j: (i, j)), pl.BlockSpec((tm, k), lambda i, j: (i, 0)),
                  pl.BlockSpec((k, tn), lambda i, j: (0, j))],
        out_specs=pl.BlockSpec((tm, tn), lambda i, j: (i, j)),
        compiler_params=_params("parallel", "parallel"),
    )(h, y, w)


def _ple_forward(h1, gain, p, wg, wp, tm, tn):
    s_len, d = h1.shape
    kp = p.shape[1]

    def body(hrow_ref, hcol_ref, g_ref, p_ref, wg_ref, wp_ref, o_ref, n1_ref, gate_ref, pp_ref):
        @pl.when(pl.program_id(1) == 0)
        def _():
            n1_ref[...] = _rms(hrow_ref[...], g_ref[...]).astype(BF16)

        gate = jax.nn.sigmoid(jnp.dot(n1_ref[...], wg_ref[...], preferred_element_type=F32))
        pp = jnp.dot(p_ref[...].astype(BF16), wp_ref[...], preferred_element_type=F32)
        o_ref[...] = hcol_ref[...] + gate * pp
        gate_ref[...] = gate.astype(BF16)
        pp_ref[...] = pp.astype(BF16)

    col = pl.BlockSpec((tm, tn), lambda i, j: (i, j))
    return pl.pallas_call(
        body, name="ple_forward",
        out_shape=(jax.ShapeDtypeStruct((s_len, d), F32), jax.ShapeDtypeStruct((s_len, d), BF16),
                   jax.ShapeDtypeStruct((s_len, d), BF16), jax.ShapeDtypeStruct((s_len, d), BF16)),
        grid=(s_len // tm, d // tn),
        in_specs=[pl.BlockSpec((tm, d), lambda i, j: (i, 0)), col, pl.BlockSpec((1, d), lambda i, j: (0, 0)),
                  pl.BlockSpec((tm, kp), lambda i, j: (i, 0)), pl.BlockSpec((d, tn), lambda i, j: (0, j)),
                  pl.BlockSpec((kp, tn), lambda i, j: (0, j))],
        out_specs=(col, pl.BlockSpec((tm, d), lambda i, j: (i, 0)), col, col),
        compiler_params=_params("parallel", "arbitrary"),
    )(h1, h1, gain, p, wg, wp)


def _matmul_nt(a, b, tm, tk, name, comm=None):
    m, n = a.shape
    k = b.shape[0]
    c_ops, c_in_specs, c_shapes, c_out_specs, c_sems, c_begin, c_end = _riding_exchange(comm, 2)
    n_c = len(c_ops)

    def body(a_ref, b_ref, *rest):
        c_ins, o_ref, c_outs, sems = rest[:n_c], rest[n_c], rest[n_c + 1:2 * n_c + 1], rest[2 * n_c + 1:]
        c_begin(c_ins, c_outs, sems)
        o_ref[...] = lax.dot_general(a_ref[...].astype(BF16), b_ref[...].astype(BF16), _NT, preferred_element_type=F32)
        c_end(c_ins, c_outs, sems)

    outs = pl.pallas_call(
        body, name=name,
        out_shape=(jax.ShapeDtypeStruct((m, k), F32), *c_shapes),
        grid=(m // tm, k // tk),
        in_specs=[pl.BlockSpec((tm, n), lambda i, j: (i, 0)), pl.BlockSpec((tk, n), lambda i, j: (j, 0))] + c_in_specs,
        out_specs=(pl.BlockSpec((tm, tk), lambda i, j: (i, j)), *c_out_specs),
        scratch_shapes=c_sems,
        compiler_params=_params("arbitrary", "arbitrary") if comm else _params("parallel", "parallel"),
    )(a, b, *c_ops)
    return outs[0], outs[1:]


def _matmul_tn(a, b, tm, tk, tn, name, comm=None):
    m, k = a.shape
    n = b.shape[1]
    n_m = m // tm
    c_ops, c_in_specs, c_shapes, c_out_specs, c_sems, c_begin, c_end = _riding_exchange(comm, 3)
    n_c = len(c_ops)

    def body(a_ref, b_ref, *rest):
        c_ins, o_ref, c_outs, acc_ref, sems = rest[:n_c], rest[n_c], rest[n_c + 1:2 * n_c + 1], rest[2 * n_c + 1], rest[2 * n_c + 2:]
        c_begin(c_ins, c_outs, sems)
        part = lax.dot_general(a_ref[...].astype(BF16), b_ref[...].astype(BF16), _TN, preferred_element_type=F32)
        _acc(acc_ref, part, pl.program_id(2) == 0)

        @pl.when(pl.program_id(2) == n_m - 1)
        def _():
            o_ref[...] = acc_ref[...].astype(BF16)

        c_end(c_ins, c_outs, sems)

    outs = pl.pallas_call(
        body, name=name,
        out_shape=(jax.ShapeDtypeStruct((k, n), BF16), *c_shapes),
        grid=(k // tk, n // tn, n_m),
        in_specs=[pl.BlockSpec((tm, tk), lambda kk, nn, mm: (mm, kk)), pl.BlockSpec((tm, tn), lambda kk, nn, mm: (mm, nn))]
                 + c_in_specs,
        out_specs=(pl.BlockSpec((tk, tn), lambda kk, nn, mm: (kk, nn)), *c_out_specs),
        scratch_shapes=[pltpu.VMEM((tk, tn), F32)] + c_sems,
        compiler_params=_params("arbitrary", "arbitrary", "arbitrary") if comm else _params("parallel", "parallel", "arbitrary"),
    )(a, b, *c_ops)
    return outs[0], outs[1:]


def _acc(ref, val, first):
    @pl.when(first)
    def _():
        ref[...] = val

    @pl.when(jnp.logical_not(first))
    def _():
        ref[...] += val


def _loss_grad(h, target, tm):
    s_len, d = h.shape

    def body(h_ref, t_ref, dh_ref, loss_ref):
        e = h_ref[...] - t_ref[...]
        dh_ref[...] = e * (1.0 / d)
        part = jnp.sum(jnp.sum(e * e, axis=-1, keepdims=True), axis=0, keepdims=True) * (0.5 / d)
        _acc(loss_ref, jnp.broadcast_to(part, loss_ref.shape), pl.program_id(0) == 0)

    row = pl.BlockSpec((tm, d), lambda i: (i, 0))
    return pl.pallas_call(
        body, name="loss_grad",
        out_shape=(jax.ShapeDtypeStruct((s_len, d), F32), jax.ShapeDtypeStruct((1, 128), F32)),
        grid=(s_len // tm,),
        in_specs=[row, row],
        out_specs=(row, pl.BlockSpec((1, 128), lambda i: (0, 0))),
        compiler_params=_params("arbitrary"),
    )(h, target)


def _rms_backward(x, gain, dn, dres, tm, name):
    s_len, d = x.shape

    def body(x_ref, g_ref, dn_ref, dres_ref, dx_ref, dg_ref):
        _, vjp = jax.vjp(_rms, x_ref[...], g_ref[...])
        dx, dg = vjp(dn_ref[...])
        dx_ref[...] = dres_ref[...] + dx
        _acc(dg_ref, dg, pl.program_id(0) == 0)

    row = pl.BlockSpec((tm, d), lambda i: (i, 0))
    vec = pl.BlockSpec((1, d), lambda i: (0, 0))
    return pl.pallas_call(
        body, name=name,
        out_shape=(jax.ShapeDtypeStruct((s_len, d), F32), jax.ShapeDtypeStruct((1, d), F32)),
        grid=(s_len // tm,),
        in_specs=[row, vec, row, row],
        out_specs=(row, vec),
        compiler_params=_params("arbitrary"),
    )(x, gain, dn, dres)


def _ple_backward(dh2, gate, pp, wg, h1, gain, tm):
    s_len, d = dh2.shape

    def body(dh_ref, gate_ref, pp_ref, wg_ref, h1_ref, g_ref, dh1_ref, dgain_ref, dgp_ref, dpp_ref):
        dh = dh_ref[...]
        gate = gate_ref[...].astype(F32)
        dgp = (dh * pp_ref[...].astype(F32) * gate * (1.0 - gate)).astype(BF16)
        dgp_ref[...] = dgp
        dpp_ref[...] = (dh * gate).astype(BF16)
        dn = lax.dot_general(dgp, wg_ref[...], _NT, preferred_element_type=F32)
        _, vjp = jax.vjp(_rms, h1_ref[...], g_ref[...])
        dx, dgain = vjp(dn)
        dh1_ref[...] = dh + dx
        _acc(dgain_ref, dgain, pl.program_id(0) == 0)

    row = pl.BlockSpec((tm, d), lambda i: (i, 0))
    vec = pl.BlockSpec((1, d), lambda i: (0, 0))
    return pl.pallas_call(
        body, name="ple_backward",
        out_shape=(jax.ShapeDtypeStruct((s_len, d), F32), jax.ShapeDtypeStruct((1, d), F32),
                   jax.ShapeDtypeStruct((s_len, d), BF16), jax.ShapeDtypeStruct((s_len, d), BF16)),
        grid=(s_len // tm,),
        in_specs=[row, row, row, pl.BlockSpec((d, d), lambda i: (0, 0)), row, vec],
        out_specs=(row, vec, row, row),
        compiler_params=_params("arbitrary"),
    )(dh2, gate, pp, wg, h1, gain)


def _sgu_in_specs(tm):
    return [pl.BlockSpec((tm, W_A), lambda i: (i, O_A // W_A)),
            pl.BlockSpec((A_HEADS, HEAD), lambda i: (0, 0)), pl.BlockSpec((A_HEADS, CHUNK, CHUNK), lambda i: (0, 0, 0)),
            pl.BlockSpec((A_HEADS, CHUNK, 1), lambda i: (0, 0, 0)), pl.BlockSpec((1, 512), lambda i: (0, 0))]


def _sgu_load(a_ref, gain_ref, ws_ref, bs_ref, ga_ref, c):
    rows = slice(c * CHUNK, (c + 1) * CHUNK)
    heads = range(A_HEADS)
    u = tuple(a_ref[rows, h * HEAD:(h + 1) * HEAD].astype(F32) for h in heads)
    v = tuple(a_ref[rows, 512 + h * HEAD:512 + (h + 1) * HEAD].astype(F32) for h in heads)
    z = tuple(a_ref[rows, 1024 + h * HEAD:1024 + (h + 1) * HEAD].astype(F32) for h in heads)
    gain = tuple(gain_ref[h:h + 1, :] for h in heads)
    ws = tuple(ws_ref[h] for h in heads)
    bs = tuple(bs_ref[h] for h in heads)
    ga = tuple(ga_ref[:, h * HEAD:(h + 1) * HEAD] for h in heads)
    return u, v, z, gain, ws, bs, ga


def _sgu_forward(proj, gain, ws, bs, ga, tm):
    s_len = proj.shape[0]

    def body(a_ref, gain_ref, ws_ref, bs_ref, ga_ref, o_ref):
        for c in range(tm // CHUNK):
            out = _sgu_chunk(*_sgu_load(a_ref, gain_ref, ws_ref, bs_ref, ga_ref, c))
            for h in range(A_HEADS):
                o_ref[c * CHUNK:(c + 1) * CHUNK, h * HEAD:(h + 1) * HEAD] = out[h].astype(BF16)

    return pl.pallas_call(
        body, name="sgu_forward",
        out_shape=jax.ShapeDtypeStruct((s_len, D_MODEL), BF16),
        grid=(s_len // tm,),
        in_specs=_sgu_in_specs(tm),
        out_specs=pl.BlockSpec((tm, 512), lambda i: (i, 0)),
        compiler_params=_params("parallel"),
    )(proj, gain, ws, bs, ga)


def _sgu_backward(proj, gain, ws, bs, ga, dy, dproj, tm):
    s_len = proj.shape[0]

    def body(a_ref, gain_ref, ws_ref, bs_ref, ga_ref, dy_ref, _, da_ref, dgain_ref, dws_ref, dbs_ref, dga_ref):
        tot = None
        for c in range(tm // CHUNK):
            args = _sgu_load(a_ref, gain_ref, ws_ref, bs_ref, ga_ref, c)
            _, vjp = jax.vjp(_sgu_chunk, *args)
            rows = slice(c * CHUNK, (c + 1) * CHUNK)
            du, dv, dz, dgain, dws, dbs, dga = vjp(tuple(dy_ref[rows, h * HEAD:(h + 1) * HEAD] for h in range(A_HEADS)))
            for h in range(A_HEADS):
                da_ref[rows, h * HEAD:(h + 1) * HEAD] = du[h].astype(BF16)
                da_ref[rows, 512 + h * HEAD:512 + (h + 1) * HEAD] = dv[h].astype(BF16)
                da_ref[rows, 1024 + h * HEAD:1024 + (h + 1) * HEAD] = dz[h].astype(BF16)
            part = (dgain, dws, dbs, dga)
            tot = part if tot is None else jax.tree.map(jnp.add, tot, part)
        dgain, dws, dbs, dga = tot
        first = pl.program_id(0) == 0
        _acc(dgain_ref, jnp.concatenate(dgain, axis=0), first)
        _acc(dga_ref, jnp.concatenate(dga, axis=-1), first)
        for h in range(A_HEADS):
            _acc(dws_ref.at[h], dws[h], first)
            _acc(dbs_ref.at[h], dbs[h], first)

    small = [pl.BlockSpec((A_HEADS, HEAD), lambda i: (0, 0)), pl.BlockSpec((A_HEADS, CHUNK, CHUNK), lambda i: (0, 0, 0)),
             pl.BlockSpec((A_HEADS, CHUNK, 1), lambda i: (0, 0, 0)), pl.BlockSpec((1, 512), lambda i: (0, 0))]
    return pl.pallas_call(
        body, name="sgu_backward",
        out_shape=(jax.ShapeDtypeStruct(dproj.shape, BF16),
                   jax.ShapeDtypeStruct((A_HEADS, HEAD), F32), jax.ShapeDtypeStruct((A_HEADS, CHUNK, CHUNK), F32),
                   jax.ShapeDtypeStruct((A_HEADS, CHUNK, 1), F32), jax.ShapeDtypeStruct((1, 512), F32)),
        grid=(s_len // tm,),
        in_specs=_sgu_in_specs(tm) + [pl.BlockSpec((tm, 512), lambda i: (i, 0)), pl.BlockSpec(memory_space=pl.ANY)],
        out_specs=(pl.BlockSpec((tm, W_A), lambda i: (i, O_A // W_A)), *small),
        input_output_aliases={6: 0},
        compiler_params=_params("arbitrary"),
    )(proj, gain, ws, bs, ga, dy, dproj)


def _halo_specs(tm, width, col, n_rows):
    per = tm // HALO
    last = n_rows // HALO - 1
    return [pl.BlockSpec((HALO, width), lambda i: (jnp.maximum(i * per - 1, 0), col)),
            pl.BlockSpec((tm, width), lambda i: (i, col)),
            pl.BlockSpec((HALO, width), lambda i: (jnp.minimum((i + 1) * per, last), col))]


def _conv_masks(tm, s_len):
    r = lax.broadcasted_iota(jnp.int32, (tm + 2 * HALO, 1), 0)
    g = pl.program_id(0) * tm - HALO + r
    return (g >= 0) & (g < s_len), (r >= HALO) & (r < HALO + tm)


def _conv_inputs(b_refs, cw_ref, cb_ref, gb_ref):
    ext = jnp.concatenate([r[...] for r in b_refs], axis=0).astype(F32)
    bb, bc, bh, bz = (ext[:, j * 512:(j + 1) * 512] for j in range(4))
    prm = (cw_ref[0:1, :], cw_ref[1:2, :], cw_ref[2:3, :], cb_ref[...], gb_ref[...])
    return (bb, bc, bh, bz), prm


def _conv_forward(proj, cw, cb, gb, y, tm):
    s_len = proj.shape[0]

    def body(p0, p1, p2, cw_ref, cb_ref, gb_ref, _, o_ref):
        acts, prm = _conv_inputs((p0, p1, p2), cw_ref, cb_ref, gb_ref)
        valid, core = _conv_masks(tm, s_len)
        out = _conv_tile(*acts, *prm, *prm, valid, core)
        o_ref[...] = out[HALO:HALO + tm].astype(BF16)

    vec = pl.BlockSpec((1, 512), lambda i: (0, 0))
    return pl.pallas_call(
        body, name="conv_forward",
        out_shape=jax.ShapeDtypeStruct(y.shape, BF16),
        grid=(s_len // tm,),
        in_specs=_halo_specs(tm, W_B, O_B // W_B, s_len) + [pl.BlockSpec((3, 512), lambda i: (0, 0)), vec, vec,
                                                             pl.BlockSpec(memory_space=pl.ANY)],
        out_specs=pl.BlockSpec((tm, 512), lambda i: (i, 1)),
        input_output_aliases={6: 0},
        compiler_params=_params("parallel"),
    )(proj, proj, proj, cw, cb, gb, y)


def _conv_backward(proj, cw, cb, gb, dy, tm):
    s_len = proj.shape[0]

    def body(p0, p1, p2, cw_ref, cb_ref, gb_ref, d0, d1, d2, db_ref, dcw_ref, dcb_ref, dgb_ref):
        acts, prm = _conv_inputs((p0, p1, p2), cw_ref, cb_ref, gb_ref)
        valid, core = _conv_masks(tm, s_len)
        _, vjp = jax.vjp(lambda a, p: _conv_tile(*a, *p, *prm, valid, core), acts, prm)
        dy_ext = jnp.where(valid, jnp.concatenate([d0[...], d1[...], d2[...]], axis=0), 0.0)
        dacts, dprm = vjp(dy_ext)
        for j in range(4):
            db_ref[:, j * 512:(j + 1) * 512] = dacts[j][HALO:HALO + tm].astype(BF16)
        first = pl.program_id(0) == 0
        _acc(dcw_ref, jnp.concatenate(dprm[0:3], axis=0), first)
        _acc(dcb_ref, dprm[3], first)
        _acc(dgb_ref, dprm[4], first)

    vec = pl.BlockSpec((1, 512), lambda i: (0, 0))
    mat = pl.BlockSpec((3, 512), lambda i: (0, 0))
    return pl.pallas_call(
        body, name="conv_backward",
        out_shape=(jax.ShapeDtypeStruct((s_len, PROJ_W), BF16), jax.ShapeDtypeStruct((3, 512), F32),
                   jax.ShapeDtypeStruct((1, 512), F32), jax.ShapeDtypeStruct((1, 512), F32)),
        grid=(s_len // tm,),
        in_specs=_halo_specs(tm, W_B, O_B // W_B, s_len) + [mat, vec, vec] + _halo_specs(tm, 512, 1, s_len),
        out_specs=(pl.BlockSpec((tm, W_B), lambda i: (i, O_B // W_B)), mat, vec, vec),
        compiler_params=_params("arbitrary"),
    )(proj, proj, proj, cw, cb, gb, dy, dy, dy)


def _cgate_forward(o, proj, gc, y, tm):
    s_len = o.shape[0]

    def body(o_ref, cz_ref, gc_ref, _, y_ref):
        y_ref[...] = _cgate_tile(o_ref[...], cz_ref[...].astype(F32), gc_ref[...]).astype(BF16)

    return pl.pallas_call(
        body, name="cgate_forward",
        out_shape=jax.ShapeDtypeStruct(y.shape, BF16),
        grid=(s_len // tm,),
        in_specs=[pl.BlockSpec((tm, W_CZ), lambda i: (i, 0)), pl.BlockSpec((tm, W_CZ), lambda i: (i, O_CZ // W_CZ)),
                  pl.BlockSpec((1, W_CZ), lambda i: (0, 0)), pl.BlockSpec(memory_space=pl.ANY)],
        out_specs=pl.BlockSpec((tm, W_CZ), lambda i: (i, 1)),
        input_output_aliases={3: 0},
        compiler_params=_params("parallel"),
    )(o, proj, gc, y)


def _cgate_backward(o, proj, gc, dy, dproj, tm, stat_chunk):
    s_len = o.shape[0]
    per_stat = stat_chunk // tm

    def body(o_ref, cz_ref, gc_ref, dy_ref, _, dcz_ref, do_ref, dsum_ref, dgc_ref):
        o = o_ref[...]
        _, vjp = jax.vjp(_cgate_tile, o, cz_ref[...].astype(F32), gc_ref[...])
        do, dcz, dgc = vjp(dy_ref[...])
        dcz_ref[...] = dcz.astype(BF16)
        do_ref[...] = do.astype(BF16)
        ones = jnp.ones((8, HEAD), F32)
        for h in range(C_HEADS):
            cols = slice(h * HEAD, (h + 1) * HEAD)
            sums = lax.dot_general(ones, do[:, cols] * o[:, cols], _NT, precision=lax.Precision.HIGHEST,
                                   preferred_element_type=F32)
            dsum_ref[h, 0] = sums[0:1]
        _acc(dgc_ref, dgc, pl.program_id(0) == 0)

    row = pl.BlockSpec((tm, W_CZ), lambda i: (i, 0))
    vec = pl.BlockSpec((1, W_CZ), lambda i: (0, 0))
    return pl.pallas_call(
        body, name="cgate_backward",
        out_shape=(jax.ShapeDtypeStruct(dproj.shape, BF16), jax.ShapeDtypeStruct((s_len, W_CZ), BF16),
                   jax.ShapeDtypeStruct((C_HEADS, s_len // stat_chunk, 1, stat_chunk), F32), jax.ShapeDtypeStruct((1, W_CZ), F32)),
        grid=(s_len // tm,),
        in_specs=[row, pl.BlockSpec((tm, W_CZ), lambda i: (i, O_CZ // W_CZ)), vec,
                  pl.BlockSpec((tm, W_CZ), lambda i: (i, 1)), pl.BlockSpec(memory_space=pl.ANY)],
        out_specs=(pl.BlockSpec((tm, W_CZ), lambda i: (i, O_CZ // W_CZ)), row,
                   pl.BlockSpec((C_HEADS, 1, 1, tm), lambda i: (0, i // per_stat, 0, i % per_stat)), vec),
        input_output_aliases={4: 0},
        compiler_params=_params("arbitrary"),
    )(o, proj, gc, dy, dproj)


def _mla_small_specs():
    return [pl.BlockSpec((KV_RANK, 2 * C_HEADS * HEAD), lambda i: (0, 0)), pl.BlockSpec((1, KV_RANK), lambda i: (0, 0)),
            pl.BlockSpec((1, HEAD), lambda i: (0, 0)), pl.BlockSpec((1, ROPE), lambda i: (0, 0)),
            pl.BlockSpec((1, HEAD), lambda i: (0, 0)), pl.BlockSpec((1, ROPE), lambda i: (0, 0))]


def _mla_load(m_ref, cos_ref, sin_ref):
    qn = m_ref[:, M_QN:M_QN + C_HEADS * HEAD].astype(F32)
    qr = m_ref[:, M_QR:M_QR + C_HEADS * ROPE].astype(F32)
    ckv = m_ref[:, M_CKV:M_CKV + KV_RANK].astype(F32)
    kr = m_ref[:, M_KR:M_KR + ROPE].astype(F32)
    return qn, qr, ckv, kr, cos_ref[...], sin_ref[...]


def _mla_forward(proj, cosf, sins, wukv, kvg, qng, qrg, kng, krg, tm, kt_chunk, vt_chunk):
    s_len = proj.shape[0]

    def body(m_ref, cos_ref, sin_ref, w_ref, kvg_ref, qng_ref, qrg_ref, kng_ref, krg_ref, q_ref, k_ref, v_ref, kt_ref, vt_ref):
        q, k, v = _mla_tile(*_mla_load(m_ref, cos_ref, sin_ref), w_ref[...], kvg_ref[...], qng_ref[...], qrg_ref[...],
                            kng_ref[...], krg_ref[...])
        for h in range(C_HEADS):
            q_ref[h] = q[h].astype(BF16)
            k_ref[h] = k[h].astype(BF16)
            v_ref[h] = v[h].astype(BF16)
            kt_ref[h, 0] = jnp.concatenate([k[h][:, :HEAD].T, k[h][:, HEAD:].T], axis=0).astype(BF16)
            vt_ref[h, 0] = v[h].T.astype(BF16)

    rope_spec = pl.BlockSpec((tm, ROPE), lambda i: (i, 0))
    qk_spec = pl.BlockSpec((C_HEADS, tm, QK), lambda i: (0, i, 0))
    per_k, per_v = kt_chunk // tm, vt_chunk // tm
    return pl.pallas_call(
        body, name="mla_forward",
        out_shape=(jax.ShapeDtypeStruct((C_HEADS, s_len, QK), BF16), jax.ShapeDtypeStruct((C_HEADS, s_len, QK), BF16),
                   jax.ShapeDtypeStruct((C_HEADS, s_len, HEAD), BF16),
                   jax.ShapeDtypeStruct((C_HEADS, s_len // kt_chunk, QK, kt_chunk), BF16),
                   jax.ShapeDtypeStruct((C_HEADS, s_len // vt_chunk, HEAD, vt_chunk), BF16)),
        grid=(s_len // tm,),
        in_specs=[pl.BlockSpec((tm, W_M), lambda i: (i, O_M // W_M)), rope_spec, rope_spec] + _mla_small_specs(),
        out_specs=(qk_spec, qk_spec, pl.BlockSpec((C_HEADS, tm, HEAD), lambda i: (0, i, 0)),
                   pl.BlockSpec((C_HEADS, 1, QK, tm), lambda i: (0, i // per_k, 0, i % per_k)),
                   pl.BlockSpec((C_HEADS, 1, HEAD, tm), lambda i: (0, i // per_v, 0, i % per_v))),
        compiler_params=_params("parallel"),
    )(proj, cosf, sins, wukv, kvg, qng, qrg, kng, krg)


def _mla_backward(proj, cosf, sins, wukv, kvg, qng, qrg, kng, krg, dqt, dk, dv, dproj, tm):
    s_len = proj.shape[0]
    per_chunk = dqt.shape[3] // tm

    def body(m_ref, cos_ref, sin_ref, w_ref, kvg_ref, qng_ref, qrg_ref, kng_ref, krg_ref, dq_ref, dk_ref, dv_ref, _,
             dm_ref, dw_ref, dkvg_ref, dqng_ref, dqrg_ref, dkng_ref, dkrg_ref):
        qn, qr, ckv, kr, cosf_t, sins_t = _mla_load(m_ref, cos_ref, sin_ref)
        prm = (w_ref[...], kvg_ref[...], qng_ref[...], qrg_ref[...], kng_ref[...], krg_ref[...])
        _, vjp = jax.vjp(lambda a, p: _mla_tile(*a, cosf_t, sins_t, *p), (qn, qr, ckv, kr), prm)
        heads = range(C_HEADS)
        dacts, dprm = vjp((tuple(dq_ref[h, 0].T for h in heads), tuple(dk_ref[h] for h in heads), tuple(dv_ref[h] for h in heads)))
        dm_ref[:, M_QN:M_QN + C_HEADS * HEAD] = dacts[0].astype(BF16)
        dm_ref[:, M_QR:M_QR + C_HEADS * ROPE] = dacts[1].astype(BF16)
        dm_ref[:, M_CKV:M_CKV + KV_RANK] = dacts[2].astype(BF16)
        pad = jnp.zeros((tm, W_M - M_KR - ROPE), F32)
        dm_ref[:, M_KR:W_M] = jnp.concatenate([dacts[3], pad], axis=-1).astype(BF16)
        first = pl.program_id(0) == 0
        for ref, val in zip((dw_ref, dkvg_ref, dqng_ref, dqrg_ref, dkng_ref, dkrg_ref), dprm):
            _acc(ref, val.astype(F32), first)

    rope_spec = pl.BlockSpec((tm, ROPE), lambda i: (i, 0))
    qk_spec = pl.BlockSpec((C_HEADS, tm, QK), lambda i: (0, i, 0))
    small = _mla_small_specs()
    return pl.pallas_call(
        body, name="mla_backward",
        out_shape=(jax.ShapeDtypeStruct(dproj.shape, BF16), jax.ShapeDtypeStruct((KV_RANK, 2 * C_HEADS * HEAD), F32),
                   jax.ShapeDtypeStruct((1, KV_RANK), F32), jax.ShapeDtypeStruct((1, HEAD), F32),
                   jax.ShapeDtypeStruct((1, ROPE), F32), jax.ShapeDtypeStruct((1, HEAD), F32),
                   jax.ShapeDtypeStruct((1, ROPE), F32)),
        grid=(s_len // tm,),
        in_specs=[pl.BlockSpec((tm, W_M), lambda i: (i, O_M // W_M)), rope_spec, rope_spec] + small
                 + [pl.BlockSpec((C_HEADS, 1, QK, tm), lambda i: (0, i // per_chunk, 0, i % per_chunk)), qk_spec,
                    pl.BlockSpec((C_HEADS, tm, HEAD), lambda i: (0, i, 0)), pl.BlockSpec(memory_space=pl.ANY)],
        out_specs=(pl.BlockSpec((tm, W_M), lambda i: (i, O_M // W_M)), *small),
        input_output_aliases={12: 0},
        compiler_params=_params("arbitrary"),
    )(proj, cosf, sins, wukv, kvg, qng, qrg, kng, krg, dqt, dk, dv, dproj)


def _attention_forward(q, k, vt, tq, stat_chunk, comm=None):
    n_heads, s_len, _ = q.shape
    n_chunks, _, ck = vt.shape[1:]
    c_ops, c_in_specs, c_shapes, c_out_specs, c_sems, c_begin, c_end = _riding_exchange(comm, 2)
    n_c = len(c_ops)

    def body(q_ref, k_ref, vt_ref, *rest):
        c_ins, (o_ref, lse_ref), c_outs, sems = rest[:n_c], rest[n_c:n_c + 2], rest[n_c + 2:2 * n_c + 2], rest[2 * n_c + 2:]
        c_begin(c_ins, c_outs, sems)
        q_t = q_ref[0]

        def step(j, carry):
            m_old, l_old, acc = carry
            k_j = k_ref[0, pl.ds(pl.multiple_of(j * ck, ck), ck), :]
            s = lax.dot_general(k_j, q_t, _NT, preferred_element_type=F32)
            m_new = jnp.maximum(m_old, jnp.max(s, axis=0, keepdims=True))
            p = jnp.exp2(s - m_new)
            alpha = jnp.exp2(m_old - m_new)
            l_new = alpha * l_old + jnp.sum(p, axis=0, keepdims=True)
            acc = alpha * acc + jnp.dot(vt_ref[0, j], p.astype(BF16), preferred_element_type=F32)
            return m_new, l_new, acc

        init = (jnp.full((1, tq), -jnp.inf, F32), jnp.zeros((1, tq), F32), jnp.zeros((HEAD, tq), F32))
        m_fin, l_fin, acc = lax.fori_loop(0, n_chunks, step, init)
        o_ref[...] = (acc / l_fin).T
        lse_ref[0, 0] = m_fin + jnp.log2(l_fin)
        c_end(c_ins, c_outs, sems)

    per_stat = stat_chunk // tq
    outs = pl.pallas_call(
        body, name="attention_forward",
        out_shape=(jax.ShapeDtypeStruct((s_len, n_heads * HEAD), F32),
                   jax.ShapeDtypeStruct((n_heads, s_len // stat_chunk, 1, stat_chunk), F32), *c_shapes),
        grid=(n_heads, s_len // tq),
        in_specs=[pl.BlockSpec((1, tq, QK), lambda h, i: (h, i, 0)), pl.BlockSpec((1, s_len, QK), lambda h, i: (h, 0, 0)),
                  pl.BlockSpec((1, n_chunks, HEAD, ck), lambda h, i: (h, 0, 0, 0))] + c_in_specs,
        out_specs=(pl.BlockSpec((tq, HEAD), lambda h, i: (i, h)),
                   pl.BlockSpec((1, 1, 1, tq), lambda h, i: (h, i // per_stat, 0, i % per_stat)), *c_out_specs),
        scratch_shapes=c_sems,
        compiler_params=_params("arbitrary", "arbitrary") if comm else _params("parallel", "parallel"),
    )(q, k, vt, *c_ops)
    return outs[0], outs[1], outs[2:]


def _attention_backward(q, k, kt, v, do, lse, dsum, comm=None):
    n_heads, s_len, _ = q.shape
    tk = kt.shape[3]
    n_q, _, cq = lse.shape[1:]
    c_ops, c_in_specs, c_shapes, c_out_specs, c_sems, c_begin, c_end = _riding_exchange(comm, 2)
    n_c = len(c_ops)

    def body(q_ref, k_ref, kt_ref, v_ref, do_ref, lse_ref, dsum_ref, *rest):
        c_ins, (dqt_ref, dk_ref, dv_ref), c_outs, sems = rest[:n_c], rest[n_c:n_c + 3], rest[n_c + 3:2 * n_c + 3], rest[2 * n_c + 3:]
        c_begin(c_ins, c_outs, sems)
        first = pl.program_id(1) == 0
        k_j, kt_j, v_j = k_ref[0], kt_ref[0, 0], v_ref[0]

        def step(i, carry):
            dk, dv = carry
            rows = pl.ds(pl.multiple_of(i * cq, cq), cq)
            q_i, do_i = q_ref[0, rows, :], do_ref[rows, :]
            s = lax.dot_general(k_j, q_i, _NT, preferred_element_type=F32)
            p = jnp.exp2(s - lse_ref[0, i])
            dp = lax.dot_general(v_j, do_i, _NT, preferred_element_type=F32)
            ds = (p * (dp - dsum_ref[0, i]) * LN_2).astype(BF16)
            dv = dv + jnp.dot(p.astype(BF16), do_i, preferred_element_type=F32)
            dk = dk + jnp.dot(ds, q_i, preferred_element_type=F32)
            _acc(dqt_ref.at[0, i], jnp.dot(kt_j, ds, preferred_element_type=F32), first)
            return dk, dv

        dk, dv = lax.fori_loop(0, n_q, step, (jnp.zeros((tk, QK), F32), jnp.zeros((tk, HEAD), F32)))
        dk_ref[0] = dk
        dv_ref[0] = dv
        c_end(c_ins, c_outs, sems)

    stat = pl.BlockSpec((1, n_q, 1, cq), lambda h, j: (h, 0, 0, 0))
    outs = pl.pallas_call(
        body, name="attention_backward",
        out_shape=(jax.ShapeDtypeStruct((n_heads, n_q, QK, cq), F32), jax.ShapeDtypeStruct((n_heads, s_len, QK), F32),
                   jax.ShapeDtypeStruct((n_heads, s_len, HEAD), F32), *c_shapes),
        grid=(n_heads, s_len // tk),
        in_specs=[pl.BlockSpec((1, s_len, QK), lambda h, j: (h, 0, 0)), pl.BlockSpec((1, tk, QK), lambda h, j: (h, j, 0)),
                  pl.BlockSpec((1, 1, QK, tk), lambda h, j: (h, j, 0, 0)),
                  pl.BlockSpec((1, tk, HEAD), lambda h, j: (h, j, 0)), pl.BlockSpec((s_len, HEAD), lambda h, j: (0, h)),
                  stat, stat] + c_in_specs,
        out_specs=(pl.BlockSpec((1, n_q, QK, cq), lambda h, j: (h, 0, 0, 0)), pl.BlockSpec((1, tk, QK), lambda h, j: (h, j, 0)),
                   pl.BlockSpec((1, tk, HEAD), lambda h, j: (h, j, 0)), *c_out_specs),
        scratch_shapes=c_sems,
        compiler_params=_params("arbitrary", "arbitrary") if comm else _params("parallel", "arbitrary"),
    )(q, k, kt, v, do, lse, dsum, *c_ops)
    return outs[0], outs[1], outs[2], outs[3:]


def _exchange(arrs, gather, name):
    n = len(arrs)

    def body(*refs):
        plan = _exchange_plan(refs[:n], refs[n:2 * n], gather, *refs[2 * n:])
        _exchange_start(plan)
        _exchange_wait(plan)

    any_spec = pl.BlockSpec(memory_space=pl.ANY)
    return pl.pallas_call(
        body, name=name,
        out_shape=_exchange_out_shapes(arrs, gather),
        in_specs=[any_spec] * n,
        out_specs=tuple([any_spec] * n),
        scratch_shapes=_exchange_semaphores(n),
        compiler_params=pltpu.CompilerParams(has_side_effects=True),
    )(*arrs)


def _exchange_out_shapes(arrs, gather):
    return tuple(jax.ShapeDtypeStruct((N_DEV, *(a.shape if g else a.shape[1:])), a.dtype) for a, g in zip(arrs, gather))


def _exchange_semaphores(n):
    n_remote = n * (N_DEV - 1)
    return [pltpu.SemaphoreType.DMA((n_remote,)), pltpu.SemaphoreType.DMA((n_remote,)), pltpu.SemaphoreType.DMA((n,))]


def _exchange_plan(ins, outs, gather, send_sems, recv_sems, local_sems):
    n = len(ins)
    x, y, c = lax.axis_index("x"), lax.axis_index("y"), lax.axis_index("c")
    me = 4 * x + 2 * y + c

    def block_for(a, dev):
        return ins[a] if gather[a] else ins[a].at[dev]

    local = [pltpu.make_async_copy(block_for(a, me), outs[a].at[me], local_sems.at[a]) for a in range(n)]
    remote = []
    for k in range(1, N_DEV):
        px = 1 - x if k & 4 else x
        py = 1 - y if k & 2 else y
        pc = 1 - c if k & 1 else c
        peer = 4 * px + 2 * py + pc
        for a in range(n):
            idx = a * (N_DEV - 1) + k - 1
            send = pltpu.make_async_remote_copy(
                src_ref=block_for(a, peer), dst_ref=outs[a].at[me], send_sem=send_sems.at[idx], recv_sem=recv_sems.at[idx],
                device_id=(px, py, pc), device_id_type=pl.DeviceIdType.MESH)
            arrive = pltpu.make_async_remote_copy(
                src_ref=block_for(a, peer), dst_ref=outs[a].at[peer], send_sem=send_sems.at[idx], recv_sem=recv_sems.at[idx],
                device_id=(px, py, pc), device_id_type=pl.DeviceIdType.MESH)
            remote.append((send, arrive))
    return local, remote


def _exchange_start(plan):
    local, remote = plan
    for cp in local:
        cp.start()
    for send, _ in remote:
        send.start()


def _exchange_wait(plan):
    local, remote = plan
    for send, arrive in remote:
        send.wait_send()
        arrive.wait_recv()
    for cp in local:
        cp.wait()


def _riding_exchange(comm, n_grid):
    if comm is None:
        return [], [], (), (), [], lambda *_: None, lambda *_: None
    arrs, gather = comm
    n = len(arrs)
    any_spec = pl.BlockSpec(memory_space=pl.ANY)

    def begin(ins, outs, sems):
        @pl.when(functools.reduce(jnp.logical_and, [pl.program_id(d) == 0 for d in range(n_grid)]))
        def _():
            _exchange_start(_exchange_plan(ins, outs, gather, *sems))

    def end(ins, outs, sems):
        @pl.when(functools.reduce(jnp.logical_and, [pl.program_id(d) == pl.num_programs(d) - 1 for d in range(n_grid)]))
        def _():
            _exchange_wait(_exchange_plan(ins, outs, gather, *sems))

    return (list(arrs), [any_spec] * n, _exchange_out_shapes(arrs, gather), tuple([any_spec] * n), _exchange_semaphores(n),
            begin, end)


ADAM_TILE_ELEMS = 256 * 1024


def _sum_adam(parts, w, m, v, layer, prev, name):
    n_parts, r, c = parts.shape
    tm, tc = r, c
    if r % 16 == 0:
        while tm * c > ADAM_TILE_ELEMS and tm % 16 == 0:
            tm //= 2
    else:
        while r * tc > ADAM_TILE_ELEMS and tc % 256 == 0:
            tc //= 2

    def body(p_ref, w_ref, m_ref, v_ref, *rest):
        g_ref, d_ref, nm_ref, nv_ref = rest[-4:]
        g = p_ref[0].astype(F32)
        for s in range(1, n_parts):
            g = g + p_ref[s].astype(F32)
        m_new = ADAM_B1 * m_ref[...] + (1.0 - ADAM_B1) * g
        v_new = ADAM_B2 * v_ref[...] + (1.0 - ADAM_B2) * (g * g)
        m_hat = m_new / (1.0 - ADAM_B1 ** ADAM_STEP)
        v_hat = v_new / (1.0 - ADAM_B2 ** ADAM_STEP)
        g_ref[...] = g
        d_ref[...] = -ADAM_LR * (m_hat / (jnp.sqrt(v_hat) + ADAM_EPS) + ADAM_WD * w_ref[...])
        nm_ref[...] = m_new
        nv_ref[...] = v_new

    slab = pl.BlockSpec((None, tm, tc), lambda i, j: (layer, i, j))
    n_prev = 0 if prev is None else 4
    return pl.pallas_call(
        body, name=name,
        out_shape=(jax.ShapeDtypeStruct(w.shape, F32),) * 4,
        grid=(r // tm, c // tc),
        in_specs=[pl.BlockSpec((n_parts, tm, tc), lambda i, j: (0, i, j)), slab, slab, slab]
                 + [pl.BlockSpec(memory_space=pl.ANY)] * n_prev,
        out_specs=(slab, slab, slab, slab),
        input_output_aliases={4 + j: j for j in range(n_prev)},
        compiler_params=_params("parallel", "parallel"),
    )(parts, w, m, v, *(prev or ()))


def _permute_in(w):
    k = w.shape[0]
    q = w[:, 3584:5120].reshape(k, C_HEADS, QK)
    return jnp.concatenate(
        [w[:, 1536:3584], w[:, 5696:6720], w[:, 0:1536], q[:, :, :HEAD].reshape(k, C_HEADS * HEAD),
         q[:, :, HEAD:].reshape(k, C_HEADS * ROPE), w[:, 5120:5632], w[:, 5632:5696],
         jnp.zeros((k, PROJ_W - IN_WIDTH), w.dtype)], axis=1)


def _unpermute_in(g):
    k = g.shape[0]
    qn = g[:, O_M + M_QN:O_M + M_QR].reshape(k, C_HEADS, HEAD)
    qr = g[:, O_M + M_QR:O_M + M_CKV].reshape(k, C_HEADS, ROPE)
    q = jnp.concatenate([qn, qr], axis=-1).reshape(k, C_HEADS * QK)
    return jnp.concatenate(
        [g[:, O_A:O_A + W_A], g[:, O_B:O_B + W_B], q, g[:, O_M + M_CKV:O_M + M_KR],
         g[:, O_M + M_KR:O_M + M_KR + ROPE], g[:, O_CZ:O_CZ + W_CZ]], axis=1)


SMALL = ("attn_norm", "sgu_norm", "w_spatial", "b_spatial", "conv_b", "kv_norm", "q_nope_norm", "q_rope_norm",
         "k_nope_norm", "k_rope_norm", "out_norm", "ple_norm")
PACK_ROWS = 256


def _pack(tensors):
    flat = jnp.concatenate([t.reshape(-1) for t in tensors])
    rows = -(-flat.shape[0] // (128 * PACK_ROWS)) * PACK_ROWS
    return jnp.pad(flat, (0, rows * 128 - flat.shape[0])).reshape(rows, 128)


def _unpack(packed, like):
    flat = packed.reshape(-1)
    out, pos = [], 0
    for t in like:
        out.append(flat[pos:pos + t.size].reshape(t.shape))
        pos += t.size
    return out


def _tile(s_len, want):
    return min(want, s_len)


ATT_FWD_QUERIES = 512
ATT_FWD_KEYS = 8192
ATT_BWD_KEYS = 1024
ATT_BWD_QUERIES = 4096


def _layer_forward(h, p_l, cosf, sins, w, sm, comm, comm_rest):
    s_len = h.shape[0]
    tm = _tile(s_len, 512)
    proj, hn, rest = _norm_matmul(h, sm["attn_norm"], w["w_in"], _tile(s_len, 1024), 768, comm_rest)
    if comm_rest is not None:
        w = {**w, **_assemble_rest(rest)}
    ga, gb, gc = sm["out_norm"][:, 0:512], sm["out_norm"][:, 512:1024], sm["out_norm"][:, 1024:2048]
    y = _sgu_forward(proj, sm["sgu_norm"], sm["w_spatial"], sm["b_spatial"], ga, _tile(s_len, 256))
    y = _conv_forward(proj, w["conv_w"], sm["conv_b"], gb, y, _tile(s_len, 256))
    q, k, v, kt, vt = _mla_forward(proj, cosf, sins, w["w_ukv"], sm["kv_norm"], sm["q_nope_norm"], sm["q_rope_norm"],
                                   sm["k_nope_norm"], sm["k_rope_norm"], _tile(s_len, 256), _tile(s_len, ATT_BWD_KEYS),
                                   _tile(s_len, ATT_FWD_KEYS))
    o, lse, arrived = _attention_forward(q, k, vt, _tile(s_len, ATT_FWD_QUERIES), _tile(s_len, ATT_BWD_QUERIES), comm)
    y = _cgate_forward(o, proj, gc, y, _tile(s_len, 256))
    h1 = _out_matmul(h, y, w["w_out"], _tile(s_len, 1024), 1024)
    h2, n1, gate, pp = _ple_forward(h1, sm["ple_norm"], p_l, w["w_ple_gate"], w["w_ple_proj"], tm, 1024)
    saved = dict(h=h, hn=hn, proj=proj, y=y, q=q, k=k, v=v, kt=kt, o=o, lse=lse, h1=h1, n1=n1, gate=gate, pp=pp)
    return h2, saved, w, arrived


def _layer_backward(dh2, p_l, cosf, sins, w, sm, sv, comm, scatter_own):
    s_len = dh2.shape[0]
    tm = _tile(s_len, 512)
    tr = _tile(s_len, 256)
    big, small = {}, {}
    dh1, small["ple_norm"], dgp, dpp = _ple_backward(dh2, sv["gate"], sv["pp"], w["w_ple_gate"], sv["h1"], sm["ple_norm"], tm)
    big["w_ple_proj"], _ = _matmul_tn(p_l, dpp, _tile(s_len, 2048), PLE_DIM, 1024, "grad_w_ple_proj")
    big["w_ple_gate"], _ = _matmul_tn(sv["n1"], dgp, _tile(s_len, 2048), 1024, 1024, "grad_w_ple_gate")
    dy, _ = _matmul_nt(dh1, w["w_out"], _tile(s_len, 1024), 1024, "grad_branches")
    big["w_out"], _ = _matmul_tn(sv["y"], dh1, _tile(s_len, 2048), 1024, 1024, "grad_w_out")
    ga, gb, gc = sm["out_norm"][:, 0:512], sm["out_norm"][:, 512:1024], sm["out_norm"][:, 1024:2048]
    dproj, dcw, small["conv_b"], dgb = _conv_backward(sv["proj"], w["conv_w"], sm["conv_b"], gb, dy, tr)
    big["conv_w"] = dcw
    dproj, do, dsum, dgc = _cgate_backward(sv["o"], sv["proj"], gc, dy, dproj, tm, _tile(s_len, ATT_BWD_QUERIES))
    dqt, dk, dv, arrived = _attention_backward(sv["q"], sv["k"], sv["kt"], sv["v"], do, sv["lse"], dsum, comm)
    (dproj, big["w_ukv"], small["kv_norm"], small["q_nope_norm"], small["q_rope_norm"], small["k_nope_norm"],
     small["k_rope_norm"]) = _mla_backward(sv["proj"], cosf, sins, w["w_ukv"], sm["kv_norm"], sm["q_nope_norm"],
                                            sm["q_rope_norm"], sm["k_nope_norm"], sm["k_rope_norm"], dqt, dk, dv, dproj, tr)
    dproj, small["sgu_norm"], small["w_spatial"], small["b_spatial"], dga = _sgu_backward(
        sv["proj"], sm["sgu_norm"], sm["w_spatial"], sm["b_spatial"], ga, dy, dproj, tm)
    small["out_norm"] = jnp.concatenate([dga, dgb, dgc], axis=1)
    parts_rest = _parts_rest(big)
    g_in, arrived_rest = _matmul_tn(sv["hn"], dproj, _tile(s_len, 2048), 512, 2304, "grad_w_in",
                                    (parts_rest, [False] * len(parts_rest)) if scatter_own else None)
    parts = [_part_w_in(g_in)] + parts_rest
    dhn, arrived_in = _matmul_nt(dproj, w["w_in"], _tile(s_len, 1024), 256, "grad_attn_norm_in",
                                 (parts[:1], [False]) if scatter_own else None)
    dh, small["attn_norm"] = _rms_backward(sv["h"], sm["attn_norm"], dhn, dh1, tr, "attn_norm_backward")
    return dh, parts, big["conv_w"], small, arrived, (*arrived_in, *arrived_rest) if scatter_own else None


def _layer_small(params, layer):
    return dict(
        attn_norm=params["attn_norm"][layer][None, :], sgu_norm=params["sgu_norm"][layer],
        w_spatial=params["w_spatial"][layer], b_spatial=params["b_spatial"][layer][:, :, None],
        conv_b=params["conv_b"][layer][None, :], kv_norm=params["kv_norm"][layer][None, :],
        q_nope_norm=params["q_nope_norm"][layer][None, :], q_rope_norm=params["q_rope_norm"][layer][None, :],
        k_nope_norm=params["k_nope_norm"][layer][None, :], k_rope_norm=params["k_rope_norm"][layer][None, :],
        out_norm=params["out_norm"][layer][None, :], ple_norm=params["ple_norm"][layer][None, :])


BIG = ("w_in", "w_ukv", "w_out", "w_ple_gate", "w_ple_proj")


def _assemble_w_in(g_in):
    return _permute_in(g_in.transpose(1, 0, 2).reshape(g_in.shape[1], IN_WIDTH))


def _assemble_rest(gathered):
    g_ukv, g_out, g_gate, g_proj = gathered
    w_ukv = g_ukv.reshape(N_DEV, KV_RANK, 2, HEAD).transpose(1, 2, 0, 3).reshape(KV_RANK, 2 * C_HEADS * HEAD)
    return dict(w_ukv=w_ukv, w_out=g_out.reshape(D_MODEL, D_MODEL), w_ple_gate=g_gate.reshape(D_MODEL, D_MODEL),
                w_ple_proj=g_proj.transpose(1, 0, 2).reshape(PLE_DIM, D_MODEL))


def _part_w_in(g_in):
    return _unpermute_in(g_in).reshape(g_in.shape[0], N_DEV, -1).transpose(1, 2, 0).astype(BF16)


def _parts_rest(big):
    return [
        big["w_ukv"].reshape(KV_RANK, 2, N_DEV, HEAD).transpose(2, 0, 1, 3).reshape(N_DEV, KV_RANK, 2 * HEAD).astype(BF16),
        big["w_out"].reshape(N_DEV, -1, D_MODEL).astype(BF16),
        big["w_ple_gate"].reshape(N_DEV, -1, D_MODEL).astype(BF16),
        big["w_ple_proj"].reshape(PLE_DIM, N_DEV, -1).transpose(1, 0, 2).astype(BF16)]


def _step_local(xs, ps, pos, target, shards, conv_w, params):
    inv = 1.0 / (ROPE_BASE ** (jnp.arange(0, ROPE, 2, dtype=F32) / ROPE))
    ang = pos.astype(F32)[:, None] * inv
    cos, sin = jnp.cos(ang), jnp.sin(ang)
    cosf = jnp.concatenate([cos, cos], axis=-1)
    sins = jnp.concatenate([-sin, sin], axis=-1)

    def gather_of(names, layer):
        return [shards[n][layer] for n in names], [True] * len(names)

    h = xs
    saved, weights = [], []
    smalls = [_layer_small(params, layer) for layer in range(DEPTH)]
    (first_w_in,) = _exchange(*gather_of(BIG[:1], 0), "gather_first_w_in")
    w = dict(w_in=_assemble_w_in(first_w_in), conv_w=conv_w[0])
    for layer in range(DEPTH):
        comm = gather_of(BIG, layer + 1) if layer + 1 < DEPTH else None
        comm_rest = gather_of(BIG[1:], 0) if layer == 0 else None
        h, sv, w, arrived = _layer_forward(h, ps[layer], cosf, sins, w, smalls[layer], comm, comm_rest)
        saved.append(sv)
        weights.append(w)
        if comm is not None:
            w = dict(w_in=_assemble_w_in(arrived[0]), conv_w=conv_w[layer + 1], **_assemble_rest(arrived[1:]))
    dh, loss = _loss_grad(h, target, _tile(h.shape[0], 512))
    received, conv_grads, small_grads = [None] * DEPTH, [None] * DEPTH, [None] * DEPTH
    comm = None
    for layer in reversed(range(DEPTH)):
        dh, parts, conv_grads[layer], small_grads[layer], arrived, arrived_own = _layer_backward(
            dh, ps[layer], cosf, sins, weights[layer], smalls[layer], saved[layer], comm, layer == 0)
        if comm is not None:
            received[layer + 1] = arrived
        comm = (parts, [False] * len(parts))
    received[0] = arrived_own
    return loss, dh, received, conv_grads, small_grads


def kernel(x, p, positions, attn_norm, w_in, sgu_norm, w_spatial, b_spatial, conv_w, conv_b, kv_norm, w_ukv, q_nope_norm, q_rope_norm, k_nope_norm, k_rope_norm, out_norm, w_out, ple_norm, w_ple_gate, w_ple_proj, loss_target, m_attn_norm, m_w_in, m_sgu_norm, m_w_spatial, m_b_spatial, m_conv_w, m_conv_b, m_kv_norm, m_w_ukv, m_q_nope_norm, m_q_rope_norm, m_k_nope_norm, m_k_rope_norm, m_out_norm, m_w_out, m_ple_norm, m_w_ple_gate, m_w_ple_proj, v_attn_norm, v_w_in, v_sgu_norm, v_w_spatial, v_b_spatial, v_conv_w, v_conv_b, v_kv_norm, v_w_ukv, v_q_nope_norm, v_q_rope_norm, v_k_nope_norm, v_k_rope_norm, v_out_norm, v_w_out, v_ple_norm, v_w_ple_gate, v_w_ple_proj):
    order = ("attn_norm", "w_in", "sgu_norm", "w_spatial", "b_spatial", "conv_w", "conv_b", "kv_norm", "w_ukv",
             "q_nope_norm", "q_rope_norm", "k_nope_norm", "k_rope_norm", "out_norm", "w_out", "ple_norm", "w_ple_gate",
             "w_ple_proj")
    wts = dict(zip(order, (attn_norm, w_in, sgu_norm, w_spatial, b_spatial, conv_w, conv_b, kv_norm, w_ukv, q_nope_norm,
                           q_rope_norm, k_nope_norm, k_rope_norm, out_norm, w_out, ple_norm, w_ple_gate, w_ple_proj)))
    mom = dict(zip(order, (m_attn_norm, m_w_in, m_sgu_norm, m_w_spatial, m_b_spatial, m_conv_w, m_conv_b, m_kv_norm, m_w_ukv,
                           m_q_nope_norm, m_q_rope_norm, m_k_nope_norm, m_k_rope_norm, m_out_norm, m_w_out, m_ple_norm,
                           m_w_ple_gate, m_w_ple_proj)))
    var = dict(zip(order, (v_attn_norm, v_w_in, v_sgu_norm, v_w_spatial, v_b_spatial, v_conv_w, v_conv_b, v_kv_norm, v_w_ukv,
                           v_q_nope_norm, v_q_rope_norm, v_k_nope_norm, v_k_rope_norm, v_out_norm, v_w_out, v_ple_norm,
                           v_w_ple_gate, v_w_ple_proj)))

    conv_shard = wts["conv_w"]
    (conv_all,) = _exchange([conv_shard.reshape(-1, 128)], [True], "gather_conv_w")
    conv_full = conv_all.reshape(N_DEV, DEPTH, 3, -1).transpose(1, 2, 0, 3).reshape(DEPTH, 3, -1)
    shards = {n: wts[n].astype(BF16) for n in BIG}
    loss_part, grad_x, received, conv_grads, small_grads = _step_local(
        x[0], p[:, 0], positions[0], loss_target[0], shards, conv_full, wts)
    loss = lax.psum(loss_part[0, 0], ("x", "y", "c"))

    def small_grad(name):
        g = jnp.stack([sg[name] for sg in small_grads])
        return g.reshape(wts[name].shape)

    conv_grad = jnp.stack(conv_grads)
    like = [wts[n] for n in SMALL] + [conv_grad]
    packed = _pack([small_grad(n) for n in SMALL] + [conv_grad])
    (small_parts,) = _exchange([packed], [True], "gather_small_grads")
    filler = [jnp.zeros_like(conv_grad), jnp.zeros_like(conv_grad), jnp.ones_like(conv_grad)]
    small_out = _sum_adam(small_parts, *(_pack([src[n] for n in SMALL] + [fill])[None] for src, fill in zip((wts, mom, var), filler)),
                          0, None, "adam_small")
    unpacked = [_unpack(o[0], like) for o in small_out]
    results = {n: vals for n, vals in zip(SMALL, zip(*[u[:-1] for u in unpacked]))}
    me = 4 * lax.axis_index("x") + 2 * lax.axis_index("y") + lax.axis_index("c")
    width = conv_shard.shape[2]
    conv_local = lax.dynamic_slice_in_dim(unpacked[0][-1], me * width, width, axis=2)
    as_slab = (lambda t: t.reshape(1, -1, width))
    conv_out = _sum_adam(as_slab(conv_local), as_slab(conv_shard), as_slab(mom["conv_w"]), as_slab(var["conv_w"]), 0, None,
                         "adam_conv_w")
    results["conv_w"] = tuple(o.reshape(conv_shard.shape) for o in conv_out)

    for j, name in enumerate(BIG):
        view = (lambda t: jnp.swapaxes(t, 1, 2)) if name == "w_in" else (lambda t: t)
        outs = None
        for layer in range(DEPTH):
            outs = _sum_adam(received[layer][j], view(wts[name]), view(mom[name]), view(var[name]), layer, outs, "adam_" + name)
        results[name] = tuple(view(o) for o in outs)

    grads, deltas, new_m, new_v = ([results[n][j] for n in order] for j in range(4))
    return (loss, grad_x[None], *grads, *deltas, *new_m, *new_v)
```

```python
import functools

import jax
import jax.numpy as jnp
from jax import lax
from jax.experimental import pallas as pl
from jax.experimental.pallas import tpu as pltpu

F32 = jnp.float32
BF16 = jnp.bfloat16

N_DEV = 8
DEPTH = 4
D_MODEL = 2048
EPS = 1e-6
CHUNK = 128
A_HEADS = 4
HEAD = 128
ROPE = 64
HALF = ROPE // 2
C_HEADS = 8
KV_RANK = 512
PLE_DIM = 256
ROPE_BASE = 10000.0
IN_WIDTH = 6720
QK = HEAD + ROPE
SCALE = QK ** -0.5
LOG2_E = 1.4426950408889634
LN_2 = 0.6931471805599453
Q_SCALE = SCALE * LOG2_E
HALO = 8

O_B = 0
W_B = 2048
O_CZ = 2048
W_CZ = 1024
O_A = 3072
W_A = 1536
O_M = 4608
W_M = 2304
M_QN, M_QR, M_CKV, M_KR = 0, 1024, 1536, 2048
PROJ_W = 6912

ADAM_LR = 0.001
ADAM_B1 = 0.9
ADAM_B2 = 0.999
ADAM_EPS = 1e-08
ADAM_WD = 0.01
ADAM_STEP = 10

VMEM_LIMIT = 56 * 1024 * 1024

_NT = (((1,), (1,)), ((), ()))
_TN = (((0,), (0,)), ((), ()))


def _params(*sem):
    return pltpu.CompilerParams(dimension_semantics=sem, vmem_limit_bytes=VMEM_LIMIT)


@jax.custom_vjp
def _bdot(a, b):
    return jnp.dot(a.astype(BF16), b.astype(BF16), preferred_element_type=F32)


def _bdot_fwd(a, b):
    return _bdot(a, b), (a, b)


def _bdot_bwd(res, g):
    a, b = res
    gb = g.astype(BF16)
    da = lax.dot_general(gb, b.astype(BF16), _NT, preferred_element_type=F32)
    db = lax.dot_general(a.astype(BF16), gb, _TN, preferred_element_type=F32)
    return da.astype(a.dtype), db.astype(b.dtype)


_bdot.defvjp(_bdot_fwd, _bdot_bwd)


@functools.partial(jax.custom_vjp, nondiff_argnums=(1,))
def _split(x, n):
    w = x.shape[-1] // n
    return tuple(x[:, i * w:(i + 1) * w] for i in range(n))


def _split_fwd(x, n):
    return _split(x, n), None


def _split_bwd(n, _, gs):
    return (jnp.concatenate(gs, axis=-1),)


_split.defvjp(_split_fwd, _split_bwd)


@functools.partial(jax.custom_vjp, nondiff_argnums=(1,))
def _shift_rows(x, k):
    return pltpu.roll(x, k % x.shape[0], 0)


def _shift_rows_fwd(x, k):
    return _shift_rows(x, k), None


def _shift_rows_bwd(k, _, g):
    return (_shift_rows(g, -k),)


_shift_rows.defvjp(_shift_rows_fwd, _shift_rows_bwd)


@jax.custom_vjp
def _swap_halves(x):
    h = x.shape[-1] // 2
    return jnp.concatenate([x[:, h:], x[:, :h]], axis=-1)


def _swap_halves_fwd(x):
    return _swap_halves(x), None


def _swap_halves_bwd(_, g):
    return (_swap_halves(g),)


_swap_halves.defvjp(_swap_halves_fwd, _swap_halves_bwd)


def _rms(x, g):
    return x * lax.rsqrt(jnp.mean(x * x, axis=-1, keepdims=True) + EPS) * g


def _rope(x, cosf, sins):
    return x * cosf + _swap_halves(x) * sins


def _sgu_chunk(u, v, z, gain, ws, bs, ga):
    ys = []
    for h in range(A_HEADS):
        vn = _rms(v[h], gain[h])
        s = _bdot(ws[h], vn) + bs[h]
        ys.append(u[h] * s * jax.nn.silu(z[h]))
    ss = sum(jnp.sum(y * y, axis=-1, keepdims=True) for y in ys) * (1.0 / (A_HEADS * HEAD))
    r = lax.rsqrt(ss + EPS)
    return tuple(ys[h] * r * ga[h] for h in range(A_HEADS))


def _conv_tile(bb, bc, bh, bz, w0, w1, w2, cb, gb, w0h, w1h, w2h, cbh, gbh, valid, core):
    t = jnp.where(valid, bc * bh, 0.0)
    y = (jnp.where(core, cb, cbh)
         + _shift_rows(t, 1) * jnp.where(core, w0, w0h)
         + t * jnp.where(core, w1, w1h)
         + _shift_rows(t, -1) * jnp.where(core, w2, w2h))
    return _rms(bb * y * jax.nn.silu(bz), jnp.where(core, gb, gbh))


def _cgate_tile(o, cz, gc):
    return _rms(o * jax.nn.silu(cz), gc)


def _mla_tile(qn, qr, ckv, kr, cosf, sins, wukv, kvg, qng, qrg, kng, krg):
    kv = _split(_bdot(_rms(ckv, kvg), wukv), 2 * C_HEADS)
    k_r = _rope(_rms(kr, krg), cosf, sins)
    qn_h = _split(qn, C_HEADS)
    qr_h = _split(qr, C_HEADS)
    q, k, v = [], [], []
    for h in range(C_HEADS):
        q.append(jnp.concatenate([_rms(qn_h[h], qng), _rope(_rms(qr_h[h], qrg), cosf, sins)], axis=-1) * Q_SCALE)
        k.append(jnp.concatenate([_rms(kv[h], kng), k_r], axis=-1))
        v.append(kv[C_HEADS + h])
    return tuple(q), tuple(k), tuple(v)


def _norm_matmul(h, gain, w, tm, tn, comm=None):
    s_len, k = h.shape
    n = w.shape[1]
    c_ops, c_in_specs, c_shapes, c_out_specs, c_sems, c_begin, c_end = _riding_exchange(comm, 2)
    n_c = len(c_ops)

    def body(h_ref, g_ref, w_ref, *rest):
        c_ins, (o_ref, hn_ref), c_outs, sems = rest[:n_c], rest[n_c:n_c + 2], rest[n_c + 2:2 * n_c + 2], rest[2 * n_c + 2:]
        c_begin(c_ins, c_outs, sems)

        @pl.when(pl.program_id(1) == 0)
        def _():
            hn_ref[...] = _rms(h_ref[...], g_ref[...]).astype(BF16)

        o_ref[...] = jnp.dot(hn_ref[...], w_ref[...], preferred_element_type=F32).astype(BF16)
        c_end(c_ins, c_outs, sems)

    outs = pl.pallas_call(
        body, name="norm_matmul",
        out_shape=(jax.ShapeDtypeStruct((s_len, n), BF16), jax.ShapeDtypeStruct((s_len, k), BF16), *c_shapes),
        grid=(s_len // tm, n // tn),
        in_specs=[pl.BlockSpec((tm, k), lambda i, j: (i, 0)), pl.BlockSpec((1, k), lambda i, j: (0, 0)),
                  pl.BlockSpec((k, tn), lambda i, j: (0, j))] + c_in_specs,
        out_specs=(pl.BlockSpec((tm, tn), lambda i, j: (i, j)), pl.BlockSpec((tm, k), lambda i, j: (i, 0)), *c_out_specs),
        scratch_shapes=c_sems,
        compiler_params=_params("arbitrary", "arbitrary") if comm else _params("parallel", "arbitrary"),
    )(h, gain, w, *c_ops)
    return outs[0], outs[1], outs[2:]


def _out_matmul(h, y, w, tm, tn):
    s_len, n = h.shape
    k = y.shape[1]

    def body(h_ref, y_ref, w_ref, o_ref):
        o_ref[...] = h_ref[...] + jnp.dot(y_ref[...], w_ref[...], preferred_element_type=F32)

    return pl.pallas_call(
        body, name="out_matmul",
        out_shape=jax.ShapeDtypeStruct((s_len, n), F32),
        grid=(s_len // tm, n // tn),
        in_specs=[pl.BlockSpec((tm, tn), lambda i, j: (i, j)), pl.BlockSpec((tm, k), lambda i, j: (i, 0)),
                  pl.BlockSpec((k, tn), lambda i, j: (0, j))],
        out_specs=pl.BlockSpec((tm, tn), lambda i, j: (i, j)),
        compiler_params=_params("parallel", "parallel"),
    )(h, y, w)


def _ple_forward(h1, gain, p, wg, wp, tm, tn):
    s_len, d = h1.shape
    kp = p.shape[1]

    def body(hrow_ref, g_ref, p_ref, wg_ref, wp_ref, o_ref, n1_ref, gate_ref, pp_ref):
        j = pl.program_id(1)

        @pl.when(j == 0)
        def _():
            n1_ref[...] = _rms(hrow_ref[...], g_ref[...]).astype(BF16)

        gate = jax.nn.sigmoid(jnp.dot(n1_ref[...], wg_ref[...], preferred_element_type=F32))
        pp = jnp.dot(p_ref[...].astype(BF16), wp_ref[...], preferred_element_type=F32)
        o_ref[...] = hrow_ref[:, pl.ds(pl.multiple_of(j * tn, tn), tn)] + gate * pp
        gate_ref[...] = gate.astype(BF16)
        pp_ref[...] = pp.astype(BF16)

    col = pl.BlockSpec((tm, tn), lambda i, j: (i, j))
    return pl.pallas_call(
        body, name="ple_forward",
        out_shape=(jax.ShapeDtypeStruct((s_len, d), F32), jax.ShapeDtypeStruct((s_len, d), BF16),
                   jax.ShapeDtypeStruct((s_len, d), BF16), jax.ShapeDtypeStruct((s_len, d), BF16)),
        grid=(s_len // tm, d // tn),
        in_specs=[pl.BlockSpec((tm, d), lambda i, j: (i, 0)), pl.BlockSpec((1, d), lambda i, j: (0, 0)),
                  pl.BlockSpec((tm, kp), lambda i, j: (i, 0)), pl.BlockSpec((d, tn), lambda i, j: (0, j)),
                  pl.BlockSpec((kp, tn), lambda i, j: (0, j))],
        out_specs=(col, pl.BlockSpec((tm, d), lambda i, j: (i, 0)), col, col),
        compiler_params=_params("parallel", "arbitrary"),
    )(h1, gain, p, wg, wp)


def _matmul_nt(a, b, tm, tk, name, comm=None):
    m, n = a.shape
    k = b.shape[0]
    c_ops, c_in_specs, c_shapes, c_out_specs, c_sems, c_begin, c_end = _riding_exchange(comm, 2)
    n_c = len(c_ops)

    def body(a_ref, b_ref, *rest):
        c_ins, o_ref, c_outs, sems = rest[:n_c], rest[n_c], rest[n_c + 1:2 * n_c + 1], rest[2 * n_c + 1:]
        c_begin(c_ins, c_outs, sems)
        o_ref[...] = lax.dot_general(a_ref[...].astype(BF16), b_ref[...].astype(BF16), _NT, preferred_element_type=F32)
        c_end(c_ins, c_outs, sems)

    outs = pl.pallas_call(
        body, name=name,
        out_shape=(jax.ShapeDtypeStruct((m, k), F32), *c_shapes),
        grid=(m // tm, k // tk),
        in_specs=[pl.BlockSpec((tm, n), lambda i, j: (i, 0)), pl.BlockSpec((tk, n), lambda i, j: (j, 0))] + c_in_specs,
        out_specs=(pl.BlockSpec((tm, tk), lambda i, j: (i, j)), *c_out_specs),
        scratch_shapes=c_sems,
        compiler_params=_params("arbitrary", "arbitrary") if comm else _params("parallel", "parallel"),
    )(a, b, *c_ops)
    return outs[0], outs[1:]


def _matmul_tn(a, b, tm, tk, tn, name, comm=None):
    m, k = a.shape
    n = b.shape[1]
    n_m = m // tm
    c_ops, c_in_specs, c_shapes, c_out_specs, c_sems, c_begin, c_end = _riding_exchange(comm, 3)
    n_c = len(c_ops)

    def body(a_ref, b_ref, *rest):
        c_ins, o_ref, c_outs, acc_ref, sems = rest[:n_c], rest[n_c], rest[n_c + 1:2 * n_c + 1], rest[2 * n_c + 1], rest[2 * n_c + 2:]
        c_begin(c_ins, c_outs, sems)
        part = lax.dot_general(a_ref[...].astype(BF16), b_ref[...].astype(BF16), _TN, preferred_element_type=F32)
        _acc(acc_ref, part, pl.program_id(2) == 0)

        @pl.when(pl.program_id(2) == n_m - 1)
        def _():
            o_ref[...] = acc_ref[...].astype(BF16)

        c_end(c_ins, c_outs, sems)

    outs = pl.pallas_call(
        body, name=name,
        out_shape=(jax.ShapeDtypeStruct((k, n), BF16), *c_shapes),
        grid=(k // tk, n // tn, n_m),
        in_specs=[pl.BlockSpec((tm, tk), lambda kk, nn, mm: (mm, kk)), pl.BlockSpec((tm, tn), lambda kk, nn, mm: (mm, nn))]
                 + c_in_specs,
        out_specs=(pl.BlockSpec((tk, tn), lambda kk, nn, mm: (kk, nn)), *c_out_specs),
        scratch_shapes=[pltpu.VMEM((tk, tn), F32)] + c_sems,
        compiler_params=_params("arbitrary", "arbitrary", "arbitrary") if comm else _params("parallel", "parallel", "arbitrary"),
    )(a, b, *c_ops)
    return outs[0], outs[1:]


def _acc(ref, val, first):
    @pl.when(first)
    def _():
        ref[...] = val

    @pl.when(jnp.logical_not(first))
    def _():
        ref[...] += val


def _loss_grad(h, target, tm):
    s_len, d = h.shape

    def body(h_ref, t_ref, dh_ref, loss_ref):
        e = h_ref[...] - t_ref[...]
        dh_ref[...] = e * (1.0 / d)
        part = jnp.sum(jnp.sum(e * e, axis=-1, keepdims=True), axis=0, keepdims=True) * (0.5 / d)
        _acc(loss_ref, jnp.broadcast_to(part, loss_ref.shape), pl.program_id(0) == 0)

    row = pl.BlockSpec((tm, d), lambda i: (i, 0))
    return pl.pallas_call(
        body, name="loss_grad",
        out_shape=(jax.ShapeDtypeStruct((s_len, d), F32), jax.ShapeDtypeStruct((1, 128), F32)),
        grid=(s_len // tm,),
        in_specs=[row, row],
        out_specs=(row, pl.BlockSpec((1, 128), lambda i: (0, 0))),
        compiler_params=_params("arbitrary"),
    )(h, target)


def _rms_backward(x, gain, dn, dres, tm, name):
    s_len, d = x.shape

    def body(x_ref, g_ref, dn_ref, dres_ref, dx_ref, dg_ref):
        _, vjp = jax.vjp(_rms, x_ref[...], g_ref[...])
        dx, dg = vjp(dn_ref[...])
        dx_ref[...] = dres_ref[...] + dx
        _acc(dg_ref, dg, pl.program_id(0) == 0)

    row = pl.BlockSpec((tm, d), lambda i: (i, 0))
    vec = pl.BlockSpec((1, d), lambda i: (0, 0))
    return pl.pallas_call(
        body, name=name,
        out_shape=(jax.ShapeDtypeStruct((s_len, d), F32), jax.ShapeDtypeStruct((1, d), F32)),
        grid=(s_len // tm,),
        in_specs=[row, vec, row, row],
        out_specs=(row, vec),
        compiler_params=_params("arbitrary"),
    )(x, gain, dn, dres)


def _ple_backward(dh2, gate, pp, wg, h1, gain, tm):
    s_len, d = dh2.shape

    def body(dh_ref, gate_ref, pp_ref, wg_ref, h1_ref, g_ref, dh1_ref, dgain_ref, dgp_ref, dpp_ref):
        dh = dh_ref[...]
        gate = gate_ref[...].astype(F32)
        dgp = (dh * pp_ref[...].astype(F32) * gate * (1.0 - gate)).astype(BF16)
        dgp_ref[...] = dgp
        dpp_ref[...] = (dh * gate).astype(BF16)
        dn = lax.dot_general(dgp, wg_ref[...], _NT, preferred_element_type=F32)
        _, vjp = jax.vjp(_rms, h1_ref[...], g_ref[...])
        dx, dgain = vjp(dn)
        dh1_ref[...] = dh + dx
        _acc(dgain_ref, dgain, pl.program_id(0) == 0)

    row = pl.BlockSpec((tm, d), lambda i: (i, 0))
    vec = pl.BlockSpec((1, d), lambda i: (0, 0))
    return pl.pallas_call(
        body, name="ple_backward",
        out_shape=(jax.ShapeDtypeStruct((s_len, d), F32), jax.ShapeDtypeStruct((1, d), F32),
                   jax.ShapeDtypeStruct((s_len, d), BF16), jax.ShapeDtypeStruct((s_len, d), BF16)),
        grid=(s_len // tm,),
        in_specs=[row, row, row, pl.BlockSpec((d, d), lambda i: (0, 0)), row, vec],
        out_specs=(row, vec, row, row),
        compiler_params=_params("arbitrary"),
    )(dh2, gate, pp, wg, h1, gain)


def _sgu_in_specs(tm):
    return [pl.BlockSpec((tm, W_A), lambda i: (i, O_A // W_A)),
            pl.BlockSpec((A_HEADS, HEAD), lambda i: (0, 0)), pl.BlockSpec((A_HEADS, CHUNK, CHUNK), lambda i: (0, 0, 0)),
            pl.BlockSpec((A_HEADS, CHUNK, 1), lambda i: (0, 0, 0)), pl.BlockSpec((1, 512), lambda i: (0, 0))]


def _sgu_load(a_ref, gain_ref, ws_ref, bs_ref, ga_ref, c):
    rows = slice(c * CHUNK, (c + 1) * CHUNK)
    heads = range(A_HEADS)
    u = tuple(a_ref[rows, h * HEAD:(h + 1) * HEAD].astype(F32) for h in heads)
    v = tuple(a_ref[rows, 512 + h * HEAD:512 + (h + 1) * HEAD].astype(F32) for h in heads)
    z = tuple(a_ref[rows, 1024 + h * HEAD:1024 + (h + 1) * HEAD].astype(F32) for h in heads)
    gain = tuple(gain_ref[h:h + 1, :] for h in heads)
    ws = tuple(ws_ref[h] for h in heads)
    bs = tuple(bs_ref[h] for h in heads)
    ga = tuple(ga_ref[:, h * HEAD:(h + 1) * HEAD] for h in heads)
    return u, v, z, gain, ws, bs, ga


def _sgu_forward(proj, gain, ws, bs, ga, tm):
    s_len = proj.shape[0]

    def body(a_ref, gain_ref, ws_ref, bs_ref, ga_ref, o_ref):
        for c in range(tm // CHUNK):
            out = _sgu_chunk(*_sgu_load(a_ref, gain_ref, ws_ref, bs_ref, ga_ref, c))
            for h in range(A_HEADS):
                o_ref[c * CHUNK:(c + 1) * CHUNK, h * HEAD:(h + 1) * HEAD] = out[h].astype(BF16)

    return pl.pallas_call(
        body, name="sgu_forward",
        out_shape=jax.ShapeDtypeStruct((s_len, D_MODEL), BF16),
        grid=(s_len // tm,),
        in_specs=_sgu_in_specs(tm),
        out_specs=pl.BlockSpec((tm, 512), lambda i: (i, 0)),
        compiler_params=_params("parallel"),
    )(proj, gain, ws, bs, ga)


def _sgu_backward(proj, gain, ws, bs, ga, dy, dproj, tm):
    s_len = proj.shape[0]

    def body(a_ref, gain_ref, ws_ref, bs_ref, ga_ref, dy_ref, _, da_ref, dgain_ref, dws_ref, dbs_ref, dga_ref):
        tot = None
        for c in range(tm // CHUNK):
            args = _sgu_load(a_ref, gain_ref, ws_ref, bs_ref, ga_ref, c)
            _, vjp = jax.vjp(_sgu_chunk, *args)
            rows = slice(c * CHUNK, (c + 1) * CHUNK)
            du, dv, dz, dgain, dws, dbs, dga = vjp(tuple(dy_ref[rows, h * HEAD:(h + 1) * HEAD] for h in range(A_HEADS)))
            for h in range(A_HEADS):
                da_ref[rows, h * HEAD:(h + 1) * HEAD] = du[h].astype(BF16)
                da_ref[rows, 512 + h * HEAD:512 + (h + 1) * HEAD] = dv[h].astype(BF16)
                da_ref[rows, 1024 + h * HEAD:1024 + (h + 1) * HEAD] = dz[h].astype(BF16)
            part = (dgain, dws, dbs, dga)
            tot = part if tot is None else jax.tree.map(jnp.add, tot, part)
        dgain, dws, dbs, dga = tot
        first = pl.program_id(0) == 0
        _acc(dgain_ref, jnp.concatenate(dgain, axis=0), first)
        _acc(dga_ref, jnp.concatenate(dga, axis=-1), first)
        for h in range(A_HEADS):
            _acc(dws_ref.at[h], dws[h], first)
            _acc(dbs_ref.at[h], dbs[h], first)

    small = [pl.BlockSpec((A_HEADS, HEAD), lambda i: (0, 0)), pl.BlockSpec((A_HEADS, CHUNK, CHUNK), lambda i: (0, 0, 0)),
             pl.BlockSpec((A_HEADS, CHUNK, 1), lambda i: (0, 0, 0)), pl.BlockSpec((1, 512), lambda i: (0, 0))]
    return pl.pallas_call(
        body, name="sgu_backward",
        out_shape=(jax.ShapeDtypeStruct(dproj.shape, BF16),
                   jax.ShapeDtypeStruct((A_HEADS, HEAD), F32), jax.ShapeDtypeStruct((A_HEADS, CHUNK, CHUNK), F32),
                   jax.ShapeDtypeStruct((A_HEADS, CHUNK, 1), F32), jax.ShapeDtypeStruct((1, 512), F32)),
        grid=(s_len // tm,),
        in_specs=_sgu_in_specs(tm) + [pl.BlockSpec((tm, 512), lambda i: (i, 0)), pl.BlockSpec(memory_space=pl.ANY)],
        out_specs=(pl.BlockSpec((tm, W_A), lambda i: (i, O_A // W_A)), *small),
        input_output_aliases={6: 0},
        compiler_params=_params("arbitrary"),
    )(proj, gain, ws, bs, ga, dy, dproj)


def _halo_specs(tm, width, col, n_rows):
    per = tm // HALO
    last = n_rows // HALO - 1
    return [pl.BlockSpec((HALO, width), lambda i: (jnp.maximum(i * per - 1, 0), col)),
            pl.BlockSpec((tm, width), lambda i: (i, col)),
            pl.BlockSpec((HALO, width), lambda i: (jnp.minimum((i + 1) * per, last), col))]


def _conv_masks(tm, s_len):
    r = lax.broadcasted_iota(jnp.int32, (tm + 2 * HALO, 1), 0)
    g = pl.program_id(0) * tm - HALO + r
    return (g >= 0) & (g < s_len), (r >= HALO) & (r < HALO + tm)


def _conv_inputs(b_refs, cw_ref, cb_ref, gb_ref):
    ext = jnp.concatenate([r[...] for r in b_refs], axis=0).astype(F32)
    bb, bc, bh, bz = (ext[:, j * 512:(j + 1) * 512] for j in range(4))
    prm = (cw_ref[0:1, :], cw_ref[1:2, :], cw_ref[2:3, :], cb_ref[...], gb_ref[...])
    return (bb, bc, bh, bz), prm


def _conv_forward(proj, cw, cb, gb, y, tm):
    s_len = proj.shape[0]

    def body(p0, p1, p2, cw_ref, cb_ref, gb_ref, _, o_ref):
        acts, prm = _conv_inputs((p0, p1, p2), cw_ref, cb_ref, gb_ref)
        valid, core = _conv_masks(tm, s_len)
        out = _conv_tile(*acts, *prm, *prm, valid, core)
        o_ref[...] = out[HALO:HALO + tm].astype(BF16)

    vec = pl.BlockSpec((1, 512), lambda i: (0, 0))
    return pl.pallas_call(
        body, name="conv_forward",
        out_shape=jax.ShapeDtypeStruct(y.shape, BF16),
        grid=(s_len // tm,),
        in_specs=_halo_specs(tm, W_B, O_B // W_B, s_len) + [pl.BlockSpec((3, 512), lambda i: (0, 0)), vec, vec,
                                                             pl.BlockSpec(memory_space=pl.ANY)],
        out_specs=pl.BlockSpec((tm, 512), lambda i: (i, 1)),
        input_output_aliases={6: 0},
        compiler_params=_params("parallel"),
    )(proj, proj, proj, cw, cb, gb, y)


def _conv_backward(proj, cw, cb, gb, dy, tm):
    s_len = proj.shape[0]

    def body(p0, p1, p2, cw_ref, cb_ref, gb_ref, d0, d1, d2, db_ref, dcw_ref, dcb_ref, dgb_ref):
        acts, prm = _conv_inputs((p0, p1, p2), cw_ref, cb_ref, gb_ref)
        valid, core = _conv_masks(tm, s_len)
        _, vjp = jax.vjp(lambda a, p: _conv_tile(*a, *p, *prm, valid, core), acts, prm)
        dy_ext = jnp.where(valid, jnp.concatenate([d0[...], d1[...], d2[...]], axis=0), 0.0)
        dacts, dprm = vjp(dy_ext)
        for j in range(4):
            db_ref[:, j * 512:(j + 1) * 512] = dacts[j][HALO:HALO + tm].astype(BF16)
        first = pl.program_id(0) == 0
        _acc(dcw_ref, jnp.concatenate(dprm[0:3], axis=0), first)
        _acc(dcb_ref, dprm[3], first)
        _acc(dgb_ref, dprm[4], first)

    vec = pl.BlockSpec((1, 512), lambda i: (0, 0))
    mat = pl.BlockSpec((3, 512), lambda i: (0, 0))
    return pl.pallas_call(
        body, name="conv_backward",
        out_shape=(jax.ShapeDtypeStruct((s_len, PROJ_W), BF16), jax.ShapeDtypeStruct((3, 512), F32),
                   jax.ShapeDtypeStruct((1, 512), F32), jax.ShapeDtypeStruct((1, 512), F32)),
        grid=(s_len // tm,),
        in_specs=_halo_specs(tm, W_B, O_B // W_B, s_len) + [mat, vec, vec] + _halo_specs(tm, 512, 1, s_len),
        out_specs=(pl.BlockSpec((tm, W_B), lambda i: (i, O_B // W_B)), mat, vec, vec),
        compiler_params=_params("arbitrary"),
    )(proj, proj, proj, cw, cb, gb, dy, dy, dy)


def _cgate_forward(o, proj, gc, y, tm):
    s_len = o.shape[0]

    def body(o_ref, cz_ref, gc_ref, _, y_ref):
        y_ref[...] = _cgate_tile(o_ref[...], cz_ref[...].astype(F32), gc_ref[...]).astype(BF16)

    return pl.pallas_call(
        body, name="cgate_forward",
        out_shape=jax.ShapeDtypeStruct(y.shape, BF16),
        grid=(s_len // tm,),
        in_specs=[pl.BlockSpec((tm, W_CZ), lambda i: (i, 0)), pl.BlockSpec((tm, W_CZ), lambda i: (i, O_CZ // W_CZ)),
                  pl.BlockSpec((1, W_CZ), lambda i: (0, 0)), pl.BlockSpec(memory_space=pl.ANY)],
        out_specs=pl.BlockSpec((tm, W_CZ), lambda i: (i, 1)),
        input_output_aliases={3: 0},
        compiler_params=_params("parallel"),
    )(o, proj, gc, y)


def _cgate_backward(o, proj, gc, dy, dproj, tm, stat_chunk):
    s_len = o.shape[0]
    per_stat = stat_chunk // tm

    def body(o_ref, cz_ref, gc_ref, dy_ref, _, dcz_ref, do_ref, dsum_ref, dgc_ref):
        o = o_ref[...]
        _, vjp = jax.vjp(_cgate_tile, o, cz_ref[...].astype(F32), gc_ref[...])
        do, dcz, dgc = vjp(dy_ref[...])
        dcz_ref[...] = dcz.astype(BF16)
        do_ref[...] = do.astype(BF16)
        ones = jnp.ones((8, HEAD), F32)
        for h in range(C_HEADS):
            cols = slice(h * HEAD, (h + 1) * HEAD)
            sums = lax.dot_general(ones, do[:, cols] * o[:, cols], _NT, precision=lax.Precision.HIGHEST,
                                   preferred_element_type=F32)
            dsum_ref[h, 0] = sums[0:1]
        _acc(dgc_ref, dgc, pl.program_id(0) == 0)

    row = pl.BlockSpec((tm, W_CZ), lambda i: (i, 0))
    vec = pl.BlockSpec((1, W_CZ), lambda i: (0, 0))
    return pl.pallas_call(
        body, name="cgate_backward",
        out_shape=(jax.ShapeDtypeStruct(dproj.shape, BF16), jax.ShapeDtypeStruct((s_len, W_CZ), BF16),
                   jax.ShapeDtypeStruct((C_HEADS, s_len // stat_chunk, 1, stat_chunk), F32), jax.ShapeDtypeStruct((1, W_CZ), F32)),
        grid=(s_len // tm,),
        in_specs=[row, pl.BlockSpec((tm, W_CZ), lambda i: (i, O_CZ // W_CZ)), vec,
                  pl.BlockSpec((tm, W_CZ), lambda i: (i, 1)), pl.BlockSpec(memory_space=pl.ANY)],
        out_specs=(pl.BlockSpec((tm, W_CZ), lambda i: (i, O_CZ // W_CZ)), row,
                   pl.BlockSpec((C_HEADS, 1, 1, tm), lambda i: (0, i // per_stat, 0, i % per_stat)), vec),
        input_output_aliases={4: 0},
        compiler_params=_params("arbitrary"),
    )(o, proj, gc, dy, dproj)


def _mla_small_specs():
    return [pl.BlockSpec((KV_RANK, 2 * C_HEADS * HEAD), lambda i: (0, 0)), pl.BlockSpec((1, KV_RANK), lambda i: (0, 0)),
            pl.BlockSpec((1, HEAD), lambda i: (0, 0)), pl.BlockSpec((1, ROPE), lambda i: (0, 0)),
            pl.BlockSpec((1, HEAD), lambda i: (0, 0)), pl.BlockSpec((1, ROPE), lambda i: (0, 0))]


def _mla_load(m_ref, cos_ref, sin_ref):
    qn = m_ref[:, M_QN:M_QN + C_HEADS * HEAD].astype(F32)
    qr = m_ref[:, M_QR:M_QR + C_HEADS * ROPE].astype(F32)
    ckv = m_ref[:, M_CKV:M_CKV + KV_RANK].astype(F32)
    kr = m_ref[:, M_KR:M_KR + ROPE].astype(F32)
    return qn, qr, ckv, kr, cos_ref[...], sin_ref[...]


def _mla_forward(proj, cosf, sins, wukv, kvg, qng, qrg, kng, krg, tm, kt_chunk, vt_chunk):
    s_len = proj.shape[0]

    def body(m_ref, cos_ref, sin_ref, w_ref, kvg_ref, qng_ref, qrg_ref, kng_ref, krg_ref, q_ref, k_ref, v_ref, kt_ref, vt_ref):
        q, k, v = _mla_tile(*_mla_load(m_ref, cos_ref, sin_ref), w_ref[...], kvg_ref[...], qng_ref[...], qrg_ref[...],
                            kng_ref[...], krg_ref[...])
        for h in range(C_HEADS):
            q_ref[h] = q[h].astype(BF16)
            k_ref[h] = k[h].astype(BF16)
            v_ref[h] = v[h].astype(BF16)
            kt_ref[h, 0] = jnp.concatenate([k[h][:, :HEAD].T, k[h][:, HEAD:].T], axis=0).astype(BF16)
            vt_ref[h, 0] = v[h].T.astype(BF16)

    rope_spec = pl.BlockSpec((tm, ROPE), lambda i: (i, 0))
    qk_spec = pl.BlockSpec((C_HEADS, tm, QK), lambda i: (0, i, 0))
    per_k, per_v = kt_chunk // tm, vt_chunk // tm
    return pl.pallas_call(
        body, name="mla_forward",
        out_shape=(jax.ShapeDtypeStruct((C_HEADS, s_len, QK), BF16), jax.ShapeDtypeStruct((C_HEADS, s_len, QK), BF16),
                   jax.ShapeDtypeStruct((C_HEADS, s_len, HEAD), BF16),
                   jax.ShapeDtypeStruct((C_HEADS, s_len // kt_chunk, QK, kt_chunk), BF16),
                   jax.ShapeDtypeStruct((C_HEADS, s_len // vt_chunk, HEAD, vt_chunk), BF16)),
        grid=(s_len // tm,),
        in_specs=[pl.BlockSpec((tm, W_M), lambda i: (i, O_M // W_M)), rope_spec, rope_spec] + _mla_small_specs(),
        out_specs=(qk_spec, qk_spec, pl.BlockSpec((C_HEADS, tm, HEAD), lambda i: (0, i, 0)),
                   pl.BlockSpec((C_HEADS, 1, QK, tm), lambda i: (0, i // per_k, 0, i % per_k)),
                   pl.BlockSpec((C_HEADS, 1, HEAD, tm), lambda i: (0, i // per_v, 0, i % per_v))),
        compiler_params=_params("parallel"),
    )(proj, cosf, sins, wukv, kvg, qng, qrg, kng, krg)


def _mla_backward(proj, cosf, sins, wukv, kvg, qng, qrg, kng, krg, dqt, dk, dv, dproj, tm):
    s_len = proj.shape[0]
    per_chunk = dqt.shape[3] // tm

    def body(m_ref, cos_ref, sin_ref, w_ref, kvg_ref, qng_ref, qrg_ref, kng_ref, krg_ref, dq_ref, dk_ref, dv_ref, _,
             dm_ref, dw_ref, dkvg_ref, dqng_ref, dqrg_ref, dkng_ref, dkrg_ref):
        qn, qr, ckv, kr, cosf_t, sins_t = _mla_load(m_ref, cos_ref, sin_ref)
        prm = (w_ref[...], kvg_ref[...], qng_ref[...], qrg_ref[...], kng_ref[...], krg_ref[...])
        _, vjp = jax.vjp(lambda a, p: _mla_tile(*a, cosf_t, sins_t, *p), (qn, qr, ckv, kr), prm)
        heads = range(C_HEADS)
        dacts, dprm = vjp((tuple(dq_ref[h, 0].T for h in heads), tuple(dk_ref[h] for h in heads), tuple(dv_ref[h] for h in heads)))
        dm_ref[:, M_QN:M_QN + C_HEADS * HEAD] = dacts[0].astype(BF16)
        dm_ref[:, M_QR:M_QR + C_HEADS * ROPE] = dacts[1].astype(BF16)
        dm_ref[:, M_CKV:M_CKV + KV_RANK] = dacts[2].astype(BF16)
        pad = jnp.zeros((tm, W_M - M_KR - ROPE), F32)
        dm_ref[:, M_KR:W_M] = jnp.concatenate([dacts[3], pad], axis=-1).astype(BF16)
        first = pl.program_id(0) == 0
        for ref, val in zip((dw_ref, dkvg_ref, dqng_ref, dqrg_ref, dkng_ref, dkrg_ref), dprm):
            _acc(ref, val.astype(F32), first)

    rope_spec = pl.BlockSpec((tm, ROPE), lambda i: (i, 0))
    qk_spec = pl.BlockSpec((C_HEADS, tm, QK), lambda i: (0, i, 0))
    small = _mla_small_specs()
    return pl.pallas_call(
        body, name="mla_backward",
        out_shape=(jax.ShapeDtypeStruct(dproj.shape, BF16), jax.ShapeDtypeStruct((KV_RANK, 2 * C_HEADS * HEAD), F32),
                   jax.ShapeDtypeStruct((1, KV_RANK), F32), jax.ShapeDtypeStruct((1, HEAD), F32),
                   jax.ShapeDtypeStruct((1, ROPE), F32), jax.ShapeDtypeStruct((1, HEAD), F32),
                   jax.ShapeDtypeStruct((1, ROPE), F32)),
        grid=(s_len // tm,),
        in_specs=[pl.BlockSpec((tm, W_M), lambda i: (i, O_M // W_M)), rope_spec, rope_spec] + small
                 + [pl.BlockSpec((C_HEADS, 1, QK, tm), lambda i: (0, i // per_chunk, 0, i % per_chunk)), qk_spec,
                    pl.BlockSpec((C_HEADS, tm, HEAD), lambda i: (0, i, 0)), pl.BlockSpec(memory_space=pl.ANY)],
        out_specs=(pl.BlockSpec((tm, W_M), lambda i: (i, O_M // W_M)), *small),
        input_output_aliases={12: 0},
        compiler_params=_params("arbitrary"),
    )(proj, cosf, sins, wukv, kvg, qng, qrg, kng, krg, dqt, dk, dv, dproj)


def _attention_forward(q, k, vt, tq, stat_chunk, comm=None):
    n_heads, s_len, _ = q.shape
    n_chunks, _, ck = vt.shape[1:]
    c_ops, c_in_specs, c_shapes, c_out_specs, c_sems, c_begin, c_end = _riding_exchange(comm, 2)
    n_c = len(c_ops)

    def body(q_ref, k_ref, vt_ref, *rest):
        c_ins, (o_ref, lse_ref), c_outs, sems = rest[:n_c], rest[n_c:n_c + 2], rest[n_c + 2:2 * n_c + 2], rest[2 * n_c + 2:]
        c_begin(c_ins, c_outs, sems)
        q_t = q_ref[0]

        def step(j, carry):
            m_old, l_old, acc = carry
            k_j = k_ref[0, pl.ds(pl.multiple_of(j * ck, ck), ck), :]
            s = lax.dot_general(k_j, q_t, _NT, preferred_element_type=F32)
            m_new = jnp.maximum(m_old, jnp.max(s, axis=0, keepdims=True))
            p = jnp.exp2(s - m_new)
            alpha = jnp.exp2(m_old - m_new)
            l_new = alpha * l_old + jnp.sum(p, axis=0, keepdims=True)
            acc = alpha * acc + jnp.dot(vt_ref[0, j], p.astype(BF16), preferred_element_type=F32)
            return m_new, l_new, acc

        init = (jnp.full((1, tq), -jnp.inf, F32), jnp.zeros((1, tq), F32), jnp.zeros((HEAD, tq), F32))
        m_fin, l_fin, acc = lax.fori_loop(0, n_chunks, step, init)
        o_ref[...] = (acc / l_fin).T
        lse_ref[0, 0] = m_fin + jnp.log2(l_fin)
        c_end(c_ins, c_outs, sems)

    per_stat = stat_chunk // tq
    outs = pl.pallas_call(
        body, name="attention_forward",
        out_shape=(jax.ShapeDtypeStruct((s_len, n_heads * HEAD), F32),
                   jax.ShapeDtypeStruct((n_heads, s_len // stat_chunk, 1, stat_chunk), F32), *c_shapes),
        grid=(n_heads, s_len // tq),
        in_specs=[pl.BlockSpec((1, tq, QK), lambda h, i: (h, i, 0)), pl.BlockSpec((1, s_len, QK), lambda h, i: (h, 0, 0)),
                  pl.BlockSpec((1, n_chunks, HEAD, ck), lambda h, i: (h, 0, 0, 0))] + c_in_specs,
        out_specs=(pl.BlockSpec((tq, HEAD), lambda h, i: (i, h)),
                   pl.BlockSpec((1, 1, 1, tq), lambda h, i: (h, i // per_stat, 0, i % per_stat)), *c_out_specs),
        scratch_shapes=c_sems,
        compiler_params=_params("arbitrary", "arbitrary") if comm else _params("parallel", "parallel"),
    )(q, k, vt, *c_ops)
    return outs[0], outs[1], outs[2:]


def _attention_backward(q, k, kt, v, do, lse, dsum, comm=None):
    n_heads, s_len, _ = q.shape
    tk = kt.shape[3]
    n_q, _, cq = lse.shape[1:]
    c_ops, c_in_specs, c_shapes, c_out_specs, c_sems, c_begin, c_end = _riding_exchange(comm, 2)
    n_c = len(c_ops)

    def body(q_ref, k_ref, kt_ref, v_ref, do_ref, lse_ref, dsum_ref, *rest):
        c_ins, (dqt_ref, dk_ref, dv_ref), c_outs, sems = rest[:n_c], rest[n_c:n_c + 3], rest[n_c + 3:2 * n_c + 3], rest[2 * n_c + 3:]
        c_begin(c_ins, c_outs, sems)
        first = pl.program_id(1) == 0
        k_j, kt_j, v_j = k_ref[0], kt_ref[0, 0], v_ref[0]

        def step(i, carry):
            dk, dv = carry
            rows = pl.ds(pl.multiple_of(i * cq, cq), cq)
            q_i, do_i = q_ref[0, rows, :], do_ref[rows, :]
            s = lax.dot_general(k_j, q_i, _NT, preferred_element_type=F32)
            p = jnp.exp2(s - lse_ref[0, i])
            dp = lax.dot_general(v_j, do_i, _NT, preferred_element_type=F32)
            ds = (p * (dp - dsum_ref[0, i]) * LN_2).astype(BF16)
            dv = dv + jnp.dot(p.astype(BF16), do_i, preferred_element_type=F32)
            dk = dk + jnp.dot(ds, q_i, preferred_element_type=F32)
            _acc(dqt_ref.at[0, i], jnp.dot(kt_j, ds, preferred_element_type=F32), first)
            return dk, dv

        dk, dv = lax.fori_loop(0, n_q, step, (jnp.zeros((tk, QK), F32), jnp.zeros((tk, HEAD), F32)))
        dk_ref[0] = dk
        dv_ref[0] = dv
        c_end(c_ins, c_outs, sems)

    stat = pl.BlockSpec((1, n_q, 1, cq), lambda h, j: (h, 0, 0, 0))
    outs = pl.pallas_call(
        body, name="attention_backward",
        out_shape=(jax.ShapeDtypeStruct((n_heads, n_q, QK, cq), F32), jax.ShapeDtypeStruct((n_heads, s_len, QK), F32),
                   jax.ShapeDtypeStruct((n_heads, s_len, HEAD), F32), *c_shapes),
        grid=(n_heads, s_len // tk),
        in_specs=[pl.BlockSpec((1, s_len, QK), lambda h, j: (h, 0, 0)), pl.BlockSpec((1, tk, QK), lambda h, j: (h, j, 0)),
                  pl.BlockSpec((1, 1, QK, tk), lambda h, j: (h, j, 0, 0)),
                  pl.BlockSpec((1, tk, HEAD), lambda h, j: (h, j, 0)), pl.BlockSpec((s_len, HEAD), lambda h, j: (0, h)),
                  stat, stat] + c_in_specs,
        out_specs=(pl.BlockSpec((1, n_q, QK, cq), lambda h, j: (h, 0, 0, 0)), pl.BlockSpec((1, tk, QK), lambda h, j: (h, j, 0)),
                   pl.BlockSpec((1, tk, HEAD), lambda h, j: (h, j, 0)), *c_out_specs),
        scratch_shapes=c_sems,
        compiler_params=_params("arbitrary", "arbitrary") if comm else _params("parallel", "arbitrary"),
    )(q, k, kt, v, do, lse, dsum, *c_ops)
    return outs[0], outs[1], outs[2], outs[3:]


def _exchange(arrs, gather, name):
    n = len(arrs)

    def body(*refs):
        plan = _exchange_plan(refs[:n], refs[n:2 * n], gather, *refs[2 * n:])
        _exchange_start(plan)
        _exchange_wait(plan)

    any_spec = pl.BlockSpec(memory_space=pl.ANY)
    return pl.pallas_call(
        body, name=name,
        out_shape=_exchange_out_shapes(arrs, gather),
        in_specs=[any_spec] * n,
        out_specs=tuple([any_spec] * n),
        scratch_shapes=_exchange_semaphores(n),
        compiler_params=pltpu.CompilerParams(has_side_effects=True),
    )(*arrs)


def _exchange_out_shapes(arrs, gather):
    return tuple(jax.ShapeDtypeStruct((N_DEV, *(a.shape if g else a.shape[1:])), a.dtype) for a, g in zip(arrs, gather))


def _exchange_semaphores(n):
    n_remote = n * (N_DEV - 1)
    return [pltpu.SemaphoreType.DMA((n_remote,)), pltpu.SemaphoreType.DMA((n_remote,)), pltpu.SemaphoreType.DMA((n,))]


def _exchange_plan(ins, outs, gather, send_sems, recv_sems, local_sems):
    n = len(ins)
    x, y, c = lax.axis_index("x"), lax.axis_index("y"), lax.axis_index("c")
    me = 4 * x + 2 * y + c

    def block_for(a, dev):
        return ins[a] if gather[a] else ins[a].at[dev]

    local = [pltpu.make_async_copy(block_for(a, me), outs[a].at[me], local_sems.at[a]) for a in range(n)]
    remote = []
    for k in range(1, N_DEV):
        px = 1 - x if k & 4 else x
        py = 1 - y if k & 2 else y
        pc = 1 - c if k & 1 else c
        peer = 4 * px + 2 * py + pc
        for a in range(n):
            idx = a * (N_DEV - 1) + k - 1
            send = pltpu.make_async_remote_copy(
                src_ref=block_for(a, peer), dst_ref=outs[a].at[me], send_sem=send_sems.at[idx], recv_sem=recv_sems.at[idx],
                device_id=(px, py, pc), device_id_type=pl.DeviceIdType.MESH)
            arrive = pltpu.make_async_remote_copy(
                src_ref=block_for(a, peer), dst_ref=outs[a].at[peer], send_sem=send_sems.at[idx], recv_sem=recv_sems.at[idx],
                device_id=(px, py, pc), device_id_type=pl.DeviceIdType.MESH)
            remote.append((send, arrive))
    return local, remote


def _exchange_start(plan):
    local, remote = plan
    for cp in local:
        cp.start()
    for send, _ in remote:
        send.start()


def _exchange_wait(plan):
    local, remote = plan
    for send, arrive in remote:
        send.wait_send()
        arrive.wait_recv()
    for cp in local:
        cp.wait()


def _riding_exchange(comm, n_grid):
    if comm is None:
        return [], [], (), (), [], lambda *_: None, lambda *_: None
    arrs, gather = comm
    n = len(arrs)
    any_spec = pl.BlockSpec(memory_space=pl.ANY)

    def begin(ins, outs, sems):
        @pl.when(functools.reduce(jnp.logical_and, [pl.program_id(d) == 0 for d in range(n_grid)]))
        def _():
            _exchange_start(_exchange_plan(ins, outs, gather, *sems))

    def end(ins, outs, sems):
        @pl.when(functools.reduce(jnp.logical_and, [pl.program_id(d) == pl.num_programs(d) - 1 for d in range(n_grid)]))
        def _():
            _exchange_wait(_exchange_plan(ins, outs, gather, *sems))

    return (list(arrs), [any_spec] * n, _exchange_out_shapes(arrs, gather), tuple([any_spec] * n), _exchange_semaphores(n),
            begin, end)


ADAM_TILE_ELEMS = 256 * 1024


def _sum_adam(parts, w, m, v, layer, prev, name):
    n_parts, r, c = parts.shape
    tm, tc = r, c
    if r % 16 == 0:
        while tm * c > ADAM_TILE_ELEMS and tm % 16 == 0:
            tm //= 2
    else:
        while r * tc > ADAM_TILE_ELEMS and tc % 256 == 0:
            tc //= 2

    def body(p_ref, w_ref, m_ref, v_ref, *rest):
        g_ref, d_ref, nm_ref, nv_ref = rest[-4:]
        g = p_ref[0].astype(F32)
        for s in range(1, n_parts):
            g = g + p_ref[s].astype(F32)
        m_new = ADAM_B1 * m_ref[...] + (1.0 - ADAM_B1) * g
        v_new = ADAM_B2 * v_ref[...] + (1.0 - ADAM_B2) * (g * g)
        m_hat = m_new / (1.0 - ADAM_B1 ** ADAM_STEP)
        v_hat = v_new / (1.0 - ADAM_B2 ** ADAM_STEP)
        g_ref[...] = g
        d_ref[...] = -ADAM_LR * (m_hat / (jnp.sqrt(v_hat) + ADAM_EPS) + ADAM_WD * w_ref[...])
        nm_ref[...] = m_new
        nv_ref[...] = v_new

    slab = pl.BlockSpec((None, tm, tc), lambda i, j: (layer, i, j))
    n_prev = 0 if prev is None else 4
    return pl.pallas_call(
        body, name=name,
        out_shape=(jax.ShapeDtypeStruct(w.shape, F32),) * 4,
        grid=(r // tm, c // tc),
        in_specs=[pl.BlockSpec((n_parts, tm, tc), lambda i, j: (0, i, j)), slab, slab, slab]
                 + [pl.BlockSpec(memory_space=pl.ANY)] * n_prev,
        out_specs=(slab, slab, slab, slab),
        input_output_aliases={4 + j: j for j in range(n_prev)},
        compiler_params=_params("parallel", "parallel"),
    )(parts, w, m, v, *(prev or ()))


def _permute_in(w):
    k = w.shape[0]
    q = w[:, 3584:5120].reshape(k, C_HEADS, QK)
    return jnp.concatenate(
        [w[:, 1536:3584], w[:, 5696:6720], w[:, 0:1536], q[:, :, :HEAD].reshape(k, C_HEADS * HEAD),
         q[:, :, HEAD:].reshape(k, C_HEADS * ROPE), w[:, 5120:5632], w[:, 5632:5696],
         jnp.zeros((k, PROJ_W - IN_WIDTH), w.dtype)], axis=1)


def _unpermute_in(g):
    k = g.shape[0]
    qn = g[:, O_M + M_QN:O_M + M_QR].reshape(k, C_HEADS, HEAD)
    qr = g[:, O_M + M_QR:O_M + M_CKV].reshape(k, C_HEADS, ROPE)
    q = jnp.concatenate([qn, qr], axis=-1).reshape(k, C_HEADS * QK)
    return jnp.concatenate(
        [g[:, O_A:O_A + W_A], g[:, O_B:O_B + W_B], q, g[:, O_M + M_CKV:O_M + M_KR],
         g[:, O_M + M_KR:O_M + M_KR + ROPE], g[:, O_CZ:O_CZ + W_CZ]], axis=1)


SMALL = ("attn_norm", "sgu_norm", "w_spatial", "b_spatial", "conv_b", "kv_norm", "q_nope_norm", "q_rope_norm",
         "k_nope_norm", "k_rope_norm", "out_norm", "ple_norm")
PACK_ROWS = 256


def _pack(tensors):
    flat = jnp.concatenate([t.reshape(-1) for t in tensors])
    rows = -(-flat.shape[0] // (128 * PACK_ROWS)) * PACK_ROWS
    return jnp.pad(flat, (0, rows * 128 - flat.shape[0])).reshape(rows, 128)


def _unpack(packed, like):
    flat = packed.reshape(-1)
    out, pos = [], 0
    for t in like:
        out.append(flat[pos:pos + t.size].reshape(t.shape))
        pos += t.size
    return out


def _tile(s_len, want):
    return min(want, s_len)


ATT_FWD_QUERIES = 512
ATT_FWD_KEYS = 8192
ATT_BWD_KEYS = 1024
ATT_BWD_QUERIES = 4096


def _layer_forward(h, p_l, cosf, sins, w, sm, comm, comm_rest):
    s_len = h.shape[0]
    tm = _tile(s_len, 512)
    proj, hn, rest = _norm_matmul(h, sm["attn_norm"], w["w_in"], _tile(s_len, 1024), 768, comm_rest)
    if comm_rest is not None:
        w = {**w, **_assemble_rest(rest)}
    ga, gb, gc = sm["out_norm"][:, 0:512], sm["out_norm"][:, 512:1024], sm["out_norm"][:, 1024:2048]
    y = _sgu_forward(proj, sm["sgu_norm"], sm["w_spatial"], sm["b_spatial"], ga, _tile(s_len, 256))
    y = _conv_forward(proj, w["conv_w"], sm["conv_b"], gb, y, _tile(s_len, 256))
    q, k, v, kt, vt = _mla_forward(proj, cosf, sins, w["w_ukv"], sm["kv_norm"], sm["q_nope_norm"], sm["q_rope_norm"],
                                   sm["k_nope_norm"], sm["k_rope_norm"], tm, _tile(s_len, ATT_BWD_KEYS),
                                   _tile(s_len, ATT_FWD_KEYS))
    o, lse, arrived = _attention_forward(q, k, vt, _tile(s_len, ATT_FWD_QUERIES), _tile(s_len, ATT_BWD_QUERIES), comm)
    y = _cgate_forward(o, proj, gc, y, _tile(s_len, 256))
    h1 = _out_matmul(h, y, w["w_out"], _tile(s_len, 1024), 1024)
    h2, n1, gate, pp = _ple_forward(h1, sm["ple_norm"], p_l, w["w_ple_gate"], w["w_ple_proj"], tm, 1024)
    saved = dict(h=h, hn=hn, proj=proj, y=y, q=q, k=k, v=v, kt=kt, o=o, lse=lse, h1=h1, n1=n1, gate=gate, pp=pp)
    return h2, saved, w, arrived


def _layer_backward(dh2, p_l, cosf, sins, w, sm, sv, comm, scatter_own):
    s_len = dh2.shape[0]
    tm = _tile(s_len, 512)
    tr = _tile(s_len, 256)
    big, small = {}, {}
    dh1, small["ple_norm"], dgp, dpp = _ple_backward(dh2, sv["gate"], sv["pp"], w["w_ple_gate"], sv["h1"], sm["ple_norm"], tm)
    big["w_ple_proj"], _ = _matmul_tn(p_l, dpp, _tile(s_len, 2048), PLE_DIM, 1024, "grad_w_ple_proj")
    big["w_ple_gate"], _ = _matmul_tn(sv["n1"], dgp, _tile(s_len, 2048), 1024, 1024, "grad_w_ple_gate")
    dy, _ = _matmul_nt(dh1, w["w_out"], _tile(s_len, 1024), 1024, "grad_branches")
    big["w_out"], _ = _matmul_tn(sv["y"], dh1, _tile(s_len, 2048), 1024, 1024, "grad_w_out")
    ga, gb, gc = sm["out_norm"][:, 0:512], sm["out_norm"][:, 512:1024], sm["out_norm"][:, 1024:2048]
    dproj, dcw, small["conv_b"], dgb = _conv_backward(sv["proj"], w["conv_w"], sm["conv_b"], gb, dy, tr)
    big["conv_w"] = dcw
    dproj, do, dsum, dgc = _cgate_backward(sv["o"], sv["proj"], gc, dy, dproj, tm, _tile(s_len, ATT_BWD_QUERIES))
    dqt, dk, dv, arrived = _attention_backward(sv["q"], sv["k"], sv["kt"], sv["v"], do, sv["lse"], dsum, comm)
    (dproj, big["w_ukv"], small["kv_norm"], small["q_nope_norm"], small["q_rope_norm"], small["k_nope_norm"],
     small["k_rope_norm"]) = _mla_backward(sv["proj"], cosf, sins, w["w_ukv"], sm["kv_norm"], sm["q_nope_norm"],
                                            sm["q_rope_norm"], sm["k_nope_norm"], sm["k_rope_norm"], dqt, dk, dv, dproj, tr)
    dproj, small["sgu_norm"], small["w_spatial"], small["b_spatial"], dga = _sgu_backward(
        sv["proj"], sm["sgu_norm"], sm["w_spatial"], sm["b_spatial"], ga, dy, dproj, tm)
    small["out_norm"] = jnp.concatenate([dga, dgb, dgc], axis=1)
    parts_rest = _parts_rest(big)
    g_in, arrived_rest = _matmul_tn(sv["hn"], dproj, _tile(s_len, 2048), 512, 2304, "grad_w_in",
                                    (parts_rest, [False] * len(parts_rest)) if scatter_own else None)
    parts = [_part_w_in(g_in)] + parts_rest
    dhn, arrived_in = _matmul_nt(dproj, w["w_in"], _tile(s_len, 1024), 256, "grad_attn_norm_in",
                                 (parts[:1], [False]) if scatter_own else None)
    dh, small["attn_norm"] = _rms_backward(sv["h"], sm["attn_norm"], dhn, dh1, tr, "attn_norm_backward")
    return dh, parts, big["conv_w"], small, arrived, (*arrived_in, *arrived_rest) if scatter_own else None


def _layer_small(params, layer):
    return dict(
        attn_norm=params["attn_norm"][layer][None, :], sgu_norm=params["sgu_norm"][layer],
        w_spatial=params["w_spatial"][layer], b_spatial=params["b_spatial"][layer][:, :, None],
        conv_b=params["conv_b"][layer][None, :], kv_norm=params["kv_norm"][layer][None, :],
        q_nope_norm=params["q_nope_norm"][layer][None, :], q_rope_norm=params["q_rope_norm"][layer][None, :],
        k_nope_norm=params["k_nope_norm"][layer][None, :], k_rope_norm=params["k_rope_norm"][layer][None, :],
        out_norm=params["out_norm"][layer][None, :], ple_norm=params["ple_norm"][layer][None, :])


BIG = ("w_in", "w_ukv", "w_out", "w_ple_gate", "w_ple_proj")


def _assemble_w_in(g_in):
    return _permute_in(g_in.transpose(1, 0, 2).reshape(g_in.shape[1], IN_WIDTH))


def _assemble_rest(gathered):
    g_ukv, g_out, g_gate, g_proj = gathered
    w_ukv = g_ukv.reshape(N_DEV, KV_RANK, 2, HEAD).transpose(1, 2, 0, 3).reshape(KV_RANK, 2 * C_HEADS * HEAD)
    return dict(w_ukv=w_ukv, w_out=g_out.reshape(D_MODEL, D_MODEL), w_ple_gate=g_gate.reshape(D_MODEL, D_MODEL),
                w_ple_proj=g_proj.transpose(1, 0, 2).reshape(PLE_DIM, D_MODEL))


def _part_w_in(g_in):
    return _unpermute_in(g_in).reshape(g_in.shape[0], N_DEV, -1).transpose(1, 2, 0).astype(BF16)


def _parts_rest(big):
    return [
        big["w_ukv"].reshape(KV_RANK, 2, N_DEV, HEAD).transpose(2, 0, 1, 3).reshape(N_DEV, KV_RANK, 2 * HEAD).astype(BF16),
        big["w_out"].reshape(N_DEV, -1, D_MODEL).astype(BF16),
        big["w_ple_gate"].reshape(N_DEV, -1, D_MODEL).astype(BF16),
        big["w_ple_proj"].reshape(PLE_DIM, N_DEV, -1).transpose(1, 0, 2).astype(BF16)]


def _step_local(xs, ps, pos, target, shards, conv_w, params):
    inv = 1.0 / (ROPE_BASE ** (jnp.arange(0, ROPE, 2, dtype=F32) / ROPE))
    ang = pos.astype(F32)[:, None] * inv
    cos, sin = jnp.cos(ang), jnp.sin(ang)
    cosf = jnp.concatenate([cos, cos], axis=-1)
    sins = jnp.concatenate([-sin, sin], axis=-1)

    def gather_of(names, layer):
        return [shards[n][layer] for n in names], [True] * len(names)

    h = xs
    saved, weights = [], []
    smalls = [_layer_small(params, layer) for layer in range(DEPTH)]
    (first_w_in,) = _exchange(*gather_of(BIG[:1], 0), "gather_first_w_in")
    w = dict(w_in=_assemble_w_in(first_w_in), conv_w=conv_w[0])
    for layer in range(DEPTH):
        comm = gather_of(BIG, layer + 1) if layer + 1 < DEPTH else None
        comm_rest = gather_of(BIG[1:], 0) if layer == 0 else None
        h, sv, w, arrived = _layer_forward(h, ps[layer], cosf, sins, w, smalls[layer], comm, comm_rest)
        saved.append(sv)
        weights.append(w)
        if comm is not None:
            w = dict(w_in=_assemble_w_in(arrived[0]), conv_w=conv_w[layer + 1], **_assemble_rest(arrived[1:]))
    dh, loss = _loss_grad(h, target, _tile(h.shape[0], 512))
    received, conv_grads, small_grads = [None] * DEPTH, [None] * DEPTH, [None] * DEPTH
    comm = None
    for layer in reversed(range(DEPTH)):
        dh, parts, conv_grads[layer], small_grads[layer], arrived, arrived_own = _layer_backward(
            dh, ps[layer], cosf, sins, weights[layer], smalls[layer], saved[layer], comm, layer == 0)
        if comm is not None:
            received[layer + 1] = arrived
        comm = (parts, [False] * len(parts))
    received[0] = arrived_own
    return loss, dh, received, conv_grads, small_grads


def kernel(x, p, positions, attn_norm, w_in, sgu_norm, w_spatial, b_spatial, conv_w, conv_b, kv_norm, w_ukv, q_nope_norm, q_rope_norm, k_nope_norm, k_rope_norm, out_norm, w_out, ple_norm, w_ple_gate, w_ple_proj, loss_target, m_attn_norm, m_w_in, m_sgu_norm, m_w_spatial, m_b_spatial, m_conv_w, m_conv_b, m_kv_norm, m_w_ukv, m_q_nope_norm, m_q_rope_norm, m_k_nope_norm, m_k_rope_norm, m_out_norm, m_w_out, m_ple_norm, m_w_ple_gate, m_w_ple_proj, v_attn_norm, v_w_in, v_sgu_norm, v_w_spatial, v_b_spatial, v_conv_w, v_conv_b, v_kv_norm, v_w_ukv, v_q_nope_norm, v_q_rope_norm, v_k_nope_norm, v_k_rope_norm, v_out_norm, v_w_out, v_ple_norm, v_w_ple_gate, v_w_ple_proj):
    order = ("attn_norm", "w_in", "sgu_norm", "w_spatial", "b_spatial", "conv_w", "conv_b", "kv_norm", "w_ukv",
             "q_nope_norm", "q_rope_norm", "k_nope_norm", "k_rope_norm", "out_norm", "w_out", "ple_norm", "w_ple_gate",
             "w_ple_proj")
    wts = dict(zip(order, (attn_norm, w_in, sgu_norm, w_spatial, b_spatial, conv_w, conv_b, kv_norm, w_ukv, q_nope_norm,
                           q_rope_norm, k_nope_norm, k_rope_norm, out_norm, w_out, ple_norm, w_ple_gate, w_ple_proj)))
    mom = dict(zip(order, (m_attn_norm, m_w_in, m_sgu_norm, m_w_spatial, m_b_spatial, m_conv_w, m_conv_b, m_kv_norm, m_w_ukv,
                           m_q_nope_norm, m_q_rope_norm, m_k_nope_norm, m_k_rope_norm, m_out_norm, m_w_out, m_ple_norm,
                           m_w_ple_gate, m_w_ple_proj)))
    var = dict(zip(order, (v_attn_norm, v_w_in, v_sgu_norm, v_w_spatial, v_b_spatial, v_conv_w, v_conv_b, v_kv_norm, v_w_ukv,
                           v_q_nope_norm, v_q_rope_norm, v_k_nope_norm, v_k_rope_norm, v_out_norm, v_w_out, v_ple_norm,
                           v_w_ple_gate, v_w_ple_proj)))

    conv_shard = wts["conv_w"]
    (conv_all,) = _exchange([conv_shard.reshape(-1, 128)], [True], "gather_conv_w")
    conv_full = conv_all.reshape(N_DEV, DEPTH, 3, -1).transpose(1, 2, 0, 3).reshape(DEPTH, 3, -1)
    shards = {n: wts[n].astype(BF16) for n in BIG}
    loss_part, grad_x, received, conv_grads, small_grads = _step_local(
        x[0], p[:, 0], positions[0], loss_target[0], shards, conv_full, wts)
    loss = lax.psum(loss_part[0, 0], ("x", "y", "c"))

    def small_grad(name):
        g = jnp.stack([sg[name] for sg in small_grads])
        return g.reshape(wts[name].shape)

    conv_grad = jnp.stack(conv_grads)
    like = [wts[n] for n in SMALL] + [conv_grad]
    packed = _pack([small_grad(n) for n in SMALL] + [conv_grad])
    (small_parts,) = _exchange([packed], [True], "gather_small_grads")
    filler = [jnp.zeros_like(conv_grad), jnp.zeros_like(conv_grad), jnp.ones_like(conv_grad)]
    small_out = _sum_adam(small_parts, *(_pack([src[n] for n in SMALL] + [fill])[None] for src, fill in zip((wts, mom, var), filler)),
                          0, None, "adam_small")
    unpacked = [_unpack(o[0], like) for o in small_out]
    results = {n: vals for n, vals in zip(SMALL, zip(*[u[:-1] for u in unpacked]))}
    me = 4 * lax.axis_index("x") + 2 * lax.axis_index("y") + lax.axis_index("c")
    width = conv_shard.shape[2]
    conv_local = lax.dynamic_slice_in_dim(unpacked[0][-1], me * width, width, axis=2)
    as_slab = (lambda t: t.reshape(1, -1, width))
    conv_out = _sum_adam(as_slab(conv_local), as_slab(conv_shard), as_slab(mom["conv_w"]), as_slab(var["conv_w"]), 0, None,
                         "adam_conv_w")
    results["conv_w"] = tuple(o.reshape(conv_shard.shape) for o in conv_out)

    for j, name in enumerate(BIG):
        view = (lambda t: jnp.swapaxes(t, 1, 2)) if name == "w_in" else (lambda t: t)
        outs = None
        for layer in range(DEPTH):
            outs = _sum_adam(received[layer][j], view(wts[name]), view(mom[name]), view(var[name]), layer, outs, "adam_" + name)
        results[name] = tuple(view(o) for o in outs)

    grads, deltas, new_m, new_v = ([results[n][j] for n in order] for j in range(4))
    return (loss, grad_x[None], *grads, *deltas, *new_m, *new_v)
```

```python
import functools

import jax
import jax.numpy as jnp
from jax import lax
from jax.experimental import pallas as pl
from jax.experimental.pallas import tpu as pltpu

F32 = jnp.float32
BF16 = jnp.bfloat16

N_DEV = 8
DEPTH = 4
D_MODEL = 2048
EPS = 1e-6
CHUNK = 128
A_HEADS = 4
HEAD = 128
ROPE = 64
HALF = ROPE // 2
C_HEADS = 8
KV_RANK = 512
PLE_DIM = 256
ROPE_BASE = 10000.0
IN_WIDTH = 6720
QK = HEAD + ROPE
SCALE = QK ** -0.5
LOG2_E = 1.4426950408889634
LN_2 = 0.6931471805599453
Q_SCALE = SCALE * LOG2_E
HALO = 8

O_B = 0
W_B = 2048
O_CZ = 2048
W_CZ = 1024
O_A = 3072
W_A = 1536
O_M = 4608
W_M = 2304
M_QN, M_QR, M_CKV, M_KR = 0, 1024, 1536, 2048
PROJ_W = 6912

ADAM_LR = 0.001
ADAM_B1 = 0.9
ADAM_B2 = 0.999
ADAM_EPS = 1e-08
ADAM_WD = 0.01
ADAM_STEP = 10

VMEM_LIMIT = 56 * 1024 * 1024

_NT = (((1,), (1,)), ((), ()))
_TN = (((0,), (0,)), ((), ()))


def _params(*sem):
    return pltpu.CompilerParams(dimension_semantics=sem, vmem_limit_bytes=VMEM_LIMIT)


@jax.custom_vjp
def _bdot(a, b):
    return jnp.dot(a.astype(BF16), b.astype(BF16), preferred_element_type=F32)


def _bdot_fwd(a, b):
    return _bdot(a, b), (a, b)


def _bdot_bwd(res, g):
    a, b = res
    gb = g.astype(BF16)
    da = lax.dot_general(gb, b.astype(BF16), _NT, preferred_element_type=F32)
    db = lax.dot_general(a.astype(BF16), gb, _TN, preferred_element_type=F32)
    return da.astype(a.dtype), db.astype(b.dtype)


_bdot.defvjp(_bdot_fwd, _bdot_bwd)


@functools.partial(jax.custom_vjp, nondiff_argnums=(1,))
def _split(x, n):
    w = x.shape[-1] // n
    return tuple(x[:, i * w:(i + 1) * w] for i in range(n))


def _split_fwd(x, n):
    return _split(x, n), None


def _split_bwd(n, _, gs):
    return (jnp.concatenate(gs, axis=-1),)


_split.defvjp(_split_fwd, _split_bwd)


@functools.partial(jax.custom_vjp, nondiff_argnums=(1,))
def _shift_rows(x, k):
    return pltpu.roll(x, k % x.shape[0], 0)


def _shift_rows_fwd(x, k):
    return _shift_rows(x, k), None


def _shift_rows_bwd(k, _, g):
    return (_shift_rows(g, -k),)


_shift_rows.defvjp(_shift_rows_fwd, _shift_rows_bwd)


@jax.custom_vjp
def _swap_halves(x):
    h = x.shape[-1] // 2
    return jnp.concatenate([x[:, h:], x[:, :h]], axis=-1)


def _swap_halves_fwd(x):
    return _swap_halves(x), None


def _swap_halves_bwd(_, g):
    return (_swap_halves(g),)


_swap_halves.defvjp(_swap_halves_fwd, _swap_halves_bwd)


def _rms(x, g):
    return x * lax.rsqrt(jnp.mean(x * x, axis=-1, keepdims=True) + EPS) * g


def _rope(x, cosf, sins):
    return x * cosf + _swap_halves(x) * sins


def _sgu_chunk(u, v, z, gain, ws, bs, ga):
    ys = []
    for h in range(A_HEADS):
        vn = _rms(v[h], gain[h])
        s = _bdot(ws[h], vn) + bs[h]
        ys.append(u[h] * s * jax.nn.silu(z[h]))
    ss = sum(jnp.sum(y * y, axis=-1, keepdims=True) for y in ys) * (1.0 / (A_HEADS * HEAD))
    r = lax.rsqrt(ss + EPS)
    return tuple(ys[h] * r * ga[h] for h in range(A_HEADS))


def _conv_tile(bb, bc, bh, bz, w0, w1, w2, cb, gb, w0h, w1h, w2h, cbh, gbh, valid, core):
    t = jnp.where(valid, bc * bh, 0.0)
    y = (jnp.where(core, cb, cbh)
         + _shift_rows(t, 1) * jnp.where(core, w0, w0h)
         + t * jnp.where(core, w1, w1h)
         + _shift_rows(t, -1) * jnp.where(core, w2, w2h))
    return _rms(bb * y * jax.nn.silu(bz), jnp.where(core, gb, gbh))


def _cgate_tile(o, cz, gc):
    return _rms(o * jax.nn.silu(cz), gc)


def _mla_tile(qn, qr, ckv, kr, cosf, sins, wukv, kvg, qng, qrg, kng, krg):
    kv = _split(_bdot(_rms(ckv, kvg), wukv), 2 * C_HEADS)
    k_r = _rope(_rms(kr, krg), cosf, sins)
    qn_h = _split(qn, C_HEADS)
    qr_h = _split(qr, C_HEADS)
    q, k, v = [], [], []
    for h in range(C_HEADS):
        q.append(jnp.concatenate([_rms(qn_h[h], qng), _rope(_rms(qr_h[h], qrg), cosf, sins)], axis=-1) * Q_SCALE)
        k.append(jnp.concatenate([_rms(kv[h], kng), k_r], axis=-1))
        v.append(kv[C_HEADS + h])
    return tuple(q), tuple(k), tuple(v)


def _norm_matmul(h, gain, w, tm, tn, comm=None):
    s_len, k = h.shape
    n = w.shape[1]
    c_ops, c_in_specs, c_shapes, c_out_specs, c_sems, c_begin, c_end = _riding_exchange(comm, 2)
    n_c = len(c_ops)

    def body(h_ref, g_ref, w_ref, *rest):
        c_ins, (o_ref, hn_ref), c_outs, sems = rest[:n_c], rest[n_c:n_c + 2], rest[n_c + 2:2 * n_c + 2], rest[2 * n_c + 2:]
        c_begin(c_ins, c_outs, sems)

        @pl.when(pl.program_id(1) == 0)
        def _():
            hn_ref[...] = _rms(h_ref[...], g_ref[...]).astype(BF16)

        o_ref[...] = jnp.dot(hn_ref[...], w_ref[...], preferred_element_type=F32).astype(BF16)
        c_end(c_ins, c_outs, sems)

    outs = pl.pallas_call(
        body, name="norm_matmul",
        out_shape=(jax.ShapeDtypeStruct((s_len, n), BF16), jax.ShapeDtypeStruct((s_len, k), BF16), *c_shapes),
        grid=(s_len // tm, n // tn),
        in_specs=[pl.BlockSpec((tm, k), lambda i, j: (i, 0)), pl.BlockSpec((1, k), lambda i, j: (0, 0)),
                  pl.BlockSpec((k, tn), lambda i, j: (0, j))] + c_in_specs,
        out_specs=(pl.BlockSpec((tm, tn), lambda i, j: (i, j)), pl.BlockSpec((tm, k), lambda i, j: (i, 0)), *c_out_specs),
        scratch_shapes=c_sems,
        compiler_params=_params("arbitrary", "arbitrary") if comm else _params("parallel", "arbitrary"),
    )(h, gain, w, *c_ops)
    return outs[0], outs[1], outs[2:]


def _out_matmul(h, y, w, tm, tn):
    s_len, n = h.shape
    k = y.shape[1]

    def body(h_ref, y_ref, w_ref, o_ref):
        o_ref[...] = h_ref[...] + jnp.dot(y_ref[...], w_ref[...], preferred_element_type=F32)

    return pl.pallas_call(
        body, name="out_matmul",
        out_shape=jax.ShapeDtypeStruct((s_len, n), F32),
        grid=(s_len // tm, n // tn),
        in_specs=[pl.BlockSpec((tm, tn), lambda i, j: (i, j)), pl.BlockSpec((tm, k), lambda i, j: (i, 0)),
                  pl.BlockSpec((k, tn), lambda i, j: (0, j))],
        out_specs=pl.BlockSpec((tm, tn), lambda i, j: (i, j)),
        compiler_params=_params("parallel", "parallel"),
    )(h, y, w)


def _ple_forward(h1, gain, p, wg, wp, tm, tn):
    s_len, d = h1.shape
    kp = p.shape[1]

    def body(hrow_ref, g_ref, p_ref, wg_ref, wp_ref, o_ref, n1_ref, gate_ref, pp_ref):
        j = pl.program_id(1)

        @pl.when(j == 0)
        def _():
            n1_ref[...] = _rms(hrow_ref[...], g_ref[...]).astype(BF16)

        gate = jax.nn.sigmoid(jnp.dot(n1_ref[...], wg_ref[...], preferred_element_type=F32))
        pp = jnp.dot(p_ref[...].astype(BF16), wp_ref[...], preferred_element_type=F32)
        o_ref[...] = hrow_ref[:, pl.ds(pl.multiple_of(j * tn, tn), tn)] + gate * pp
        gate_ref[...] = gate.astype(BF16)
        pp_ref[...] = pp.astype(BF16)

    col = pl.BlockSpec((tm, tn), lambda i, j: (i, j))
    return pl.pallas_call(
        body, name="ple_forward",
        out_shape=(jax.ShapeDtypeStruct((s_len, d), F32), jax.ShapeDtypeStruct((s_len, d), BF16),
                   jax.ShapeDtypeStruct((s_len, d), BF16), jax.ShapeDtypeStruct((s_len, d), BF16)),
        grid=(s_len // tm, d // tn),
        in_specs=[pl.BlockSpec((tm, d), lambda i, j: (i, 0)), pl.BlockSpec((1, d), lambda i, j: (0, 0)),
                  pl.BlockSpec((tm, kp), lambda i, j: (i, 0)), pl.BlockSpec((d, tn), lambda i, j: (0, j)),
                  pl.BlockSpec((kp, tn), lambda i, j: (0, j))],
        out_specs=(col, pl.BlockSpec((tm, d), lambda i, j: (i, 0)), col, col),
        compiler_params=_params("parallel", "arbitrary"),
    )(h1, gain, p, wg, wp)


def _matmul_nt(a, b, tm, tk, name, comm=None):
    m, n = a.shape
    k = b.shape[0]
    c_ops, c_in_specs, c_shapes, c_out_specs, c_sems, c_begin, c_end = _riding_exchange(comm, 2)
    n_c = len(c_ops)

    def body(a_ref, b_ref, *rest):
        c_ins, o_ref, c_outs, sems = rest[:n_c], rest[n_c], rest[n_c + 1:2 * n_c + 1], rest[2 * n_c + 1:]
        c_begin(c_ins, c_outs, sems)
        o_ref[...] = lax.dot_general(a_ref[...].astype(BF16), b_ref[...].astype(BF16), _NT, preferred_element_type=F32)
        c_end(c_ins, c_outs, sems)

    outs = pl.pallas_call(
        body, name=name,
        out_shape=(jax.ShapeDtypeStruct((m, k), F32), *c_shapes),
        grid=(m // tm, k // tk),
        in_specs=[pl.BlockSpec((tm, n), lambda i, j: (i, 0)), pl.BlockSpec((tk, n), lambda i, j: (j, 0))] + c_in_specs,
        out_specs=(pl.BlockSpec((tm, tk), lambda i, j: (i, j)), *c_out_specs),
        scratch_shapes=c_sems,
        compiler_params=_params("arbitrary", "arbitrary") if comm else _params("parallel", "parallel"),
    )(a, b, *c_ops)
    return outs[0], outs[1:]


def _matmul_tn(a, b, tm, tk, tn, name, comm=None):
    m, k = a.shape
    n = b.shape[1]
    n_m = m // tm
    c_ops, c_in_specs, c_shapes, c_out_specs, c_sems, c_begin, c_end = _riding_exchange(comm, 3)
    n_c = len(c_ops)

    def body(a_ref, b_ref, *rest):
        c_ins, o_ref, c_outs, acc_ref, sems = rest[:n_c], rest[n_c], rest[n_c + 1:2 * n_c + 1], rest[2 * n_c + 1], rest[2 * n_c + 2:]
        c_begin(c_ins, c_outs, sems)
        part = lax.dot_general(a_ref[...].astype(BF16), b_ref[...].astype(BF16), _TN, preferred_element_type=F32)
        _acc(acc_ref, part, pl.program_id(2) == 0)

        @pl.when(pl.program_id(2) == n_m - 1)
        def _():
            o_ref[...] = acc_ref[...].astype(BF16)

        c_end(c_ins, c_outs, sems)

    outs = pl.pallas_call(
        body, name=name,
        out_shape=(jax.ShapeDtypeStruct((k, n), BF16), *c_shapes),
        grid=(k // tk, n // tn, n_m),
        in_specs=[pl.BlockSpec((tm, tk), lambda kk, nn, mm: (mm, kk)), pl.BlockSpec((tm, tn), lambda kk, nn, mm: (mm, nn))]
                 + c_in_specs,
        out_specs=(pl.BlockSpec((tk, tn), lambda kk, nn, mm: (kk, nn)), *c_out_specs),
        scratch_shapes=[pltpu.VMEM((tk, tn), F32)] + c_sems,
        compiler_params=_params("arbitrary", "arbitrary", "arbitrary") if comm else _params("parallel", "parallel", "arbitrary"),
    )(a, b, *c_ops)
    return outs[0], outs[1:]


def _acc(ref, val, first):
    @pl.when(first)
    def _():
        ref[...] = val

    @pl.when(jnp.logical_not(first))
    def _():
        ref[...] += val


def _loss_grad(h, target, tm):
    s_len, d = h.shape

    def body(h_ref, t_ref, dh_ref, loss_ref):
        e = h_ref[...] - t_ref[...]
        dh_ref[...] = e * (1.0 / d)
        part = jnp.sum(jnp.sum(e * e, axis=-1, keepdims=True), axis=0, keepdims=True) * (0.5 / d)
        _acc(loss_ref, jnp.broadcast_to(part, loss_ref.shape), pl.program_id(0) == 0)

    row = pl.BlockSpec((tm, d), lambda i: (i, 0))
    return pl.pallas_call(
        body, name="loss_grad",
        out_shape=(jax.ShapeDtypeStruct((s_len, d), F32), jax.ShapeDtypeStruct((1, 128), F32)),
        grid=(s_len // tm,),
        in_specs=[row, row],
        out_specs=(row, pl.BlockSpec((1, 128), lambda i: (0, 0))),
        compiler_params=_params("arbitrary"),
    )(h, target)


def _rms_backward(x, gain, dn, dres, tm, name):
    s_len, d = x.shape

    def body(x_ref, g_ref, dn_ref, dres_ref, dx_ref, dg_ref):
        _, vjp = jax.vjp(_rms, x_ref[...], g_ref[...])
        dx, dg = vjp(dn_ref[...])
        dx_ref[...] = dres_ref[...] + dx
        _acc(dg_ref, dg, pl.program_id(0) == 0)

    row = pl.BlockSpec((tm, d), lambda i: (i, 0))
    vec = pl.BlockSpec((1, d), lambda i: (0, 0))
    return pl.pallas_call(
        body, name=name,
        out_shape=(jax.ShapeDtypeStruct((s_len, d), F32), jax.ShapeDtypeStruct((1, d), F32)),
        grid=(s_len // tm,),
        in_specs=[row, vec, row, row],
        out_specs=(row, vec),
        compiler_params=_params("arbitrary"),
    )(x, gain, dn, dres)


def _ple_backward(dh2, gate, pp, wg, h1, gain, tm):
    s_len, d = dh2.shape

    def body(dh_ref, gate_ref, pp_ref, wg_ref, h1_ref, g_ref, dh1_ref, dgain_ref, dgp_ref, dpp_ref):
        dh = dh_ref[...]
        gate = gate_ref[...].astype(F32)
        dgp = (dh * pp_ref[...].astype(F32) * gate * (1.0 - gate)).astype(BF16)
        dgp_ref[...] = dgp
        dpp_ref[...] = (dh * gate).astype(BF16)
        dn = lax.dot_general(dgp, wg_ref[...], _NT, preferred_element_type=F32)
        _, vjp = jax.vjp(_rms, h1_ref[...], g_ref[...])
        dx, dgain = vjp(dn)
        dh1_ref[...] = dh + dx
        _acc(dgain_ref, dgain, pl.program_id(0) == 0)

    row = pl.BlockSpec((tm, d), lambda i: (i, 0))
    vec = pl.BlockSpec((1, d), lambda i: (0, 0))
    return pl.pallas_call(
        body, name="ple_backward",
        out_shape=(jax.ShapeDtypeStruct((s_len, d), F32), jax.ShapeDtypeStruct((1, d), F32),
                   jax.ShapeDtypeStruct((s_len, d), BF16), jax.ShapeDtypeStruct((s_len, d), BF16)),
        grid=(s_len // tm,),
        in_specs=[row, row, row, pl.BlockSpec((d, d), lambda i: (0, 0)), row, vec],
        out_specs=(row, vec, row, row),
        compiler_params=_params("arbitrary"),
    )(dh2, gate, pp, wg, h1, gain)


def _sgu_in_specs(tm):
    return [pl.BlockSpec((tm, W_A), lambda i: (i, O_A // W_A)),
            pl.BlockSpec((A_HEADS, HEAD), lambda i: (0, 0)), pl.BlockSpec((A_HEADS, CHUNK, CHUNK), lambda i: (0, 0, 0)),
            pl.BlockSpec((A_HEADS, CHUNK, 1), lambda i: (0, 0, 0)), pl.BlockSpec((1, 512), lambda i: (0, 0))]


def _sgu_load(a_ref, gain_ref, ws_ref, bs_ref, ga_ref, c):
    rows = slice(c * CHUNK, (c + 1) * CHUNK)
    heads = range(A_HEADS)
    u = tuple(a_ref[rows, h * HEAD:(h + 1) * HEAD].astype(F32) for h in heads)
    v = tuple(a_ref[rows, 512 + h * HEAD:512 + (h + 1) * HEAD].astype(F32) for h in heads)
    z = tuple(a_ref[rows, 1024 + h * HEAD:1024 + (h + 1) * HEAD].astype(F32) for h in heads)
    gain = tuple(gain_ref[h:h + 1, :] for h in heads)
    ws = tuple(ws_ref[h] for h in heads)
    bs = tuple(bs_ref[h] for h in heads)
    ga = tuple(ga_ref[:, h * HEAD:(h + 1) * HEAD] for h in heads)
    return u, v, z, gain, ws, bs, ga


def _sgu_forward(proj, gain, ws, bs, ga, tm):
    s_len = proj.shape[0]

    def body(a_ref, gain_ref, ws_ref, bs_ref, ga_ref, o_ref):
        for c in range(tm // CHUNK):
            out = _sgu_chunk(*_sgu_load(a_ref, gain_ref, ws_ref, bs_ref, ga_ref, c))
            for h in range(A_HEADS):
                o_ref[c * CHUNK:(c + 1) * CHUNK, h * HEAD:(h + 1) * HEAD] = out[h].astype(BF16)

    return pl.pallas_call(
        body, name="sgu_forward",
        out_shape=jax.ShapeDtypeStruct((s_len, D_MODEL), BF16),
        grid=(s_len // tm,),
        in_specs=_sgu_in_specs(tm),
        out_specs=pl.BlockSpec((tm, 512), lambda i: (i, 0)),
        compiler_params=_params("parallel"),
    )(proj, gain, ws, bs, ga)


def _sgu_backward(proj, gain, ws, bs, ga, dy, dproj, tm):
    s_len = proj.shape[0]

    def body(a_ref, gain_ref, ws_ref, bs_ref, ga_ref, dy_ref, _, da_ref, dgain_ref, dws_ref, dbs_ref, dga_ref):
        tot = None
        for c in range(tm // CHUNK):
            args = _sgu_load(a_ref, gain_ref, ws_ref, bs_ref, ga_ref, c)
            _, vjp = jax.vjp(_sgu_chunk, *args)
            rows = slice(c * CHUNK, (c + 1) * CHUNK)
            du, dv, dz, dgain, dws, dbs, dga = vjp(tuple(dy_ref[rows, h * HEAD:(h + 1) * HEAD] for h in range(A_HEADS)))
            for h in range(A_HEADS):
                da_ref[rows, h * HEAD:(h + 1) * HEAD] = du[h].astype(BF16)
                da_ref[rows, 512 + h * HEAD:512 + (h + 1) * HEAD] = dv[h].astype(BF16)
                da_ref[rows, 1024 + h * HEAD:1024 + (h + 1) * HEAD] = dz[h].astype(BF16)
            part = (dgain, dws, dbs, dga)
            tot = part if tot is None else jax.tree.map(jnp.add, tot, part)
        dgain, dws, dbs, dga = tot
        first = pl.program_id(0) == 0
        _acc(dgain_ref, jnp.concatenate(dgain, axis=0), first)
        _acc(dga_ref, jnp.concatenate(dga, axis=-1), first)
        for h in range(A_HEADS):
            _acc(dws_ref.at[h], dws[h], first)
            _acc(dbs_ref.at[h], dbs[h], first)

    small = [pl.BlockSpec((A_HEADS, HEAD), lambda i: (0, 0)), pl.BlockSpec((A_HEADS, CHUNK, CHUNK), lambda i: (0, 0, 0)),
             pl.BlockSpec((A_HEADS, CHUNK, 1), lambda i: (0, 0, 0)), pl.BlockSpec((1, 512), lambda i: (0, 0))]
    return pl.pallas_call(
        body, name="sgu_backward",
        out_shape=(jax.ShapeDtypeStruct(dproj.shape, BF16),
                   jax.ShapeDtypeStruct((A_HEADS, HEAD), F32), jax.ShapeDtypeStruct((A_HEADS, CHUNK, CHUNK), F32),
                   jax.ShapeDtypeStruct((A_HEADS, CHUNK, 1), F32), jax.ShapeDtypeStruct((1, 512), F32)),
        grid=(s_len // tm,),
        in_specs=_sgu_in_specs(tm) + [pl.BlockSpec((tm, 512), lambda i: (i, 0)), pl.BlockSpec(memory_space=pl.ANY)],
        out_specs=(pl.BlockSpec((tm, W_A), lambda i: (i, O_A // W_A)), *small),
        input_output_aliases={6: 0},
        compiler_params=_params("arbitrary"),
    )(proj, gain, ws, bs, ga, dy, dproj)


def _halo_specs(tm, width, col, n_rows):
    per = tm // HALO
    last = n_rows // HALO - 1
    return [pl.BlockSpec((HALO, width), lambda i: (jnp.maximum(i * per - 1, 0), col)),
            pl.BlockSpec((tm, width), lambda i: (i, col)),
            pl.BlockSpec((HALO, width), lambda i: (jnp.minimum((i + 1) * per, last), col))]


def _conv_masks(tm, s_len):
    r = lax.broadcasted_iota(jnp.int32, (tm + 2 * HALO, 1), 0)
    g = pl.program_id(0) * tm - HALO + r
    return (g >= 0) & (g < s_len), (r >= HALO) & (r < HALO + tm)


def _conv_inputs(b_refs, cw_ref, cb_ref, gb_ref):
    ext = jnp.concatenate([r[...] for r in b_refs], axis=0).astype(F32)
    bb, bc, bh, bz = (ext[:, j * 512:(j + 1) * 512] for j in range(4))
    prm = (cw_ref[0:1, :], cw_ref[1:2, :], cw_ref[2:3, :], cb_ref[...], gb_ref[...])
    return (bb, bc, bh, bz), prm


def _conv_forward(proj, cw, cb, gb, y, tm):
    s_len = proj.shape[0]

    def body(p0, p1, p2, cw_ref, cb_ref, gb_ref, _, o_ref):
        acts, prm = _conv_inputs((p0, p1, p2), cw_ref, cb_ref, gb_ref)
        valid, core = _conv_masks(tm, s_len)
        out = _conv_tile(*acts, *prm, *prm, valid, core)
        o_ref[...] = out[HALO:HALO + tm].astype(BF16)

    vec = pl.BlockSpec((1, 512), lambda i: (0, 0))
    return pl.pallas_call(
        body, name="conv_forward",
        out_shape=jax.ShapeDtypeStruct(y.shape, BF16),
        grid=(s_len // tm,),
        in_specs=_halo_specs(tm, W_B, O_B // W_B, s_len) + [pl.BlockSpec((3, 512), lambda i: (0, 0)), vec, vec,
                                                             pl.BlockSpec(memory_space=pl.ANY)],
        out_specs=pl.BlockSpec((tm, 512), lambda i: (i, 1)),
        input_output_aliases={6: 0},
        compiler_params=_params("parallel"),
    )(proj, proj, proj, cw, cb, gb, y)


def _conv_backward(proj, cw, cb, gb, dy, tm):
    s_len = proj.shape[0]

    def body(p0, p1, p2, cw_ref, cb_ref, gb_ref, d0, d1, d2, db_ref, dcw_ref, dcb_ref, dgb_ref):
        acts, prm = _conv_inputs((p0, p1, p2), cw_ref, cb_ref, gb_ref)
        valid, core = _conv_masks(tm, s_len)
        _, vjp = jax.vjp(lambda a, p: _conv_tile(*a, *p, *prm, valid, core), acts, prm)
        dy_ext = jnp.where(valid, jnp.concatenate([d0[...], d1[...], d2[...]], axis=0), 0.0)
        dacts, dprm = vjp(dy_ext)
        for j in range(4):
            db_ref[:, j * 512:(j + 1) * 512] = dacts[j][HALO:HALO + tm].astype(BF16)
        first = pl.program_id(0) == 0
        _acc(dcw_ref, jnp.concatenate(dprm[0:3], axis=0), first)
        _acc(dcb_ref, dprm[3], first)
        _acc(dgb_ref, dprm[4], first)

    vec = pl.BlockSpec((1, 512), lambda i: (0, 0))
    mat = pl.BlockSpec((3, 512), lambda i: (0, 0))
    return pl.pallas_call(
        body, name="conv_backward",
        out_shape=(jax.ShapeDtypeStruct((s_len, PROJ_W), BF16), jax.ShapeDtypeStruct((3, 512), F32),
                   jax.ShapeDtypeStruct((1, 512), F32), jax.ShapeDtypeStruct((1, 512), F32)),
        grid=(s_len // tm,),
        in_specs=_halo_specs(tm, W_B, O_B // W_B, s_len) + [mat, vec, vec] + _halo_specs(tm, 512, 1, s_len),
        out_specs=(pl.BlockSpec((tm, W_B), lambda i: (i, O_B // W_B)), mat, vec, vec),
        compiler_params=_params("arbitrary"),
    )(proj, proj, proj, cw, cb, gb, dy, dy, dy)


def _cgate_forward(o, proj, gc, y, tm):
    s_len = o.shape[0]

    def body(o_ref, cz_ref, gc_ref, _, y_ref):
        y_ref[...] = _cgate_tile(o_ref[...], cz_ref[...].astype(F32), gc_ref[...]).astype(BF16)

    return pl.pallas_call(
        body, name="cgate_forward",
        out_shape=jax.ShapeDtypeStruct(y.shape, BF16),
        grid=(s_len // tm,),
        in_specs=[pl.BlockSpec((tm, W_CZ), lambda i: (i, 0)), pl.BlockSpec((tm, W_CZ), lambda i: (i, O_CZ // W_CZ)),
                  pl.BlockSpec((1, W_CZ), lambda i: (0, 0)), pl.BlockSpec(memory_space=pl.ANY)],
        out_specs=pl.BlockSpec((tm, W_CZ), lambda i: (i, 1)),
        input_output_aliases={3: 0},
        compiler_params=_params("parallel"),
    )(o, proj, gc, y)


def _cgate_backward(o, proj, gc, dy, dproj, tm, stat_chunk):
    s_len = o.shape[0]
    per_stat = stat_chunk // tm

    def body(o_ref, cz_ref, gc_ref, dy_ref, _, dcz_ref, do_ref, dsum_ref, dgc_ref):
        o = o_ref[...]
        _, vjp = jax.vjp(_cgate_tile, o, cz_ref[...].astype(F32), gc_ref[...])
        do, dcz, dgc = vjp(dy_ref[...])
        dcz_ref[...] = dcz.astype(BF16)
        do_ref[...] = do.astype(BF16)
        ones = jnp.ones((8, HEAD), F32)
        for h in range(C_HEADS):
            cols = slice(h * HEAD, (h + 1) * HEAD)
            sums = lax.dot_general(ones, do[:, cols] * o[:, cols], _NT, precision=lax.Precision.HIGHEST,
                                   preferred_element_type=F32)
            dsum_ref[h, 0] = sums[0:1]
        _acc(dgc_ref, dgc, pl.program_id(0) == 0)

    row = pl.BlockSpec((tm, W_CZ), lambda i: (i, 0))
    vec = pl.BlockSpec((1, W_CZ), lambda i: (0, 0))
    return pl.pallas_call(
        body, name="cgate_backward",
        out_shape=(jax.ShapeDtypeStruct(dproj.shape, BF16), jax.ShapeDtypeStruct((s_len, W_CZ), BF16),
                   jax.ShapeDtypeStruct((C_HEADS, s_len // stat_chunk, 1, stat_chunk), F32), jax.ShapeDtypeStruct((1, W_CZ), F32)),
        grid=(s_len // tm,),
        in_specs=[row, pl.BlockSpec((tm, W_CZ), lambda i: (i, O_CZ // W_CZ)), vec,
                  pl.BlockSpec((tm, W_CZ), lambda i: (i, 1)), pl.BlockSpec(memory_space=pl.ANY)],
        out_specs=(pl.BlockSpec((tm, W_CZ), lambda i: (i, O_CZ // W_CZ)), row,
                   pl.BlockSpec((C_HEADS, 1, 1, tm), lambda i: (0, i // per_stat, 0, i % per_stat)), vec),
        input_output_aliases={4: 0},
        compiler_params=_params("arbitrary"),
    )(o, proj, gc, dy, dproj)


def _mla_small_specs():
    return [pl.BlockSpec((KV_RANK, 2 * C_HEADS * HEAD), lambda i: (0, 0)), pl.BlockSpec((1, KV_RANK), lambda i: (0, 0)),
            pl.BlockSpec((1, HEAD), lambda i: (0, 0)), pl.BlockSpec((1, ROPE), lambda i: (0, 0)),
            pl.BlockSpec((1, HEAD), lambda i: (0, 0)), pl.BlockSpec((1, ROPE), lambda i: (0, 0))]


def _mla_load(m_ref, cos_ref, sin_ref):
    qn = m_ref[:, M_QN:M_QN + C_HEADS * HEAD].astype(F32)
    qr = m_ref[:, M_QR:M_QR + C_HEADS * ROPE].astype(F32)
    ckv = m_ref[:, M_CKV:M_CKV + KV_RANK].astype(F32)
    kr = m_ref[:, M_KR:M_KR + ROPE].astype(F32)
    return qn, qr, ckv, kr, cos_ref[...], sin_ref[...]


def _mla_forward(proj, cosf, sins, wukv, kvg, qng, qrg, kng, krg, tm, kt_chunk, vt_chunk):
    s_len = proj.shape[0]

    def body(m_ref, cos_ref, sin_ref, w_ref, kvg_ref, qng_ref, qrg_ref, kng_ref, krg_ref, q_ref, k_ref, v_ref, kt_ref, vt_ref):
        q, k, v = _mla_tile(*_mla_load(m_ref, cos_ref, sin_ref), w_ref[...], kvg_ref[...], qng_ref[...], qrg_ref[...],
                            kng_ref[...], krg_ref[...])
        for h in range(C_HEADS):
            q_ref[h] = q[h].astype(BF16)
            k_ref[h] = k[h].astype(BF16)
            v_ref[h] = v[h].astype(BF16)
            kt_ref[h, 0] = jnp.concatenate([k[h][:, :HEAD].T, k[h][:, HEAD:].T], axis=0).astype(BF16)
            vt_ref[h, 0] = v[h].T.astype(BF16)

    rope_spec = pl.BlockSpec((tm, ROPE), lambda i: (i, 0))
    qk_spec = pl.BlockSpec((C_HEADS, tm, QK), lambda i: (0, i, 0))
    per_k, per_v = kt_chunk // tm, vt_chunk // tm
    return pl.pallas_call(
        body, name="mla_forward",
        out_shape=(jax.ShapeDtypeStruct((C_HEADS, s_len, QK), BF16), jax.ShapeDtypeStruct((C_HEADS, s_len, QK), BF16),
                   jax.ShapeDtypeStruct((C_HEADS, s_len, HEAD), BF16),
                   jax.ShapeDtypeStruct((C_HEADS, s_len // kt_chunk, QK, kt_chunk), BF16),
                   jax.ShapeDtypeStruct((C_HEADS, s_len // vt_chunk, HEAD, vt_chunk), BF16)),
        grid=(s_len // tm,),
        in_specs=[pl.BlockSpec((tm, W_M), lambda i: (i, O_M // W_M)), rope_spec, rope_spec] + _mla_small_specs(),
        out_specs=(qk_spec, qk_spec, pl.BlockSpec((C_HEADS, tm, HEAD), lambda i: (0, i, 0)),
                   pl.BlockSpec((C_HEADS, 1, QK, tm), lambda i: (0, i // per_k, 0, i % per_k)),
                   pl.BlockSpec((C_HEADS, 1, HEAD, tm), lambda i: (0, i // per_v, 0, i % per_v))),
        compiler_params=_params("parallel"),
    )(proj, cosf, sins, wukv, kvg, qng, qrg, kng, krg)


def _mla_backward(proj, cosf, sins, wukv, kvg, qng, qrg, kng, krg, dqt, dk, dv, dproj, tm):
    s_len = proj.shape[0]
    per_chunk = dqt.shape[3] // tm

    def body(m_ref, cos_ref, sin_ref, w_ref, kvg_ref, qng_ref, qrg_ref, kng_ref, krg_ref, dq_ref, dk_ref, dv_ref, _,
             dm_ref, dw_ref, dkvg_ref, dqng_ref, dqrg_ref, dkng_ref, dkrg_ref):
        qn, qr, ckv, kr, cosf_t, sins_t = _mla_load(m_ref, cos_ref, sin_ref)
        prm = (w_ref[...], kvg_ref[...], qng_ref[...], qrg_ref[...], kng_ref[...], krg_ref[...])
        _, vjp = jax.vjp(lambda a, p: _mla_tile(*a, cosf_t, sins_t, *p), (qn, qr, ckv, kr), prm)
        heads = range(C_HEADS)
        dacts, dprm = vjp((tuple(dq_ref[h, 0].T for h in heads), tuple(dk_ref[h] for h in heads), tuple(dv_ref[h] for h in heads)))
        dm_ref[:, M_QN:M_QN + C_HEADS * HEAD] = dacts[0].astype(BF16)
        dm_ref[:, M_QR:M_QR + C_HEADS * ROPE] = dacts[1].astype(BF16)
        dm_ref[:, M_CKV:M_CKV + KV_RANK] = dacts[2].astype(BF16)
        pad = jnp.zeros((tm, W_M - M_KR - ROPE), F32)
        dm_ref[:, M_KR:W_M] = jnp.concatenate([dacts[3], pad], axis=-1).astype(BF16)
        first = pl.program_id(0) == 0
        for ref, val in zip((dw_ref, dkvg_ref, dqng_ref, dqrg_ref, dkng_ref, dkrg_ref), dprm):
            _acc(ref, val.astype(F32), first)

    rope_spec = pl.BlockSpec((tm, ROPE), lambda i: (i, 0))
    qk_spec = pl.BlockSpec((C_HEADS, tm, QK), lambda i: (0, i, 0))
    small = _mla_small_specs()
    return pl.pallas_call(
        body, name="mla_backward",
        out_shape=(jax.ShapeDtypeStruct(dproj.shape, BF16), jax.ShapeDtypeStruct((KV_RANK, 2 * C_HEADS * HEAD), F32),
                   jax.ShapeDtypeStruct((1, KV_RANK), F32), jax.ShapeDtypeStruct((1, HEAD), F32),
                   jax.ShapeDtypeStruct((1, ROPE), F32), jax.ShapeDtypeStruct((1, HEAD), F32),
                   jax.ShapeDtypeStruct((1, ROPE), F32)),
        grid=(s_len // tm,),
        in_specs=[pl.BlockSpec((tm, W_M), lambda i: (i, O_M // W_M)), rope_spec, rope_spec] + small
                 + [pl.BlockSpec((C_HEADS, 1, QK, tm), lambda i: (0, i // per_chunk, 0, i % per_chunk)), qk_spec,
                    pl.BlockSpec((C_HEADS, tm, HEAD), lambda i: (0, i, 0)), pl.BlockSpec(memory_space=pl.ANY)],
        out_specs=(pl.BlockSpec((tm, W_M), lambda i: (i, O_M // W_M)), *small),
        input_output_aliases={12: 0},
        compiler_params=_params("arbitrary"),
    )(proj, cosf, sins, wukv, kvg, qng, qrg, kng, krg, dqt, dk, dv, dproj)


def _attention_forward(q, k, vt, tq, stat_chunk, comm=None):
    n_heads, s_len, _ = q.shape
    n_chunks, _, ck = vt.shape[1:]
    c_ops, c_in_specs, c_shapes, c_out_specs, c_sems, c_begin, c_end = _riding_exchange(comm, 2)
    n_c = len(c_ops)

    def body(q_ref, k_ref, vt_ref, *rest):
        c_ins, (o_ref, lse_ref), c_outs, sems = rest[:n_c], rest[n_c:n_c + 2], rest[n_c + 2:2 * n_c + 2], rest[2 * n_c + 2:]
        c_begin(c_ins, c_outs, sems)
        q_t = q_ref[0]

        def step(j, carry):
            m_old, l_old, acc = carry
            k_j = k_ref[0, pl.ds(pl.multiple_of(j * ck, ck), ck), :]
            s = lax.dot_general(k_j, q_t, _NT, preferred_element_type=F32)
            m_new = jnp.maximum(m_old, jnp.max(s, axis=0, keepdims=True))
            p = jnp.exp2(s - m_new)
            alpha = jnp.exp2(m_old - m_new)
            l_new = alpha * l_old + jnp.sum(p, axis=0, keepdims=True)
            acc = alpha * acc + jnp.dot(vt_ref[0, j], p.astype(BF16), preferred_element_type=F32)
            return m_new, l_new, acc

        init = (jnp.full((1, tq), -jnp.inf, F32), jnp.zeros((1, tq), F32), jnp.zeros((HEAD, tq), F32))
        m_fin, l_fin, acc = lax.fori_loop(0, n_chunks, step, init)
        o_ref[...] = (acc / l_fin).T
        lse_ref[0, 0] = m_fin + jnp.log2(l_fin)
        c_end(c_ins, c_outs, sems)

    per_stat = stat_chunk // tq
    outs = pl.pallas_call(
        body, name="attention_forward",
        out_shape=(jax.ShapeDtypeStruct((s_len, n_heads * HEAD), F32),
                   jax.ShapeDtypeStruct((n_heads, s_len // stat_chunk, 1, stat_chunk), F32), *c_shapes),
        grid=(n_heads, s_len // tq),
        in_specs=[pl.BlockSpec((1, tq, QK), lambda h, i: (h, i, 0)), pl.BlockSpec((1, s_len, QK), lambda h, i: (h, 0, 0)),
                  pl.BlockSpec((1, n_chunks, HEAD, ck), lambda h, i: (h, 0, 0, 0))] + c_in_specs,
        out_specs=(pl.BlockSpec((tq, HEAD), lambda h, i: (i, h)),
                   pl.BlockSpec((1, 1, 1, tq), lambda h, i: (h, i // per_stat, 0, i % per_stat)), *c_out_specs),
        scratch_shapes=c_sems,
        compiler_params=_params("arbitrary", "arbitrary") if comm else _params("parallel", "parallel"),
    )(q, k, vt, *c_ops)
    return outs[0], outs[1], outs[2:]


def _attention_backward(q, k, kt, v, do, lse, dsum, comm=None):
    n_heads, s_len, _ = q.shape
    tk = kt.shape[3]
    n_q, _, cq = lse.shape[1:]
    c_ops, c_in_specs, c_shapes, c_out_specs, c_sems, c_begin, c_end = _riding_exchange(comm, 2)
    n_c = len(c_ops)

    def body(q_ref, k_ref, kt_ref, v_ref, do_ref, lse_ref, dsum_ref, *rest):
        c_ins, (dqt_ref, dk_ref, dv_ref), c_outs, sems = rest[:n_c], rest[n_c:n_c + 3], rest[n_c + 3:2 * n_c + 3], rest[2 * n_c + 3:]
        c_begin(c_ins, c_outs, sems)
        first = pl.program_id(1) == 0
        k_j, kt_j, v_j = k_ref[0], kt_ref[0, 0], v_ref[0]

        def step(i, carry):
            dk, dv = carry
            rows = pl.ds(pl.multiple_of(i * cq, cq), cq)
            q_i, do_i = q_ref[0, rows, :], do_ref[rows, :]
            s = lax.dot_general(k_j, q_i, _NT, preferred_element_type=F32)
            p = jnp.exp2(s - lse_ref[0, i])
            dp = lax.dot_general(v_j, do_i, _NT, preferred_element_type=F32)
            ds = (p * (dp - dsum_ref[0, i]) * LN_2).astype(BF16)
            dv = dv + jnp.dot(p.astype(BF16), do_i, preferred_element_type=F32)
            dk = dk + jnp.dot(ds, q_i, preferred_element_type=F32)
            _acc(dqt_ref.at[0, i], jnp.dot(kt_j, ds, preferred_element_type=F32), first)
            return dk, dv

        dk, dv = lax.fori_loop(0, n_q, step, (jnp.zeros((tk, QK), F32), jnp.zeros((tk, HEAD), F32)))
        dk_ref[0] = dk
        dv_ref[0] = dv
        c_end(c_ins, c_outs, sems)

    stat = pl.BlockSpec((1, n_q, 1, cq), lambda h, j: (h, 0, 0, 0))
    outs = pl.pallas_call(
        body, name="attention_backward",
        out_shape=(jax.ShapeDtypeStruct((n_heads, n_q, QK, cq), F32), jax.ShapeDtypeStruct((n_heads, s_len, QK), F32),
                   jax.ShapeDtypeStruct((n_heads, s_len, HEAD), F32), *c_shapes),
        grid=(n_heads, s_len // tk),
        in_specs=[pl.BlockSpec((1, s_len, QK), lambda h, j: (h, 0, 0)), pl.BlockSpec((1, tk, QK), lambda h, j: (h, j, 0)),
                  pl.BlockSpec((1, 1, QK, tk), lambda h, j: (h, j, 0, 0)),
                  pl.BlockSpec((1, tk, HEAD), lambda h, j: (h, j, 0)), pl.BlockSpec((s_len, HEAD), lambda h, j: (0, h)),
                  stat, stat] + c_in_specs,
        out_specs=(pl.BlockSpec((1, n_q, QK, cq), lambda h, j: (h, 0, 0, 0)), pl.BlockSpec((1, tk, QK), lambda h, j: (h, j, 0)),
                   pl.BlockSpec((1, tk, HEAD), lambda h, j: (h, j, 0)), *c_out_specs),
        scratch_shapes=c_sems,
        compiler_params=_params("arbitrary", "arbitrary") if comm else _params("parallel", "arbitrary"),
    )(q, k, kt, v, do, lse, dsum, *c_ops)
    return outs[0], outs[1], outs[2], outs[3:]


def _exchange(arrs, gather, name):
    n = len(arrs)

    def body(*refs):
        plan = _exchange_plan(refs[:n], refs[n:2 * n], gather, *refs[2 * n:])
        _exchange_start(plan)
        _exchange_wait(plan)

    any_spec = pl.BlockSpec(memory_space=pl.ANY)
    return pl.pallas_call(
        body, name=name,
        out_shape=_exchange_out_shapes(arrs, gather),
        in_specs=[any_spec] * n,
        out_specs=tuple([any_spec] * n),
        scratch_shapes=_exchange_semaphores(n),
        compiler_params=pltpu.CompilerParams(has_side_effects=True),
    )(*arrs)


def _gather_via_sibling(block, name):
    def body(x_ref, out_ref, send_sems, recv_sems, local_sem):
        x, y, c = lax.axis_index("x"), lax.axis_index("y"), lax.axis_index("c")
        me, sibling = (x, y, c), (x, y, 1 - c)
        chips = [(1 - x, y), (x, 1 - y), (1 - x, 1 - y)]

        def slot(px, py, pc):
            return out_ref.at[4 * px + 2 * py + pc]

        def copy(k, block_of, to, src=None):
            return pltpu.make_async_remote_copy(
                src_ref=slot(*block_of) if src is None else src, dst_ref=slot(*block_of), send_sem=send_sems.at[k],
                recv_sem=recv_sems.at[k], device_id=to, device_id_type=pl.DeviceIdType.MESH)

        mine = pltpu.make_async_copy(x_ref, slot(*me), local_sem)
        mine.start()
        first = [copy(0, me, sibling, src=x_ref)] + [copy(1 + j, me, (*chip, c), src=x_ref) for j, chip in enumerate(chips)]
        for cp in first:
            cp.start()
        passed = [copy(4 + j, (*chip, c), sibling) for j, chip in enumerate(chips)]
        for j, chip in enumerate(chips):
            copy(1 + j, (*chip, c), me).wait_recv()
            passed[j].start()
        copy(0, sibling, me).wait_recv()
        for j, chip in enumerate(chips):
            copy(4 + j, (*chip, 1 - c), me).wait_recv()
        for cp in first + passed:
            cp.wait_send()
        mine.wait()

    any_spec = pl.BlockSpec(memory_space=pl.ANY)
    return pl.pallas_call(
        body, name=name,
        out_shape=jax.ShapeDtypeStruct((N_DEV, *block.shape), block.dtype),
        in_specs=[any_spec], out_specs=any_spec,
        scratch_shapes=[pltpu.SemaphoreType.DMA((N_DEV - 1,)), pltpu.SemaphoreType.DMA((N_DEV - 1,)), pltpu.SemaphoreType.DMA],
        compiler_params=pltpu.CompilerParams(has_side_effects=True),
    )(block)


def _exchange_out_shapes(arrs, gather):
    return tuple(jax.ShapeDtypeStruct((N_DEV, *(a.shape if g else a.shape[1:])), a.dtype) for a, g in zip(arrs, gather))


def _exchange_semaphores(n):
    n_remote = n * (N_DEV - 1)
    return [pltpu.SemaphoreType.DMA((n_remote,)), pltpu.SemaphoreType.DMA((n_remote,)), pltpu.SemaphoreType.DMA((n,))]


def _exchange_plan(ins, outs, gather, send_sems, recv_sems, local_sems):
    n = len(ins)
    x, y, c = lax.axis_index("x"), lax.axis_index("y"), lax.axis_index("c")
    me = 4 * x + 2 * y + c

    def block_for(a, dev):
        return ins[a] if gather[a] else ins[a].at[dev]

    local = [pltpu.make_async_copy(block_for(a, me), outs[a].at[me], local_sems.at[a]) for a in range(n)]
    remote = []
    for k in range(1, N_DEV):
        px = 1 - x if k & 4 else x
        py = 1 - y if k & 2 else y
        pc = 1 - c if k & 1 else c
        peer = 4 * px + 2 * py + pc
        for a in range(n):
            idx = a * (N_DEV - 1) + k - 1
            send = pltpu.make_async_remote_copy(
                src_ref=block_for(a, peer), dst_ref=outs[a].at[me], send_sem=send_sems.at[idx], recv_sem=recv_sems.at[idx],
                device_id=(px, py, pc), device_id_type=pl.DeviceIdType.MESH)
            arrive = pltpu.make_async_remote_copy(
                src_ref=block_for(a, peer), dst_ref=outs[a].at[peer], send_sem=send_sems.at[idx], recv_sem=recv_sems.at[idx],
                device_id=(px, py, pc), device_id_type=pl.DeviceIdType.MESH)
            remote.append((send, arrive))
    return local, remote


def _exchange_start(plan):
    local, remote = plan
    for cp in local:
        cp.start()
    for send, _ in remote:
        send.start()


def _exchange_wait(plan):
    local, remote = plan
    for send, arrive in remote:
        send.wait_send()
        arrive.wait_recv()
    for cp in local:
        cp.wait()


def _riding_exchange(comm, n_grid):
    if comm is None:
        return [], [], (), (), [], lambda *_: None, lambda *_: None
    arrs, gather = comm
    n = len(arrs)
    any_spec = pl.BlockSpec(memory_space=pl.ANY)

    def begin(ins, outs, sems):
        @pl.when(functools.reduce(jnp.logical_and, [pl.program_id(d) == 0 for d in range(n_grid)]))
        def _():
            _exchange_start(_exchange_plan(ins, outs, gather, *sems))

    def end(ins, outs, sems):
        @pl.when(functools.reduce(jnp.logical_and, [pl.program_id(d) == pl.num_programs(d) - 1 for d in range(n_grid)]))
        def _():
            _exchange_wait(_exchange_plan(ins, outs, gather, *sems))

    return (list(arrs), [any_spec] * n, _exchange_out_shapes(arrs, gather), tuple([any_spec] * n), _exchange_semaphores(n),
            begin, end)


ADAM_TILE_ELEMS = 256 * 1024


def _sum_adam(parts, w, m, v, layer, prev, name):
    n_parts, r, c = parts.shape
    tm, tc = r, c
    if r % 16 == 0:
        while tm * c > ADAM_TILE_ELEMS and tm % 16 == 0:
            tm //= 2
    else:
        while r * tc > ADAM_TILE_ELEMS and tc % 256 == 0:
            tc //= 2

    def body(p_ref, w_ref, m_ref, v_ref, *rest):
        g_ref, d_ref, nm_ref, nv_ref = rest[-4:]
        g = p_ref[0].astype(F32)
        for s in range(1, n_parts):
            g = g + p_ref[s].astype(F32)
        m_new = ADAM_B1 * m_ref[...] + (1.0 - ADAM_B1) * g
        v_new = ADAM_B2 * v_ref[...] + (1.0 - ADAM_B2) * (g * g)
        m_hat = m_new / (1.0 - ADAM_B1 ** ADAM_STEP)
        v_hat = v_new / (1.0 - ADAM_B2 ** ADAM_STEP)
        g_ref[...] = g
        d_ref[...] = -ADAM_LR * (m_hat / (jnp.sqrt(v_hat) + ADAM_EPS) + ADAM_WD * w_ref[...])
        nm_ref[...] = m_new
        nv_ref[...] = v_new

    slab = pl.BlockSpec((None, tm, tc), lambda i, j: (layer, i, j))
    n_prev = 0 if prev is None else 4
    return pl.pallas_call(
        body, name=name,
        out_shape=(jax.ShapeDtypeStruct(w.shape, F32),) * 4,
        grid=(r // tm, c // tc),
        in_specs=[pl.BlockSpec((n_parts, tm, tc), lambda i, j: (0, i, j)), slab, slab, slab]
                 + [pl.BlockSpec(memory_space=pl.ANY)] * n_prev,
        out_specs=(slab, slab, slab, slab),
        input_output_aliases={4 + j: j for j in range(n_prev)},
        compiler_params=_params("parallel", "parallel"),
    )(parts, w, m, v, *(prev or ()))


def _permute_in(w):
    k = w.shape[0]
    q = w[:, 3584:5120].reshape(k, C_HEADS, QK)
    return jnp.concatenate(
        [w[:, 1536:3584], w[:, 5696:6720], w[:, 0:1536], q[:, :, :HEAD].reshape(k, C_HEADS * HEAD),
         q[:, :, HEAD:].reshape(k, C_HEADS * ROPE), w[:, 5120:5632], w[:, 5632:5696],
         jnp.zeros((k, PROJ_W - IN_WIDTH), w.dtype)], axis=1)


def _unpermute_in(g):
    k = g.shape[0]
    qn = g[:, O_M + M_QN:O_M + M_QR].reshape(k, C_HEADS, HEAD)
    qr = g[:, O_M + M_QR:O_M + M_CKV].reshape(k, C_HEADS, ROPE)
    q = jnp.concatenate([qn, qr], axis=-1).reshape(k, C_HEADS * QK)
    return jnp.concatenate(
        [g[:, O_A:O_A + W_A], g[:, O_B:O_B + W_B], q, g[:, O_M + M_CKV:O_M + M_KR],
         g[:, O_M + M_KR:O_M + M_KR + ROPE], g[:, O_CZ:O_CZ + W_CZ]], axis=1)


SMALL = ("attn_norm", "sgu_norm", "w_spatial", "b_spatial", "conv_b", "kv_norm", "q_nope_norm", "q_rope_norm",
         "k_nope_norm", "k_rope_norm", "out_norm", "ple_norm")
PACK_ROWS = 256


def _pack(tensors):
    flat = jnp.concatenate([t.reshape(-1) for t in tensors])
    rows = -(-flat.shape[0] // (128 * PACK_ROWS)) * PACK_ROWS
    return jnp.pad(flat, (0, rows * 128 - flat.shape[0])).reshape(rows, 128)


def _unpack(packed, like):
    flat = packed.reshape(-1)
    out, pos = [], 0
    for t in like:
        out.append(flat[pos:pos + t.size].reshape(t.shape))
        pos += t.size
    return out


def _tile(s_len, want):
    return min(want, s_len)


ATT_FWD_QUERIES = 512
ATT_FWD_KEYS = 8192
ATT_BWD_KEYS = 1024
ATT_BWD_QUERIES = 4096


def _layer_forward(h, p_l, cosf, sins, w, sm, comm, comm_rest):
    s_len = h.shape[0]
    tm = _tile(s_len, 512)
    proj, hn, rest = _norm_matmul(h, sm["attn_norm"], w["w_in"], _tile(s_len, 1024), 768, comm_rest)
    if comm_rest is not None:
        w = {**w, **_assemble_rest(rest)}
    ga, gb, gc = sm["out_norm"][:, 0:512], sm["out_norm"][:, 512:1024], sm["out_norm"][:, 1024:2048]
    y = _sgu_forward(proj, sm["sgu_norm"], sm["w_spatial"], sm["b_spatial"], ga, _tile(s_len, 256))
    y = _conv_forward(proj, w["conv_w"], sm["conv_b"], gb, y, _tile(s_len, 256))
    q, k, v, kt, vt = _mla_forward(proj, cosf, sins, w["w_ukv"], sm["kv_norm"], sm["q_nope_norm"], sm["q_rope_norm"],
                                   sm["k_nope_norm"], sm["k_rope_norm"], tm, _tile(s_len, ATT_BWD_KEYS),
                                   _tile(s_len, ATT_FWD_KEYS))
    o, lse, arrived = _attention_forward(q, k, vt, _tile(s_len, ATT_FWD_QUERIES), _tile(s_len, ATT_BWD_QUERIES), comm)
    y = _cgate_forward(o, proj, gc, y, _tile(s_len, 256))
    h1 = _out_matmul(h, y, w["w_out"], _tile(s_len, 1024), 1024)
    h2, n1, gate, pp = _ple_forward(h1, sm["ple_norm"], p_l, w["w_ple_gate"], w["w_ple_proj"], tm, 1024)
    saved = dict(h=h, hn=hn, proj=proj, y=y, q=q, k=k, v=v, kt=kt, o=o, lse=lse, h1=h1, n1=n1, gate=gate, pp=pp)
    return h2, saved, w, arrived


def _layer_backward(dh2, p_l, cosf, sins, w, sm, sv, comm, scatter_own):
    s_len = dh2.shape[0]
    tm = _tile(s_len, 512)
    tr = _tile(s_len, 256)
    big, small = {}, {}
    dh1, small["ple_norm"], dgp, dpp = _ple_backward(dh2, sv["gate"], sv["pp"], w["w_ple_gate"], sv["h1"], sm["ple_norm"], tm)
    big["w_ple_proj"], _ = _matmul_tn(p_l, dpp, _tile(s_len, 2048), PLE_DIM, 1024, "grad_w_ple_proj")
    big["w_ple_gate"], _ = _matmul_tn(sv["n1"], dgp, _tile(s_len, 2048), 1024, 1024, "grad_w_ple_gate")
    dy, _ = _matmul_nt(dh1, w["w_out"], _tile(s_len, 1024), 1024, "grad_branches")
    big["w_out"], _ = _matmul_tn(sv["y"], dh1, _tile(s_len, 2048), 1024, 1024, "grad_w_out")
    ga, gb, gc = sm["out_norm"][:, 0:512], sm["out_norm"][:, 512:1024], sm["out_norm"][:, 1024:2048]
    dproj, dcw, small["conv_b"], dgb = _conv_backward(sv["proj"], w["conv_w"], sm["conv_b"], gb, dy, tr)
    big["conv_w"] = dcw
    dproj, do, dsum, dgc = _cgate_backward(sv["o"], sv["proj"], gc, dy, dproj, tm, _tile(s_len, ATT_BWD_QUERIES))
    dqt, dk, dv, arrived = _attention_backward(sv["q"], sv["k"], sv["kt"], sv["v"], do, sv["lse"], dsum, comm)
    (dproj, big["w_ukv"], small["kv_norm"], small["q_nope_norm"], small["q_rope_norm"], small["k_nope_norm"],
     small["k_rope_norm"]) = _mla_backward(sv["proj"], cosf, sins, w["w_ukv"], sm["kv_norm"], sm["q_nope_norm"],
                                            sm["q_rope_norm"], sm["k_nope_norm"], sm["k_rope_norm"], dqt, dk, dv, dproj, tr)
    dproj, small["sgu_norm"], small["w_spatial"], small["b_spatial"], dga = _sgu_backward(
        sv["proj"], sm["sgu_norm"], sm["w_spatial"], sm["b_spatial"], ga, dy, dproj, tm)
    small["out_norm"] = jnp.concatenate([dga, dgb, dgc], axis=1)
    parts_rest = _parts_rest(big)
    g_in, arrived_rest = _matmul_tn(sv["hn"], dproj, _tile(s_len, 2048), 512, 2304, "grad_w_in",
                                    (parts_rest, [False] * len(parts_rest)) if scatter_own else None)
    parts = [_part_w_in(g_in)] + parts_rest
    dhn, arrived_in = _matmul_nt(dproj, w["w_in"], _tile(s_len, 1024), 256, "grad_attn_norm_in",
                                 (parts[:1], [False]) if scatter_own else None)
    dh, small["attn_norm"] = _rms_backward(sv["h"], sm["attn_norm"], dhn, dh1, tr, "attn_norm_backward")
    return dh, parts, big["conv_w"], small, arrived, (*arrived_in, *arrived_rest) if scatter_own else None


def _layer_small(params, layer):
    return dict(
        attn_norm=params["attn_norm"][layer][None, :], sgu_norm=params["sgu_norm"][layer],
        w_spatial=params["w_spatial"][layer], b_spatial=params["b_spatial"][layer][:, :, None],
        conv_b=params["conv_b"][layer][None, :], kv_norm=params["kv_norm"][layer][None, :],
        q_nope_norm=params["q_nope_norm"][layer][None, :], q_rope_norm=params["q_rope_norm"][layer][None, :],
        k_nope_norm=params["k_nope_norm"][layer][None, :], k_rope_norm=params["k_rope_norm"][layer][None, :],
        out_norm=params["out_norm"][layer][None, :], ple_norm=params["ple_norm"][layer][None, :])


BIG = ("w_in", "w_ukv", "w_out", "w_ple_gate", "w_ple_proj")


def _assemble_w_in(g_in):
    return _permute_in(g_in.transpose(1, 0, 2).reshape(g_in.shape[1], IN_WIDTH))


def _assemble_rest(gathered):
    g_ukv, g_out, g_gate, g_proj = gathered
    w_ukv = g_ukv.reshape(N_DEV, KV_RANK, 2, HEAD).transpose(1, 2, 0, 3).reshape(KV_RANK, 2 * C_HEADS * HEAD)
    return dict(w_ukv=w_ukv, w_out=g_out.reshape(D_MODEL, D_MODEL), w_ple_gate=g_gate.reshape(D_MODEL, D_MODEL),
                w_ple_proj=g_proj.transpose(1, 0, 2).reshape(PLE_DIM, D_MODEL))


def _part_w_in(g_in):
    return _unpermute_in(g_in).reshape(g_in.shape[0], N_DEV, -1).transpose(1, 2, 0).astype(BF16)


def _parts_rest(big):
    return [
        big["w_ukv"].reshape(KV_RANK, 2, N_DEV, HEAD).transpose(2, 0, 1, 3).reshape(N_DEV, KV_RANK, 2 * HEAD).astype(BF16),
        big["w_out"].reshape(N_DEV, -1, D_MODEL).astype(BF16),
        big["w_ple_gate"].reshape(N_DEV, -1, D_MODEL).astype(BF16),
        big["w_ple_proj"].reshape(PLE_DIM, N_DEV, -1).transpose(1, 0, 2).astype(BF16)]


def _step_local(xs, ps, pos, target, shards, conv_w, params):
    inv = 1.0 / (ROPE_BASE ** (jnp.arange(0, ROPE, 2, dtype=F32) / ROPE))
    ang = pos.astype(F32)[:, None] * inv
    cos, sin = jnp.cos(ang), jnp.sin(ang)
    cosf = jnp.concatenate([cos, cos], axis=-1)
    sins = jnp.concatenate([-sin, sin], axis=-1)

    def gather_of(names, layer):
        return [shards[n][layer] for n in names], [True] * len(names)

    h = xs
    saved, weights = [], []
    smalls = [_layer_small(params, layer) for layer in range(DEPTH)]
    first_w_in = _gather_via_sibling(shards["w_in"][0], "gather_first_w_in")
    w = dict(w_in=_assemble_w_in(first_w_in), conv_w=conv_w[0])
    for layer in range(DEPTH):
        comm = gather_of(BIG, layer + 1) if layer + 1 < DEPTH else None
        comm_rest = gather_of(BIG[1:], 0) if layer == 0 else None
        h, sv, w, arrived = _layer_forward(h, ps[layer], cosf, sins, w, smalls[layer], comm, comm_rest)
        saved.append(sv)
        weights.append(w)
        if comm is not None:
            w = dict(w_in=_assemble_w_in(arrived[0]), conv_w=conv_w[layer + 1], **_assemble_rest(arrived[1:]))
    dh, loss = _loss_grad(h, target, _tile(h.shape[0], 512))
    received, conv_grads, small_grads = [None] * DEPTH, [None] * DEPTH, [None] * DEPTH
    comm = None
    for layer in reversed(range(DEPTH)):
        dh, parts, conv_grads[layer], small_grads[layer], arrived, arrived_own = _layer_backward(
            dh, ps[layer], cosf, sins, weights[layer], smalls[layer], saved[layer], comm, layer == 0)
        if comm is not None:
            received[layer + 1] = arrived
        comm = (parts, [False] * len(parts))
    received[0] = arrived_own
    return loss, dh, received, conv_grads, small_grads


def kernel(x, p, positions, attn_norm, w_in, sgu_norm, w_spatial, b_spatial, conv_w, conv_b, kv_norm, w_ukv, q_nope_norm, q_rope_norm, k_nope_norm, k_rope_norm, out_norm, w_out, ple_norm, w_ple_gate, w_ple_proj, loss_target, m_attn_norm, m_w_in, m_sgu_norm, m_w_spatial, m_b_spatial, m_conv_w, m_conv_b, m_kv_norm, m_w_ukv, m_q_nope_norm, m_q_rope_norm, m_k_nope_norm, m_k_rope_norm, m_out_norm, m_w_out, m_ple_norm, m_w_ple_gate, m_w_ple_proj, v_attn_norm, v_w_in, v_sgu_norm, v_w_spatial, v_b_spatial, v_conv_w, v_conv_b, v_kv_norm, v_w_ukv, v_q_nope_norm, v_q_rope_norm, v_k_nope_norm, v_k_rope_norm, v_out_norm, v_w_out, v_ple_norm, v_w_ple_gate, v_w_ple_proj):
    order = ("attn_norm", "w_in", "sgu_norm", "w_spatial", "b_spatial", "conv_w", "conv_b", "kv_norm", "w_ukv",
             "q_nope_norm", "q_rope_norm", "k_nope_norm", "k_rope_norm", "out_norm", "w_out", "ple_norm", "w_ple_gate",
             "w_ple_proj")
    wts = dict(zip(order, (attn_norm, w_in, sgu_norm, w_spatial, b_spatial, conv_w, conv_b, kv_norm, w_ukv, q_nope_norm,
                           q_rope_norm, k_nope_norm, k_rope_norm, out_norm, w_out, ple_norm, w_ple_gate, w_ple_proj)))
    mom = dict(zip(order, (m_attn_norm, m_w_in, m_sgu_norm, m_w_spatial, m_b_spatial, m_conv_w, m_conv_b, m_kv_norm, m_w_ukv,
                           m_q_nope_norm, m_q_rope_norm, m_k_nope_norm, m_k_rope_norm, m_out_norm, m_w_out, m_ple_norm,
                           m_w_ple_gate, m_w_ple_proj)))
    var = dict(zip(order, (v_attn_norm, v_w_in, v_sgu_norm, v_w_spatial, v_b_spatial, v_conv_w, v_conv_b, v_kv_norm, v_w_ukv,
                           v_q_nope_norm, v_q_rope_norm, v_k_nope_norm, v_k_rope_norm, v_out_norm, v_w_out, v_ple_norm,
                           v_w_ple_gate, v_w_ple_proj)))

    conv_shard = wts["conv_w"]
    (conv_all,) = _exchange([conv_shard.reshape(-1, 128)], [True], "gather_conv_w")
    conv_full = conv_all.reshape(N_DEV, DEPTH, 3, -1).transpose(1, 2, 0, 3).reshape(DEPTH, 3, -1)
    shards = {n: wts[n].astype(BF16) for n in BIG}
    loss_part, grad_x, received, conv_grads, small_grads = _step_local(
        x[0], p[:, 0], positions[0], loss_target[0], shards, conv_full, wts)
    loss = lax.psum(loss_part[0, 0], ("x", "y", "c"))

    def small_grad(name):
        g = jnp.stack([sg[name] for sg in small_grads])
        return g.reshape(wts[name].shape)

    conv_grad = jnp.stack(conv_grads)
    like = [wts[n] for n in SMALL] + [conv_grad]
    packed = _pack([small_grad(n) for n in SMALL] + [conv_grad])
    (small_parts,) = _exchange([packed], [True], "gather_small_grads")
    filler = [jnp.zeros_like(conv_grad), jnp.zeros_like(conv_grad), jnp.ones_like(conv_grad)]
    small_out = _sum_adam(small_parts, *(_pack([src[n] for n in SMALL] + [fill])[None] for src, fill in zip((wts, mom, var), filler)),
                          0, None, "adam_small")
    unpacked = [_unpack(o[0], like) for o in small_out]
    results = {n: vals for n, vals in zip(SMALL, zip(*[u[:-1] for u in unpacked]))}
    me = 4 * lax.axis_index("x") + 2 * lax.axis_index("y") + lax.axis_index("c")
    width = conv_shard.shape[2]
    conv_local = lax.dynamic_slice_in_dim(unpacked[0][-1], me * width, width, axis=2)
    as_slab = (lambda t: t.reshape(1, -1, width))
    conv_out = _sum_adam(as_slab(conv_local), as_slab(conv_shard), as_slab(mom["conv_w"]), as_slab(var["conv_w"]), 0, None,
                         "adam_conv_w")
    results["conv_w"] = tuple(o.reshape(conv_shard.shape) for o in conv_out)

    for j, name in enumerate(BIG):
        view = (lambda t: jnp.swapaxes(t, 1, 2)) if name == "w_in" else (lambda t: t)
        outs = None
        for layer in range(DEPTH):
            outs = _sum_adam(received[layer][j], view(wts[name]), view(mom[name]), view(var[name]), layer, outs, "adam_" + name)
        results[name] = tuple(view(o) for o in outs)

    grads, deltas, new_m, new_v = ([results[n][j] for n in order] for j in range(4))
    return (loss, grad_x[None], *grads, *deltas, *new_m, *new_v)
```

```python
import functools

import jax
import jax.numpy as jnp
from jax import lax
from jax.experimental import pallas as pl
from jax.experimental.pallas import tpu as pltpu

F32 = jnp.float32
BF16 = jnp.bfloat16

N_DEV = 8
DEPTH = 4
D_MODEL = 2048
EPS = 1e-6
CHUNK = 128
A_HEADS = 4
HEAD = 128
ROPE = 64
HALF = ROPE // 2
C_HEADS = 8
KV_RANK = 512
PLE_DIM = 256
ROPE_BASE = 10000.0
IN_WIDTH = 6720
QK = HEAD + ROPE
SCALE = QK ** -0.5
LOG2_E = 1.4426950408889634
LN_2 = 0.6931471805599453
Q_SCALE = SCALE * LOG2_E
HALO = 8

O_B = 0
W_B = 2048
O_CZ = 2048
W_CZ = 1024
O_A = 3072
W_A = 1536
O_M = 4608
W_M = 2304
M_QN, M_QR, M_CKV, M_KR = 0, 1024, 1536, 2048
PROJ_W = 6912

ADAM_LR = 0.001
ADAM_B1 = 0.9
ADAM_B2 = 0.999
ADAM_EPS = 1e-08
ADAM_WD = 0.01
ADAM_STEP = 10

VMEM_LIMIT = 56 * 1024 * 1024

_NT = (((1,), (1,)), ((), ()))
_TN = (((0,), (0,)), ((), ()))


def _params(*sem):
    return pltpu.CompilerParams(dimension_semantics=sem, vmem_limit_bytes=VMEM_LIMIT)


@jax.custom_vjp
def _bdot(a, b):
    return jnp.dot(a.astype(BF16), b.astype(BF16), preferred_element_type=F32)


def _bdot_fwd(a, b):
    return _bdot(a, b), (a, b)


def _bdot_bwd(res, g):
    a, b = res
    gb = g.astype(BF16)
    da = lax.dot_general(gb, b.astype(BF16), _NT, preferred_element_type=F32)
    db = lax.dot_general(a.astype(BF16), gb, _TN, preferred_element_type=F32)
    return da.astype(a.dtype), db.astype(b.dtype)


_bdot.defvjp(_bdot_fwd, _bdot_bwd)


@functools.partial(jax.custom_vjp, nondiff_argnums=(1,))
def _split(x, n):
    w = x.shape[-1] // n
    return tuple(x[:, i * w:(i + 1) * w] for i in range(n))


def _split_fwd(x, n):
    return _split(x, n), None


def _split_bwd(n, _, gs):
    return (jnp.concatenate(gs, axis=-1),)


_split.defvjp(_split_fwd, _split_bwd)


@functools.partial(jax.custom_vjp, nondiff_argnums=(1,))
def _shift_rows(x, k):
    return pltpu.roll(x, k % x.shape[0], 0)


def _shift_rows_fwd(x, k):
    return _shift_rows(x, k), None


def _shift_rows_bwd(k, _, g):
    return (_shift_rows(g, -k),)


_shift_rows.defvjp(_shift_rows_fwd, _shift_rows_bwd)


@jax.custom_vjp
def _swap_halves(x):
    h = x.shape[-1] // 2
    return jnp.concatenate([x[:, h:], x[:, :h]], axis=-1)


def _swap_halves_fwd(x):
    return _swap_halves(x), None


def _swap_halves_bwd(_, g):
    return (_swap_halves(g),)


_swap_halves.defvjp(_swap_halves_fwd, _swap_halves_bwd)


def _rms(x, g):
    return x * lax.rsqrt(jnp.mean(x * x, axis=-1, keepdims=True) + EPS) * g


def _rope(x, cosf, sins):
    return x * cosf + _swap_halves(x) * sins


def _sgu_chunk(u, v, z, gain, ws, bs, ga):
    ys = []
    for h in range(A_HEADS):
        vn = _rms(v[h], gain[h])
        s = _bdot(ws[h], vn) + bs[h]
        ys.append(u[h] * s * jax.nn.silu(z[h]))
    ss = sum(jnp.sum(y * y, axis=-1, keepdims=True) for y in ys) * (1.0 / (A_HEADS * HEAD))
    r = lax.rsqrt(ss + EPS)
    return tuple(ys[h] * r * ga[h] for h in range(A_HEADS))


def _conv_tile(bb, bc, bh, bz, w0, w1, w2, cb, gb, w0h, w1h, w2h, cbh, gbh, valid, core):
    t = jnp.where(valid, bc * bh, 0.0)
    y = (jnp.where(core, cb, cbh)
         + _shift_rows(t, 1) * jnp.where(core, w0, w0h)
         + t * jnp.where(core, w1, w1h)
         + _shift_rows(t, -1) * jnp.where(core, w2, w2h))
    return _rms(bb * y * jax.nn.silu(bz), jnp.where(core, gb, gbh))


def _cgate_tile(o, cz, gc):
    return _rms(o * jax.nn.silu(cz), gc)


def _mla_tile(qn, qr, ckv, kr, cosf, sins, wukv, kvg, qng, qrg, kng, krg):
    kv = _split(_bdot(_rms(ckv, kvg), wukv), 2 * C_HEADS)
    k_r = _rope(_rms(kr, krg), cosf, sins)
    qn_h = _split(qn, C_HEADS)
    qr_h = _split(qr, C_HEADS)
    q, k, v = [], [], []
    for h in range(C_HEADS):
        q.append(jnp.concatenate([_rms(qn_h[h], qng), _rope(_rms(qr_h[h], qrg), cosf, sins)], axis=-1) * Q_SCALE)
        k.append(jnp.concatenate([_rms(kv[h], kng), k_r], axis=-1))
        v.append(kv[C_HEADS + h])
    return tuple(q), tuple(k), tuple(v)


def _norm_matmul(h, gain, w, tm, tn, comm=None):
    s_len, k = h.shape
    n = w.shape[1]
    c_ops, c_in_specs, c_shapes, c_out_specs, c_sems, c_begin, c_end = _riding_exchange(comm, 2)
    n_c = len(c_ops)

    def body(h_ref, g_ref, w_ref, *rest):
        c_ins, (o_ref, hn_ref), c_outs, sems = rest[:n_c], rest[n_c:n_c + 2], rest[n_c + 2:2 * n_c + 2], rest[2 * n_c + 2:]
        c_begin(c_ins, c_outs, sems)

        @pl.when(pl.program_id(1) == 0)
        def _():
            hn_ref[...] = _rms(h_ref[...], g_ref[...]).astype(BF16)

        o_ref[...] = jnp.dot(hn_ref[...], w_ref[...], preferred_element_type=F32).astype(BF16)
        c_end(c_ins, c_outs, sems)

    outs = pl.pallas_call(
        body, name="norm_matmul",
        out_shape=(jax.ShapeDtypeStruct((s_len, n), BF16), jax.ShapeDtypeStruct((s_len, k), BF16), *c_shapes),
        grid=(s_len // tm, n // tn),
        in_specs=[pl.BlockSpec((tm, k), lambda i, j: (i, 0)), pl.BlockSpec((1, k), lambda i, j: (0, 0)),
                  pl.BlockSpec((k, tn), lambda i, j: (0, j))] + c_in_specs,
        out_specs=(pl.BlockSpec((tm, tn), lambda i, j: (i, j)), pl.BlockSpec((tm, k), lambda i, j: (i, 0)), *c_out_specs),
        scratch_shapes=c_sems,
        compiler_params=_params("arbitrary", "arbitrary") if comm else _params("parallel", "arbitrary"),
    )(h, gain, w, *c_ops)
    return outs[0], outs[1], outs[2:]


def _out_matmul(h, y, w, tm, tn):
    s_len, n = h.shape
    k = y.shape[1]

    def body(h_ref, y_ref, w_ref, o_ref):
        o_ref[...] = h_ref[...] + jnp.dot(y_ref[...], w_ref[...], preferred_element_type=F32)

    return pl.pallas_call(
        body, name="out_matmul",
        out_shape=jax.ShapeDtypeStruct((s_len, n), F32),
        grid=(s_len // tm, n // tn),
        in_specs=[pl.BlockSpec((tm, tn), lambda i, j: (i, j)), pl.BlockSpec((tm, k), lambda i, j: (i, 0)),
                  pl.BlockSpec((k, tn), lambda i, j: (0, j))],
        out_specs=pl.BlockSpec((tm, tn), lambda i, j: (i, j)),
        compiler_params=_params("parallel", "parallel"),
    )(h, y, w)


def _ple_forward(h1, gain, p, wg, wp, tm, tn):
    s_len, d = h1.shape
    kp = p.shape[1]

    def body(hrow_ref, g_ref, p_ref, wg_ref, wp_ref, o_ref, n1_ref, gate_ref, pp_ref):
        j = pl.program_id(1)

        @pl.when(j == 0)
        def _():
            n1_ref[...] = _rms(hrow_ref[...], g_ref[...]).astype(BF16)

        gate = jax.nn.sigmoid(jnp.dot(n1_ref[...], wg_ref[...], preferred_element_type=F32))
        pp = jnp.dot(p_ref[...].astype(BF16), wp_ref[...], preferred_element_type=F32)
        o_ref[...] = hrow_ref[:, pl.ds(pl.multiple_of(j * tn, tn), tn)] + gate * pp
        gate_ref[...] = gate.astype(BF16)
        pp_ref[...] = pp.astype(BF16)

    col = pl.BlockSpec((tm, tn), lambda i, j: (i, j))
    return pl.pallas_call(
        body, name="ple_forward",
        out_shape=(jax.ShapeDtypeStruct((s_len, d), F32), jax.ShapeDtypeStruct((s_len, d), BF16),
                   jax.ShapeDtypeStruct((s_len, d), BF16), jax.ShapeDtypeStruct((s_len, d), BF16)),
        grid=(s_len // tm, d // tn),
        in_specs=[pl.BlockSpec((tm, d), lambda i, j: (i, 0)), pl.BlockSpec((1, d), lambda i, j: (0, 0)),
                  pl.BlockSpec((tm, kp), lambda i, j: (i, 0)), pl.BlockSpec((d, tn), lambda i, j: (0, j)),
                  pl.BlockSpec((kp, tn), lambda i, j: (0, j))],
        out_specs=(col, pl.BlockSpec((tm, d), lambda i, j: (i, 0)), col, col),
        compiler_params=_params("parallel", "arbitrary"),
    )(h1, gain, p, wg, wp)


def _matmul_nt(a, b, tm, tk, name, comm=None):
    m, n = a.shape
    k = b.shape[0]
    c_ops, c_in_specs, c_shapes, c_out_specs, c_sems, c_begin, c_end = _riding_exchange(comm, 2)
    n_c = len(c_ops)

    def body(a_ref, b_ref, *rest):
        c_ins, o_ref, c_outs, sems = rest[:n_c], rest[n_c], rest[n_c + 1:2 * n_c + 1], rest[2 * n_c + 1:]
        c_begin(c_ins, c_outs, sems)
        o_ref[...] = lax.dot_general(a_ref[...].astype(BF16), b_ref[...].astype(BF16), _NT, preferred_element_type=F32)
        c_end(c_ins, c_outs, sems)

    outs = pl.pallas_call(
        body, name=name,
        out_shape=(jax.ShapeDtypeStruct((m, k), F32), *c_shapes),
        grid=(m // tm, k // tk),
        in_specs=[pl.BlockSpec((tm, n), lambda i, j: (i, 0)), pl.BlockSpec((tk, n), lambda i, j: (j, 0))] + c_in_specs,
        out_specs=(pl.BlockSpec((tm, tk), lambda i, j: (i, j)), *c_out_specs),
        scratch_shapes=c_sems,
        compiler_params=_params("arbitrary", "arbitrary") if comm else _params("parallel", "parallel"),
    )(a, b, *c_ops)
    return outs[0], outs[1:]


def _matmul_tn(a, b, tm, tk, tn, name, comm=None):
    m, k = a.shape
    n = b.shape[1]
    n_m = m // tm
    c_ops, c_in_specs, c_shapes, c_out_specs, c_sems, c_begin, c_end = _riding_exchange(comm, 3)
    n_c = len(c_ops)

    def body(a_ref, b_ref, *rest):
        c_ins, o_ref, c_outs, acc_ref, sems = rest[:n_c], rest[n_c], rest[n_c + 1:2 * n_c + 1], rest[2 * n_c + 1], rest[2 * n_c + 2:]
        c_begin(c_ins, c_outs, sems)
        part = lax.dot_general(a_ref[...].astype(BF16), b_ref[...].astype(BF16), _TN, preferred_element_type=F32)
        _acc(acc_ref, part, pl.program_id(2) == 0)

        @pl.when(pl.program_id(2) == n_m - 1)
        def _():
            o_ref[...] = acc_ref[...].astype(BF16)

        c_end(c_ins, c_outs, sems)

    outs = pl.pallas_call(
        body, name=name,
        out_shape=(jax.ShapeDtypeStruct((k, n), BF16), *c_shapes),
        grid=(k // tk, n // tn, n_m),
        in_specs=[pl.BlockSpec((tm, tk), lambda kk, nn, mm: (mm, kk)), pl.BlockSpec((tm, tn), lambda kk, nn, mm: (mm, nn))]
                 + c_in_specs,
        out_specs=(pl.BlockSpec((tk, tn), lambda kk, nn, mm: (kk, nn)), *c_out_specs),
        scratch_shapes=[pltpu.VMEM((tk, tn), F32)] + c_sems,
        compiler_params=_params("arbitrary", "arbitrary", "arbitrary") if comm else _params("parallel", "parallel", "arbitrary"),
    )(a, b, *c_ops)
    return outs[0], outs[1:]


def _acc(ref, val, first):
    @pl.when(first)
    def _():
        ref[...] = val

    @pl.when(jnp.logical_not(first))
    def _():
        ref[...] += val


def _loss_grad(h, target, tm):
    s_len, d = h.shape

    def body(h_ref, t_ref, dh_ref, loss_ref):
        e = h_ref[...] - t_ref[...]
        dh_ref[...] = e * (1.0 / d)
        part = jnp.sum(jnp.sum(e * e, axis=-1, keepdims=True), axis=0, keepdims=True) * (0.5 / d)
        _acc(loss_ref, jnp.broadcast_to(part, loss_ref.shape), pl.program_id(0) == 0)

    row = pl.BlockSpec((tm, d), lambda i: (i, 0))
    return pl.pallas_call(
        body, name="loss_grad",
        out_shape=(jax.ShapeDtypeStruct((s_len, d), F32), jax.ShapeDtypeStruct((1, 128), F32)),
        grid=(s_len // tm,),
        in_specs=[row, row],
        out_specs=(row, pl.BlockSpec((1, 128), lambda i: (0, 0))),
        compiler_params=_params("arbitrary"),
    )(h, target)


def _rms_backward(x, gain, dn, dres, tm, name):
    s_len, d = x.shape

    def body(x_ref, g_ref, dn_ref, dres_ref, dx_ref, dg_ref):
        _, vjp = jax.vjp(_rms, x_ref[...], g_ref[...])
        dx, dg = vjp(dn_ref[...])
        dx_ref[...] = dres_ref[...] + dx
        _acc(dg_ref, dg, pl.program_id(0) == 0)

    row = pl.BlockSpec((tm, d), lambda i: (i, 0))
    vec = pl.BlockSpec((1, d), lambda i: (0, 0))
    return pl.pallas_call(
        body, name=name,
        out_shape=(jax.ShapeDtypeStruct((s_len, d), F32), jax.ShapeDtypeStruct((1, d), F32)),
        grid=(s_len // tm,),
        in_specs=[row, vec, row, row],
        out_specs=(row, vec),
        compiler_params=_params("arbitrary"),
    )(x, gain, dn, dres)


def _ple_backward(dh2, gate, pp, wg, h1, gain, tm):
    s_len, d = dh2.shape

    def body(dh_ref, gate_ref, pp_ref, wg_ref, h1_ref, g_ref, dh1_ref, dgain_ref, dgp_ref, dpp_ref):
        dh = dh_ref[...]
        gate = gate_ref[...].astype(F32)
        dgp = (dh * pp_ref[...].astype(F32) * gate * (1.0 - gate)).astype(BF16)
        dgp_ref[...] = dgp
        dpp_ref[...] = (dh * gate).astype(BF16)
        dn = lax.dot_general(dgp, wg_ref[...], _NT, preferred_element_type=F32)
        _, vjp = jax.vjp(_rms, h1_ref[...], g_ref[...])
        dx, dgain = vjp(dn)
        dh1_ref[...] = dh + dx
        _acc(dgain_ref, dgain, pl.program_id(0) == 0)

    row = pl.BlockSpec((tm, d), lambda i: (i, 0))
    vec = pl.BlockSpec((1, d), lambda i: (0, 0))
    return pl.pallas_call(
        body, name="ple_backward",
        out_shape=(jax.ShapeDtypeStruct((s_len, d), F32), jax.ShapeDtypeStruct((1, d), F32),
                   jax.ShapeDtypeStruct((s_len, d), BF16), jax.ShapeDtypeStruct((s_len, d), BF16)),
        grid=(s_len // tm,),
        in_specs=[row, row, row, pl.BlockSpec((d, d), lambda i: (0, 0)), row, vec],
        out_specs=(row, vec, row, row),
        compiler_params=_params("arbitrary"),
    )(dh2, gate, pp, wg, h1, gain)


def _sgu_in_specs(tm):
    return [pl.BlockSpec((tm, W_A), lambda i: (i, O_A // W_A)),
            pl.BlockSpec((A_HEADS, HEAD), lambda i: (0, 0)), pl.BlockSpec((A_HEADS, CHUNK, CHUNK), lambda i: (0, 0, 0)),
            pl.BlockSpec((A_HEADS, CHUNK, 1), lambda i: (0, 0, 0)), pl.BlockSpec((1, 512), lambda i: (0, 0))]


def _sgu_load(a_ref, gain_ref, ws_ref, bs_ref, ga_ref, c):
    rows = slice(c * CHUNK, (c + 1) * CHUNK)
    heads = range(A_HEADS)
    u = tuple(a_ref[rows, h * HEAD:(h + 1) * HEAD].astype(F32) for h in heads)
    v = tuple(a_ref[rows, 512 + h * HEAD:512 + (h + 1) * HEAD].astype(F32) for h in heads)
    z = tuple(a_ref[rows, 1024 + h * HEAD:1024 + (h + 1) * HEAD].astype(F32) for h in heads)
    gain = tuple(gain_ref[h:h + 1, :] for h in heads)
    ws = tuple(ws_ref[h] for h in heads)
    bs = tuple(bs_ref[h] for h in heads)
    ga = tuple(ga_ref[:, h * HEAD:(h + 1) * HEAD] for h in heads)
    return u, v, z, gain, ws, bs, ga


def _sgu_forward(proj, gain, ws, bs, ga, tm):
    s_len = proj.shape[0]

    def body(a_ref, gain_ref, ws_ref, bs_ref, ga_ref, o_ref):
        for c in range(tm // CHUNK):
            out = _sgu_chunk(*_sgu_load(a_ref, gain_ref, ws_ref, bs_ref, ga_ref, c))
            for h in range(A_HEADS):
                o_ref[c * CHUNK:(c + 1) * CHUNK, h * HEAD:(h + 1) * HEAD] = out[h].astype(BF16)

    return pl.pallas_call(
        body, name="sgu_forward",
        out_shape=jax.ShapeDtypeStruct((s_len, D_MODEL), BF16),
        grid=(s_len // tm,),
        in_specs=_sgu_in_specs(tm),
        out_specs=pl.BlockSpec((tm, 512), lambda i: (i, 0)),
        compiler_params=_params("parallel"),
    )(proj, gain, ws, bs, ga)


def _sgu_backward(proj, gain, ws, bs, ga, dy, dproj, tm):
    s_len = proj.shape[0]

    def body(a_ref, gain_ref, ws_ref, bs_ref, ga_ref, dy_ref, _, da_ref, dgain_ref, dws_ref, dbs_ref, dga_ref):
        tot = None
        for c in range(tm // CHUNK):
            args = _sgu_load(a_ref, gain_ref, ws_ref, bs_ref, ga_ref, c)
            _, vjp = jax.vjp(_sgu_chunk, *args)
            rows = slice(c * CHUNK, (c + 1) * CHUNK)
            du, dv, dz, dgain, dws, dbs, dga = vjp(tuple(dy_ref[rows, h * HEAD:(h + 1) * HEAD] for h in range(A_HEADS)))
            for h in range(A_HEADS):
                da_ref[rows, h * HEAD:(h + 1) * HEAD] = du[h].astype(BF16)
                da_ref[rows, 512 + h * HEAD:512 + (h + 1) * HEAD] = dv[h].astype(BF16)
                da_ref[rows, 1024 + h * HEAD:1024 + (h + 1) * HEAD] = dz[h].astype(BF16)
            part = (dgain, dws, dbs, dga)
            tot = part if tot is None else jax.tree.map(jnp.add, tot, part)
        dgain, dws, dbs, dga = tot
        first = pl.program_id(0) == 0
        _acc(dgain_ref, jnp.concatenate(dgain, axis=0), first)
        _acc(dga_ref, jnp.concatenate(dga, axis=-1), first)
        for h in range(A_HEADS):
            _acc(dws_ref.at[h], dws[h], first)
            _acc(dbs_ref.at[h], dbs[h], first)

    small = [pl.BlockSpec((A_HEADS, HEAD), lambda i: (0, 0)), pl.BlockSpec((A_HEADS, CHUNK, CHUNK), lambda i: (0, 0, 0)),
             pl.BlockSpec((A_HEADS, CHUNK, 1), lambda i: (0, 0, 0)), pl.BlockSpec((1, 512), lambda i: (0, 0))]
    return pl.pallas_call(
        body, name="sgu_backward",
        out_shape=(jax.ShapeDtypeStruct(dproj.shape, BF16),
                   jax.ShapeDtypeStruct((A_HEADS, HEAD), F32), jax.ShapeDtypeStruct((A_HEADS, CHUNK, CHUNK), F32),
                   jax.ShapeDtypeStruct((A_HEADS, CHUNK, 1), F32), jax.ShapeDtypeStruct((1, 512), F32)),
        grid=(s_len // tm,),
        in_specs=_sgu_in_specs(tm) + [pl.BlockSpec((tm, 512), lambda i: (i, 0)), pl.BlockSpec(memory_space=pl.ANY)],
        out_specs=(pl.BlockSpec((tm, W_A), lambda i: (i, O_A // W_A)), *small),
        input_output_aliases={6: 0},
        compiler_params=_params("arbitrary"),
    )(proj, gain, ws, bs, ga, dy, dproj)


def _halo_specs(tm, width, col, n_rows):
    per = tm // HALO
    last = n_rows // HALO - 1
    return [pl.BlockSpec((HALO, width), lambda i: (jnp.maximum(i * per - 1, 0), col)),
            pl.BlockSpec((tm, width), lambda i: (i, col)),
            pl.BlockSpec((HALO, width), lambda i: (jnp.minimum((i + 1) * per, last), col))]


def _conv_masks(tm, s_len):
    r = lax.broadcasted_iota(jnp.int32, (tm + 2 * HALO, 1), 0)
    g = pl.program_id(0) * tm - HALO + r
    return (g >= 0) & (g < s_len), (r >= HALO) & (r < HALO + tm)


def _conv_inputs(b_refs, cw_ref, cb_ref, gb_ref):
    ext = jnp.concatenate([r[...] for r in b_refs], axis=0).astype(F32)
    bb, bc, bh, bz = (ext[:, j * 512:(j + 1) * 512] for j in range(4))
    prm = (cw_ref[0:1, :], cw_ref[1:2, :], cw_ref[2:3, :], cb_ref[...], gb_ref[...])
    return (bb, bc, bh, bz), prm


def _conv_forward(proj, cw, cb, gb, y, tm):
    s_len = proj.shape[0]

    def body(p0, p1, p2, cw_ref, cb_ref, gb_ref, _, o_ref):
        acts, prm = _conv_inputs((p0, p1, p2), cw_ref, cb_ref, gb_ref)
        valid, core = _conv_masks(tm, s_len)
        out = _conv_tile(*acts, *prm, *prm, valid, core)
        o_ref[...] = out[HALO:HALO + tm].astype(BF16)

    vec = pl.BlockSpec((1, 512), lambda i: (0, 0))
    return pl.pallas_call(
        body, name="conv_forward",
        out_shape=jax.ShapeDtypeStruct(y.shape, BF16),
        grid=(s_len // tm,),
        in_specs=_halo_specs(tm, W_B, O_B // W_B, s_len) + [pl.BlockSpec((3, 512), lambda i: (0, 0)), vec, vec,
                                                             pl.BlockSpec(memory_space=pl.ANY)],
        out_specs=pl.BlockSpec((tm, 512), lambda i: (i, 1)),
        input_output_aliases={6: 0},
        compiler_params=_params("parallel"),
    )(proj, proj, proj, cw, cb, gb, y)


def _conv_backward(proj, cw, cb, gb, dy, tm):
    s_len = proj.shape[0]

    def body(p0, p1, p2, cw_ref, cb_ref, gb_ref, d0, d1, d2, db_ref, dcw_ref, dcb_ref, dgb_ref):
        acts, prm = _conv_inputs((p0, p1, p2), cw_ref, cb_ref, gb_ref)
        valid, core = _conv_masks(tm, s_len)
        _, vjp = jax.vjp(lambda a, p: _conv_tile(*a, *p, *prm, valid, core), acts, prm)
        dy_ext = jnp.where(valid, jnp.concatenate([d0[...], d1[...], d2[...]], axis=0), 0.0)
        dacts, dprm = vjp(dy_ext)
        for j in range(4):
            db_ref[:, j * 512:(j + 1) * 512] = dacts[j][HALO:HALO + tm].astype(BF16)
        first = pl.program_id(0) == 0
        _acc(dcw_ref, jnp.concatenate(dprm[0:3], axis=0), first)
        _acc(dcb_ref, dprm[3], first)
        _acc(dgb_ref, dprm[4], first)

    vec = pl.BlockSpec((1, 512), lambda i: (0, 0))
    mat = pl.BlockSpec((3, 512), lambda i: (0, 0))
    return pl.pallas_call(
        body, name="conv_backward",
        out_shape=(jax.ShapeDtypeStruct((s_len, PROJ_W), BF16), jax.ShapeDtypeStruct((3, 512), F32),
                   jax.ShapeDtypeStruct((1, 512), F32), jax.ShapeDtypeStruct((1, 512), F32)),
        grid=(s_len // tm,),
        in_specs=_halo_specs(tm, W_B, O_B // W_B, s_len) + [mat, vec, vec] + _halo_specs(tm, 512, 1, s_len),
        out_specs=(pl.BlockSpec((tm, W_B), lambda i: (i, O_B // W_B)), mat, vec, vec),
        compiler_params=_params("arbitrary"),
    )(proj, proj, proj, cw, cb, gb, dy, dy, dy)


def _cgate_forward(o, proj, gc, y, tm):
    s_len = o.shape[0]

    def body(o_ref, cz_ref, gc_ref, _, y_ref):
        y_ref[...] = _cgate_tile(o_ref[...], cz_ref[...].astype(F32), gc_ref[...]).astype(BF16)

    return pl.pallas_call(
        body, name="cgate_forward",
        out_shape=jax.ShapeDtypeStruct(y.shape, BF16),
        grid=(s_len // tm,),
        in_specs=[pl.BlockSpec((tm, W_CZ), lambda i: (i, 0)), pl.BlockSpec((tm, W_CZ), lambda i: (i, O_CZ // W_CZ)),
                  pl.BlockSpec((1, W_CZ), lambda i: (0, 0)), pl.BlockSpec(memory_space=pl.ANY)],
        out_specs=pl.BlockSpec((tm, W_CZ), lambda i: (i, 1)),
        input_output_aliases={3: 0},
        compiler_params=_params("parallel"),
    )(o, proj, gc, y)


def _cgate_backward(o, proj, gc, dy, dproj, tm, stat_chunk):
    s_len = o.shape[0]
    per_stat = stat_chunk // tm

    def body(o_ref, cz_ref, gc_ref, dy_ref, _, dcz_ref, do_ref, dsum_ref, dgc_ref):
        o = o_ref[...]
        _, vjp = jax.vjp(_cgate_tile, o, cz_ref[...].astype(F32), gc_ref[...])
        do, dcz, dgc = vjp(dy_ref[...])
        dcz_ref[...] = dcz.astype(BF16)
        do_ref[...] = do.astype(BF16)
        ones = jnp.ones((8, HEAD), F32)
        for h in range(C_HEADS):
            cols = slice(h * HEAD, (h + 1) * HEAD)
            sums = lax.dot_general(ones, do[:, cols] * o[:, cols], _NT, precision=lax.Precision.HIGHEST,
                                   preferred_element_type=F32)
            dsum_ref[h, 0] = sums[0:1]
        _acc(dgc_ref, dgc, pl.program_id(0) == 0)

    row = pl.BlockSpec((tm, W_CZ), lambda i: (i, 0))
    vec = pl.BlockSpec((1, W_CZ), lambda i: (0, 0))
    return pl.pallas_call(
        body, name="cgate_backward",
        out_shape=(jax.ShapeDtypeStruct(dproj.shape, BF16), jax.ShapeDtypeStruct((s_len, W_CZ), BF16),
                   jax.ShapeDtypeStruct((C_HEADS, s_len // stat_chunk, 1, stat_chunk), F32), jax.ShapeDtypeStruct((1, W_CZ), F32)),
        grid=(s_len // tm,),
        in_specs=[row, pl.BlockSpec((tm, W_CZ), lambda i: (i, O_CZ // W_CZ)), vec,
                  pl.BlockSpec((tm, W_CZ), lambda i: (i, 1)), pl.BlockSpec(memory_space=pl.ANY)],
        out_specs=(pl.BlockSpec((tm, W_CZ), lambda i: (i, O_CZ // W_CZ)), row,
                   pl.BlockSpec((C_HEADS, 1, 1, tm), lambda i: (0, i // per_stat, 0, i % per_stat)), vec),
        input_output_aliases={4: 0},
        compiler_params=_params("arbitrary"),
    )(o, proj, gc, dy, dproj)


def _mla_small_specs():
    return [pl.BlockSpec((KV_RANK, 2 * C_HEADS * HEAD), lambda i: (0, 0)), pl.BlockSpec((1, KV_RANK), lambda i: (0, 0)),
            pl.BlockSpec((1, HEAD), lambda i: (0, 0)), pl.BlockSpec((1, ROPE), lambda i: (0, 0)),
            pl.BlockSpec((1, HEAD), lambda i: (0, 0)), pl.BlockSpec((1, ROPE), lambda i: (0, 0))]


def _mla_load(m_ref, cos_ref, sin_ref):
    qn = m_ref[:, M_QN:M_QN + C_HEADS * HEAD].astype(F32)
    qr = m_ref[:, M_QR:M_QR + C_HEADS * ROPE].astype(F32)
    ckv = m_ref[:, M_CKV:M_CKV + KV_RANK].astype(F32)
    kr = m_ref[:, M_KR:M_KR + ROPE].astype(F32)
    return qn, qr, ckv, kr, cos_ref[...], sin_ref[...]


def _mla_forward(proj, cosf, sins, wukv, kvg, qng, qrg, kng, krg, tm, kt_chunk, vt_chunk):
    s_len = proj.shape[0]

    def body(m_ref, cos_ref, sin_ref, w_ref, kvg_ref, qng_ref, qrg_ref, kng_ref, krg_ref, q_ref, k_ref, v_ref, kt_ref, vt_ref):
        q, k, v = _mla_tile(*_mla_load(m_ref, cos_ref, sin_ref), w_ref[...], kvg_ref[...], qng_ref[...], qrg_ref[...],
                            kng_ref[...], krg_ref[...])
        for h in range(C_HEADS):
            q_ref[h] = q[h].astype(BF16)
            k_ref[h] = k[h].astype(BF16)
            v_ref[h] = v[h].astype(BF16)
            kt_ref[h, 0] = jnp.concatenate([k[h][:, :HEAD].T, k[h][:, HEAD:].T], axis=0).astype(BF16)
            vt_ref[h, 0] = v[h].T.astype(BF16)

    rope_spec = pl.BlockSpec((tm, ROPE), lambda i: (i, 0))
    qk_spec = pl.BlockSpec((C_HEADS, tm, QK), lambda i: (0, i, 0))
    per_k, per_v = kt_chunk // tm, vt_chunk // tm
    return pl.pallas_call(
        body, name="mla_forward",
        out_shape=(jax.ShapeDtypeStruct((C_HEADS, s_len, QK), BF16), jax.ShapeDtypeStruct((C_HEADS, s_len, QK), BF16),
                   jax.ShapeDtypeStruct((C_HEADS, s_len, HEAD), BF16),
                   jax.ShapeDtypeStruct((C_HEADS, s_len // kt_chunk, QK, kt_chunk), BF16),
                   jax.ShapeDtypeStruct((C_HEADS, s_len // vt_chunk, HEAD, vt_chunk), BF16)),
        grid=(s_len // tm,),
        in_specs=[pl.BlockSpec((tm, W_M), lambda i: (i, O_M // W_M)), rope_spec, rope_spec] + _mla_small_specs(),
        out_specs=(qk_spec, qk_spec, pl.BlockSpec((C_HEADS, tm, HEAD), lambda i: (0, i, 0)),
                   pl.BlockSpec((C_HEADS, 1, QK, tm), lambda i: (0, i // per_k, 0, i % per_k)),
                   pl.BlockSpec((C_HEADS, 1, HEAD, tm), lambda i: (0, i // per_v, 0, i % per_v))),
        compiler_params=_params("parallel"),
    )(proj, cosf, sins, wukv, kvg, qng, qrg, kng, krg)


def _mla_backward(proj, cosf, sins, wukv, kvg, qng, qrg, kng, krg, dqt, dk, dv, dproj, tm):
    s_len = proj.shape[0]
    per_chunk = dqt.shape[3] // tm

    def body(m_ref, cos_ref, sin_ref, w_ref, kvg_ref, qng_ref, qrg_ref, kng_ref, krg_ref, dq_ref, dk_ref, dv_ref, _,
             dm_ref, dw_ref, dkvg_ref, dqng_ref, dqrg_ref, dkng_ref, dkrg_ref):
        qn, qr, ckv, kr, cosf_t, sins_t = _mla_load(m_ref, cos_ref, sin_ref)
        prm = (w_ref[...], kvg_ref[...], qng_ref[...], qrg_ref[...], kng_ref[...], krg_ref[...])
        _, vjp = jax.vjp(lambda a, p: _mla_tile(*a, cosf_t, sins_t, *p), (qn, qr, ckv, kr), prm)
        heads = range(C_HEADS)
        dacts, dprm = vjp((tuple(dq_ref[h, 0].T for h in heads), tuple(dk_ref[h] for h in heads), tuple(dv_ref[h] for h in heads)))
        dm_ref[:, M_QN:M_QN + C_HEADS * HEAD] = dacts[0].astype(BF16)
        dm_ref[:, M_QR:M_QR + C_HEADS * ROPE] = dacts[1].astype(BF16)
        dm_ref[:, M_CKV:M_CKV + KV_RANK] = dacts[2].astype(BF16)
        pad = jnp.zeros((tm, W_M - M_KR - ROPE), F32)
        dm_ref[:, M_KR:W_M] = jnp.concatenate([dacts[3], pad], axis=-1).astype(BF16)
        first = pl.program_id(0) == 0
        for ref, val in zip((dw_ref, dkvg_ref, dqng_ref, dqrg_ref, dkng_ref, dkrg_ref), dprm):
            _acc(ref, val.astype(F32), first)

    rope_spec = pl.BlockSpec((tm, ROPE), lambda i: (i, 0))
    qk_spec = pl.BlockSpec((C_HEADS, tm, QK), lambda i: (0, i, 0))
    small = _mla_small_specs()
    return pl.pallas_call(
        body, name="mla_backward",
        out_shape=(jax.ShapeDtypeStruct(dproj.shape, BF16), jax.ShapeDtypeStruct((KV_RANK, 2 * C_HEADS * HEAD), F32),
                   jax.ShapeDtypeStruct((1, KV_RANK), F32), jax.ShapeDtypeStruct((1, HEAD), F32),
                   jax.ShapeDtypeStruct((1, ROPE), F32), jax.ShapeDtypeStruct((1, HEAD), F32),
                   jax.ShapeDtypeStruct((1, ROPE), F32)),
        grid=(s_len // tm,),
        in_specs=[pl.BlockSpec((tm, W_M), lambda i: (i, O_M // W_M)), rope_spec, rope_spec] + small
                 + [pl.BlockSpec((C_HEADS, 1, QK, tm), lambda i: (0, i // per_chunk, 0, i % per_chunk)), qk_spec,
                    pl.BlockSpec((C_HEADS, tm, HEAD), lambda i: (0, i, 0)), pl.BlockSpec(memory_space=pl.ANY)],
        out_specs=(pl.BlockSpec((tm, W_M), lambda i: (i, O_M // W_M)), *small),
        input_output_aliases={12: 0},
        compiler_params=_params("arbitrary"),
    )(proj, cosf, sins, wukv, kvg, qng, qrg, kng, krg, dqt, dk, dv, dproj)


def _attention_forward(q, k, vt, tq, stat_chunk, comm=None):
    n_heads, s_len, _ = q.shape
    n_chunks, _, ck = vt.shape[1:]
    c_ops, c_in_specs, c_shapes, c_out_specs, c_sems, c_begin, c_end = _riding_exchange(comm, 2)
    n_c = len(c_ops)

    def body(q_ref, k_ref, vt_ref, *rest):
        c_ins, (o_ref, lse_ref), c_outs, sems = rest[:n_c], rest[n_c:n_c + 2], rest[n_c + 2:2 * n_c + 2], rest[2 * n_c + 2:]
        c_begin(c_ins, c_outs, sems)
        q_t = q_ref[0]

        def step(j, carry):
            m_old, l_old, acc = carry
            k_j = k_ref[0, pl.ds(pl.multiple_of(j * ck, ck), ck), :]
            s = lax.dot_general(k_j, q_t, _NT, preferred_element_type=F32)
            m_new = jnp.maximum(m_old, jnp.max(s, axis=0, keepdims=True))
            p = jnp.exp2(s - m_new)
            alpha = jnp.exp2(m_old - m_new)
            l_new = alpha * l_old + jnp.sum(p, axis=0, keepdims=True)
            acc = alpha * acc + jnp.dot(vt_ref[0, j], p.astype(BF16), preferred_element_type=F32)
            return m_new, l_new, acc

        init = (jnp.full((1, tq), -jnp.inf, F32), jnp.zeros((1, tq), F32), jnp.zeros((HEAD, tq), F32))
        m_fin, l_fin, acc = lax.fori_loop(0, n_chunks, step, init)
        o_ref[...] = (acc / l_fin).T
        lse_ref[0, 0] = m_fin + jnp.log2(l_fin)
        c_end(c_ins, c_outs, sems)

    per_stat = stat_chunk // tq
    outs = pl.pallas_call(
        body, name="attention_forward",
        out_shape=(jax.ShapeDtypeStruct((s_len, n_heads * HEAD), F32),
                   jax.ShapeDtypeStruct((n_heads, s_len // stat_chunk, 1, stat_chunk), F32), *c_shapes),
        grid=(n_heads, s_len // tq),
        in_specs=[pl.BlockSpec((1, tq, QK), lambda h, i: (h, i, 0)), pl.BlockSpec((1, s_len, QK), lambda h, i: (h, 0, 0)),
                  pl.BlockSpec((1, n_chunks, HEAD, ck), lambda h, i: (h, 0, 0, 0))] + c_in_specs,
        out_specs=(pl.BlockSpec((tq, HEAD), lambda h, i: (i, h)),
                   pl.BlockSpec((1, 1, 1, tq), lambda h, i: (h, i // per_stat, 0, i % per_stat)), *c_out_specs),
        scratch_shapes=c_sems,
        compiler_params=_params("arbitrary", "arbitrary") if comm else _params("parallel", "parallel"),
    )(q, k, vt, *c_ops)
    return outs[0], outs[1], outs[2:]


def _attention_backward(q, k, kt, v, do, lse, dsum, comm=None):
    n_heads, s_len, _ = q.shape
    tk = kt.shape[3]
    n_q, _, cq = lse.shape[1:]
    c_ops, c_in_specs, c_shapes, c_out_specs, c_sems, c_begin, c_end = _riding_exchange(comm, 2)
    n_c = len(c_ops)

    def body(q_ref, k_ref, kt_ref, v_ref, do_ref, lse_ref, dsum_ref, *rest):
        c_ins, (dqt_ref, dk_ref, dv_ref), c_outs, sems = rest[:n_c], rest[n_c:n_c + 3], rest[n_c + 3:2 * n_c + 3], rest[2 * n_c + 3:]
        c_begin(c_ins, c_outs, sems)
        first = pl.program_id(1) == 0
        k_j, kt_j, v_j = k_ref[0], kt_ref[0, 0], v_ref[0]

        def step(i, carry):
            dk, dv = carry
            rows = pl.ds(pl.multiple_of(i * cq, cq), cq)
            q_i, do_i = q_ref[0, rows, :], do_ref[rows, :]
            s = lax.dot_general(k_j, q_i, _NT, preferred_element_type=F32)
            p = jnp.exp2(s - lse_ref[0, i])
            dp = lax.dot_general(v_j, do_i, _NT, preferred_element_type=F32)
            ds = (p * (dp - dsum_ref[0, i]) * LN_2).astype(BF16)
            dv = dv + jnp.dot(p.astype(BF16), do_i, preferred_element_type=F32)
            dk = dk + jnp.dot(ds, q_i, preferred_element_type=F32)
            _acc(dqt_ref.at[0, i], jnp.dot(kt_j, ds, preferred_element_type=F32), first)
            return dk, dv

        dk, dv = lax.fori_loop(0, n_q, step, (jnp.zeros((tk, QK), F32), jnp.zeros((tk, HEAD), F32)))
        dk_ref[0] = dk
        dv_ref[0] = dv
        c_end(c_ins, c_outs, sems)

    stat = pl.BlockSpec((1, n_q, 1, cq), lambda h, j: (h, 0, 0, 0))
    outs = pl.pallas_call(
        body, name="attention_backward",
        out_shape=(jax.ShapeDtypeStruct((n_heads, n_q, QK, cq), F32), jax.ShapeDtypeStruct((n_heads, s_len, QK), F32),
                   jax.ShapeDtypeStruct((n_heads, s_len, HEAD), F32), *c_shapes),
        grid=(n_heads, s_len // tk),
        in_specs=[pl.BlockSpec((1, s_len, QK), lambda h, j: (h, 0, 0)), pl.BlockSpec((1, tk, QK), lambda h, j: (h, j, 0)),
                  pl.BlockSpec((1, 1, QK, tk), lambda h, j: (h, j, 0, 0)),
                  pl.BlockSpec((1, tk, HEAD), lambda h, j: (h, j, 0)), pl.BlockSpec((s_len, HEAD), lambda h, j: (0, h)),
                  stat, stat] + c_in_specs,
        out_specs=(pl.BlockSpec((1, n_q, QK, cq), lambda h, j: (h, 0, 0, 0)), pl.BlockSpec((1, tk, QK), lambda h, j: (h, j, 0)),
                   pl.BlockSpec((1, tk, HEAD), lambda h, j: (h, j, 0)), *c_out_specs),
        scratch_shapes=c_sems,
        compiler_params=_params("arbitrary", "arbitrary") if comm else _params("parallel", "arbitrary"),
    )(q, k, kt, v, do, lse, dsum, *c_ops)
    return outs[0], outs[1], outs[2], outs[3:]


def _exchange(arrs, gather, name):
    n = len(arrs)

    def body(*refs):
        plan = _exchange_plan(refs[:n], refs[n:2 * n], gather, *refs[2 * n:])
        _exchange_start(plan)
        _exchange_wait(plan)

    any_spec = pl.BlockSpec(memory_space=pl.ANY)
    return pl.pallas_call(
        body, name=name,
        out_shape=_exchange_out_shapes(arrs, gather),
        in_specs=[any_spec] * n,
        out_specs=tuple([any_spec] * n),
        scratch_shapes=_exchange_semaphores(n),
        compiler_params=pltpu.CompilerParams(has_side_effects=True),
    )(*arrs)


def _gather_via_sibling(block, name):
    def body(x_ref, out_ref, send_sems, recv_sems, local_sem):
        x, y, c = lax.axis_index("x"), lax.axis_index("y"), lax.axis_index("c")
        me, sibling = (x, y, c), (x, y, 1 - c)
        chips = [(1 - x, y), (x, 1 - y), (1 - x, 1 - y)]

        def slot(px, py, pc):
            return out_ref.at[4 * px + 2 * py + pc]

        def copy(k, block_of, to, src=None):
            return pltpu.make_async_remote_copy(
                src_ref=slot(*block_of) if src is None else src, dst_ref=slot(*block_of), send_sem=send_sems.at[k],
                recv_sem=recv_sems.at[k], device_id=to, device_id_type=pl.DeviceIdType.MESH)

        mine = pltpu.make_async_copy(x_ref, slot(*me), local_sem)
        mine.start()
        first = [copy(0, me, sibling, src=x_ref)] + [copy(1 + j, me, (*chip, c), src=x_ref) for j, chip in enumerate(chips)]
        for cp in first:
            cp.start()
        passed = [copy(4 + j, (*chip, c), sibling) for j, chip in enumerate(chips)]
        for j, chip in enumerate(chips):
            copy(1 + j, (*chip, c), me).wait_recv()
            passed[j].start()
        copy(0, sibling, me).wait_recv()
        for j, chip in enumerate(chips):
            copy(4 + j, (*chip, 1 - c), me).wait_recv()
        for cp in first + passed:
            cp.wait_send()
        mine.wait()

    any_spec = pl.BlockSpec(memory_space=pl.ANY)
    return pl.pallas_call(
        body, name=name,
        out_shape=jax.ShapeDtypeStruct((N_DEV, *block.shape), block.dtype),
        in_specs=[any_spec], out_specs=any_spec,
        scratch_shapes=[pltpu.SemaphoreType.DMA((N_DEV - 1,)), pltpu.SemaphoreType.DMA((N_DEV - 1,)), pltpu.SemaphoreType.DMA],
        compiler_params=pltpu.CompilerParams(has_side_effects=True),
    )(block)


def _exchange_out_shapes(arrs, gather):
    return tuple(jax.ShapeDtypeStruct((N_DEV, *(a.shape if g else a.shape[1:])), a.dtype) for a, g in zip(arrs, gather))


def _exchange_semaphores(n):
    n_remote = n * (N_DEV - 1)
    return [pltpu.SemaphoreType.DMA((n_remote,)), pltpu.SemaphoreType.DMA((n_remote,)), pltpu.SemaphoreType.DMA((n,))]


def _exchange_plan(ins, outs, gather, send_sems, recv_sems, local_sems):
    n = len(ins)
    x, y, c = lax.axis_index("x"), lax.axis_index("y"), lax.axis_index("c")
    me = 4 * x + 2 * y + c

    def block_for(a, dev):
        return ins[a] if gather[a] else ins[a].at[dev]

    local = [pltpu.make_async_copy(block_for(a, me), outs[a].at[me], local_sems.at[a]) for a in range(n)]
    remote = []
    for k in range(1, N_DEV):
        px = 1 - x if k & 4 else x
        py = 1 - y if k & 2 else y
        pc = 1 - c if k & 1 else c
        peer = 4 * px + 2 * py + pc
        for a in range(n):
            idx = a * (N_DEV - 1) + k - 1
            send = pltpu.make_async_remote_copy(
                src_ref=block_for(a, peer), dst_ref=outs[a].at[me], send_sem=send_sems.at[idx], recv_sem=recv_sems.at[idx],
                device_id=(px, py, pc), device_id_type=pl.DeviceIdType.MESH)
            arrive = pltpu.make_async_remote_copy(
                src_ref=block_for(a, peer), dst_ref=outs[a].at[peer], send_sem=send_sems.at[idx], recv_sem=recv_sems.at[idx],
                device_id=(px, py, pc), device_id_type=pl.DeviceIdType.MESH)
            remote.append((send, arrive))
    return local, remote


def _exchange_start(plan):
    local, remote = plan
    for cp in local:
        cp.start()
    for send, _ in remote:
        send.start()


def _exchange_wait(plan):
    local, remote = plan
    for send, arrive in remote:
        send.wait_send()
        arrive.wait_recv()
    for cp in local:
        cp.wait()


def _riding_exchange(comm, n_grid):
    if comm is None:
        return [], [], (), (), [], lambda *_: None, lambda *_: None
    arrs, gather = comm
    n = len(arrs)
    any_spec = pl.BlockSpec(memory_space=pl.ANY)

    def begin(ins, outs, sems):
        @pl.when(functools.reduce(jnp.logical_and, [pl.program_id(d) == 0 for d in range(n_grid)]))
        def _():
            _exchange_start(_exchange_plan(ins, outs, gather, *sems))

    def end(ins, outs, sems):
        @pl.when(functools.reduce(jnp.logical_and, [pl.program_id(d) == pl.num_programs(d) - 1 for d in range(n_grid)]))
        def _():
            _exchange_wait(_exchange_plan(ins, outs, gather, *sems))

    return (list(arrs), [any_spec] * n, _exchange_out_shapes(arrs, gather), tuple([any_spec] * n), _exchange_semaphores(n),
            begin, end)


ADAM_TILE_ELEMS = 256 * 1024


def _sum_adam(parts, w, m, v, layer, prev, name):
    n_parts, r, c = parts.shape
    tm, tc = r, c
    if r % 16 == 0:
        while tm * c > ADAM_TILE_ELEMS and tm % 16 == 0:
            tm //= 2
    else:
        while r * tc > ADAM_TILE_ELEMS and tc % 256 == 0:
            tc //= 2

    def body(p_ref, w_ref, m_ref, v_ref, *rest):
        g_ref, d_ref, nm_ref, nv_ref = rest[-4:]
        g = p_ref[0].astype(F32)
        for s in range(1, n_parts):
            g = g + p_ref[s].astype(F32)
        m_new = ADAM_B1 * m_ref[...] + (1.0 - ADAM_B1) * g
        v_new = ADAM_B2 * v_ref[...] + (1.0 - ADAM_B2) * (g * g)
        m_hat = m_new / (1.0 - ADAM_B1 ** ADAM_STEP)
        v_hat = v_new / (1.0 - ADAM_B2 ** ADAM_STEP)
        g_ref[...] = g
        d_ref[...] = -ADAM_LR * (m_hat / (jnp.sqrt(v_hat) + ADAM_EPS) + ADAM_WD * w_ref[...])
        nm_ref[...] = m_new
        nv_ref[...] = v_new

    slab = pl.BlockSpec((None, tm, tc), lambda i, j: (layer, i, j))
    n_prev = 0 if prev is None else 4
    return pl.pallas_call(
        body, name=name,
        out_shape=(jax.ShapeDtypeStruct(w.shape, F32),) * 4,
        grid=(r // tm, c // tc),
        in_specs=[pl.BlockSpec((n_parts, tm, tc), lambda i, j: (0, i, j)), slab, slab, slab]
                 + [pl.BlockSpec(memory_space=pl.ANY)] * n_prev,
        out_specs=(slab, slab, slab, slab),
        input_output_aliases={4 + j: j for j in range(n_prev)},
        compiler_params=_params("parallel", "parallel"),
    )(parts, w, m, v, *(prev or ()))


def _permute_in(w):
    k = w.shape[0]
    q = w[:, 3584:5120].reshape(k, C_HEADS, QK)
    return jnp.concatenate(
        [w[:, 1536:3584], w[:, 5696:6720], w[:, 0:1536], q[:, :, :HEAD].reshape(k, C_HEADS * HEAD),
         q[:, :, HEAD:].reshape(k, C_HEADS * ROPE), w[:, 5120:5632], w[:, 5632:5696],
         jnp.zeros((k, PROJ_W - IN_WIDTH), w.dtype)], axis=1)


def _unpermute_in(g):
    k = g.shape[0]
    qn = g[:, O_M + M_QN:O_M + M_QR].reshape(k, C_HEADS, HEAD)
    qr = g[:, O_M + M_QR:O_M + M_CKV].reshape(k, C_HEADS, ROPE)
    q = jnp.concatenate([qn, qr], axis=-1).reshape(k, C_HEADS * QK)
    return jnp.concatenate(
        [g[:, O_A:O_A + W_A], g[:, O_B:O_B + W_B], q, g[:, O_M + M_CKV:O_M + M_KR],
         g[:, O_M + M_KR:O_M + M_KR + ROPE], g[:, O_CZ:O_CZ + W_CZ]], axis=1)


SMALL = ("attn_norm", "sgu_norm", "w_spatial", "b_spatial", "conv_b", "kv_norm", "q_nope_norm", "q_rope_norm",
         "k_nope_norm", "k_rope_norm", "out_norm", "ple_norm")
PACK_ROWS = 256


def _pack(tensors):
    flat = jnp.concatenate([t.reshape(-1) for t in tensors])
    rows = -(-flat.shape[0] // (128 * PACK_ROWS)) * PACK_ROWS
    return jnp.pad(flat, (0, rows * 128 - flat.shape[0])).reshape(rows, 128)


def _unpack(packed, like):
    flat = packed.reshape(-1)
    out, pos = [], 0
    for t in like:
        out.append(flat[pos:pos + t.size].reshape(t.shape))
        pos += t.size
    return out


def _tile(s_len, want):
    return min(want, s_len)


ATT_FWD_QUERIES = 512
ATT_FWD_KEYS = 8192
ATT_BWD_KEYS = 1024
ATT_BWD_QUERIES = 4096


def _layer_forward(h, p_l, cosf, sins, w, sm, comm, comm_rest):
    s_len = h.shape[0]
    tm = _tile(s_len, 512)
    proj, hn, rest = _norm_matmul(h, sm["attn_norm"], w["w_in"], _tile(s_len, 1024), 768, comm_rest)
    if comm_rest is not None:
        w = {**w, **_assemble_rest(rest)}
    ga, gb, gc = sm["out_norm"][:, 0:512], sm["out_norm"][:, 512:1024], sm["out_norm"][:, 1024:2048]
    y = _sgu_forward(proj, sm["sgu_norm"], sm["w_spatial"], sm["b_spatial"], ga, _tile(s_len, 256))
    y = _conv_forward(proj, w["conv_w"], sm["conv_b"], gb, y, _tile(s_len, 256))
    q, k, v, kt, vt = _mla_forward(proj, cosf, sins, w["w_ukv"], sm["kv_norm"], sm["q_nope_norm"], sm["q_rope_norm"],
                                   sm["k_nope_norm"], sm["k_rope_norm"], tm, _tile(s_len, ATT_BWD_KEYS),
                                   _tile(s_len, ATT_FWD_KEYS))
    o, lse, arrived = _attention_forward(q, k, vt, _tile(s_len, ATT_FWD_QUERIES), _tile(s_len, ATT_BWD_QUERIES), comm)
    y = _cgate_forward(o, proj, gc, y, _tile(s_len, 256))
    h1 = _out_matmul(h, y, w["w_out"], _tile(s_len, 1024), 1024)
    h2, n1, gate, pp = _ple_forward(h1, sm["ple_norm"], p_l, w["w_ple_gate"], w["w_ple_proj"], tm, 1024)
    saved = dict(h=h, hn=hn, proj=proj, y=y, q=q, k=k, v=v, kt=kt, o=o, lse=lse, h1=h1, n1=n1, gate=gate, pp=pp)
    return h2, saved, w, arrived


def _layer_backward(dh2, p_l, cosf, sins, w, sm, sv, comm, scatter_own):
    s_len = dh2.shape[0]
    tm = _tile(s_len, 512)
    tr = _tile(s_len, 256)
    big, small = {}, {}
    dh1, small["ple_norm"], dgp, dpp = _ple_backward(dh2, sv["gate"], sv["pp"], w["w_ple_gate"], sv["h1"], sm["ple_norm"], tm)
    big["w_ple_proj"], _ = _matmul_tn(p_l, dpp, _tile(s_len, 2048), PLE_DIM, 1024, "grad_w_ple_proj")
    big["w_ple_gate"], _ = _matmul_tn(sv["n1"], dgp, _tile(s_len, 2048), 1024, 1024, "grad_w_ple_gate")
    dy, _ = _matmul_nt(dh1, w["w_out"], _tile(s_len, 1024), 1024, "grad_branches")
    big["w_out"], _ = _matmul_tn(sv["y"], dh1, _tile(s_len, 2048), 1024, 1024, "grad_w_out")
    ga, gb, gc = sm["out_norm"][:, 0:512], sm["out_norm"][:, 512:1024], sm["out_norm"][:, 1024:2048]
    dproj, dcw, small["conv_b"], dgb = _conv_backward(sv["proj"], w["conv_w"], sm["conv_b"], gb, dy, tr)
    big["conv_w"] = dcw
    dproj, do, dsum, dgc = _cgate_backward(sv["o"], sv["proj"], gc, dy, dproj, tm, _tile(s_len, ATT_BWD_QUERIES))
    dqt, dk, dv, arrived = _attention_backward(sv["q"], sv["k"], sv["kt"], sv["v"], do, sv["lse"], dsum, comm)
    (dproj, big["w_ukv"], small["kv_norm"], small["q_nope_norm"], small["q_rope_norm"], small["k_nope_norm"],
     small["k_rope_norm"]) = _mla_backward(sv["proj"], cosf, sins, w["w_ukv"], sm["kv_norm"], sm["q_nope_norm"],
                                            sm["q_rope_norm"], sm["k_nope_norm"], sm["k_rope_norm"], dqt, dk, dv, dproj, tr)
    dproj, small["sgu_norm"], small["w_spatial"], small["b_spatial"], dga = _sgu_backward(
        sv["proj"], sm["sgu_norm"], sm["w_spatial"], sm["b_spatial"], ga, dy, dproj, tm)
    small["out_norm"] = jnp.concatenate([dga, dgb, dgc], axis=1)
    parts_rest = _parts_rest(big)
    g_in, arrived_rest = _matmul_tn(sv["hn"], dproj, _tile(s_len, 2048), 512, 2304, "grad_w_in",
                                    (parts_rest, [False] * len(parts_rest)) if scatter_own else None)
    parts = [_part_w_in(g_in)] + parts_rest
    dhn, arrived_in = _matmul_nt(dproj, w["w_in"], _tile(s_len, 1024), 256, "grad_attn_norm_in",
                                 (parts[:1], [False]) if scatter_own else None)
    dh, small["attn_norm"] = _rms_backward(sv["h"], sm["attn_norm"], dhn, dh1, tr, "attn_norm_backward")
    return dh, parts, big["conv_w"], small, arrived, (*arrived_in, *arrived_rest) if scatter_own else None


def _layer_small(params, layer):
    return dict(
        attn_norm=params["attn_norm"][layer][None, :], sgu_norm=params["sgu_norm"][layer],
        w_spatial=params["w_spatial"][layer], b_spatial=params["b_spatial"][layer][:, :, None],
        conv_b=params["conv_b"][layer][None, :], kv_norm=params["kv_norm"][layer][None, :],
        q_nope_norm=params["q_nope_norm"][layer][None, :], q_rope_norm=params["q_rope_norm"][layer][None, :],
        k_nope_norm=params["k_nope_norm"][layer][None, :], k_rope_norm=params["k_rope_norm"][layer][None, :],
        out_norm=params["out_norm"][layer][None, :], ple_norm=params["ple_norm"][layer][None, :])


BIG = ("w_in", "w_ukv", "w_out", "w_ple_gate", "w_ple_proj")


def _assemble_w_in(g_in):
    return _permute_in(g_in.transpose(1, 0, 2).reshape(g_in.shape[1], IN_WIDTH))


def _assemble_rest(gathered):
    g_ukv, g_out, g_gate, g_proj = gathered
    w_ukv = g_ukv.reshape(N_DEV, KV_RANK, 2, HEAD).transpose(1, 2, 0, 3).reshape(KV_RANK, 2 * C_HEADS * HEAD)
    return dict(w_ukv=w_ukv, w_out=g_out.reshape(D_MODEL, D_MODEL), w_ple_gate=g_gate.reshape(D_MODEL, D_MODEL),
                w_ple_proj=g_proj.transpose(1, 0, 2).reshape(PLE_DIM, D_MODEL))


def _part_w_in(g_in):
    return _unpermute_in(g_in).reshape(g_in.shape[0], N_DEV, -1).transpose(1, 2, 0).astype(BF16)


def _parts_rest(big):
    return [
        big["w_ukv"].reshape(KV_RANK, 2, N_DEV, HEAD).transpose(2, 0, 1, 3).reshape(N_DEV, KV_RANK, 2 * HEAD).astype(BF16),
        big["w_out"].reshape(N_DEV, -1, D_MODEL).astype(BF16),
        big["w_ple_gate"].reshape(N_DEV, -1, D_MODEL).astype(BF16),
        big["w_ple_proj"].reshape(PLE_DIM, N_DEV, -1).transpose(1, 0, 2).astype(BF16)]


def _step_local(xs, ps, pos, target, shards, conv_w, params):
    inv = 1.0 / (ROPE_BASE ** (jnp.arange(0, ROPE, 2, dtype=F32) / ROPE))
    ang = pos.astype(F32)[:, None] * inv
    cos, sin = jnp.cos(ang), jnp.sin(ang)
    cosf = jnp.concatenate([cos, cos], axis=-1)
    sins = jnp.concatenate([-sin, sin], axis=-1)

    def gather_of(names, layer):
        return [shards[n][layer] for n in names], [True] * len(names)

    h = xs
    saved, weights = [], []
    smalls = [_layer_small(params, layer) for layer in range(DEPTH)]
    first_w_in = _gather_via_sibling(shards["w_in"][0], "gather_first_w_in")
    w = dict(w_in=_assemble_w_in(first_w_in), conv_w=conv_w[0])
    for layer in range(DEPTH):
        comm = gather_of(BIG, layer + 1) if layer + 1 < DEPTH else None
        comm_rest = gather_of(BIG[1:], 0) if layer == 0 else None
        h, sv, w, arrived = _layer_forward(h, ps[layer], cosf, sins, w, smalls[layer], comm, comm_rest)
        saved.append(sv)
        weights.append(w)
        if comm is not None:
            w = dict(w_in=_assemble_w_in(arrived[0]), conv_w=conv_w[layer + 1], **_assemble_rest(arrived[1:]))
    dh, loss = _loss_grad(h, target, _tile(h.shape[0], 512))
    received, conv_grads, small_grads = [None] * DEPTH, [None] * DEPTH, [None] * DEPTH
    comm = None
    for layer in reversed(range(DEPTH)):
        dh, parts, conv_grads[layer], small_grads[layer], arrived, arrived_own = _layer_backward(
            dh, ps[layer], cosf, sins, weights[layer], smalls[layer], saved[layer], comm, layer == 0)
        if comm is not None:
            received[layer + 1] = arrived
        comm = (parts, [False] * len(parts))
    received[0] = arrived_own
    return loss, dh, received, conv_grads, small_grads


def kernel(x, p, positions, attn_norm, w_in, sgu_norm, w_spatial, b_spatial, conv_w, conv_b, kv_norm, w_ukv, q_nope_norm, q_rope_norm, k_nope_norm, k_rope_norm, out_norm, w_out, ple_norm, w_ple_gate, w_ple_proj, loss_target, m_attn_norm, m_w_in, m_sgu_norm, m_w_spatial, m_b_spatial, m_conv_w, m_conv_b, m_kv_norm, m_w_ukv, m_q_nope_norm, m_q_rope_norm, m_k_nope_norm, m_k_rope_norm, m_out_norm, m_w_out, m_ple_norm, m_w_ple_gate, m_w_ple_proj, v_attn_norm, v_w_in, v_sgu_norm, v_w_spatial, v_b_spatial, v_conv_w, v_conv_b, v_kv_norm, v_w_ukv, v_q_nope_norm, v_q_rope_norm, v_k_nope_norm, v_k_rope_norm, v_out_norm, v_w_out, v_ple_norm, v_w_ple_gate, v_w_ple_proj):
    order = ("attn_norm", "w_in", "sgu_norm", "w_spatial", "b_spatial", "conv_w", "conv_b", "kv_norm", "w_ukv",
             "q_nope_norm", "q_rope_norm", "k_nope_norm", "k_rope_norm", "out_norm", "w_out", "ple_norm", "w_ple_gate",
             "w_ple_proj")
    wts = dict(zip(order, (attn_norm, w_in, sgu_norm, w_spatial, b_spatial, conv_w, conv_b, kv_norm, w_ukv, q_nope_norm,
                           q_rope_norm, k_nope_norm, k_rope_norm, out_norm, w_out, ple_norm, w_ple_gate, w_ple_proj)))
    mom = dict(zip(order, (m_attn_norm, m_w_in, m_sgu_norm, m_w_spatial, m_b_spatial, m_conv_w, m_conv_b, m_kv_norm, m_w_ukv,
                           m_q_nope_norm, m_q_rope_norm, m_k_nope_norm, m_k_rope_norm, m_out_norm, m_w_out, m_ple_norm,
                           m_w_ple_gate, m_w_ple_proj)))
    var = dict(zip(order, (v_attn_norm, v_w_in, v_sgu_norm, v_w_spatial, v_b_spatial, v_conv_w, v_conv_b, v_kv_norm, v_w_ukv,
                           v_q_nope_norm, v_q_rope_norm, v_k_nope_norm, v_k_rope_norm, v_out_norm, v_w_out, v_ple_norm,
                           v_w_ple_gate, v_w_ple_proj)))

    conv_shard = wts["conv_w"]
    (conv_all,) = _exchange([conv_shard.reshape(-1, 128)], [True], "gather_conv_w")
    conv_full = conv_all.reshape(N_DEV, DEPTH, 3, -1).transpose(1, 2, 0, 3).reshape(DEPTH, 3, -1)
    shards = {n: wts[n].astype(BF16) for n in BIG}
    loss_part, grad_x, received, conv_grads, small_grads = _step_local(
        x[0], p[:, 0], positions[0], loss_target[0], shards, conv_full, wts)
    loss = lax.psum(loss_part[0, 0], ("x", "y", "c"))

    def small_grad(name):
        g = jnp.stack([sg[name] for sg in small_grads])
        return g.reshape(wts[name].shape)

    conv_grad = jnp.stack(conv_grads)
    like = [wts[n] for n in SMALL] + [conv_grad]
    packed = _pack([small_grad(n) for n in SMALL] + [conv_grad])
    small_parts = _gather_via_sibling(packed, "gather_small_grads")
    filler = [jnp.zeros_like(conv_grad), jnp.zeros_like(conv_grad), jnp.ones_like(conv_grad)]
    small_out = _sum_adam(small_parts, *(_pack([src[n] for n in SMALL] + [fill])[None] for src, fill in zip((wts, mom, var), filler)),
                          0, None, "adam_small")
    unpacked = [_unpack(o[0], like) for o in small_out]
    results = {n: vals for n, vals in zip(SMALL, zip(*[u[:-1] for u in unpacked]))}
    me = 4 * lax.axis_index("x") + 2 * lax.axis_index("y") + lax.axis_index("c")
    width = conv_shard.shape[2]
    conv_local = lax.dynamic_slice_in_dim(unpacked[0][-1], me * width, width, axis=2)
    as_slab = (lambda t: t.reshape(1, -1, width))
    conv_out = _sum_adam(as_slab(conv_local), as_slab(conv_shard), as_slab(mom["conv_w"]), as_slab(var["conv_w"]), 0, None,
                         "adam_conv_w")
    results["conv_w"] = tuple(o.reshape(conv_shard.shape) for o in conv_out)

    for j, name in enumerate(BIG):
        view = (lambda t: jnp.swapaxes(t, 1, 2)) if name == "w_in" else (lambda t: t)
        outs = None
        for layer in range(DEPTH):
            outs = _sum_adam(received[layer][j], view(wts[name]), view(mom[name]), view(var[name]), layer, outs, "adam_" + name)
        results[name] = tuple(view(o) for o in outs)

    grads, deltas, new_m, new_v = ([results[n][j] for n in order] for j in range(4))
    return (loss, grad_x[None], *grads, *deltas, *new_m, *new_v)
```

```python
import functools

import jax
import jax.numpy as jnp
from jax import lax
from jax.experimental import pallas as pl
from jax.experimental.pallas import tpu as pltpu

F32 = jnp.float32
BF16 = jnp.bfloat16

N_DEV = 8
DEPTH = 4
D_MODEL = 2048
EPS = 1e-6
CHUNK = 128
A_HEADS = 4
HEAD = 128
ROPE = 64
HALF = ROPE // 2
C_HEADS = 8
KV_RANK = 512
PLE_DIM = 256
ROPE_BASE = 10000.0
IN_WIDTH = 6720
QK = HEAD + ROPE
SCALE = QK ** -0.5
LOG2_E = 1.4426950408889634
LN_2 = 0.6931471805599453
Q_SCALE = SCALE * LOG2_E
HALO = 8

O_B = 0
W_B = 2048
O_CZ = 2048
W_CZ = 1024
O_A = 3072
W_A = 1536
O_M = 4608
W_M = 2304
M_QN, M_QR, M_CKV, M_KR = 0, 1024, 1536, 2048
PROJ_W = 6912

ADAM_LR = 0.001
ADAM_B1 = 0.9
ADAM_B2 = 0.999
ADAM_EPS = 1e-08
ADAM_WD = 0.01
ADAM_STEP = 10

VMEM_LIMIT = 56 * 1024 * 1024

_NT = (((1,), (1,)), ((), ()))
_TN = (((0,), (0,)), ((), ()))


def _params(*sem):
    return pltpu.CompilerParams(dimension_semantics=sem, vmem_limit_bytes=VMEM_LIMIT)


@jax.custom_vjp
def _bdot(a, b):
    return jnp.dot(a.astype(BF16), b.astype(BF16), preferred_element_type=F32)


def _bdot_fwd(a, b):
    return _bdot(a, b), (a, b)


def _bdot_bwd(res, g):
    a, b = res
    gb = g.astype(BF16)
    da = lax.dot_general(gb, b.astype(BF16), _NT, preferred_element_type=F32)
    db = lax.dot_general(a.astype(BF16), gb, _TN, preferred_element_type=F32)
    return da.astype(a.dtype), db.astype(b.dtype)


_bdot.defvjp(_bdot_fwd, _bdot_bwd)


@functools.partial(jax.custom_vjp, nondiff_argnums=(1,))
def _split(x, n):
    w = x.shape[-1] // n
    return tuple(x[:, i * w:(i + 1) * w] for i in range(n))


def _split_fwd(x, n):
    return _split(x, n), None


def _split_bwd(n, _, gs):
    return (jnp.concatenate(gs, axis=-1),)


_split.defvjp(_split_fwd, _split_bwd)


@functools.partial(jax.custom_vjp, nondiff_argnums=(1,))
def _shift_rows(x, k):
    return pltpu.roll(x, k % x.shape[0], 0)


def _shift_rows_fwd(x, k):
    return _shift_rows(x, k), None


def _shift_rows_bwd(k, _, g):
    return (_shift_rows(g, -k),)


_shift_rows.defvjp(_shift_rows_fwd, _shift_rows_bwd)


@jax.custom_vjp
def _swap_halves(x):
    h = x.shape[-1] // 2
    return jnp.concatenate([x[:, h:], x[:, :h]], axis=-1)


def _swap_halves_fwd(x):
    return _swap_halves(x), None


def _swap_halves_bwd(_, g):
    return (_swap_halves(g),)


_swap_halves.defvjp(_swap_halves_fwd, _swap_halves_bwd)


def _rms(x, g):
    return x * lax.rsqrt(jnp.mean(x * x, axis=-1, keepdims=True) + EPS) * g


def _rope(x, cosf, sins):
    return x * cosf + _swap_halves(x) * sins


def _sgu_chunk(u, v, z, gain, ws, bs, ga):
    ys = []
    for h in range(A_HEADS):
        vn = _rms(v[h], gain[h])
        s = _bdot(ws[h], vn) + bs[h]
        ys.append(u[h] * s * jax.nn.silu(z[h]))
    ss = sum(jnp.sum(y * y, axis=-1, keepdims=True) for y in ys) * (1.0 / (A_HEADS * HEAD))
    r = lax.rsqrt(ss + EPS)
    return tuple(ys[h] * r * ga[h] for h in range(A_HEADS))


def _conv_tile(bb, bc, bh, bz, w0, w1, w2, cb, gb, w0h, w1h, w2h, cbh, gbh, valid, core):
    t = jnp.where(valid, bc * bh, 0.0)
    y = (jnp.where(core, cb, cbh)
         + _shift_rows(t, 1) * jnp.where(core, w0, w0h)
         + t * jnp.where(core, w1, w1h)
         + _shift_rows(t, -1) * jnp.where(core, w2, w2h))
    return _rms(bb * y * jax.nn.silu(bz), jnp.where(core, gb, gbh))


def _cgate_tile(o, cz, gc):
    return _rms(o * jax.nn.silu(cz), gc)


def _mla_tile(qn, qr, ckv, kr, cosf, sins, wukv, kvg, qng, qrg, kng, krg):
    kv = _split(_bdot(_rms(ckv, kvg), wukv), 2 * C_HEADS)
    k_r = _rope(_rms(kr, krg), cosf, sins)
    qn_h = _split(qn, C_HEADS)
    qr_h = _split(qr, C_HEADS)
    q, k, v = [], [], []
    for h in range(C_HEADS):
        q.append(jnp.concatenate([_rms(qn_h[h], qng), _rope(_rms(qr_h[h], qrg), cosf, sins)], axis=-1) * Q_SCALE)
        k.append(jnp.concatenate([_rms(kv[h], kng), k_r], axis=-1))
        v.append(kv[C_HEADS + h])
    return tuple(q), tuple(k), tuple(v)


def _norm_matmul(h, gain, w, tm, tn, comm=None):
    s_len, k = h.shape
    n = w.shape[1]
    c_ops, c_in_specs, c_shapes, c_out_specs, c_sems, c_begin, c_end = _riding_exchange(comm, 2)
    n_c = len(c_ops)

    def body(h_ref, g_ref, w_ref, *rest):
        c_ins, (o_ref, hn_ref), c_outs, sems = rest[:n_c], rest[n_c:n_c + 2], rest[n_c + 2:2 * n_c + 2], rest[2 * n_c + 2:]
        c_begin(c_ins, c_outs, sems)

        @pl.when(pl.program_id(1) == 0)
        def _():
            hn_ref[...] = _rms(h_ref[...], g_ref[...]).astype(BF16)

        o_ref[...] = jnp.dot(hn_ref[...], w_ref[...], preferred_element_type=F32).astype(BF16)
        c_end(c_ins, c_outs, sems)

    outs = pl.pallas_call(
        body, name="norm_matmul",
        out_shape=(jax.ShapeDtypeStruct((s_len, n), BF16), jax.ShapeDtypeStruct((s_len, k), BF16), *c_shapes),
        grid=(s_len // tm, n // tn),
        in_specs=[pl.BlockSpec((tm, k), lambda i, j: (i, 0)), pl.BlockSpec((1, k), lambda i, j: (0, 0)),
                  pl.BlockSpec((k, tn), lambda i, j: (0, j))] + c_in_specs,
        out_specs=(pl.BlockSpec((tm, tn), lambda i, j: (i, j)), pl.BlockSpec((tm, k), lambda i, j: (i, 0)), *c_out_specs),
        scratch_shapes=c_sems,
        compiler_params=_params("arbitrary", "arbitrary") if comm else _params("parallel", "arbitrary"),
    )(h, gain, w, *c_ops)
    return outs[0], outs[1], outs[2:]


def _out_matmul(h, y, w, tm, tn):
    s_len, n = h.shape
    k = y.shape[1]

    def body(h_ref, y_ref, w_ref, o_ref):
        o_ref[...] = h_ref[...] + jnp.dot(y_ref[...], w_ref[...], preferred_element_type=F32)

    return pl.pallas_call(
        body, name="out_matmul",
        out_shape=jax.ShapeDtypeStruct((s_len, n), F32),
        grid=(s_len // tm, n // tn),
        in_specs=[pl.BlockSpec((tm, tn), lambda i, j: (i, j)), pl.BlockSpec((tm, k), lambda i, j: (i, 0)),
                  pl.BlockSpec((k, tn), lambda i, j: (0, j))],
        out_specs=pl.BlockSpec((tm, tn), lambda i, j: (i, j)),
        compiler_params=_params("parallel", "parallel"),
    )(h, y, w)


def _ple_forward(h1, gain, p, wg, wp, tm, tn):
    s_len, d = h1.shape
    kp = p.shape[1]

    def body(hrow_ref, g_ref, p_ref, wg_ref, wp_ref, o_ref, n1_ref, gate_ref, pp_ref):
        j = pl.program_id(1)

        @pl.when(j == 0)
        def _():
            n1_ref[...] = _rms(hrow_ref[...], g_ref[...]).astype(BF16)

        gate = jax.nn.sigmoid(jnp.dot(n1_ref[...], wg_ref[...], preferred_element_type=F32))
        pp = jnp.dot(p_ref[...].astype(BF16), wp_ref[...], preferred_element_type=F32)
        o_ref[...] = hrow_ref[:, pl.ds(pl.multiple_of(j * tn, tn), tn)] + gate * pp
        gate_ref[...] = gate.astype(BF16)
        pp_ref[...] = pp.astype(BF16)

    col = pl.BlockSpec((tm, tn), lambda i, j: (i, j))
    return pl.pallas_call(
        body, name="ple_forward",
        out_shape=(jax.ShapeDtypeStruct((s_len, d), F32), jax.ShapeDtypeStruct((s_len, d), BF16),
                   jax.ShapeDtypeStruct((s_len, d), BF16), jax.ShapeDtypeStruct((s_len, d), BF16)),
        grid=(s_len // tm, d // tn),
        in_specs=[pl.BlockSpec((tm, d), lambda i, j: (i, 0)), pl.BlockSpec((1, d), lambda i, j: (0, 0)),
                  pl.BlockSpec((tm, kp), lambda i, j: (i, 0)), pl.BlockSpec((d, tn), lambda i, j: (0, j)),
                  pl.BlockSpec((kp, tn), lambda i, j: (0, j))],
        out_specs=(col, pl.BlockSpec((tm, d), lambda i, j: (i, 0)), col, col),
        compiler_params=_params("parallel", "arbitrary"),
    )(h1, gain, p, wg, wp)


def _matmul_nt(a, b, tm, tk, name, comm=None):
    m, n = a.shape
    k = b.shape[0]
    c_ops, c_in_specs, c_shapes, c_out_specs, c_sems, c_begin, c_end = _riding_exchange(comm, 2)
    n_c = len(c_ops)

    def body(a_ref, b_ref, *rest):
        c_ins, o_ref, c_outs, sems = rest[:n_c], rest[n_c], rest[n_c + 1:2 * n_c + 1], rest[2 * n_c + 1:]
        c_begin(c_ins, c_outs, sems)
        o_ref[...] = lax.dot_general(a_ref[...].astype(BF16), b_ref[...].astype(BF16), _NT, preferred_element_type=F32)
        c_end(c_ins, c_outs, sems)

    outs = pl.pallas_call(
        body, name=name,
        out_shape=(jax.ShapeDtypeStruct((m, k), F32), *c_shapes),
        grid=(m // tm, k // tk),
        in_specs=[pl.BlockSpec((tm, n), lambda i, j: (i, 0)), pl.BlockSpec((tk, n), lambda i, j: (j, 0))] + c_in_specs,
        out_specs=(pl.BlockSpec((tm, tk), lambda i, j: (i, j)), *c_out_specs),
        scratch_shapes=c_sems,
        compiler_params=_params("arbitrary", "arbitrary") if comm else _params("parallel", "parallel"),
    )(a, b, *c_ops)
    return outs[0], outs[1:]


def _matmul_tn(a, b, tm, tk, tn, name, comm=None):
    m, k = a.shape
    n = b.shape[1]
    n_m = m // tm
    c_ops, c_in_specs, c_shapes, c_out_specs, c_sems, c_begin, c_end = _riding_exchange(comm, 3)
    n_c = len(c_ops)

    def body(a_ref, b_ref, *rest):
        c_ins, o_ref, c_outs, acc_ref, sems = rest[:n_c], rest[n_c], rest[n_c + 1:2 * n_c + 1], rest[2 * n_c + 1], rest[2 * n_c + 2:]
        c_begin(c_ins, c_outs, sems)
        part = lax.dot_general(a_ref[...].astype(BF16), b_ref[...].astype(BF16), _TN, preferred_element_type=F32)
        _acc(acc_ref, part, pl.program_id(2) == 0)

        @pl.when(pl.program_id(2) == n_m - 1)
        def _():
            o_ref[...] = acc_ref[...].astype(BF16)

        c_end(c_ins, c_outs, sems)

    outs = pl.pallas_call(
        body, name=name,
        out_shape=(jax.ShapeDtypeStruct((k, n), BF16), *c_shapes),
        grid=(k // tk, n // tn, n_m),
        in_specs=[pl.BlockSpec((tm, tk), lambda kk, nn, mm: (mm, kk)), pl.BlockSpec((tm, tn), lambda kk, nn, mm: (mm, nn))]
                 + c_in_specs,
        out_specs=(pl.BlockSpec((tk, tn), lambda kk, nn, mm: (kk, nn)), *c_out_specs),
        scratch_shapes=[pltpu.VMEM((tk, tn), F32)] + c_sems,
        compiler_params=_params("arbitrary", "arbitrary", "arbitrary") if comm else _params("parallel", "parallel", "arbitrary"),
    )(a, b, *c_ops)
    return outs[0], outs[1:]


def _acc(ref, val, first):
    @pl.when(first)
    def _():
        ref[...] = val

    @pl.when(jnp.logical_not(first))
    def _():
        ref[...] += val


def _loss_grad(h, target, tm):
    s_len, d = h.shape

    def body(h_ref, t_ref, dh_ref, loss_ref):
        e = h_ref[...] - t_ref[...]
        dh_ref[...] = e * (1.0 / d)
        part = jnp.sum(jnp.sum(e * e, axis=-1, keepdims=True), axis=0, keepdims=True) * (0.5 / d)
        _acc(loss_ref, jnp.broadcast_to(part, loss_ref.shape), pl.program_id(0) == 0)

    row = pl.BlockSpec((tm, d), lambda i: (i, 0))
    return pl.pallas_call(
        body, name="loss_grad",
        out_shape=(jax.ShapeDtypeStruct((s_len, d), F32), jax.ShapeDtypeStruct((1, 128), F32)),
        grid=(s_len // tm,),
        in_specs=[row, row],
        out_specs=(row, pl.BlockSpec((1, 128), lambda i: (0, 0))),
        compiler_params=_params("arbitrary"),
    )(h, target)


def _rms_backward(x, gain, dn, dres, tm, name):
    s_len, d = x.shape

    def body(x_ref, g_ref, dn_ref, dres_ref, dx_ref, dg_ref):
        _, vjp = jax.vjp(_rms, x_ref[...], g_ref[...])
        dx, dg = vjp(dn_ref[...])
        dx_ref[...] = dres_ref[...] + dx
        _acc(dg_ref, dg, pl.program_id(0) == 0)

    row = pl.BlockSpec((tm, d), lambda i: (i, 0))
    vec = pl.BlockSpec((1, d), lambda i: (0, 0))
    return pl.pallas_call(
        body, name=name,
        out_shape=(jax.ShapeDtypeStruct((s_len, d), F32), jax.ShapeDtypeStruct((1, d), F32)),
        grid=(s_len // tm,),
        in_specs=[row, vec, row, row],
        out_specs=(row, vec),
        compiler_params=_params("arbitrary"),
    )(x, gain, dn, dres)


def _ple_backward(dh2, gate, pp, wg, h1, gain, tm):
    s_len, d = dh2.shape

    def body(dh_ref, gate_ref, pp_ref, wg_ref, h1_ref, g_ref, dh1_ref, dgain_ref, dgp_ref, dpp_ref):
        dh = dh_ref[...]
        gate = gate_ref[...].astype(F32)
        dgp = (dh * pp_ref[...].astype(F32) * gate * (1.0 - gate)).astype(BF16)
        dgp_ref[...] = dgp
        dpp_ref[...] = (dh * gate).astype(BF16)
        dn = lax.dot_general(dgp, wg_ref[...], _NT, preferred_element_type=F32)
        _, vjp = jax.vjp(_rms, h1_ref[...], g_ref[...])
        dx, dgain = vjp(dn)
        dh1_ref[...] = dh + dx
        _acc(dgain_ref, dgain, pl.program_id(0) == 0)

    row = pl.BlockSpec((tm, d), lambda i: (i, 0))
    vec = pl.BlockSpec((1, d), lambda i: (0, 0))
    return pl.pallas_call(
        body, name="ple_backward",
        out_shape=(jax.ShapeDtypeStruct((s_len, d), F32), jax.ShapeDtypeStruct((1, d), F32),
                   jax.ShapeDtypeStruct((s_len, d), BF16), jax.ShapeDtypeStruct((s_len, d), BF16)),
        grid=(s_len // tm,),
        in_specs=[row, row, row, pl.BlockSpec((d, d), lambda i: (0, 0)), row, vec],
        out_specs=(row, vec, row, row),
        compiler_params=_params("arbitrary"),
    )(dh2, gate, pp, wg, h1, gain)


def _sgu_in_specs(tm):
    return [pl.BlockSpec((tm, W_A), lambda i: (i, O_A // W_A)),
            pl.BlockSpec((A_HEADS, HEAD), lambda i: (0, 0)), pl.BlockSpec((A_HEADS, CHUNK, CHUNK), lambda i: (0, 0, 0)),
            pl.BlockSpec((A_HEADS, CHUNK, 1), lambda i: (0, 0, 0)), pl.BlockSpec((1, 512), lambda i: (0, 0))]


def _sgu_load(a_ref, gain_ref, ws_ref, bs_ref, ga_ref, c):
    rows = slice(c * CHUNK, (c + 1) * CHUNK)
    heads = range(A_HEADS)
    u = tuple(a_ref[rows, h * HEAD:(h + 1) * HEAD].astype(F32) for h in heads)
    v = tuple(a_ref[rows, 512 + h * HEAD:512 + (h + 1) * HEAD].astype(F32) for h in heads)
    z = tuple(a_ref[rows, 1024 + h * HEAD:1024 + (h + 1) * HEAD].astype(F32) for h in heads)
    gain = tuple(gain_ref[h:h + 1, :] for h in heads)
    ws = tuple(ws_ref[h] for h in heads)
    bs = tuple(bs_ref[h] for h in heads)
    ga = tuple(ga_ref[:, h * HEAD:(h + 1) * HEAD] for h in heads)
    return u, v, z, gain, ws, bs, ga


def _sgu_forward(proj, gain, ws, bs, ga, tm):
    s_len = proj.shape[0]

    def body(a_ref, gain_ref, ws_ref, bs_ref, ga_ref, o_ref):
        for c in range(tm // CHUNK):
            out = _sgu_chunk(*_sgu_load(a_ref, gain_ref, ws_ref, bs_ref, ga_ref, c))
            for h in range(A_HEADS):
                o_ref[c * CHUNK:(c + 1) * CHUNK, h * HEAD:(h + 1) * HEAD] = out[h].astype(BF16)

    return pl.pallas_call(
        body, name="sgu_forward",
        out_shape=jax.ShapeDtypeStruct((s_len, D_MODEL), BF16),
        grid=(s_len // tm,),
        in_specs=_sgu_in_specs(tm),
        out_specs=pl.BlockSpec((tm, 512), lambda i: (i, 0)),
        compiler_params=_params("parallel"),
    )(proj, gain, ws, bs, ga)


def _sgu_backward(proj, gain, ws, bs, ga, dy, dproj, tm):
    s_len = proj.shape[0]

    def body(a_ref, gain_ref, ws_ref, bs_ref, ga_ref, dy_ref, _, da_ref, dgain_ref, dws_ref, dbs_ref, dga_ref):
        tot = None
        for c in range(tm // CHUNK):
            args = _sgu_load(a_ref, gain_ref, ws_ref, bs_ref, ga_ref, c)
            _, vjp = jax.vjp(_sgu_chunk, *args)
            rows = slice(c * CHUNK, (c + 1) * CHUNK)
            du, dv, dz, dgain, dws, dbs, dga = vjp(tuple(dy_ref[rows, h * HEAD:(h + 1) * HEAD] for h in range(A_HEADS)))
            for h in range(A_HEADS):
                da_ref[rows, h * HEAD:(h + 1) * HEAD] = du[h].astype(BF16)
                da_ref[rows, 512 + h * HEAD:512 + (h + 1) * HEAD] = dv[h].astype(BF16)
                da_ref[rows, 1024 + h * HEAD:1024 + (h + 1) * HEAD] = dz[h].astype(BF16)
            part = (dgain, dws, dbs, dga)
            tot = part if tot is None else jax.tree.map(jnp.add, tot, part)
        dgain, dws, dbs, dga = tot
        first = pl.program_id(0) == 0
        _acc(dgain_ref, jnp.concatenate(dgain, axis=0), first)
        _acc(dga_ref, jnp.concatenate(dga, axis=-1), first)
        for h in range(A_HEADS):
            _acc(dws_ref.at[h], dws[h], first)
            _acc(dbs_ref.at[h], dbs[h], first)

    small = [pl.BlockSpec((A_HEADS, HEAD), lambda i: (0, 0)), pl.BlockSpec((A_HEADS, CHUNK, CHUNK), lambda i: (0, 0, 0)),
             pl.BlockSpec((A_HEADS, CHUNK, 1), lambda i: (0, 0, 0)), pl.BlockSpec((1, 512), lambda i: (0, 0))]
    return pl.pallas_call(
        body, name="sgu_backward",
        out_shape=(jax.ShapeDtypeStruct(dproj.shape, BF16),
                   jax.ShapeDtypeStruct((A_HEADS, HEAD), F32), jax.ShapeDtypeStruct((A_HEADS, CHUNK, CHUNK), F32),
                   jax.ShapeDtypeStruct((A_HEADS, CHUNK, 1), F32), jax.ShapeDtypeStruct((1, 512), F32)),
        grid=(s_len // tm,),
        in_specs=_sgu_in_specs(tm) + [pl.BlockSpec((tm, 512), lambda i: (i, 0)), pl.BlockSpec(memory_space=pl.ANY)],
        out_specs=(pl.BlockSpec((tm, W_A), lambda i: (i, O_A // W_A)), *small),
        input_output_aliases={6: 0},
        compiler_params=_params("arbitrary"),
    )(proj, gain, ws, bs, ga, dy, dproj)


def _halo_specs(tm, width, col, n_rows):
    per = tm // HALO
    last = n_rows // HALO - 1
    return [pl.BlockSpec((HALO, width), lambda i: (jnp.maximum(i * per - 1, 0), col)),
            pl.BlockSpec((tm, width), lambda i: (i, col)),
            pl.BlockSpec((HALO, width), lambda i: (jnp.minimum((i + 1) * per, last), col))]


def _conv_masks(tm, s_len):
    r = lax.broadcasted_iota(jnp.int32, (tm + 2 * HALO, 1), 0)
    g = pl.program_id(0) * tm - HALO + r
    return (g >= 0) & (g < s_len), (r >= HALO) & (r < HALO + tm)


def _conv_inputs(b_refs, cw_ref, cb_ref, gb_ref):
    ext = jnp.concatenate([r[...] for r in b_refs], axis=0).astype(F32)
    bb, bc, bh, bz = (ext[:, j * 512:(j + 1) * 512] for j in range(4))
    prm = (cw_ref[0:1, :], cw_ref[1:2, :], cw_ref[2:3, :], cb_ref[...], gb_ref[...])
    return (bb, bc, bh, bz), prm


def _conv_forward(proj, cw, cb, gb, y, tm):
    s_len = proj.shape[0]

    def body(p0, p1, p2, cw_ref, cb_ref, gb_ref, _, o_ref):
        acts, prm = _conv_inputs((p0, p1, p2), cw_ref, cb_ref, gb_ref)
        valid, core = _conv_masks(tm, s_len)
        out = _conv_tile(*acts, *prm, *prm, valid, core)
        o_ref[...] = out[HALO:HALO + tm].astype(BF16)

    vec = pl.BlockSpec((1, 512), lambda i: (0, 0))
    return pl.pallas_call(
        body, name="conv_forward",
        out_shape=jax.ShapeDtypeStruct(y.shape, BF16),
        grid=(s_len // tm,),
        in_specs=_halo_specs(tm, W_B, O_B // W_B, s_len) + [pl.BlockSpec((3, 512), lambda i: (0, 0)), vec, vec,
                                                             pl.BlockSpec(memory_space=pl.ANY)],
        out_specs=pl.BlockSpec((tm, 512), lambda i: (i, 1)),
        input_output_aliases={6: 0},
        compiler_params=_params("parallel"),
    )(proj, proj, proj, cw, cb, gb, y)


def _conv_backward(proj, cw, cb, gb, dy, tm):
    s_len = proj.shape[0]

    def body(p0, p1, p2, cw_ref, cb_ref, gb_ref, d0, d1, d2, db_ref, dcw_ref, dcb_ref, dgb_ref):
        acts, prm = _conv_inputs((p0, p1, p2), cw_ref, cb_ref, gb_ref)
        valid, core = _conv_masks(tm, s_len)
        _, vjp = jax.vjp(lambda a, p: _conv_tile(*a, *p, *prm, valid, core), acts, prm)
        dy_ext = jnp.where(valid, jnp.concatenate([d0[...], d1[...], d2[...]], axis=0), 0.0)
        dacts, dprm = vjp(dy_ext)
        for j in range(4):
            db_ref[:, j * 512:(j + 1) * 512] = dacts[j][HALO:HALO + tm].astype(BF16)
        first = pl.program_id(0) == 0
        _acc(dcw_ref, jnp.concatenate(dprm[0:3], axis=0), first)
        _acc(dcb_ref, dprm[3], first)
        _acc(dgb_ref, dprm[4], first)

    vec = pl.BlockSpec((1, 512), lambda i: (0, 0))
    mat = pl.BlockSpec((3, 512), lambda i: (0, 0))
    return pl.pallas_call(
        body, name="conv_backward",
        out_shape=(jax.ShapeDtypeStruct((s_len, PROJ_W), BF16), jax.ShapeDtypeStruct((3, 512), F32),
                   jax.ShapeDtypeStruct((1, 512), F32), jax.ShapeDtypeStruct((1, 512), F32)),
        grid=(s_len // tm,),
        in_specs=_halo_specs(tm, W_B, O_B // W_B, s_len) + [mat, vec, vec] + _halo_specs(tm, 512, 1, s_len),
        out_specs=(pl.BlockSpec((tm, W_B), lambda i: (i, O_B // W_B)), mat, vec, vec),
        compiler_params=_params("arbitrary"),
    )(proj, proj, proj, cw, cb, gb, dy, dy, dy)


def _cgate_forward(o, proj, gc, y, tm):
    s_len = o.shape[0]

    def body(o_ref, cz_ref, gc_ref, _, y_ref):
        y_ref[...] = _cgate_tile(o_ref[...], cz_ref[...].astype(F32), gc_ref[...]).astype(BF16)

    return pl.pallas_call(
        body, name="cgate_forward",
        out_shape=jax.ShapeDtypeStruct(y.shape, BF16),
        grid=(s_len // tm,),
        in_specs=[pl.BlockSpec((tm, W_CZ), lambda i: (i, 0)), pl.BlockSpec((tm, W_CZ), lambda i: (i, O_CZ // W_CZ)),
                  pl.BlockSpec((1, W_CZ), lambda i: (0, 0)), pl.BlockSpec(memory_space=pl.ANY)],
        out_specs=pl.BlockSpec((tm, W_CZ), lambda i: (i, 1)),
        input_output_aliases={3: 0},
        compiler_params=_params("parallel"),
    )(o, proj, gc, y)


def _cgate_backward(o, proj, gc, dy, dproj, tm, stat_chunk):
    s_len = o.shape[0]
    per_stat = stat_chunk // tm

    def body(o_ref, cz_ref, gc_ref, dy_ref, _, dcz_ref, do_ref, dsum_ref, dgc_ref):
        o = o_ref[...]
        _, vjp = jax.vjp(_cgate_tile, o, cz_ref[...].astype(F32), gc_ref[...])
        do, dcz, dgc = vjp(dy_ref[...])
        dcz_ref[...] = dcz.astype(BF16)
        do_ref[...] = do.astype(BF16)
        ones = jnp.ones((8, HEAD), F32)
        for h in range(C_HEADS):
            cols = slice(h * HEAD, (h + 1) * HEAD)
            sums = lax.dot_general(ones, do[:, cols] * o[:, cols], _NT, precision=lax.Precision.HIGHEST,
                                   preferred_element_type=F32)
            dsum_ref[h, 0] = sums[0:1]
        _acc(dgc_ref, dgc, pl.program_id(0) == 0)

    row = pl.BlockSpec((tm, W_CZ), lambda i: (i, 0))
    vec = pl.BlockSpec((1, W_CZ), lambda i: (0, 0))
    return pl.pallas_call(
        body, name="cgate_backward",
        out_shape=(jax.ShapeDtypeStruct(dproj.shape, BF16), jax.ShapeDtypeStruct((s_len, W_CZ), BF16),
                   jax.ShapeDtypeStruct((C_HEADS, s_len // stat_chunk, 1, stat_chunk), F32), jax.ShapeDtypeStruct((1, W_CZ), F32)),
        grid=(s_len // tm,),
        in_specs=[row, pl.BlockSpec((tm, W_CZ), lambda i: (i, O_CZ // W_CZ)), vec,
                  pl.BlockSpec((tm, W_CZ), lambda i: (i, 1)), pl.BlockSpec(memory_space=pl.ANY)],
        out_specs=(pl.BlockSpec((tm, W_CZ), lambda i: (i, O_CZ // W_CZ)), row,
                   pl.BlockSpec((C_HEADS, 1, 1, tm), lambda i: (0, i // per_stat, 0, i % per_stat)), vec),
        input_output_aliases={4: 0},
        compiler_params=_params("arbitrary"),
    )(o, proj, gc, dy, dproj)


def _mla_small_specs():
    return [pl.BlockSpec((KV_RANK, 2 * C_HEADS * HEAD), lambda i: (0, 0)), pl.BlockSpec((1, KV_RANK), lambda i: (0, 0)),
            pl.BlockSpec((1, HEAD), lambda i: (0, 0)), pl.BlockSpec((1, ROPE), lambda i: (0, 0)),
            pl.BlockSpec((1, HEAD), lambda i: (0, 0)), pl.BlockSpec((1, ROPE), lambda i: (0, 0))]


def _mla_load(m_ref, cos_ref, sin_ref):
    qn = m_ref[:, M_QN:M_QN + C_HEADS * HEAD].astype(F32)
    qr = m_ref[:, M_QR:M_QR + C_HEADS * ROPE].astype(F32)
    ckv = m_ref[:, M_CKV:M_CKV + KV_RANK].astype(F32)
    kr = m_ref[:, M_KR:M_KR + ROPE].astype(F32)
    return qn, qr, ckv, kr, cos_ref[...], sin_ref[...]


def _mla_forward(proj, cosf, sins, wukv, kvg, qng, qrg, kng, krg, tm, kt_chunk, vt_chunk):
    s_len = proj.shape[0]

    def body(m_ref, cos_ref, sin_ref, w_ref, kvg_ref, qng_ref, qrg_ref, kng_ref, krg_ref, q_ref, k_ref, v_ref, kt_ref, vt_ref):
        q, k, v = _mla_tile(*_mla_load(m_ref, cos_ref, sin_ref), w_ref[...], kvg_ref[...], qng_ref[...], qrg_ref[...],
                            kng_ref[...], krg_ref[...])
        for h in range(C_HEADS):
            q_ref[h] = q[h].astype(BF16)
            k_ref[h] = k[h].astype(BF16)
            v_ref[h] = v[h].astype(BF16)
            kt_ref[h, 0] = jnp.concatenate([k[h][:, :HEAD].T, k[h][:, HEAD:].T], axis=0).astype(BF16)
            vt_ref[h, 0] = v[h].T.astype(BF16)

    rope_spec = pl.BlockSpec((tm, ROPE), lambda i: (i, 0))
    qk_spec = pl.BlockSpec((C_HEADS, tm, QK), lambda i: (0, i, 0))
    per_k, per_v = kt_chunk // tm, vt_chunk // tm
    return pl.pallas_call(
        body, name="mla_forward",
        out_shape=(jax.ShapeDtypeStruct((C_HEADS, s_len, QK), BF16), jax.ShapeDtypeStruct((C_HEADS, s_len, QK), BF16),
                   jax.ShapeDtypeStruct((C_HEADS, s_len, HEAD), BF16),
                   jax.ShapeDtypeStruct((C_HEADS, s_len // kt_chunk, QK, kt_chunk), BF16),
                   jax.ShapeDtypeStruct((C_HEADS, s_len // vt_chunk, HEAD, vt_chunk), BF16)),
        grid=(s_len // tm,),
        in_specs=[pl.BlockSpec((tm, W_M), lambda i: (i, O_M // W_M)), rope_spec, rope_spec] + _mla_small_specs(),
        out_specs=(qk_spec, qk_spec, pl.BlockSpec((C_HEADS, tm, HEAD), lambda i: (0, i, 0)),
                   pl.BlockSpec((C_HEADS, 1, QK, tm), lambda i: (0, i // per_k, 0, i % per_k)),
                   pl.BlockSpec((C_HEADS, 1, HEAD, tm), lambda i: (0, i // per_v, 0, i % per_v))),
        compiler_params=_params("parallel"),
    )(proj, cosf, sins, wukv, kvg, qng, qrg, kng, krg)


def _mla_backward(proj, cosf, sins, wukv, kvg, qng, qrg, kng, krg, dqt, dk, dv, dproj, tm):
    s_len = proj.shape[0]
    per_chunk = dqt.shape[3] // tm

    def body(m_ref, cos_ref, sin_ref, w_ref, kvg_ref, qng_ref, qrg_ref, kng_ref, krg_ref, dq_ref, dk_ref, dv_ref, _,
             dm_ref, dw_ref, dkvg_ref, dqng_ref, dqrg_ref, dkng_ref, dkrg_ref):
        qn, qr, ckv, kr, cosf_t, sins_t = _mla_load(m_ref, cos_ref, sin_ref)
        prm = (w_ref[...], kvg_ref[...], qng_ref[...], qrg_ref[...], kng_ref[...], krg_ref[...])
        _, vjp = jax.vjp(lambda a, p: _mla_tile(*a, cosf_t, sins_t, *p), (qn, qr, ckv, kr), prm)
        heads = range(C_HEADS)
        dacts, dprm = vjp((tuple(dq_ref[h, 0].T for h in heads), tuple(dk_ref[h] for h in heads), tuple(dv_ref[h] for h in heads)))
        dm_ref[:, M_QN:M_QN + C_HEADS * HEAD] = dacts[0].astype(BF16)
        dm_ref[:, M_QR:M_QR + C_HEADS * ROPE] = dacts[1].astype(BF16)
        dm_ref[:, M_CKV:M_CKV + KV_RANK] = dacts[2].astype(BF16)
        pad = jnp.zeros((tm, W_M - M_KR - ROPE), F32)
        dm_ref[:, M_KR:W_M] = jnp.concatenate([dacts[3], pad], axis=-1).astype(BF16)
        first = pl.program_id(0) == 0
        for ref, val in zip((dw_ref, dkvg_ref, dqng_ref, dqrg_ref, dkng_ref, dkrg_ref), dprm):
            _acc(ref, val.astype(F32), first)

    rope_spec = pl.BlockSpec((tm, ROPE), lambda i: (i, 0))
    qk_spec = pl.BlockSpec((C_HEADS, tm, QK), lambda i: (0, i, 0))
    small = _mla_small_specs()
    return pl.pallas_call(
        body, name="mla_backward",
        out_shape=(jax.ShapeDtypeStruct(dproj.shape, BF16), jax.ShapeDtypeStruct((KV_RANK, 2 * C_HEADS * HEAD), F32),
                   jax.ShapeDtypeStruct((1, KV_RANK), F32), jax.ShapeDtypeStruct((1, HEAD), F32),
                   jax.ShapeDtypeStruct((1, ROPE), F32), jax.ShapeDtypeStruct((1, HEAD), F32),
                   jax.ShapeDtypeStruct((1, ROPE), F32)),
        grid=(s_len // tm,),
        in_specs=[pl.BlockSpec((tm, W_M), lambda i: (i, O_M // W_M)), rope_spec, rope_spec] + small
                 + [pl.BlockSpec((C_HEADS, 1, QK, tm), lambda i: (0, i // per_chunk, 0, i % per_chunk)), qk_spec,
                    pl.BlockSpec((C_HEADS, tm, HEAD), lambda i: (0, i, 0)), pl.BlockSpec(memory_space=pl.ANY)],
        out_specs=(pl.BlockSpec((tm, W_M), lambda i: (i, O_M // W_M)), *small),
        input_output_aliases={12: 0},
        compiler_params=_params("arbitrary"),
    )(proj, cosf, sins, wukv, kvg, qng, qrg, kng, krg, dqt, dk, dv, dproj)


def _attention_forward(q, k, vt, tq, stat_chunk, comm=None):
    n_heads, s_len, _ = q.shape
    n_chunks, _, ck = vt.shape[1:]
    c_ops, c_in_specs, c_shapes, c_out_specs, c_sems, c_begin, c_end = _riding_exchange(comm, 2)
    n_c = len(c_ops)

    def body(q_ref, k_ref, vt_ref, *rest):
        c_ins, (o_ref, lse_ref), c_outs, sems = rest[:n_c], rest[n_c:n_c + 2], rest[n_c + 2:2 * n_c + 2], rest[2 * n_c + 2:]
        c_begin(c_ins, c_outs, sems)
        q_t = q_ref[0]

        def step(j, carry):
            m_old, l_old, acc = carry
            k_j = k_ref[0, pl.ds(pl.multiple_of(j * ck, ck), ck), :]
            s = lax.dot_general(k_j, q_t, _NT, preferred_element_type=F32)
            m_new = jnp.maximum(m_old, jnp.max(s, axis=0, keepdims=True))
            p = jnp.exp2(s - m_new)
            alpha = jnp.exp2(m_old - m_new)
            l_new = alpha * l_old + jnp.sum(p, axis=0, keepdims=True)
            acc = alpha * acc + jnp.dot(vt_ref[0, j], p.astype(BF16), preferred_element_type=F32)
            return m_new, l_new, acc

        init = (jnp.full((1, tq), -jnp.inf, F32), jnp.zeros((1, tq), F32), jnp.zeros((HEAD, tq), F32))
        m_fin, l_fin, acc = lax.fori_loop(0, n_chunks, step, init)
        o_ref[...] = (acc / l_fin).T
        lse_ref[0, 0] = m_fin + jnp.log2(l_fin)
        c_end(c_ins, c_outs, sems)

    per_stat = stat_chunk // tq
    outs = pl.pallas_call(
        body, name="attention_forward",
        out_shape=(jax.ShapeDtypeStruct((s_len, n_heads * HEAD), F32),
                   jax.ShapeDtypeStruct((n_heads, s_len // stat_chunk, 1, stat_chunk), F32), *c_shapes),
        grid=(n_heads, s_len // tq),
        in_specs=[pl.BlockSpec((1, tq, QK), lambda h, i: (h, i, 0)), pl.BlockSpec((1, s_len, QK), lambda h, i: (h, 0, 0)),
                  pl.BlockSpec((1, n_chunks, HEAD, ck), lambda h, i: (h, 0, 0, 0))] + c_in_specs,
        out_specs=(pl.BlockSpec((tq, HEAD), lambda h, i: (i, h)),
                   pl.BlockSpec((1, 1, 1, tq), lambda h, i: (h, i // per_stat, 0, i % per_stat)), *c_out_specs),
        scratch_shapes=c_sems,
        compiler_params=_params("arbitrary", "arbitrary") if comm else _params("parallel", "parallel"),
    )(q, k, vt, *c_ops)
    return outs[0], outs[1], outs[2:]


def _attention_backward(q, k, kt, v, do, lse, dsum, comm=None):
    n_heads, s_len, _ = q.shape
    tk = kt.shape[3]
    n_q, _, cq = lse.shape[1:]
    c_ops, c_in_specs, c_shapes, c_out_specs, c_sems, c_begin, c_end = _riding_exchange(comm, 2)
    n_c = len(c_ops)

    def body(q_ref, k_ref, kt_ref, v_ref, do_ref, lse_ref, dsum_ref, *rest):
        c_ins, (dqt_ref, dk_ref, dv_ref), c_outs, sems = rest[:n_c], rest[n_c:n_c + 3], rest[n_c + 3:2 * n_c + 3], rest[2 * n_c + 3:]
        c_begin(c_ins, c_outs, sems)
        first = pl.program_id(1) == 0
        k_j, kt_j, v_j = k_ref[0], kt_ref[0, 0], v_ref[0]

        def step(i, carry):
            dk, dv = carry
            rows = pl.ds(pl.multiple_of(i * cq, cq), cq)
            q_i, do_i = q_ref[0, rows, :], do_ref[rows, :]
            s = lax.dot_general(k_j, q_i, _NT, preferred_element_type=F32)
            p = jnp.exp2(s - lse_ref[0, i])
            dp = lax.dot_general(v_j, do_i, _NT, preferred_element_type=F32)
            ds = (p * (dp - dsum_ref[0, i]) * LN_2).astype(BF16)
            dv = dv + jnp.dot(p.astype(BF16), do_i, preferred_element_type=F32)
            dk = dk + jnp.dot(ds, q_i, preferred_element_type=F32)
            _acc(dqt_ref.at[0, i], jnp.dot(kt_j, ds, preferred_element_type=F32), first)
            return dk, dv

        dk, dv = lax.fori_loop(0, n_q, step, (jnp.zeros((tk, QK), F32), jnp.zeros((tk, HEAD), F32)))
        dk_ref[0] = dk
        dv_ref[0] = dv
        c_end(c_ins, c_outs, sems)

    stat = pl.BlockSpec((1, n_q, 1, cq), lambda h, j: (h, 0, 0, 0))
    outs = pl.pallas_call(
        body, name="attention_backward",
        out_shape=(jax.ShapeDtypeStruct((n_heads, n_q, QK, cq), F32), jax.ShapeDtypeStruct((n_heads, s_len, QK), F32),
                   jax.ShapeDtypeStruct((n_heads, s_len, HEAD), F32), *c_shapes),
        grid=(n_heads, s_len // tk),
        in_specs=[pl.BlockSpec((1, s_len, QK), lambda h, j: (h, 0, 0)), pl.BlockSpec((1, tk, QK), lambda h, j: (h, j, 0)),
                  pl.BlockSpec((1, 1, QK, tk), lambda h, j: (h, j, 0, 0)),
                  pl.BlockSpec((1, tk, HEAD), lambda h, j: (h, j, 0)), pl.BlockSpec((s_len, HEAD), lambda h, j: (0, h)),
                  stat, stat] + c_in_specs,
        out_specs=(pl.BlockSpec((1, n_q, QK, cq), lambda h, j: (h, 0, 0, 0)), pl.BlockSpec((1, tk, QK), lambda h, j: (h, j, 0)),
                   pl.BlockSpec((1, tk, HEAD), lambda h, j: (h, j, 0)), *c_out_specs),
        scratch_shapes=c_sems,
        compiler_params=_params("arbitrary", "arbitrary") if comm else _params("parallel", "arbitrary"),
    )(q, k, kt, v, do, lse, dsum, *c_ops)
    return outs[0], outs[1], outs[2], outs[3:]


def _exchange(arrs, gather, name):
    n = len(arrs)

    def body(*refs):
        plan = _exchange_plan(refs[:n], refs[n:2 * n], gather, *refs[2 * n:])
        _exchange_start(plan)
        _exchange_wait(plan)

    any_spec = pl.BlockSpec(memory_space=pl.ANY)
    return pl.pallas_call(
        body, name=name,
        out_shape=_exchange_out_shapes(arrs, gather),
        in_specs=[any_spec] * n,
        out_specs=tuple([any_spec] * n),
        scratch_shapes=_exchange_semaphores(n),
        compiler_params=pltpu.CompilerParams(has_side_effects=True),
    )(*arrs)


def _gather_via_sibling(block, name):
    def body(x_ref, out_ref, *sems):
        plan = _sibling_plan([x_ref], [out_ref], *sems)
        _sibling_start(plan)
        _sibling_forward(plan)
        _sibling_finish(plan)

    any_spec = pl.BlockSpec(memory_space=pl.ANY)
    return pl.pallas_call(
        body, name=name,
        out_shape=jax.ShapeDtypeStruct((N_DEV, *block.shape), block.dtype),
        in_specs=[any_spec], out_specs=any_spec,
        scratch_shapes=_exchange_semaphores(1),
        compiler_params=pltpu.CompilerParams(has_side_effects=True),
    )(block)


def _sibling_plan(ins, outs, send_sems, recv_sems, local_sems):
    x, y, c = lax.axis_index("x"), lax.axis_index("y"), lax.axis_index("c")
    me, sibling = (x, y, c), (x, y, 1 - c)
    chips = [(1 - x, y), (x, 1 - y), (1 - x, 1 - y)]
    local, first, ici_arrivals, passes, sibling_arrivals = [], [], [], [], []
    for a, (x_ref, out_ref) in enumerate(zip(ins, outs)):
        def slot(px, py, pc, out_ref=out_ref):
            return out_ref.at[4 * px + 2 * py + pc]

        def copy(k, block_of, to, src=None, a=a, slot=slot):
            idx = a * (N_DEV - 1) + k
            return pltpu.make_async_remote_copy(
                src_ref=slot(*block_of) if src is None else src, dst_ref=slot(*block_of), send_sem=send_sems.at[idx],
                recv_sem=recv_sems.at[idx], device_id=to, device_id_type=pl.DeviceIdType.MESH)

        local.append(pltpu.make_async_copy(x_ref, slot(*me), local_sems.at[a]))
        first += [copy(0, me, sibling, src=x_ref)] + [copy(1 + j, me, (*chip, c), src=x_ref) for j, chip in enumerate(chips)]
        ici_arrivals += [copy(1 + j, (*chip, c), me) for j, chip in enumerate(chips)]
        passes += [copy(4 + j, (*chip, c), sibling) for j, chip in enumerate(chips)]
        sibling_arrivals += [copy(0, sibling, me)] + [copy(4 + j, (*chip, 1 - c), me) for j, chip in enumerate(chips)]
    return local, first, ici_arrivals, passes, sibling_arrivals


def _sibling_start(plan):
    local, first = plan[0], plan[1]
    for cp in local + first:
        cp.start()


def _sibling_forward(plan):
    for arrival, onward in zip(plan[2], plan[3]):
        arrival.wait_recv()
        onward.start()


def _sibling_finish(plan):
    local, first, _, passes, sibling_arrivals = plan
    for arrival in sibling_arrivals:
        arrival.wait_recv()
    for cp in first + passes:
        cp.wait_send()
    for cp in local:
        cp.wait()


def _exchange_out_shapes(arrs, gather):
    return tuple(jax.ShapeDtypeStruct((N_DEV, *(a.shape if g else a.shape[1:])), a.dtype) for a, g in zip(arrs, gather))


def _exchange_semaphores(n):
    n_remote = n * (N_DEV - 1)
    return [pltpu.SemaphoreType.DMA((n_remote,)), pltpu.SemaphoreType.DMA((n_remote,)), pltpu.SemaphoreType.DMA((n,))]


def _exchange_plan(ins, outs, gather, send_sems, recv_sems, local_sems):
    n = len(ins)
    x, y, c = lax.axis_index("x"), lax.axis_index("y"), lax.axis_index("c")
    me = 4 * x + 2 * y + c

    def block_for(a, dev):
        return ins[a] if gather[a] else ins[a].at[dev]

    local = [pltpu.make_async_copy(block_for(a, me), outs[a].at[me], local_sems.at[a]) for a in range(n)]
    remote = []
    for k in range(1, N_DEV):
        px = 1 - x if k & 4 else x
        py = 1 - y if k & 2 else y
        pc = 1 - c if k & 1 else c
        peer = 4 * px + 2 * py + pc
        for a in range(n):
            idx = a * (N_DEV - 1) + k - 1
            send = pltpu.make_async_remote_copy(
                src_ref=block_for(a, peer), dst_ref=outs[a].at[me], send_sem=send_sems.at[idx], recv_sem=recv_sems.at[idx],
                device_id=(px, py, pc), device_id_type=pl.DeviceIdType.MESH)
            arrive = pltpu.make_async_remote_copy(
                src_ref=block_for(a, peer), dst_ref=outs[a].at[peer], send_sem=send_sems.at[idx], recv_sem=recv_sems.at[idx],
                device_id=(px, py, pc), device_id_type=pl.DeviceIdType.MESH)
            remote.append((send, arrive))
    return local, remote


def _exchange_start(plan):
    local, remote = plan
    for cp in local:
        cp.start()
    for send, _ in remote:
        send.start()


def _exchange_wait(plan):
    local, remote = plan
    for send, arrive in remote:
        send.wait_send()
        arrive.wait_recv()
    for cp in local:
        cp.wait()


def _riding_exchange(comm, n_grid):
    if comm is None:
        return [], [], (), (), [], lambda *_: None, lambda *_: None
    arrs, gather, *via_sibling = comm
    via_sibling = bool(via_sibling and via_sibling[0])
    n = len(arrs)
    any_spec = pl.BlockSpec(memory_space=pl.ANY)

    def at_step(step_of_first_axis):
        rest = [pl.program_id(d) == (0 if step_of_first_axis != "last" else pl.num_programs(d) - 1) for d in range(1, n_grid)]
        lead = {"first": 0, "middle": pl.num_programs(0) // 2, "last": pl.num_programs(0) - 1}[step_of_first_axis]
        return functools.reduce(jnp.logical_and, [pl.program_id(0) == lead] + rest)

    def begin(ins, outs, sems):
        @pl.when(at_step("first"))
        def _():
            if via_sibling:
                _sibling_start(_sibling_plan(ins, outs, *sems))
            else:
                _exchange_start(_exchange_plan(ins, outs, gather, *sems))

        if via_sibling:
            @pl.when(at_step("middle"))
            def _():
                _sibling_forward(_sibling_plan(ins, outs, *sems))

    def end(ins, outs, sems):
        @pl.when(at_step("last"))
        def _():
            if via_sibling:
                _sibling_finish(_sibling_plan(ins, outs, *sems))
            else:
                _exchange_wait(_exchange_plan(ins, outs, gather, *sems))

    return (list(arrs), [any_spec] * n, _exchange_out_shapes(arrs, gather), tuple([any_spec] * n), _exchange_semaphores(n),
            begin, end)


ADAM_TILE_ELEMS = 256 * 1024


def _sum_adam(parts, w, m, v, layer, prev, name):
    n_parts, r, c = parts.shape
    tm, tc = r, c
    if r % 16 == 0:
        while tm * c > ADAM_TILE_ELEMS and tm % 16 == 0:
            tm //= 2
    else:
        while r * tc > ADAM_TILE_ELEMS and tc % 256 == 0:
            tc //= 2

    def body(p_ref, w_ref, m_ref, v_ref, *rest):
        g_ref, d_ref, nm_ref, nv_ref = rest[-4:]
        g = p_ref[0].astype(F32)
        for s in range(1, n_parts):
            g = g + p_ref[s].astype(F32)
        m_new = ADAM_B1 * m_ref[...] + (1.0 - ADAM_B1) * g
        v_new = ADAM_B2 * v_ref[...] + (1.0 - ADAM_B2) * (g * g)
        m_hat = m_new / (1.0 - ADAM_B1 ** ADAM_STEP)
        v_hat = v_new / (1.0 - ADAM_B2 ** ADAM_STEP)
        g_ref[...] = g
        d_ref[...] = -ADAM_LR * (m_hat / (jnp.sqrt(v_hat) + ADAM_EPS) + ADAM_WD * w_ref[...])
        nm_ref[...] = m_new
        nv_ref[...] = v_new

    slab = pl.BlockSpec((None, tm, tc), lambda i, j: (layer, i, j))
    n_prev = 0 if prev is None else 4
    return pl.pallas_call(
        body, name=name,
        out_shape=(jax.ShapeDtypeStruct(w.shape, F32),) * 4,
        grid=(r // tm, c // tc),
        in_specs=[pl.BlockSpec((n_parts, tm, tc), lambda i, j: (0, i, j)), slab, slab, slab]
                 + [pl.BlockSpec(memory_space=pl.ANY)] * n_prev,
        out_specs=(slab, slab, slab, slab),
        input_output_aliases={4 + j: j for j in range(n_prev)},
        compiler_params=_params("parallel", "parallel"),
    )(parts, w, m, v, *(prev or ()))


def _permute_in(w):
    k = w.shape[0]
    q = w[:, 3584:5120].reshape(k, C_HEADS, QK)
    return jnp.concatenate(
        [w[:, 1536:3584], w[:, 5696:6720], w[:, 0:1536], q[:, :, :HEAD].reshape(k, C_HEADS * HEAD),
         q[:, :, HEAD:].reshape(k, C_HEADS * ROPE), w[:, 5120:5632], w[:, 5632:5696],
         jnp.zeros((k, PROJ_W - IN_WIDTH), w.dtype)], axis=1)


def _unpermute_in(g):
    k = g.shape[0]
    qn = g[:, O_M + M_QN:O_M + M_QR].reshape(k, C_HEADS, HEAD)
    qr = g[:, O_M + M_QR:O_M + M_CKV].reshape(k, C_HEADS, ROPE)
    q = jnp.concatenate([qn, qr], axis=-1).reshape(k, C_HEADS * QK)
    return jnp.concatenate(
        [g[:, O_A:O_A + W_A], g[:, O_B:O_B + W_B], q, g[:, O_M + M_CKV:O_M + M_KR],
         g[:, O_M + M_KR:O_M + M_KR + ROPE], g[:, O_CZ:O_CZ + W_CZ]], axis=1)


SMALL = ("attn_norm", "sgu_norm", "w_spatial", "b_spatial", "conv_b", "kv_norm", "q_nope_norm", "q_rope_norm",
         "k_nope_norm", "k_rope_norm", "out_norm", "ple_norm")
PACK_ROWS = 256


def _pack(tensors):
    flat = jnp.concatenate([t.reshape(-1) for t in tensors])
    rows = -(-flat.shape[0] // (128 * PACK_ROWS)) * PACK_ROWS
    return jnp.pad(flat, (0, rows * 128 - flat.shape[0])).reshape(rows, 128)


def _unpack(packed, like):
    flat = packed.reshape(-1)
    out, pos = [], 0
    for t in like:
        out.append(flat[pos:pos + t.size].reshape(t.shape))
        pos += t.size
    return out


def _tile(s_len, want):
    return min(want, s_len)


ATT_FWD_QUERIES = 512
ATT_FWD_KEYS = 8192
ATT_BWD_KEYS = 1024
ATT_BWD_QUERIES = 4096


def _layer_forward(h, p_l, cosf, sins, w, sm, comm, comm_rest):
    s_len = h.shape[0]
    tm = _tile(s_len, 512)
    proj, hn, rest = _norm_matmul(h, sm["attn_norm"], w["w_in"], _tile(s_len, 1024), 768, comm_rest)
    if comm_rest is not None:
        w = {**w, **_assemble_rest(rest)}
    ga, gb, gc = sm["out_norm"][:, 0:512], sm["out_norm"][:, 512:1024], sm["out_norm"][:, 1024:2048]
    y = _sgu_forward(proj, sm["sgu_norm"], sm["w_spatial"], sm["b_spatial"], ga, _tile(s_len, 256))
    y = _conv_forward(proj, w["conv_w"], sm["conv_b"], gb, y, _tile(s_len, 256))
    q, k, v, kt, vt = _mla_forward(proj, cosf, sins, w["w_ukv"], sm["kv_norm"], sm["q_nope_norm"], sm["q_rope_norm"],
                                   sm["k_nope_norm"], sm["k_rope_norm"], tm, _tile(s_len, ATT_BWD_KEYS),
                                   _tile(s_len, ATT_FWD_KEYS))
    o, lse, arrived = _attention_forward(q, k, vt, _tile(s_len, ATT_FWD_QUERIES), _tile(s_len, ATT_BWD_QUERIES), comm)
    y = _cgate_forward(o, proj, gc, y, _tile(s_len, 256))
    h1 = _out_matmul(h, y, w["w_out"], _tile(s_len, 1024), 1024)
    h2, n1, gate, pp = _ple_forward(h1, sm["ple_norm"], p_l, w["w_ple_gate"], w["w_ple_proj"], tm, 1024)
    saved = dict(h=h, hn=hn, proj=proj, y=y, q=q, k=k, v=v, kt=kt, o=o, lse=lse, h1=h1, n1=n1, gate=gate, pp=pp)
    return h2, saved, w, arrived


def _layer_backward(dh2, p_l, cosf, sins, w, sm, sv, comm, scatter_own):
    s_len = dh2.shape[0]
    tm = _tile(s_len, 512)
    tr = _tile(s_len, 256)
    big, small = {}, {}
    dh1, small["ple_norm"], dgp, dpp = _ple_backward(dh2, sv["gate"], sv["pp"], w["w_ple_gate"], sv["h1"], sm["ple_norm"], tm)
    big["w_ple_proj"], _ = _matmul_tn(p_l, dpp, _tile(s_len, 2048), PLE_DIM, 1024, "grad_w_ple_proj")
    big["w_ple_gate"], _ = _matmul_tn(sv["n1"], dgp, _tile(s_len, 2048), 1024, 1024, "grad_w_ple_gate")
    dy, _ = _matmul_nt(dh1, w["w_out"], _tile(s_len, 1024), 1024, "grad_branches")
    big["w_out"], _ = _matmul_tn(sv["y"], dh1, _tile(s_len, 2048), 1024, 1024, "grad_w_out")
    ga, gb, gc = sm["out_norm"][:, 0:512], sm["out_norm"][:, 512:1024], sm["out_norm"][:, 1024:2048]
    dproj, dcw, small["conv_b"], dgb = _conv_backward(sv["proj"], w["conv_w"], sm["conv_b"], gb, dy, tr)
    big["conv_w"] = dcw
    dproj, do, dsum, dgc = _cgate_backward(sv["o"], sv["proj"], gc, dy, dproj, tm, _tile(s_len, ATT_BWD_QUERIES))
    dqt, dk, dv, arrived = _attention_backward(sv["q"], sv["k"], sv["kt"], sv["v"], do, sv["lse"], dsum, comm)
    (dproj, big["w_ukv"], small["kv_norm"], small["q_nope_norm"], small["q_rope_norm"], small["k_nope_norm"],
     small["k_rope_norm"]) = _mla_backward(sv["proj"], cosf, sins, w["w_ukv"], sm["kv_norm"], sm["q_nope_norm"],
                                            sm["q_rope_norm"], sm["k_nope_norm"], sm["k_rope_norm"], dqt, dk, dv, dproj, tr)
    dproj, small["sgu_norm"], small["w_spatial"], small["b_spatial"], dga = _sgu_backward(
        sv["proj"], sm["sgu_norm"], sm["w_spatial"], sm["b_spatial"], ga, dy, dproj, tm)
    small["out_norm"] = jnp.concatenate([dga, dgb, dgc], axis=1)
    parts_rest = _parts_rest(big)
    g_in, arrived_rest = _matmul_tn(sv["hn"], dproj, _tile(s_len, 2048), 512, 2304, "grad_w_in",
                                    (parts_rest, [False] * len(parts_rest)) if scatter_own else None)
    parts = [_part_w_in(g_in)] + parts_rest
    dhn, arrived_in = _matmul_nt(dproj, w["w_in"], _tile(s_len, 1024), 256, "grad_attn_norm_in",
                                 (parts[:1], [False]) if scatter_own else None)
    dh, small["attn_norm"] = _rms_backward(sv["h"], sm["attn_norm"], dhn, dh1, tr, "attn_norm_backward")
    return dh, parts, big["conv_w"], small, arrived, (*arrived_in, *arrived_rest) if scatter_own else None


def _layer_small(params, layer):
    return dict(
        attn_norm=params["attn_norm"][layer][None, :], sgu_norm=params["sgu_norm"][layer],
        w_spatial=params["w_spatial"][layer], b_spatial=params["b_spatial"][layer][:, :, None],
        conv_b=params["conv_b"][layer][None, :], kv_norm=params["kv_norm"][layer][None, :],
        q_nope_norm=params["q_nope_norm"][layer][None, :], q_rope_norm=params["q_rope_norm"][layer][None, :],
        k_nope_norm=params["k_nope_norm"][layer][None, :], k_rope_norm=params["k_rope_norm"][layer][None, :],
        out_norm=params["out_norm"][layer][None, :], ple_norm=params["ple_norm"][layer][None, :])


BIG = ("w_in", "w_ukv", "w_out", "w_ple_gate", "w_ple_proj")


def _assemble_w_in(g_in):
    return _permute_in(g_in.transpose(1, 0, 2).reshape(g_in.shape[1], IN_WIDTH))


def _assemble_rest(gathered):
    g_ukv, g_out, g_gate, g_proj = gathered
    w_ukv = g_ukv.reshape(N_DEV, KV_RANK, 2, HEAD).transpose(1, 2, 0, 3).reshape(KV_RANK, 2 * C_HEADS * HEAD)
    return dict(w_ukv=w_ukv, w_out=g_out.reshape(D_MODEL, D_MODEL), w_ple_gate=g_gate.reshape(D_MODEL, D_MODEL),
                w_ple_proj=g_proj.transpose(1, 0, 2).reshape(PLE_DIM, D_MODEL))


def _part_w_in(g_in):
    return _unpermute_in(g_in).reshape(g_in.shape[0], N_DEV, -1).transpose(1, 2, 0).astype(BF16)


def _parts_rest(big):
    return [
        big["w_ukv"].reshape(KV_RANK, 2, N_DEV, HEAD).transpose(2, 0, 1, 3).reshape(N_DEV, KV_RANK, 2 * HEAD).astype(BF16),
        big["w_out"].reshape(N_DEV, -1, D_MODEL).astype(BF16),
        big["w_ple_gate"].reshape(N_DEV, -1, D_MODEL).astype(BF16),
        big["w_ple_proj"].reshape(PLE_DIM, N_DEV, -1).transpose(1, 0, 2).astype(BF16)]


def _step_local(xs, ps, pos, target, shards, conv_w, params):
    inv = 1.0 / (ROPE_BASE ** (jnp.arange(0, ROPE, 2, dtype=F32) / ROPE))
    ang = pos.astype(F32)[:, None] * inv
    cos, sin = jnp.cos(ang), jnp.sin(ang)
    cosf = jnp.concatenate([cos, cos], axis=-1)
    sins = jnp.concatenate([-sin, sin], axis=-1)

    def gather_of(names, layer):
        return [shards[n][layer] for n in names], [True] * len(names)

    h = xs
    saved, weights = [], []
    smalls = [_layer_small(params, layer) for layer in range(DEPTH)]
    first_w_in = _gather_via_sibling(shards["w_in"][0], "gather_first_w_in")
    w = dict(w_in=_assemble_w_in(first_w_in), conv_w=conv_w[0])
    for layer in range(DEPTH):
        comm = (*gather_of(BIG, layer + 1), True) if layer + 1 < DEPTH else None
        comm_rest = gather_of(BIG[1:], 0) if layer == 0 else None
        h, sv, w, arrived = _layer_forward(h, ps[layer], cosf, sins, w, smalls[layer], comm, comm_rest)
        saved.append(sv)
        weights.append(w)
        if comm is not None:
            w = dict(w_in=_assemble_w_in(arrived[0]), conv_w=conv_w[layer + 1], **_assemble_rest(arrived[1:]))
    dh, loss = _loss_grad(h, target, _tile(h.shape[0], 512))
    received, conv_grads, small_grads = [None] * DEPTH, [None] * DEPTH, [None] * DEPTH
    comm = None
    for layer in reversed(range(DEPTH)):
        dh, parts, conv_grads[layer], small_grads[layer], arrived, arrived_own = _layer_backward(
            dh, ps[layer], cosf, sins, weights[layer], smalls[layer], saved[layer], comm, layer == 0)
        if comm is not None:
            received[layer + 1] = arrived
        comm = (parts, [False] * len(parts))
    received[0] = arrived_own
    return loss, dh, received, conv_grads, small_grads


def kernel(x, p, positions, attn_norm, w_in, sgu_norm, w_spatial, b_spatial, conv_w, conv_b, kv_norm, w_ukv, q_nope_norm, q_rope_norm, k_nope_norm, k_rope_norm, out_norm, w_out, ple_norm, w_ple_gate, w_ple_proj, loss_target, m_attn_norm, m_w_in, m_sgu_norm, m_w_spatial, m_b_spatial, m_conv_w, m_conv_b, m_kv_norm, m_w_ukv, m_q_nope_norm, m_q_rope_norm, m_k_nope_norm, m_k_rope_norm, m_out_norm, m_w_out, m_ple_norm, m_w_ple_gate, m_w_ple_proj, v_attn_norm, v_w_in, v_sgu_norm, v_w_spatial, v_b_spatial, v_conv_w, v_conv_b, v_kv_norm, v_w_ukv, v_q_nope_norm, v_q_rope_norm, v_k_nope_norm, v_k_rope_norm, v_out_norm, v_w_out, v_ple_norm, v_w_ple_gate, v_w_ple_proj):
    order = ("attn_norm", "w_in", "sgu_norm", "w_spatial", "b_spatial", "conv_w", "conv_b", "kv_norm", "w_ukv",
             "q_nope_norm", "q_rope_norm", "k_nope_norm", "k_rope_norm", "out_norm", "w_out", "ple_norm", "w_ple_gate",
             "w_ple_proj")
    wts = dict(zip(order, (attn_norm, w_in, sgu_norm, w_spatial, b_spatial, conv_w, conv_b, kv_norm, w_ukv, q_nope_norm,
                           q_rope_norm, k_nope_norm, k_rope_norm, out_norm, w_out, ple_norm, w_ple_gate, w_ple_proj)))
    mom = dict(zip(order, (m_attn_norm, m_w_in, m_sgu_norm, m_w_spatial, m_b_spatial, m_conv_w, m_conv_b, m_kv_norm, m_w_ukv,
                           m_q_nope_norm, m_q_rope_norm, m_k_nope_norm, m_k_rope_norm, m_out_norm, m_w_out, m_ple_norm,
                           m_w_ple_gate, m_w_ple_proj)))
    var = dict(zip(order, (v_attn_norm, v_w_in, v_sgu_norm, v_w_spatial, v_b_spatial, v_conv_w, v_conv_b, v_kv_norm, v_w_ukv,
                           v_q_nope_norm, v_q_rope_norm, v_k_nope_norm, v_k_rope_norm, v_out_norm, v_w_out, v_ple_norm,
                           v_w_ple_gate, v_w_ple_proj)))

    conv_shard = wts["conv_w"]
    (conv_all,) = _exchange([conv_shard.reshape(-1, 128)], [True], "gather_conv_w")
    conv_full = conv_all.reshape(N_DEV, DEPTH, 3, -1).transpose(1, 2, 0, 3).reshape(DEPTH, 3, -1)
    shards = {n: wts[n].astype(BF16) for n in BIG}
    loss_part, grad_x, received, conv_grads, small_grads = _step_local(
        x[0], p[:, 0], positions[0], loss_target[0], shards, conv_full, wts)
    loss = lax.psum(loss_part[0, 0], ("x", "y", "c"))

    def small_grad(name):
        g = jnp.stack([sg[name] for sg in small_grads])
        return g.reshape(wts[name].shape)

    conv_grad = jnp.stack(conv_grads)
    like = [wts[n] for n in SMALL] + [conv_grad]
    packed = _pack([small_grad(n) for n in SMALL] + [conv_grad])
    small_parts = _gather_via_sibling(packed, "gather_small_grads")
    filler = [jnp.zeros_like(conv_grad), jnp.zeros_like(conv_grad), jnp.ones_like(conv_grad)]
    small_out = _sum_adam(small_parts, *(_pack([src[n] for n in SMALL] + [fill])[None] for src, fill in zip((wts, mom, var), filler)),
                          0, None, "adam_small")
    unpacked = [_unpack(o[0], like) for o in small_out]
    results = {n: vals for n, vals in zip(SMALL, zip(*[u[:-1] for u in unpacked]))}
    me = 4 * lax.axis_index("x") + 2 * lax.axis_index("y") + lax.axis_index("c")
    width = conv_shard.shape[2]
    conv_local = lax.dynamic_slice_in_dim(unpacked[0][-1], me * width, width, axis=2)
    as_slab = (lambda t: t.reshape(1, -1, width))
    conv_out = _sum_adam(as_slab(conv_local), as_slab(conv_shard), as_slab(mom["conv_w"]), as_slab(var["conv_w"]), 0, None,
                         "adam_conv_w")
    results["conv_w"] = tuple(o.reshape(conv_shard.shape) for o in conv_out)

    for j, name in enumerate(BIG):
        view = (lambda t: jnp.swapaxes(t, 1, 2)) if name == "w_in" else (lambda t: t)
        outs = None
        for layer in range(DEPTH):
            outs = _sum_adam(received[layer][j], view(wts[name]), view(mom[name]), view(var[name]), layer, outs, "adam_" + name)
        results[name] = tuple(view(o) for o in outs)

    grads, deltas, new_m, new_v = ([results[n][j] for n in order] for j in range(4))
    return (loss, grad_x[None], *grads, *deltas, *new_m, *new_v)
```

```python
import functools

import jax
import jax.numpy as jnp
from jax import lax
from jax.experimental import pallas as pl
from jax.experimental.pallas import tpu as pltpu

F32 = jnp.float32
BF16 = jnp.bfloat16

N_DEV = 8
DEPTH = 4
D_MODEL = 2048
EPS = 1e-6
CHUNK = 128
A_HEADS = 4
HEAD = 128
ROPE = 64
HALF = ROPE // 2
C_HEADS = 8
KV_RANK = 512
PLE_DIM = 256
ROPE_BASE = 10000.0
IN_WIDTH = 6720
QK = HEAD + ROPE
SCALE = QK ** -0.5
LOG2_E = 1.4426950408889634
LN_2 = 0.6931471805599453
Q_SCALE = SCALE * LOG2_E
HALO = 8

O_B = 0
W_B = 2048
O_CZ = 2048
W_CZ = 1024
O_A = 3072
W_A = 1536
O_M = 4608
W_M = 2304
M_QN, M_QR, M_CKV, M_KR = 0, 1024, 1536, 2048
PROJ_W = 6912

ADAM_LR = 0.001
ADAM_B1 = 0.9
ADAM_B2 = 0.999
ADAM_EPS = 1e-08
ADAM_WD = 0.01
ADAM_STEP = 10

VMEM_LIMIT = 56 * 1024 * 1024

_NT = (((1,), (1,)), ((), ()))
_TN = (((0,), (0,)), ((), ()))


def _params(*sem):
    return pltpu.CompilerParams(dimension_semantics=sem, vmem_limit_bytes=VMEM_LIMIT)


@jax.custom_vjp
def _bdot(a, b):
    return jnp.dot(a.astype(BF16), b.astype(BF16), preferred_element_type=F32)


def _bdot_fwd(a, b):
    return _bdot(a, b), (a, b)


def _bdot_bwd(res, g):
    a, b = res
    gb = g.astype(BF16)
    da = lax.dot_general(gb, b.astype(BF16), _NT, preferred_element_type=F32)
    db = lax.dot_general(a.astype(BF16), gb, _TN, preferred_element_type=F32)
    return da.astype(a.dtype), db.astype(b.dtype)


_bdot.defvjp(_bdot_fwd, _bdot_bwd)


@functools.partial(jax.custom_vjp, nondiff_argnums=(1,))
def _split(x, n):
    w = x.shape[-1] // n
    return tuple(x[:, i * w:(i + 1) * w] for i in range(n))


def _split_fwd(x, n):
    return _split(x, n), None


def _split_bwd(n, _, gs):
    return (jnp.concatenate(gs, axis=-1),)


_split.defvjp(_split_fwd, _split_bwd)


@functools.partial(jax.custom_vjp, nondiff_argnums=(1,))
def _shift_rows(x, k):
    return pltpu.roll(x, k % x.shape[0], 0)


def _shift_rows_fwd(x, k):
    return _shift_rows(x, k), None


def _shift_rows_bwd(k, _, g):
    return (_shift_rows(g, -k),)


_shift_rows.defvjp(_shift_rows_fwd, _shift_rows_bwd)


@jax.custom_vjp
def _swap_halves(x):
    h = x.shape[-1] // 2
    return jnp.concatenate([x[:, h:], x[:, :h]], axis=-1)


def _swap_halves_fwd(x):
    return _swap_halves(x), None


def _swap_halves_bwd(_, g):
    return (_swap_halves(g),)


_swap_halves.defvjp(_swap_halves_fwd, _swap_halves_bwd)


def _rms(x, g):
    return x * lax.rsqrt(jnp.mean(x * x, axis=-1, keepdims=True) + EPS) * g


def _rope(x, cosf, sins):
    return x * cosf + _swap_halves(x) * sins


def _sgu_chunk(u, v, z, gain, ws, bs, ga):
    ys = []
    for h in range(A_HEADS):
        vn = _rms(v[h], gain[h])
        s = _bdot(ws[h], vn) + bs[h]
        ys.append(u[h] * s * jax.nn.silu(z[h]))
    ss = sum(jnp.sum(y * y, axis=-1, keepdims=True) for y in ys) * (1.0 / (A_HEADS * HEAD))
    r = lax.rsqrt(ss + EPS)
    return tuple(ys[h] * r * ga[h] for h in range(A_HEADS))


def _conv_tile(bb, bc, bh, bz, w0, w1, w2, cb, gb, w0h, w1h, w2h, cbh, gbh, valid, core):
    t = jnp.where(valid, bc * bh, 0.0)
    y = (jnp.where(core, cb, cbh)
         + _shift_rows(t, 1) * jnp.where(core, w0, w0h)
         + t * jnp.where(core, w1, w1h)
         + _shift_rows(t, -1) * jnp.where(core, w2, w2h))
    return _rms(bb * y * jax.nn.silu(bz), jnp.where(core, gb, gbh))


def _cgate_tile(o, cz, gc):
    return _rms(o * jax.nn.silu(cz), gc)


def _mla_tile(qn, qr, ckv, kr, cosf, sins, wukv, kvg, qng, qrg, kng, krg):
    kv = _split(_bdot(_rms(ckv, kvg), wukv), 2 * C_HEADS)
    k_r = _rope(_rms(kr, krg), cosf, sins)
    qn_h = _split(qn, C_HEADS)
    qr_h = _split(qr, C_HEADS)
    q, k, v = [], [], []
    for h in range(C_HEADS):
        q.append(jnp.concatenate([_rms(qn_h[h], qng), _rope(_rms(qr_h[h], qrg), cosf, sins)], axis=-1) * Q_SCALE)
        k.append(jnp.concatenate([_rms(kv[h], kng), k_r], axis=-1))
        v.append(kv[C_HEADS + h])
    return tuple(q), tuple(k), tuple(v)


def _norm_matmul(h, gain, w, tm, tn, comm=None):
    s_len, k = h.shape
    n = w.shape[1]
    c_ops, c_in_specs, c_shapes, c_out_specs, c_sems, c_begin, c_end = _riding_exchange(comm, 2)
    n_c = len(c_ops)

    def body(h_ref, g_ref, w_ref, *rest):
        c_ins, (o_ref, hn_ref), c_outs, sems = rest[:n_c], rest[n_c:n_c + 2], rest[n_c + 2:2 * n_c + 2], rest[2 * n_c + 2:]
        c_begin(c_ins, c_outs, sems)

        @pl.when(pl.program_id(1) == 0)
        def _():
            hn_ref[...] = _rms(h_ref[...], g_ref[...]).astype(BF16)

        o_ref[...] = jnp.dot(hn_ref[...], w_ref[...], preferred_element_type=F32).astype(BF16)
        c_end(c_ins, c_outs, sems)

    outs = pl.pallas_call(
        body, name="norm_matmul",
        out_shape=(jax.ShapeDtypeStruct((s_len, n), BF16), jax.ShapeDtypeStruct((s_len, k), BF16), *c_shapes),
        grid=(s_len // tm, n // tn),
        in_specs=[pl.BlockSpec((tm, k), lambda i, j: (i, 0)), pl.BlockSpec((1, k), lambda i, j: (0, 0)),
                  pl.BlockSpec((k, tn), lambda i, j: (0, j))] + c_in_specs,
        out_specs=(pl.BlockSpec((tm, tn), lambda i, j: (i, j)), pl.BlockSpec((tm, k), lambda i, j: (i, 0)), *c_out_specs),
        scratch_shapes=c_sems,
        compiler_params=_params("arbitrary", "arbitrary") if comm else _params("parallel", "arbitrary"),
    )(h, gain, w, *c_ops)
    return outs[0], outs[1], outs[2:]


def _out_matmul(h, y, w, tm, tn):
    s_len, n = h.shape
    k = y.shape[1]

    def body(h_ref, y_ref, w_ref, o_ref):
        o_ref[...] = h_ref[...] + jnp.dot(y_ref[...], w_ref[...], preferred_element_type=F32)

    return pl.pallas_call(
        body, name="out_matmul",
        out_shape=jax.ShapeDtypeStruct((s_len, n), F32),
        grid=(s_len // tm, n // tn),
        in_specs=[pl.BlockSpec((tm, tn), lambda i, j: (i, j)), pl.BlockSpec((tm, k), lambda i, j: (i, 0)),
                  pl.BlockSpec((k, tn), lambda i, j: (0, j))],
        out_specs=pl.BlockSpec((tm, tn), lambda i, j: (i, j)),
        compiler_params=_params("parallel", "parallel"),
    )(h, y, w)


def _ple_forward(h1, gain, p, wg, wp, tm, tn):
    s_len, d = h1.shape
    kp = p.shape[1]

    def body(hrow_ref, g_ref, p_ref, wg_ref, wp_ref, o_ref, n1_ref, gate_ref, pp_ref):
        j = pl.program_id(1)

        @pl.when(j == 0)
        def _():
            n1_ref[...] = _rms(hrow_ref[...], g_ref[...]).astype(BF16)

        gate = jax.nn.sigmoid(jnp.dot(n1_ref[...], wg_ref[...], preferred_element_type=F32))
        pp = jnp.dot(p_ref[...].astype(BF16), wp_ref[...], preferred_element_type=F32)
        o_ref[...] = hrow_ref[:, pl.ds(pl.multiple_of(j * tn, tn), tn)] + gate * pp
        gate_ref[...] = gate.astype(BF16)
        pp_ref[...] = pp.astype(BF16)

    col = pl.BlockSpec((tm, tn), lambda i, j: (i, j))
    return pl.pallas_call(
        body, name="ple_forward",
        out_shape=(jax.ShapeDtypeStruct((s_len, d), F32), jax.ShapeDtypeStruct((s_len, d), BF16),
                   jax.ShapeDtypeStruct((s_len, d), BF16), jax.ShapeDtypeStruct((s_len, d), BF16)),
        grid=(s_len // tm, d // tn),
        in_specs=[pl.BlockSpec((tm, d), lambda i, j: (i, 0)), pl.BlockSpec((1, d), lambda i, j: (0, 0)),
                  pl.BlockSpec((tm, kp), lambda i, j: (i, 0)), pl.BlockSpec((d, tn), lambda i, j: (0, j)),
                  pl.BlockSpec((kp, tn), lambda i, j: (0, j))],
        out_specs=(col, pl.BlockSpec((tm, d), lambda i, j: (i, 0)), col, col),
        compiler_params=_params("parallel", "arbitrary"),
    )(h1, gain, p, wg, wp)


def _matmul_nt(a, b, tm, tk, name, comm=None):
    m, n = a.shape
    k = b.shape[0]
    c_ops, c_in_specs, c_shapes, c_out_specs, c_sems, c_begin, c_end = _riding_exchange(comm, 2)
    n_c = len(c_ops)

    def body(a_ref, b_ref, *rest):
        c_ins, o_ref, c_outs, sems = rest[:n_c], rest[n_c], rest[n_c + 1:2 * n_c + 1], rest[2 * n_c + 1:]
        c_begin(c_ins, c_outs, sems)
        o_ref[...] = lax.dot_general(a_ref[...].astype(BF16), b_ref[...].astype(BF16), _NT, preferred_element_type=F32)
        c_end(c_ins, c_outs, sems)

    outs = pl.pallas_call(
        body, name=name,
        out_shape=(jax.ShapeDtypeStruct((m, k), F32), *c_shapes),
        grid=(m // tm, k // tk),
        in_specs=[pl.BlockSpec((tm, n), lambda i, j: (i, 0)), pl.BlockSpec((tk, n), lambda i, j: (j, 0))] + c_in_specs,
        out_specs=(pl.BlockSpec((tm, tk), lambda i, j: (i, j)), *c_out_specs),
        scratch_shapes=c_sems,
        compiler_params=_params("arbitrary", "arbitrary") if comm else _params("parallel", "parallel"),
    )(a, b, *c_ops)
    return outs[0], outs[1:]


def _matmul_tn(a, b, tm, tk, tn, name, comm=None):
    m, k = a.shape
    n = b.shape[1]
    n_m = m // tm
    c_ops, c_in_specs, c_shapes, c_out_specs, c_sems, c_begin, c_end = _riding_exchange(comm, 3)
    n_c = len(c_ops)

    def body(a_ref, b_ref, *rest):
        c_ins, o_ref, c_outs, acc_ref, sems = rest[:n_c], rest[n_c], rest[n_c + 1:2 * n_c + 1], rest[2 * n_c + 1], rest[2 * n_c + 2:]
        c_begin(c_ins, c_outs, sems)
        part = lax.dot_general(a_ref[...].astype(BF16), b_ref[...].astype(BF16), _TN, preferred_element_type=F32)
        _acc(acc_ref, part, pl.program_id(2) == 0)

        @pl.when(pl.program_id(2) == n_m - 1)
        def _():
            o_ref[...] = acc_ref[...].astype(BF16)

        c_end(c_ins, c_outs, sems)

    outs = pl.pallas_call(
        body, name=name,
        out_shape=(jax.ShapeDtypeStruct((k, n), BF16), *c_shapes),
        grid=(k // tk, n // tn, n_m),
        in_specs=[pl.BlockSpec((tm, tk), lambda kk, nn, mm: (mm, kk)), pl.BlockSpec((tm, tn), lambda kk, nn, mm: (mm, nn))]
                 + c_in_specs,
        out_specs=(pl.BlockSpec((tk, tn), lambda kk, nn, mm: (kk, nn)), *c_out_specs),
        scratch_shapes=[pltpu.VMEM((tk, tn), F32)] + c_sems,
        compiler_params=_params("arbitrary", "arbitrary", "arbitrary") if comm else _params("parallel", "parallel", "arbitrary"),
    )(a, b, *c_ops)
    return outs[0], outs[1:]


def _acc(ref, val, first):
    @pl.when(first)
    def _():
        ref[...] = val

    @pl.when(jnp.logical_not(first))
    def _():
        ref[...] += val


def _loss_grad(h, target, tm):
    s_len, d = h.shape

    def body(h_ref, t_ref, dh_ref, loss_ref):
        e = h_ref[...] - t_ref[...]
        dh_ref[...] = e * (1.0 / d)
        part = jnp.sum(jnp.sum(e * e, axis=-1, keepdims=True), axis=0, keepdims=True) * (0.5 / d)
        _acc(loss_ref, jnp.broadcast_to(part, loss_ref.shape), pl.program_id(0) == 0)

    row = pl.BlockSpec((tm, d), lambda i: (i, 0))
    return pl.pallas_call(
        body, name="loss_grad",
        out_shape=(jax.ShapeDtypeStruct((s_len, d), F32), jax.ShapeDtypeStruct((1, 128), F32)),
        grid=(s_len // tm,),
        in_specs=[row, row],
        out_specs=(row, pl.BlockSpec((1, 128), lambda i: (0, 0))),
        compiler_params=_params("arbitrary"),
    )(h, target)


def _rms_backward(x, gain, dn, dres, tm, name):
    s_len, d = x.shape

    def body(x_ref, g_ref, dn_ref, dres_ref, dx_ref, dg_ref):
        _, vjp = jax.vjp(_rms, x_ref[...], g_ref[...])
        dx, dg = vjp(dn_ref[...])
        dx_ref[...] = dres_ref[...] + dx
        _acc(dg_ref, dg, pl.program_id(0) == 0)

    row = pl.BlockSpec((tm, d), lambda i: (i, 0))
    vec = pl.BlockSpec((1, d), lambda i: (0, 0))
    return pl.pallas_call(
        body, name=name,
        out_shape=(jax.ShapeDtypeStruct((s_len, d), F32), jax.ShapeDtypeStruct((1, d), F32)),
        grid=(s_len // tm,),
        in_specs=[row, vec, row, row],
        out_specs=(row, vec),
        compiler_params=_params("arbitrary"),
    )(x, gain, dn, dres)


def _ple_backward(dh2, gate, pp, wg, h1, gain, tm):
    s_len, d = dh2.shape

    def body(dh_ref, gate_ref, pp_ref, wg_ref, h1_ref, g_ref, dh1_ref, dgain_ref, dgp_ref, dpp_ref):
        dh = dh_ref[...]
        gate = gate_ref[...].astype(F32)
        dgp = (dh * pp_ref[...].astype(F32) * gate * (1.0 - gate)).astype(BF16)
        dgp_ref[...] = dgp
        dpp_ref[...] = (dh * gate).astype(BF16)
        dn = lax.dot_general(dgp, wg_ref[...], _NT, preferred_element_type=F32)
        _, vjp = jax.vjp(_rms, h1_ref[...], g_ref[...])
        dx, dgain = vjp(dn)
        dh1_ref[...] = dh + dx
        _acc(dgain_ref, dgain, pl.program_id(0) == 0)

    row = pl.BlockSpec((tm, d), lambda i: (i, 0))
    vec = pl.BlockSpec((1, d), lambda i: (0, 0))
    return pl.pallas_call(
        body, name="ple_backward",
        out_shape=(jax.ShapeDtypeStruct((s_len, d), F32), jax.ShapeDtypeStruct((1, d), F32),
                   jax.ShapeDtypeStruct((s_len, d), BF16), jax.ShapeDtypeStruct((s_len, d), BF16)),
        grid=(s_len // tm,),
        in_specs=[row, row, row, pl.BlockSpec((d, d), lambda i: (0, 0)), row, vec],
        out_specs=(row, vec, row, row),
        compiler_params=_params("arbitrary"),
    )(dh2, gate, pp, wg, h1, gain)


def _sgu_in_specs(tm):
    return [pl.BlockSpec((tm, W_A), lambda i: (i, O_A // W_A)),
            pl.BlockSpec((A_HEADS, HEAD), lambda i: (0, 0)), pl.BlockSpec((A_HEADS, CHUNK, CHUNK), lambda i: (0, 0, 0)),
            pl.BlockSpec((A_HEADS, CHUNK, 1), lambda i: (0, 0, 0)), pl.BlockSpec((1, 512), lambda i: (0, 0))]


def _sgu_load(a_ref, gain_ref, ws_ref, bs_ref, ga_ref, c):
    rows = slice(c * CHUNK, (c + 1) * CHUNK)
    heads = range(A_HEADS)
    u = tuple(a_ref[rows, h * HEAD:(h + 1) * HEAD].astype(F32) for h in heads)
    v = tuple(a_ref[rows, 512 + h * HEAD:512 + (h + 1) * HEAD].astype(F32) for h in heads)
    z = tuple(a_ref[rows, 1024 + h * HEAD:1024 + (h + 1) * HEAD].astype(F32) for h in heads)
    gain = tuple(gain_ref[h:h + 1, :] for h in heads)
    ws = tuple(ws_ref[h] for h in heads)
    bs = tuple(bs_ref[h] for h in heads)
    ga = tuple(ga_ref[:, h * HEAD:(h + 1) * HEAD] for h in heads)
    return u, v, z, gain, ws, bs, ga


def _sgu_forward(proj, gain, ws, bs, ga, tm):
    s_len = proj.shape[0]

    def body(a_ref, gain_ref, ws_ref, bs_ref, ga_ref, o_ref):
        for c in range(tm // CHUNK):
            out = _sgu_chunk(*_sgu_load(a_ref, gain_ref, ws_ref, bs_ref, ga_ref, c))
            for h in range(A_HEADS):
                o_ref[c * CHUNK:(c + 1) * CHUNK, h * HEAD:(h + 1) * HEAD] = out[h].astype(BF16)

    return pl.pallas_call(
        body, name="sgu_forward",
        out_shape=jax.ShapeDtypeStruct((s_len, D_MODEL), BF16),
        grid=(s_len // tm,),
        in_specs=_sgu_in_specs(tm),
        out_specs=pl.BlockSpec((tm, 512), lambda i: (i, 0)),
        compiler_params=_params("parallel"),
    )(proj, gain, ws, bs, ga)


def _sgu_backward(proj, gain, ws, bs, ga, dy, dproj, tm):
    s_len = proj.shape[0]

    def body(a_ref, gain_ref, ws_ref, bs_ref, ga_ref, dy_ref, _, da_ref, dgain_ref, dws_ref, dbs_ref, dga_ref):
        tot = None
        for c in range(tm // CHUNK):
            args = _sgu_load(a_ref, gain_ref, ws_ref, bs_ref, ga_ref, c)
            _, vjp = jax.vjp(_sgu_chunk, *args)
            rows = slice(c * CHUNK, (c + 1) * CHUNK)
            du, dv, dz, dgain, dws, dbs, dga = vjp(tuple(dy_ref[rows, h * HEAD:(h + 1) * HEAD] for h in range(A_HEADS)))
            for h in range(A_HEADS):
                da_ref[rows, h * HEAD:(h + 1) * HEAD] = du[h].astype(BF16)
                da_ref[rows, 512 + h * HEAD:512 + (h + 1) * HEAD] = dv[h].astype(BF16)
                da_ref[rows, 1024 + h * HEAD:1024 + (h + 1) * HEAD] = dz[h].astype(BF16)
            part = (dgain, dws, dbs, dga)
            tot = part if tot is None else jax.tree.map(jnp.add, tot, part)
        dgain, dws, dbs, dga = tot
        first = pl.program_id(0) == 0
        _acc(dgain_ref, jnp.concatenate(dgain, axis=0), first)
        _acc(dga_ref, jnp.concatenate(dga, axis=-1), first)
        for h in range(A_HEADS):
            _acc(dws_ref.at[h], dws[h], first)
            _acc(dbs_ref.at[h], dbs[h], first)

    small = [pl.BlockSpec((A_HEADS, HEAD), lambda i: (0, 0)), pl.BlockSpec((A_HEADS, CHUNK, CHUNK), lambda i: (0, 0, 0)),
             pl.BlockSpec((A_HEADS, CHUNK, 1), lambda i: (0, 0, 0)), pl.BlockSpec((1, 512), lambda i: (0, 0))]
    return pl.pallas_call(
        body, name="sgu_backward",
        out_shape=(jax.ShapeDtypeStruct(dproj.shape, BF16),
                   jax.ShapeDtypeStruct((A_HEADS, HEAD), F32), jax.ShapeDtypeStruct((A_HEADS, CHUNK, CHUNK), F32),
                   jax.ShapeDtypeStruct((A_HEADS, CHUNK, 1), F32), jax.ShapeDtypeStruct((1, 512), F32)),
        grid=(s_len // tm,),
        in_specs=_sgu_in_specs(tm) + [pl.BlockSpec((tm, 512), lambda i: (i, 0)), pl.BlockSpec(memory_space=pl.ANY)],
        out_specs=(pl.BlockSpec((tm, W_A), lambda i: (i, O_A // W_A)), *small),
        input_output_aliases={6: 0},
        compiler_params=_params("arbitrary"),
    )(proj, gain, ws, bs, ga, dy, dproj)


def _halo_specs(tm, width, col, n_rows):
    per = tm // HALO
    last = n_rows // HALO - 1
    return [pl.BlockSpec((HALO, width), lambda i: (jnp.maximum(i * per - 1, 0), col)),
            pl.BlockSpec((tm, width), lambda i: (i, col)),
            pl.BlockSpec((HALO, width), lambda i: (jnp.minimum((i + 1) * per, last), col))]


def _conv_masks(tm, s_len):
    r = lax.broadcasted_iota(jnp.int32, (tm + 2 * HALO, 1), 0)
    g = pl.program_id(0) * tm - HALO + r
    return (g >= 0) & (g < s_len), (r >= HALO) & (r < HALO + tm)


def _conv_inputs(b_refs, cw_ref, cb_ref, gb_ref):
    ext = jnp.concatenate([r[...] for r in b_refs], axis=0).astype(F32)
    bb, bc, bh, bz = (ext[:, j * 512:(j + 1) * 512] for j in range(4))
    prm = (cw_ref[0:1, :], cw_ref[1:2, :], cw_ref[2:3, :], cb_ref[...], gb_ref[...])
    return (bb, bc, bh, bz), prm


def _conv_forward(proj, cw, cb, gb, y, tm):
    s_len = proj.shape[0]

    def body(p0, p1, p2, cw_ref, cb_ref, gb_ref, _, o_ref):
        acts, prm = _conv_inputs((p0, p1, p2), cw_ref, cb_ref, gb_ref)
        valid, core = _conv_masks(tm, s_len)
        out = _conv_tile(*acts, *prm, *prm, valid, core)
        o_ref[...] = out[HALO:HALO + tm].astype(BF16)

    vec = pl.BlockSpec((1, 512), lambda i: (0, 0))
    return pl.pallas_call(
        body, name="conv_forward",
        out_shape=jax.ShapeDtypeStruct(y.shape, BF16),
        grid=(s_len // tm,),
        in_specs=_halo_specs(tm, W_B, O_B // W_B, s_len) + [pl.BlockSpec((3, 512), lambda i: (0, 0)), vec, vec,
                                                             pl.BlockSpec(memory_space=pl.ANY)],
        out_specs=pl.BlockSpec((tm, 512), lambda i: (i, 1)),
        input_output_aliases={6: 0},
        compiler_params=_params("parallel"),
    )(proj, proj, proj, cw, cb, gb, y)


def _conv_backward(proj, cw, cb, gb, dy, tm):
    s_len = proj.shape[0]

    def body(p0, p1, p2, cw_ref, cb_ref, gb_ref, d0, d1, d2, db_ref, dcw_ref, dcb_ref, dgb_ref):
        acts, prm = _conv_inputs((p0, p1, p2), cw_ref, cb_ref, gb_ref)
        valid, core = _conv_masks(tm, s_len)
        _, vjp = jax.vjp(lambda a, p: _conv_tile(*a, *p, *prm, valid, core), acts, prm)
        dy_ext = jnp.where(valid, jnp.concatenate([d0[...], d1[...], d2[...]], axis=0), 0.0)
        dacts, dprm = vjp(dy_ext)
        for j in range(4):
            db_ref[:, j * 512:(j + 1) * 512] = dacts[j][HALO:HALO + tm].astype(BF16)
        first = pl.program_id(0) == 0
        _acc(dcw_ref, jnp.concatenate(dprm[0:3], axis=0), first)
        _acc(dcb_ref, dprm[3], first)
        _acc(dgb_ref, dprm[4], first)

    vec = pl.BlockSpec((1, 512), lambda i: (0, 0))
    mat = pl.BlockSpec((3, 512), lambda i: (0, 0))
    return pl.pallas_call(
        body, name="conv_backward",
        out_shape=(jax.ShapeDtypeStruct((s_len, PROJ_W), BF16), jax.ShapeDtypeStruct((3, 512), F32),
                   jax.ShapeDtypeStruct((1, 512), F32), jax.ShapeDtypeStruct((1, 512), F32)),
        grid=(s_len // tm,),
        in_specs=_halo_specs(tm, W_B, O_B // W_B, s_len) + [mat, vec, vec] + _halo_specs(tm, 512, 1, s_len),
        out_specs=(pl.BlockSpec((tm, W_B), lambda i: (i, O_B // W_B)), mat, vec, vec),
        compiler_params=_params("arbitrary"),
    )(proj, proj, proj, cw, cb, gb, dy, dy, dy)


def _cgate_forward(o, proj, gc, y, tm):
    s_len = o.shape[0]

    def body(o_ref, cz_ref, gc_ref, _, y_ref):
        y_ref[...] = _cgate_tile(o_ref[...], cz_ref[...].astype(F32), gc_ref[...]).astype(BF16)

    return pl.pallas_call(
        body, name="cgate_forward",
        out_shape=jax.ShapeDtypeStruct(y.shape, BF16),
        grid=(s_len // tm,),
        in_specs=[pl.BlockSpec((tm, W_CZ), lambda i: (i, 0)), pl.BlockSpec((tm, W_CZ), lambda i: (i, O_CZ // W_CZ)),
                  pl.BlockSpec((1, W_CZ), lambda i: (0, 0)), pl.BlockSpec(memory_space=pl.ANY)],
        out_specs=pl.BlockSpec((tm, W_CZ), lambda i: (i, 1)),
        input_output_aliases={3: 0},
        compiler_params=_params("parallel"),
    )(o, proj, gc, y)


def _cgate_backward(o, proj, gc, dy, dproj, tm, stat_chunk):
    s_len = o.shape[0]
    per_stat = stat_chunk // tm

    def body(o_ref, cz_ref, gc_ref, dy_ref, _, dcz_ref, do_ref, dsum_ref, dgc_ref):
        o = o_ref[...]
        _, vjp = jax.vjp(_cgate_tile, o, cz_ref[...].astype(F32), gc_ref[...])
        do, dcz, dgc = vjp(dy_ref[...])
        dcz_ref[...] = dcz.astype(BF16)
        do_ref[...] = do.astype(BF16)
        ones = jnp.ones((8, HEAD), F32)
        for h in range(C_HEADS):
            cols = slice(h * HEAD, (h + 1) * HEAD)
            sums = lax.dot_general(ones, do[:, cols] * o[:, cols], _NT, precision=lax.Precision.HIGHEST,
                                   preferred_element_type=F32)
            dsum_ref[h, 0] = sums[0:1]
        _acc(dgc_ref, dgc, pl.program_id(0) == 0)

    row = pl.BlockSpec((tm, W_CZ), lambda i: (i, 0))
    vec = pl.BlockSpec((1, W_CZ), lambda i: (0, 0))
    return pl.pallas_call(
        body, name="cgate_backward",
        out_shape=(jax.ShapeDtypeStruct(dproj.shape, BF16), jax.ShapeDtypeStruct((s_len, W_CZ), BF16),
                   jax.ShapeDtypeStruct((C_HEADS, s_len // stat_chunk, 1, stat_chunk), F32), jax.ShapeDtypeStruct((1, W_CZ), F32)),
        grid=(s_len // tm,),
        in_specs=[row, pl.BlockSpec((tm, W_CZ), lambda i: (i, O_CZ // W_CZ)), vec,
                  pl.BlockSpec((tm, W_CZ), lambda i: (i, 1)), pl.BlockSpec(memory_space=pl.ANY)],
        out_specs=(pl.BlockSpec((tm, W_CZ), lambda i: (i, O_CZ // W_CZ)), row,
                   pl.BlockSpec((C_HEADS, 1, 1, tm), lambda i: (0, i // per_stat, 0, i % per_stat)), vec),
        input_output_aliases={4: 0},
        compiler_params=_params("arbitrary"),
    )(o, proj, gc, dy, dproj)


def _mla_small_specs():
    return [pl.BlockSpec((KV_RANK, 2 * C_HEADS * HEAD), lambda i: (0, 0)), pl.BlockSpec((1, KV_RANK), lambda i: (0, 0)),
            pl.BlockSpec((1, HEAD), lambda i: (0, 0)), pl.BlockSpec((1, ROPE), lambda i: (0, 0)),
            pl.BlockSpec((1, HEAD), lambda i: (0, 0)), pl.BlockSpec((1, ROPE), lambda i: (0, 0))]


def _mla_load(m_ref, cos_ref, sin_ref):
    qn = m_ref[:, M_QN:M_QN + C_HEADS * HEAD].astype(F32)
    qr = m_ref[:, M_QR:M_QR + C_HEADS * ROPE].astype(F32)
    ckv = m_ref[:, M_CKV:M_CKV + KV_RANK].astype(F32)
    kr = m_ref[:, M_KR:M_KR + ROPE].astype(F32)
    return qn, qr, ckv, kr, cos_ref[...], sin_ref[...]


def _mla_forward(proj, cosf, sins, wukv, kvg, qng, qrg, kng, krg, tm, kt_chunk, vt_chunk):
    s_len = proj.shape[0]

    def body(m_ref, cos_ref, sin_ref, w_ref, kvg_ref, qng_ref, qrg_ref, kng_ref, krg_ref, q_ref, k_ref, v_ref, kt_ref, vt_ref):
        q, k, v = _mla_tile(*_mla_load(m_ref, cos_ref, sin_ref), w_ref[...], kvg_ref[...], qng_ref[...], qrg_ref[...],
                            kng_ref[...], krg_ref[...])
        for h in range(C_HEADS):
            q_ref[h] = q[h].astype(BF16)
            k_ref[h] = k[h].astype(BF16)
            v_ref[h] = v[h].astype(BF16)
            kt_ref[h, 0] = jnp.concatenate([k[h][:, :HEAD].T, k[h][:, HEAD:].T], axis=0).astype(BF16)
            vt_ref[h, 0] = v[h].T.astype(BF16)

    rope_spec = pl.BlockSpec((tm, ROPE), lambda i: (i, 0))
    qk_spec = pl.BlockSpec((C_HEADS, tm, QK), lambda i: (0, i, 0))
    per_k, per_v = kt_chunk // tm, vt_chunk // tm
    return pl.pallas_call(
        body, name="mla_forward",
        out_shape=(jax.ShapeDtypeStruct((C_HEADS, s_len, QK), BF16), jax.ShapeDtypeStruct((C_HEADS, s_len, QK), BF16),
                   jax.ShapeDtypeStruct((C_HEADS, s_len, HEAD), BF16),
                   jax.ShapeDtypeStruct((C_HEADS, s_len // kt_chunk, QK, kt_chunk), BF16),
                   jax.ShapeDtypeStruct((C_HEADS, s_len // vt_chunk, HEAD, vt_chunk), BF16)),
        grid=(s_len // tm,),
        in_specs=[pl.BlockSpec((tm, W_M), lambda i: (i, O_M // W_M)), rope_spec, rope_spec] + _mla_small_specs(),
        out_specs=(qk_spec, qk_spec, pl.BlockSpec((C_HEADS, tm, HEAD), lambda i: (0, i, 0)),
                   pl.BlockSpec((C_HEADS, 1, QK, tm), lambda i: (0, i // per_k, 0, i % per_k)),
                   pl.BlockSpec((C_HEADS, 1, HEAD, tm), lambda i: (0, i // per_v, 0, i % per_v))),
        compiler_params=_params("parallel"),
    )(proj, cosf, sins, wukv, kvg, qng, qrg, kng, krg)


def _mla_backward(proj, cosf, sins, wukv, kvg, qng, qrg, kng, krg, dqt, dk, dv, dproj, tm):
    s_len = proj.shape[0]
    per_chunk = dqt.shape[3] // tm

    def body(m_ref, cos_ref, sin_ref, w_ref, kvg_ref, qng_ref, qrg_ref, kng_ref, krg_ref, dq_ref, dk_ref, dv_ref, _,
             dm_ref, dw_ref, dkvg_ref, dqng_ref, dqrg_ref, dkng_ref, dkrg_ref):
        qn, qr, ckv, kr, cosf_t, sins_t = _mla_load(m_ref, cos_ref, sin_ref)
        prm = (w_ref[...], kvg_ref[...], qng_ref[...], qrg_ref[...], kng_ref[...], krg_ref[...])
        _, vjp = jax.vjp(lambda a, p: _mla_tile(*a, cosf_t, sins_t, *p), (qn, qr, ckv, kr), prm)
        heads = range(C_HEADS)
        dacts, dprm = vjp((tuple(dq_ref[h, 0].T for h in heads), tuple(dk_ref[h] for h in heads), tuple(dv_ref[h] for h in heads)))
        dm_ref[:, M_QN:M_QN + C_HEADS * HEAD] = dacts[0].astype(BF16)
        dm_ref[:, M_QR:M_QR + C_HEADS * ROPE] = dacts[1].astype(BF16)
        dm_ref[:, M_CKV:M_CKV + KV_RANK] = dacts[2].astype(BF16)
        pad = jnp.zeros((tm, W_M - M_KR - ROPE), F32)
        dm_ref[:, M_KR:W_M] = jnp.concatenate([dacts[3], pad], axis=-1).astype(BF16)
        first = pl.program_id(0) == 0
        for ref, val in zip((dw_ref, dkvg_ref, dqng_ref, dqrg_ref, dkng_ref, dkrg_ref), dprm):
            _acc(ref, val.astype(F32), first)

    rope_spec = pl.BlockSpec((tm, ROPE), lambda i: (i, 0))
    qk_spec = pl.BlockSpec((C_HEADS, tm, QK), lambda i: (0, i, 0))
    small = _mla_small_specs()
    return pl.pallas_call(
        body, name="mla_backward",
        out_shape=(jax.ShapeDtypeStruct(dproj.shape, BF16), jax.ShapeDtypeStruct((KV_RANK, 2 * C_HEADS * HEAD), F32),
                   jax.ShapeDtypeStruct((1, KV_RANK), F32), jax.ShapeDtypeStruct((1, HEAD), F32),
                   jax.ShapeDtypeStruct((1, ROPE), F32), jax.ShapeDtypeStruct((1, HEAD), F32),
                   jax.ShapeDtypeStruct((1, ROPE), F32)),
        grid=(s_len // tm,),
        in_specs=[pl.BlockSpec((tm, W_M), lambda i: (i, O_M // W_M)), rope_spec, rope_spec] + small
                 + [pl.BlockSpec((C_HEADS, 1, QK, tm), lambda i: (0, i // per_chunk, 0, i % per_chunk)), qk_spec,
                    pl.BlockSpec((C_HEADS, tm, HEAD), lambda i: (0, i, 0)), pl.BlockSpec(memory_space=pl.ANY)],
        out_specs=(pl.BlockSpec((tm, W_M), lambda i: (i, O_M // W_M)), *small),
        input_output_aliases={12: 0},
        compiler_params=_params("arbitrary"),
    )(proj, cosf, sins, wukv, kvg, qng, qrg, kng, krg, dqt, dk, dv, dproj)


def _attention_forward(q, k, vt, tq, stat_chunk, comm=None):
    n_heads, s_len, _ = q.shape
    n_chunks, _, ck = vt.shape[1:]
    c_ops, c_in_specs, c_shapes, c_out_specs, c_sems, c_begin, c_end = _riding_exchange(comm, 2)
    n_c = len(c_ops)

    def body(q_ref, k_ref, vt_ref, *rest):
        c_ins, (o_ref, lse_ref), c_outs, sems = rest[:n_c], rest[n_c:n_c + 2], rest[n_c + 2:2 * n_c + 2], rest[2 * n_c + 2:]
        c_begin(c_ins, c_outs, sems)
        q_t = q_ref[0]

        def step(j, carry):
            m_old, l_old, acc = carry
            k_j = k_ref[0, pl.ds(pl.multiple_of(j * ck, ck), ck), :]
            s = lax.dot_general(k_j, q_t, _NT, preferred_element_type=F32)
            m_new = jnp.maximum(m_old, jnp.max(s, axis=0, keepdims=True))
            p = jnp.exp2(s - m_new)
            alpha = jnp.exp2(m_old - m_new)
            l_new = alpha * l_old + jnp.sum(p, axis=0, keepdims=True)
            acc = alpha * acc + jnp.dot(vt_ref[0, j], p.astype(BF16), preferred_element_type=F32)
            return m_new, l_new, acc

        init = (jnp.full((1, tq), -jnp.inf, F32), jnp.zeros((1, tq), F32), jnp.zeros((HEAD, tq), F32))
        m_fin, l_fin, acc = lax.fori_loop(0, n_chunks, step, init)
        o_ref[...] = (acc / l_fin).T
        lse_ref[0, 0] = m_fin + jnp.log2(l_fin)
        c_end(c_ins, c_outs, sems)

    per_stat = stat_chunk // tq
    outs = pl.pallas_call(
        body, name="attention_forward",
        out_shape=(jax.ShapeDtypeStruct((s_len, n_heads * HEAD), F32),
                   jax.ShapeDtypeStruct((n_heads, s_len // stat_chunk, 1, stat_chunk), F32), *c_shapes),
        grid=(n_heads, s_len // tq),
        in_specs=[pl.BlockSpec((1, tq, QK), lambda h, i: (h, i, 0)), pl.BlockSpec((1, s_len, QK), lambda h, i: (h, 0, 0)),
                  pl.BlockSpec((1, n_chunks, HEAD, ck), lambda h, i: (h, 0, 0, 0))] + c_in_specs,
        out_specs=(pl.BlockSpec((tq, HEAD), lambda h, i: (i, h)),
                   pl.BlockSpec((1, 1, 1, tq), lambda h, i: (h, i // per_stat, 0, i % per_stat)), *c_out_specs),
        scratch_shapes=c_sems,
        compiler_params=_params("arbitrary", "arbitrary") if comm else _params("parallel", "parallel"),
    )(q, k, vt, *c_ops)
    return outs[0], outs[1], outs[2:]


def _attention_backward(q, k, kt, v, do, lse, dsum, comm=None):
    n_heads, s_len, _ = q.shape
    tk = kt.shape[3]
    n_q, _, cq = lse.shape[1:]
    c_ops, c_in_specs, c_shapes, c_out_specs, c_sems, c_begin, c_end = _riding_exchange(comm, 2)
    n_c = len(c_ops)

    def body(q_ref, k_ref, kt_ref, v_ref, do_ref, lse_ref, dsum_ref, *rest):
        c_ins, (dqt_ref, dk_ref, dv_ref), c_outs, sems = rest[:n_c], rest[n_c:n_c + 3], rest[n_c + 3:2 * n_c + 3], rest[2 * n_c + 3:]
        c_begin(c_ins, c_outs, sems)
        first = pl.program_id(1) == 0
        k_j, kt_j, v_j = k_ref[0], kt_ref[0, 0], v_ref[0]

        def step(i, carry):
            dk, dv = carry
            rows = pl.ds(pl.multiple_of(i * cq, cq), cq)
            q_i, do_i = q_ref[0, rows, :], do_ref[rows, :]
            s = lax.dot_general(k_j, q_i, _NT, preferred_element_type=F32)
            p = jnp.exp2(s - lse_ref[0, i])
            dp = lax.dot_general(v_j, do_i, _NT, preferred_element_type=F32)
            ds = (p * (dp - dsum_ref[0, i]) * LN_2).astype(BF16)
            dv = dv + jnp.dot(p.astype(BF16), do_i, preferred_element_type=F32)
            dk = dk + jnp.dot(ds, q_i, preferred_element_type=F32)
            _acc(dqt_ref.at[0, i], jnp.dot(kt_j, ds, preferred_element_type=F32), first)
            return dk, dv

        dk, dv = lax.fori_loop(0, n_q, step, (jnp.zeros((tk, QK), F32), jnp.zeros((tk, HEAD), F32)))
        dk_ref[0] = dk
        dv_ref[0] = dv
        c_end(c_ins, c_outs, sems)

    stat = pl.BlockSpec((1, n_q, 1, cq), lambda h, j: (h, 0, 0, 0))
    outs = pl.pallas_call(
        body, name="attention_backward",
        out_shape=(jax.ShapeDtypeStruct((n_heads, n_q, QK, cq), F32), jax.ShapeDtypeStruct((n_heads, s_len, QK), F32),
                   jax.ShapeDtypeStruct((n_heads, s_len, HEAD), F32), *c_shapes),
        grid=(n_heads, s_len // tk),
        in_specs=[pl.BlockSpec((1, s_len, QK), lambda h, j: (h, 0, 0)), pl.BlockSpec((1, tk, QK), lambda h, j: (h, j, 0)),
                  pl.BlockSpec((1, 1, QK, tk), lambda h, j: (h, j, 0, 0)),
                  pl.BlockSpec((1, tk, HEAD), lambda h, j: (h, j, 0)), pl.BlockSpec((s_len, HEAD), lambda h, j: (0, h)),
                  stat, stat] + c_in_specs,
        out_specs=(pl.BlockSpec((1, n_q, QK, cq), lambda h, j: (h, 0, 0, 0)), pl.BlockSpec((1, tk, QK), lambda h, j: (h, j, 0)),
                   pl.BlockSpec((1, tk, HEAD), lambda h, j: (h, j, 0)), *c_out_specs),
        scratch_shapes=c_sems,
        compiler_params=_params("arbitrary", "arbitrary") if comm else _params("parallel", "arbitrary"),
    )(q, k, kt, v, do, lse, dsum, *c_ops)
    return outs[0], outs[1], outs[2], outs[3:]


def _exchange(arrs, gather, name):
    n = len(arrs)

    def body(*refs):
        plan = _exchange_plan(refs[:n], refs[n:2 * n], gather, *refs[2 * n:])
        _exchange_start(plan)
        _exchange_wait(plan)

    any_spec = pl.BlockSpec(memory_space=pl.ANY)
    return pl.pallas_call(
        body, name=name,
        out_shape=_exchange_out_shapes(arrs, gather),
        in_specs=[any_spec] * n,
        out_specs=tuple([any_spec] * n),
        scratch_shapes=_exchange_semaphores(n),
        compiler_params=pltpu.CompilerParams(has_side_effects=True),
    )(*arrs)


def _gather_via_sibling(block, name):
    def body(x_ref, out_ref, *sems):
        plan = _sibling_plan([x_ref], [out_ref], *sems)
        _sibling_start(plan)
        _sibling_forward(plan)
        _sibling_finish(plan)

    any_spec = pl.BlockSpec(memory_space=pl.ANY)
    return pl.pallas_call(
        body, name=name,
        out_shape=jax.ShapeDtypeStruct((N_DEV, *block.shape), block.dtype),
        in_specs=[any_spec], out_specs=any_spec,
        scratch_shapes=_exchange_semaphores(1),
        compiler_params=pltpu.CompilerParams(has_side_effects=True),
    )(block)


def _sibling_plan(ins, outs, send_sems, recv_sems, local_sems):
    x, y, c = lax.axis_index("x"), lax.axis_index("y"), lax.axis_index("c")
    me, sibling = (x, y, c), (x, y, 1 - c)
    chips = [(1 - x, y), (x, 1 - y), (1 - x, 1 - y)]
    local, first, ici_arrivals, passes, sibling_arrivals = [], [], [], [], []
    for a, (x_ref, out_ref) in enumerate(zip(ins, outs)):
        def slot(px, py, pc, out_ref=out_ref):
            return out_ref.at[4 * px + 2 * py + pc]

        def copy(k, block_of, to, src=None, a=a, slot=slot):
            idx = a * (N_DEV - 1) + k
            return pltpu.make_async_remote_copy(
                src_ref=slot(*block_of) if src is None else src, dst_ref=slot(*block_of), send_sem=send_sems.at[idx],
                recv_sem=recv_sems.at[idx], device_id=to, device_id_type=pl.DeviceIdType.MESH)

        local.append(pltpu.make_async_copy(x_ref, slot(*me), local_sems.at[a]))
        first += [copy(0, me, sibling, src=x_ref)] + [copy(1 + j, me, (*chip, c), src=x_ref) for j, chip in enumerate(chips)]
        ici_arrivals += [copy(1 + j, (*chip, c), me) for j, chip in enumerate(chips)]
        passes += [copy(4 + j, (*chip, c), sibling) for j, chip in enumerate(chips)]
        sibling_arrivals += [copy(0, sibling, me)] + [copy(4 + j, (*chip, 1 - c), me) for j, chip in enumerate(chips)]
    return local, first, ici_arrivals, passes, sibling_arrivals


def _sibling_start(plan):
    local, first = plan[0], plan[1]
    for cp in local + first:
        cp.start()


def _sibling_forward(plan):
    for arrival, onward in zip(plan[2], plan[3]):
        arrival.wait_recv()
        onward.start()


def _sibling_finish(plan):
    local, first, _, passes, sibling_arrivals = plan
    for arrival in sibling_arrivals:
        arrival.wait_recv()
    for cp in first + passes:
        cp.wait_send()
    for cp in local:
        cp.wait()


def _exchange_out_shapes(arrs, gather):
    return tuple(jax.ShapeDtypeStruct((N_DEV, *(a.shape if g else a.shape[1:])), a.dtype) for a, g in zip(arrs, gather))


def _exchange_semaphores(n):
    n_remote = n * (N_DEV - 1)
    return [pltpu.SemaphoreType.DMA((n_remote,)), pltpu.SemaphoreType.DMA((n_remote,)), pltpu.SemaphoreType.DMA((n,))]


def _exchange_plan(ins, outs, gather, send_sems, recv_sems, local_sems):
    n = len(ins)
    x, y, c = lax.axis_index("x"), lax.axis_index("y"), lax.axis_index("c")
    me = 4 * x + 2 * y + c

    def block_for(a, dev):
        return ins[a] if gather[a] else ins[a].at[dev]

    local = [pltpu.make_async_copy(block_for(a, me), outs[a].at[me], local_sems.at[a]) for a in range(n)]
    remote = []
    for k in range(1, N_DEV):
        px = 1 - x if k & 4 else x
        py = 1 - y if k & 2 else y
        pc = 1 - c if k & 1 else c
        peer = 4 * px + 2 * py + pc
        for a in range(n):
            idx = a * (N_DEV - 1) + k - 1
            send = pltpu.make_async_remote_copy(
                src_ref=block_for(a, peer), dst_ref=outs[a].at[me], send_sem=send_sems.at[idx], recv_sem=recv_sems.at[idx],
                device_id=(px, py, pc), device_id_type=pl.DeviceIdType.MESH)
            arrive = pltpu.make_async_remote_copy(
                src_ref=block_for(a, peer), dst_ref=outs[a].at[peer], send_sem=send_sems.at[idx], recv_sem=recv_sems.at[idx],
                device_id=(px, py, pc), device_id_type=pl.DeviceIdType.MESH)
            remote.append((send, arrive))
    return local, remote


def _exchange_start(plan):
    local, remote = plan
    for cp in local:
        cp.start()
    for send, _ in remote:
        send.start()


def _exchange_wait(plan):
    local, remote = plan
    for send, arrive in remote:
        send.wait_send()
        arrive.wait_recv()
    for cp in local:
        cp.wait()


def _riding_exchange(comm, n_grid):
    if comm is None:
        return [], [], (), (), [], lambda *_: None, lambda *_: None
    arrs, gather, *via_sibling = comm
    via_sibling = bool(via_sibling and via_sibling[0])
    n = len(arrs)
    any_spec = pl.BlockSpec(memory_space=pl.ANY)

    def at_step(step_of_first_axis):
        rest = [pl.program_id(d) == (0 if step_of_first_axis != "last" else pl.num_programs(d) - 1) for d in range(1, n_grid)]
        lead = {"first": 0, "middle": pl.num_programs(0) // 2, "last": pl.num_programs(0) - 1}[step_of_first_axis]
        return functools.reduce(jnp.logical_and, [pl.program_id(0) == lead] + rest)

    def begin(ins, outs, sems):
        @pl.when(at_step("first"))
        def _():
            if via_sibling:
                _sibling_start(_sibling_plan(ins, outs, *sems))
            else:
                _exchange_start(_exchange_plan(ins, outs, gather, *sems))

        if via_sibling:
            @pl.when(at_step("middle"))
            def _():
                _sibling_forward(_sibling_plan(ins, outs, *sems))

    def end(ins, outs, sems):
        @pl.when(at_step("last"))
        def _():
            if via_sibling:
                _sibling_finish(_sibling_plan(ins, outs, *sems))
            else:
                _exchange_wait(_exchange_plan(ins, outs, gather, *sems))

    return (list(arrs), [any_spec] * n, _exchange_out_shapes(arrs, gather), tuple([any_spec] * n), _exchange_semaphores(n),
            begin, end)


ADAM_TILE_ELEMS = 256 * 1024


def _sum_adam(parts, w, m, v, layer, prev, name):
    n_parts, r, c = parts.shape
    tm, tc = r, c
    if r % 16 == 0:
        while tm * c > ADAM_TILE_ELEMS and tm % 16 == 0:
            tm //= 2
    else:
        while r * tc > ADAM_TILE_ELEMS and tc % 256 == 0:
            tc //= 2

    def body(p_ref, w_ref, m_ref, v_ref, *rest):
        g_ref, d_ref, nm_ref, nv_ref = rest[-4:]
        g = p_ref[0].astype(F32)
        for s in range(1, n_parts):
            g = g + p_ref[s].astype(F32)
        m_new = ADAM_B1 * m_ref[...] + (1.0 - ADAM_B1) * g
        v_new = ADAM_B2 * v_ref[...] + (1.0 - ADAM_B2) * (g * g)
        m_hat = m_new / (1.0 - ADAM_B1 ** ADAM_STEP)
        v_hat = v_new / (1.0 - ADAM_B2 ** ADAM_STEP)
        g_ref[...] = g
        d_ref[...] = -ADAM_LR * (m_hat / (jnp.sqrt(v_hat) + ADAM_EPS) + ADAM_WD * w_ref[...])
        nm_ref[...] = m_new
        nv_ref[...] = v_new

    slab = pl.BlockSpec((None, tm, tc), lambda i, j: (layer, i, j))
    n_prev = 0 if prev is None else 4
    return pl.pallas_call(
        body, name=name,
        out_shape=(jax.ShapeDtypeStruct(w.shape, F32),) * 4,
        grid=(r // tm, c // tc),
        in_specs=[pl.BlockSpec((n_parts, tm, tc), lambda i, j: (0, i, j)), slab, slab, slab]
                 + [pl.BlockSpec(memory_space=pl.ANY)] * n_prev,
        out_specs=(slab, slab, slab, slab),
        input_output_aliases={4 + j: j for j in range(n_prev)},
        compiler_params=_params("parallel", "parallel"),
    )(parts, w, m, v, *(prev or ()))


def _permute_in(w):
    k = w.shape[0]
    q = w[:, 3584:5120].reshape(k, C_HEADS, QK)
    return jnp.concatenate(
        [w[:, 1536:3584], w[:, 5696:6720], w[:, 0:1536], q[:, :, :HEAD].reshape(k, C_HEADS * HEAD),
         q[:, :, HEAD:].reshape(k, C_HEADS * ROPE), w[:, 5120:5632], w[:, 5632:5696],
         jnp.zeros((k, PROJ_W - IN_WIDTH), w.dtype)], axis=1)


def _unpermute_in(g):
    k = g.shape[0]
    qn = g[:, O_M + M_QN:O_M + M_QR].reshape(k, C_HEADS, HEAD)
    qr = g[:, O_M + M_QR:O_M + M_CKV].reshape(k, C_HEADS, ROPE)
    q = jnp.concatenate([qn, qr], axis=-1).reshape(k, C_HEADS * QK)
    return jnp.concatenate(
        [g[:, O_A:O_A + W_A], g[:, O_B:O_B + W_B], q, g[:, O_M + M_CKV:O_M + M_KR],
         g[:, O_M + M_KR:O_M + M_KR + ROPE], g[:, O_CZ:O_CZ + W_CZ]], axis=1)


SMALL = ("attn_norm", "sgu_norm", "w_spatial", "b_spatial", "conv_b", "kv_norm", "q_nope_norm", "q_rope_norm",
         "k_nope_norm", "k_rope_norm", "out_norm", "ple_norm")
PACK_ROWS = 256


def _pack(tensors):
    flat = jnp.concatenate([t.reshape(-1) for t in tensors])
    rows = -(-flat.shape[0] // (128 * PACK_ROWS)) * PACK_ROWS
    return jnp.pad(flat, (0, rows * 128 - flat.shape[0])).reshape(rows, 128)


def _unpack(packed, like):
    flat = packed.reshape(-1)
    out, pos = [], 0
    for t in like:
        out.append(flat[pos:pos + t.size].reshape(t.shape))
        pos += t.size
    return out


def _tile(s_len, want):
    return min(want, s_len)


ATT_FWD_QUERIES = 512
ATT_FWD_KEYS = 8192
ATT_BWD_KEYS = 1024
ATT_BWD_QUERIES = 4096


def _layer_forward(h, p_l, cosf, sins, w, sm, comm, comm_rest):
    s_len = h.shape[0]
    tm = _tile(s_len, 512)
    proj, hn, rest = _norm_matmul(h, sm["attn_norm"], w["w_in"], _tile(s_len, 1024), 768, comm_rest)
    if comm_rest is not None:
        w = {**w, **_assemble_rest(rest)}
    ga, gb, gc = sm["out_norm"][:, 0:512], sm["out_norm"][:, 512:1024], sm["out_norm"][:, 1024:2048]
    y = _sgu_forward(proj, sm["sgu_norm"], sm["w_spatial"], sm["b_spatial"], ga, _tile(s_len, 256))
    y = _conv_forward(proj, w["conv_w"], sm["conv_b"], gb, y, _tile(s_len, 256))
    q, k, v, kt, vt = _mla_forward(proj, cosf, sins, w["w_ukv"], sm["kv_norm"], sm["q_nope_norm"], sm["q_rope_norm"],
                                   sm["k_nope_norm"], sm["k_rope_norm"], tm, _tile(s_len, ATT_BWD_KEYS),
                                   _tile(s_len, ATT_FWD_KEYS))
    o, lse, arrived = _attention_forward(q, k, vt, _tile(s_len, ATT_FWD_QUERIES), _tile(s_len, ATT_BWD_QUERIES), comm)
    y = _cgate_forward(o, proj, gc, y, _tile(s_len, 256))
    h1 = _out_matmul(h, y, w["w_out"], _tile(s_len, 1024), 1024)
    h2, n1, gate, pp = _ple_forward(h1, sm["ple_norm"], p_l, w["w_ple_gate"], w["w_ple_proj"], tm, D_MODEL)
    saved = dict(h=h, hn=hn, proj=proj, y=y, q=q, k=k, v=v, kt=kt, o=o, lse=lse, h1=h1, n1=n1, gate=gate, pp=pp)
    return h2, saved, w, arrived


def _layer_backward(dh2, p_l, cosf, sins, w, sm, sv, comm, scatter_own):
    s_len = dh2.shape[0]
    tm = _tile(s_len, 512)
    tr = _tile(s_len, 256)
    big, small = {}, {}
    dh1, small["ple_norm"], dgp, dpp = _ple_backward(dh2, sv["gate"], sv["pp"], w["w_ple_gate"], sv["h1"], sm["ple_norm"], tm)
    big["w_ple_proj"], _ = _matmul_tn(p_l, dpp, _tile(s_len, 2048), PLE_DIM, 1024, "grad_w_ple_proj")
    big["w_ple_gate"], _ = _matmul_tn(sv["n1"], dgp, _tile(s_len, 2048), 1024, 1024, "grad_w_ple_gate")
    dy, _ = _matmul_nt(dh1, w["w_out"], _tile(s_len, 1024), 1024, "grad_branches")
    big["w_out"], _ = _matmul_tn(sv["y"], dh1, _tile(s_len, 2048), 1024, 1024, "grad_w_out")
    ga, gb, gc = sm["out_norm"][:, 0:512], sm["out_norm"][:, 512:1024], sm["out_norm"][:, 1024:2048]
    dproj, dcw, small["conv_b"], dgb = _conv_backward(sv["proj"], w["conv_w"], sm["conv_b"], gb, dy, tr)
    big["conv_w"] = dcw
    dproj, do, dsum, dgc = _cgate_backward(sv["o"], sv["proj"], gc, dy, dproj, tm, _tile(s_len, ATT_BWD_QUERIES))
    dqt, dk, dv, arrived = _attention_backward(sv["q"], sv["k"], sv["kt"], sv["v"], do, sv["lse"], dsum, comm)
    (dproj, big["w_ukv"], small["kv_norm"], small["q_nope_norm"], small["q_rope_norm"], small["k_nope_norm"],
     small["k_rope_norm"]) = _mla_backward(sv["proj"], cosf, sins, w["w_ukv"], sm["kv_norm"], sm["q_nope_norm"],
                                            sm["q_rope_norm"], sm["k_nope_norm"], sm["k_rope_norm"], dqt, dk, dv, dproj, tr)
    dproj, small["sgu_norm"], small["w_spatial"], small["b_spatial"], dga = _sgu_backward(
        sv["proj"], sm["sgu_norm"], sm["w_spatial"], sm["b_spatial"], ga, dy, dproj, tm)
    small["out_norm"] = jnp.concatenate([dga, dgb, dgc], axis=1)
    parts_rest = _parts_rest(big)
    g_in, arrived_rest = _matmul_tn(sv["hn"], dproj, _tile(s_len, 2048), 512, 2304, "grad_w_in",
                                    (parts_rest, [False] * len(parts_rest)) if scatter_own else None)
    parts = [_part_w_in(g_in)] + parts_rest
    dhn, arrived_in = _matmul_nt(dproj, w["w_in"], _tile(s_len, 1024), 256, "grad_attn_norm_in",
                                 (parts[:1], [False]) if scatter_own else None)
    dh, small["attn_norm"] = _rms_backward(sv["h"], sm["attn_norm"], dhn, dh1, tr, "attn_norm_backward")
    return dh, parts, big["conv_w"], small, arrived, (*arrived_in, *arrived_rest) if scatter_own else None


def _layer_small(params, layer):
    return dict(
        attn_norm=params["attn_norm"][layer][None, :], sgu_norm=params["sgu_norm"][layer],
        w_spatial=params["w_spatial"][layer], b_spatial=params["b_spatial"][layer][:, :, None],
        conv_b=params["conv_b"][layer][None, :], kv_norm=params["kv_norm"][layer][None, :],
        q_nope_norm=params["q_nope_norm"][layer][None, :], q_rope_norm=params["q_rope_norm"][layer][None, :],
        k_nope_norm=params["k_nope_norm"][layer][None, :], k_rope_norm=params["k_rope_norm"][layer][None, :],
        out_norm=params["out_norm"][layer][None, :], ple_norm=params["ple_norm"][layer][None, :])


BIG = ("w_in", "w_ukv", "w_out", "w_ple_gate", "w_ple_proj")


def _assemble_w_in(g_in):
    return _permute_in(g_in.transpose(1, 0, 2).reshape(g_in.shape[1], IN_WIDTH))


def _assemble_rest(gathered):
    g_ukv, g_out, g_gate, g_proj = gathered
    w_ukv = g_ukv.reshape(N_DEV, KV_RANK, 2, HEAD).transpose(1, 2, 0, 3).reshape(KV_RANK, 2 * C_HEADS * HEAD)
    return dict(w_ukv=w_ukv, w_out=g_out.reshape(D_MODEL, D_MODEL), w_ple_gate=g_gate.reshape(D_MODEL, D_MODEL),
                w_ple_proj=g_proj.transpose(1, 0, 2).reshape(PLE_DIM, D_MODEL))


def _part_w_in(g_in):
    return _unpermute_in(g_in).reshape(g_in.shape[0], N_DEV, -1).transpose(1, 2, 0).astype(BF16)


def _parts_rest(big):
    return [
        big["w_ukv"].reshape(KV_RANK, 2, N_DEV, HEAD).transpose(2, 0, 1, 3).reshape(N_DEV, KV_RANK, 2 * HEAD).astype(BF16),
        big["w_out"].reshape(N_DEV, -1, D_MODEL).astype(BF16),
        big["w_ple_gate"].reshape(N_DEV, -1, D_MODEL).astype(BF16),
        big["w_ple_proj"].reshape(PLE_DIM, N_DEV, -1).transpose(1, 0, 2).astype(BF16)]


def _step_local(xs, ps, pos, target, shards, conv_w, params):
    inv = 1.0 / (ROPE_BASE ** (jnp.arange(0, ROPE, 2, dtype=F32) / ROPE))
    ang = pos.astype(F32)[:, None] * inv
    cos, sin = jnp.cos(ang), jnp.sin(ang)
    cosf = jnp.concatenate([cos, cos], axis=-1)
    sins = jnp.concatenate([-sin, sin], axis=-1)

    def gather_of(names, layer):
        return [shards[n][layer] for n in names], [True] * len(names)

    h = xs
    saved, weights = [], []
    smalls = [_layer_small(params, layer) for layer in range(DEPTH)]
    first_w_in = _gather_via_sibling(shards["w_in"][0], "gather_first_w_in")
    w = dict(w_in=_assemble_w_in(first_w_in), conv_w=conv_w[0])
    for layer in range(DEPTH):
        comm = (*gather_of(BIG, layer + 1), True) if layer + 1 < DEPTH else None
        comm_rest = gather_of(BIG[1:], 0) if layer == 0 else None
        h, sv, w, arrived = _layer_forward(h, ps[layer], cosf, sins, w, smalls[layer], comm, comm_rest)
        saved.append(sv)
        weights.append(w)
        if comm is not None:
            w = dict(w_in=_assemble_w_in(arrived[0]), conv_w=conv_w[layer + 1], **_assemble_rest(arrived[1:]))
    dh, loss = _loss_grad(h, target, _tile(h.shape[0], 512))
    received, conv_grads, small_grads = [None] * DEPTH, [None] * DEPTH, [None] * DEPTH
    comm = None
    for layer in reversed(range(DEPTH)):
        dh, parts, conv_grads[layer], small_grads[layer], arrived, arrived_own = _layer_backward(
            dh, ps[layer], cosf, sins, weights[layer], smalls[layer], saved[layer], comm, layer == 0)
        if comm is not None:
            received[layer + 1] = arrived
        comm = (parts, [False] * len(parts))
    received[0] = arrived_own
    return loss, dh, received, conv_grads, small_grads


def kernel(x, p, positions, attn_norm, w_in, sgu_norm, w_spatial, b_spatial, conv_w, conv_b, kv_norm, w_ukv, q_nope_norm, q_rope_norm, k_nope_norm, k_rope_norm, out_norm, w_out, ple_norm, w_ple_gate, w_ple_proj, loss_target, m_attn_norm, m_w_in, m_sgu_norm, m_w_spatial, m_b_spatial, m_conv_w, m_conv_b, m_kv_norm, m_w_ukv, m_q_nope_norm, m_q_rope_norm, m_k_nope_norm, m_k_rope_norm, m_out_norm, m_w_out, m_ple_norm, m_w_ple_gate, m_w_ple_proj, v_attn_norm, v_w_in, v_sgu_norm, v_w_spatial, v_b_spatial, v_conv_w, v_conv_b, v_kv_norm, v_w_ukv, v_q_nope_norm, v_q_rope_norm, v_k_nope_norm, v_k_rope_norm, v_out_norm, v_w_out, v_ple_norm, v_w_ple_gate, v_w_ple_proj):
    order = ("attn_norm", "w_in", "sgu_norm", "w_spatial", "b_spatial", "conv_w", "conv_b", "kv_norm", "w_ukv",
             "q_nope_norm", "q_rope_norm", "k_nope_norm", "k_rope_norm", "out_norm", "w_out", "ple_norm", "w_ple_gate",
             "w_ple_proj")
    wts = dict(zip(order, (attn_norm, w_in, sgu_norm, w_spatial, b_spatial, conv_w, conv_b, kv_norm, w_ukv, q_nope_norm,
                           q_rope_norm, k_nope_norm, k_rope_norm, out_norm, w_out, ple_norm, w_ple_gate, w_ple_proj)))
    mom = dict(zip(order, (m_attn_norm, m_w_in, m_sgu_norm, m_w_spatial, m_b_spatial, m_conv_w, m_conv_b, m_kv_norm, m_w_ukv,
                           m_q_nope_norm, m_q_rope_norm, m_k_nope_norm, m_k_rope_norm, m_out_norm, m_w_out, m_ple_norm,
                           m_w_ple_gate, m_w_ple_proj)))
    var = dict(zip(order, (v_attn_norm, v_w_in, v_sgu_norm, v_w_spatial, v_b_spatial, v_conv_w, v_conv_b, v_kv_norm, v_w_ukv,
                           v_q_nope_norm, v_q_rope_norm, v_k_nope_norm, v_k_rope_norm, v_out_norm, v_w_out, v_ple_norm,
                           v_w_ple_gate, v_w_ple_proj)))

    conv_shard = wts["conv_w"]
    (conv_all,) = _exchange([conv_shard.reshape(-1, 128)], [True], "gather_conv_w")
    conv_full = conv_all.reshape(N_DEV, DEPTH, 3, -1).transpose(1, 2, 0, 3).reshape(DEPTH, 3, -1)
    shards = {n: wts[n].astype(BF16) for n in BIG}
    loss_part, grad_x, received, conv_grads, small_grads = _step_local(
        x[0], p[:, 0], positions[0], loss_target[0], shards, conv_full, wts)
    loss = lax.psum(loss_part[0, 0], ("x", "y", "c"))

    def small_grad(name):
        g = jnp.stack([sg[name] for sg in small_grads])
        return g.reshape(wts[name].shape)

    conv_grad = jnp.stack(conv_grads)
    like = [wts[n] for n in SMALL] + [conv_grad]
    packed = _pack([small_grad(n) for n in SMALL] + [conv_grad])
    small_parts = _gather_via_sibling(packed, "gather_small_grads")
    filler = [jnp.zeros_like(conv_grad), jnp.zeros_like(conv_grad), jnp.ones_like(conv_grad)]
    small_out = _sum_adam(small_parts, *(_pack([src[n] for n in SMALL] + [fill])[None] for src, fill in zip((wts, mom, var), filler)),
                          0, None, "adam_small")
    unpacked = [_unpack(o[0], like) for o in small_out]
    results = {n: vals for n, vals in zip(SMALL, zip(*[u[:-1] for u in unpacked]))}
    me = 4 * lax.axis_index("x") + 2 * lax.axis_index("y") + lax.axis_index("c")
    width = conv_shard.shape[2]
    conv_local = lax.dynamic_slice_in_dim(unpacked[0][-1], me * width, width, axis=2)
    as_slab = (lambda t: t.reshape(1, -1, width))
    conv_out = _sum_adam(as_slab(conv_local), as_slab(conv_shard), as_slab(mom["conv_w"]), as_slab(var["conv_w"]), 0, None,
                         "adam_conv_w")
    results["conv_w"] = tuple(o.reshape(conv_shard.shape) for o in conv_out)

    for j, name in enumerate(BIG):
        view = (lambda t: jnp.swapaxes(t, 1, 2)) if name == "w_in" else (lambda t: t)
        outs = None
        for layer in range(DEPTH):
            outs = _sum_adam(received[layer][j], view(wts[name]), view(mom[name]), view(var[name]), layer, outs, "adam_" + name)
        results[name] = tuple(view(o) for o in outs)

    grads, deltas, new_m, new_v = ([results[n][j] for n in order] for j in range(4))
    return (loss, grad_x[None], *grads, *deltas, *new_m, *new_v)
```

```python
import functools

import jax
import jax.numpy as jnp
from jax import lax
from jax.experimental import pallas as pl
from jax.experimental.pallas import tpu as pltpu

F32 = jnp.float32
BF16 = jnp.bfloat16

N_DEV = 8
DEPTH = 4
D_MODEL = 2048
EPS = 1e-6
CHUNK = 128
A_HEADS = 4
HEAD = 128
ROPE = 64
HALF = ROPE // 2
C_HEADS = 8
KV_RANK = 512
PLE_DIM = 256
ROPE_BASE = 10000.0
IN_WIDTH = 6720
QK = HEAD + ROPE
SCALE = QK ** -0.5
LOG2_E = 1.4426950408889634
LN_2 = 0.6931471805599453
Q_SCALE = SCALE * LOG2_E
HALO = 8

O_B = 0
W_B = 2048
O_CZ = 2048
W_CZ = 1024
O_A = 3072
W_A = 1536
O_M = 4608
W_M = 2304
M_QN, M_QR, M_CKV, M_KR = 0, 1024, 1536, 2048
PROJ_W = 6912

ADAM_LR = 0.001
ADAM_B1 = 0.9
ADAM_B2 = 0.999
ADAM_EPS = 1e-08
ADAM_WD = 0.01
ADAM_STEP = 10

VMEM_LIMIT = 56 * 1024 * 1024

_NT = (((1,), (1,)), ((), ()))
_TN = (((0,), (0,)), ((), ()))


def _params(*sem):
    return pltpu.CompilerParams(dimension_semantics=sem, vmem_limit_bytes=VMEM_LIMIT)


@jax.custom_vjp
def _bdot(a, b):
    return jnp.dot(a.astype(BF16), b.astype(BF16), preferred_element_type=F32)


def _bdot_fwd(a, b):
    return _bdot(a, b), (a, b)


def _bdot_bwd(res, g):
    a, b = res
    gb = g.astype(BF16)
    da = lax.dot_general(gb, b.astype(BF16), _NT, preferred_element_type=F32)
    db = lax.dot_general(a.astype(BF16), gb, _TN, preferred_element_type=F32)
    return da.astype(a.dtype), db.astype(b.dtype)


_bdot.defvjp(_bdot_fwd, _bdot_bwd)


@functools.partial(jax.custom_vjp, nondiff_argnums=(1,))
def _split(x, n):
    w = x.shape[-1] // n
    return tuple(x[:, i * w:(i + 1) * w] for i in range(n))


def _split_fwd(x, n):
    return _split(x, n), None


def _split_bwd(n, _, gs):
    return (jnp.concatenate(gs, axis=-1),)


_split.defvjp(_split_fwd, _split_bwd)


@functools.partial(jax.custom_vjp, nondiff_argnums=(1,))
def _shift_rows(x, k):
    return pltpu.roll(x, k % x.shape[0], 0)


def _shift_rows_fwd(x, k):
    return _shift_rows(x, k), None


def _shift_rows_bwd(k, _, g):
    return (_shift_rows(g, -k),)


_shift_rows.defvjp(_shift_rows_fwd, _shift_rows_bwd)


@jax.custom_vjp
def _swap_halves(x):
    h = x.shape[-1] // 2
    return jnp.concatenate([x[:, h:], x[:, :h]], axis=-1)


def _swap_halves_fwd(x):
    return _swap_halves(x), None


def _swap_halves_bwd(_, g):
    return (_swap_halves(g),)


_swap_halves.defvjp(_swap_halves_fwd, _swap_halves_bwd)


def _rms(x, g):
    return x * lax.rsqrt(jnp.mean(x * x, axis=-1, keepdims=True) + EPS) * g


def _rope(x, cosf, sins):
    return x * cosf + _swap_halves(x) * sins


def _sgu_chunk(u, v, z, gain, ws, bs, ga):
    ys = []
    for h in range(A_HEADS):
        vn = _rms(v[h], gain[h])
        s = _bdot(ws[h], vn) + bs[h]
        ys.append(u[h] * s * jax.nn.silu(z[h]))
    ss = sum(jnp.sum(y * y, axis=-1, keepdims=True) for y in ys) * (1.0 / (A_HEADS * HEAD))
    r = lax.rsqrt(ss + EPS)
    return tuple(ys[h] * r * ga[h] for h in range(A_HEADS))


def _conv_tile(bb, bc, bh, bz, w0, w1, w2, cb, gb, w0h, w1h, w2h, cbh, gbh, valid, core):
    t = jnp.where(valid, bc * bh, 0.0)
    y = (jnp.where(core, cb, cbh)
         + _shift_rows(t, 1) * jnp.where(core, w0, w0h)
         + t * jnp.where(core, w1, w1h)
         + _shift_rows(t, -1) * jnp.where(core, w2, w2h))
    return _rms(bb * y * jax.nn.silu(bz), jnp.where(core, gb, gbh))


def _cgate_tile(o, cz, gc):
    return _rms(o * jax.nn.silu(cz), gc)


def _mla_tile(qn, qr, ckv, kr, cosf, sins, wukv, kvg, qng, qrg, kng, krg):
    kv = _split(_bdot(_rms(ckv, kvg), wukv), 2 * C_HEADS)
    k_r = _rope(_rms(kr, krg), cosf, sins)
    qn_h = _split(qn, C_HEADS)
    qr_h = _split(qr, C_HEADS)
    q, k, v = [], [], []
    for h in range(C_HEADS):
        q.append(jnp.concatenate([_rms(qn_h[h], qng), _rope(_rms(qr_h[h], qrg), cosf, sins)], axis=-1) * Q_SCALE)
        k.append(jnp.concatenate([_rms(kv[h], kng), k_r], axis=-1))
        v.append(kv[C_HEADS + h])
    return tuple(q), tuple(k), tuple(v)


def _norm_matmul(h, gain, w, tm, tn, comm=None):
    s_len, k = h.shape
    n = w.shape[1]
    c_ops, c_in_specs, c_shapes, c_out_specs, c_sems, c_begin, c_end = _riding_exchange(comm, 2)
    n_c = len(c_ops)

    def body(h_ref, g_ref, w_ref, *rest):
        c_ins, (o_ref, hn_ref), c_outs, sems = rest[:n_c], rest[n_c:n_c + 2], rest[n_c + 2:2 * n_c + 2], rest[2 * n_c + 2:]
        c_begin(c_ins, c_outs, sems)

        @pl.when(pl.program_id(1) == 0)
        def _():
            hn_ref[...] = _rms(h_ref[...], g_ref[...]).astype(BF16)

        o_ref[...] = jnp.dot(hn_ref[...], w_ref[...], preferred_element_type=F32).astype(BF16)
        c_end(c_ins, c_outs, sems)

    outs = pl.pallas_call(
        body, name="norm_matmul",
        out_shape=(jax.ShapeDtypeStruct((s_len, n), BF16), jax.ShapeDtypeStruct((s_len, k), BF16), *c_shapes),
        grid=(s_len // tm, n // tn),
        in_specs=[pl.BlockSpec((tm, k), lambda i, j: (i, 0)), pl.BlockSpec((1, k), lambda i, j: (0, 0)),
                  pl.BlockSpec((k, tn), lambda i, j: (0, j))] + c_in_specs,
        out_specs=(pl.BlockSpec((tm, tn), lambda i, j: (i, j)), pl.BlockSpec((tm, k), lambda i, j: (i, 0)), *c_out_specs),
        scratch_shapes=c_sems,
        compiler_params=_params("arbitrary", "arbitrary") if comm else _params("parallel", "arbitrary"),
    )(h, gain, w, *c_ops)
    return outs[0], outs[1], outs[2:]


def _out_matmul(h, y, w, tm, tn):
    s_len, n = h.shape
    k = y.shape[1]

    def body(h_ref, y_ref, w_ref, o_ref):
        o_ref[...] = h_ref[...] + jnp.dot(y_ref[...], w_ref[...], preferred_element_type=F32)

    return pl.pallas_call(
        body, name="out_matmul",
        out_shape=jax.ShapeDtypeStruct((s_len, n), F32),
        grid=(s_len // tm, n // tn),
        in_specs=[pl.BlockSpec((tm, tn), lambda i, j: (i, j)), pl.BlockSpec((tm, k), lambda i, j: (i, 0)),
                  pl.BlockSpec((k, tn), lambda i, j: (0, j))],
        out_specs=pl.BlockSpec((tm, tn), lambda i, j: (i, j)),
        compiler_params=_params("parallel", "parallel"),
    )(h, y, w)


def _ple_forward(h1, gain, p, wg, wp, tm, tn):
    s_len, d = h1.shape
    kp = p.shape[1]

    def body(hrow_ref, g_ref, p_ref, wg_ref, wp_ref, o_ref, n1_ref, gate_ref, pp_ref):
        j = pl.program_id(1)

        @pl.when(j == 0)
        def _():
            n1_ref[...] = _rms(hrow_ref[...], g_ref[...]).astype(BF16)

        gate = jax.nn.sigmoid(jnp.dot(n1_ref[...], wg_ref[...], preferred_element_type=F32))
        pp = jnp.dot(p_ref[...].astype(BF16), wp_ref[...], preferred_element_type=F32)
        o_ref[...] = hrow_ref[:, pl.ds(pl.multiple_of(j * tn, tn), tn)] + gate * pp
        gate_ref[...] = gate.astype(BF16)
        pp_ref[...] = pp.astype(BF16)

    col = pl.BlockSpec((tm, tn), lambda i, j: (i, j))
    return pl.pallas_call(
        body, name="ple_forward",
        out_shape=(jax.ShapeDtypeStruct((s_len, d), F32), jax.ShapeDtypeStruct((s_len, d), BF16),
                   jax.ShapeDtypeStruct((s_len, d), BF16), jax.ShapeDtypeStruct((s_len, d), BF16)),
        grid=(s_len // tm, d // tn),
        in_specs=[pl.BlockSpec((tm, d), lambda i, j: (i, 0)), pl.BlockSpec((1, d), lambda i, j: (0, 0)),
                  pl.BlockSpec((tm, kp), lambda i, j: (i, 0)), pl.BlockSpec((d, tn), lambda i, j: (0, j)),
                  pl.BlockSpec((kp, tn), lambda i, j: (0, j))],
        out_specs=(col, pl.BlockSpec((tm, d), lambda i, j: (i, 0)), col, col),
        compiler_params=_params("parallel", "arbitrary"),
    )(h1, gain, p, wg, wp)


def _matmul_nt(a, b, tm, tk, name, comm=None):
    m, n = a.shape
    k = b.shape[0]
    c_ops, c_in_specs, c_shapes, c_out_specs, c_sems, c_begin, c_end = _riding_exchange(comm, 2)
    n_c = len(c_ops)

    def body(a_ref, b_ref, *rest):
        c_ins, o_ref, c_outs, sems = rest[:n_c], rest[n_c], rest[n_c + 1:2 * n_c + 1], rest[2 * n_c + 1:]
        c_begin(c_ins, c_outs, sems)
        o_ref[...] = lax.dot_general(a_ref[...].astype(BF16), b_ref[...].astype(BF16), _NT, preferred_element_type=F32)
        c_end(c_ins, c_outs, sems)

    outs = pl.pallas_call(
        body, name=name,
        out_shape=(jax.ShapeDtypeStruct((m, k), F32), *c_shapes),
        grid=(m // tm, k // tk),
        in_specs=[pl.BlockSpec((tm, n), lambda i, j: (i, 0)), pl.BlockSpec((tk, n), lambda i, j: (j, 0))] + c_in_specs,
        out_specs=(pl.BlockSpec((tm, tk), lambda i, j: (i, j)), *c_out_specs),
        scratch_shapes=c_sems,
        compiler_params=_params("arbitrary", "arbitrary") if comm else _params("parallel", "parallel"),
    )(a, b, *c_ops)
    return outs[0], outs[1:]


def _matmul_tn(a, b, tm, tk, tn, name, comm=None):
    m, k = a.shape
    n = b.shape[1]
    n_m = m // tm
    c_ops, c_in_specs, c_shapes, c_out_specs, c_sems, c_begin, c_end = _riding_exchange(comm, 3)
    n_c = len(c_ops)

    def body(a_ref, b_ref, *rest):
        c_ins, o_ref, c_outs, acc_ref, sems = rest[:n_c], rest[n_c], rest[n_c + 1:2 * n_c + 1], rest[2 * n_c + 1], rest[2 * n_c + 2:]
        c_begin(c_ins, c_outs, sems)
        part = lax.dot_general(a_ref[...].astype(BF16), b_ref[...].astype(BF16), _TN, preferred_element_type=F32)
        _acc(acc_ref, part, pl.program_id(2) == 0)

        @pl.when(pl.program_id(2) == n_m - 1)
        def _():
            o_ref[...] = acc_ref[...].astype(BF16)

        c_end(c_ins, c_outs, sems)

    outs = pl.pallas_call(
        body, name=name,
        out_shape=(jax.ShapeDtypeStruct((k, n), BF16), *c_shapes),
        grid=(k // tk, n // tn, n_m),
        in_specs=[pl.BlockSpec((tm, tk), lambda kk, nn, mm: (mm, kk)), pl.BlockSpec((tm, tn), lambda kk, nn, mm: (mm, nn))]
                 + c_in_specs,
        out_specs=(pl.BlockSpec((tk, tn), lambda kk, nn, mm: (kk, nn)), *c_out_specs),
        scratch_shapes=[pltpu.VMEM((tk, tn), F32)] + c_sems,
        compiler_params=_params("arbitrary", "arbitrary", "arbitrary") if comm else _params("parallel", "parallel", "arbitrary"),
    )(a, b, *c_ops)
    return outs[0], outs[1:]


def _acc(ref, val, first):
    @pl.when(first)
    def _():
        ref[...] = val

    @pl.when(jnp.logical_not(first))
    def _():
        ref[...] += val


def _loss_grad(h, target, tm):
    s_len, d = h.shape

    def body(h_ref, t_ref, dh_ref, loss_ref):
        e = h_ref[...] - t_ref[...]
        dh_ref[...] = e * (1.0 / d)
        part = jnp.sum(jnp.sum(e * e, axis=-1, keepdims=True), axis=0, keepdims=True) * (0.5 / d)
        _acc(loss_ref, jnp.broadcast_to(part, loss_ref.shape), pl.program_id(0) == 0)

    row = pl.BlockSpec((tm, d), lambda i: (i, 0))
    return pl.pallas_call(
        body, name="loss_grad",
        out_shape=(jax.ShapeDtypeStruct((s_len, d), F32), jax.ShapeDtypeStruct((1, 128), F32)),
        grid=(s_len // tm,),
        in_specs=[row, row],
        out_specs=(row, pl.BlockSpec((1, 128), lambda i: (0, 0))),
        compiler_params=_params("arbitrary"),
    )(h, target)


def _rms_backward(x, gain, dn, dres, tm, name):
    s_len, d = x.shape

    def body(x_ref, g_ref, dn_ref, dres_ref, dx_ref, dg_ref):
        _, vjp = jax.vjp(_rms, x_ref[...], g_ref[...])
        dx, dg = vjp(dn_ref[...])
        dx_ref[...] = dres_ref[...] + dx
        _acc(dg_ref, dg, pl.program_id(0) == 0)

    row = pl.BlockSpec((tm, d), lambda i: (i, 0))
    vec = pl.BlockSpec((1, d), lambda i: (0, 0))
    return pl.pallas_call(
        body, name=name,
        out_shape=(jax.ShapeDtypeStruct((s_len, d), F32), jax.ShapeDtypeStruct((1, d), F32)),
        grid=(s_len // tm,),
        in_specs=[row, vec, row, row],
        out_specs=(row, vec),
        compiler_params=_params("arbitrary"),
    )(x, gain, dn, dres)


def _ple_backward(dh2, gate, pp, wg, h1, gain, tm):
    s_len, d = dh2.shape

    def body(dh_ref, gate_ref, pp_ref, wg_ref, h1_ref, g_ref, dh1_ref, dgain_ref, dgp_ref, dpp_ref):
        dh = dh_ref[...]
        gate = gate_ref[...].astype(F32)
        dgp = (dh * pp_ref[...].astype(F32) * gate * (1.0 - gate)).astype(BF16)
        dgp_ref[...] = dgp
        dpp_ref[...] = (dh * gate).astype(BF16)
        dn = lax.dot_general(dgp, wg_ref[...], _NT, preferred_element_type=F32)
        _, vjp = jax.vjp(_rms, h1_ref[...], g_ref[...])
        dx, dgain = vjp(dn)
        dh1_ref[...] = dh + dx
        _acc(dgain_ref, dgain, pl.program_id(0) == 0)

    row = pl.BlockSpec((tm, d), lambda i: (i, 0))
    vec = pl.BlockSpec((1, d), lambda i: (0, 0))
    return pl.pallas_call(
        body, name="ple_backward",
        out_shape=(jax.ShapeDtypeStruct((s_len, d), F32), jax.ShapeDtypeStruct((1, d), F32),
                   jax.ShapeDtypeStruct((s_len, d), BF16), jax.ShapeDtypeStruct((s_len, d), BF16)),
        grid=(s_len // tm,),
        in_specs=[row, row, row, pl.BlockSpec((d, d), lambda i: (0, 0)), row, vec],
        out_specs=(row, vec, row, row),
        compiler_params=_params("arbitrary"),
    )(dh2, gate, pp, wg, h1, gain)


def _sgu_in_specs(tm):
    return [pl.BlockSpec((tm, W_A), lambda i: (i, O_A // W_A)),
            pl.BlockSpec((A_HEADS, HEAD), lambda i: (0, 0)), pl.BlockSpec((A_HEADS, CHUNK, CHUNK), lambda i: (0, 0, 0)),
            pl.BlockSpec((A_HEADS, CHUNK, 1), lambda i: (0, 0, 0)), pl.BlockSpec((1, 512), lambda i: (0, 0))]


def _sgu_load(a_ref, gain_ref, ws_ref, bs_ref, ga_ref, c):
    rows = slice(c * CHUNK, (c + 1) * CHUNK)
    heads = range(A_HEADS)
    u = tuple(a_ref[rows, h * HEAD:(h + 1) * HEAD].astype(F32) for h in heads)
    v = tuple(a_ref[rows, 512 + h * HEAD:512 + (h + 1) * HEAD].astype(F32) for h in heads)
    z = tuple(a_ref[rows, 1024 + h * HEAD:1024 + (h + 1) * HEAD].astype(F32) for h in heads)
    gain = tuple(gain_ref[h:h + 1, :] for h in heads)
    ws = tuple(ws_ref[h] for h in heads)
    bs = tuple(bs_ref[h] for h in heads)
    ga = tuple(ga_ref[:, h * HEAD:(h + 1) * HEAD] for h in heads)
    return u, v, z, gain, ws, bs, ga


def _sgu_forward(proj, gain, ws, bs, ga, tm):
    s_len = proj.shape[0]

    def body(a_ref, gain_ref, ws_ref, bs_ref, ga_ref, o_ref):
        for c in range(tm // CHUNK):
            out = _sgu_chunk(*_sgu_load(a_ref, gain_ref, ws_ref, bs_ref, ga_ref, c))
            for h in range(A_HEADS):
                o_ref[c * CHUNK:(c + 1) * CHUNK, h * HEAD:(h + 1) * HEAD] = out[h].astype(BF16)

    return pl.pallas_call(
        body, name="sgu_forward",
        out_shape=jax.ShapeDtypeStruct((s_len, D_MODEL), BF16),
        grid=(s_len // tm,),
        in_specs=_sgu_in_specs(tm),
        out_specs=pl.BlockSpec((tm, 512), lambda i: (i, 0)),
        compiler_params=_params("parallel"),
    )(proj, gain, ws, bs, ga)


def _sgu_backward(proj, gain, ws, bs, ga, dy, dproj, tm):
    s_len = proj.shape[0]

    def body(a_ref, gain_ref, ws_ref, bs_ref, ga_ref, dy_ref, _, da_ref, dgain_ref, dws_ref, dbs_ref, dga_ref):
        tot = None
        for c in range(tm // CHUNK):
            args = _sgu_load(a_ref, gain_ref, ws_ref, bs_ref, ga_ref, c)
            _, vjp = jax.vjp(_sgu_chunk, *args)
            rows = slice(c * CHUNK, (c + 1) * CHUNK)
            du, dv, dz, dgain, dws, dbs, dga = vjp(tuple(dy_ref[rows, h * HEAD:(h + 1) * HEAD] for h in range(A_HEADS)))
            for h in range(A_HEADS):
                da_ref[rows, h * HEAD:(h + 1) * HEAD] = du[h].astype(BF16)
                da_ref[rows, 512 + h * HEAD:512 + (h + 1) * HEAD] = dv[h].astype(BF16)
                da_ref[rows, 1024 + h * HEAD:1024 + (h + 1) * HEAD] = dz[h].astype(BF16)
            part = (dgain, dws, dbs, dga)
            tot = part if tot is None else jax.tree.map(jnp.add, tot, part)
        dgain, dws, dbs, dga = tot
        first = pl.program_id(0) == 0
        _acc(dgain_ref, jnp.concatenate(dgain, axis=0), first)
        _acc(dga_ref, jnp.concatenate(dga, axis=-1), first)
        for h in range(A_HEADS):
            _acc(dws_ref.at[h], dws[h], first)
            _acc(dbs_ref.at[h], dbs[h], first)

    small = [pl.BlockSpec((A_HEADS, HEAD), lambda i: (0, 0)), pl.BlockSpec((A_HEADS, CHUNK, CHUNK), lambda i: (0, 0, 0)),
             pl.BlockSpec((A_HEADS, CHUNK, 1), lambda i: (0, 0, 0)), pl.BlockSpec((1, 512), lambda i: (0, 0))]
    return pl.pallas_call(
        body, name="sgu_backward",
        out_shape=(jax.ShapeDtypeStruct(dproj.shape, BF16),
                   jax.ShapeDtypeStruct((A_HEADS, HEAD), F32), jax.ShapeDtypeStruct((A_HEADS, CHUNK, CHUNK), F32),
                   jax.ShapeDtypeStruct((A_HEADS, CHUNK, 1), F32), jax.ShapeDtypeStruct((1, 512), F32)),
        grid=(s_len // tm,),
        in_specs=_sgu_in_specs(tm) + [pl.BlockSpec((tm, 512), lambda i: (i, 0)), pl.BlockSpec(memory_space=pl.ANY)],
        out_specs=(pl.BlockSpec((tm, W_A), lambda i: (i, O_A // W_A)), *small),
        input_output_aliases={6: 0},
        compiler_params=_params("arbitrary"),
    )(proj, gain, ws, bs, ga, dy, dproj)


def _halo_specs(tm, width, col, n_rows):
    per = tm // HALO
    last = n_rows // HALO - 1
    return [pl.BlockSpec((HALO, width), lambda i: (jnp.maximum(i * per - 1, 0), col)),
            pl.BlockSpec((tm, width), lambda i: (i, col)),
            pl.BlockSpec((HALO, width), lambda i: (jnp.minimum((i + 1) * per, last), col))]


def _conv_masks(tm, s_len):
    r = lax.broadcasted_iota(jnp.int32, (tm + 2 * HALO, 1), 0)
    g = pl.program_id(0) * tm - HALO + r
    return (g >= 0) & (g < s_len), (r >= HALO) & (r < HALO + tm)


def _conv_inputs(b_refs, cw_ref, cb_ref, gb_ref):
    ext = jnp.concatenate([r[...] for r in b_refs], axis=0).astype(F32)
    bb, bc, bh, bz = (ext[:, j * 512:(j + 1) * 512] for j in range(4))
    prm = (cw_ref[0:1, :], cw_ref[1:2, :], cw_ref[2:3, :], cb_ref[...], gb_ref[...])
    return (bb, bc, bh, bz), prm


def _conv_forward(proj, cw, cb, gb, y, tm):
    s_len = proj.shape[0]

    def body(p0, p1, p2, cw_ref, cb_ref, gb_ref, _, o_ref):
        acts, prm = _conv_inputs((p0, p1, p2), cw_ref, cb_ref, gb_ref)
        valid, core = _conv_masks(tm, s_len)
        out = _conv_tile(*acts, *prm, *prm, valid, core)
        o_ref[...] = out[HALO:HALO + tm].astype(BF16)

    vec = pl.BlockSpec((1, 512), lambda i: (0, 0))
    return pl.pallas_call(
        body, name="conv_forward",
        out_shape=jax.ShapeDtypeStruct(y.shape, BF16),
        grid=(s_len // tm,),
        in_specs=_halo_specs(tm, W_B, O_B // W_B, s_len) + [pl.BlockSpec((3, 512), lambda i: (0, 0)), vec, vec,
                                                             pl.BlockSpec(memory_space=pl.ANY)],
        out_specs=pl.BlockSpec((tm, 512), lambda i: (i, 1)),
        input_output_aliases={6: 0},
        compiler_params=_params("parallel"),
    )(proj, proj, proj, cw, cb, gb, y)


def _conv_backward(proj, cw, cb, gb, dy, tm):
    s_len = proj.shape[0]

    def body(p0, p1, p2, cw_ref, cb_ref, gb_ref, d0, d1, d2, db_ref, dcw_ref, dcb_ref, dgb_ref):
        acts, prm = _conv_inputs((p0, p1, p2), cw_ref, cb_ref, gb_ref)
        valid, core = _conv_masks(tm, s_len)
        _, vjp = jax.vjp(lambda a, p: _conv_tile(*a, *p, *prm, valid, core), acts, prm)
        dy_ext = jnp.where(valid, jnp.concatenate([d0[...], d1[...], d2[...]], axis=0), 0.0)
        dacts, dprm = vjp(dy_ext)
        for j in range(4):
            db_ref[:, j * 512:(j + 1) * 512] = dacts[j][HALO:HALO + tm].astype(BF16)
        first = pl.program_id(0) == 0
        _acc(dcw_ref, jnp.concatenate(dprm[0:3], axis=0), first)
        _acc(dcb_ref, dprm[3], first)
        _acc(dgb_ref, dprm[4], first)

    vec = pl.BlockSpec((1, 512), lambda i: (0, 0))
    mat = pl.BlockSpec((3, 512), lambda i: (0, 0))
    return pl.pallas_call(
        body, name="conv_backward",
        out_shape=(jax.ShapeDtypeStruct((s_len, PROJ_W), BF16), jax.ShapeDtypeStruct((3, 512), F32),
                   jax.ShapeDtypeStruct((1, 512), F32), jax.ShapeDtypeStruct((1, 512), F32)),
        grid=(s_len // tm,),
        in_specs=_halo_specs(tm, W_B, O_B // W_B, s_len) + [mat, vec, vec] + _halo_specs(tm, 512, 1, s_len),
        out_specs=(pl.BlockSpec((tm, W_B), lambda i: (i, O_B // W_B)), mat, vec, vec),
        compiler_params=_params("arbitrary"),
    )(proj, proj, proj, cw, cb, gb, dy, dy, dy)


def _cgate_forward(o, proj, gc, y, tm):
    s_len = o.shape[0]

    def body(o_ref, cz_ref, gc_ref, _, y_ref):
        y_ref[...] = _cgate_tile(o_ref[...], cz_ref[...].astype(F32), gc_ref[...]).astype(BF16)

    return pl.pallas_call(
        body, name="cgate_forward",
        out_shape=jax.ShapeDtypeStruct(y.shape, BF16),
        grid=(s_len // tm,),
        in_specs=[pl.BlockSpec((tm, W_CZ), lambda i: (i, 0)), pl.BlockSpec((tm, W_CZ), lambda i: (i, O_CZ // W_CZ)),
                  pl.BlockSpec((1, W_CZ), lambda i: (0, 0)), pl.BlockSpec(memory_space=pl.ANY)],
        out_specs=pl.BlockSpec((tm, W_CZ), lambda i: (i, 1)),
        input_output_aliases={3: 0},
        compiler_params=_params("parallel"),
    )(o, proj, gc, y)


def _cgate_backward(o, proj, gc, dy, dproj, tm, stat_chunk):
    s_len = o.shape[0]
    per_stat = stat_chunk // tm

    def body(o_ref, cz_ref, gc_ref, dy_ref, _, dcz_ref, do_ref, dsum_ref, dgc_ref):
        o = o_ref[...]
        _, vjp = jax.vjp(_cgate_tile, o, cz_ref[...].astype(F32), gc_ref[...])
        do, dcz, dgc = vjp(dy_ref[...])
        dcz_ref[...] = dcz.astype(BF16)
        do_ref[...] = do.astype(BF16)
        ones = jnp.ones((8, HEAD), F32)
        for h in range(C_HEADS):
            cols = slice(h * HEAD, (h + 1) * HEAD)
            sums = lax.dot_general(ones, do[:, cols] * o[:, cols], _NT, precision=lax.Precision.HIGHEST,
                                   preferred_element_type=F32)
            dsum_ref[h, 0] = sums[0:1]
        _acc(dgc_ref, dgc, pl.program_id(0) == 0)

    row = pl.BlockSpec((tm, W_CZ), lambda i: (i, 0))
    vec = pl.BlockSpec((1, W_CZ), lambda i: (0, 0))
    return pl.pallas_call(
        body, name="cgate_backward",
        out_shape=(jax.ShapeDtypeStruct(dproj.shape, BF16), jax.ShapeDtypeStruct((s_len, W_CZ), BF16),
                   jax.ShapeDtypeStruct((C_HEADS, s_len // stat_chunk, 1, stat_chunk), F32), jax.ShapeDtypeStruct((1, W_CZ), F32)),
        grid=(s_len // tm,),
        in_specs=[row, pl.BlockSpec((tm, W_CZ), lambda i: (i, O_CZ // W_CZ)), vec,
                  pl.BlockSpec((tm, W_CZ), lambda i: (i, 1)), pl.BlockSpec(memory_space=pl.ANY)],
        out_specs=(pl.BlockSpec((tm, W_CZ), lambda i: (i, O_CZ // W_CZ)), row,
                   pl.BlockSpec((C_HEADS, 1, 1, tm), lambda i: (0, i // per_stat, 0, i % per_stat)), vec),
        input_output_aliases={4: 0},
        compiler_params=_params("arbitrary"),
    )(o, proj, gc, dy, dproj)


def _mla_small_specs():
    return [pl.BlockSpec((KV_RANK, 2 * C_HEADS * HEAD), lambda i: (0, 0)), pl.BlockSpec((1, KV_RANK), lambda i: (0, 0)),
            pl.BlockSpec((1, HEAD), lambda i: (0, 0)), pl.BlockSpec((1, ROPE), lambda i: (0, 0)),
            pl.BlockSpec((1, HEAD), lambda i: (0, 0)), pl.BlockSpec((1, ROPE), lambda i: (0, 0))]


def _mla_load(m_ref, cos_ref, sin_ref):
    qn = m_ref[:, M_QN:M_QN + C_HEADS * HEAD].astype(F32)
    qr = m_ref[:, M_QR:M_QR + C_HEADS * ROPE].astype(F32)
    ckv = m_ref[:, M_CKV:M_CKV + KV_RANK].astype(F32)
    kr = m_ref[:, M_KR:M_KR + ROPE].astype(F32)
    return qn, qr, ckv, kr, cos_ref[...], sin_ref[...]


def _mla_forward(proj, cosf, sins, wukv, kvg, qng, qrg, kng, krg, tm, kt_chunk, vt_chunk):
    s_len = proj.shape[0]

    def body(m_ref, cos_ref, sin_ref, w_ref, kvg_ref, qng_ref, qrg_ref, kng_ref, krg_ref, q_ref, k_ref, v_ref, kt_ref, vt_ref):
        q, k, v = _mla_tile(*_mla_load(m_ref, cos_ref, sin_ref), w_ref[...], kvg_ref[...], qng_ref[...], qrg_ref[...],
                            kng_ref[...], krg_ref[...])
        for h in range(C_HEADS):
            q_ref[h] = q[h].astype(BF16)
            k_ref[h] = k[h].astype(BF16)
            v_ref[h] = v[h].astype(BF16)
            kt_ref[h, 0] = jnp.concatenate([k[h][:, :HEAD].T, k[h][:, HEAD:].T], axis=0).astype(BF16)
            vt_ref[h, 0] = v[h].T.astype(BF16)

    rope_spec = pl.BlockSpec((tm, ROPE), lambda i: (i, 0))
    qk_spec = pl.BlockSpec((C_HEADS, tm, QK), lambda i: (0, i, 0))
    per_k, per_v = kt_chunk // tm, vt_chunk // tm
    return pl.pallas_call(
        body, name="mla_forward",
        out_shape=(jax.ShapeDtypeStruct((C_HEADS, s_len, QK), BF16), jax.ShapeDtypeStruct((C_HEADS, s_len, QK), BF16),
                   jax.ShapeDtypeStruct((C_HEADS, s_len, HEAD), BF16),
                   jax.ShapeDtypeStruct((C_HEADS, s_len // kt_chunk, QK, kt_chunk), BF16),
                   jax.ShapeDtypeStruct((C_HEADS, s_len // vt_chunk, HEAD, vt_chunk), BF16)),
        grid=(s_len // tm,),
        in_specs=[pl.BlockSpec((tm, W_M), lambda i: (i, O_M // W_M)), rope_spec, rope_spec] + _mla_small_specs(),
        out_specs=(qk_spec, qk_spec, pl.BlockSpec((C_HEADS, tm, HEAD), lambda i: (0, i, 0)),
                   pl.BlockSpec((C_HEADS, 1, QK, tm), lambda i: (0, i // per_k, 0, i % per_k)),
                   pl.BlockSpec((C_HEADS, 1, HEAD, tm), lambda i: (0, i // per_v, 0, i % per_v))),
        compiler_params=_params("parallel"),
    )(proj, cosf, sins, wukv, kvg, qng, qrg, kng, krg)


def _mla_backward(proj, cosf, sins, wukv, kvg, qng, qrg, kng, krg, dqt, dk, dv, dproj, tm):
    s_len = proj.shape[0]
    per_chunk = dqt.shape[3] // tm

    def body(m_ref, cos_ref, sin_ref, w_ref, kvg_ref, qng_ref, qrg_ref, kng_ref, krg_ref, dq_ref, dk_ref, dv_ref, _,
             dm_ref, dw_ref, dkvg_ref, dqng_ref, dqrg_ref, dkng_ref, dkrg_ref):
        qn, qr, ckv, kr, cosf_t, sins_t = _mla_load(m_ref, cos_ref, sin_ref)
        prm = (w_ref[...], kvg_ref[...], qng_ref[...], qrg_ref[...], kng_ref[...], krg_ref[...])
        _, vjp = jax.vjp(lambda a, p: _mla_tile(*a, cosf_t, sins_t, *p), (qn, qr, ckv, kr), prm)
        heads = range(C_HEADS)
        dacts, dprm = vjp((tuple(dq_ref[h, 0].T for h in heads), tuple(dk_ref[h] for h in heads), tuple(dv_ref[h] for h in heads)))
        dm_ref[:, M_QN:M_QN + C_HEADS * HEAD] = dacts[0].astype(BF16)
        dm_ref[:, M_QR:M_QR + C_HEADS * ROPE] = dacts[1].astype(BF16)
        dm_ref[:, M_CKV:M_CKV + KV_RANK] = dacts[2].astype(BF16)
        pad = jnp.zeros((tm, W_M - M_KR - ROPE), F32)
        dm_ref[:, M_KR:W_M] = jnp.concatenate([dacts[3], pad], axis=-1).astype(BF16)
        first = pl.program_id(0) == 0
        for ref, val in zip((dw_ref, dkvg_ref, dqng_ref, dqrg_ref, dkng_ref, dkrg_ref), dprm):
            _acc(ref, val.astype(F32), first)

    rope_spec = pl.BlockSpec((tm, ROPE), lambda i: (i, 0))
    qk_spec = pl.BlockSpec((C_HEADS, tm, QK), lambda i: (0, i, 0))
    small = _mla_small_specs()
    return pl.pallas_call(
        body, name="mla_backward",
        out_shape=(jax.ShapeDtypeStruct(dproj.shape, BF16), jax.ShapeDtypeStruct((KV_RANK, 2 * C_HEADS * HEAD), F32),
                   jax.ShapeDtypeStruct((1, KV_RANK), F32), jax.ShapeDtypeStruct((1, HEAD), F32),
                   jax.ShapeDtypeStruct((1, ROPE), F32), jax.ShapeDtypeStruct((1, HEAD), F32),
                   jax.ShapeDtypeStruct((1, ROPE), F32)),
        grid=(s_len // tm,),
        in_specs=[pl.BlockSpec((tm, W_M), lambda i: (i, O_M // W_M)), rope_spec, rope_spec] + small
                 + [pl.BlockSpec((C_HEADS, 1, QK, tm), lambda i: (0, i // per_chunk, 0, i % per_chunk)), qk_spec,
                    pl.BlockSpec((C_HEADS, tm, HEAD), lambda i: (0, i, 0)), pl.BlockSpec(memory_space=pl.ANY)],
        out_specs=(pl.BlockSpec((tm, W_M), lambda i: (i, O_M // W_M)), *small),
        input_output_aliases={12: 0},
        compiler_params=_params("arbitrary"),
    )(proj, cosf, sins, wukv, kvg, qng, qrg, kng, krg, dqt, dk, dv, dproj)


def _attention_forward(q, k, vt, tq, stat_chunk, comm=None):
    n_heads, s_len, _ = q.shape
    n_chunks, _, ck = vt.shape[1:]
    c_ops, c_in_specs, c_shapes, c_out_specs, c_sems, c_begin, c_end = _riding_exchange(comm, 2)
    n_c = len(c_ops)

    def body(q_ref, k_ref, vt_ref, *rest):
        c_ins, (o_ref, lse_ref), c_outs, sems = rest[:n_c], rest[n_c:n_c + 2], rest[n_c + 2:2 * n_c + 2], rest[2 * n_c + 2:]
        c_begin(c_ins, c_outs, sems)
        q_t = q_ref[0]

        def step(j, carry):
            m_old, l_old, acc = carry
            k_j = k_ref[0, pl.ds(pl.multiple_of(j * ck, ck), ck), :]
            s = lax.dot_general(k_j, q_t, _NT, preferred_element_type=F32)
            m_new = jnp.maximum(m_old, jnp.max(s, axis=0, keepdims=True))
            p = jnp.exp2(s - m_new)
            alpha = jnp.exp2(m_old - m_new)
            l_new = alpha * l_old + jnp.sum(p, axis=0, keepdims=True)
            acc = alpha * acc + jnp.dot(vt_ref[0, j], p.astype(BF16), preferred_element_type=F32)
            return m_new, l_new, acc

        init = (jnp.full((1, tq), -jnp.inf, F32), jnp.zeros((1, tq), F32), jnp.zeros((HEAD, tq), F32))
        m_fin, l_fin, acc = lax.fori_loop(0, n_chunks, step, init)
        o_ref[...] = (acc / l_fin).T
        lse_ref[0, 0] = m_fin + jnp.log2(l_fin)
        c_end(c_ins, c_outs, sems)

    per_stat = stat_chunk // tq
    outs = pl.pallas_call(
        body, name="attention_forward",
        out_shape=(jax.ShapeDtypeStruct((s_len, n_heads * HEAD), F32),
                   jax.ShapeDtypeStruct((n_heads, s_len // stat_chunk, 1, stat_chunk), F32), *c_shapes),
        grid=(n_heads, s_len // tq),
        in_specs=[pl.BlockSpec((1, tq, QK), lambda h, i: (h, i, 0)), pl.BlockSpec((1, s_len, QK), lambda h, i: (h, 0, 0)),
                  pl.BlockSpec((1, n_chunks, HEAD, ck), lambda h, i: (h, 0, 0, 0))] + c_in_specs,
        out_specs=(pl.BlockSpec((tq, HEAD), lambda h, i: (i, h)),
                   pl.BlockSpec((1, 1, 1, tq), lambda h, i: (h, i // per_stat, 0, i % per_stat)), *c_out_specs),
        scratch_shapes=c_sems,
        compiler_params=_params("arbitrary", "arbitrary") if comm else _params("parallel", "parallel"),
    )(q, k, vt, *c_ops)
    return outs[0], outs[1], outs[2:]


def _attention_backward(q, k, kt, v, do, lse, dsum, comm=None):
    n_heads, s_len, _ = q.shape
    tk = kt.shape[3]
    n_q, _, cq = lse.shape[1:]
    c_ops, c_in_specs, c_shapes, c_out_specs, c_sems, c_begin, c_end = _riding_exchange(comm, 2)
    n_c = len(c_ops)

    def body(q_ref, k_ref, kt_ref, v_ref, do_ref, lse_ref, dsum_ref, *rest):
        c_ins, (dqt_ref, dk_ref, dv_ref), c_outs, sems = rest[:n_c], rest[n_c:n_c + 3], rest[n_c + 3:2 * n_c + 3], rest[2 * n_c + 3:]
        c_begin(c_ins, c_outs, sems)
        first = pl.program_id(1) == 0
        k_j, kt_j, v_j = k_ref[0], kt_ref[0, 0], v_ref[0]

        def step(i, carry):
            dk, dv = carry
            rows = pl.ds(pl.multiple_of(i * cq, cq), cq)
            q_i, do_i = q_ref[0, rows, :], do_ref[rows, :]
            s = lax.dot_general(k_j, q_i, _NT, preferred_element_type=F32)
            p = jnp.exp2(s - lse_ref[0, i])
            dp = lax.dot_general(v_j, do_i, _NT, preferred_element_type=F32)
            ds = (p * (dp - dsum_ref[0, i]) * LN_2).astype(BF16)
            dv = dv + jnp.dot(p.astype(BF16), do_i, preferred_element_type=F32)
            dk = dk + jnp.dot(ds, q_i, preferred_element_type=F32)
            _acc(dqt_ref.at[0, i], jnp.dot(kt_j, ds, preferred_element_type=F32), first)
            return dk, dv

        dk, dv = lax.fori_loop(0, n_q, step, (jnp.zeros((tk, QK), F32), jnp.zeros((tk, HEAD), F32)))
        dk_ref[0] = dk
        dv_ref[0] = dv
        c_end(c_ins, c_outs, sems)

    stat = pl.BlockSpec((1, n_q, 1, cq), lambda h, j: (h, 0, 0, 0))
    outs = pl.pallas_call(
        body, name="attention_backward",
        out_shape=(jax.ShapeDtypeStruct((n_heads, n_q, QK, cq), F32), jax.ShapeDtypeStruct((n_heads, s_len, QK), F32),
                   jax.ShapeDtypeStruct((n_heads, s_len, HEAD), F32), *c_shapes),
        grid=(n_heads, s_len // tk),
        in_specs=[pl.BlockSpec((1, s_len, QK), lambda h, j: (h, 0, 0)), pl.BlockSpec((1, tk, QK), lambda h, j: (h, j, 0)),
                  pl.BlockSpec((1, 1, QK, tk), lambda h, j: (h, j, 0, 0)),
                  pl.BlockSpec((1, tk, HEAD), lambda h, j: (h, j, 0)), pl.BlockSpec((s_len, HEAD), lambda h, j: (0, h)),
                  stat, stat] + c_in_specs,
        out_specs=(pl.BlockSpec((1, n_q, QK, cq), lambda h, j: (h, 0, 0, 0)), pl.BlockSpec((1, tk, QK), lambda h, j: (h, j, 0)),
                   pl.BlockSpec((1, tk, HEAD), lambda h, j: (h, j, 0)), *c_out_specs),
        scratch_shapes=c_sems,
        compiler_params=_params("arbitrary", "arbitrary") if comm else _params("parallel", "arbitrary"),
    )(q, k, kt, v, do, lse, dsum, *c_ops)
    return outs[0], outs[1], outs[2], outs[3:]


def _exchange(arrs, gather, name):
    n = len(arrs)

    def body(*refs):
        plan = _exchange_plan(refs[:n], refs[n:2 * n], gather, *refs[2 * n:])
        _exchange_start(plan)
        _exchange_wait(plan)

    any_spec = pl.BlockSpec(memory_space=pl.ANY)
    return pl.pallas_call(
        body, name=name,
        out_shape=_exchange_out_shapes(arrs, gather),
        in_specs=[any_spec] * n,
        out_specs=tuple([any_spec] * n),
        scratch_shapes=_exchange_semaphores(n),
        compiler_params=pltpu.CompilerParams(has_side_effects=True),
    )(*arrs)


def _gather_via_sibling(block, name):
    def body(x_ref, out_ref, *sems):
        plan = _sibling_plan([x_ref], [out_ref], *sems)
        _sibling_start(plan)
        _sibling_forward(plan)
        _sibling_finish(plan)

    any_spec = pl.BlockSpec(memory_space=pl.ANY)
    return pl.pallas_call(
        body, name=name,
        out_shape=jax.ShapeDtypeStruct((N_DEV, *block.shape), block.dtype),
        in_specs=[any_spec], out_specs=any_spec,
        scratch_shapes=_exchange_semaphores(1),
        compiler_params=pltpu.CompilerParams(has_side_effects=True),
    )(block)


def _sibling_plan(ins, outs, send_sems, recv_sems, local_sems):
    x, y, c = lax.axis_index("x"), lax.axis_index("y"), lax.axis_index("c")
    me, sibling = (x, y, c), (x, y, 1 - c)
    chips = [(1 - x, y), (x, 1 - y), (1 - x, 1 - y)]
    local, first, ici_arrivals, passes, sibling_arrivals = [], [], [], [], []
    for a, (x_ref, out_ref) in enumerate(zip(ins, outs)):
        def slot(px, py, pc, out_ref=out_ref):
            return out_ref.at[4 * px + 2 * py + pc]

        def copy(k, block_of, to, src=None, a=a, slot=slot):
            idx = a * (N_DEV - 1) + k
            return pltpu.make_async_remote_copy(
                src_ref=slot(*block_of) if src is None else src, dst_ref=slot(*block_of), send_sem=send_sems.at[idx],
                recv_sem=recv_sems.at[idx], device_id=to, device_id_type=pl.DeviceIdType.MESH)

        local.append(pltpu.make_async_copy(x_ref, slot(*me), local_sems.at[a]))
        first += [copy(0, me, sibling, src=x_ref)] + [copy(1 + j, me, (*chip, c), src=x_ref) for j, chip in enumerate(chips)]
        ici_arrivals += [copy(1 + j, (*chip, c), me) for j, chip in enumerate(chips)]
        passes += [copy(4 + j, (*chip, c), sibling) for j, chip in enumerate(chips)]
        sibling_arrivals += [copy(0, sibling, me)] + [copy(4 + j, (*chip, 1 - c), me) for j, chip in enumerate(chips)]
    return local, first, ici_arrivals, passes, sibling_arrivals


def _sibling_start(plan):
    local, first = plan[0], plan[1]
    for cp in local + first:
        cp.start()


def _sibling_forward(plan):
    for arrival, onward in zip(plan[2], plan[3]):
        arrival.wait_recv()
        onward.start()


def _sibling_finish(plan):
    local, first, _, passes, sibling_arrivals = plan
    for arrival in sibling_arrivals:
        arrival.wait_recv()
    for cp in first + passes:
        cp.wait_send()
    for cp in local:
        cp.wait()


def _exchange_out_shapes(arrs, gather):
    return tuple(jax.ShapeDtypeStruct((N_DEV, *(a.shape if g else a.shape[1:])), a.dtype) for a, g in zip(arrs, gather))


def _exchange_semaphores(n):
    n_remote = n * (N_DEV - 1)
    return [pltpu.SemaphoreType.DMA((n_remote,)), pltpu.SemaphoreType.DMA((n_remote,)), pltpu.SemaphoreType.DMA((n,))]


def _exchange_plan(ins, outs, gather, send_sems, recv_sems, local_sems):
    n = len(ins)
    x, y, c = lax.axis_index("x"), lax.axis_index("y"), lax.axis_index("c")
    me = 4 * x + 2 * y + c

    def block_for(a, dev):
        return ins[a] if gather[a] else ins[a].at[dev]

    local = [pltpu.make_async_copy(block_for(a, me), outs[a].at[me], local_sems.at[a]) for a in range(n)]
    remote = []
    for k in range(1, N_DEV):
        px = 1 - x if k & 4 else x
        py = 1 - y if k & 2 else y
        pc = 1 - c if k & 1 else c
        peer = 4 * px + 2 * py + pc
        for a in range(n):
            idx = a * (N_DEV - 1) + k - 1
            send = pltpu.make_async_remote_copy(
                src_ref=block_for(a, peer), dst_ref=outs[a].at[me], send_sem=send_sems.at[idx], recv_sem=recv_sems.at[idx],
                device_id=(px, py, pc), device_id_type=pl.DeviceIdType.MESH)
            arrive = pltpu.make_async_remote_copy(
                src_ref=block_for(a, peer), dst_ref=outs[a].at[peer], send_sem=send_sems.at[idx], recv_sem=recv_sems.at[idx],
                device_id=(px, py, pc), device_id_type=pl.DeviceIdType.MESH)
            remote.append((send, arrive))
    return local, remote


def _exchange_start(plan):
    local, remote = plan
    for cp in local:
        cp.start()
    for send, _ in remote:
        send.start()


def _exchange_wait(plan):
    local, remote = plan
    for send, arrive in remote:
        send.wait_send()
        arrive.wait_recv()
    for cp in local:
        cp.wait()


def _riding_exchange(comm, n_grid):
    if comm is None:
        return [], [], (), (), [], lambda *_: None, lambda *_: None
    arrs, gather, *via_sibling = comm
    via_sibling = bool(via_sibling and via_sibling[0])
    n = len(arrs)
    any_spec = pl.BlockSpec(memory_space=pl.ANY)

    def at_step(step_of_first_axis):
        rest = [pl.program_id(d) == (0 if step_of_first_axis != "last" else pl.num_programs(d) - 1) for d in range(1, n_grid)]
        lead = {"first": 0, "middle": pl.num_programs(0) // 2, "last": pl.num_programs(0) - 1}[step_of_first_axis]
        return functools.reduce(jnp.logical_and, [pl.program_id(0) == lead] + rest)

    def begin(ins, outs, sems):
        @pl.when(at_step("first"))
        def _():
            if via_sibling:
                _sibling_start(_sibling_plan(ins, outs, *sems))
            else:
                _exchange_start(_exchange_plan(ins, outs, gather, *sems))

        if via_sibling:
            @pl.when(at_step("middle"))
            def _():
                _sibling_forward(_sibling_plan(ins, outs, *sems))

    def end(ins, outs, sems):
        @pl.when(at_step("last"))
        def _():
            if via_sibling:
                _sibling_finish(_sibling_plan(ins, outs, *sems))
            else:
                _exchange_wait(_exchange_plan(ins, outs, gather, *sems))

    return (list(arrs), [any_spec] * n, _exchange_out_shapes(arrs, gather), tuple([any_spec] * n), _exchange_semaphores(n),
            begin, end)


ADAM_TILE_ELEMS = 256 * 1024


def _sum_adam(parts, w, m, v, layer, prev, name):
    n_parts, r, c = parts.shape
    tm, tc = r, c
    if r % 16 == 0:
        while tm * c > ADAM_TILE_ELEMS and tm % 16 == 0:
            tm //= 2
    else:
        while r * tc > ADAM_TILE_ELEMS and tc % 256 == 0:
            tc //= 2

    def body(p_ref, w_ref, m_ref, v_ref, *rest):
        g_ref, d_ref, nm_ref, nv_ref = rest[-4:]
        g = p_ref[0].astype(F32)
        for s in range(1, n_parts):
            g = g + p_ref[s].astype(F32)
        m_new = ADAM_B1 * m_ref[...] + (1.0 - ADAM_B1) * g
        v_new = ADAM_B2 * v_ref[...] + (1.0 - ADAM_B2) * (g * g)
        m_hat = m_new / (1.0 - ADAM_B1 ** ADAM_STEP)
        v_hat = v_new / (1.0 - ADAM_B2 ** ADAM_STEP)
        g_ref[...] = g
        d_ref[...] = -ADAM_LR * (m_hat / (jnp.sqrt(v_hat) + ADAM_EPS) + ADAM_WD * w_ref[...])
        nm_ref[...] = m_new
        nv_ref[...] = v_new

    slab = pl.BlockSpec((None, tm, tc), lambda i, j: (layer, i, j))
    n_prev = 0 if prev is None else 4
    return pl.pallas_call(
        body, name=name,
        out_shape=(jax.ShapeDtypeStruct(w.shape, F32),) * 4,
        grid=(r // tm, c // tc),
        in_specs=[pl.BlockSpec((n_parts, tm, tc), lambda i, j: (0, i, j)), slab, slab, slab]
                 + [pl.BlockSpec(memory_space=pl.ANY)] * n_prev,
        out_specs=(slab, slab, slab, slab),
        input_output_aliases={4 + j: j for j in range(n_prev)},
        compiler_params=_params("parallel", "parallel"),
    )(parts, w, m, v, *(prev or ()))


def _permute_in(w):
    k = w.shape[0]
    q = w[:, 3584:5120].reshape(k, C_HEADS, QK)
    return jnp.concatenate(
        [w[:, 1536:3584], w[:, 5696:6720], w[:, 0:1536], q[:, :, :HEAD].reshape(k, C_HEADS * HEAD),
         q[:, :, HEAD:].reshape(k, C_HEADS * ROPE), w[:, 5120:5632], w[:, 5632:5696],
         jnp.zeros((k, PROJ_W - IN_WIDTH), w.dtype)], axis=1)


def _unpermute_in(g):
    k = g.shape[0]
    qn = g[:, O_M + M_QN:O_M + M_QR].reshape(k, C_HEADS, HEAD)
    qr = g[:, O_M + M_QR:O_M + M_CKV].reshape(k, C_HEADS, ROPE)
    q = jnp.concatenate([qn, qr], axis=-1).reshape(k, C_HEADS * QK)
    return jnp.concatenate(
        [g[:, O_A:O_A + W_A], g[:, O_B:O_B + W_B], q, g[:, O_M + M_CKV:O_M + M_KR],
         g[:, O_M + M_KR:O_M + M_KR + ROPE], g[:, O_CZ:O_CZ + W_CZ]], axis=1)


SMALL = ("attn_norm", "sgu_norm", "w_spatial", "b_spatial", "conv_b", "kv_norm", "q_nope_norm", "q_rope_norm",
         "k_nope_norm", "k_rope_norm", "out_norm", "ple_norm")
PACK_ROWS = 256


def _pack(tensors):
    flat = jnp.concatenate([t.reshape(-1) for t in tensors])
    rows = -(-flat.shape[0] // (128 * PACK_ROWS)) * PACK_ROWS
    return jnp.pad(flat, (0, rows * 128 - flat.shape[0])).reshape(rows, 128)


def _unpack(packed, like):
    flat = packed.reshape(-1)
    out, pos = [], 0
    for t in like:
        out.append(flat[pos:pos + t.size].reshape(t.shape))
        pos += t.size
    return out


def _tile(s_len, want):
    return min(want, s_len)


ATT_FWD_QUERIES = 512
ATT_FWD_KEYS = 8192
ATT_BWD_KEYS = 1024
ATT_BWD_QUERIES = 4096


def _layer_forward(h, p_l, cosf, sins, w, sm, comm, comm_rest):
    s_len = h.shape[0]
    tm = _tile(s_len, 512)
    proj, hn, rest = _norm_matmul(h, sm["attn_norm"], w["w_in"], _tile(s_len, 1024), 768, comm_rest)
    if comm_rest is not None:
        w = {**w, **_assemble_rest(rest)}
    ga, gb, gc = sm["out_norm"][:, 0:512], sm["out_norm"][:, 512:1024], sm["out_norm"][:, 1024:2048]
    y = _sgu_forward(proj, sm["sgu_norm"], sm["w_spatial"], sm["b_spatial"], ga, _tile(s_len, 256))
    y = _conv_forward(proj, w["conv_w"], sm["conv_b"], gb, y, _tile(s_len, 256))
    q, k, v, kt, vt = _mla_forward(proj, cosf, sins, w["w_ukv"], sm["kv_norm"], sm["q_nope_norm"], sm["q_rope_norm"],
                                   sm["k_nope_norm"], sm["k_rope_norm"], tm, _tile(s_len, ATT_BWD_KEYS),
                                   _tile(s_len, ATT_FWD_KEYS))
    o, lse, arrived = _attention_forward(q, k, vt, _tile(s_len, ATT_FWD_QUERIES), _tile(s_len, ATT_BWD_QUERIES), comm)
    y = _cgate_forward(o, proj, gc, y, _tile(s_len, 256))
    h1 = _out_matmul(h, y, w["w_out"], tm, D_MODEL)
    h2, n1, gate, pp = _ple_forward(h1, sm["ple_norm"], p_l, w["w_ple_gate"], w["w_ple_proj"], tm, D_MODEL)
    saved = dict(h=h, hn=hn, proj=proj, y=y, q=q, k=k, v=v, kt=kt, o=o, lse=lse, h1=h1, n1=n1, gate=gate, pp=pp)
    return h2, saved, w, arrived


def _layer_backward(dh2, p_l, cosf, sins, w, sm, sv, comm, scatter_own):
    s_len = dh2.shape[0]
    tm = _tile(s_len, 512)
    tr = _tile(s_len, 256)
    big, small = {}, {}
    dh1, small["ple_norm"], dgp, dpp = _ple_backward(dh2, sv["gate"], sv["pp"], w["w_ple_gate"], sv["h1"], sm["ple_norm"], tm)
    big["w_ple_proj"], _ = _matmul_tn(p_l, dpp, _tile(s_len, 2048), PLE_DIM, 1024, "grad_w_ple_proj")
    big["w_ple_gate"], _ = _matmul_tn(sv["n1"], dgp, _tile(s_len, 2048), 1024, 1024, "grad_w_ple_gate")
    dy, _ = _matmul_nt(dh1, w["w_out"], tm, D_MODEL, "grad_branches")
    big["w_out"], _ = _matmul_tn(sv["y"], dh1, _tile(s_len, 2048), 1024, 1024, "grad_w_out")
    ga, gb, gc = sm["out_norm"][:, 0:512], sm["out_norm"][:, 512:1024], sm["out_norm"][:, 1024:2048]
    dproj, dcw, small["conv_b"], dgb = _conv_backward(sv["proj"], w["conv_w"], sm["conv_b"], gb, dy, tr)
    big["conv_w"] = dcw
    dproj, do, dsum, dgc = _cgate_backward(sv["o"], sv["proj"], gc, dy, dproj, tm, _tile(s_len, ATT_BWD_QUERIES))
    dqt, dk, dv, arrived = _attention_backward(sv["q"], sv["k"], sv["kt"], sv["v"], do, sv["lse"], dsum, comm)
    (dproj, big["w_ukv"], small["kv_norm"], small["q_nope_norm"], small["q_rope_norm"], small["k_nope_norm"],
     small["k_rope_norm"]) = _mla_backward(sv["proj"], cosf, sins, w["w_ukv"], sm["kv_norm"], sm["q_nope_norm"],
                                            sm["q_rope_norm"], sm["k_nope_norm"], sm["k_rope_norm"], dqt, dk, dv, dproj, tr)
    dproj, small["sgu_norm"], small["w_spatial"], small["b_spatial"], dga = _sgu_backward(
        sv["proj"], sm["sgu_norm"], sm["w_spatial"], sm["b_spatial"], ga, dy, dproj, tm)
    small["out_norm"] = jnp.concatenate([dga, dgb, dgc], axis=1)
    parts_rest = _parts_rest(big)
    g_in, arrived_rest = _matmul_tn(sv["hn"], dproj, _tile(s_len, 2048), 512, 2304, "grad_w_in",
                                    (parts_rest, [False] * len(parts_rest)) if scatter_own else None)
    parts = [_part_w_in(g_in)] + parts_rest
    dhn, arrived_in = _matmul_nt(dproj, w["w_in"], _tile(s_len, 1024), 256, "grad_attn_norm_in",
                                 (parts[:1], [False]) if scatter_own else None)
    dh, small["attn_norm"] = _rms_backward(sv["h"], sm["attn_norm"], dhn, dh1, tr, "attn_norm_backward")
    return dh, parts, big["conv_w"], small, arrived, (*arrived_in, *arrived_rest) if scatter_own else None


def _layer_small(params, layer):
    return dict(
        attn_norm=params["attn_norm"][layer][None, :], sgu_norm=params["sgu_norm"][layer],
        w_spatial=params["w_spatial"][layer], b_spatial=params["b_spatial"][layer][:, :, None],
        conv_b=params["conv_b"][layer][None, :], kv_norm=params["kv_norm"][layer][None, :],
        q_nope_norm=params["q_nope_norm"][layer][None, :], q_rope_norm=params["q_rope_norm"][layer][None, :],
        k_nope_norm=params["k_nope_norm"][layer][None, :], k_rope_norm=params["k_rope_norm"][layer][None, :],
        out_norm=params["out_norm"][layer][None, :], ple_norm=params["ple_norm"][layer][None, :])


BIG = ("w_in", "w_ukv", "w_out", "w_ple_gate", "w_ple_proj")


def _assemble_w_in(g_in):
    return _permute_in(g_in.transpose(1, 0, 2).reshape(g_in.shape[1], IN_WIDTH))


def _assemble_rest(gathered):
    g_ukv, g_out, g_gate, g_proj = gathered
    w_ukv = g_ukv.reshape(N_DEV, KV_RANK, 2, HEAD).transpose(1, 2, 0, 3).reshape(KV_RANK, 2 * C_HEADS * HEAD)
    return dict(w_ukv=w_ukv, w_out=g_out.reshape(D_MODEL, D_MODEL), w_ple_gate=g_gate.reshape(D_MODEL, D_MODEL),
                w_ple_proj=g_proj.transpose(1, 0, 2).reshape(PLE_DIM, D_MODEL))


def _part_w_in(g_in):
    return _unpermute_in(g_in).reshape(g_in.shape[0], N_DEV, -1).transpose(1, 2, 0).astype(BF16)


def _parts_rest(big):
    return [
        big["w_ukv"].reshape(KV_RANK, 2, N_DEV, HEAD).transpose(2, 0, 1, 3).reshape(N_DEV, KV_RANK, 2 * HEAD).astype(BF16),
        big["w_out"].reshape(N_DEV, -1, D_MODEL).astype(BF16),
        big["w_ple_gate"].reshape(N_DEV, -1, D_MODEL).astype(BF16),
        big["w_ple_proj"].reshape(PLE_DIM, N_DEV, -1).transpose(1, 0, 2).astype(BF16)]


def _step_local(xs, ps, pos, target, shards, conv_w, params):
    inv = 1.0 / (ROPE_BASE ** (jnp.arange(0, ROPE, 2, dtype=F32) / ROPE))
    ang = pos.astype(F32)[:, None] * inv
    cos, sin = jnp.cos(ang), jnp.sin(ang)
    cosf = jnp.concatenate([cos, cos], axis=-1)
    sins = jnp.concatenate([-sin, sin], axis=-1)

    def gather_of(names, layer):
        return [shards[n][layer] for n in names], [True] * len(names)

    h = xs
    saved, weights = [], []
    smalls = [_layer_small(params, layer) for layer in range(DEPTH)]
    first_w_in = _gather_via_sibling(shards["w_in"][0], "gather_first_w_in")
    w = dict(w_in=_assemble_w_in(first_w_in), conv_w=conv_w[0])
    for layer in range(DEPTH):
        comm = (*gather_of(BIG, layer + 1), True) if layer + 1 < DEPTH else None
        comm_rest = gather_of(BIG[1:], 0) if layer == 0 else None
        h, sv, w, arrived = _layer_forward(h, ps[layer], cosf, sins, w, smalls[layer], comm, comm_rest)
        saved.append(sv)
        weights.append(w)
        if comm is not None:
            w = dict(w_in=_assemble_w_in(arrived[0]), conv_w=conv_w[layer + 1], **_assemble_rest(arrived[1:]))
    dh, loss = _loss_grad(h, target, _tile(h.shape[0], 512))
    received, conv_grads, small_grads = [None] * DEPTH, [None] * DEPTH, [None] * DEPTH
    comm = None
    for layer in reversed(range(DEPTH)):
        dh, parts, conv_grads[layer], small_grads[layer], arrived, arrived_own = _layer_backward(
            dh, ps[layer], cosf, sins, weights[layer], smalls[layer], saved[layer], comm, layer == 0)
        if comm is not None:
            received[layer + 1] = arrived
        comm = (parts, [False] * len(parts))
    received[0] = arrived_own
    return loss, dh, received, conv_grads, small_grads


def kernel(x, p, positions, attn_norm, w_in, sgu_norm, w_spatial, b_spatial, conv_w, conv_b, kv_norm, w_ukv, q_nope_norm, q_rope_norm, k_nope_norm, k_rope_norm, out_norm, w_out, ple_norm, w_ple_gate, w_ple_proj, loss_target, m_attn_norm, m_w_in, m_sgu_norm, m_w_spatial, m_b_spatial, m_conv_w, m_conv_b, m_kv_norm, m_w_ukv, m_q_nope_norm, m_q_rope_norm, m_k_nope_norm, m_k_rope_norm, m_out_norm, m_w_out, m_ple_norm, m_w_ple_gate, m_w_ple_proj, v_attn_norm, v_w_in, v_sgu_norm, v_w_spatial, v_b_spatial, v_conv_w, v_conv_b, v_kv_norm, v_w_ukv, v_q_nope_norm, v_q_rope_norm, v_k_nope_norm, v_k_rope_norm, v_out_norm, v_w_out, v_ple_norm, v_w_ple_gate, v_w_ple_proj):
    order = ("attn_norm", "w_in", "sgu_norm", "w_spatial", "b_spatial", "conv_w", "conv_b", "kv_norm", "w_ukv",
             "q_nope_norm", "q_rope_norm", "k_nope_norm", "k_rope_norm", "out_norm", "w_out", "ple_norm", "w_ple_gate",
             "w_ple_proj")
    wts = dict(zip(order, (attn_norm, w_in, sgu_norm, w_spatial, b_spatial, conv_w, conv_b, kv_norm, w_ukv, q_nope_norm,
                           q_rope_norm, k_nope_norm, k_rope_norm, out_norm, w_out, ple_norm, w_ple_gate, w_ple_proj)))
    mom = dict(zip(order, (m_attn_norm, m_w_in, m_sgu_norm, m_w_spatial, m_b_spatial, m_conv_w, m_conv_b, m_kv_norm, m_w_ukv,
                           m_q_nope_norm, m_q_rope_norm, m_k_nope_norm, m_k_rope_norm, m_out_norm, m_w_out, m_ple_norm,
                           m_w_ple_gate, m_w_ple_proj)))
    var = dict(zip(order, (v_attn_norm, v_w_in, v_sgu_norm, v_w_spatial, v_b_spatial, v_conv_w, v_conv_b, v_kv_norm, v_w_ukv,
                           v_q_nope_norm, v_q_rope_norm, v_k_nope_norm, v_k_rope_norm, v_out_norm, v_w_out, v_ple_norm,
                           v_w_ple_gate, v_w_ple_proj)))

    conv_shard = wts["conv_w"]
    (conv_all,) = _exchange([conv_shard.reshape(-1, 128)], [True], "gather_conv_w")
    conv_full = conv_all.reshape(N_DEV, DEPTH, 3, -1).transpose(1, 2, 0, 3).reshape(DEPTH, 3, -1)
    shards = {n: wts[n].astype(BF16) for n in BIG}
    loss_part, grad_x, received, conv_grads, small_grads = _step_local(
        x[0], p[:, 0], positions[0], loss_target[0], shards, conv_full, wts)
    loss = lax.psum(loss_part[0, 0], ("x", "y", "c"))

    def small_grad(name):
        g = jnp.stack([sg[name] for sg in small_grads])
        return g.reshape(wts[name].shape)

    conv_grad = jnp.stack(conv_grads)
    like = [wts[n] for n in SMALL] + [conv_grad]
    packed = _pack([small_grad(n) for n in SMALL] + [conv_grad])
    small_parts = _gather_via_sibling(packed, "gather_small_grads")
    filler = [jnp.zeros_like(conv_grad), jnp.zeros_like(conv_grad), jnp.ones_like(conv_grad)]
    small_out = _sum_adam(small_parts, *(_pack([src[n] for n in SMALL] + [fill])[None] for src, fill in zip((wts, mom, var), filler)),
                          0, None, "adam_small")
    unpacked = [_unpack(o[0], like) for o in small_out]
    results = {n: vals for n, vals in zip(SMALL, zip(*[u[:-1] for u in unpacked]))}
    me = 4 * lax.axis_index("x") + 2 * lax.axis_index("y") + lax.axis_index("c")
    width = conv_shard.shape[2]
    conv_local = lax.dynamic_slice_in_dim(unpacked[0][-1], me * width, width, axis=2)
    as_slab = (lambda t: t.reshape(1, -1, width))
    conv_out = _sum_adam(as_slab(conv_local), as_slab(conv_shard), as_slab(mom["conv_w"]), as_slab(var["conv_w"]), 0, None,
                         "adam_conv_w")
    results["conv_w"] = tuple(o.reshape(conv_shard.shape) for o in conv_out)

    for j, name in enumerate(BIG):
        view = (lambda t: jnp.swapaxes(t, 1, 2)) if name == "w_in" else (lambda t: t)
        outs = None
        for layer in range(DEPTH):
            outs = _sum_adam(received[layer][j], view(wts[name]), view(mom[name]), view(var[name]), layer, outs, "adam_" + name)
        results[name] = tuple(view(o) for o in outs)

    grads, deltas, new_m, new_v = ([results[n][j] for n in order] for j in range(4))
    return (loss, grad_x[None], *grads, *deltas, *new_m, *new_v)
```

```python
import functools

import jax
import jax.numpy as jnp
from jax import lax
from jax.experimental import pallas as pl
from jax.experimental.pallas import tpu as pltpu

F32 = jnp.float32
BF16 = jnp.bfloat16

N_DEV = 8
DEPTH = 4
D_MODEL = 2048
EPS = 1e-6
CHUNK = 128
A_HEADS = 4
HEAD = 128
ROPE = 64
HALF = ROPE // 2
C_HEADS = 8
KV_RANK = 512
PLE_DIM = 256
ROPE_BASE = 10000.0
IN_WIDTH = 6720
QK = HEAD + ROPE
SCALE = QK ** -0.5
LOG2_E = 1.4426950408889634
LN_2 = 0.6931471805599453
Q_SCALE = SCALE * LOG2_E
HALO = 8

O_B = 0
W_B = 2048
O_CZ = 2048
W_CZ = 1024
O_A = 3072
W_A = 1536
O_M = 4608
W_M = 2304
M_QN, M_QR, M_CKV, M_KR = 0, 1024, 1536, 2048
PROJ_W = 6912

ADAM_LR = 0.001
ADAM_B1 = 0.9
ADAM_B2 = 0.999
ADAM_EPS = 1e-08
ADAM_WD = 0.01
ADAM_STEP = 10

VMEM_LIMIT = 56 * 1024 * 1024

_NT = (((1,), (1,)), ((), ()))
_TN = (((0,), (0,)), ((), ()))


def _params(*sem):
    return pltpu.CompilerParams(dimension_semantics=sem, vmem_limit_bytes=VMEM_LIMIT)


@jax.custom_vjp
def _bdot(a, b):
    return jnp.dot(a.astype(BF16), b.astype(BF16), preferred_element_type=F32)


def _bdot_fwd(a, b):
    return _bdot(a, b), (a, b)


def _bdot_bwd(res, g):
    a, b = res
    gb = g.astype(BF16)
    da = lax.dot_general(gb, b.astype(BF16), _NT, preferred_element_type=F32)
    db = lax.dot_general(a.astype(BF16), gb, _TN, preferred_element_type=F32)
    return da.astype(a.dtype), db.astype(b.dtype)


_bdot.defvjp(_bdot_fwd, _bdot_bwd)


@functools.partial(jax.custom_vjp, nondiff_argnums=(1,))
def _split(x, n):
    w = x.shape[-1] // n
    return tuple(x[:, i * w:(i + 1) * w] for i in range(n))


def _split_fwd(x, n):
    return _split(x, n), None


def _split_bwd(n, _, gs):
    return (jnp.concatenate(gs, axis=-1),)


_split.defvjp(_split_fwd, _split_bwd)


@functools.partial(jax.custom_vjp, nondiff_argnums=(1,))
def _shift_rows(x, k):
    return pltpu.roll(x, k % x.shape[0], 0)


def _shift_rows_fwd(x, k):
    return _shift_rows(x, k), None


def _shift_rows_bwd(k, _, g):
    return (_shift_rows(g, -k),)


_shift_rows.defvjp(_shift_rows_fwd, _shift_rows_bwd)


@jax.custom_vjp
def _swap_halves(x):
    h = x.shape[-1] // 2
    return jnp.concatenate([x[:, h:], x[:, :h]], axis=-1)


def _swap_halves_fwd(x):
    return _swap_halves(x), None


def _swap_halves_bwd(_, g):
    return (_swap_halves(g),)


_swap_halves.defvjp(_swap_halves_fwd, _swap_halves_bwd)


def _rms(x, g):
    return x * lax.rsqrt(jnp.mean(x * x, axis=-1, keepdims=True) + EPS) * g


def _rope(x, cosf, sins):
    return x * cosf + _swap_halves(x) * sins


def _sgu_chunk(u, v, z, gain, ws, bs, ga):
    ys = []
    for h in range(A_HEADS):
        vn = _rms(v[h], gain[h])
        s = _bdot(ws[h], vn) + bs[h]
        ys.append(u[h] * s * jax.nn.silu(z[h]))
    ss = sum(jnp.sum(y * y, axis=-1, keepdims=True) for y in ys) * (1.0 / (A_HEADS * HEAD))
    r = lax.rsqrt(ss + EPS)
    return tuple(ys[h] * r * ga[h] for h in range(A_HEADS))


def _conv_tile(bb, bc, bh, bz, w0, w1, w2, cb, gb, w0h, w1h, w2h, cbh, gbh, valid, core):
    t = jnp.where(valid, bc * bh, 0.0)
    y = (jnp.where(core, cb, cbh)
         + _shift_rows(t, 1) * jnp.where(core, w0, w0h)
         + t * jnp.where(core, w1, w1h)
         + _shift_rows(t, -1) * jnp.where(core, w2, w2h))
    return _rms(bb * y * jax.nn.silu(bz), jnp.where(core, gb, gbh))


def _cgate_tile(o, cz, gc):
    return _rms(o * jax.nn.silu(cz), gc)


def _mla_tile(qn, qr, ckv, kr, cosf, sins, wukv, kvg, qng, qrg, kng, krg):
    kv = _split(_bdot(_rms(ckv, kvg), wukv), 2 * C_HEADS)
    k_r = _rope(_rms(kr, krg), cosf, sins)
    qn_h = _split(qn, C_HEADS)
    qr_h = _split(qr, C_HEADS)
    q, k, v = [], [], []
    for h in range(C_HEADS):
        q.append(jnp.concatenate([_rms(qn_h[h], qng), _rope(_rms(qr_h[h], qrg), cosf, sins)], axis=-1) * Q_SCALE)
        k.append(jnp.concatenate([_rms(kv[h], kng), k_r], axis=-1))
        v.append(kv[C_HEADS + h])
    return tuple(q), tuple(k), tuple(v)


def _norm_matmul(h, gain, w, tm, tn, comm=None):
    s_len, k = h.shape
    n = w.shape[1]
    c_ops, c_in_specs, c_shapes, c_out_specs, c_sems, c_begin, c_end = _riding_exchange(comm, 2)
    n_c = len(c_ops)

    def body(h_ref, g_ref, w_ref, *rest):
        c_ins, (o_ref, hn_ref), c_outs, sems = rest[:n_c], rest[n_c:n_c + 2], rest[n_c + 2:2 * n_c + 2], rest[2 * n_c + 2:]
        c_begin(c_ins, c_outs, sems)

        @pl.when(pl.program_id(1) == 0)
        def _():
            hn_ref[...] = _rms(h_ref[...], g_ref[...]).astype(BF16)

        o_ref[...] = jnp.dot(hn_ref[...], w_ref[...], preferred_element_type=F32).astype(BF16)
        c_end(c_ins, c_outs, sems)

    outs = pl.pallas_call(
        body, name="norm_matmul",
        out_shape=(jax.ShapeDtypeStruct((s_len, n), BF16), jax.ShapeDtypeStruct((s_len, k), BF16), *c_shapes),
        grid=(s_len // tm, n // tn),
        in_specs=[pl.BlockSpec((tm, k), lambda i, j: (i, 0)), pl.BlockSpec((1, k), lambda i, j: (0, 0)),
                  pl.BlockSpec((k, tn), lambda i, j: (0, j))] + c_in_specs,
        out_specs=(pl.BlockSpec((tm, tn), lambda i, j: (i, j)), pl.BlockSpec((tm, k), lambda i, j: (i, 0)), *c_out_specs),
        scratch_shapes=c_sems,
        compiler_params=_params("arbitrary", "arbitrary") if comm else _params("parallel", "arbitrary"),
    )(h, gain, w, *c_ops)
    return outs[0], outs[1], outs[2:]


def _out_matmul(h, y, w, tm, tn):
    s_len, n = h.shape
    k = y.shape[1]

    def body(h_ref, y_ref, w_ref, o_ref):
        o_ref[...] = h_ref[...] + jnp.dot(y_ref[...], w_ref[...], preferred_element_type=F32)

    return pl.pallas_call(
        body, name="out_matmul",
        out_shape=jax.ShapeDtypeStruct((s_len, n), F32),
        grid=(s_len // tm, n // tn),
        in_specs=[pl.BlockSpec((tm, tn), lambda i, j: (i, j)), pl.BlockSpec((tm, k), lambda i, j: (i, 0)),
                  pl.BlockSpec((k, tn), lambda i, j: (0, j))],
        out_specs=pl.BlockSpec((tm, tn), lambda i, j: (i, j)),
        compiler_params=_params("parallel", "parallel"),
    )(h, y, w)


def _ple_forward(h1, gain, p, wg, wp, tm, tn, target=None):
    s_len, d = h1.shape
    kp = p.shape[1]
    with_loss = target is not None

    def body(hrow_ref, g_ref, p_ref, wg_ref, wp_ref, *rest):
        t_ref = rest[0] if with_loss else None
        o_ref, n1_ref, gate_ref, pp_ref = rest[with_loss:with_loss + 4]
        j = pl.program_id(1)

        @pl.when(j == 0)
        def _():
            n1_ref[...] = _rms(hrow_ref[...], g_ref[...]).astype(BF16)

        gate = jax.nn.sigmoid(jnp.dot(n1_ref[...], wg_ref[...], preferred_element_type=F32))
        pp = jnp.dot(p_ref[...].astype(BF16), wp_ref[...], preferred_element_type=F32)
        h2 = hrow_ref[:, pl.ds(pl.multiple_of(j * tn, tn), tn)] + gate * pp
        gate_ref[...] = gate.astype(BF16)
        pp_ref[...] = pp.astype(BF16)
        if with_loss:
            e = h2 - t_ref[...]
            o_ref[...] = e * (1.0 / d)
            part = jnp.sum(jnp.sum(e * e, axis=-1, keepdims=True), axis=0, keepdims=True) * (0.5 / d)
            _acc(rest[-1], jnp.broadcast_to(part, (1, 128)), jnp.logical_and(pl.program_id(0) == 0, j == 0))
        else:
            o_ref[...] = h2

    col = pl.BlockSpec((tm, tn), lambda i, j: (i, j))
    return pl.pallas_call(
        body, name="ple_forward",
        out_shape=(jax.ShapeDtypeStruct((s_len, d), F32), jax.ShapeDtypeStruct((s_len, d), BF16),
                   jax.ShapeDtypeStruct((s_len, d), BF16), jax.ShapeDtypeStruct((s_len, d), BF16))
                  + ((jax.ShapeDtypeStruct((1, 128), F32),) if with_loss else ()),
        grid=(s_len // tm, d // tn),
        in_specs=[pl.BlockSpec((tm, d), lambda i, j: (i, 0)), pl.BlockSpec((1, d), lambda i, j: (0, 0)),
                  pl.BlockSpec((tm, kp), lambda i, j: (i, 0)), pl.BlockSpec((d, tn), lambda i, j: (0, j)),
                  pl.BlockSpec((kp, tn), lambda i, j: (0, j))] + ([col] if with_loss else []),
        out_specs=(col, pl.BlockSpec((tm, d), lambda i, j: (i, 0)), col, col)
                  + ((pl.BlockSpec((1, 128), lambda i, j: (0, 0)),) if with_loss else ()),
        compiler_params=_params("arbitrary", "arbitrary") if with_loss else _params("parallel", "arbitrary"),
    )(h1, gain, p, wg, wp, *((target,) if with_loss else ()))


def _matmul_nt(a, b, tm, tk, name, comm=None):
    m, n = a.shape
    k = b.shape[0]
    c_ops, c_in_specs, c_shapes, c_out_specs, c_sems, c_begin, c_end = _riding_exchange(comm, 2)
    n_c = len(c_ops)

    def body(a_ref, b_ref, *rest):
        c_ins, o_ref, c_outs, sems = rest[:n_c], rest[n_c], rest[n_c + 1:2 * n_c + 1], rest[2 * n_c + 1:]
        c_begin(c_ins, c_outs, sems)
        o_ref[...] = lax.dot_general(a_ref[...].astype(BF16), b_ref[...].astype(BF16), _NT, preferred_element_type=F32)
        c_end(c_ins, c_outs, sems)

    outs = pl.pallas_call(
        body, name=name,
        out_shape=(jax.ShapeDtypeStruct((m, k), F32), *c_shapes),
        grid=(m // tm, k // tk),
        in_specs=[pl.BlockSpec((tm, n), lambda i, j: (i, 0)), pl.BlockSpec((tk, n), lambda i, j: (j, 0))] + c_in_specs,
        out_specs=(pl.BlockSpec((tm, tk), lambda i, j: (i, j)), *c_out_specs),
        scratch_shapes=c_sems,
        compiler_params=_params("arbitrary", "arbitrary") if comm else _params("parallel", "parallel"),
    )(a, b, *c_ops)
    return outs[0], outs[1:]


def _matmul_tn(a, b, tm, tk, tn, name, comm=None):
    m, k = a.shape
    n = b.shape[1]
    n_m = m // tm
    c_ops, c_in_specs, c_shapes, c_out_specs, c_sems, c_begin, c_end = _riding_exchange(comm, 3)
    n_c = len(c_ops)

    def body(a_ref, b_ref, *rest):
        c_ins, o_ref, c_outs, acc_ref, sems = rest[:n_c], rest[n_c], rest[n_c + 1:2 * n_c + 1], rest[2 * n_c + 1], rest[2 * n_c + 2:]
        c_begin(c_ins, c_outs, sems)
        part = lax.dot_general(a_ref[...].astype(BF16), b_ref[...].astype(BF16), _TN, preferred_element_type=F32)
        _acc(acc_ref, part, pl.program_id(2) == 0)

        @pl.when(pl.program_id(2) == n_m - 1)
        def _():
            o_ref[...] = acc_ref[...].astype(BF16)

        c_end(c_ins, c_outs, sems)

    outs = pl.pallas_call(
        body, name=name,
        out_shape=(jax.ShapeDtypeStruct((k, n), BF16), *c_shapes),
        grid=(k // tk, n // tn, n_m),
        in_specs=[pl.BlockSpec((tm, tk), lambda kk, nn, mm: (mm, kk)), pl.BlockSpec((tm, tn), lambda kk, nn, mm: (mm, nn))]
                 + c_in_specs,
        out_specs=(pl.BlockSpec((tk, tn), lambda kk, nn, mm: (kk, nn)), *c_out_specs),
        scratch_shapes=[pltpu.VMEM((tk, tn), F32)] + c_sems,
        compiler_params=_params("arbitrary", "arbitrary", "arbitrary") if comm else _params("parallel", "parallel", "arbitrary"),
    )(a, b, *c_ops)
    return outs[0], outs[1:]


def _acc(ref, val, first):
    @pl.when(first)
    def _():
        ref[...] = val

    @pl.when(jnp.logical_not(first))
    def _():
        ref[...] += val


def _rms_backward(x, gain, dn, dres, tm, name):
    s_len, d = x.shape

    def body(x_ref, g_ref, dn_ref, dres_ref, dx_ref, dg_ref):
        _, vjp = jax.vjp(_rms, x_ref[...], g_ref[...])
        dx, dg = vjp(dn_ref[...])
        dx_ref[...] = dres_ref[...] + dx
        _acc(dg_ref, dg, pl.program_id(0) == 0)

    row = pl.BlockSpec((tm, d), lambda i: (i, 0))
    vec = pl.BlockSpec((1, d), lambda i: (0, 0))
    return pl.pallas_call(
        body, name=name,
        out_shape=(jax.ShapeDtypeStruct((s_len, d), F32), jax.ShapeDtypeStruct((1, d), F32)),
        grid=(s_len // tm,),
        in_specs=[row, vec, row, row],
        out_specs=(row, vec),
        compiler_params=_params("arbitrary"),
    )(x, gain, dn, dres)


def _ple_backward(dh2, gate, pp, wg, h1, gain, tm):
    s_len, d = dh2.shape

    def body(dh_ref, gate_ref, pp_ref, wg_ref, h1_ref, g_ref, dh1_ref, dgain_ref, dgp_ref, dpp_ref):
        dh = dh_ref[...]
        gate = gate_ref[...].astype(F32)
        dgp = (dh * pp_ref[...].astype(F32) * gate * (1.0 - gate)).astype(BF16)
        dgp_ref[...] = dgp
        dpp_ref[...] = (dh * gate).astype(BF16)
        dn = lax.dot_general(dgp, wg_ref[...], _NT, preferred_element_type=F32)
        _, vjp = jax.vjp(_rms, h1_ref[...], g_ref[...])
        dx, dgain = vjp(dn)
        dh1_ref[...] = dh + dx
        _acc(dgain_ref, dgain, pl.program_id(0) == 0)

    row = pl.BlockSpec((tm, d), lambda i: (i, 0))
    vec = pl.BlockSpec((1, d), lambda i: (0, 0))
    return pl.pallas_call(
        body, name="ple_backward",
        out_shape=(jax.ShapeDtypeStruct((s_len, d), F32), jax.ShapeDtypeStruct((1, d), F32),
                   jax.ShapeDtypeStruct((s_len, d), BF16), jax.ShapeDtypeStruct((s_len, d), BF16)),
        grid=(s_len // tm,),
        in_specs=[row, row, row, pl.BlockSpec((d, d), lambda i: (0, 0)), row, vec],
        out_specs=(row, vec, row, row),
        compiler_params=_params("arbitrary"),
    )(dh2, gate, pp, wg, h1, gain)


def _sgu_in_specs(tm):
    return [pl.BlockSpec((tm, W_A), lambda i: (i, O_A // W_A)),
            pl.BlockSpec((A_HEADS, HEAD), lambda i: (0, 0)), pl.BlockSpec((A_HEADS, CHUNK, CHUNK), lambda i: (0, 0, 0)),
            pl.BlockSpec((A_HEADS, CHUNK, 1), lambda i: (0, 0, 0)), pl.BlockSpec((1, 512), lambda i: (0, 0))]


def _sgu_load(a_ref, gain_ref, ws_ref, bs_ref, ga_ref, c):
    rows = slice(c * CHUNK, (c + 1) * CHUNK)
    heads = range(A_HEADS)
    u = tuple(a_ref[rows, h * HEAD:(h + 1) * HEAD].astype(F32) for h in heads)
    v = tuple(a_ref[rows, 512 + h * HEAD:512 + (h + 1) * HEAD].astype(F32) for h in heads)
    z = tuple(a_ref[rows, 1024 + h * HEAD:1024 + (h + 1) * HEAD].astype(F32) for h in heads)
    gain = tuple(gain_ref[h:h + 1, :] for h in heads)
    ws = tuple(ws_ref[h] for h in heads)
    bs = tuple(bs_ref[h] for h in heads)
    ga = tuple(ga_ref[:, h * HEAD:(h + 1) * HEAD] for h in heads)
    return u, v, z, gain, ws, bs, ga


def _sgu_forward(proj, gain, ws, bs, ga, tm):
    s_len = proj.shape[0]

    def body(a_ref, gain_ref, ws_ref, bs_ref, ga_ref, o_ref):
        for c in range(tm // CHUNK):
            out = _sgu_chunk(*_sgu_load(a_ref, gain_ref, ws_ref, bs_ref, ga_ref, c))
            for h in range(A_HEADS):
                o_ref[c * CHUNK:(c + 1) * CHUNK, h * HEAD:(h + 1) * HEAD] = out[h].astype(BF16)

    return pl.pallas_call(
        body, name="sgu_forward",
        out_shape=jax.ShapeDtypeStruct((s_len, D_MODEL), BF16),
        grid=(s_len // tm,),
        in_specs=_sgu_in_specs(tm),
        out_specs=pl.BlockSpec((tm, 512), lambda i: (i, 0)),
        compiler_params=_params("parallel"),
    )(proj, gain, ws, bs, ga)


def _sgu_backward(proj, gain, ws, bs, ga, dy, dproj, tm):
    s_len = proj.shape[0]

    def body(a_ref, gain_ref, ws_ref, bs_ref, ga_ref, dy_ref, _, da_ref, dgain_ref, dws_ref, dbs_ref, dga_ref):
        tot = None
        for c in range(tm // CHUNK):
            args = _sgu_load(a_ref, gain_ref, ws_ref, bs_ref, ga_ref, c)
            _, vjp = jax.vjp(_sgu_chunk, *args)
            rows = slice(c * CHUNK, (c + 1) * CHUNK)
            du, dv, dz, dgain, dws, dbs, dga = vjp(tuple(dy_ref[rows, h * HEAD:(h + 1) * HEAD] for h in range(A_HEADS)))
            for h in range(A_HEADS):
                da_ref[rows, h * HEAD:(h + 1) * HEAD] = du[h].astype(BF16)
                da_ref[rows, 512 + h * HEAD:512 + (h + 1) * HEAD] = dv[h].astype(BF16)
                da_ref[rows, 1024 + h * HEAD:1024 + (h + 1) * HEAD] = dz[h].astype(BF16)
            part = (dgain, dws, dbs, dga)
            tot = part if tot is None else jax.tree.map(jnp.add, tot, part)
        dgain, dws, dbs, dga = tot
        first = pl.program_id(0) == 0
        _acc(dgain_ref, jnp.concatenate(dgain, axis=0), first)
        _acc(dga_ref, jnp.concatenate(dga, axis=-1), first)
        for h in range(A_HEADS):
            _acc(dws_ref.at[h], dws[h], first)
            _acc(dbs_ref.at[h], dbs[h], first)

    small = [pl.BlockSpec((A_HEADS, HEAD), lambda i: (0, 0)), pl.BlockSpec((A_HEADS, CHUNK, CHUNK), lambda i: (0, 0, 0)),
             pl.BlockSpec((A_HEADS, CHUNK, 1), lambda i: (0, 0, 0)), pl.BlockSpec((1, 512), lambda i: (0, 0))]
    return pl.pallas_call(
        body, name="sgu_backward",
        out_shape=(jax.ShapeDtypeStruct(dproj.shape, BF16),
                   jax.ShapeDtypeStruct((A_HEADS, HEAD), F32), jax.ShapeDtypeStruct((A_HEADS, CHUNK, CHUNK), F32),
                   jax.ShapeDtypeStruct((A_HEADS, CHUNK, 1), F32), jax.ShapeDtypeStruct((1, 512), F32)),
        grid=(s_len // tm,),
        in_specs=_sgu_in_specs(tm) + [pl.BlockSpec((tm, 512), lambda i: (i, 0)), pl.BlockSpec(memory_space=pl.ANY)],
        out_specs=(pl.BlockSpec((tm, W_A), lambda i: (i, O_A // W_A)), *small),
        input_output_aliases={6: 0},
        compiler_params=_params("arbitrary"),
    )(proj, gain, ws, bs, ga, dy, dproj)


def _halo_specs(tm, width, col, n_rows):
    per = tm // HALO
    last = n_rows // HALO - 1
    return [pl.BlockSpec((HALO, width), lambda i: (jnp.maximum(i * per - 1, 0), col)),
            pl.BlockSpec((tm, width), lambda i: (i, col)),
            pl.BlockSpec((HALO, width), lambda i: (jnp.minimum((i + 1) * per, last), col))]


def _conv_masks(tm, s_len):
    r = lax.broadcasted_iota(jnp.int32, (tm + 2 * HALO, 1), 0)
    g = pl.program_id(0) * tm - HALO + r
    return (g >= 0) & (g < s_len), (r >= HALO) & (r < HALO + tm)


def _conv_inputs(b_refs, cw_ref, cb_ref, gb_ref):
    ext = jnp.concatenate([r[...] for r in b_refs], axis=0).astype(F32)
    bb, bc, bh, bz = (ext[:, j * 512:(j + 1) * 512] for j in range(4))
    prm = (cw_ref[0:1, :], cw_ref[1:2, :], cw_ref[2:3, :], cb_ref[...], gb_ref[...])
    return (bb, bc, bh, bz), prm


def _conv_forward(proj, cw, cb, gb, y, tm):
    s_len = proj.shape[0]

    def body(p0, p1, p2, cw_ref, cb_ref, gb_ref, _, o_ref):
        acts, prm = _conv_inputs((p0, p1, p2), cw_ref, cb_ref, gb_ref)
        valid, core = _conv_masks(tm, s_len)
        out = _conv_tile(*acts, *prm, *prm, valid, core)
        o_ref[...] = out[HALO:HALO + tm].astype(BF16)

    vec = pl.BlockSpec((1, 512), lambda i: (0, 0))
    return pl.pallas_call(
        body, name="conv_forward",
        out_shape=jax.ShapeDtypeStruct(y.shape, BF16),
        grid=(s_len // tm,),
        in_specs=_halo_specs(tm, W_B, O_B // W_B, s_len) + [pl.BlockSpec((3, 512), lambda i: (0, 0)), vec, vec,
                                                             pl.BlockSpec(memory_space=pl.ANY)],
        out_specs=pl.BlockSpec((tm, 512), lambda i: (i, 1)),
        input_output_aliases={6: 0},
        compiler_params=_params("parallel"),
    )(proj, proj, proj, cw, cb, gb, y)


def _conv_backward(proj, cw, cb, gb, dy, tm):
    s_len = proj.shape[0]

    def body(p0, p1, p2, cw_ref, cb_ref, gb_ref, d0, d1, d2, db_ref, dcw_ref, dcb_ref, dgb_ref):
        acts, prm = _conv_inputs((p0, p1, p2), cw_ref, cb_ref, gb_ref)
        valid, core = _conv_masks(tm, s_len)
        _, vjp = jax.vjp(lambda a, p: _conv_tile(*a, *p, *prm, valid, core), acts, prm)
        dy_ext = jnp.where(valid, jnp.concatenate([d0[...], d1[...], d2[...]], axis=0), 0.0)
        dacts, dprm = vjp(dy_ext)
        for j in range(4):
            db_ref[:, j * 512:(j + 1) * 512] = dacts[j][HALO:HALO + tm].astype(BF16)
        first = pl.program_id(0) == 0
        _acc(dcw_ref, jnp.concatenate(dprm[0:3], axis=0), first)
        _acc(dcb_ref, dprm[3], first)
        _acc(dgb_ref, dprm[4], first)

    vec = pl.BlockSpec((1, 512), lambda i: (0, 0))
    mat = pl.BlockSpec((3, 512), lambda i: (0, 0))
    return pl.pallas_call(
        body, name="conv_backward",
        out_shape=(jax.ShapeDtypeStruct((s_len, PROJ_W), BF16), jax.ShapeDtypeStruct((3, 512), F32),
                   jax.ShapeDtypeStruct((1, 512), F32), jax.ShapeDtypeStruct((1, 512), F32)),
        grid=(s_len // tm,),
        in_specs=_halo_specs(tm, W_B, O_B // W_B, s_len) + [mat, vec, vec] + _halo_specs(tm, 512, 1, s_len),
        out_specs=(pl.BlockSpec((tm, W_B), lambda i: (i, O_B // W_B)), mat, vec, vec),
        compiler_params=_params("arbitrary"),
    )(proj, proj, proj, cw, cb, gb, dy, dy, dy)


def _cgate_forward(o, proj, gc, y, tm):
    s_len = o.shape[0]

    def body(o_ref, cz_ref, gc_ref, _, y_ref):
        y_ref[...] = _cgate_tile(o_ref[...], cz_ref[...].astype(F32), gc_ref[...]).astype(BF16)

    return pl.pallas_call(
        body, name="cgate_forward",
        out_shape=jax.ShapeDtypeStruct(y.shape, BF16),
        grid=(s_len // tm,),
        in_specs=[pl.BlockSpec((tm, W_CZ), lambda i: (i, 0)), pl.BlockSpec((tm, W_CZ), lambda i: (i, O_CZ // W_CZ)),
                  pl.BlockSpec((1, W_CZ), lambda i: (0, 0)), pl.BlockSpec(memory_space=pl.ANY)],
        out_specs=pl.BlockSpec((tm, W_CZ), lambda i: (i, 1)),
        input_output_aliases={3: 0},
        compiler_params=_params("parallel"),
    )(o, proj, gc, y)


def _cgate_backward(o, proj, gc, dy, dproj, tm, stat_chunk):
    s_len = o.shape[0]
    per_stat = stat_chunk // tm

    def body(o_ref, cz_ref, gc_ref, dy_ref, _, dcz_ref, do_ref, dsum_ref, dgc_ref):
        o = o_ref[...]
        _, vjp = jax.vjp(_cgate_tile, o, cz_ref[...].astype(F32), gc_ref[...])
        do, dcz, dgc = vjp(dy_ref[...])
        dcz_ref[...] = dcz.astype(BF16)
        do_ref[...] = do.astype(BF16)
        ones = jnp.ones((8, HEAD), F32)
        for h in range(C_HEADS):
            cols = slice(h * HEAD, (h + 1) * HEAD)
            sums = lax.dot_general(ones, do[:, cols] * o[:, cols], _NT, precision=lax.Precision.HIGHEST,
                                   preferred_element_type=F32)
            dsum_ref[h, 0] = sums[0:1]
        _acc(dgc_ref, dgc, pl.program_id(0) == 0)

    row = pl.BlockSpec((tm, W_CZ), lambda i: (i, 0))
    vec = pl.BlockSpec((1, W_CZ), lambda i: (0, 0))
    return pl.pallas_call(
        body, name="cgate_backward",
        out_shape=(jax.ShapeDtypeStruct(dproj.shape, BF16), jax.ShapeDtypeStruct((s_len, W_CZ), BF16),
                   jax.ShapeDtypeStruct((C_HEADS, s_len // stat_chunk, 1, stat_chunk), F32), jax.ShapeDtypeStruct((1, W_CZ), F32)),
        grid=(s_len // tm,),
        in_specs=[row, pl.BlockSpec((tm, W_CZ), lambda i: (i, O_CZ // W_CZ)), vec,
                  pl.BlockSpec((tm, W_CZ), lambda i: (i, 1)), pl.BlockSpec(memory_space=pl.ANY)],
        out_specs=(pl.BlockSpec((tm, W_CZ), lambda i: (i, O_CZ // W_CZ)), row,
                   pl.BlockSpec((C_HEADS, 1, 1, tm), lambda i: (0, i // per_stat, 0, i % per_stat)), vec),
        input_output_aliases={4: 0},
        compiler_params=_params("arbitrary"),
    )(o, proj, gc, dy, dproj)


def _mla_small_specs():
    return [pl.BlockSpec((KV_RANK, 2 * C_HEADS * HEAD), lambda i: (0, 0)), pl.BlockSpec((1, KV_RANK), lambda i: (0, 0)),
            pl.BlockSpec((1, HEAD), lambda i: (0, 0)), pl.BlockSpec((1, ROPE), lambda i: (0, 0)),
            pl.BlockSpec((1, HEAD), lambda i: (0, 0)), pl.BlockSpec((1, ROPE), lambda i: (0, 0))]


def _mla_load(m_ref, cos_ref, sin_ref):
    qn = m_ref[:, M_QN:M_QN + C_HEADS * HEAD].astype(F32)
    qr = m_ref[:, M_QR:M_QR + C_HEADS * ROPE].astype(F32)
    ckv = m_ref[:, M_CKV:M_CKV + KV_RANK].astype(F32)
    kr = m_ref[:, M_KR:M_KR + ROPE].astype(F32)
    return qn, qr, ckv, kr, cos_ref[...], sin_ref[...]


def _mla_forward(proj, cosf, sins, wukv, kvg, qng, qrg, kng, krg, tm, kt_chunk, vt_chunk):
    s_len = proj.shape[0]

    def body(m_ref, cos_ref, sin_ref, w_ref, kvg_ref, qng_ref, qrg_ref, kng_ref, krg_ref, q_ref, k_ref, v_ref, kt_ref, vt_ref):
        q, k, v = _mla_tile(*_mla_load(m_ref, cos_ref, sin_ref), w_ref[...], kvg_ref[...], qng_ref[...], qrg_ref[...],
                            kng_ref[...], krg_ref[...])
        for h in range(C_HEADS):
            q_ref[h] = q[h].astype(BF16)
            k_ref[h] = k[h].astype(BF16)
            v_ref[h] = v[h].astype(BF16)
            kt_ref[h, 0] = jnp.concatenate([k[h][:, :HEAD].T, k[h][:, HEAD:].T], axis=0).astype(BF16)
            vt_ref[h, 0] = v[h].T.astype(BF16)

    rope_spec = pl.BlockSpec((tm, ROPE), lambda i: (i, 0))
    qk_spec = pl.BlockSpec((C_HEADS, tm, QK), lambda i: (0, i, 0))
    per_k, per_v = kt_chunk // tm, vt_chunk // tm
    return pl.pallas_call(
        body, name="mla_forward",
        out_shape=(jax.ShapeDtypeStruct((C_HEADS, s_len, QK), BF16), jax.ShapeDtypeStruct((C_HEADS, s_len, QK), BF16),
                   jax.ShapeDtypeStruct((C_HEADS, s_len, HEAD), BF16),
                   jax.ShapeDtypeStruct((C_HEADS, s_len // kt_chunk, QK, kt_chunk), BF16),
                   jax.ShapeDtypeStruct((C_HEADS, s_len // vt_chunk, HEAD, vt_chunk), BF16)),
        grid=(s_len // tm,),
        in_specs=[pl.BlockSpec((tm, W_M), lambda i: (i, O_M // W_M)), rope_spec, rope_spec] + _mla_small_specs(),
        out_specs=(qk_spec, qk_spec, pl.BlockSpec((C_HEADS, tm, HEAD), lambda i: (0, i, 0)),
                   pl.BlockSpec((C_HEADS, 1, QK, tm), lambda i: (0, i // per_k, 0, i % per_k)),
                   pl.BlockSpec((C_HEADS, 1, HEAD, tm), lambda i: (0, i // per_v, 0, i % per_v))),
        compiler_params=_params("parallel"),
    )(proj, cosf, sins, wukv, kvg, qng, qrg, kng, krg)


def _mla_backward(proj, cosf, sins, wukv, kvg, qng, qrg, kng, krg, dqt, dk, dv, dproj, tm):
    s_len = proj.shape[0]
    per_chunk = dqt.shape[3] // tm

    def body(m_ref, cos_ref, sin_ref, w_ref, kvg_ref, qng_ref, qrg_ref, kng_ref, krg_ref, dq_ref, dk_ref, dv_ref, _,
             dm_ref, dw_ref, dkvg_ref, dqng_ref, dqrg_ref, dkng_ref, dkrg_ref):
        qn, qr, ckv, kr, cosf_t, sins_t = _mla_load(m_ref, cos_ref, sin_ref)
        prm = (w_ref[...], kvg_ref[...], qng_ref[...], qrg_ref[...], kng_ref[...], krg_ref[...])
        _, vjp = jax.vjp(lambda a, p: _mla_tile(*a, cosf_t, sins_t, *p), (qn, qr, ckv, kr), prm)
        heads = range(C_HEADS)
        dacts, dprm = vjp((tuple(dq_ref[h, 0].T for h in heads), tuple(dk_ref[h] for h in heads), tuple(dv_ref[h] for h in heads)))
        dm_ref[:, M_QN:M_QN + C_HEADS * HEAD] = dacts[0].astype(BF16)
        dm_ref[:, M_QR:M_QR + C_HEADS * ROPE] = dacts[1].astype(BF16)
        dm_ref[:, M_CKV:M_CKV + KV_RANK] = dacts[2].astype(BF16)
        pad = jnp.zeros((tm, W_M - M_KR - ROPE), F32)
        dm_ref[:, M_KR:W_M] = jnp.concatenate([dacts[3], pad], axis=-1).astype(BF16)
        first = pl.program_id(0) == 0
        for ref, val in zip((dw_ref, dkvg_ref, dqng_ref, dqrg_ref, dkng_ref, dkrg_ref), dprm):
            _acc(ref, val.astype(F32), first)

    rope_spec = pl.BlockSpec((tm, ROPE), lambda i: (i, 0))
    qk_spec = pl.BlockSpec((C_HEADS, tm, QK), lambda i: (0, i, 0))
    small = _mla_small_specs()
    return pl.pallas_call(
        body, name="mla_backward",
        out_shape=(jax.ShapeDtypeStruct(dproj.shape, BF16), jax.ShapeDtypeStruct((KV_RANK, 2 * C_HEADS * HEAD), F32),
                   jax.ShapeDtypeStruct((1, KV_RANK), F32), jax.ShapeDtypeStruct((1, HEAD), F32),
                   jax.ShapeDtypeStruct((1, ROPE), F32), jax.ShapeDtypeStruct((1, HEAD), F32),
                   jax.ShapeDtypeStruct((1, ROPE), F32)),
        grid=(s_len // tm,),
        in_specs=[pl.BlockSpec((tm, W_M), lambda i: (i, O_M // W_M)), rope_spec, rope_spec] + small
                 + [pl.BlockSpec((C_HEADS, 1, QK, tm), lambda i: (0, i // per_chunk, 0, i % per_chunk)), qk_spec,
                    pl.BlockSpec((C_HEADS, tm, HEAD), lambda i: (0, i, 0)), pl.BlockSpec(memory_space=pl.ANY)],
        out_specs=(pl.BlockSpec((tm, W_M), lambda i: (i, O_M // W_M)), *small),
        input_output_aliases={12: 0},
        compiler_params=_params("arbitrary"),
    )(proj, cosf, sins, wukv, kvg, qng, qrg, kng, krg, dqt, dk, dv, dproj)


def _attention_forward(q, k, vt, tq, stat_chunk, comm=None):
    n_heads, s_len, _ = q.shape
    n_chunks, _, ck = vt.shape[1:]
    c_ops, c_in_specs, c_shapes, c_out_specs, c_sems, c_begin, c_end = _riding_exchange(comm, 2)
    n_c = len(c_ops)

    def body(q_ref, k_ref, vt_ref, *rest):
        c_ins, (o_ref, lse_ref), c_outs, sems = rest[:n_c], rest[n_c:n_c + 2], rest[n_c + 2:2 * n_c + 2], rest[2 * n_c + 2:]
        c_begin(c_ins, c_outs, sems)
        q_t = q_ref[0]

        def step(j, carry):
            m_old, l_old, acc = carry
            k_j = k_ref[0, pl.ds(pl.multiple_of(j * ck, ck), ck), :]
            s = lax.dot_general(k_j, q_t, _NT, preferred_element_type=F32)
            m_new = jnp.maximum(m_old, jnp.max(s, axis=0, keepdims=True))
            p = jnp.exp2(s - m_new)
            alpha = jnp.exp2(m_old - m_new)
            l_new = alpha * l_old + jnp.sum(p, axis=0, keepdims=True)
            acc = alpha * acc + jnp.dot(vt_ref[0, j], p.astype(BF16), preferred_element_type=F32)
            return m_new, l_new, acc

        init = (jnp.full((1, tq), -jnp.inf, F32), jnp.zeros((1, tq), F32), jnp.zeros((HEAD, tq), F32))
        m_fin, l_fin, acc = lax.fori_loop(0, n_chunks, step, init)
        o_ref[...] = (acc / l_fin).T
        lse_ref[0, 0] = m_fin + jnp.log2(l_fin)
        c_end(c_ins, c_outs, sems)

    per_stat = stat_chunk // tq
    outs = pl.pallas_call(
        body, name="attention_forward",
        out_shape=(jax.ShapeDtypeStruct((s_len, n_heads * HEAD), F32),
                   jax.ShapeDtypeStruct((n_heads, s_len // stat_chunk, 1, stat_chunk), F32), *c_shapes),
        grid=(n_heads, s_len // tq),
        in_specs=[pl.BlockSpec((1, tq, QK), lambda h, i: (h, i, 0)), pl.BlockSpec((1, s_len, QK), lambda h, i: (h, 0, 0)),
                  pl.BlockSpec((1, n_chunks, HEAD, ck), lambda h, i: (h, 0, 0, 0))] + c_in_specs,
        out_specs=(pl.BlockSpec((tq, HEAD), lambda h, i: (i, h)),
                   pl.BlockSpec((1, 1, 1, tq), lambda h, i: (h, i // per_stat, 0, i % per_stat)), *c_out_specs),
        scratch_shapes=c_sems,
        compiler_params=_params("arbitrary", "arbitrary") if comm else _params("parallel", "parallel"),
    )(q, k, vt, *c_ops)
    return outs[0], outs[1], outs[2:]


def _attention_backward(q, k, kt, v, do, lse, dsum, comm=None):
    n_heads, s_len, _ = q.shape
    tk = kt.shape[3]
    n_q, _, cq = lse.shape[1:]
    c_ops, c_in_specs, c_shapes, c_out_specs, c_sems, c_begin, c_end = _riding_exchange(comm, 2)
    n_c = len(c_ops)

    def body(q_ref, k_ref, kt_ref, v_ref, do_ref, lse_ref, dsum_ref, *rest):
        c_ins, (dqt_ref, dk_ref, dv_ref), c_outs, sems = rest[:n_c], rest[n_c:n_c + 3], rest[n_c + 3:2 * n_c + 3], rest[2 * n_c + 3:]
        c_begin(c_ins, c_outs, sems)
        first = pl.program_id(1) == 0
        k_j, kt_j, v_j = k_ref[0], kt_ref[0, 0], v_ref[0]

        def step(i, carry):
            dk, dv = carry
            rows = pl.ds(pl.multiple_of(i * cq, cq), cq)
            q_i, do_i = q_ref[0, rows, :], do_ref[rows, :]
            s = lax.dot_general(k_j, q_i, _NT, preferred_element_type=F32)
            p = jnp.exp2(s - lse_ref[0, i])
            dp = lax.dot_general(v_j, do_i, _NT, preferred_element_type=F32)
            ds = (p * (dp - dsum_ref[0, i]) * LN_2).astype(BF16)
            dv = dv + jnp.dot(p.astype(BF16), do_i, preferred_element_type=F32)
            dk = dk + jnp.dot(ds, q_i, preferred_element_type=F32)
            _acc(dqt_ref.at[0, i], jnp.dot(kt_j, ds, preferred_element_type=F32), first)
            return dk, dv

        dk, dv = lax.fori_loop(0, n_q, step, (jnp.zeros((tk, QK), F32), jnp.zeros((tk, HEAD), F32)))
        dk_ref[0] = dk
        dv_ref[0] = dv
        c_end(c_ins, c_outs, sems)

    stat = pl.BlockSpec((1, n_q, 1, cq), lambda h, j: (h, 0, 0, 0))
    outs = pl.pallas_call(
        body, name="attention_backward",
        out_shape=(jax.ShapeDtypeStruct((n_heads, n_q, QK, cq), F32), jax.ShapeDtypeStruct((n_heads, s_len, QK), F32),
                   jax.ShapeDtypeStruct((n_heads, s_len, HEAD), F32), *c_shapes),
        grid=(n_heads, s_len // tk),
        in_specs=[pl.BlockSpec((1, s_len, QK), lambda h, j: (h, 0, 0)), pl.BlockSpec((1, tk, QK), lambda h, j: (h, j, 0)),
                  pl.BlockSpec((1, 1, QK, tk), lambda h, j: (h, j, 0, 0)),
                  pl.BlockSpec((1, tk, HEAD), lambda h, j: (h, j, 0)), pl.BlockSpec((s_len, HEAD), lambda h, j: (0, h)),
                  stat, stat] + c_in_specs,
        out_specs=(pl.BlockSpec((1, n_q, QK, cq), lambda h, j: (h, 0, 0, 0)), pl.BlockSpec((1, tk, QK), lambda h, j: (h, j, 0)),
                   pl.BlockSpec((1, tk, HEAD), lambda h, j: (h, j, 0)), *c_out_specs),
        scratch_shapes=c_sems,
        compiler_params=_params("arbitrary", "arbitrary") if comm else _params("parallel", "arbitrary"),
    )(q, k, kt, v, do, lse, dsum, *c_ops)
    return outs[0], outs[1], outs[2], outs[3:]


def _exchange(arrs, gather, name):
    n = len(arrs)

    def body(*refs):
        plan = _exchange_plan(refs[:n], refs[n:2 * n], gather, *refs[2 * n:])
        _exchange_start(plan)
        _exchange_wait(plan)

    any_spec = pl.BlockSpec(memory_space=pl.ANY)
    return pl.pallas_call(
        body, name=name,
        out_shape=_exchange_out_shapes(arrs, gather),
        in_specs=[any_spec] * n,
        out_specs=tuple([any_spec] * n),
        scratch_shapes=_exchange_semaphores(n),
        compiler_params=pltpu.CompilerParams(has_side_effects=True),
    )(*arrs)


def _gather_via_sibling(block, name):
    def body(x_ref, out_ref, *sems):
        plan = _sibling_plan([x_ref], [out_ref], *sems)
        _sibling_start(plan)
        _sibling_forward(plan)
        _sibling_finish(plan)

    any_spec = pl.BlockSpec(memory_space=pl.ANY)
    return pl.pallas_call(
        body, name=name,
        out_shape=jax.ShapeDtypeStruct((N_DEV, *block.shape), block.dtype),
        in_specs=[any_spec], out_specs=any_spec,
        scratch_shapes=_exchange_semaphores(1),
        compiler_params=pltpu.CompilerParams(has_side_effects=True),
    )(block)


def _sibling_plan(ins, outs, send_sems, recv_sems, local_sems):
    x, y, c = lax.axis_index("x"), lax.axis_index("y"), lax.axis_index("c")
    me, sibling = (x, y, c), (x, y, 1 - c)
    chips = [(1 - x, y), (x, 1 - y), (1 - x, 1 - y)]
    local, first, ici_arrivals, passes, sibling_arrivals = [], [], [], [], []
    for a, (x_ref, out_ref) in enumerate(zip(ins, outs)):
        def slot(px, py, pc, out_ref=out_ref):
            return out_ref.at[4 * px + 2 * py + pc]

        def copy(k, block_of, to, src=None, a=a, slot=slot):
            idx = a * (N_DEV - 1) + k
            return pltpu.make_async_remote_copy(
                src_ref=slot(*block_of) if src is None else src, dst_ref=slot(*block_of), send_sem=send_sems.at[idx],
                recv_sem=recv_sems.at[idx], device_id=to, device_id_type=pl.DeviceIdType.MESH)

        local.append(pltpu.make_async_copy(x_ref, slot(*me), local_sems.at[a]))
        first += [copy(0, me, sibling, src=x_ref)] + [copy(1 + j, me, (*chip, c), src=x_ref) for j, chip in enumerate(chips)]
        ici_arrivals += [copy(1 + j, (*chip, c), me) for j, chip in enumerate(chips)]
        passes += [copy(4 + j, (*chip, c), sibling) for j, chip in enumerate(chips)]
        sibling_arrivals += [copy(0, sibling, me)] + [copy(4 + j, (*chip, 1 - c), me) for j, chip in enumerate(chips)]
    return local, first, ici_arrivals, passes, sibling_arrivals


def _sibling_start(plan):
    local, first = plan[0], plan[1]
    for cp in local + first:
        cp.start()


def _sibling_forward(plan):
    for arrival, onward in zip(plan[2], plan[3]):
        arrival.wait_recv()
        onward.start()


def _sibling_finish(plan):
    local, first, _, passes, sibling_arrivals = plan
    for arrival in sibling_arrivals:
        arrival.wait_recv()
    for cp in first + passes:
        cp.wait_send()
    for cp in local:
        cp.wait()


def _exchange_out_shapes(arrs, gather):
    return tuple(jax.ShapeDtypeStruct((N_DEV, *(a.shape if g else a.shape[1:])), a.dtype) for a, g in zip(arrs, gather))


def _exchange_semaphores(n):
    n_remote = n * (N_DEV - 1)
    return [pltpu.SemaphoreType.DMA((n_remote,)), pltpu.SemaphoreType.DMA((n_remote,)), pltpu.SemaphoreType.DMA((n,))]


def _exchange_plan(ins, outs, gather, send_sems, recv_sems, local_sems):
    n = len(ins)
    x, y, c = lax.axis_index("x"), lax.axis_index("y"), lax.axis_index("c")
    me = 4 * x + 2 * y + c

    def block_for(a, dev):
        return ins[a] if gather[a] else ins[a].at[dev]

    local = [pltpu.make_async_copy(block_for(a, me), outs[a].at[me], local_sems.at[a]) for a in range(n)]
    remote = []
    for k in range(1, N_DEV):
        px = 1 - x if k & 4 else x
        py = 1 - y if k & 2 else y
        pc = 1 - c if k & 1 else c
        peer = 4 * px + 2 * py + pc
        for a in range(n):
            idx = a * (N_DEV - 1) + k - 1
            send = pltpu.make_async_remote_copy(
                src_ref=block_for(a, peer), dst_ref=outs[a].at[me], send_sem=send_sems.at[idx], recv_sem=recv_sems.at[idx],
                device_id=(px, py, pc), device_id_type=pl.DeviceIdType.MESH)
            arrive = pltpu.make_async_remote_copy(
                src_ref=block_for(a, peer), dst_ref=outs[a].at[peer], send_sem=send_sems.at[idx], recv_sem=recv_sems.at[idx],
                device_id=(px, py, pc), device_id_type=pl.DeviceIdType.MESH)
            remote.append((send, arrive))
    return local, remote


def _exchange_start(plan):
    local, remote = plan
    for cp in local:
        cp.start()
    for send, _ in remote:
        send.start()


def _exchange_wait(plan):
    local, remote = plan
    for send, arrive in remote:
        send.wait_send()
        arrive.wait_recv()
    for cp in local:
        cp.wait()


def _riding_exchange(comm, n_grid):
    if comm is None:
        return [], [], (), (), [], lambda *_: None, lambda *_: None
    arrs, gather, *via_sibling = comm
    via_sibling = bool(via_sibling and via_sibling[0])
    n = len(arrs)
    any_spec = pl.BlockSpec(memory_space=pl.ANY)

    def at_step(step_of_first_axis):
        rest = [pl.program_id(d) == (0 if step_of_first_axis != "last" else pl.num_programs(d) - 1) for d in range(1, n_grid)]
        lead = {"first": 0, "middle": pl.num_programs(0) // 2, "last": pl.num_programs(0) - 1}[step_of_first_axis]
        return functools.reduce(jnp.logical_and, [pl.program_id(0) == lead] + rest)

    def begin(ins, outs, sems):
        @pl.when(at_step("first"))
        def _():
            if via_sibling:
                _sibling_start(_sibling_plan(ins, outs, *sems))
            else:
                _exchange_start(_exchange_plan(ins, outs, gather, *sems))

        if via_sibling:
            @pl.when(at_step("middle"))
            def _():
                _sibling_forward(_sibling_plan(ins, outs, *sems))

    def end(ins, outs, sems):
        @pl.when(at_step("last"))
        def _():
            if via_sibling:
                _sibling_finish(_sibling_plan(ins, outs, *sems))
            else:
                _exchange_wait(_exchange_plan(ins, outs, gather, *sems))

    return (list(arrs), [any_spec] * n, _exchange_out_shapes(arrs, gather), tuple([any_spec] * n), _exchange_semaphores(n),
            begin, end)


ADAM_TILE_ELEMS = 256 * 1024


def _sum_adam(parts, w, m, v, layer, prev, name):
    n_parts, r, c = parts.shape
    tm, tc = r, c
    if r % 16 == 0:
        while tm * c > ADAM_TILE_ELEMS and tm % 16 == 0:
            tm //= 2
    else:
        while r * tc > ADAM_TILE_ELEMS and tc % 256 == 0:
            tc //= 2

    def body(p_ref, w_ref, m_ref, v_ref, *rest):
        g_ref, d_ref, nm_ref, nv_ref = rest[-4:]
        g = p_ref[0].astype(F32)
        for s in range(1, n_parts):
            g = g + p_ref[s].astype(F32)
        m_new = ADAM_B1 * m_ref[...] + (1.0 - ADAM_B1) * g
        v_new = ADAM_B2 * v_ref[...] + (1.0 - ADAM_B2) * (g * g)
        m_hat = m_new / (1.0 - ADAM_B1 ** ADAM_STEP)
        v_hat = v_new / (1.0 - ADAM_B2 ** ADAM_STEP)
        g_ref[...] = g
        d_ref[...] = -ADAM_LR * (m_hat / (jnp.sqrt(v_hat) + ADAM_EPS) + ADAM_WD * w_ref[...])
        nm_ref[...] = m_new
        nv_ref[...] = v_new

    slab = pl.BlockSpec((None, tm, tc), lambda i, j: (layer, i, j))
    n_prev = 0 if prev is None else 4
    return pl.pallas_call(
        body, name=name,
        out_shape=(jax.ShapeDtypeStruct(w.shape, F32),) * 4,
        grid=(r // tm, c // tc),
        in_specs=[pl.BlockSpec((n_parts, tm, tc), lambda i, j: (0, i, j)), slab, slab, slab]
                 + [pl.BlockSpec(memory_space=pl.ANY)] * n_prev,
        out_specs=(slab, slab, slab, slab),
        input_output_aliases={4 + j: j for j in range(n_prev)},
        compiler_params=_params("parallel", "parallel"),
    )(parts, w, m, v, *(prev or ()))


def _permute_in(w):
    k = w.shape[0]
    q = w[:, 3584:5120].reshape(k, C_HEADS, QK)
    return jnp.concatenate(
        [w[:, 1536:3584], w[:, 5696:6720], w[:, 0:1536], q[:, :, :HEAD].reshape(k, C_HEADS * HEAD),
         q[:, :, HEAD:].reshape(k, C_HEADS * ROPE), w[:, 5120:5632], w[:, 5632:5696],
         jnp.zeros((k, PROJ_W - IN_WIDTH), w.dtype)], axis=1)


def _unpermute_in(g):
    k = g.shape[0]
    qn = g[:, O_M + M_QN:O_M + M_QR].reshape(k, C_HEADS, HEAD)
    qr = g[:, O_M + M_QR:O_M + M_CKV].reshape(k, C_HEADS, ROPE)
    q = jnp.concatenate([qn, qr], axis=-1).reshape(k, C_HEADS * QK)
    return jnp.concatenate(
        [g[:, O_A:O_A + W_A], g[:, O_B:O_B + W_B], q, g[:, O_M + M_CKV:O_M + M_KR],
         g[:, O_M + M_KR:O_M + M_KR + ROPE], g[:, O_CZ:O_CZ + W_CZ]], axis=1)


SMALL = ("attn_norm", "sgu_norm", "w_spatial", "b_spatial", "conv_b", "kv_norm", "q_nope_norm", "q_rope_norm",
         "k_nope_norm", "k_rope_norm", "out_norm", "ple_norm")
PACK_ROWS = 256


def _pack(tensors):
    flat = jnp.concatenate([t.reshape(-1) for t in tensors])
    rows = -(-flat.shape[0] // (128 * PACK_ROWS)) * PACK_ROWS
    return jnp.pad(flat, (0, rows * 128 - flat.shape[0])).reshape(rows, 128)


def _unpack(packed, like):
    flat = packed.reshape(-1)
    out, pos = [], 0
    for t in like:
        out.append(flat[pos:pos + t.size].reshape(t.shape))
        pos += t.size
    return out


def _tile(s_len, want):
    return min(want, s_len)


ATT_FWD_QUERIES = 512
ATT_FWD_KEYS = 8192
ATT_BWD_KEYS = 1024
ATT_BWD_QUERIES = 4096


def _layer_forward(h, p_l, cosf, sins, w, sm, comm, comm_rest, target=None):
    s_len = h.shape[0]
    tm = _tile(s_len, 512)
    proj, hn, rest = _norm_matmul(h, sm["attn_norm"], w["w_in"], _tile(s_len, 1024), 768, comm_rest)
    if comm_rest is not None:
        w = {**w, **_assemble_rest(rest)}
    ga, gb, gc = sm["out_norm"][:, 0:512], sm["out_norm"][:, 512:1024], sm["out_norm"][:, 1024:2048]
    y = _sgu_forward(proj, sm["sgu_norm"], sm["w_spatial"], sm["b_spatial"], ga, _tile(s_len, 256))
    y = _conv_forward(proj, w["conv_w"], sm["conv_b"], gb, y, _tile(s_len, 256))
    q, k, v, kt, vt = _mla_forward(proj, cosf, sins, w["w_ukv"], sm["kv_norm"], sm["q_nope_norm"], sm["q_rope_norm"],
                                   sm["k_nope_norm"], sm["k_rope_norm"], tm, _tile(s_len, ATT_BWD_KEYS),
                                   _tile(s_len, ATT_FWD_KEYS))
    o, lse, arrived = _attention_forward(q, k, vt, _tile(s_len, ATT_FWD_QUERIES), _tile(s_len, ATT_BWD_QUERIES), comm)
    y = _cgate_forward(o, proj, gc, y, _tile(s_len, 256))
    h1 = _out_matmul(h, y, w["w_out"], tm, D_MODEL)
    h2, n1, gate, pp, *loss = _ple_forward(h1, sm["ple_norm"], p_l, w["w_ple_gate"], w["w_ple_proj"], tm, D_MODEL, target)
    saved = dict(h=h, hn=hn, proj=proj, y=y, q=q, k=k, v=v, kt=kt, o=o, lse=lse, h1=h1, n1=n1, gate=gate, pp=pp)
    return h2, saved, w, arrived, loss[0] if loss else None


def _layer_backward(dh2, p_l, cosf, sins, w, sm, sv, comm, scatter_own):
    s_len = dh2.shape[0]
    tm = _tile(s_len, 512)
    tr = _tile(s_len, 256)
    big, small = {}, {}
    dh1, small["ple_norm"], dgp, dpp = _ple_backward(dh2, sv["gate"], sv["pp"], w["w_ple_gate"], sv["h1"], sm["ple_norm"], tm)
    big["w_ple_proj"], _ = _matmul_tn(p_l, dpp, _tile(s_len, 2048), PLE_DIM, 1024, "grad_w_ple_proj")
    big["w_ple_gate"], _ = _matmul_tn(sv["n1"], dgp, _tile(s_len, 2048), 1024, 1024, "grad_w_ple_gate")
    dy, _ = _matmul_nt(dh1, w["w_out"], tm, D_MODEL, "grad_branches")
    big["w_out"], _ = _matmul_tn(sv["y"], dh1, _tile(s_len, 2048), 1024, 1024, "grad_w_out")
    ga, gb, gc = sm["out_norm"][:, 0:512], sm["out_norm"][:, 512:1024], sm["out_norm"][:, 1024:2048]
    dproj, dcw, small["conv_b"], dgb = _conv_backward(sv["proj"], w["conv_w"], sm["conv_b"], gb, dy, tr)
    big["conv_w"] = dcw
    dproj, do, dsum, dgc = _cgate_backward(sv["o"], sv["proj"], gc, dy, dproj, tm, _tile(s_len, ATT_BWD_QUERIES))
    dqt, dk, dv, arrived = _attention_backward(sv["q"], sv["k"], sv["kt"], sv["v"], do, sv["lse"], dsum, comm)
    (dproj, big["w_ukv"], small["kv_norm"], small["q_nope_norm"], small["q_rope_norm"], small["k_nope_norm"],
     small["k_rope_norm"]) = _mla_backward(sv["proj"], cosf, sins, w["w_ukv"], sm["kv_norm"], sm["q_nope_norm"],
                                            sm["q_rope_norm"], sm["k_nope_norm"], sm["k_rope_norm"], dqt, dk, dv, dproj, tr)
    dproj, small["sgu_norm"], small["w_spatial"], small["b_spatial"], dga = _sgu_backward(
        sv["proj"], sm["sgu_norm"], sm["w_spatial"], sm["b_spatial"], ga, dy, dproj, tm)
    small["out_norm"] = jnp.concatenate([dga, dgb, dgc], axis=1)
    parts_rest = _parts_rest(big)
    g_in, arrived_rest = _matmul_tn(sv["hn"], dproj, _tile(s_len, 2048), 512, 2304, "grad_w_in",
                                    (parts_rest, [False] * len(parts_rest)) if scatter_own else None)
    parts = [_part_w_in(g_in)] + parts_rest
    dhn, arrived_in = _matmul_nt(dproj, w["w_in"], _tile(s_len, 1024), 256, "grad_attn_norm_in",
                                 (parts[:1], [False]) if scatter_own else None)
    dh, small["attn_norm"] = _rms_backward(sv["h"], sm["attn_norm"], dhn, dh1, tr, "attn_norm_backward")
    return dh, parts, big["conv_w"], small, arrived, (*arrived_in, *arrived_rest) if scatter_own else None


def _layer_small(params, layer):
    return dict(
        attn_norm=params["attn_norm"][layer][None, :], sgu_norm=params["sgu_norm"][layer],
        w_spatial=params["w_spatial"][layer], b_spatial=params["b_spatial"][layer][:, :, None],
        conv_b=params["conv_b"][layer][None, :], kv_norm=params["kv_norm"][layer][None, :],
        q_nope_norm=params["q_nope_norm"][layer][None, :], q_rope_norm=params["q_rope_norm"][layer][None, :],
        k_nope_norm=params["k_nope_norm"][layer][None, :], k_rope_norm=params["k_rope_norm"][layer][None, :],
        out_norm=params["out_norm"][layer][None, :], ple_norm=params["ple_norm"][layer][None, :])


BIG = ("w_in", "w_ukv", "w_out", "w_ple_gate", "w_ple_proj")


def _assemble_w_in(g_in):
    return _permute_in(g_in.transpose(1, 0, 2).reshape(g_in.shape[1], IN_WIDTH))


def _assemble_rest(gathered):
    g_ukv, g_out, g_gate, g_proj = gathered
    w_ukv = g_ukv.reshape(N_DEV, KV_RANK, 2, HEAD).transpose(1, 2, 0, 3).reshape(KV_RANK, 2 * C_HEADS * HEAD)
    return dict(w_ukv=w_ukv, w_out=g_out.reshape(D_MODEL, D_MODEL), w_ple_gate=g_gate.reshape(D_MODEL, D_MODEL),
                w_ple_proj=g_proj.transpose(1, 0, 2).reshape(PLE_DIM, D_MODEL))


def _part_w_in(g_in):
    return _unpermute_in(g_in).reshape(g_in.shape[0], N_DEV, -1).transpose(1, 2, 0).astype(BF16)


def _parts_rest(big):
    return [
        big["w_ukv"].reshape(KV_RANK, 2, N_DEV, HEAD).transpose(2, 0, 1, 3).reshape(N_DEV, KV_RANK, 2 * HEAD).astype(BF16),
        big["w_out"].reshape(N_DEV, -1, D_MODEL).astype(BF16),
        big["w_ple_gate"].reshape(N_DEV, -1, D_MODEL).astype(BF16),
        big["w_ple_proj"].reshape(PLE_DIM, N_DEV, -1).transpose(1, 0, 2).astype(BF16)]


def _step_local(xs, ps, pos, target, shards, conv_w, params):
    inv = 1.0 / (ROPE_BASE ** (jnp.arange(0, ROPE, 2, dtype=F32) / ROPE))
    ang = pos.astype(F32)[:, None] * inv
    cos, sin = jnp.cos(ang), jnp.sin(ang)
    cosf = jnp.concatenate([cos, cos], axis=-1)
    sins = jnp.concatenate([-sin, sin], axis=-1)

    def gather_of(names, layer):
        return [shards[n][layer] for n in names], [True] * len(names)

    h = xs
    saved, weights = [], []
    smalls = [_layer_small(params, layer) for layer in range(DEPTH)]
    first_w_in = _gather_via_sibling(shards["w_in"][0], "gather_first_w_in")
    w = dict(w_in=_assemble_w_in(first_w_in), conv_w=conv_w[0])
    for layer in range(DEPTH):
        comm = (*gather_of(BIG, layer + 1), True) if layer + 1 < DEPTH else None
        comm_rest = gather_of(BIG[1:], 0) if layer == 0 else None
        h, sv, w, arrived, loss = _layer_forward(h, ps[layer], cosf, sins, w, smalls[layer], comm, comm_rest,
                                                 target if layer == DEPTH - 1 else None)
        saved.append(sv)
        weights.append(w)
        if comm is not None:
            w = dict(w_in=_assemble_w_in(arrived[0]), conv_w=conv_w[layer + 1], **_assemble_rest(arrived[1:]))
    dh = h
    received, conv_grads, small_grads = [None] * DEPTH, [None] * DEPTH, [None] * DEPTH
    comm = None
    for layer in reversed(range(DEPTH)):
        dh, parts, conv_grads[layer], small_grads[layer], arrived, arrived_own = _layer_backward(
            dh, ps[layer], cosf, sins, weights[layer], smalls[layer], saved[layer], comm, layer == 0)
        if comm is not None:
            received[layer + 1] = arrived
        comm = (parts, [False] * len(parts))
    received[0] = arrived_own
    return loss, dh, received, conv_grads, small_grads


def kernel(x, p, positions, attn_norm, w_in, sgu_norm, w_spatial, b_spatial, conv_w, conv_b, kv_norm, w_ukv, q_nope_norm, q_rope_norm, k_nope_norm, k_rope_norm, out_norm, w_out, ple_norm, w_ple_gate, w_ple_proj, loss_target, m_attn_norm, m_w_in, m_sgu_norm, m_w_spatial, m_b_spatial, m_conv_w, m_conv_b, m_kv_norm, m_w_ukv, m_q_nope_norm, m_q_rope_norm, m_k_nope_norm, m_k_rope_norm, m_out_norm, m_w_out, m_ple_norm, m_w_ple_gate, m_w_ple_proj, v_attn_norm, v_w_in, v_sgu_norm, v_w_spatial, v_b_spatial, v_conv_w, v_conv_b, v_kv_norm, v_w_ukv, v_q_nope_norm, v_q_rope_norm, v_k_nope_norm, v_k_rope_norm, v_out_norm, v_w_out, v_ple_norm, v_w_ple_gate, v_w_ple_proj):
    order = ("attn_norm", "w_in", "sgu_norm", "w_spatial", "b_spatial", "conv_w", "conv_b", "kv_norm", "w_ukv",
             "q_nope_norm", "q_rope_norm", "k_nope_norm", "k_rope_norm", "out_norm", "w_out", "ple_norm", "w_ple_gate",
             "w_ple_proj")
    wts = dict(zip(order, (attn_norm, w_in, sgu_norm, w_spatial, b_spatial, conv_w, conv_b, kv_norm, w_ukv, q_nope_norm,
                           q_rope_norm, k_nope_norm, k_rope_norm, out_norm, w_out, ple_norm, w_ple_gate, w_ple_proj)))
    mom = dict(zip(order, (m_attn_norm, m_w_in, m_sgu_norm, m_w_spatial, m_b_spatial, m_conv_w, m_conv_b, m_kv_norm, m_w_ukv,
                           m_q_nope_norm, m_q_rope_norm, m_k_nope_norm, m_k_rope_norm, m_out_norm, m_w_out, m_ple_norm,
                           m_w_ple_gate, m_w_ple_proj)))
    var = dict(zip(order, (v_attn_norm, v_w_in, v_sgu_norm, v_w_spatial, v_b_spatial, v_conv_w, v_conv_b, v_kv_norm, v_w_ukv,
                           v_q_nope_norm, v_q_rope_norm, v_k_nope_norm, v_k_rope_norm, v_out_norm, v_w_out, v_ple_norm,
                           v_w_ple_gate, v_w_ple_proj)))

    conv_shard = wts["conv_w"]
    (conv_all,) = _exchange([conv_shard.reshape(-1, 128)], [True], "gather_conv_w")
    conv_full = conv_all.reshape(N_DEV, DEPTH, 3, -1).transpose(1, 2, 0, 3).reshape(DEPTH, 3, -1)
    shards = {n: wts[n].astype(BF16) for n in BIG}
    loss_part, grad_x, received, conv_grads, small_grads = _step_local(
        x[0], p[:, 0], positions[0], loss_target[0], shards, conv_full, wts)
    loss = lax.psum(loss_part[0, 0], ("x", "y", "c"))

    def small_grad(name):
        g = jnp.stack([sg[name] for sg in small_grads])
        return g.reshape(wts[name].shape)

    conv_grad = jnp.stack(conv_grads)
    like = [wts[n] for n in SMALL] + [conv_grad]
    packed = _pack([small_grad(n) for n in SMALL] + [conv_grad])
    small_parts = _gather_via_sibling(packed, "gather_small_grads")
    filler = [jnp.zeros_like(conv_grad), jnp.zeros_like(conv_grad), jnp.ones_like(conv_grad)]
    small_out = _sum_adam(small_parts, *(_pack([src[n] for n in SMALL] + [fill])[None] for src, fill in zip((wts, mom, var), filler)),
                          0, None, "adam_small")
    unpacked = [_unpack(o[0], like) for o in small_out]
    results = {n: vals for n, vals in zip(SMALL, zip(*[u[:-1] for u in unpacked]))}
    me = 4 * lax.axis_index("x") + 2 * lax.axis_index("y") + lax.axis_index("c")
    width = conv_shard.shape[2]
    conv_local = lax.dynamic_slice_in_dim(unpacked[0][-1], me * width, width, axis=2)
    as_slab = (lambda t: t.reshape(1, -1, width))
    conv_out = _sum_adam(as_slab(conv_local), as_slab(conv_shard), as_slab(mom["conv_w"]), as_slab(var["conv_w"]), 0, None,
                         "adam_conv_w")
    results["conv_w"] = tuple(o.reshape(conv_shard.shape) for o in conv_out)

    for j, name in enumerate(BIG):
        view = (lambda t: jnp.swapaxes(t, 1, 2)) if name == "w_in" else (lambda t: t)
        outs = None
        for layer in range(DEPTH):
            outs = _sum_adam(received[layer][j], view(wts[name]), view(mom[name]), view(var[name]), layer, outs, "adam_" + name)
        results[name] = tuple(view(o) for o in outs)

    grads, deltas, new_m, new_v = ([results[n][j] for n in order] for j in range(4))
    return (loss, grad_x[None], *grads, *deltas, *new_m, *new_v)
```
